```python
import math
import jax, jax.numpy as jnp
from jax import lax
import numpy as np

D_MODEL = 1024
BATCH = 32
SEQ = 2048
DEPTH = 1

HEAD_DIM = 64
N_HEADS = D_MODEL // HEAD_DIM
N_HEADS_FOX = N_HEADS // 2
N_HEADS_DIL = N_HEADS - N_HEADS_FOX
W_FOX = N_HEADS_FOX * HEAD_DIM
W_DIL = N_HEADS_DIL * HEAD_DIM
DILATION_PAIRS = ((128, 1), (512, 4), (2048, 16))
ROPE_THETA = 500000.0
ROPE_DIM = HEAD_DIM // 4
Q_BLOCK = 128
D_FF = -(-8 * D_MODEL // (3 * 256)) * 256
EPS = 1e-6
NEG = -1e30
IN_SPLITS = (W_FOX, 2 * W_FOX, 3 * W_FOX, 3 * W_FOX + N_HEADS_FOX,
             3 * W_FOX + N_HEADS_FOX + W_DIL, 3 * W_FOX + N_HEADS_FOX + 2 * W_DIL)
IN_COLS = 3 * W_FOX + N_HEADS_FOX + 3 * W_DIL

kernel_name = "hymba_fox_dilated_hybrid"


def rms_norm(x, g):
    xf = x.astype(jnp.float32)
    y = xf * lax.rsqrt(jnp.mean(xf * xf, axis=-1, keepdims=True) + EPS)
    return (y * g.astype(jnp.float32)).astype(x.dtype)


def partial_rope(x, pos):
    half = ROPE_DIM // 2
    inv_freq = jnp.power(jnp.float32(ROPE_THETA),
                         -jnp.arange(half, dtype=jnp.float32) * 2.0 / ROPE_DIM)
    ang = pos.astype(jnp.float32)[:, None] * inv_freq[None, :]
    cos = jnp.cos(ang)[None, :, None, :]
    sin = jnp.sin(ang)[None, :, None, :]
    x1 = x[..., :half]
    x2 = x[..., half:ROPE_DIM]
    return jnp.concatenate([x1 * cos - x2 * sin, x2 * cos + x1 * sin, x[..., ROPE_DIM:]], axis=-1)


def fox_attention(q, k, v, log_f):
    S = q.shape[1]
    c = jnp.transpose(jnp.cumsum(log_f, axis=1), (0, 2, 1))
    scale = HEAD_DIM ** -0.5
    outs = []
    for i in range(S // Q_BLOCK):
        q0, q1 = i * Q_BLOCK, (i + 1) * Q_BLOCK
        s = jnp.einsum('bqhe,bkhe->bhqk', q[:, q0:q1], k[:, :q1]) * scale
        s = s + (c[:, :, q0:q1, None] - c[:, :, None, :q1])
        mask = np.arange(q0, q1)[:, None] >= np.arange(q1)[None, :]
        s = jnp.where(mask[None, None], s, NEG)
        p = jax.nn.softmax(s, axis=-1)
        outs.append(jnp.einsum('bhqk,bkhe->bqhe', p, v[:, :q1]))
    return jnp.concatenate(outs, axis=1)


def dilated_branch(q, k, v, window, dilation):
    B, S, H, D = q.shape
    n = S // dilation
    wk = window // dilation
    pad = min(wk, n)
    bq = math.gcd(Q_BLOCK, n)
    nb = n // bq
    L = bq + pad
    scale = HEAD_DIM ** -0.5
    qs = q.reshape(B, nb, bq, dilation, H, D)
    kp = jnp.pad(k.reshape(B, n, dilation, H, D), ((0, 0), (pad, 0), (0, 0), (0, 0), (0, 0)))
    vp = jnp.pad(v.reshape(B, n, dilation, H, D), ((0, 0), (pad, 0), (0, 0), (0, 0), (0, 0)))
    starts = np.arange(nb) * bq
    idx = starts[:, None] + np.arange(L)[None, :]
    kb = kp[:, idx]
    vb = vp[:, idx]
    s = jnp.einsum('bnirhe,bnjrhe->bnrhij', qs, kb) * scale
    ii = np.arange(bq)[:, None]
    jj = np.arange(L)[None, :]
    dist = ii + pad - jj
    key_pos = starts[:, None, None] + jj[None] - pad
    mask = (dist >= 0)[None] & (dist <= wk)[None] & (key_pos >= 0)
    s = jnp.where(mask[None, :, None, None], s, NEG)
    lse = jax.nn.logsumexp(s, axis=-1)
    p = jnp.exp(s - lse[..., None])
    o = jnp.einsum('bnrhij,bnjrhe->bnirhe', p, vb).reshape(B, S, H, D)
    lse = jnp.transpose(lse, (0, 1, 4, 2, 3)).reshape(B, S, H)
    return o, lse


def dilated_attention(q, k, v):
    outs, lses = [], []
    for window, dilation in DILATION_PAIRS:
        o, l = dilated_branch(q, k, v, window, dilation)
        outs.append(o)
        lses.append(l)
    w = jax.nn.softmax(jnp.stack(lses, axis=0), axis=0)
    return jnp.sum(w[..., None] * jnp.stack(outs, axis=0), axis=0)


def _fwd_setup_inputs(seed: int = 0) -> dict:
    key = jax.random.key(seed)
    ks = jax.random.split(key, 16)
    f32 = jnp.float32

    def gain(k, shape):
        return jnp.ones(shape, f32) + 0.02 * jax.random.normal(k, shape, f32)

    return {
        "x": jax.random.normal(ks[0], (BATCH, SEQ, D_MODEL), f32),
        "g_mix": gain(ks[1], (DEPTH, D_MODEL)),
        "w_in": jax.random.normal(ks[2], (DEPTH, D_MODEL, IN_COLS), f32) * D_MODEL ** -0.5,
        "b_forget": jax.random.uniform(ks[3], (DEPTH, N_HEADS_FOX), f32, minval=1.0, maxval=4.0),
        "g_q_fox": gain(ks[4], (DEPTH, HEAD_DIM)),
        "g_k_fox": gain(ks[5], (DEPTH, HEAD_DIM)),
        "g_q_dil": gain(ks[6], (DEPTH, HEAD_DIM)),
        "g_k_dil": gain(ks[7], (DEPTH, HEAD_DIM)),
        "g_out_fox": gain(ks[8], (DEPTH, W_FOX)),
        "g_out_dil": gain(ks[9], (DEPTH, W_DIL)),
        "w_out": jax.random.normal(ks[10], (DEPTH, D_MODEL, D_MODEL), f32) * D_MODEL ** -0.5,
        "g_ffn": gain(ks[11], (DEPTH, D_MODEL)),
        "w_gate": jax.random.normal(ks[12], (DEPTH, D_MODEL, D_FF), f32) * D_MODEL ** -0.5,
        "w_up": jax.random.normal(ks[13], (DEPTH, D_MODEL, D_FF), f32) * D_MODEL ** -0.5,
        "w_down": jax.random.normal(ks[14], (DEPTH, D_FF, D_MODEL), f32) * D_FF ** -0.5,
    }


def _fwd_reference(x, g_mix, w_in, b_forget, g_q_fox, g_k_fox, g_q_dil, g_k_dil,
              g_out_fox, g_out_dil, w_out, g_ffn, w_gate, w_up, w_down):
    B, S, _ = x.shape
    f32 = jnp.float32
    pos = jnp.arange(S)

    def heads(t, n_h):
        return t.reshape(B, S, n_h, HEAD_DIM).astype(f32)

    for l in range(DEPTH):
        h = rms_norm(x, g_mix[l])
        proj = jnp.einsum('bsd,dc->bsc', h, w_in[l])
        qa, ka, va, fa, qd, kd, vd = jnp.split(proj, IN_SPLITS, axis=-1)

        qa = rms_norm(heads(qa, N_HEADS_FOX), g_q_fox[l])
        ka = rms_norm(heads(ka, N_HEADS_FOX), g_k_fox[l])
        va = heads(va, N_HEADS_FOX)
        log_f = jax.nn.log_sigmoid(fa.astype(f32) + b_forget[l].astype(f32))
        o_fox = fox_attention(qa, ka, va, log_f).reshape(B, S, W_FOX)

        qd = partial_rope(rms_norm(heads(qd, N_HEADS_DIL), g_q_dil[l]), pos)
        kd = partial_rope(rms_norm(heads(kd, N_HEADS_DIL), g_k_dil[l]), pos)
        vd = heads(vd, N_HEADS_DIL)
        o_dil = dilated_attention(qd, kd, vd).reshape(B, S, W_DIL)

        o = jnp.concatenate([rms_norm(o_fox, g_out_fox[l]), rms_norm(o_dil, g_out_dil[l])], axis=-1)
        x = x + jnp.einsum('bsc,cd->bsd', o.astype(x.dtype), w_out[l])

        h = rms_norm(x, g_ffn[l])
        a = jnp.einsum('bsd,df->bsf', h, w_gate[l])
        u = jnp.einsum('bsd,df->bsf', h, w_up[l])
        x = x + jnp.einsum('bsf,fd->bsd', jax.nn.silu(a) * u, w_down[l])
    return x


import jax as _jax
import jax.numpy as _jnp

TWIN_FORMAT = 'train_step'
FWD_PARAMS = ['x', 'g_mix', 'w_in', 'b_forget', 'g_q_fox', 'g_k_fox', 'g_q_dil', 'g_k_dil', 'g_out_fox', 'g_out_dil', 'w_out', 'g_ffn', 'w_gate', 'w_up', 'w_down']
TWIN_WEIGHTS = ['g_mix', 'w_in', 'b_forget', 'g_q_fox', 'g_k_fox', 'g_q_dil', 'g_k_dil', 'g_out_fox', 'g_out_dil', 'w_out', 'g_ffn', 'w_gate', 'w_up', 'w_down']
TWIN_DIFF_INPUT = 'x'
TWIN_INPUTS = ['x', 'g_mix', 'w_in', 'b_forget', 'g_q_fox', 'g_k_fox', 'g_q_dil', 'g_k_dil', 'g_out_fox', 'g_out_dil', 'w_out', 'g_ffn', 'w_gate', 'w_up', 'w_down', 'loss_target', 'm_g_mix', 'm_w_in', 'm_b_forget', 'm_g_q_fox', 'm_g_k_fox', 'm_g_q_dil', 'm_g_k_dil', 'm_g_out_fox', 'm_g_out_dil', 'm_w_out', 'm_g_ffn', 'm_w_gate', 'm_w_up', 'm_w_down', 'v_g_mix', 'v_w_in', 'v_b_forget', 'v_g_q_fox', 'v_g_k_fox', 'v_g_q_dil', 'v_g_k_dil', 'v_g_out_fox', 'v_g_out_dil', 'v_w_out', 'v_g_ffn', 'v_w_gate', 'v_w_up', 'v_w_down']
TWIN_OUTPUTS = ['loss', 'grad_x', 'grad_g_mix', 'grad_w_in', 'grad_b_forget', 'grad_g_q_fox', 'grad_g_k_fox', 'grad_g_q_dil', 'grad_g_k_dil', 'grad_g_out_fox', 'grad_g_out_dil', 'grad_w_out', 'grad_g_ffn', 'grad_w_gate', 'grad_w_up', 'grad_w_down', 'delta_g_mix', 'delta_w_in', 'delta_b_forget', 'delta_g_q_fox', 'delta_g_k_fox', 'delta_g_q_dil', 'delta_g_k_dil', 'delta_g_out_fox', 'delta_g_out_dil', 'delta_w_out', 'delta_g_ffn', 'delta_w_gate', 'delta_w_up', 'delta_w_down', 'new_m_g_mix', 'new_m_w_in', 'new_m_b_forget', 'new_m_g_q_fox', 'new_m_g_k_fox', 'new_m_g_q_dil', 'new_m_g_k_dil', 'new_m_g_out_fox', 'new_m_g_out_dil', 'new_m_w_out', 'new_m_g_ffn', 'new_m_w_gate', 'new_m_w_up', 'new_m_w_down', 'new_v_g_mix', 'new_v_w_in', 'new_v_b_forget', 'new_v_g_q_fox', 'new_v_g_k_fox', 'new_v_g_q_dil', 'new_v_g_k_dil', 'new_v_g_out_fox', 'new_v_g_out_dil', 'new_v_w_out', 'new_v_g_ffn', 'new_v_w_gate', 'new_v_w_up', 'new_v_w_down']
TWIN_LEAF_KINDS = {'loss': 'loss', 'grad_x': 'grad_x', 'grad_g_mix': 'grad_w', 'grad_w_in': 'grad_w', 'grad_b_forget': 'grad_w', 'grad_g_q_fox': 'grad_w', 'grad_g_k_fox': 'grad_w', 'grad_g_q_dil': 'grad_w', 'grad_g_k_dil': 'grad_w', 'grad_g_out_fox': 'grad_w', 'grad_g_out_dil': 'grad_w', 'grad_w_out': 'grad_w', 'grad_g_ffn': 'grad_w', 'grad_w_gate': 'grad_w', 'grad_w_up': 'grad_w', 'grad_w_down': 'grad_w', 'delta_g_mix': 'delta_w', 'delta_w_in': 'delta_w', 'delta_b_forget': 'delta_w', 'delta_g_q_fox': 'delta_w', 'delta_g_k_fox': 'delta_w', 'delta_g_q_dil': 'delta_w', 'delta_g_k_dil': 'delta_w', 'delta_g_out_fox': 'delta_w', 'delta_g_out_dil': 'delta_w', 'delta_w_out': 'delta_w', 'delta_g_ffn': 'delta_w', 'delta_w_gate': 'delta_w', 'delta_w_up': 'delta_w', 'delta_w_down': 'delta_w', 'new_m_g_mix': 'new_m', 'new_m_w_in': 'new_m', 'new_m_b_forget': 'new_m', 'new_m_g_q_fox': 'new_m', 'new_m_g_k_fox': 'new_m', 'new_m_g_q_dil': 'new_m', 'new_m_g_k_dil': 'new_m', 'new_m_g_out_fox': 'new_m', 'new_m_g_out_dil': 'new_m', 'new_m_w_out': 'new_m', 'new_m_g_ffn': 'new_m', 'new_m_w_gate': 'new_m', 'new_m_w_up': 'new_m', 'new_m_w_down': 'new_m', 'new_v_g_mix': 'new_v', 'new_v_w_in': 'new_v', 'new_v_b_forget': 'new_v', 'new_v_g_q_fox': 'new_v', 'new_v_g_k_fox': 'new_v', 'new_v_g_q_dil': 'new_v', 'new_v_g_k_dil': 'new_v', 'new_v_g_out_fox': 'new_v', 'new_v_g_out_dil': 'new_v', 'new_v_w_out': 'new_v', 'new_v_g_ffn': 'new_v', 'new_v_w_gate': 'new_v', 'new_v_w_up': 'new_v', 'new_v_w_down': 'new_v'}


def _forward(args):
    return _fwd_reference(*[args[k] for k in FWD_PARAMS])


def _output_shape():
    out = _jax.eval_shape(lambda: _forward(_fwd_setup_inputs(0)))
    return out.shape, out.dtype

N_MICROBATCH = 1
ADAM_LR = 0.001
ADAM_B1 = 0.9
ADAM_B2 = 0.999
ADAM_EPS = 1e-08
ADAM_WD = 0.01
ADAM_STEP = 10
PER_EXAMPLE_BATCH_AXIS = {'x': 0, 'loss_target': 0}
SHARED_INPUTS = []
_WEIGHT_DTYPES = {'g_mix': _jnp.float32, 'w_in': _jnp.float32, 'b_forget': _jnp.float32, 'g_q_fox': _jnp.float32, 'g_k_fox': _jnp.float32, 'g_q_dil': _jnp.float32, 'g_k_dil': _jnp.float32, 'g_out_fox': _jnp.float32, 'g_out_dil': _jnp.float32, 'w_out': _jnp.float32, 'g_ffn': _jnp.float32, 'w_gate': _jnp.float32, 'w_up': _jnp.float32, 'w_down': _jnp.float32}
MOMENT_SCALE = {'g_mix': 1.296220e+00, 'w_in': 6.947170e-01, 'b_forget': 8.999906e+00, 'g_q_fox': 1.420372e+00, 'g_k_fox': 1.401268e+00, 'g_q_dil': 2.374556e+00, 'g_k_dil': 2.003291e+00, 'g_out_fox': 7.462372e+01, 'g_out_dil': 6.343442e+01, 'w_out': 1.468447e+00, 'g_ffn': 4.941184e+01, 'w_gate': 2.969632e-01, 'w_up': 3.131894e-01, 'w_down': 4.804493e-01}


def _to_microbatches(a, axis):
    t = _jnp.moveaxis(a, axis, 0)
    t = t.reshape((N_MICROBATCH, t.shape[0] // N_MICROBATCH) + t.shape[1:])
    return _jnp.moveaxis(t, 1, axis + 1)


def setup_inputs(seed: int = 0) -> dict:
    inp = _fwd_setup_inputs(seed)
    key = _jax.random.fold_in(_jax.random.key(seed), 7919)
    shape, _ = _output_shape()
    out = dict(inp)
    out["loss_target"] = _jax.random.normal(_jax.random.fold_in(key, 0), shape, _jnp.float32)
    for i, name in enumerate(TWIN_WEIGHTS):
        w = inp[name].astype(_jnp.float32)
        if MOMENT_SCALE is None:
            s = _jnp.sqrt(_jnp.mean(_jnp.square(w)) + 1e-30)
        else:
            s = MOMENT_SCALE[name]
        km, kv = _jax.random.split(_jax.random.fold_in(key, i + 1))
        out[name] = w
        out["m_" + name] = s * _jax.random.normal(km, w.shape, _jnp.float32)
        out["v_" + name] = (s * s) * _jax.random.uniform(kv, w.shape, _jnp.float32, 0.5, 1.5)
    if N_MICROBATCH > 1:
        for name, axis in PER_EXAMPLE_BATCH_AXIS.items():
            out[name] = _to_microbatches(out[name], axis)
    return {'x': out['x'], 'g_mix': out['g_mix'], 'w_in': out['w_in'], 'b_forget': out['b_forget'], 'g_q_fox': out['g_q_fox'], 'g_k_fox': out['g_k_fox'], 'g_q_dil': out['g_q_dil'], 'g_k_dil': out['g_k_dil'], 'g_out_fox': out['g_out_fox'], 'g_out_dil': out['g_out_dil'], 'w_out': out['w_out'], 'g_ffn': out['g_ffn'], 'w_gate': out['w_gate'], 'w_up': out['w_up'], 'w_down': out['w_down'], 'loss_target': out['loss_target'], 'm_g_mix': out['m_g_mix'], 'm_w_in': out['m_w_in'], 'm_b_forget': out['m_b_forget'], 'm_g_q_fox': out['m_g_q_fox'], 'm_g_k_fox': out['m_g_k_fox'], 'm_g_q_dil': out['m_g_q_dil'], 'm_g_k_dil': out['m_g_k_dil'], 'm_g_out_fox': out['m_g_out_fox'], 'm_g_out_dil': out['m_g_out_dil'], 'm_w_out': out['m_w_out'], 'm_g_ffn': out['m_g_ffn'], 'm_w_gate': out['m_w_gate'], 'm_w_up': out['m_w_up'], 'm_w_down': out['m_w_down'], 'v_g_mix': out['v_g_mix'], 'v_w_in': out['v_w_in'], 'v_b_forget': out['v_b_forget'], 'v_g_q_fox': out['v_g_q_fox'], 'v_g_k_fox': out['v_g_k_fox'], 'v_g_q_dil': out['v_g_q_dil'], 'v_g_k_dil': out['v_g_k_dil'], 'v_g_out_fox': out['v_g_out_fox'], 'v_g_out_dil': out['v_g_out_dil'], 'v_w_out': out['v_w_out'], 'v_g_ffn': out['v_g_ffn'], 'v_w_gate': out['v_w_gate'], 'v_w_up': out['v_w_up'], 'v_w_down': out['v_w_down']}


def _loss(weights, diff, rest, loss_target):
    with _jax.named_scope("forward"):
        args = {**rest, TWIN_DIFF_INPUT: diff, **{k: w.astype(_WEIGHT_DTYPES[k]) for k, w in weights.items()}}
        y = _forward(args)
    with _jax.named_scope("loss_head"):
        err = _jnp.square(y.astype(_jnp.float32) - loss_target)
        return 0.5 * _jnp.sum(_jnp.mean(err, axis=-1)) if err.ndim else 0.5 * err


def _adamw(w, g, m, v):
    m = ADAM_B1 * m + (1.0 - ADAM_B1) * g
    v = ADAM_B2 * v + (1.0 - ADAM_B2) * _jnp.square(g)
    m_hat = m / (1.0 - ADAM_B1 ** ADAM_STEP)
    v_hat = v / (1.0 - ADAM_B2 ** ADAM_STEP)
    delta = -ADAM_LR * (m_hat / (_jnp.sqrt(v_hat) + ADAM_EPS) + ADAM_WD * w)
    return delta, m, v


def reference(x, g_mix, w_in, b_forget, g_q_fox, g_k_fox, g_q_dil, g_k_dil, g_out_fox, g_out_dil, w_out, g_ffn, w_gate, w_up, w_down, loss_target, m_g_mix, m_w_in, m_b_forget, m_g_q_fox, m_g_k_fox, m_g_q_dil, m_g_k_dil, m_g_out_fox, m_g_out_dil, m_w_out, m_g_ffn, m_w_gate, m_w_up, m_w_down, v_g_mix, v_w_in, v_b_forget, v_g_q_fox, v_g_k_fox, v_g_q_dil, v_g_k_dil, v_g_out_fox, v_g_out_dil, v_w_out, v_g_ffn, v_w_gate, v_w_up, v_w_down):
    given = dict(x=x, g_mix=g_mix, w_in=w_in, b_forget=b_forget, g_q_fox=g_q_fox, g_k_fox=g_k_fox, g_q_dil=g_q_dil, g_k_dil=g_k_dil, g_out_fox=g_out_fox, g_out_dil=g_out_dil, w_out=w_out, g_ffn=g_ffn, w_gate=w_gate, w_up=w_up, w_down=w_down, loss_target=loss_target, m_g_mix=m_g_mix, m_w_in=m_w_in, m_b_forget=m_b_forget, m_g_q_fox=m_g_q_fox, m_g_k_fox=m_g_k_fox, m_g_q_dil=m_g_q_dil, m_g_k_dil=m_g_k_dil, m_g_out_fox=m_g_out_fox, m_g_out_dil=m_g_out_dil, m_w_out=m_w_out, m_g_ffn=m_g_ffn, m_w_gate=m_w_gate, m_w_up=m_w_up, m_w_down=m_w_down, v_g_mix=v_g_mix, v_w_in=v_w_in, v_b_forget=v_b_forget, v_g_q_fox=v_g_q_fox, v_g_k_fox=v_g_k_fox, v_g_q_dil=v_g_q_dil, v_g_k_dil=v_g_k_dil, v_g_out_fox=v_g_out_fox, v_g_out_dil=v_g_out_dil, v_w_out=v_w_out, v_g_ffn=v_g_ffn, v_w_gate=v_w_gate, v_w_up=v_w_up, v_w_down=v_w_down)
    weights = {n: given[n] for n in TWIN_WEIGHTS}
    shared = {n: given[n] for n in SHARED_INPUTS}
    per_example = {n: given[n] for n in ['x']}
    grad_fn = _jax.value_and_grad(_loss, argnums=(0, 1))

    def one_microbatch(ex, loss_target):
        ex = dict(ex)
        diff = ex.pop(TWIN_DIFF_INPUT)
        return grad_fn(weights, diff, {**shared, **ex}, loss_target)

    if N_MICROBATCH == 1:
        loss, (grad_w, grad_x) = one_microbatch(per_example, given["loss_target"])
    else:
        def body(carry, xs):
            loss_sum, grad_sum = carry
            l_k, (gw_k, gx_k) = one_microbatch(xs[0], xs[1])
            with _jax.named_scope("update"):
                return (loss_sum + l_k, _jax.tree.map(_jnp.add, grad_sum, gw_k)), gx_k

        init = (_jnp.zeros((), _jnp.float32), _jax.tree.map(_jnp.zeros_like, weights))
        (loss, grad_w), grad_x = _jax.lax.scan(body, init, (per_example, given["loss_target"]))
    with _jax.named_scope("update"):
        delta_w, new_m, new_v = {}, {}, {}
        for n in TWIN_WEIGHTS:
            delta_w[n], new_m[n], new_v[n] = _adamw(weights[n], grad_w[n], given["m_" + n], given["v_" + n])
    return (loss, grad_x, *[grad_w[n] for n in TWIN_WEIGHTS], *[delta_w[n] for n in TWIN_WEIGHTS],
            *[new_m[n] for n in TWIN_WEIGHTS], *[new_v[n] for n in TWIN_WEIGHTS])
```

```python
import functools
import math

import numpy as np
import jax
import jax.numpy as jnp
from jax import lax
from jax.experimental import pallas as pl
from jax.experimental.pallas import tpu as pltpu

F32, BF16 = jnp.float32, jnp.bfloat16
MESH = pl.DeviceIdType.MESH

EPS = 1e-6
NEG = -1e30
HEAD_DIM = 64
SCALE = HEAD_DIM ** -0.5
ROPE_THETA = 500000.0
ROPE_DIM = HEAD_DIM // 4
LANES = 128
W_GROUP = 512
N_FOX_HEADS = 8
VMEM_LIMIT = 56 * 1024 * 1024
DILATIONS = (1, 4, 16)
BAND = 128

ADAM_LR, ADAM_B1, ADAM_B2, ADAM_EPS, ADAM_WD, ADAM_STEP = 0.001, 0.9, 0.999, 1e-08, 0.01, 10

NT = (((1,), (1,)), ((), ()))
TN = (((0,), (0,)), ((), ()))


def _params(sem=None):
    return pltpu.CompilerParams(dimension_semantics=sem, vmem_limit_bytes=VMEM_LIMIT)


def _dot(a, b, dims=None):
    if dims is None:
        return jnp.dot(a, b, preferred_element_type=F32)
    return lax.dot_general(a, b, dims, preferred_element_type=F32)


def _group_ones():
    i = lax.broadcasted_iota(jnp.int32, (LANES, LANES), 0) >> 6
    j = lax.broadcasted_iota(jnp.int32, (LANES, LANES), 1) >> 6
    return (i == j).astype(BF16)


def _split3(x):
    a = x.astype(BF16)
    r = x - a.astype(F32)
    b = r.astype(BF16)
    c = (r - b.astype(F32)).astype(BF16)
    return a, b, c


def _groupsum(x, ones):
    a, b, c = _split3(x)
    return _dot(a, ones) + _dot(b, ones) + _dot(c, ones)


def _head_masks():
    lane = lax.broadcasted_iota(jnp.int32, (1, LANES), 1)
    return [(lane < HEAD_DIM).astype(F32), (lane >= HEAD_DIM).astype(F32)]


def _head_norm(raw, gain, ones):
    r = lax.rsqrt(_groupsum(raw * raw, ones) * (1.0 / HEAD_DIM) + EPS)
    return raw * r, r


def _head_norm_bwd(dy, xhat, r, gain, ones):
    u = dy * gain
    dgain = jnp.sum(dy * xhat, axis=0, keepdims=True)
    draw = r * (u - xhat * (_groupsum(u * xhat, ones) * (1.0 / HEAD_DIM)))
    return draw, dgain


def _rope(x, cos, s_up, s_dn):
    return x * cos + pltpu.roll(x, LANES - 8, 1) * s_up + pltpu.roll(x, 8, 1) * s_dn


def _rope_bwd(dy, cos, s_up, s_dn):
    return dy * cos + pltpu.roll(dy * s_up, 8, 1) + pltpu.roll(dy * s_dn, LANES - 8, 1)


def _rope_tables(seq):
    half = ROPE_DIM // 2
    inv_freq = jnp.power(jnp.float32(ROPE_THETA), -jnp.arange(half, dtype=F32) * 2.0 / ROPE_DIM)
    ang = jnp.arange(seq).astype(F32)[:, None] * inv_freq[None, :]
    cos, sin = jnp.cos(ang), jnp.sin(ang)
    one = jnp.ones((seq, HEAD_DIM - ROPE_DIM), F32)
    zero_h = jnp.zeros((seq, half), F32)
    zero_r = jnp.zeros((seq, HEAD_DIM - ROPE_DIM), F32)
    c = jnp.concatenate([cos, cos, one], axis=1)
    up = jnp.concatenate([-sin, zero_h, zero_r], axis=1)
    dn = jnp.concatenate([zero_h, sin, zero_r], axis=1)
    return jnp.tile(c, (1, 2)), jnp.tile(up, (1, 2)), jnp.tile(dn, (1, 2))


def _row_tile(rows, cap=256):
    best = rows
    for t in range(8, min(rows, cap) + 1, 8):
        if rows % t == 0:
            best = t
    return best


def _in_proj(x, g_mix, w1, wft):
    t, d = x.shape
    n = w1.shape[1]
    tt = 512

    def body(x_ref, g_ref, w_ref, wf_ref, p_ref, fa_ref, h_ref):
        xx = x_ref[...]
        r = lax.rsqrt(jnp.mean(xx * xx, axis=-1, keepdims=True) + EPS)
        h = (xx * r * g_ref[...]).astype(BF16)
        h_ref[...] = h
        for j in range(n // W_GROUP):
            cols = slice(j * W_GROUP, (j + 1) * W_GROUP)
            p_ref[:, cols] = _dot(h, w_ref[:, cols]).astype(BF16)
        fa_ref[...] = _dot(wf_ref[...], h, NT)

    return pl.pallas_call(
        body, name="in_proj", grid=(t // tt,),
        in_specs=[pl.BlockSpec((tt, d), lambda i: (i, 0)), pl.BlockSpec((1, d), lambda i: (0, 0)),
                  pl.BlockSpec(memory_space=pltpu.VMEM), pl.BlockSpec(memory_space=pltpu.VMEM)],
        out_specs=[pl.BlockSpec((tt, n), lambda i: (i, 0)), pl.BlockSpec((8, tt), lambda i: (0, i)),
                   pl.BlockSpec((tt, d), lambda i: (i, 0))],
        out_shape=[jax.ShapeDtypeStruct((t, n), BF16), jax.ShapeDtypeStruct((8, t), F32),
                   jax.ShapeDtypeStruct((t, d), BF16)],
        compiler_params=_params(("arbitrary",)),
    )(x, g_mix, w1, wft)


def _tri(n, upper):
    i = lax.broadcasted_iota(jnp.int32, (n, n), 0)
    j = lax.broadcasted_iota(jnp.int32, (n, n), 1)
    return ((i <= j) if upper else (i >= j)).astype(BF16)


def _gate_fwd(fa_row, b_col, seq):
    t = fa_row.shape[1]
    cb = 256

    def body(fa_ref, b_ref, c_ref):
        tri = _tri(cb, True)
        carry = jnp.zeros((8, 1), F32)
        for k in range(seq // cb):
            z = fa_ref[:, k * cb:(k + 1) * cb] + b_ref[...]
            lf = jnp.minimum(z, 0.0) - jnp.log(1.0 + jnp.exp(-jnp.abs(z)))
            a, b, c = _split3(lf)
            blk = _dot(a, tri) + _dot(b, tri) + _dot(c, tri) + carry
            c_ref[:, k * cb:(k + 1) * cb] = blk
            carry = blk[:, cb - 1:cb]

    return pl.pallas_call(
        body, name="gate_fwd", grid=(t // seq,),
        in_specs=[pl.BlockSpec((8, seq), lambda i: (0, i)), pl.BlockSpec((8, 1), lambda i: (0, 0))],
        out_specs=pl.BlockSpec((8, seq), lambda i: (0, i)),
        out_shape=jax.ShapeDtypeStruct((8, t), F32),
        compiler_params=_params(("arbitrary",)),
    )(fa_row, b_col)


def _gate_bwd(dc_row, fa_row, b_col, seq):
    t = fa_row.shape[1]
    cb = 256

    def body(dc_ref, fa_ref, b_ref, dfa_ref, db_ref):
        @pl.when(pl.program_id(0) == 0)
        def _():
            db_ref[...] = jnp.zeros_like(db_ref)

        tri = _tri(cb, False)
        carry = jnp.zeros((8, 1), F32)
        dbs = jnp.zeros((8, 1), F32)
        for k in reversed(range(seq // cb)):
            a, b, c = _split3(dc_ref[:, k * cb:(k + 1) * cb])
            dlf = _dot(a, tri) + _dot(b, tri) + _dot(c, tri) + carry
            carry = dlf[:, 0:1]
            z = fa_ref[:, k * cb:(k + 1) * cb] + b_ref[...]
            dfa = dlf / (1.0 + jnp.exp(z))
            dfa_ref[:, k * cb:(k + 1) * cb] = dfa
            dbs = dbs + jnp.sum(dfa, axis=1, keepdims=True)
        db_ref[...] += jnp.broadcast_to(dbs, (8, LANES))

    return pl.pallas_call(
        body, name="gate_bwd", grid=(t // seq,),
        in_specs=[pl.BlockSpec((8, seq), lambda i: (0, i)), pl.BlockSpec((8, seq), lambda i: (0, i)),
                  pl.BlockSpec((8, 1), lambda i: (0, 0))],
        out_specs=[pl.BlockSpec((8, seq), lambda i: (0, i)), pl.BlockSpec((8, LANES), lambda i: (0, 0))],
        out_shape=[jax.ShapeDtypeStruct((8, t), F32), jax.ShapeDtypeStruct((8, LANES), F32)],
        compiler_params=_params(("arbitrary",)),
    )(dc_row, fa_row, b_col)


def _attn_out(o_fox, o_dil, x, g_fox, g_dil, w_out):
    t, d = x.shape
    w = o_fox.shape[1]
    tt = 512

    def body(of_ref, od_ref, x_ref, gf_ref, gd_ref, w_ref, x1_ref, on_ref):
        acc = x_ref[...]
        for k, (o_ref, g_ref) in enumerate(((of_ref, gf_ref), (od_ref, gd_ref))):
            o = o_ref[...]
            r = lax.rsqrt(jnp.mean(o * o, axis=-1, keepdims=True) + EPS)
            on = (o * r * g_ref[...]).astype(BF16)
            on_ref[:, k * w:(k + 1) * w] = on
            acc = acc + _dot(on, w_ref[k * w:(k + 1) * w, :])
        x1_ref[...] = acc

    return pl.pallas_call(
        body, name="attn_out", grid=(t // tt,),
        in_specs=[pl.BlockSpec((tt, w), lambda i: (i, 0)), pl.BlockSpec((tt, w), lambda i: (i, 0)),
                  pl.BlockSpec((tt, d), lambda i: (i, 0)), pl.BlockSpec((1, w), lambda i: (0, 0)),
                  pl.BlockSpec((1, w), lambda i: (0, 0)), pl.BlockSpec(memory_space=pltpu.VMEM)],
        out_specs=[pl.BlockSpec((tt, d), lambda i: (i, 0)), pl.BlockSpec((tt, 2 * w), lambda i: (i, 0))],
        out_shape=[jax.ShapeDtypeStruct((t, d), F32), jax.ShapeDtypeStruct((t, 2 * w), BF16)],
        compiler_params=_params(("arbitrary",)),
    )(o_fox, o_dil, x, g_fox, g_dil, w_out)


def _attn_out_bwd(dx1, o_fox, o_dil, g_fox, g_dil, w_out):
    t, d = dx1.shape
    w = o_fox.shape[1]
    tt = 512

    def body(dx_ref, of_ref, od_ref, gf_ref, gd_ref, w_ref, dof_ref, dod_ref, dgf_ref, dgd_ref):
        @pl.when(pl.program_id(0) == 0)
        def _():
            dgf_ref[...] = jnp.zeros_like(dgf_ref)
            dgd_ref[...] = jnp.zeros_like(dgd_ref)

        dxb = dx_ref[...].astype(BF16)
        for k, (o_ref, g_ref, do_ref, dg_ref) in enumerate(
                ((of_ref, gf_ref, dof_ref, dgf_ref), (od_ref, gd_ref, dod_ref, dgd_ref))):
            don = _dot(dxb, w_ref[k * w:(k + 1) * w, :], NT)
            o = o_ref[...]
            r = lax.rsqrt(jnp.mean(o * o, axis=-1, keepdims=True) + EPS)
            xhat = o * r
            u = don * g_ref[...]
            do_ref[...] = r * (u - xhat * jnp.mean(u * xhat, axis=-1, keepdims=True))
            dg_ref[0:1, :] += jnp.sum(don * xhat, axis=0, keepdims=True)

    return pl.pallas_call(
        body, name="attn_out_bwd", grid=(t // tt,),
        in_specs=[pl.BlockSpec((tt, d), lambda i: (i, 0)), pl.BlockSpec((tt, w), lambda i: (i, 0)),
                  pl.BlockSpec((tt, w), lambda i: (i, 0)), pl.BlockSpec((1, w), lambda i: (0, 0)),
                  pl.BlockSpec((1, w), lambda i: (0, 0)), pl.BlockSpec(memory_space=pltpu.VMEM)],
        out_specs=[pl.BlockSpec((tt, w), lambda i: (i, 0)), pl.BlockSpec((tt, w), lambda i: (i, 0)),
                   pl.BlockSpec((8, w), lambda i: (0, 0)), pl.BlockSpec((8, w), lambda i: (0, 0))],
        out_shape=[jax.ShapeDtypeStruct((t, w), F32), jax.ShapeDtypeStruct((t, w), F32),
                   jax.ShapeDtypeStruct((8, w), F32), jax.ShapeDtypeStruct((8, w), F32)],
        compiler_params=_params(("arbitrary",)),
    )(dx1, o_fox, o_dil, g_fox, g_dil, w_out)


def _ffn_fwd(x1, target, g_ffn, w_gate, w_up, w_down):
    t, d = x1.shape
    f = w_gate.shape[1]
    tt = 256

    def body(x_ref, t_ref, g_ref, wg_ref, wu_ref, wd_ref, a_ref, u_ref, dy_ref, loss_ref):
        xx = x_ref[...]
        r = lax.rsqrt(jnp.mean(xx * xx, axis=-1, keepdims=True) + EPS)
        h = (xx * r * g_ref[...]).astype(BF16)
        a = _dot(h, wg_ref[...])
        u = _dot(h, wu_ref[...])
        a_ref[...] = a.astype(BF16)
        u_ref[...] = u.astype(BF16)
        s = (a / (1.0 + jnp.exp(-a)) * u).astype(BF16)
        y = xx + _dot(s, wd_ref[...])
        e = y - t_ref[...]
        dy_ref[...] = e * (1.0 / d)
        loss_ref[...] = jnp.broadcast_to(0.5 * jnp.sum(e * e) * (1.0 / d), (1, 8, LANES))

    return pl.pallas_call(
        body, name="ffn_fwd", grid=(t // tt,),
        in_specs=[pl.BlockSpec((tt, d), lambda i: (i, 0)), pl.BlockSpec((tt, d), lambda i: (i, 0)),
                  pl.BlockSpec((1, d), lambda i: (0, 0)), pl.BlockSpec(memory_space=pltpu.VMEM),
                  pl.BlockSpec(memory_space=pltpu.VMEM), pl.BlockSpec(memory_space=pltpu.VMEM)],
        out_specs=[pl.BlockSpec((tt, f), lambda i: (i, 0)), pl.BlockSpec((tt, f), lambda i: (i, 0)),
                   pl.BlockSpec((tt, d), lambda i: (i, 0)), pl.BlockSpec((1, 8, LANES), lambda i: (i, 0, 0))],
        out_shape=[jax.ShapeDtypeStruct((t, f), BF16), jax.ShapeDtypeStruct((t, f), BF16),
                   jax.ShapeDtypeStruct((t, d), F32), jax.ShapeDtypeStruct((t // tt, 8, LANES), F32)],
        compiler_params=_params(("arbitrary",)),
    )(x1, target, g_ffn, w_gate, w_up, w_down)


def _ffn_bwd(dy, a, u, x1, g_ffn, w_gate, w_up, w_down):
    t, d = x1.shape
    f = w_gate.shape[1]
    tt = 256

    def body(dy_ref, a_ref, u_ref, x_ref, g_ref, wg_ref, wu_ref, wd_ref,
             dx_ref, s_ref, da_ref, du_ref, h_ref, dg_ref):
        @pl.when(pl.program_id(0) == 0)
        def _():
            dg_ref[...] = jnp.zeros_like(dg_ref)

        dy_ = dy_ref[...]
        ds = _dot(dy_.astype(BF16), wd_ref[...], NT)
        a_ = a_ref[...].astype(F32)
        u_ = u_ref[...].astype(F32)
        sig = 1.0 / (1.0 + jnp.exp(-a_))
        silu = a_ * sig
        s_ref[...] = (silu * u_).astype(BF16)
        da = (ds * u_ * (sig * (1.0 + a_ * (1.0 - sig)))).astype(BF16)
        du = (ds * silu).astype(BF16)
        da_ref[...] = da
        du_ref[...] = du
        dh = _dot(da, wg_ref[...], NT) + _dot(du, wu_ref[...], NT)
        xx = x_ref[...]
        r = lax.rsqrt(jnp.mean(xx * xx, axis=-1, keepdims=True) + EPS)
        xhat = xx * r
        g = g_ref[...]
        h_ref[...] = (xhat * g).astype(BF16)
        uu = dh * g
        dx_ref[...] = dy_ + r * (uu - xhat * jnp.mean(uu * xhat, axis=-1, keepdims=True))
        dg_ref[0:1, :] += jnp.sum(dh * xhat, axis=0, keepdims=True)

    return pl.pallas_call(
        body, name="ffn_bwd", grid=(t // tt,),
        in_specs=[pl.BlockSpec((tt, d), lambda i: (i, 0)), pl.BlockSpec((tt, f), lambda i: (i, 0)),
                  pl.BlockSpec((tt, f), lambda i: (i, 0)), pl.BlockSpec((tt, d), lambda i: (i, 0)),
                  pl.BlockSpec((1, d), lambda i: (0, 0)), pl.BlockSpec(memory_space=pltpu.VMEM),
                  pl.BlockSpec(memory_space=pltpu.VMEM), pl.BlockSpec(memory_space=pltpu.VMEM)],
        out_specs=[pl.BlockSpec((tt, d), lambda i: (i, 0)), pl.BlockSpec((tt, f), lambda i: (i, 0)),
                   pl.BlockSpec((tt, f), lambda i: (i, 0)), pl.BlockSpec((tt, f), lambda i: (i, 0)),
                   pl.BlockSpec((tt, d), lambda i: (i, 0)), pl.BlockSpec((8, d), lambda i: (0, 0))],
        out_shape=[jax.ShapeDtypeStruct((t, d), F32), jax.ShapeDtypeStruct((t, f), BF16),
                   jax.ShapeDtypeStruct((t, f), BF16), jax.ShapeDtypeStruct((t, f), BF16),
                   jax.ShapeDtypeStruct((t, d), BF16), jax.ShapeDtypeStruct((8, d), F32)],
        compiler_params=_params(("arbitrary",)),
    )(dy, a, u, x1, g_ffn, w_gate, w_up, w_down)


def _in_proj_bwd(dparts, dfa_row, w1, wft, x, g_mix, dx1):
    t, d = x.shape
    tt = 512
    npart = len(dparts)

    def body(*refs):
        dp_refs = refs[:npart]
        dfa_ref, w_ref, wf_ref, x_ref, g_ref, dx1_ref, dx_ref, dg_ref = refs[npart:]

        @pl.when(pl.program_id(0) == 0)
        def _():
            dg_ref[...] = jnp.zeros_like(dg_ref)

        dh = _dot(dfa_ref[...].astype(BF16), wf_ref[...], TN)
        for j in range(npart):
            dh = dh + _dot(dp_refs[j][...], w_ref[:, j * W_GROUP:(j + 1) * W_GROUP], NT)
        xx = x_ref[...]
        r = lax.rsqrt(jnp.mean(xx * xx, axis=-1, keepdims=True) + EPS)
        xhat = xx * r
        uu = dh * g_ref[...]
        dx_ref[...] = dx1_ref[...] + r * (uu - xhat * jnp.mean(uu * xhat, axis=-1, keepdims=True))
        dg_ref[0:1, :] += jnp.sum(dh * xhat, axis=0, keepdims=True)

    return pl.pallas_call(
        body, name="in_proj_bwd", grid=(t // tt,),
        in_specs=[pl.BlockSpec((tt, W_GROUP), lambda i: (i, 0)) for _ in range(npart)]
        + [pl.BlockSpec((8, tt), lambda i: (0, i)), pl.BlockSpec(memory_space=pltpu.VMEM),
           pl.BlockSpec(memory_space=pltpu.VMEM), pl.BlockSpec((tt, d), lambda i: (i, 0)),
           pl.BlockSpec((1, d), lambda i: (0, 0)), pl.BlockSpec((tt, d), lambda i: (i, 0))],
        out_specs=[pl.BlockSpec((tt, d), lambda i: (i, 0)), pl.BlockSpec((8, d), lambda i: (0, 0))],
        out_shape=[jax.ShapeDtypeStruct((t, d), F32), jax.ShapeDtypeStruct((8, d), F32)],
        compiler_params=_params(("arbitrary",)),
    )(*dparts, dfa_row, w1, wft, x, g_mix, dx1)


def _tn_matmul(a, b, name):
    t, m = a.shape
    n = b.shape[1]
    tm = 512 if m % 512 == 0 else m
    tn = 512 if n % 512 == 0 else (256 if n % 256 == 0 else n)
    tk = 1024
    nk = t // tk

    def body(a_ref, b_ref, o_ref, acc):
        k = pl.program_id(2)

        @pl.when(k == 0)
        def _():
            acc[...] = jnp.zeros_like(acc)

        acc[...] += _dot(a_ref[...].astype(BF16), b_ref[...].astype(BF16), TN)

        @pl.when(k == nk - 1)
        def _():
            o_ref[...] = acc[...]

    return pl.pallas_call(
        body, name=name, grid=(m // tm, n // tn, nk),
        in_specs=[pl.BlockSpec((tk, tm), lambda i, j, k: (k, i)), pl.BlockSpec((tk, tn), lambda i, j, k: (k, j))],
        out_specs=pl.BlockSpec((tm, tn), lambda i, j, k: (i, j)),
        out_shape=jax.ShapeDtypeStruct((m, n), F32),
        scratch_shapes=[pltpu.VMEM((tm, tn), F32)],
        compiler_params=_params(("arbitrary", "arbitrary", "arbitrary")),
    )(a, b)


def _row_matmul(a_row, b, name):
    t, n = b.shape
    tk = 1024
    nk = t // tk

    def body(a_ref, b_ref, o_ref):
        @pl.when(pl.program_id(0) == 0)
        def _():
            o_ref[...] = jnp.zeros_like(o_ref)

        o_ref[...] += _dot(a_ref[...].astype(BF16), b_ref[...])

    return pl.pallas_call(
        body, name=name, grid=(nk,),
        in_specs=[pl.BlockSpec((8, tk), lambda k: (0, k)), pl.BlockSpec((tk, n), lambda k: (k, 0))],
        out_specs=pl.BlockSpec((8, n), lambda k: (0, 0)),
        out_shape=jax.ShapeDtypeStruct((8, n), F32),
        compiler_params=_params(("arbitrary",)),
    )(a_row, b)


FOX_TQ = 256


def _fox_bias_row(c_ref, hd, q0, k0, size):
    c0 = c_ref[0, hd:hd + 1, pl.ds(q0, LANES)][:, 0:1]
    return c0 - c_ref[0, hd:hd + 1, pl.ds(k0, size)]


def _fox_fwd(proj, c3, gq, gk, nb, seq):
    t = nb * seq
    tq = FOX_TQ
    nq = seq // tq
    npair = N_FOX_HEADS // 2

    def body(q_ref, k_ref, v_ref, c_ref, gq_ref, gk_ref, o_ref, lse_ref, qs, ks, vs):
        ones = _group_ones()
        masks = _head_masks()
        qhat, _ = _head_norm(q_ref[...].astype(F32), None, ones)
        khat, _ = _head_norm(k_ref[...].astype(F32), None, ones)
        qs[...] = (qhat * gq_ref[...] * SCALE).astype(BF16)
        kn = khat * gk_ref[...]
        for hd in range(2):
            ks[hd] = (kn * masks[hd]).astype(BF16)
        vs[...] = v_ref[...]
        row = lax.broadcasted_iota(jnp.int32, (tq, tq), 0)
        col = lax.broadcasted_iota(jnp.int32, (tq, tq), 1)
        causal = col <= row

        for qi in range(nq):
            q0 = qi * tq
            q_blk = qs[q0:q0 + tq, :]
            o_tot = jnp.zeros((tq, LANES), F32)
            lse_tot = jnp.zeros((tq, LANES), F32)
            for hd in range(2):
                def kv_step(k0, carry, diag, hd=hd, q0=q0, q_blk=q_blk):
                    m, l, acc = carry
                    s = _dot(q_blk, ks[hd, pl.ds(k0, tq), :], NT) + _fox_bias_row(c_ref, hd, q0, k0, tq)
                    if diag:
                        s = jnp.where(causal, s, NEG)
                    m_new = jnp.maximum(m, jnp.max(s, axis=-1, keepdims=True))
                    alpha = jnp.exp(m - m_new)
                    p = jnp.exp(s - m_new)
                    l = l * alpha + jnp.sum(p, axis=-1, keepdims=True)
                    acc = acc * alpha + _dot(p.astype(BF16), vs[pl.ds(k0, tq), :])
                    return m_new, l, acc

                carry = (jnp.full((tq, 1), NEG, F32), jnp.zeros((tq, 1), F32), jnp.zeros((tq, LANES), F32))
                if qi > 0:
                    carry = lax.fori_loop(
                        0, qi, lambda j, cr: kv_step(pl.multiple_of(j * tq, tq), cr, False), carry)
                m, l, acc = kv_step(q0, carry, True)
                o_tot = o_tot + (acc / l) * masks[hd]
                lse_tot = lse_tot + (m + jnp.log(l)) * masks[hd]
            o_ref[q0:q0 + tq, :] = o_tot
            lse_ref[q0:q0 + tq, :] = lse_tot

    blk = lambda off: pl.BlockSpec((seq, LANES), lambda b, p: (b, off + p))
    return pl.pallas_call(
        body, name="fox_fwd", grid=(nb, npair),
        in_specs=[blk(0), blk(npair), blk(2 * npair), pl.BlockSpec((1, 2, seq), lambda b, p: (p, 0, b)),
                  pl.BlockSpec((1, LANES), lambda b, p: (0, 0)), pl.BlockSpec((1, LANES), lambda b, p: (0, 0))],
        out_specs=[blk(0), blk(0)],
        out_shape=[jax.ShapeDtypeStruct((t, W_GROUP), F32), jax.ShapeDtypeStruct((t, W_GROUP), F32)],
        scratch_shapes=[pltpu.VMEM((seq, LANES), BF16), pltpu.VMEM((2, seq, LANES), BF16),
                        pltpu.VMEM((seq, LANES), BF16)],
        compiler_params=_params(("arbitrary", "arbitrary")),
    )(proj, proj, proj, c3, gq, gk)


def _fox_bwd(proj, c3, gq, gk, do, o, lse, nb, seq):
    t = nb * seq
    tq = FOX_TQ
    nq = seq // tq
    npair = N_FOX_HEADS // 2

    def body(q_ref, k_ref, v_ref, c_ref, gq_ref, gk_ref, do_ref, o_ref, lse_ref,
             dq_ref, dk_ref, dv_ref, dc_ref, dg_ref, qs, ks, vs, dos, delta, dq_acc, dk_acc, dv_acc, row_sum):
        @pl.when((pl.program_id(0) == 0) & (pl.program_id(1) == 0))
        def _():
            dg_ref[...] = jnp.zeros_like(dg_ref)

        ones = _group_ones()
        masks = _head_masks()
        qhat, rq = _head_norm(q_ref[...].astype(F32), None, ones)
        khat, rk = _head_norm(k_ref[...].astype(F32), None, ones)
        qs[...] = (qhat * gq_ref[...] * SCALE).astype(BF16)
        kn = khat * gk_ref[...]
        vv = v_ref[...].astype(F32)
        for hd in range(2):
            ks[hd] = (kn * masks[hd]).astype(BF16)
            vs[hd] = (vv * masks[hd]).astype(BF16)
        dof = do_ref[...]
        dos[...] = dof.astype(BF16)
        delta[...] = _groupsum(dof * o_ref[...], ones)
        dq_acc[...] = jnp.zeros_like(dq_acc)
        dk_acc[...] = jnp.zeros_like(dk_acc)
        dv_acc[...] = jnp.zeros_like(dv_acc)
        row_sum[...] = jnp.zeros_like(row_sum)
        row = lax.broadcasted_iota(jnp.int32, (tq, tq), 0)
        col = lax.broadcasted_iota(jnp.int32, (tq, tq), 1)
        causal = col <= row

        for hd in range(2):
            lane0 = hd * HEAD_DIM
            for kj in range(nq):
                k0 = kj * tq
                k_blk = ks[hd, k0:k0 + tq, :]
                v_blk = vs[hd, k0:k0 + tq, :]

                def q_step(q0, carry, diag, hd=hd, lane0=lane0, k0=k0, k_blk=k_blk, v_blk=v_blk):
                    dk_j, dv_j, dc_j = carry
                    rows = pl.ds(q0, tq)
                    q_blk = qs[rows, :]
                    do_blk = dos[rows, :]
                    s = _dot(q_blk, k_blk, NT) + _fox_bias_row(c_ref, hd, q0, k0, tq)
                    p = jnp.exp(s - lse_ref[rows, lane0:lane0 + 1])
                    if diag:
                        p = jnp.where(causal, p, 0.0)
                    dp = _dot(do_blk, v_blk, NT)
                    ds = p * (dp - delta[rows, lane0:lane0 + 1])
                    dsb = ds.astype(BF16)
                    dv_j = dv_j + _dot(p.astype(BF16), do_blk, TN)
                    dk_j = dk_j + _dot(dsb, q_blk, TN)
                    dq_acc[rows, :] += _dot(dsb, k_blk)
                    dc_j = dc_j - jnp.sum(ds, axis=0, keepdims=True)
                    row_sum[rows, :] += jnp.sum(ds, axis=1, keepdims=True) * masks[hd]
                    return dk_j, dv_j, dc_j

                carry = (jnp.zeros((tq, LANES), F32), jnp.zeros((tq, LANES), F32), jnp.zeros((1, tq), F32))
                carry = q_step(k0, carry, True)
                if kj < nq - 1:
                    carry = lax.fori_loop(
                        kj + 1, nq, lambda i, cr: q_step(pl.multiple_of(i * tq, tq), cr, False), carry)
                dk_j, dv_j, dc_j = carry
                dk_acc[k0:k0 + tq, :] += dk_j * masks[hd]
                dv_acc[k0:k0 + tq, :] += dv_j * masks[hd]
                dc_ref[0, hd:hd + 1, k0:k0 + tq] = dc_j

        for qi in range(nq):
            q0 = qi * tq
            sums = row_sum[q0:q0 + tq, :].T
            for hd in range(2):
                dc_ref[0, hd:hd + 1, q0:q0 + tq] += sums[hd * HEAD_DIM:hd * HEAD_DIM + 1, :]

        dq_raw, dgq = _head_norm_bwd(dq_acc[...] * SCALE, qhat, rq, gq_ref[...], ones)
        dk_raw, dgk = _head_norm_bwd(dk_acc[...], khat, rk, gk_ref[...], ones)
        dq_ref[...] = dq_raw.astype(BF16)
        dk_ref[...] = dk_raw.astype(BF16)
        dv_ref[...] = dv_acc[...].astype(BF16)
        dg_ref[0:1, :] += dgq
        dg_ref[1:2, :] += dgk

    blk = lambda off: pl.BlockSpec((seq, LANES), lambda b, p: (b, off + p))
    vec = pl.BlockSpec((1, LANES), lambda b, p: (0, 0))
    c_spec = pl.BlockSpec((1, 2, seq), lambda b, p: (p, 0, b))
    return pl.pallas_call(
        body, name="fox_bwd", grid=(nb, npair),
        in_specs=[blk(0), blk(npair), blk(2 * npair), c_spec, vec, vec, blk(0), blk(0), blk(0)],
        out_specs=[blk(0), blk(0), blk(0), c_spec, pl.BlockSpec((8, LANES), lambda b, p: (0, 0))],
        out_shape=[jax.ShapeDtypeStruct((t, W_GROUP), BF16), jax.ShapeDtypeStruct((t, W_GROUP), BF16),
                   jax.ShapeDtypeStruct((t, W_GROUP), BF16), jax.ShapeDtypeStruct((npair, 2, t), F32),
                   jax.ShapeDtypeStruct((8, LANES), F32)],
        scratch_shapes=[pltpu.VMEM((seq, LANES), BF16), pltpu.VMEM((2, seq, LANES), BF16),
                        pltpu.VMEM((2, seq, LANES), BF16), pltpu.VMEM((seq, LANES), BF16),
                        pltpu.VMEM((seq, LANES), F32), pltpu.VMEM((seq, LANES), F32),
                        pltpu.VMEM((seq, LANES), F32), pltpu.VMEM((seq, LANES), F32),
                        pltpu.VMEM((seq, LANES), F32)],
        compiler_params=_params(("arbitrary", "arbitrary")),
    )(proj, proj, proj, c3, gq, gk, do, o, lse)


def _dil_block_start(e, d, seq):
    per_res = seq // (d * BAND)
    if per_res == 1:
        return e, None, None
    shift = per_res.bit_length() - 1
    r = e >> shift
    i = e & (per_res - 1)
    start = i * (BAND * d) + r
    has_prev = i > 0
    prev = jnp.where(has_prev, start - BAND * d, start)
    return start, prev, has_prev


def _rows(start, d):
    return pl.ds(start, BAND) if d == 1 else pl.ds(start, BAND, stride=d)


def _dil_masks(has_prev):
    a = lax.broadcasted_iota(jnp.int32, (BAND, 2 * BAND), 0)
    j = lax.broadcasted_iota(jnp.int32, (BAND, 2 * BAND), 1)
    prev_ok = (j < BAND) & (j >= a) & has_prev
    cur_ok = (j >= BAND) & (j - BAND <= a)
    return prev_ok | cur_ok


def _dil_prep(q_ref, k_ref, gq_ref, gk_ref, cos_ref, up_ref, dn_ref, ones):
    qhat, rq = _head_norm(q_ref[...].astype(F32), None, ones)
    khat, rk = _head_norm(k_ref[...].astype(F32), None, ones)
    cos, up, dn = cos_ref[...], up_ref[...], dn_ref[...]
    qn = _rope(qhat * gq_ref[...], cos, up, dn) * SCALE
    kn = _rope(khat * gk_ref[...], cos, up, dn)
    return qhat, rq, khat, rk, qn, kn


def _dil_fwd(proj, gq, gk, cos, up, dn, nb, seq):
    t = nb * seq
    npair = W_GROUP // LANES
    off = 3 * npair
    nblk = seq // BAND

    def body(q_ref, k_ref, v_ref, gq_ref, gk_ref, cos_ref, up_ref, dn_ref, o_ref, lse_ref,
             qs, ks, vs, m_s, l_s, o_s):
        ones = _group_ones()
        masks = _head_masks()
        _, _, _, _, qn, kn = _dil_prep(q_ref, k_ref, gq_ref, gk_ref, cos_ref, up_ref, dn_ref, ones)
        qs[...] = qn
        ks[...] = kn
        vs[...] = v_ref[...].astype(F32)
        m_s[...] = jnp.full_like(m_s, NEG)
        l_s[...] = jnp.zeros_like(l_s)
        o_s[...] = jnp.zeros_like(o_s)
        a_i = lax.broadcasted_iota(jnp.int32, (BAND, BAND), 0)
        j_i = lax.broadcasted_iota(jnp.int32, (BAND, BAND), 1)
        causal = j_i <= a_i

        for d in DILATIONS:
            def block(e, carry, d=d):
                start, prev, has_prev = _dil_block_start(e, d, seq)
                cur = _rows(start, d)
                qb = qs[cur, :]
                if prev is None:
                    kc = ks[cur, :].astype(BF16)
                    vc = vs[cur, :].astype(BF16)
                    ok = causal
                else:
                    pr = _rows(prev, d)
                    kc = jnp.concatenate([ks[pr, :], ks[cur, :]], axis=0).astype(BF16)
                    vc = jnp.concatenate([vs[pr, :], vs[cur, :]], axis=0).astype(BF16)
                    ok = _dil_masks(has_prev)
                m_b = jnp.zeros((BAND, LANES), F32)
                l_b = jnp.zeros((BAND, LANES), F32)
                o_b = jnp.zeros((BAND, LANES), F32)
                for hd in range(2):
                    s = _dot((qb * masks[hd]).astype(BF16), kc, NT)
                    s = jnp.where(ok, s, NEG)
                    m = jnp.max(s, axis=-1, keepdims=True)
                    p = jnp.exp(s - m)
                    m_b = m_b + m * masks[hd]
                    l_b = l_b + jnp.sum(p, axis=-1, keepdims=True) * masks[hd]
                    o_b = o_b + _dot(p.astype(BF16), vc) * masks[hd]
                m_old = m_s[cur, :]
                m_new = jnp.maximum(m_old, m_b)
                w_old = jnp.exp(m_old - m_new)
                w_b = jnp.exp(m_b - m_new)
                l_s[cur, :] = l_s[cur, :] * w_old + l_b * w_b
                o_s[cur, :] = o_s[cur, :] * w_old + o_b * w_b
                m_s[cur, :] = m_new
                return carry

            lax.fori_loop(0, nblk, block, 0)

        l = l_s[...]
        o_ref[...] = o_s[...] / l
        lse_ref[...] = m_s[...] + jnp.log(l)

    blk = lambda o_: pl.BlockSpec((seq, LANES), lambda b, p: (b, o_ + p))
    vec = pl.BlockSpec((1, LANES), lambda b, p: (0, 0))
    tab = pl.BlockSpec((seq, LANES), lambda b, p: (0, 0))
    return pl.pallas_call(
        body, name="dil_fwd", grid=(nb, npair),
        in_specs=[blk(off), blk(off + npair), blk(off + 2 * npair), vec, vec, tab, tab, tab],
        out_specs=[blk(0), blk(0)],
        out_shape=[jax.ShapeDtypeStruct((t, W_GROUP), F32), jax.ShapeDtypeStruct((t, W_GROUP), F32)],
        scratch_shapes=[pltpu.VMEM((seq, LANES), F32) for _ in range(6)],
        compiler_params=_params(("arbitrary", "arbitrary")),
    )(proj, proj, proj, gq, gk, cos, up, dn)


def _dil_bwd(proj, gq, gk, cos, up, dn, do, o, lse, nb, seq):
    t = nb * seq
    npair = W_GROUP // LANES
    off = 3 * npair
    nblk = seq // BAND

    def body(q_ref, k_ref, v_ref, gq_ref, gk_ref, cos_ref, up_ref, dn_ref, do_ref, o_ref, lse_ref,
             dq_ref, dk_ref, dv_ref, dg_ref, qs, ks, vs, delta, dq_s, dk_s, dv_s):
        @pl.when((pl.program_id(0) == 0) & (pl.program_id(1) == 0))
        def _():
            dg_ref[...] = jnp.zeros_like(dg_ref)

        ones = _group_ones()
        masks = _head_masks()
        qhat, rq, khat, rk, qn, kn = _dil_prep(q_ref, k_ref, gq_ref, gk_ref, cos_ref, up_ref, dn_ref, ones)
        qs[...] = qn
        ks[...] = kn
        vs[...] = v_ref[...].astype(F32)
        delta[...] = _groupsum(do_ref[...] * o_ref[...], ones)
        dq_s[...] = jnp.zeros_like(dq_s)
        dk_s[...] = jnp.zeros_like(dk_s)
        dv_s[...] = jnp.zeros_like(dv_s)
        a_i = lax.broadcasted_iota(jnp.int32, (BAND, BAND), 0)
        j_i = lax.broadcasted_iota(jnp.int32, (BAND, BAND), 1)
        causal = j_i <= a_i

        for d in DILATIONS:
            def block(e, carry, d=d):
                start, prev, has_prev = _dil_block_start(e, d, seq)
                cur = _rows(start, d)
                qb = qs[cur, :]
                dob = do_ref[cur, :]
                lse_b = lse_ref[cur, :]
                delta_b = delta[cur, :]
                if prev is None:
                    kc = ks[cur, :].astype(BF16)
                    vc = vs[cur, :].astype(BF16)
                    ok = causal
                else:
                    pr = _rows(prev, d)
                    kc = jnp.concatenate([ks[pr, :], ks[cur, :]], axis=0).astype(BF16)
                    vc = jnp.concatenate([vs[pr, :], vs[cur, :]], axis=0).astype(BF16)
                    ok = _dil_masks(has_prev)
                dq_b = jnp.zeros((BAND, LANES), F32)
                dk_b = jnp.zeros((kc.shape[0], LANES), F32)
                dv_b = jnp.zeros((kc.shape[0], LANES), F32)
                for hd in range(2):
                    lane0 = hd * HEAD_DIM
                    qm = (qb * masks[hd]).astype(BF16)
                    dom = (dob * masks[hd]).astype(BF16)
                    s = _dot(qm, kc, NT)
                    p = jnp.where(ok, jnp.exp(s - lse_b[:, lane0:lane0 + 1]), 0.0)
                    dp = _dot(dom, vc, NT)
                    ds = (p * (dp - delta_b[:, lane0:lane0 + 1])).astype(BF16)
                    dq_b = dq_b + _dot(ds, kc) * masks[hd]
                    dk_b = dk_b + _dot(ds, qm, TN)
                    dv_b = dv_b + _dot(p.astype(BF16), dom, TN)
                dq_s[cur, :] += dq_b
                if prev is None:
                    dk_s[cur, :] += dk_b
                    dv_s[cur, :] += dv_b
                else:
                    dk_s[pr, :] += dk_b[:BAND]
                    dv_s[pr, :] += dv_b[:BAND]
                    dk_s[cur, :] += dk_b[BAND:]
                    dv_s[cur, :] += dv_b[BAND:]
                return carry

            lax.fori_loop(0, nblk, block, 0)

        cos, up, dn = cos_ref[...], up_ref[...], dn_ref[...]
        dq_raw, dgq = _head_norm_bwd(_rope_bwd(dq_s[...] * SCALE, cos, up, dn), qhat, rq, gq_ref[...], ones)
        dk_raw, dgk = _head_norm_bwd(_rope_bwd(dk_s[...], cos, up, dn), khat, rk, gk_ref[...], ones)
        dq_ref[...] = dq_raw.astype(BF16)
        dk_ref[...] = dk_raw.astype(BF16)
        dv_ref[...] = dv_s[...].astype(BF16)
        dg_ref[0:1, :] += dgq
        dg_ref[1:2, :] += dgk

    blk = lambda o_: pl.BlockSpec((seq, LANES), lambda b, p: (b, o_ + p))
    vec = pl.BlockSpec((1, LANES), lambda b, p: (0, 0))
    tab = pl.BlockSpec((seq, LANES), lambda b, p: (0, 0))
    return pl.pallas_call(
        body, name="dil_bwd", grid=(nb, npair),
        in_specs=[blk(off), blk(off + npair), blk(off + 2 * npair), vec, vec, tab, tab, tab,
                  blk(0), blk(0), blk(0)],
        out_specs=[blk(0), blk(0), blk(0), pl.BlockSpec((8, LANES), lambda b, p: (0, 0))],
        out_shape=[jax.ShapeDtypeStruct((t, W_GROUP), BF16), jax.ShapeDtypeStruct((t, W_GROUP), BF16),
                   jax.ShapeDtypeStruct((t, W_GROUP), BF16), jax.ShapeDtypeStruct((8, LANES), F32)],
        scratch_shapes=[pltpu.VMEM((seq, LANES), F32) for _ in range(7)],
        compiler_params=_params(("arbitrary", "arbitrary")),
    )(proj, proj, proj, gq, gk, cos, up, dn, do, o, lse)


def _adamw(w, g, m, v, name):
    rows, cols = w.shape
    tr = _row_tile(rows) if rows >= 8 else rows
    c1 = 1.0 - ADAM_B1 ** ADAM_STEP
    c2 = 1.0 - ADAM_B2 ** ADAM_STEP

    def body(w_ref, g_ref, m_ref, v_ref, d_ref, nm_ref, nv_ref):
        g_ = g_ref[...]
        nm = ADAM_B1 * m_ref[...] + (1.0 - ADAM_B1) * g_
        nv = ADAM_B2 * v_ref[...] + (1.0 - ADAM_B2) * (g_ * g_)
        nm_ref[...] = nm
        nv_ref[...] = nv
        d_ref[...] = -ADAM_LR * ((nm / c1) / (jnp.sqrt(nv / c2) + ADAM_EPS) + ADAM_WD * w_ref[...])

    spec = pl.BlockSpec((tr, cols), lambda i: (i, 0))
    shape = jax.ShapeDtypeStruct((rows, cols), F32)
    return pl.pallas_call(
        body, name=name, grid=(rows // tr,), in_specs=[spec] * 4, out_specs=[spec] * 3,
        out_shape=[shape] * 3, compiler_params=_params(("arbitrary",)),
    )(w, g, m, v)


def _place():
    x, y, c = lax.axis_index("x"), lax.axis_index("y"), lax.axis_index("c")
    chips = [(1 - x, y), (x, 1 - y), (1 - x, 1 - y)]
    return x, y, c, chips


def _gather_weight(w, name):
    rows, cols = w.shape
    half_rows = rows // 2

    def body(w_ref, out_ref, send_sems, recv_sems):
        x, y, c, chips = _place()
        sibling = (x, y, 1 - c)
        mine = 2 * x + y
        lo = pl.multiple_of(c * half_rows, 16)
        lo_sib = pl.multiple_of((1 - c) * half_rows, 16)
        out_ref[mine] = w_ref[...].astype(BF16)

        def copy(k, shard, first_row, to):
            ref = out_ref.at[shard, pl.ds(first_row, half_rows), :]
            return pltpu.make_async_remote_copy(src_ref=ref, dst_ref=ref, send_sem=send_sems.at[k],
                                                recv_sem=recv_sems.at[k], device_id=to, device_id_type=MESH)

        sends = [copy(k, mine, lo, (cx, cy, c)) for k, (cx, cy) in enumerate(chips)]
        for cp in sends:
            cp.start()
        passed = []
        for k, (cx, cy) in enumerate(chips):
            theirs = 2 * cx + cy
            copy(k, theirs, lo, (cx, cy, c)).wait_recv()
            fw = copy(3 + k, theirs, lo, sibling)
            fw.start()
            passed.append(fw)
        for k, (cx, cy) in enumerate(chips):
            copy(3 + k, 2 * cx + cy, lo_sib, sibling).wait_recv()
        for cp in sends + passed:
            cp.wait_send()

    return pl.pallas_call(
        body, name=name,
        in_specs=[pl.BlockSpec(memory_space=pltpu.VMEM)],
        out_specs=pl.BlockSpec(memory_space=pltpu.VMEM),
        out_shape=jax.ShapeDtypeStruct((4, rows, cols), BF16),
        scratch_shapes=[pltpu.SemaphoreType.DMA((6,)), pltpu.SemaphoreType.DMA((6,))],
        compiler_params=pltpu.CompilerParams(vmem_limit_bytes=VMEM_LIMIT),
    )(w)


def _reduce_scatter_weight(g4, name):
    _, rows, cols = g4.shape
    half_rows = rows // 2

    def body(g_ref, out_ref, sib_buf, stage, landed, send_sems, recv_sems):
        x, y, c, chips = _place()
        sibling = (x, y, 1 - c)
        mine = 2 * x + y
        lo = pl.multiple_of(c * half_rows, 8)
        lo_sib = pl.multiple_of((1 - c) * half_rows, 8)

        def copy(k, src, dst, to):
            return pltpu.make_async_remote_copy(src_ref=src, dst_ref=dst, send_sem=send_sems.at[k],
                                                recv_sem=recv_sems.at[k], device_id=to, device_id_type=MESH)

        swap = copy(0, g_ref.at[:, pl.ds(lo_sib, half_rows), :], sib_buf, sibling)
        swap.start()
        swap.wait_recv()
        sends = []
        for k, (cx, cy) in enumerate(chips):
            theirs = 2 * cx + cy
            stage[k] = (g_ref[theirs, pl.ds(lo, half_rows), :] + sib_buf[theirs]).astype(BF16)
            cp = copy(1 + k, stage.at[k], landed.at[k], (cx, cy, c))
            cp.start()
            sends.append(cp)
        acc = g_ref[mine, pl.ds(lo, half_rows), :] + sib_buf[mine]
        for k, (cx, cy) in enumerate(chips):
            copy(1 + k, stage.at[k], landed.at[k], (cx, cy, c)).wait_recv()
            acc = acc + landed[k].astype(F32)
        out_ref[pl.ds(lo, half_rows), :] = acc
        done = copy(4, out_ref.at[pl.ds(lo, half_rows), :], out_ref.at[pl.ds(lo, half_rows), :], sibling)
        done.start()
        copy(4, out_ref.at[pl.ds(lo_sib, half_rows), :], out_ref.at[pl.ds(lo_sib, half_rows), :], sibling).wait_recv()
        for cp in [swap] + sends + [done]:
            cp.wait_send()

    return pl.pallas_call(
        body, name=name,
        in_specs=[pl.BlockSpec(memory_space=pltpu.VMEM)],
        out_specs=pl.BlockSpec(memory_space=pltpu.VMEM),
        out_shape=jax.ShapeDtypeStruct((rows, cols), F32),
        scratch_shapes=[pltpu.VMEM((4, half_rows, cols), F32), pltpu.VMEM((3, half_rows, cols), BF16),
                        pltpu.VMEM((3, half_rows, cols), BF16),
                        pltpu.SemaphoreType.DMA((5,)), pltpu.SemaphoreType.DMA((5,))],
        compiler_params=pltpu.CompilerParams(vmem_limit_bytes=VMEM_LIMIT),
    )(g4)


def _all_sum_small(v):
    shape = v.shape

    def body(v_ref, out_ref, buf, send_sems, recv_sems):
        x, y, c, _ = _place()
        me = 4 * x + 2 * y + c
        buf[me] = v_ref[...]
        flips = [(dx, dy, dc) for dx in (0, 1) for dy in (0, 1) for dc in (0, 1)][1:]

        def copy(k, slot, flip):
            dx, dy, dc = flip
            to = (1 - x if dx else x, 1 - y if dy else y, 1 - c if dc else c)
            return pltpu.make_async_remote_copy(src_ref=buf.at[slot], dst_ref=buf.at[slot], send_sem=send_sems.at[k],
                                                recv_sem=recv_sems.at[k], device_id=to, device_id_type=MESH)

        sends = [copy(k, me, flip) for k, flip in enumerate(flips)]
        for cp in sends:
            cp.start()
        for k, (dx, dy, dc) in enumerate(flips):
            sender = 4 * (1 - x if dx else x) + 2 * (1 - y if dy else y) + (1 - c if dc else c)
            copy(k, sender, (dx, dy, dc)).wait_recv()
        for cp in sends:
            cp.wait_send()
        total = buf[0]
        for i in range(1, 8):
            total = total + buf[i]
        out_ref[...] = total

    return pl.pallas_call(
        body, name="all_sum_small",
        in_specs=[pl.BlockSpec(memory_space=pltpu.VMEM)],
        out_specs=pl.BlockSpec(memory_space=pltpu.VMEM),
        out_shape=jax.ShapeDtypeStruct(shape, F32),
        scratch_shapes=[pltpu.VMEM((8,) + shape, F32), pltpu.SemaphoreType.DMA((7,)), pltpu.SemaphoreType.DMA((7,))],
    )(v)


SMALL = (("g_mix", 1024), ("g_ffn", 1024), ("g_out_fox", 512), ("g_out_dil", 512), ("g_q_fox", 64),
         ("g_k_fox", 64), ("g_q_dil", 64), ("g_k_dil", 64), ("b_forget", 8))
SMALL_PACKED = (32, LANES)


def _local_grads(x, target, gains, w1, wft, w_out, w_gate, w_up, w_down, nb, seq):
    tile2 = lambda g: jnp.tile(g, (1, 2))
    gq_f, gk_f, gq_d, gk_d = (tile2(gains[n]) for n in ("g_q_fox", "g_k_fox", "g_q_dil", "g_k_dil"))
    b_col = gains["b_forget"].reshape(N_FOX_HEADS, 1)
    cos, up, dn = _rope_tables(seq)
    npair = N_FOX_HEADS // 2

    proj, fa_row, h1 = _in_proj(x, gains["g_mix"], w1, wft)
    c_row = _gate_fwd(fa_row, b_col, seq)
    c3 = c_row.reshape(npair, 2, nb * seq)
    o_fox, lse_fox = _fox_fwd(proj, c3, gq_f, gk_f, nb, seq)
    o_dil, lse_dil = _dil_fwd(proj, gq_d, gk_d, cos, up, dn, nb, seq)
    x1, o_n = _attn_out(o_fox, o_dil, x, gains["g_out_fox"], gains["g_out_dil"], w_out)
    a, u, dy, loss_parts = _ffn_fwd(x1, target, gains["g_ffn"], w_gate, w_up, w_down)
    loss = jnp.sum(loss_parts[:, 0, 0])

    dx1, s, da, du, h2, dg_ffn = _ffn_bwd(dy, a, u, x1, gains["g_ffn"], w_gate, w_up, w_down)
    d_w_down = _tn_matmul(s, dy, "dw_down")
    d_w_gate = _tn_matmul(h2, da, "dw_gate")
    d_w_up = _tn_matmul(h2, du, "dw_up")
    d_w_out = _tn_matmul(o_n, dx1, "dw_out")
    do_fox, do_dil, dg_of, dg_od = _attn_out_bwd(dx1, o_fox, o_dil, gains["g_out_fox"], gains["g_out_dil"], w_out)
    dq_f, dk_f, dv_f, dc3, dg_fox = _fox_bwd(proj, c3, gq_f, gk_f, do_fox, o_fox, lse_fox, nb, seq)
    dq_d, dk_d, dv_d, dg_dil = _dil_bwd(proj, gq_d, gk_d, cos, up, dn, do_dil, o_dil, lse_dil, nb, seq)
    dfa_row, db = _gate_bwd(dc3.reshape(N_FOX_HEADS, nb * seq), fa_row, b_col, seq)
    dparts = [dq_f, dk_f, dv_f, dq_d, dk_d, dv_d]
    grad_x, dg_mix = _in_proj_bwd(dparts, dfa_row, w1, wft, x, gains["g_mix"], dx1)
    d_w1 = jnp.concatenate([_tn_matmul(h1, dp, "dw_in_%d" % j) for j, dp in enumerate(dparts)], axis=1)
    d_wf = _row_matmul(dfa_row, h1, "dw_forget")

    fold = lambda g2: (g2[:, :HEAD_DIM] + g2[:, HEAD_DIM:])
    small = {
        "g_mix": dg_mix[0:1], "g_ffn": dg_ffn[0:1], "g_out_fox": dg_of[0:1], "g_out_dil": dg_od[0:1],
        "g_q_fox": fold(dg_fox[0:1]), "g_k_fox": fold(dg_fox[1:2]),
        "g_q_dil": fold(dg_dil[0:1]), "g_k_dil": fold(dg_dil[1:2]),
        "b_forget": db[:, 0].reshape(1, N_FOX_HEADS),
    }
    big = {"w1": d_w1, "wf": d_wf, "w_out": d_w_out, "w_gate": d_w_gate, "w_up": d_w_up, "w_down": d_w_down}
    return loss, grad_x, big, small


def _shards_of_columns(full, n=4):
    r, nc = full.shape
    return full.reshape(r, n, nc // n).transpose(1, 0, 2)


def _columns_of_shards(slabs):
    n, r, c = slabs.shape
    return slabs.transpose(1, 0, 2).reshape(r, n * c)


def kernel(x, g_mix, w_in, b_forget, g_q_fox, g_k_fox, g_q_dil, g_k_dil, g_out_fox, g_out_dil, w_out, g_ffn, w_gate, w_up, w_down, loss_target, m_g_mix, m_w_in, m_b_forget, m_g_q_fox, m_g_k_fox, m_g_q_dil, m_g_k_dil, m_g_out_fox, m_g_out_dil, m_w_out, m_g_ffn, m_w_gate, m_w_up, m_w_down, v_g_mix, v_w_in, v_b_forget, v_g_q_fox, v_g_k_fox, v_g_q_dil, v_g_k_dil, v_g_out_fox, v_g_out_dil, v_w_out, v_g_ffn, v_w_gate, v_w_up, v_w_down):
    nb, seq, d = x.shape
    weights = dict(g_mix=g_mix, w_in=w_in, b_forget=b_forget, g_q_fox=g_q_fox, g_k_fox=g_k_fox, g_q_dil=g_q_dil,
                   g_k_dil=g_k_dil, g_out_fox=g_out_fox, g_out_dil=g_out_dil, w_out=w_out, g_ffn=g_ffn,
                   w_gate=w_gate, w_up=w_up, w_down=w_down)
    m_in = dict(g_mix=m_g_mix, w_in=m_w_in, b_forget=m_b_forget, g_q_fox=m_g_q_fox, g_k_fox=m_g_k_fox,
                g_q_dil=m_g_q_dil, g_k_dil=m_g_k_dil, g_out_fox=m_g_out_fox, g_out_dil=m_g_out_dil, w_out=m_w_out,
                g_ffn=m_g_ffn, w_gate=m_w_gate, w_up=m_w_up, w_down=m_w_down)
    v_in = dict(g_mix=v_g_mix, w_in=v_w_in, b_forget=v_b_forget, g_q_fox=v_g_q_fox, g_k_fox=v_g_k_fox,
                g_q_dil=v_g_q_dil, g_k_dil=v_g_k_dil, g_out_fox=v_g_out_fox, g_out_dil=v_g_out_dil, w_out=v_w_out,
                g_ffn=v_g_ffn, w_gate=v_w_gate, w_up=v_w_up, w_down=v_w_down)
    order = ["g_mix", "w_in", "b_forget", "g_q_fox", "g_k_fox", "g_q_dil", "g_k_dil", "g_out_fox", "g_out_dil",
             "w_out", "g_ffn", "w_gate", "w_up", "w_down"]

    w_in_full = _columns_of_shards(_gather_weight(w_in[0], "gather_w_in"))
    fox_w = 3 * W_GROUP
    w1 = jnp.concatenate([w_in_full[:, :fox_w], w_in_full[:, fox_w + N_FOX_HEADS:]], axis=1)
    wft = w_in_full[:, fox_w:fox_w + N_FOX_HEADS].T
    w_out_full = _gather_weight(w_out[0], "gather_w_out").reshape(d, d)
    w_gate_full = _columns_of_shards(_gather_weight(w_gate[0], "gather_w_gate"))
    w_up_full = _columns_of_shards(_gather_weight(w_up[0], "gather_w_up"))
    w_down_full = _gather_weight(w_down[0], "gather_w_down").reshape(-1, d)

    gains = {n: weights[n] for n, _ in SMALL}
    loss, grad_x, big, small = _local_grads(
        x.reshape(nb * seq, d), loss_target.reshape(nb * seq, d), gains,
        w1, wft, w_out_full, w_gate_full, w_up_full, w_down_full, nb, seq)
    loss = lax.psum(loss, ("x", "y", "c"))

    d_w_in_full = jnp.concatenate([big["w1"][:, :fox_w], big["wf"].T, big["w1"][:, fox_w:]], axis=1)
    grads = {
        "w_in": _reduce_scatter_weight(_shards_of_columns(d_w_in_full), "reduce_w_in"),
        "w_out": _reduce_scatter_weight(big["w_out"].reshape(4, d // 4, d), "reduce_w_out"),
        "w_gate": _reduce_scatter_weight(_shards_of_columns(big["w_gate"]), "reduce_w_gate"),
        "w_up": _reduce_scatter_weight(_shards_of_columns(big["w_up"]), "reduce_w_up"),
        "w_down": _reduce_scatter_weight(big["w_down"].reshape(4, -1, d), "reduce_w_down"),
    }
    packed = jnp.concatenate([small[n].reshape(-1) for n, _ in SMALL])
    packed = jnp.pad(packed, (0, SMALL_PACKED[0] * SMALL_PACKED[1] - packed.shape[0])).reshape(SMALL_PACKED)
    summed = _all_sum_small(packed).reshape(-1)
    pos = 0
    for n, size in SMALL:
        grads[n] = summed[pos:pos + size].reshape(1, size)
        pos += size

    deltas, new_m, new_v, grad_out = {}, {}, {}, {}
    for n in order:
        w2, m2, v2 = weights[n], m_in[n], v_in[n]
        shape = w2.shape
        flat = (lambda a: a.reshape(shape[-2], shape[-1])) if w2.ndim == 3 else (lambda a: a)
        g2 = grads[n]
        dl, nm, nv = _adamw(flat(w2), g2, flat(m2), flat(v2), "adamw_" + n)
        grad_out[n] = g2.reshape(shape)
        deltas[n], new_m[n], new_v[n] = dl.reshape(shape), nm.reshape(shape), nv.reshape(shape)

    return (loss, grad_x.reshape(nb, seq, d), *[grad_out[n] for n in order], *[deltas[n] for n in order],
            *[new_m[n] for n in order], *[new_v[n] for n in order])
```

```python
import functools
import math

import numpy as np
import jax
import jax.numpy as jnp
from jax import lax
from jax.experimental import pallas as pl
from jax.experimental.pallas import tpu as pltpu

F32, BF16 = jnp.float32, jnp.bfloat16
MESH = pl.DeviceIdType.MESH

EPS = 1e-6
NEG = -1e30
HEAD_DIM = 64
SCALE = HEAD_DIM ** -0.5
LOG2E = math.log2(math.e)
LN2 = math.log(2.0)
ROPE_THETA = 500000.0
ROPE_DIM = HEAD_DIM // 4
LANES = 128
W_GROUP = 512
N_FOX_HEADS = 8
VMEM_LIMIT = 56 * 1024 * 1024
DILATIONS = (1, 4, 16)
BAND = 128

ADAM_LR, ADAM_B1, ADAM_B2, ADAM_EPS, ADAM_WD, ADAM_STEP = 0.001, 0.9, 0.999, 1e-08, 0.01, 10

NT = (((1,), (1,)), ((), ()))
TN = (((0,), (0,)), ((), ()))


def _params(sem=None):
    return pltpu.CompilerParams(dimension_semantics=sem, vmem_limit_bytes=VMEM_LIMIT)


def _dot(a, b, dims=None):
    if dims is None:
        return jnp.dot(a, b, preferred_element_type=F32)
    return lax.dot_general(a, b, dims, preferred_element_type=F32)


def _group_ones():
    i = lax.broadcasted_iota(jnp.int32, (LANES, LANES), 0) >> 6
    j = lax.broadcasted_iota(jnp.int32, (LANES, LANES), 1) >> 6
    return (i == j).astype(BF16)


def _split3(x):
    a = x.astype(BF16)
    r = x - a.astype(F32)
    b = r.astype(BF16)
    c = (r - b.astype(F32)).astype(BF16)
    return a, b, c


def _groupsum(x, ones):
    a, b, c = _split3(x)
    return _dot(a, ones) + _dot(b, ones) + _dot(c, ones)


def _head_masks():
    lane = lax.broadcasted_iota(jnp.int32, (1, LANES), 1)
    return [(lane < HEAD_DIM).astype(F32), (lane >= HEAD_DIM).astype(F32)]


def _head_norm(raw, gain, ones):
    r = lax.rsqrt(_groupsum(raw * raw, ones) * (1.0 / HEAD_DIM) + EPS)
    return raw * r, r


def _head_norm_bwd(dy, xhat, r, gain, ones):
    u = dy * gain
    dgain = jnp.sum(dy * xhat, axis=0, keepdims=True)
    draw = r * (u - xhat * (_groupsum(u * xhat, ones) * (1.0 / HEAD_DIM)))
    return draw, dgain


def _rope(x, cos, s_up, s_dn):
    return x * cos + pltpu.roll(x, LANES - 8, 1) * s_up + pltpu.roll(x, 8, 1) * s_dn


def _rope_bwd(dy, cos, s_up, s_dn):
    return dy * cos + pltpu.roll(dy * s_up, 8, 1) + pltpu.roll(dy * s_dn, LANES - 8, 1)


def _rope_tables(seq):
    half = ROPE_DIM // 2
    inv_freq = jnp.power(jnp.float32(ROPE_THETA), -jnp.arange(half, dtype=F32) * 2.0 / ROPE_DIM)
    ang = jnp.arange(seq).astype(F32)[:, None] * inv_freq[None, :]
    cos, sin = jnp.cos(ang), jnp.sin(ang)
    one = jnp.ones((seq, HEAD_DIM - ROPE_DIM), F32)
    zero_h = jnp.zeros((seq, half), F32)
    zero_r = jnp.zeros((seq, HEAD_DIM - ROPE_DIM), F32)
    c = jnp.concatenate([cos, cos, one], axis=1)
    up = jnp.concatenate([-sin, zero_h, zero_r], axis=1)
    dn = jnp.concatenate([zero_h, sin, zero_r], axis=1)
    return jnp.tile(c, (1, 2)), jnp.tile(up, (1, 2)), jnp.tile(dn, (1, 2))


def _row_tile(rows, cap=256):
    best = rows
    for t in range(8, min(rows, cap) + 1, 8):
        if rows % t == 0:
            best = t
    return best


def _in_proj(x, g_mix, w1, wft):
    t, d = x.shape
    n = w1.shape[1]
    tt = 512

    def body(x_ref, g_ref, w_ref, wf_ref, p_ref, fa_ref, h_ref):
        xx = x_ref[...]
        r = lax.rsqrt(jnp.mean(xx * xx, axis=-1, keepdims=True) + EPS)
        h = (xx * r * g_ref[...]).astype(BF16)
        h_ref[...] = h
        for j in range(n // W_GROUP):
            cols = slice(j * W_GROUP, (j + 1) * W_GROUP)
            p_ref[:, cols] = _dot(h, w_ref[:, cols]).astype(BF16)
        fa_ref[...] = _dot(wf_ref[...], h, NT)

    return pl.pallas_call(
        body, name="in_proj", grid=(t // tt,),
        in_specs=[pl.BlockSpec((tt, d), lambda i: (i, 0)), pl.BlockSpec((1, d), lambda i: (0, 0)),
                  pl.BlockSpec(memory_space=pltpu.VMEM), pl.BlockSpec(memory_space=pltpu.VMEM)],
        out_specs=[pl.BlockSpec((tt, n), lambda i: (i, 0)), pl.BlockSpec((8, tt), lambda i: (0, i)),
                   pl.BlockSpec((tt, d), lambda i: (i, 0))],
        out_shape=[jax.ShapeDtypeStruct((t, n), BF16), jax.ShapeDtypeStruct((8, t), F32),
                   jax.ShapeDtypeStruct((t, d), BF16)],
        compiler_params=_params(("arbitrary",)),
    )(x, g_mix, w1, wft)


def _tri(n, upper):
    i = lax.broadcasted_iota(jnp.int32, (n, n), 0)
    j = lax.broadcasted_iota(jnp.int32, (n, n), 1)
    return ((i <= j) if upper else (i >= j)).astype(BF16)


def _gate_fwd(fa_row, b_col, seq):
    t = fa_row.shape[1]
    cb = 256

    def body(fa_ref, b_ref, c_ref):
        tri = _tri(cb, True)
        carry = jnp.zeros((8, 1), F32)
        for k in range(seq // cb):
            z = fa_ref[:, k * cb:(k + 1) * cb] + b_ref[...]
            lf = jnp.minimum(z, 0.0) - jnp.log(1.0 + jnp.exp(-jnp.abs(z)))
            a, b, c = _split3(lf)
            blk = _dot(a, tri) + _dot(b, tri) + _dot(c, tri) + carry
            c_ref[:, k * cb:(k + 1) * cb] = blk
            carry = blk[:, cb - 1:cb]

    return pl.pallas_call(
        body, name="gate_fwd", grid=(t // seq,),
        in_specs=[pl.BlockSpec((8, seq), lambda i: (0, i)), pl.BlockSpec((8, 1), lambda i: (0, 0))],
        out_specs=pl.BlockSpec((8, seq), lambda i: (0, i)),
        out_shape=jax.ShapeDtypeStruct((8, t), F32),
        compiler_params=_params(("arbitrary",)),
    )(fa_row, b_col)


def _gate_bwd(dc_row, fa_row, b_col, seq):
    t = fa_row.shape[1]
    cb = 256

    def body(dc_ref, fa_ref, b_ref, dfa_ref, db_ref):
        @pl.when(pl.program_id(0) == 0)
        def _():
            db_ref[...] = jnp.zeros_like(db_ref)

        tri = _tri(cb, False)
        carry = jnp.zeros((8, 1), F32)
        dbs = jnp.zeros((8, 1), F32)
        for k in reversed(range(seq // cb)):
            a, b, c = _split3(dc_ref[:, k * cb:(k + 1) * cb])
            dlf = _dot(a, tri) + _dot(b, tri) + _dot(c, tri) + carry
            carry = dlf[:, 0:1]
            z = fa_ref[:, k * cb:(k + 1) * cb] + b_ref[...]
            dfa = dlf / (1.0 + jnp.exp(z))
            dfa_ref[:, k * cb:(k + 1) * cb] = dfa
            dbs = dbs + jnp.sum(dfa, axis=1, keepdims=True)
        db_ref[...] += jnp.broadcast_to(dbs, (8, LANES))

    return pl.pallas_call(
        body, name="gate_bwd", grid=(t // seq,),
        in_specs=[pl.BlockSpec((8, seq), lambda i: (0, i)), pl.BlockSpec((8, seq), lambda i: (0, i)),
                  pl.BlockSpec((8, 1), lambda i: (0, 0))],
        out_specs=[pl.BlockSpec((8, seq), lambda i: (0, i)), pl.BlockSpec((8, LANES), lambda i: (0, 0))],
        out_shape=[jax.ShapeDtypeStruct((8, t), F32), jax.ShapeDtypeStruct((8, LANES), F32)],
        compiler_params=_params(("arbitrary",)),
    )(dc_row, fa_row, b_col)


def _attn_out(o_fox, o_dil, x, g_fox, g_dil, w_out):
    t, d = x.shape
    w = o_fox.shape[1]
    tt = 512

    def body(of_ref, od_ref, x_ref, gf_ref, gd_ref, w_ref, x1_ref, on_ref):
        acc = x_ref[...]
        for k, (o_ref, g_ref) in enumerate(((of_ref, gf_ref), (od_ref, gd_ref))):
            o = o_ref[...]
            r = lax.rsqrt(jnp.mean(o * o, axis=-1, keepdims=True) + EPS)
            on = (o * r * g_ref[...]).astype(BF16)
            on_ref[:, k * w:(k + 1) * w] = on
            acc = acc + _dot(on, w_ref[k * w:(k + 1) * w, :])
        x1_ref[...] = acc

    return pl.pallas_call(
        body, name="attn_out", grid=(t // tt,),
        in_specs=[pl.BlockSpec((tt, w), lambda i: (i, 0)), pl.BlockSpec((tt, w), lambda i: (i, 0)),
                  pl.BlockSpec((tt, d), lambda i: (i, 0)), pl.BlockSpec((1, w), lambda i: (0, 0)),
                  pl.BlockSpec((1, w), lambda i: (0, 0)), pl.BlockSpec(memory_space=pltpu.VMEM)],
        out_specs=[pl.BlockSpec((tt, d), lambda i: (i, 0)), pl.BlockSpec((tt, 2 * w), lambda i: (i, 0))],
        out_shape=[jax.ShapeDtypeStruct((t, d), F32), jax.ShapeDtypeStruct((t, 2 * w), BF16)],
        compiler_params=_params(("arbitrary",)),
    )(o_fox, o_dil, x, g_fox, g_dil, w_out)


def _attn_out_bwd(dx1, o_fox, o_dil, g_fox, g_dil, w_out):
    t, d = dx1.shape
    w = o_fox.shape[1]
    tt = 512

    def body(dx_ref, of_ref, od_ref, gf_ref, gd_ref, w_ref, dof_ref, dod_ref, dgf_ref, dgd_ref):
        @pl.when(pl.program_id(0) == 0)
        def _():
            dgf_ref[...] = jnp.zeros_like(dgf_ref)
            dgd_ref[...] = jnp.zeros_like(dgd_ref)

        dxb = dx_ref[...].astype(BF16)
        for k, (o_ref, g_ref, do_ref, dg_ref) in enumerate(
                ((of_ref, gf_ref, dof_ref, dgf_ref), (od_ref, gd_ref, dod_ref, dgd_ref))):
            don = _dot(dxb, w_ref[k * w:(k + 1) * w, :], NT)
            o = o_ref[...]
            r = lax.rsqrt(jnp.mean(o * o, axis=-1, keepdims=True) + EPS)
            xhat = o * r
            u = don * g_ref[...]
            do_ref[...] = r * (u - xhat * jnp.mean(u * xhat, axis=-1, keepdims=True))
            dg_ref[0:1, :] += jnp.sum(don * xhat, axis=0, keepdims=True)

    return pl.pallas_call(
        body, name="attn_out_bwd", grid=(t // tt,),
        in_specs=[pl.BlockSpec((tt, d), lambda i: (i, 0)), pl.BlockSpec((tt, w), lambda i: (i, 0)),
                  pl.BlockSpec((tt, w), lambda i: (i, 0)), pl.BlockSpec((1, w), lambda i: (0, 0)),
                  pl.BlockSpec((1, w), lambda i: (0, 0)), pl.BlockSpec(memory_space=pltpu.VMEM)],
        out_specs=[pl.BlockSpec((tt, w), lambda i: (i, 0)), pl.BlockSpec((tt, w), lambda i: (i, 0)),
                   pl.BlockSpec((8, w), lambda i: (0, 0)), pl.BlockSpec((8, w), lambda i: (0, 0))],
        out_shape=[jax.ShapeDtypeStruct((t, w), F32), jax.ShapeDtypeStruct((t, w), F32),
                   jax.ShapeDtypeStruct((8, w), F32), jax.ShapeDtypeStruct((8, w), F32)],
        compiler_params=_params(("arbitrary",)),
    )(dx1, o_fox, o_dil, g_fox, g_dil, w_out)


def _ffn_fwd(x1, target, g_ffn, w_gate, w_up, w_down):
    t, d = x1.shape
    f = w_gate.shape[1]
    tt = 256

    def body(x_ref, t_ref, g_ref, wg_ref, wu_ref, wd_ref, a_ref, u_ref, dy_ref, loss_ref):
        xx = x_ref[...]
        r = lax.rsqrt(jnp.mean(xx * xx, axis=-1, keepdims=True) + EPS)
        h = (xx * r * g_ref[...]).astype(BF16)
        a = _dot(h, wg_ref[...])
        u = _dot(h, wu_ref[...])
        a_ref[...] = a.astype(BF16)
        u_ref[...] = u.astype(BF16)
        s = (a / (1.0 + jnp.exp(-a)) * u).astype(BF16)
        y = xx + _dot(s, wd_ref[...])
        e = y - t_ref[...]
        dy_ref[...] = e * (1.0 / d)
        loss_ref[...] = jnp.broadcast_to(0.5 * jnp.sum(e * e) * (1.0 / d), (1, 8, LANES))

    return pl.pallas_call(
        body, name="ffn_fwd", grid=(t // tt,),
        in_specs=[pl.BlockSpec((tt, d), lambda i: (i, 0)), pl.BlockSpec((tt, d), lambda i: (i, 0)),
                  pl.BlockSpec((1, d), lambda i: (0, 0)), pl.BlockSpec(memory_space=pltpu.VMEM),
                  pl.BlockSpec(memory_space=pltpu.VMEM), pl.BlockSpec(memory_space=pltpu.VMEM)],
        out_specs=[pl.BlockSpec((tt, f), lambda i: (i, 0)), pl.BlockSpec((tt, f), lambda i: (i, 0)),
                   pl.BlockSpec((tt, d), lambda i: (i, 0)), pl.BlockSpec((1, 8, LANES), lambda i: (i, 0, 0))],
        out_shape=[jax.ShapeDtypeStruct((t, f), BF16), jax.ShapeDtypeStruct((t, f), BF16),
                   jax.ShapeDtypeStruct((t, d), F32), jax.ShapeDtypeStruct((t // tt, 8, LANES), F32)],
        compiler_params=_params(("arbitrary",)),
    )(x1, target, g_ffn, w_gate, w_up, w_down)


def _ffn_bwd(dy, a, u, x1, g_ffn, w_gate, w_up, w_down):
    t, d = x1.shape
    f = w_gate.shape[1]
    tt = 256

    def body(dy_ref, a_ref, u_ref, x_ref, g_ref, wg_ref, wu_ref, wd_ref,
             dx_ref, s_ref, da_ref, du_ref, h_ref, dg_ref):
        @pl.when(pl.program_id(0) == 0)
        def _():
            dg_ref[...] = jnp.zeros_like(dg_ref)

        dy_ = dy_ref[...]
        ds = _dot(dy_.astype(BF16), wd_ref[...], NT)
        a_ = a_ref[...].astype(F32)
        u_ = u_ref[...].astype(F32)
        sig = 1.0 / (1.0 + jnp.exp(-a_))
        silu = a_ * sig
        s_ref[...] = (silu * u_).astype(BF16)
        da = (ds * u_ * (sig * (1.0 + a_ * (1.0 - sig)))).astype(BF16)
        du = (ds * silu).astype(BF16)
        da_ref[...] = da
        du_ref[...] = du
        dh = _dot(da, wg_ref[...], NT) + _dot(du, wu_ref[...], NT)
        xx = x_ref[...]
        r = lax.rsqrt(jnp.mean(xx * xx, axis=-1, keepdims=True) + EPS)
        xhat = xx * r
        g = g_ref[...]
        h_ref[...] = (xhat * g).astype(BF16)
        uu = dh * g
        dx_ref[...] = dy_ + r * (uu - xhat * jnp.mean(uu * xhat, axis=-1, keepdims=True))
        dg_ref[0:1, :] += jnp.sum(dh * xhat, axis=0, keepdims=True)

    return pl.pallas_call(
        body, name="ffn_bwd", grid=(t // tt,),
        in_specs=[pl.BlockSpec((tt, d), lambda i: (i, 0)), pl.BlockSpec((tt, f), lambda i: (i, 0)),
                  pl.BlockSpec((tt, f), lambda i: (i, 0)), pl.BlockSpec((tt, d), lambda i: (i, 0)),
                  pl.BlockSpec((1, d), lambda i: (0, 0)), pl.BlockSpec(memory_space=pltpu.VMEM),
                  pl.BlockSpec(memory_space=pltpu.VMEM), pl.BlockSpec(memory_space=pltpu.VMEM)],
        out_specs=[pl.BlockSpec((tt, d), lambda i: (i, 0)), pl.BlockSpec((tt, f), lambda i: (i, 0)),
                   pl.BlockSpec((tt, f), lambda i: (i, 0)), pl.BlockSpec((tt, f), lambda i: (i, 0)),
                   pl.BlockSpec((tt, d), lambda i: (i, 0)), pl.BlockSpec((8, d), lambda i: (0, 0))],
        out_shape=[jax.ShapeDtypeStruct((t, d), F32), jax.ShapeDtypeStruct((t, f), BF16),
                   jax.ShapeDtypeStruct((t, f), BF16), jax.ShapeDtypeStruct((t, f), BF16),
                   jax.ShapeDtypeStruct((t, d), BF16), jax.ShapeDtypeStruct((8, d), F32)],
        compiler_params=_params(("arbitrary",)),
    )(dy, a, u, x1, g_ffn, w_gate, w_up, w_down)


def _in_proj_bwd(dparts, dfa_row, w1, wft, x, g_mix, dx1):
    t, d = x.shape
    tt = 512
    npart = len(dparts)

    def body(*refs):
        dp_refs = refs[:npart]
        dfa_ref, w_ref, wf_ref, x_ref, g_ref, dx1_ref, dx_ref, dg_ref = refs[npart:]

        @pl.when(pl.program_id(0) == 0)
        def _():
            dg_ref[...] = jnp.zeros_like(dg_ref)

        dh = _dot(dfa_ref[...].astype(BF16), wf_ref[...], TN)
        for j in range(npart):
            dh = dh + _dot(dp_refs[j][...], w_ref[:, j * W_GROUP:(j + 1) * W_GROUP], NT)
        xx = x_ref[...]
        r = lax.rsqrt(jnp.mean(xx * xx, axis=-1, keepdims=True) + EPS)
        xhat = xx * r
        uu = dh * g_ref[...]
        dx_ref[...] = dx1_ref[...] + r * (uu - xhat * jnp.mean(uu * xhat, axis=-1, keepdims=True))
        dg_ref[0:1, :] += jnp.sum(dh * xhat, axis=0, keepdims=True)

    return pl.pallas_call(
        body, name="in_proj_bwd", grid=(t // tt,),
        in_specs=[pl.BlockSpec((tt, W_GROUP), lambda i: (i, 0)) for _ in range(npart)]
        + [pl.BlockSpec((8, tt), lambda i: (0, i)), pl.BlockSpec(memory_space=pltpu.VMEM),
           pl.BlockSpec(memory_space=pltpu.VMEM), pl.BlockSpec((tt, d), lambda i: (i, 0)),
           pl.BlockSpec((1, d), lambda i: (0, 0)), pl.BlockSpec((tt, d), lambda i: (i, 0))],
        out_specs=[pl.BlockSpec((tt, d), lambda i: (i, 0)), pl.BlockSpec((8, d), lambda i: (0, 0))],
        out_shape=[jax.ShapeDtypeStruct((t, d), F32), jax.ShapeDtypeStruct((8, d), F32)],
        compiler_params=_params(("arbitrary",)),
    )(*dparts, dfa_row, w1, wft, x, g_mix, dx1)


def _tn_matmul(a, b, name):
    t, m = a.shape
    n = b.shape[1]
    tm = 512 if m % 512 == 0 else m
    tn = 512 if n % 512 == 0 else (256 if n % 256 == 0 else n)
    tk = 1024
    nk = t // tk

    def body(a_ref, b_ref, o_ref, acc):
        k = pl.program_id(2)

        @pl.when(k == 0)
        def _():
            acc[...] = jnp.zeros_like(acc)

        acc[...] += _dot(a_ref[...].astype(BF16), b_ref[...].astype(BF16), TN)

        @pl.when(k == nk - 1)
        def _():
            o_ref[...] = acc[...]

    return pl.pallas_call(
        body, name=name, grid=(m // tm, n // tn, nk),
        in_specs=[pl.BlockSpec((tk, tm), lambda i, j, k: (k, i)), pl.BlockSpec((tk, tn), lambda i, j, k: (k, j))],
        out_specs=pl.BlockSpec((tm, tn), lambda i, j, k: (i, j)),
        out_shape=jax.ShapeDtypeStruct((m, n), F32),
        scratch_shapes=[pltpu.VMEM((tm, tn), F32)],
        compiler_params=_params(("arbitrary", "arbitrary", "arbitrary")),
    )(a, b)


def _row_matmul(a_row, b, name):
    t, n = b.shape
    tk = 1024
    nk = t // tk

    def body(a_ref, b_ref, o_ref):
        @pl.when(pl.program_id(0) == 0)
        def _():
            o_ref[...] = jnp.zeros_like(o_ref)

        o_ref[...] += _dot(a_ref[...].astype(BF16), b_ref[...])

    return pl.pallas_call(
        body, name=name, grid=(nk,),
        in_specs=[pl.BlockSpec((8, tk), lambda k: (0, k)), pl.BlockSpec((tk, n), lambda k: (k, 0))],
        out_specs=pl.BlockSpec((8, n), lambda k: (0, 0)),
        out_shape=jax.ShapeDtypeStruct((8, n), F32),
        compiler_params=_params(("arbitrary",)),
    )(a_row, b)


FOX_TQ = 256


def _fox_fwd(proj, c3, gq, gk, nb, seq):
    t = nb * seq
    tq = FOX_TQ
    nq = seq // tq
    npair = N_FOX_HEADS // 2

    def body(q_ref, k_ref, v_ref, c_ref, gq_ref, gk_ref, o_ref, lse_ref, qs, ks, vs):
        ones = _group_ones()
        masks = _head_masks()
        qhat, _ = _head_norm(q_ref[...].astype(F32), None, ones)
        khat, _ = _head_norm(k_ref[...].astype(F32), None, ones)
        qs[...] = (qhat * gq_ref[...] * (SCALE * LOG2E)).astype(BF16)
        kn = khat * gk_ref[...]
        for hd in range(2):
            ks[hd] = (kn * masks[hd]).astype(BF16)
        vs[...] = v_ref[...]
        row = lax.broadcasted_iota(jnp.int32, (tq, tq), 0)
        col = lax.broadcasted_iota(jnp.int32, (tq, tq), 1)
        causal = col <= row

        for qi in range(nq):
            q0 = qi * tq
            q_blk = qs[q0:q0 + tq, :]
            o_tot = jnp.zeros((tq, LANES), F32)
            lse_tot = jnp.zeros((tq, LANES), F32)
            for hd in range(2):
                crow = c_ref[0, hd:hd + 1, 0:q0 + tq] * LOG2E
                c0 = crow[:, q0:q0 + 1]
                s_d = _dot(q_blk, ks[hd, q0:q0 + tq, :], NT) + (c0 - crow[:, q0:q0 + tq])
                s_d = jnp.where(causal, s_d, NEG)
                m = jnp.max(s_d, axis=-1, keepdims=True)
                if qi > 0:
                    s_o = _dot(q_blk, ks[hd, 0:q0, :], NT) + (c0 - crow[:, 0:q0])
                    m = jnp.maximum(m, jnp.max(s_o, axis=-1, keepdims=True))
                p_d = jnp.exp2(s_d - m)
                l = jnp.sum(p_d, axis=-1, keepdims=True)
                acc = _dot(p_d.astype(BF16), vs[q0:q0 + tq, :])
                if qi > 0:
                    p_o = jnp.exp2(s_o - m)
                    l = l + jnp.sum(p_o, axis=-1, keepdims=True)
                    acc = acc + _dot(p_o.astype(BF16), vs[0:q0, :])
                o_tot = o_tot + (acc / l) * masks[hd]
                lse_tot = lse_tot + (m + jnp.log2(l) - c0) * masks[hd]
            o_ref[q0:q0 + tq, :] = o_tot
            lse_ref[q0:q0 + tq, :] = lse_tot

    blk = lambda off: pl.BlockSpec((seq, LANES), lambda b, p: (b, off + p))
    return pl.pallas_call(
        body, name="fox_fwd", grid=(nb, npair),
        in_specs=[blk(0), blk(npair), blk(2 * npair), pl.BlockSpec((1, 2, seq), lambda b, p: (p, 0, b)),
                  pl.BlockSpec((1, LANES), lambda b, p: (0, 0)), pl.BlockSpec((1, LANES), lambda b, p: (0, 0))],
        out_specs=[blk(0), blk(0)],
        out_shape=[jax.ShapeDtypeStruct((t, W_GROUP), F32), jax.ShapeDtypeStruct((t, W_GROUP), F32)],
        scratch_shapes=[pltpu.VMEM((seq, LANES), BF16), pltpu.VMEM((2, seq, LANES), BF16),
                        pltpu.VMEM((seq, LANES), BF16)],
        compiler_params=_params(("arbitrary", "arbitrary")),
    )(proj, proj, proj, c3, gq, gk)


def _fox_bwd(proj, c3, gq, gk, do, o, lse, nb, seq):
    t = nb * seq
    tq = FOX_TQ
    nq = seq // tq
    npair = N_FOX_HEADS // 2

    def body(q_ref, k_ref, v_ref, c_ref, gq_ref, gk_ref, do_ref, o_ref, lse_ref,
             dq_ref, dk_ref, dv_ref, dc_ref, dg_ref, qs, ks, vs, dos, delta, dq_acc, dk_acc, dv_acc, row_sum):
        @pl.when((pl.program_id(0) == 0) & (pl.program_id(1) == 0))
        def _():
            dg_ref[...] = jnp.zeros_like(dg_ref)

        ones = _group_ones()
        masks = _head_masks()
        qhat, rq = _head_norm(q_ref[...].astype(F32), None, ones)
        khat, rk = _head_norm(k_ref[...].astype(F32), None, ones)
        qs[...] = (qhat * gq_ref[...] * (SCALE * LOG2E)).astype(BF16)
        kn = khat * gk_ref[...]
        vv = v_ref[...].astype(F32)
        for hd in range(2):
            ks[hd] = (kn * masks[hd]).astype(BF16)
            vs[hd] = (vv * masks[hd]).astype(BF16)
        dof = do_ref[...]
        dos[...] = dof.astype(BF16)
        delta[...] = _groupsum(dof * o_ref[...], ones)
        dq_acc[...] = jnp.zeros_like(dq_acc)
        dk_acc[...] = jnp.zeros_like(dk_acc)
        dv_acc[...] = jnp.zeros_like(dv_acc)
        row_sum[...] = jnp.zeros_like(row_sum)
        row = lax.broadcasted_iota(jnp.int32, (tq, tq), 0)
        col = lax.broadcasted_iota(jnp.int32, (tq, tq), 1)
        causal = col <= row

        for hd in range(2):
            lane0 = hd * HEAD_DIM
            for kj in range(nq):
                k0 = kj * tq
                k_blk = ks[hd, k0:k0 + tq, :]
                v_blk = vs[hd, k0:k0 + tq, :]
                crow = c_ref[0, hd:hd + 1, k0:k0 + tq] * LOG2E
                ck0 = crow[:, 0:1]
                bias = ck0 - crow

                def rows_step(r0, r1, diag, hd=hd, lane0=lane0, k_blk=k_blk, v_blk=v_blk, bias=bias, ck0=ck0):
                    q_r = qs[r0:r1, :]
                    do_r = dos[r0:r1, :]
                    z = _dot(q_r, k_blk, NT) + bias
                    p = jnp.exp2(z - (lse_ref[r0:r1, lane0:lane0 + 1] + ck0))
                    if diag:
                        p = jnp.where(causal, p, 0.0)
                    dp = _dot(do_r, v_blk, NT)
                    ds = p * (dp - delta[r0:r1, lane0:lane0 + 1])
                    dsb = ds.astype(BF16)
                    dq_acc[r0:r1, :] += _dot(dsb, k_blk)
                    row_sum[r0:r1, :] += jnp.sum(ds, axis=1, keepdims=True) * masks[hd]
                    return _dot(dsb, q_r, TN), _dot(p.astype(BF16), do_r, TN), -jnp.sum(ds, axis=0, keepdims=True)

                dk_j, dv_j, dc_j = rows_step(k0, k0 + tq, True)
                if k0 + tq < seq:
                    dk_o, dv_o, dc_o = rows_step(k0 + tq, seq, False)
                    dk_j, dv_j, dc_j = dk_j + dk_o, dv_j + dv_o, dc_j + dc_o
                dk_acc[k0:k0 + tq, :] += dk_j * masks[hd]
                dv_acc[k0:k0 + tq, :] += dv_j * masks[hd]
                dc_ref[0, hd:hd + 1, k0:k0 + tq] = dc_j

        for qi in range(nq):
            q0 = qi * tq
            sums = row_sum[q0:q0 + tq, :].T
            for hd in range(2):
                dc_ref[0, hd:hd + 1, q0:q0 + tq] += sums[hd * HEAD_DIM:hd * HEAD_DIM + 1, :]

        dq_raw, dgq = _head_norm_bwd(dq_acc[...] * SCALE, qhat, rq, gq_ref[...], ones)
        dk_raw, dgk = _head_norm_bwd(dk_acc[...] * LN2, khat, rk, gk_ref[...], ones)
        dq_ref[...] = dq_raw.astype(BF16)
        dk_ref[...] = dk_raw.astype(BF16)
        dv_ref[...] = dv_acc[...].astype(BF16)
        dg_ref[0:1, :] += dgq
        dg_ref[1:2, :] += dgk

    blk = lambda off: pl.BlockSpec((seq, LANES), lambda b, p: (b, off + p))
    vec = pl.BlockSpec((1, LANES), lambda b, p: (0, 0))
    c_spec = pl.BlockSpec((1, 2, seq), lambda b, p: (p, 0, b))
    return pl.pallas_call(
        body, name="fox_bwd", grid=(nb, npair),
        in_specs=[blk(0), blk(npair), blk(2 * npair), c_spec, vec, vec, blk(0), blk(0), blk(0)],
        out_specs=[blk(0), blk(0), blk(0), c_spec, pl.BlockSpec((8, LANES), lambda b, p: (0, 0))],
        out_shape=[jax.ShapeDtypeStruct((t, W_GROUP), BF16), jax.ShapeDtypeStruct((t, W_GROUP), BF16),
                   jax.ShapeDtypeStruct((t, W_GROUP), BF16), jax.ShapeDtypeStruct((npair, 2, t), F32),
                   jax.ShapeDtypeStruct((8, LANES), F32)],
        scratch_shapes=[pltpu.VMEM((seq, LANES), BF16), pltpu.VMEM((2, seq, LANES), BF16),
                        pltpu.VMEM((2, seq, LANES), BF16), pltpu.VMEM((seq, LANES), BF16),
                        pltpu.VMEM((seq, LANES), F32), pltpu.VMEM((seq, LANES), F32),
                        pltpu.VMEM((seq, LANES), F32), pltpu.VMEM((seq, LANES), F32),
                        pltpu.VMEM((seq, LANES), F32)],
        compiler_params=_params(("arbitrary", "arbitrary")),
    )(proj, proj, proj, c3, gq, gk, do, o, lse)


def _dil_block_start(e, d, seq):
    per_res = seq // (d * BAND)
    if per_res == 1:
        return e, None, None
    shift = per_res.bit_length() - 1
    r = e >> shift
    i = e & (per_res - 1)
    start = i * (BAND * d) + r
    has_prev = i > 0
    prev = jnp.where(has_prev, start - BAND * d, start)
    return start, prev, has_prev


def _rows(start, d):
    return pl.ds(start, BAND) if d == 1 else pl.ds(start, BAND, stride=d)


def _dil_masks(has_prev):
    a = lax.broadcasted_iota(jnp.int32, (BAND, 2 * BAND), 0)
    j = lax.broadcasted_iota(jnp.int32, (BAND, 2 * BAND), 1)
    prev_ok = (j < BAND) & (j >= a) & has_prev
    cur_ok = (j >= BAND) & (j - BAND <= a)
    return prev_ok | cur_ok


def _dil_prep(q_ref, k_ref, gq_ref, gk_ref, cos_ref, up_ref, dn_ref, ones):
    qhat, rq = _head_norm(q_ref[...].astype(F32), None, ones)
    khat, rk = _head_norm(k_ref[...].astype(F32), None, ones)
    cos, up, dn = cos_ref[...], up_ref[...], dn_ref[...]
    qn = _rope(qhat * gq_ref[...], cos, up, dn) * SCALE
    kn = _rope(khat * gk_ref[...], cos, up, dn)
    return qhat, rq, khat, rk, qn, kn


def _dil_fwd(proj, gq, gk, cos, up, dn, nb, seq):
    t = nb * seq
    npair = W_GROUP // LANES
    off = 3 * npair
    nblk = seq // BAND

    def body(q_ref, k_ref, v_ref, gq_ref, gk_ref, cos_ref, up_ref, dn_ref, o_ref, lse_ref,
             qs, ks, vs, m_s, l_s, o_s):
        ones = _group_ones()
        masks = _head_masks()
        _, _, _, _, qn, kn = _dil_prep(q_ref, k_ref, gq_ref, gk_ref, cos_ref, up_ref, dn_ref, ones)
        qs[...] = qn
        ks[...] = kn
        vs[...] = v_ref[...].astype(F32)
        m_s[...] = jnp.full_like(m_s, NEG)
        l_s[...] = jnp.zeros_like(l_s)
        o_s[...] = jnp.zeros_like(o_s)
        a_i = lax.broadcasted_iota(jnp.int32, (BAND, BAND), 0)
        j_i = lax.broadcasted_iota(jnp.int32, (BAND, BAND), 1)
        causal = j_i <= a_i

        for d in DILATIONS:
            def block(e, carry, d=d):
                start, prev, has_prev = _dil_block_start(e, d, seq)
                cur = _rows(start, d)
                qb = qs[cur, :]
                if prev is None:
                    kc = ks[cur, :].astype(BF16)
                    vc = vs[cur, :].astype(BF16)
                    ok = causal
                else:
                    pr = _rows(prev, d)
                    kc = jnp.concatenate([ks[pr, :], ks[cur, :]], axis=0).astype(BF16)
                    vc = jnp.concatenate([vs[pr, :], vs[cur, :]], axis=0).astype(BF16)
                    ok = _dil_masks(has_prev)
                m_b = jnp.zeros((BAND, LANES), F32)
                l_b = jnp.zeros((BAND, LANES), F32)
                o_b = jnp.zeros((BAND, LANES), F32)
                for hd in range(2):
                    s = _dot((qb * masks[hd]).astype(BF16), kc, NT)
                    s = jnp.where(ok, s, NEG)
                    m = jnp.max(s, axis=-1, keepdims=True)
                    p = jnp.exp(s - m)
                    m_b = m_b + m * masks[hd]
                    l_b = l_b + jnp.sum(p, axis=-1, keepdims=True) * masks[hd]
                    o_b = o_b + _dot(p.astype(BF16), vc) * masks[hd]
                m_old = m_s[cur, :]
                m_new = jnp.maximum(m_old, m_b)
                w_old = jnp.exp(m_old - m_new)
                w_b = jnp.exp(m_b - m_new)
                l_s[cur, :] = l_s[cur, :] * w_old + l_b * w_b
                o_s[cur, :] = o_s[cur, :] * w_old + o_b * w_b
                m_s[cur, :] = m_new
                return carry

            lax.fori_loop(0, nblk, block, 0, unroll=2)

        l = l_s[...]
        o_ref[...] = o_s[...] / l
        lse_ref[...] = m_s[...] + jnp.log(l)

    blk = lambda o_: pl.BlockSpec((seq, LANES), lambda b, p: (b, o_ + p))
    vec = pl.BlockSpec((1, LANES), lambda b, p: (0, 0))
    tab = pl.BlockSpec((seq, LANES), lambda b, p: (0, 0))
    return pl.pallas_call(
        body, name="dil_fwd", grid=(nb, npair),
        in_specs=[blk(off), blk(off + npair), blk(off + 2 * npair), vec, vec, tab, tab, tab],
        out_specs=[blk(0), blk(0)],
        out_shape=[jax.ShapeDtypeStruct((t, W_GROUP), F32), jax.ShapeDtypeStruct((t, W_GROUP), F32)],
        scratch_shapes=[pltpu.VMEM((seq, LANES), F32) for _ in range(6)],
        compiler_params=_params(("arbitrary", "arbitrary")),
    )(proj, proj, proj, gq, gk, cos, up, dn)


def _dil_bwd(proj, gq, gk, cos, up, dn, do, o, lse, nb, seq):
    t = nb * seq
    npair = W_GROUP // LANES
    off = 3 * npair
    nblk = seq // BAND

    def body(q_ref, k_ref, v_ref, gq_ref, gk_ref, cos_ref, up_ref, dn_ref, do_ref, o_ref, lse_ref,
             dq_ref, dk_ref, dv_ref, dg_ref, qs, ks, vs, delta, dq_s, dk_s, dv_s):
        @pl.when((pl.program_id(0) == 0) & (pl.program_id(1) == 0))
        def _():
            dg_ref[...] = jnp.zeros_like(dg_ref)

        ones = _group_ones()
        masks = _head_masks()
        qhat, rq, khat, rk, qn, kn = _dil_prep(q_ref, k_ref, gq_ref, gk_ref, cos_ref, up_ref, dn_ref, ones)
        qs[...] = qn
        ks[...] = kn
        vs[...] = v_ref[...].astype(F32)
        delta[...] = _groupsum(do_ref[...] * o_ref[...], ones)
        dq_s[...] = jnp.zeros_like(dq_s)
        dk_s[...] = jnp.zeros_like(dk_s)
        dv_s[...] = jnp.zeros_like(dv_s)
        a_i = lax.broadcasted_iota(jnp.int32, (BAND, BAND), 0)
        j_i = lax.broadcasted_iota(jnp.int32, (BAND, BAND), 1)
        causal = j_i <= a_i

        for d in DILATIONS:
            def block(e, carry, d=d):
                start, prev, has_prev = _dil_block_start(e, d, seq)
                cur = _rows(start, d)
                qb = qs[cur, :]
                dob = do_ref[cur, :]
                lse_b = lse_ref[cur, :]
                delta_b = delta[cur, :]
                if prev is None:
                    kc = ks[cur, :].astype(BF16)
                    vc = vs[cur, :].astype(BF16)
                    ok = causal
                else:
                    pr = _rows(prev, d)
                    kc = jnp.concatenate([ks[pr, :], ks[cur, :]], axis=0).astype(BF16)
                    vc = jnp.concatenate([vs[pr, :], vs[cur, :]], axis=0).astype(BF16)
                    ok = _dil_masks(has_prev)
                dq_b = jnp.zeros((BAND, LANES), F32)
                dk_b = jnp.zeros((kc.shape[0], LANES), F32)
                dv_b = jnp.zeros((kc.shape[0], LANES), F32)
                for hd in range(2):
                    lane0 = hd * HEAD_DIM
                    qm = (qb * masks[hd]).astype(BF16)
                    dom = (dob * masks[hd]).astype(BF16)
                    s = _dot(qm, kc, NT)
                    p = jnp.where(ok, jnp.exp(s - lse_b[:, lane0:lane0 + 1]), 0.0)
                    dp = _dot(dom, vc, NT)
                    ds = (p * (dp - delta_b[:, lane0:lane0 + 1])).astype(BF16)
                    dq_b = dq_b + _dot(ds, kc) * masks[hd]
                    dk_b = dk_b + _dot(ds, qm, TN)
                    dv_b = dv_b + _dot(p.astype(BF16), dom, TN)
                dq_s[cur, :] += dq_b
                if prev is None:
                    dk_s[cur, :] += dk_b
                    dv_s[cur, :] += dv_b
                else:
                    dk_s[pr, :] += dk_b[:BAND]
                    dv_s[pr, :] += dv_b[:BAND]
                    dk_s[cur, :] += dk_b[BAND:]
                    dv_s[cur, :] += dv_b[BAND:]
                return carry

            lax.fori_loop(0, nblk, block, 0, unroll=2)

        cos, up, dn = cos_ref[...], up_ref[...], dn_ref[...]
        dq_raw, dgq = _head_norm_bwd(_rope_bwd(dq_s[...] * SCALE, cos, up, dn), qhat, rq, gq_ref[...], ones)
        dk_raw, dgk = _head_norm_bwd(_rope_bwd(dk_s[...], cos, up, dn), khat, rk, gk_ref[...], ones)
        dq_ref[...] = dq_raw.astype(BF16)
        dk_ref[...] = dk_raw.astype(BF16)
        dv_ref[...] = dv_s[...].astype(BF16)
        dg_ref[0:1, :] += dgq
        dg_ref[1:2, :] += dgk

    blk = lambda o_: pl.BlockSpec((seq, LANES), lambda b, p: (b, o_ + p))
    vec = pl.BlockSpec((1, LANES), lambda b, p: (0, 0))
    tab = pl.BlockSpec((seq, LANES), lambda b, p: (0, 0))
    return pl.pallas_call(
        body, name="dil_bwd", grid=(nb, npair),
        in_specs=[blk(off), blk(off + npair), blk(off + 2 * npair), vec, vec, tab, tab, tab,
                  blk(0), blk(0), blk(0)],
        out_specs=[blk(0), blk(0), blk(0), pl.BlockSpec((8, LANES), lambda b, p: (0, 0))],
        out_shape=[jax.ShapeDtypeStruct((t, W_GROUP), BF16), jax.ShapeDtypeStruct((t, W_GROUP), BF16),
                   jax.ShapeDtypeStruct((t, W_GROUP), BF16), jax.ShapeDtypeStruct((8, LANES), F32)],
        scratch_shapes=[pltpu.VMEM((seq, LANES), F32) for _ in range(7)],
        compiler_params=_params(("arbitrary", "arbitrary")),
    )(proj, proj, proj, gq, gk, cos, up, dn, do, o, lse)


def _adamw(w, g, m, v, name):
    rows, cols = w.shape
    tr = _row_tile(rows) if rows >= 8 else rows
    c1 = 1.0 - ADAM_B1 ** ADAM_STEP
    c2 = 1.0 - ADAM_B2 ** ADAM_STEP

    def body(w_ref, g_ref, m_ref, v_ref, d_ref, nm_ref, nv_ref):
        g_ = g_ref[...]
        nm = ADAM_B1 * m_ref[...] + (1.0 - ADAM_B1) * g_
        nv = ADAM_B2 * v_ref[...] + (1.0 - ADAM_B2) * (g_ * g_)
        nm_ref[...] = nm
        nv_ref[...] = nv
        d_ref[...] = -ADAM_LR * ((nm / c1) / (jnp.sqrt(nv / c2) + ADAM_EPS) + ADAM_WD * w_ref[...])

    spec = pl.BlockSpec((tr, cols), lambda i: (i, 0))
    shape = jax.ShapeDtypeStruct((rows, cols), F32)
    return pl.pallas_call(
        body, name=name, grid=(rows // tr,), in_specs=[spec] * 4, out_specs=[spec] * 3,
        out_shape=[shape] * 3, compiler_params=_params(("arbitrary",)),
    )(w, g, m, v)


def _place():
    x, y, c = lax.axis_index("x"), lax.axis_index("y"), lax.axis_index("c")
    chips = [(1 - x, y), (x, 1 - y), (1 - x, 1 - y)]
    return x, y, c, chips


def _gather_weight(w, name):
    rows, cols = w.shape
    half_rows = rows // 2

    def body(w_ref, out_ref, send_sems, recv_sems):
        x, y, c, chips = _place()
        sibling = (x, y, 1 - c)
        mine = 2 * x + y
        lo = pl.multiple_of(c * half_rows, 16)
        lo_sib = pl.multiple_of((1 - c) * half_rows, 16)
        out_ref[mine] = w_ref[...].astype(BF16)

        def copy(k, shard, first_row, to):
            ref = out_ref.at[shard, pl.ds(first_row, half_rows), :]
            return pltpu.make_async_remote_copy(src_ref=ref, dst_ref=ref, send_sem=send_sems.at[k],
                                                recv_sem=recv_sems.at[k], device_id=to, device_id_type=MESH)

        sends = [copy(k, mine, lo, (cx, cy, c)) for k, (cx, cy) in enumerate(chips)]
        for cp in sends:
            cp.start()
        passed = []
        for k, (cx, cy) in enumerate(chips):
            theirs = 2 * cx + cy
            copy(k, theirs, lo, (cx, cy, c)).wait_recv()
            fw = copy(3 + k, theirs, lo, sibling)
            fw.start()
            passed.append(fw)
        for k, (cx, cy) in enumerate(chips):
            copy(3 + k, 2 * cx + cy, lo_sib, sibling).wait_recv()
        for cp in sends + passed:
            cp.wait_send()

    return pl.pallas_call(
        body, name=name,
        in_specs=[pl.BlockSpec(memory_space=pltpu.VMEM)],
        out_specs=pl.BlockSpec(memory_space=pltpu.VMEM),
        out_shape=jax.ShapeDtypeStruct((4, rows, cols), BF16),
        scratch_shapes=[pltpu.SemaphoreType.DMA((6,)), pltpu.SemaphoreType.DMA((6,))],
        compiler_params=pltpu.CompilerParams(vmem_limit_bytes=VMEM_LIMIT),
    )(w)


def _reduce_scatter_weight(g4, name):
    _, rows, cols = g4.shape
    half_rows = rows // 2

    def body(g_ref, out_ref, sib_buf, stage, landed, send_sems, recv_sems):
        x, y, c, chips = _place()
        sibling = (x, y, 1 - c)
        mine = 2 * x + y
        lo = pl.multiple_of(c * half_rows, 8)
        lo_sib = pl.multiple_of((1 - c) * half_rows, 8)

        def copy(k, src, dst, to):
            return pltpu.make_async_remote_copy(src_ref=src, dst_ref=dst, send_sem=send_sems.at[k],
                                                recv_sem=recv_sems.at[k], device_id=to, device_id_type=MESH)

        swap = copy(0, g_ref.at[:, pl.ds(lo_sib, half_rows), :], sib_buf, sibling)
        swap.start()
        swap.wait_recv()
        sends = []
        for k, (cx, cy) in enumerate(chips):
            theirs = 2 * cx + cy
            stage[k] = (g_ref[theirs, pl.ds(lo, half_rows), :] + sib_buf[theirs]).astype(BF16)
            cp = copy(1 + k, stage.at[k], landed.at[k], (cx, cy, c))
            cp.start()
            sends.append(cp)
        acc = g_ref[mine, pl.ds(lo, half_rows), :] + sib_buf[mine]
        for k, (cx, cy) in enumerate(chips):
            copy(1 + k, stage.at[k], landed.at[k], (cx, cy, c)).wait_recv()
            acc = acc + landed[k].astype(F32)
        out_ref[pl.ds(lo, half_rows), :] = acc
        done = copy(4, out_ref.at[pl.ds(lo, half_rows), :], out_ref.at[pl.ds(lo, half_rows), :], sibling)
        done.start()
        copy(4, out_ref.at[pl.ds(lo_sib, half_rows), :], out_ref.at[pl.ds(lo_sib, half_rows), :], sibling).wait_recv()
        for cp in [swap] + sends + [done]:
            cp.wait_send()

    return pl.pallas_call(
        body, name=name,
        in_specs=[pl.BlockSpec(memory_space=pltpu.VMEM)],
        out_specs=pl.BlockSpec(memory_space=pltpu.VMEM),
        out_shape=jax.ShapeDtypeStruct((rows, cols), F32),
        scratch_shapes=[pltpu.VMEM((4, half_rows, cols), F32), pltpu.VMEM((3, half_rows, cols), BF16),
                        pltpu.VMEM((3, half_rows, cols), BF16),
                        pltpu.SemaphoreType.DMA((5,)), pltpu.SemaphoreType.DMA((5,))],
        compiler_params=pltpu.CompilerParams(vmem_limit_bytes=VMEM_LIMIT),
    )(g4)


def _all_sum_small(v):
    shape = v.shape

    def body(v_ref, out_ref, buf, send_sems, recv_sems):
        x, y, c, _ = _place()
        me = 4 * x + 2 * y + c
        buf[me] = v_ref[...]
        flips = [(dx, dy, dc) for dx in (0, 1) for dy in (0, 1) for dc in (0, 1)][1:]

        def copy(k, slot, flip):
            dx, dy, dc = flip
            to = (1 - x if dx else x, 1 - y if dy else y, 1 - c if dc else c)
            return pltpu.make_async_remote_copy(src_ref=buf.at[slot], dst_ref=buf.at[slot], send_sem=send_sems.at[k],
                                                recv_sem=recv_sems.at[k], device_id=to, device_id_type=MESH)

        sends = [copy(k, me, flip) for k, flip in enumerate(flips)]
        for cp in sends:
            cp.start()
        for k, (dx, dy, dc) in enumerate(flips):
            sender = 4 * (1 - x if dx else x) + 2 * (1 - y if dy else y) + (1 - c if dc else c)
            copy(k, sender, (dx, dy, dc)).wait_recv()
        for cp in sends:
            cp.wait_send()
        total = buf[0]
        for i in range(1, 8):
            total = total + buf[i]
        out_ref[...] = total

    return pl.pallas_call(
        body, name="all_sum_small",
        in_specs=[pl.BlockSpec(memory_space=pltpu.VMEM)],
        out_specs=pl.BlockSpec(memory_space=pltpu.VMEM),
        out_shape=jax.ShapeDtypeStruct(shape, F32),
        scratch_shapes=[pltpu.VMEM((8,) + shape, F32), pltpu.SemaphoreType.DMA((7,)), pltpu.SemaphoreType.DMA((7,))],
    )(v)


SMALL = (("g_mix", 1024), ("g_ffn", 1024), ("g_out_fox", 512), ("g_out_dil", 512), ("g_q_fox", 64),
         ("g_k_fox", 64), ("g_q_dil", 64), ("g_k_dil", 64), ("b_forget", 8))
SMALL_PACKED = (32, LANES)


def _local_grads(x, target, gains, w1, wft, w_out, w_gate, w_up, w_down, nb, seq):
    tile2 = lambda g: jnp.tile(g, (1, 2))
    gq_f, gk_f, gq_d, gk_d = (tile2(gains[n]) for n in ("g_q_fox", "g_k_fox", "g_q_dil", "g_k_dil"))
    b_col = gains["b_forget"].reshape(N_FOX_HEADS, 1)
    cos, up, dn = _rope_tables(seq)
    npair = N_FOX_HEADS // 2

    proj, fa_row, h1 = _in_proj(x, gains["g_mix"], w1, wft)
    c_row = _gate_fwd(fa_row, b_col, seq)
    c3 = c_row.reshape(npair, 2, nb * seq)
    o_fox, lse_fox = _fox_fwd(proj, c3, gq_f, gk_f, nb, seq)
    o_dil, lse_dil = _dil_fwd(proj, gq_d, gk_d, cos, up, dn, nb, seq)
    x1, o_n = _attn_out(o_fox, o_dil, x, gains["g_out_fox"], gains["g_out_dil"], w_out)
    a, u, dy, loss_parts = _ffn_fwd(x1, target, gains["g_ffn"], w_gate, w_up, w_down)
    loss = jnp.sum(loss_parts[:, 0, 0])

    dx1, s, da, du, h2, dg_ffn = _ffn_bwd(dy, a, u, x1, gains["g_ffn"], w_gate, w_up, w_down)
    d_w_down = _tn_matmul(s, dy, "dw_down")
    d_w_gate = _tn_matmul(h2, da, "dw_gate")
    d_w_up = _tn_matmul(h2, du, "dw_up")
    d_w_out = _tn_matmul(o_n, dx1, "dw_out")
    do_fox, do_dil, dg_of, dg_od = _attn_out_bwd(dx1, o_fox, o_dil, gains["g_out_fox"], gains["g_out_dil"], w_out)
    dq_f, dk_f, dv_f, dc3, dg_fox = _fox_bwd(proj, c3, gq_f, gk_f, do_fox, o_fox, lse_fox, nb, seq)
    dq_d, dk_d, dv_d, dg_dil = _dil_bwd(proj, gq_d, gk_d, cos, up, dn, do_dil, o_dil, lse_dil, nb, seq)
    dfa_row, db = _gate_bwd(dc3.reshape(N_FOX_HEADS, nb * seq), fa_row, b_col, seq)
    dparts = [dq_f, dk_f, dv_f, dq_d, dk_d, dv_d]
    grad_x, dg_mix = _in_proj_bwd(dparts, dfa_row, w1, wft, x, gains["g_mix"], dx1)
    d_w1 = jnp.concatenate([_tn_matmul(h1, dp, "dw_in_%d" % j) for j, dp in enumerate(dparts)], axis=1)
    d_wf = _row_matmul(dfa_row, h1, "dw_forget")

    fold = lambda g2: (g2[:, :HEAD_DIM] + g2[:, HEAD_DIM:])
    small = {
        "g_mix": dg_mix[0:1], "g_ffn": dg_ffn[0:1], "g_out_fox": dg_of[0:1], "g_out_dil": dg_od[0:1],
        "g_q_fox": fold(dg_fox[0:1]), "g_k_fox": fold(dg_fox[1:2]),
        "g_q_dil": fold(dg_dil[0:1]), "g_k_dil": fold(dg_dil[1:2]),
        "b_forget": db[:, 0].reshape(1, N_FOX_HEADS),
    }
    big = {"w1": d_w1, "wf": d_wf, "w_out": d_w_out, "w_gate": d_w_gate, "w_up": d_w_up, "w_down": d_w_down}
    return loss, grad_x, big, small


def _shards_of_columns(full, n=4):
    r, nc = full.shape
    return full.reshape(r, n, nc // n).transpose(1, 0, 2)


def _columns_of_shards(slabs):
    n, r, c = slabs.shape
    return slabs.transpose(1, 0, 2).reshape(r, n * c)


def kernel(x, g_mix, w_in, b_forget, g_q_fox, g_k_fox, g_q_dil, g_k_dil, g_out_fox, g_out_dil, w_out, g_ffn, w_gate, w_up, w_down, loss_target, m_g_mix, m_w_in, m_b_forget, m_g_q_fox, m_g_k_fox, m_g_q_dil, m_g_k_dil, m_g_out_fox, m_g_out_dil, m_w_out, m_g_ffn, m_w_gate, m_w_up, m_w_down, v_g_mix, v_w_in, v_b_forget, v_g_q_fox, v_g_k_fox, v_g_q_dil, v_g_k_dil, v_g_out_fox, v_g_out_dil, v_w_out, v_g_ffn, v_w_gate, v_w_up, v_w_down):
    nb, seq, d = x.shape
    weights = dict(g_mix=g_mix, w_in=w_in, b_forget=b_forget, g_q_fox=g_q_fox, g_k_fox=g_k_fox, g_q_dil=g_q_dil,
                   g_k_dil=g_k_dil, g_out_fox=g_out_fox, g_out_dil=g_out_dil, w_out=w_out, g_ffn=g_ffn,
                   w_gate=w_gate, w_up=w_up, w_down=w_down)
    m_in = dict(g_mix=m_g_mix, w_in=m_w_in, b_forget=m_b_forget, g_q_fox=m_g_q_fox, g_k_fox=m_g_k_fox,
                g_q_dil=m_g_q_dil, g_k_dil=m_g_k_dil, g_out_fox=m_g_out_fox, g_out_dil=m_g_out_dil, w_out=m_w_out,
                g_ffn=m_g_ffn, w_gate=m_w_gate, w_up=m_w_up, w_down=m_w_down)
    v_in = dict(g_mix=v_g_mix, w_in=v_w_in, b_forget=v_b_forget, g_q_fox=v_g_q_fox, g_k_fox=v_g_k_fox,
                g_q_dil=v_g_q_dil, g_k_dil=v_g_k_dil, g_out_fox=v_g_out_fox, g_out_dil=v_g_out_dil, w_out=v_w_out,
                g_ffn=v_g_ffn, w_gate=v_w_gate, w_up=v_w_up, w_down=v_w_down)
    order = ["g_mix", "w_in", "b_forget", "g_q_fox", "g_k_fox", "g_q_dil", "g_k_dil", "g_out_fox", "g_out_dil",
             "w_out", "g_ffn", "w_gate", "w_up", "w_down"]

    w_in_full = _columns_of_shards(_gather_weight(w_in[0], "gather_w_in"))
    fox_w = 3 * W_GROUP
    w1 = jnp.concatenate([w_in_full[:, :fox_w], w_in_full[:, fox_w + N_FOX_HEADS:]], axis=1)
    wft = w_in_full[:, fox_w:fox_w + N_FOX_HEADS].T
    w_out_full = _gather_weight(w_out[0], "gather_w_out").reshape(d, d)
    w_gate_full = _columns_of_shards(_gather_weight(w_gate[0], "gather_w_gate"))
    w_up_full = _columns_of_shards(_gather_weight(w_up[0], "gather_w_up"))
    w_down_full = _gather_weight(w_down[0], "gather_w_down").reshape(-1, d)

    gains = {n: weights[n] for n, _ in SMALL}
    loss, grad_x, big, small = _local_grads(
        x.reshape(nb * seq, d), loss_target.reshape(nb * seq, d), gains,
        w1, wft, w_out_full, w_gate_full, w_up_full, w_down_full, nb, seq)
    loss = lax.psum(loss, ("x", "y", "c"))

    d_w_in_full = jnp.concatenate([big["w1"][:, :fox_w], big["wf"].T, big["w1"][:, fox_w:]], axis=1)
    grads = {
        "w_in": _reduce_scatter_weight(_shards_of_columns(d_w_in_full), "reduce_w_in"),
        "w_out": _reduce_scatter_weight(big["w_out"].reshape(4, d // 4, d), "reduce_w_out"),
        "w_gate": _reduce_scatter_weight(_shards_of_columns(big["w_gate"]), "reduce_w_gate"),
        "w_up": _reduce_scatter_weight(_shards_of_columns(big["w_up"]), "reduce_w_up"),
        "w_down": _reduce_scatter_weight(big["w_down"].reshape(4, -1, d), "reduce_w_down"),
    }
    packed = jnp.concatenate([small[n].reshape(-1) for n, _ in SMALL])
    packed = jnp.pad(packed, (0, SMALL_PACKED[0] * SMALL_PACKED[1] - packed.shape[0])).reshape(SMALL_PACKED)
    summed = _all_sum_small(packed).reshape(-1)
    pos = 0
    for n, size in SMALL:
        grads[n] = summed[pos:pos + size].reshape(1, size)
        pos += size

    deltas, new_m, new_v, grad_out = {}, {}, {}, {}
    for n in order:
        w2, m2, v2 = weights[n], m_in[n], v_in[n]
        shape = w2.shape
        flat = (lambda a: a.reshape(shape[-2], shape[-1])) if w2.ndim == 3 else (lambda a: a)
        g2 = grads[n]
        dl, nm, nv = _adamw(flat(w2), g2, flat(m2), flat(v2), "adamw_" + n)
        grad_out[n] = g2.reshape(shape)
        deltas[n], new_m[n], new_v[n] = dl.reshape(shape), nm.reshape(shape), nv.reshape(shape)

    return (loss, grad_x.reshape(nb, seq, d), *[grad_out[n] for n in order], *[deltas[n] for n in order],
            *[new_m[n] for n in order], *[new_v[n] for n in order])
```

```python
import functools
import math

import numpy as np
import jax
import jax.numpy as jnp
from jax import lax
from jax.experimental import pallas as pl
from jax.experimental.pallas import tpu as pltpu

F32, BF16 = jnp.float32, jnp.bfloat16
MESH = pl.DeviceIdType.MESH

EPS = 1e-6
NEG = -1e30
HEAD_DIM = 64
SCALE = HEAD_DIM ** -0.5
LOG2E = math.log2(math.e)
LN2 = math.log(2.0)
ROPE_THETA = 500000.0
ROPE_DIM = HEAD_DIM // 4
LANES = 128
W_GROUP = 512
N_FOX_HEADS = 8
VMEM_LIMIT = 56 * 1024 * 1024
DILATIONS = (1, 4, 16)
BAND = 128

ADAM_LR, ADAM_B1, ADAM_B2, ADAM_EPS, ADAM_WD, ADAM_STEP = 0.001, 0.9, 0.999, 1e-08, 0.01, 10

NT = (((1,), (1,)), ((), ()))
TN = (((0,), (0,)), ((), ()))
BATCH_NT = (((2,), (2,)), ((0,), (0,)))
BATCH_NN = (((2,), (1,)), ((0,), (0,)))
BATCH_TN = (((1,), (1,)), ((0,), (0,)))


def _params(sem=None):
    return pltpu.CompilerParams(dimension_semantics=sem, vmem_limit_bytes=VMEM_LIMIT)


def _dot(a, b, dims=None):
    if dims is None:
        return jnp.dot(a, b, preferred_element_type=F32)
    return lax.dot_general(a, b, dims, preferred_element_type=F32)


def _group_ones():
    i = lax.broadcasted_iota(jnp.int32, (LANES, LANES), 0) >> 6
    j = lax.broadcasted_iota(jnp.int32, (LANES, LANES), 1) >> 6
    return (i == j).astype(BF16)


def _split3(x):
    a = x.astype(BF16)
    r = x - a.astype(F32)
    b = r.astype(BF16)
    c = (r - b.astype(F32)).astype(BF16)
    return a, b, c


def _groupsum(x, ones):
    a, b, c = _split3(x)
    return _dot(a, ones) + _dot(b, ones) + _dot(c, ones)


def _head_masks():
    lane = lax.broadcasted_iota(jnp.int32, (1, LANES), 1)
    return [(lane < HEAD_DIM).astype(F32), (lane >= HEAD_DIM).astype(F32)]


def _head_norm(raw, gain, ones):
    r = lax.rsqrt(_groupsum(raw * raw, ones) * (1.0 / HEAD_DIM) + EPS)
    return raw * r, r


def _head_norm_bwd(dy, xhat, r, gain, ones):
    u = dy * gain
    dgain = jnp.sum(dy * xhat, axis=0, keepdims=True)
    draw = r * (u - xhat * (_groupsum(u * xhat, ones) * (1.0 / HEAD_DIM)))
    return draw, dgain


def _rope(x, cos, s_up, s_dn):
    return x * cos + pltpu.roll(x, LANES - 8, 1) * s_up + pltpu.roll(x, 8, 1) * s_dn


def _rope_bwd(dy, cos, s_up, s_dn):
    return dy * cos + pltpu.roll(dy * s_up, 8, 1) + pltpu.roll(dy * s_dn, LANES - 8, 1)


def _rope_tables(seq):
    half = ROPE_DIM // 2
    inv_freq = jnp.power(jnp.float32(ROPE_THETA), -jnp.arange(half, dtype=F32) * 2.0 / ROPE_DIM)
    ang = jnp.arange(seq).astype(F32)[:, None] * inv_freq[None, :]
    cos, sin = jnp.cos(ang), jnp.sin(ang)
    one = jnp.ones((seq, HEAD_DIM - ROPE_DIM), F32)
    zero_h = jnp.zeros((seq, half), F32)
    zero_r = jnp.zeros((seq, HEAD_DIM - ROPE_DIM), F32)
    c = jnp.concatenate([cos, cos, one], axis=1)
    up = jnp.concatenate([-sin, zero_h, zero_r], axis=1)
    dn = jnp.concatenate([zero_h, sin, zero_r], axis=1)
    return jnp.tile(c, (1, 2)), jnp.tile(up, (1, 2)), jnp.tile(dn, (1, 2))


def _row_tile(rows, cap=256):
    best = rows
    for t in range(8, min(rows, cap) + 1, 8):
        if rows % t == 0:
            best = t
    return best


def _in_proj(x, g_mix, w1, wft):
    t, d = x.shape
    n = w1.shape[1]
    tt = 512

    def body(x_ref, g_ref, w_ref, wf_ref, p_ref, fa_ref, h_ref):
        xx = x_ref[...]
        r = lax.rsqrt(jnp.mean(xx * xx, axis=-1, keepdims=True) + EPS)
        h = (xx * r * g_ref[...]).astype(BF16)
        h_ref[...] = h
        for j in range(n // W_GROUP):
            cols = slice(j * W_GROUP, (j + 1) * W_GROUP)
            p_ref[:, cols] = _dot(h, w_ref[:, cols]).astype(BF16)
        fa_ref[...] = _dot(wf_ref[...], h, NT)

    return pl.pallas_call(
        body, name="in_proj", grid=(t // tt,),
        in_specs=[pl.BlockSpec((tt, d), lambda i: (i, 0)), pl.BlockSpec((1, d), lambda i: (0, 0)),
                  pl.BlockSpec(memory_space=pltpu.VMEM), pl.BlockSpec(memory_space=pltpu.VMEM)],
        out_specs=[pl.BlockSpec((tt, n), lambda i: (i, 0)), pl.BlockSpec((8, tt), lambda i: (0, i)),
                   pl.BlockSpec((tt, d), lambda i: (i, 0))],
        out_shape=[jax.ShapeDtypeStruct((t, n), BF16), jax.ShapeDtypeStruct((8, t), F32),
                   jax.ShapeDtypeStruct((t, d), BF16)],
        compiler_params=_params(("arbitrary",)),
    )(x, g_mix, w1, wft)


def _tri(n, upper):
    i = lax.broadcasted_iota(jnp.int32, (n, n), 0)
    j = lax.broadcasted_iota(jnp.int32, (n, n), 1)
    return ((i <= j) if upper else (i >= j)).astype(BF16)


def _gate_fwd(fa_row, b_col, seq):
    t = fa_row.shape[1]
    cb = 256

    def body(fa_ref, b_ref, c_ref):
        tri = _tri(cb, True)
        carry = jnp.zeros((8, 1), F32)
        for k in range(seq // cb):
            z = fa_ref[:, k * cb:(k + 1) * cb] + b_ref[...]
            lf = jnp.minimum(z, 0.0) - jnp.log(1.0 + jnp.exp(-jnp.abs(z)))
            a, b, c = _split3(lf)
            blk = _dot(a, tri) + _dot(b, tri) + _dot(c, tri) + carry
            c_ref[:, k * cb:(k + 1) * cb] = blk
            carry = blk[:, cb - 1:cb]

    return pl.pallas_call(
        body, name="gate_fwd", grid=(t // seq,),
        in_specs=[pl.BlockSpec((8, seq), lambda i: (0, i)), pl.BlockSpec((8, 1), lambda i: (0, 0))],
        out_specs=pl.BlockSpec((8, seq), lambda i: (0, i)),
        out_shape=jax.ShapeDtypeStruct((8, t), F32),
        compiler_params=_params(("arbitrary",)),
    )(fa_row, b_col)


def _gate_bwd(dc_row, fa_row, b_col, seq):
    t = fa_row.shape[1]
    cb = 256

    def body(dc_ref, fa_ref, b_ref, dfa_ref, db_ref):
        @pl.when(pl.program_id(0) == 0)
        def _():
            db_ref[...] = jnp.zeros_like(db_ref)

        tri = _tri(cb, False)
        carry = jnp.zeros((8, 1), F32)
        dbs = jnp.zeros((8, 1), F32)
        for k in reversed(range(seq // cb)):
            a, b, c = _split3(dc_ref[:, k * cb:(k + 1) * cb])
            dlf = _dot(a, tri) + _dot(b, tri) + _dot(c, tri) + carry
            carry = dlf[:, 0:1]
            z = fa_ref[:, k * cb:(k + 1) * cb] + b_ref[...]
            dfa = dlf / (1.0 + jnp.exp(z))
            dfa_ref[:, k * cb:(k + 1) * cb] = dfa
            dbs = dbs + jnp.sum(dfa, axis=1, keepdims=True)
        db_ref[...] += jnp.broadcast_to(dbs, (8, LANES))

    return pl.pallas_call(
        body, name="gate_bwd", grid=(t // seq,),
        in_specs=[pl.BlockSpec((8, seq), lambda i: (0, i)), pl.BlockSpec((8, seq), lambda i: (0, i)),
                  pl.BlockSpec((8, 1), lambda i: (0, 0))],
        out_specs=[pl.BlockSpec((8, seq), lambda i: (0, i)), pl.BlockSpec((8, LANES), lambda i: (0, 0))],
        out_shape=[jax.ShapeDtypeStruct((8, t), F32), jax.ShapeDtypeStruct((8, LANES), F32)],
        compiler_params=_params(("arbitrary",)),
    )(dc_row, fa_row, b_col)


def _attn_out(o_fox, o_dil, x, g_fox, g_dil, w_out):
    t, d = x.shape
    w = o_fox.shape[1]
    tt = 512

    def body(of_ref, od_ref, x_ref, gf_ref, gd_ref, w_ref, x1_ref, on_ref):
        acc = x_ref[...]
        for k, (o_ref, g_ref) in enumerate(((of_ref, gf_ref), (od_ref, gd_ref))):
            o = o_ref[...]
            r = lax.rsqrt(jnp.mean(o * o, axis=-1, keepdims=True) + EPS)
            on = (o * r * g_ref[...]).astype(BF16)
            on_ref[:, k * w:(k + 1) * w] = on
            acc = acc + _dot(on, w_ref[k * w:(k + 1) * w, :])
        x1_ref[...] = acc

    return pl.pallas_call(
        body, name="attn_out", grid=(t // tt,),
        in_specs=[pl.BlockSpec((tt, w), lambda i: (i, 0)), pl.BlockSpec((tt, w), lambda i: (i, 0)),
                  pl.BlockSpec((tt, d), lambda i: (i, 0)), pl.BlockSpec((1, w), lambda i: (0, 0)),
                  pl.BlockSpec((1, w), lambda i: (0, 0)), pl.BlockSpec(memory_space=pltpu.VMEM)],
        out_specs=[pl.BlockSpec((tt, d), lambda i: (i, 0)), pl.BlockSpec((tt, 2 * w), lambda i: (i, 0))],
        out_shape=[jax.ShapeDtypeStruct((t, d), F32), jax.ShapeDtypeStruct((t, 2 * w), BF16)],
        compiler_params=_params(("arbitrary",)),
    )(o_fox, o_dil, x, g_fox, g_dil, w_out)


def _attn_out_bwd(dx1, o_fox, o_dil, g_fox, g_dil, w_out):
    t, d = dx1.shape
    w = o_fox.shape[1]
    tt = 512

    def body(dx_ref, of_ref, od_ref, gf_ref, gd_ref, w_ref, dof_ref, dod_ref, dgf_ref, dgd_ref):
        @pl.when(pl.program_id(0) == 0)
        def _():
            dgf_ref[...] = jnp.zeros_like(dgf_ref)
            dgd_ref[...] = jnp.zeros_like(dgd_ref)

        dxb = dx_ref[...].astype(BF16)
        for k, (o_ref, g_ref, do_ref, dg_ref) in enumerate(
                ((of_ref, gf_ref, dof_ref, dgf_ref), (od_ref, gd_ref, dod_ref, dgd_ref))):
            don = _dot(dxb, w_ref[k * w:(k + 1) * w, :], NT)
            o = o_ref[...]
            r = lax.rsqrt(jnp.mean(o * o, axis=-1, keepdims=True) + EPS)
            xhat = o * r
            u = don * g_ref[...]
            do_ref[...] = r * (u - xhat * jnp.mean(u * xhat, axis=-1, keepdims=True))
            dg_ref[0:1, :] += jnp.sum(don * xhat, axis=0, keepdims=True)

    return pl.pallas_call(
        body, name="attn_out_bwd", grid=(t // tt,),
        in_specs=[pl.BlockSpec((tt, d), lambda i: (i, 0)), pl.BlockSpec((tt, w), lambda i: (i, 0)),
                  pl.BlockSpec((tt, w), lambda i: (i, 0)), pl.BlockSpec((1, w), lambda i: (0, 0)),
                  pl.BlockSpec((1, w), lambda i: (0, 0)), pl.BlockSpec(memory_space=pltpu.VMEM)],
        out_specs=[pl.BlockSpec((tt, w), lambda i: (i, 0)), pl.BlockSpec((tt, w), lambda i: (i, 0)),
                   pl.BlockSpec((8, w), lambda i: (0, 0)), pl.BlockSpec((8, w), lambda i: (0, 0))],
        out_shape=[jax.ShapeDtypeStruct((t, w), F32), jax.ShapeDtypeStruct((t, w), F32),
                   jax.ShapeDtypeStruct((8, w), F32), jax.ShapeDtypeStruct((8, w), F32)],
        compiler_params=_params(("arbitrary",)),
    )(dx1, o_fox, o_dil, g_fox, g_dil, w_out)


def _ffn_fwd(x1, target, g_ffn, w_gate, w_up, w_down):
    t, d = x1.shape
    f = w_gate.shape[1]
    tt = 256

    def body(x_ref, t_ref, g_ref, wg_ref, wu_ref, wd_ref, a_ref, u_ref, dy_ref, loss_ref):
        xx = x_ref[...]
        r = lax.rsqrt(jnp.mean(xx * xx, axis=-1, keepdims=True) + EPS)
        h = (xx * r * g_ref[...]).astype(BF16)
        a = _dot(h, wg_ref[...])
        u = _dot(h, wu_ref[...])
        a_ref[...] = a.astype(BF16)
        u_ref[...] = u.astype(BF16)
        s = (a / (1.0 + jnp.exp(-a)) * u).astype(BF16)
        y = xx + _dot(s, wd_ref[...])
        e = y - t_ref[...]
        dy_ref[...] = e * (1.0 / d)
        loss_ref[...] = jnp.broadcast_to(0.5 * jnp.sum(e * e) * (1.0 / d), (1, 8, LANES))

    return pl.pallas_call(
        body, name="ffn_fwd", grid=(t // tt,),
        in_specs=[pl.BlockSpec((tt, d), lambda i: (i, 0)), pl.BlockSpec((tt, d), lambda i: (i, 0)),
                  pl.BlockSpec((1, d), lambda i: (0, 0)), pl.BlockSpec(memory_space=pltpu.VMEM),
                  pl.BlockSpec(memory_space=pltpu.VMEM), pl.BlockSpec(memory_space=pltpu.VMEM)],
        out_specs=[pl.BlockSpec((tt, f), lambda i: (i, 0)), pl.BlockSpec((tt, f), lambda i: (i, 0)),
                   pl.BlockSpec((tt, d), lambda i: (i, 0)), pl.BlockSpec((1, 8, LANES), lambda i: (i, 0, 0))],
        out_shape=[jax.ShapeDtypeStruct((t, f), BF16), jax.ShapeDtypeStruct((t, f), BF16),
                   jax.ShapeDtypeStruct((t, d), F32), jax.ShapeDtypeStruct((t // tt, 8, LANES), F32)],
        compiler_params=_params(("arbitrary",)),
    )(x1, target, g_ffn, w_gate, w_up, w_down)


def _ffn_bwd(dy, a, u, x1, g_ffn, w_gate, w_up, w_down):
    t, d = x1.shape
    f = w_gate.shape[1]
    tt = 256

    def body(dy_ref, a_ref, u_ref, x_ref, g_ref, wg_ref, wu_ref, wd_ref,
             dx_ref, s_ref, da_ref, du_ref, h_ref, dg_ref):
        @pl.when(pl.program_id(0) == 0)
        def _():
            dg_ref[...] = jnp.zeros_like(dg_ref)

        dy_ = dy_ref[...]
        ds = _dot(dy_.astype(BF16), wd_ref[...], NT)
        a_ = a_ref[...].astype(F32)
        u_ = u_ref[...].astype(F32)
        sig = 1.0 / (1.0 + jnp.exp(-a_))
        silu = a_ * sig
        s_ref[...] = (silu * u_).astype(BF16)
        da = (ds * u_ * (sig * (1.0 + a_ * (1.0 - sig)))).astype(BF16)
        du = (ds * silu).astype(BF16)
        da_ref[...] = da
        du_ref[...] = du
        dh = _dot(da, wg_ref[...], NT) + _dot(du, wu_ref[...], NT)
        xx = x_ref[...]
        r = lax.rsqrt(jnp.mean(xx * xx, axis=-1, keepdims=True) + EPS)
        xhat = xx * r
        g = g_ref[...]
        h_ref[...] = (xhat * g).astype(BF16)
        uu = dh * g
        dx_ref[...] = dy_ + r * (uu - xhat * jnp.mean(uu * xhat, axis=-1, keepdims=True))
        dg_ref[0:1, :] += jnp.sum(dh * xhat, axis=0, keepdims=True)

    return pl.pallas_call(
        body, name="ffn_bwd", grid=(t // tt,),
        in_specs=[pl.BlockSpec((tt, d), lambda i: (i, 0)), pl.BlockSpec((tt, f), lambda i: (i, 0)),
                  pl.BlockSpec((tt, f), lambda i: (i, 0)), pl.BlockSpec((tt, d), lambda i: (i, 0)),
                  pl.BlockSpec((1, d), lambda i: (0, 0)), pl.BlockSpec(memory_space=pltpu.VMEM),
                  pl.BlockSpec(memory_space=pltpu.VMEM), pl.BlockSpec(memory_space=pltpu.VMEM)],
        out_specs=[pl.BlockSpec((tt, d), lambda i: (i, 0)), pl.BlockSpec((tt, f), lambda i: (i, 0)),
                   pl.BlockSpec((tt, f), lambda i: (i, 0)), pl.BlockSpec((tt, f), lambda i: (i, 0)),
                   pl.BlockSpec((tt, d), lambda i: (i, 0)), pl.BlockSpec((8, d), lambda i: (0, 0))],
        out_shape=[jax.ShapeDtypeStruct((t, d), F32), jax.ShapeDtypeStruct((t, f), BF16),
                   jax.ShapeDtypeStruct((t, f), BF16), jax.ShapeDtypeStruct((t, f), BF16),
                   jax.ShapeDtypeStruct((t, d), BF16), jax.ShapeDtypeStruct((8, d), F32)],
        compiler_params=_params(("arbitrary",)),
    )(dy, a, u, x1, g_ffn, w_gate, w_up, w_down)


def _in_proj_bwd(dparts, dfa_row, w1, wft, x, g_mix, dx1):
    t, d = x.shape
    tt = 512
    npart = len(dparts)

    def body(*refs):
        dp_refs = refs[:npart]
        dfa_ref, w_ref, wf_ref, x_ref, g_ref, dx1_ref, dx_ref, dg_ref = refs[npart:]

        @pl.when(pl.program_id(0) == 0)
        def _():
            dg_ref[...] = jnp.zeros_like(dg_ref)

        dh = _dot(dfa_ref[...].astype(BF16), wf_ref[...], TN)
        for j in range(npart):
            dh = dh + _dot(dp_refs[j][...], w_ref[:, j * W_GROUP:(j + 1) * W_GROUP], NT)
        xx = x_ref[...]
        r = lax.rsqrt(jnp.mean(xx * xx, axis=-1, keepdims=True) + EPS)
        xhat = xx * r
        uu = dh * g_ref[...]
        dx_ref[...] = dx1_ref[...] + r * (uu - xhat * jnp.mean(uu * xhat, axis=-1, keepdims=True))
        dg_ref[0:1, :] += jnp.sum(dh * xhat, axis=0, keepdims=True)

    return pl.pallas_call(
        body, name="in_proj_bwd", grid=(t // tt,),
        in_specs=[pl.BlockSpec((tt, W_GROUP), lambda i: (i, 0)) for _ in range(npart)]
        + [pl.BlockSpec((8, tt), lambda i: (0, i)), pl.BlockSpec(memory_space=pltpu.VMEM),
           pl.BlockSpec(memory_space=pltpu.VMEM), pl.BlockSpec((tt, d), lambda i: (i, 0)),
           pl.BlockSpec((1, d), lambda i: (0, 0)), pl.BlockSpec((tt, d), lambda i: (i, 0))],
        out_specs=[pl.BlockSpec((tt, d), lambda i: (i, 0)), pl.BlockSpec((8, d), lambda i: (0, 0))],
        out_shape=[jax.ShapeDtypeStruct((t, d), F32), jax.ShapeDtypeStruct((8, d), F32)],
        compiler_params=_params(("arbitrary",)),
    )(*dparts, dfa_row, w1, wft, x, g_mix, dx1)


def _tn_matmul(a, b, name):
    t, m = a.shape
    n = b.shape[1]
    tm = 512 if m % 512 == 0 else m
    tn = 512 if n % 512 == 0 else (256 if n % 256 == 0 else n)
    tk = 1024
    nk = t // tk

    def body(a_ref, b_ref, o_ref, acc):
        k = pl.program_id(2)

        @pl.when(k == 0)
        def _():
            acc[...] = jnp.zeros_like(acc)

        acc[...] += _dot(a_ref[...].astype(BF16), b_ref[...].astype(BF16), TN)

        @pl.when(k == nk - 1)
        def _():
            o_ref[...] = acc[...]

    return pl.pallas_call(
        body, name=name, grid=(m // tm, n // tn, nk),
        in_specs=[pl.BlockSpec((tk, tm), lambda i, j, k: (k, i)), pl.BlockSpec((tk, tn), lambda i, j, k: (k, j))],
        out_specs=pl.BlockSpec((tm, tn), lambda i, j, k: (i, j)),
        out_shape=jax.ShapeDtypeStruct((m, n), F32),
        scratch_shapes=[pltpu.VMEM((tm, tn), F32)],
        compiler_params=_params(("arbitrary", "arbitrary", "arbitrary")),
    )(a, b)


def _row_matmul(a_row, b, name):
    t, n = b.shape
    tk = 1024
    nk = t // tk

    def body(a_ref, b_ref, o_ref):
        @pl.when(pl.program_id(0) == 0)
        def _():
            o_ref[...] = jnp.zeros_like(o_ref)

        o_ref[...] += _dot(a_ref[...].astype(BF16), b_ref[...])

    return pl.pallas_call(
        body, name=name, grid=(nk,),
        in_specs=[pl.BlockSpec((8, tk), lambda k: (0, k)), pl.BlockSpec((tk, n), lambda k: (k, 0))],
        out_specs=pl.BlockSpec((8, n), lambda k: (0, 0)),
        out_shape=jax.ShapeDtypeStruct((8, n), F32),
        compiler_params=_params(("arbitrary",)),
    )(a_row, b)


FOX_TQ = 256


def _fox_fwd(proj, c3, gq, gk, nb, seq):
    t = nb * seq
    tq = FOX_TQ
    nq = seq // tq
    npair = N_FOX_HEADS // 2

    def body(q_ref, k_ref, v_ref, c_ref, gq_ref, gk_ref, o_ref, lse_ref, qs, ks, vs):
        ones = _group_ones()
        masks = _head_masks()
        qhat, _ = _head_norm(q_ref[...].astype(F32), None, ones)
        khat, _ = _head_norm(k_ref[...].astype(F32), None, ones)
        qs[...] = (qhat * gq_ref[...] * (SCALE * LOG2E)).astype(BF16)
        kn = khat * gk_ref[...]
        for hd in range(2):
            ks[hd] = (kn * masks[hd]).astype(BF16)
        vs[...] = v_ref[...]
        row = lax.broadcasted_iota(jnp.int32, (tq, tq), 0)
        col = lax.broadcasted_iota(jnp.int32, (tq, tq), 1)
        causal = col <= row

        for qi in range(nq):
            q0 = qi * tq
            q_blk = qs[q0:q0 + tq, :]
            o_tot = jnp.zeros((tq, LANES), F32)
            lse_tot = jnp.zeros((tq, LANES), F32)
            for hd in range(2):
                crow = c_ref[0, hd:hd + 1, 0:q0 + tq] * LOG2E
                c0 = crow[:, q0:q0 + 1]
                s_d = _dot(q_blk, ks[hd, q0:q0 + tq, :], NT) + (c0 - crow[:, q0:q0 + tq])
                s_d = jnp.where(causal, s_d, NEG)
                m = jnp.max(s_d, axis=-1, keepdims=True)
                if qi > 0:
                    s_o = _dot(q_blk, ks[hd, 0:q0, :], NT) + (c0 - crow[:, 0:q0])
                    m = jnp.maximum(m, jnp.max(s_o, axis=-1, keepdims=True))
                p_d = jnp.exp2(s_d - m)
                l = jnp.sum(p_d, axis=-1, keepdims=True)
                acc = _dot(p_d.astype(BF16), vs[q0:q0 + tq, :])
                if qi > 0:
                    p_o = jnp.exp2(s_o - m)
                    l = l + jnp.sum(p_o, axis=-1, keepdims=True)
                    acc = acc + _dot(p_o.astype(BF16), vs[0:q0, :])
                o_tot = o_tot + (acc / l) * masks[hd]
                lse_tot = lse_tot + (m + jnp.log2(l) - c0) * masks[hd]
            o_ref[q0:q0 + tq, :] = o_tot
            lse_ref[q0:q0 + tq, :] = lse_tot

    blk = lambda off: pl.BlockSpec((seq, LANES), lambda b, p: (b, off + p))
    return pl.pallas_call(
        body, name="fox_fwd", grid=(nb, npair),
        in_specs=[blk(0), blk(npair), blk(2 * npair), pl.BlockSpec((1, 2, seq), lambda b, p: (p, 0, b)),
                  pl.BlockSpec((1, LANES), lambda b, p: (0, 0)), pl.BlockSpec((1, LANES), lambda b, p: (0, 0))],
        out_specs=[blk(0), blk(0)],
        out_shape=[jax.ShapeDtypeStruct((t, W_GROUP), F32), jax.ShapeDtypeStruct((t, W_GROUP), F32)],
        scratch_shapes=[pltpu.VMEM((seq, LANES), BF16), pltpu.VMEM((2, seq, LANES), BF16),
                        pltpu.VMEM((seq, LANES), BF16)],
        compiler_params=_params(("arbitrary", "arbitrary")),
    )(proj, proj, proj, c3, gq, gk)


def _fox_bwd(proj, c3, gq, gk, do, o, lse, nb, seq):
    t = nb * seq
    tq = FOX_TQ
    nq = seq // tq
    npair = N_FOX_HEADS // 2

    def body(q_ref, k_ref, v_ref, c_ref, gq_ref, gk_ref, do_ref, o_ref, lse_ref,
             dq_ref, dk_ref, dv_ref, dc_ref, dg_ref, qs, ks, vs, dos, delta, dq_acc, dk_acc, dv_acc, row_sum):
        @pl.when((pl.program_id(0) == 0) & (pl.program_id(1) == 0))
        def _():
            dg_ref[...] = jnp.zeros_like(dg_ref)

        ones = _group_ones()
        masks = _head_masks()
        qhat, rq = _head_norm(q_ref[...].astype(F32), None, ones)
        khat, rk = _head_norm(k_ref[...].astype(F32), None, ones)
        qs[...] = (qhat * gq_ref[...] * (SCALE * LOG2E)).astype(BF16)
        kn = khat * gk_ref[...]
        vv = v_ref[...].astype(F32)
        for hd in range(2):
            ks[hd] = (kn * masks[hd]).astype(BF16)
            vs[hd] = (vv * masks[hd]).astype(BF16)
        dof = do_ref[...]
        dos[...] = dof.astype(BF16)
        delta[...] = _groupsum(dof * o_ref[...], ones)
        dq_acc[...] = jnp.zeros_like(dq_acc)
        dk_acc[...] = jnp.zeros_like(dk_acc)
        dv_acc[...] = jnp.zeros_like(dv_acc)
        row_sum[...] = jnp.zeros_like(row_sum)
        row = lax.broadcasted_iota(jnp.int32, (tq, tq), 0)
        col = lax.broadcasted_iota(jnp.int32, (tq, tq), 1)
        causal = col <= row

        for hd in range(2):
            lane0 = hd * HEAD_DIM
            for kj in range(nq):
                k0 = kj * tq
                k_blk = ks[hd, k0:k0 + tq, :]
                v_blk = vs[hd, k0:k0 + tq, :]
                crow = c_ref[0, hd:hd + 1, k0:k0 + tq] * LOG2E
                ck0 = crow[:, 0:1]
                bias = ck0 - crow

                def rows_step(r0, r1, diag, hd=hd, lane0=lane0, k_blk=k_blk, v_blk=v_blk, bias=bias, ck0=ck0):
                    q_r = qs[r0:r1, :]
                    do_r = dos[r0:r1, :]
                    z = _dot(q_r, k_blk, NT) + bias
                    p = jnp.exp2(z - (lse_ref[r0:r1, lane0:lane0 + 1] + ck0))
                    if diag:
                        p = jnp.where(causal, p, 0.0)
                    dp = _dot(do_r, v_blk, NT)
                    ds = p * (dp - delta[r0:r1, lane0:lane0 + 1])
                    dsb = ds.astype(BF16)
                    dq_acc[r0:r1, :] += _dot(dsb, k_blk)
                    row_sum[r0:r1, :] += jnp.sum(ds, axis=1, keepdims=True) * masks[hd]
                    return _dot(dsb, q_r, TN), _dot(p.astype(BF16), do_r, TN), -jnp.sum(ds, axis=0, keepdims=True)

                dk_j, dv_j, dc_j = rows_step(k0, k0 + tq, True)
                if k0 + tq < seq:
                    dk_o, dv_o, dc_o = rows_step(k0 + tq, seq, False)
                    dk_j, dv_j, dc_j = dk_j + dk_o, dv_j + dv_o, dc_j + dc_o
                dk_acc[k0:k0 + tq, :] += dk_j * masks[hd]
                dv_acc[k0:k0 + tq, :] += dv_j * masks[hd]
                dc_ref[0, hd:hd + 1, k0:k0 + tq] = dc_j

        for qi in range(nq):
            q0 = qi * tq
            sums = row_sum[q0:q0 + tq, :].T
            for hd in range(2):
                dc_ref[0, hd:hd + 1, q0:q0 + tq] += sums[hd * HEAD_DIM:hd * HEAD_DIM + 1, :]

        dq_raw, dgq = _head_norm_bwd(dq_acc[...] * SCALE, qhat, rq, gq_ref[...], ones)
        dk_raw, dgk = _head_norm_bwd(dk_acc[...] * LN2, khat, rk, gk_ref[...], ones)
        dq_ref[...] = dq_raw.astype(BF16)
        dk_ref[...] = dk_raw.astype(BF16)
        dv_ref[...] = dv_acc[...].astype(BF16)
        dg_ref[0:1, :] += dgq
        dg_ref[1:2, :] += dgk

    blk = lambda off: pl.BlockSpec((seq, LANES), lambda b, p: (b, off + p))
    vec = pl.BlockSpec((1, LANES), lambda b, p: (0, 0))
    c_spec = pl.BlockSpec((1, 2, seq), lambda b, p: (p, 0, b))
    return pl.pallas_call(
        body, name="fox_bwd", grid=(nb, npair),
        in_specs=[blk(0), blk(npair), blk(2 * npair), c_spec, vec, vec, blk(0), blk(0), blk(0)],
        out_specs=[blk(0), blk(0), blk(0), c_spec, pl.BlockSpec((8, LANES), lambda b, p: (0, 0))],
        out_shape=[jax.ShapeDtypeStruct((t, W_GROUP), BF16), jax.ShapeDtypeStruct((t, W_GROUP), BF16),
                   jax.ShapeDtypeStruct((t, W_GROUP), BF16), jax.ShapeDtypeStruct((npair, 2, t), F32),
                   jax.ShapeDtypeStruct((8, LANES), F32)],
        scratch_shapes=[pltpu.VMEM((seq, LANES), BF16), pltpu.VMEM((2, seq, LANES), BF16),
                        pltpu.VMEM((2, seq, LANES), BF16), pltpu.VMEM((seq, LANES), BF16),
                        pltpu.VMEM((seq, LANES), F32), pltpu.VMEM((seq, LANES), F32),
                        pltpu.VMEM((seq, LANES), F32), pltpu.VMEM((seq, LANES), F32),
                        pltpu.VMEM((seq, LANES), F32)],
        compiler_params=_params(("arbitrary", "arbitrary")),
    )(proj, proj, proj, c3, gq, gk, do, o, lse)


def _dil_prep(q_ref, k_ref, gq_ref, gk_ref, cos_ref, up_ref, dn_ref, ones):
    qhat, rq = _head_norm(q_ref[...].astype(F32), None, ones)
    khat, rk = _head_norm(k_ref[...].astype(F32), None, ones)
    cos, up, dn = cos_ref[...], up_ref[...], dn_ref[...]
    qn = _rope(qhat * gq_ref[...], cos, up, dn) * (SCALE * LOG2E)
    kn = _rope(khat * gk_ref[...], cos, up, dn)
    return qhat, rq, khat, rk, qn, kn


def _dil_keys(d, seq, kp, vp, kw, vw):
    nblk = seq // BAND
    per_res = seq // (d * BAND)
    as_blocks = lambda ref, rows: ref[rows, :].reshape(-1, BAND, LANES)
    if per_res == 1:
        a = lax.broadcasted_iota(jnp.int32, (1, BAND, BAND), 1)
        j = lax.broadcasted_iota(jnp.int32, (1, BAND, BAND), 2)
        causal = jnp.where(j <= a, 0.0, NEG)
        return as_blocks(kp, slice(0, seq)), as_blocks(vp, slice(0, seq)), [causal]
    for src, dst in ((kp, kw), (vp, vw)):
        dst[:, BAND:, :] = as_blocks(src, slice(0, seq))
        dst[1:, :BAND, :] = as_blocks(src, slice(0, seq - BAND))
        dst[0:1, :BAND, :] = jnp.zeros((1, BAND, LANES), BF16)
    a = lax.broadcasted_iota(jnp.int32, (1, BAND, 2 * BAND), 1)
    j = lax.broadcasted_iota(jnp.int32, (1, BAND, 2 * BAND), 2)
    band = jnp.where(((j < BAND) & (j >= a)) | ((j >= BAND) & (j - BAND <= a)), 0.0, NEG)
    e = lax.broadcasted_iota(jnp.int32, (nblk, 1, 2 * BAND), 0)
    j = lax.broadcasted_iota(jnp.int32, (nblk, 1, 2 * BAND), 2)
    no_prev = jnp.where(((e & (per_res - 1)) == 0) & (j < BAND), NEG, 0.0)
    return kw[...], vw[...], [band, no_prev]


def _residues(d, seq):
    n = seq // d
    if d == 1:
        return [(slice(0, seq), slice(0, seq))]
    return [(pl.ds(r, n, stride=d), slice(r * n, (r + 1) * n)) for r in range(d)]


def _dil_fwd(proj, gq, gk, cos, up, dn, nb, seq):
    t = nb * seq
    npair = W_GROUP // LANES
    off = 3 * npair

    def body(q_ref, k_ref, v_ref, gq_ref, gk_ref, cos_ref, up_ref, dn_ref, o_ref, lse_ref,
             qs, ks, vs, qp, kp, vp, kw, vw, m_b, l_b, o_b, m_s, l_s, o_s):
        ones = _group_ones()
        masks = _head_masks()
        _, _, _, _, qn, kn = _dil_prep(q_ref, k_ref, gq_ref, gk_ref, cos_ref, up_ref, dn_ref, ones)
        qs[...] = qn
        ks[...] = kn
        vs[...] = v_ref[...].astype(F32)
        nblk = seq // BAND

        for d in DILATIONS:
            for tok, res in _residues(d, seq):
                qv = qs[tok, :]
                for hd in range(2):
                    qp[hd, res, :] = (qv * masks[hd]).astype(BF16)
                kp[res, :] = ks[tok, :].astype(BF16)
                vp[res, :] = vs[tok, :].astype(BF16)
            keys_k, keys_v, bias = _dil_keys(d, seq, kp, vp, kw, vw)
            m_t = jnp.zeros((nblk, BAND, LANES), F32)
            l_t = jnp.zeros((nblk, BAND, LANES), F32)
            o_t = jnp.zeros((nblk, BAND, LANES), F32)
            for hd in range(2):
                s = _dot(qp[hd].reshape(nblk, BAND, LANES), keys_k, BATCH_NT)
                for b_ in bias:
                    s = s + b_
                m = jnp.max(s, axis=-1, keepdims=True)
                p = jnp.exp2(s - m)
                m_t = m_t + m * masks[hd]
                l_t = l_t + jnp.sum(p, axis=-1, keepdims=True) * masks[hd]
                o_t = o_t + _dot(p.astype(BF16), keys_v, BATCH_NN) * masks[hd]
            m_b[...] = m_t.reshape(seq, LANES)
            l_b[...] = l_t.reshape(seq, LANES)
            o_b[...] = o_t.reshape(seq, LANES)
            for tok, res in _residues(d, seq):
                if d == DILATIONS[0]:
                    m_s[tok, :] = m_b[res, :]
                    l_s[tok, :] = l_b[res, :]
                    o_s[tok, :] = o_b[res, :]
                else:
                    m_old = m_s[tok, :]
                    m_new = jnp.maximum(m_old, m_b[res, :])
                    w_old = jnp.exp2(m_old - m_new)
                    w_new = jnp.exp2(m_b[res, :] - m_new)
                    l_s[tok, :] = l_s[tok, :] * w_old + l_b[res, :] * w_new
                    o_s[tok, :] = o_s[tok, :] * w_old + o_b[res, :] * w_new
                    m_s[tok, :] = m_new

        l = l_s[...]
        o_ref[...] = o_s[...] / l
        lse_ref[...] = m_s[...] + jnp.log2(l)

    blk = lambda o_: pl.BlockSpec((seq, LANES), lambda b, p: (b, o_ + p))
    vec = pl.BlockSpec((1, LANES), lambda b, p: (0, 0))
    tab = pl.BlockSpec((seq, LANES), lambda b, p: (0, 0))
    f32_buf = pltpu.VMEM((seq, LANES), F32)
    bf16_buf = pltpu.VMEM((seq, LANES), BF16)
    window_buf = pltpu.VMEM((seq // BAND, 2 * BAND, LANES), BF16)
    return pl.pallas_call(
        body, name="dil_fwd", grid=(nb, npair),
        in_specs=[blk(off), blk(off + npair), blk(off + 2 * npair), vec, vec, tab, tab, tab],
        out_specs=[blk(0), blk(0)],
        out_shape=[jax.ShapeDtypeStruct((t, W_GROUP), F32), jax.ShapeDtypeStruct((t, W_GROUP), F32)],
        scratch_shapes=[f32_buf, f32_buf, f32_buf, pltpu.VMEM((2, seq, LANES), BF16), bf16_buf, bf16_buf,
                        window_buf, window_buf, f32_buf, f32_buf, f32_buf, f32_buf, f32_buf, f32_buf],
        compiler_params=_params(("arbitrary", "arbitrary")),
    )(proj, proj, proj, gq, gk, cos, up, dn)


def _dil_bwd(proj, gq, gk, cos, up, dn, do, o, lse, nb, seq):
    t = nb * seq
    npair = W_GROUP // LANES
    off = 3 * npair

    def body(q_ref, k_ref, v_ref, gq_ref, gk_ref, cos_ref, up_ref, dn_ref, do_ref, o_ref, lse_ref,
             dq_ref, dk_ref, dv_ref, dg_ref, qs, ks, vs, delta, dq_s, dk_s, dv_s,
             qp, kp, vp, dop, kw, vw, lse_p, delta_p, dq_p, dk_p, dv_p):
        @pl.when((pl.program_id(0) == 0) & (pl.program_id(1) == 0))
        def _():
            dg_ref[...] = jnp.zeros_like(dg_ref)

        ones = _group_ones()
        masks = _head_masks()
        qhat, rq, khat, rk, qn, kn = _dil_prep(q_ref, k_ref, gq_ref, gk_ref, cos_ref, up_ref, dn_ref, ones)
        qs[...] = qn
        ks[...] = kn
        vs[...] = v_ref[...].astype(F32)
        delta[...] = _groupsum(do_ref[...] * o_ref[...], ones)
        nblk = seq // BAND

        for d in DILATIONS:
            for tok, res in _residues(d, seq):
                qv = qs[tok, :]
                dov = do_ref[tok, :]
                for hd in range(2):
                    qp[hd, res, :] = (qv * masks[hd]).astype(BF16)
                    dop[hd, res, :] = (dov * masks[hd]).astype(BF16)
                kp[res, :] = ks[tok, :].astype(BF16)
                vp[res, :] = vs[tok, :].astype(BF16)
                lse_p[res, :] = lse_ref[tok, :]
                delta_p[res, :] = delta[tok, :]
            keys_k, keys_v, bias = _dil_keys(d, seq, kp, vp, kw, vw)
            nk = keys_k.shape[1]
            dq_b = jnp.zeros((nblk, BAND, LANES), F32)
            dk_b = jnp.zeros((nblk, nk, LANES), F32)
            dv_b = jnp.zeros((nblk, nk, LANES), F32)
            for hd in range(2):
                lane0 = hd * HEAD_DIM
                q3 = qp[hd].reshape(nblk, BAND, LANES)
                do3 = dop[hd].reshape(nblk, BAND, LANES)
                z = _dot(q3, keys_k, BATCH_NT)
                for b_ in bias:
                    z = z + b_
                p = jnp.exp2(z - lse_p[...].reshape(nblk, BAND, LANES)[:, :, lane0:lane0 + 1])
                dp = _dot(do3, keys_v, BATCH_NT)
                ds = (p * (dp - delta_p[...].reshape(nblk, BAND, LANES)[:, :, lane0:lane0 + 1])).astype(BF16)
                dq_b = dq_b + _dot(ds, keys_k, BATCH_NN) * masks[hd]
                dk_b = dk_b + _dot(ds, q3, BATCH_TN)
                dv_b = dv_b + _dot(p.astype(BF16), do3, BATCH_TN)
            dq_p[...] = dq_b.reshape(seq, LANES)
            for acc, out in ((dk_b, dk_p), (dv_b, dv_p)):
                out[...] = acc[:, nk - BAND:, :].reshape(seq, LANES)
                if nk > BAND:
                    out[0:seq - BAND, :] += acc[1:, :BAND, :].reshape(seq - BAND, LANES)
            for tok, res in _residues(d, seq):
                if d == DILATIONS[0]:
                    dq_s[tok, :] = dq_p[res, :]
                    dk_s[tok, :] = dk_p[res, :]
                    dv_s[tok, :] = dv_p[res, :]
                else:
                    dq_s[tok, :] += dq_p[res, :]
                    dk_s[tok, :] += dk_p[res, :]
                    dv_s[tok, :] += dv_p[res, :]

        cos, up, dn = cos_ref[...], up_ref[...], dn_ref[...]
        dq_raw, dgq = _head_norm_bwd(_rope_bwd(dq_s[...] * SCALE, cos, up, dn), qhat, rq, gq_ref[...], ones)
        dk_raw, dgk = _head_norm_bwd(_rope_bwd(dk_s[...] * LN2, cos, up, dn), khat, rk, gk_ref[...], ones)
        dq_ref[...] = dq_raw.astype(BF16)
        dk_ref[...] = dk_raw.astype(BF16)
        dv_ref[...] = dv_s[...].astype(BF16)
        dg_ref[0:1, :] += dgq
        dg_ref[1:2, :] += dgk

    blk = lambda o_: pl.BlockSpec((seq, LANES), lambda b, p: (b, o_ + p))
    vec = pl.BlockSpec((1, LANES), lambda b, p: (0, 0))
    tab = pl.BlockSpec((seq, LANES), lambda b, p: (0, 0))
    f32_buf = pltpu.VMEM((seq, LANES), F32)
    bf16_buf = pltpu.VMEM((seq, LANES), BF16)
    window_buf = pltpu.VMEM((seq // BAND, 2 * BAND, LANES), BF16)
    bf16_pair = pltpu.VMEM((2, seq, LANES), BF16)
    return pl.pallas_call(
        body, name="dil_bwd", grid=(nb, npair),
        in_specs=[blk(off), blk(off + npair), blk(off + 2 * npair), vec, vec, tab, tab, tab,
                  blk(0), blk(0), blk(0)],
        out_specs=[blk(0), blk(0), blk(0), pl.BlockSpec((8, LANES), lambda b, p: (0, 0))],
        out_shape=[jax.ShapeDtypeStruct((t, W_GROUP), BF16), jax.ShapeDtypeStruct((t, W_GROUP), BF16),
                   jax.ShapeDtypeStruct((t, W_GROUP), BF16), jax.ShapeDtypeStruct((8, LANES), F32)],
        scratch_shapes=[f32_buf] * 7 + [bf16_pair, bf16_buf, bf16_buf, bf16_pair, window_buf, window_buf]
        + [f32_buf] * 5,
        compiler_params=_params(("arbitrary", "arbitrary")),
    )(proj, proj, proj, gq, gk, cos, up, dn, do, o, lse)


def _adamw(w, g, m, v, name):
    rows, cols = w.shape
    tr = _row_tile(rows) if rows >= 8 else rows
    c1 = 1.0 - ADAM_B1 ** ADAM_STEP
    c2 = 1.0 - ADAM_B2 ** ADAM_STEP

    def body(w_ref, g_ref, m_ref, v_ref, d_ref, nm_ref, nv_ref):
        g_ = g_ref[...]
        nm = ADAM_B1 * m_ref[...] + (1.0 - ADAM_B1) * g_
        nv = ADAM_B2 * v_ref[...] + (1.0 - ADAM_B2) * (g_ * g_)
        nm_ref[...] = nm
        nv_ref[...] = nv
        d_ref[...] = -ADAM_LR * ((nm / c1) / (jnp.sqrt(nv / c2) + ADAM_EPS) + ADAM_WD * w_ref[...])

    spec = pl.BlockSpec((tr, cols), lambda i: (i, 0))
    shape = jax.ShapeDtypeStruct((rows, cols), F32)
    return pl.pallas_call(
        body, name=name, grid=(rows // tr,), in_specs=[spec] * 4, out_specs=[spec] * 3,
        out_shape=[shape] * 3, compiler_params=_params(("arbitrary",)),
    )(w, g, m, v)


def _place():
    x, y, c = lax.axis_index("x"), lax.axis_index("y"), lax.axis_index("c")
    chips = [(1 - x, y), (x, 1 - y), (1 - x, 1 - y)]
    return x, y, c, chips


def _gather_weight(w, name):
    rows, cols = w.shape
    half_rows = rows // 2

    def body(w_ref, out_ref, send_sems, recv_sems):
        x, y, c, chips = _place()
        sibling = (x, y, 1 - c)
        mine = 2 * x + y
        lo = pl.multiple_of(c * half_rows, 16)
        lo_sib = pl.multiple_of((1 - c) * half_rows, 16)
        out_ref[mine] = w_ref[...].astype(BF16)

        def copy(k, shard, first_row, to):
            ref = out_ref.at[shard, pl.ds(first_row, half_rows), :]
            return pltpu.make_async_remote_copy(src_ref=ref, dst_ref=ref, send_sem=send_sems.at[k],
                                                recv_sem=recv_sems.at[k], device_id=to, device_id_type=MESH)

        sends = [copy(k, mine, lo, (cx, cy, c)) for k, (cx, cy) in enumerate(chips)]
        for cp in sends:
            cp.start()
        passed = []
        for k, (cx, cy) in enumerate(chips):
            theirs = 2 * cx + cy
            copy(k, theirs, lo, (cx, cy, c)).wait_recv()
            fw = copy(3 + k, theirs, lo, sibling)
            fw.start()
            passed.append(fw)
        for k, (cx, cy) in enumerate(chips):
            copy(3 + k, 2 * cx + cy, lo_sib, sibling).wait_recv()
        for cp in sends + passed:
            cp.wait_send()

    return pl.pallas_call(
        body, name=name,
        in_specs=[pl.BlockSpec(memory_space=pltpu.VMEM)],
        out_specs=pl.BlockSpec(memory_space=pltpu.VMEM),
        out_shape=jax.ShapeDtypeStruct((4, rows, cols), BF16),
        scratch_shapes=[pltpu.SemaphoreType.DMA((6,)), pltpu.SemaphoreType.DMA((6,))],
        compiler_params=pltpu.CompilerParams(vmem_limit_bytes=VMEM_LIMIT),
    )(w)


def _reduce_scatter_weight(g4, name):
    _, rows, cols = g4.shape
    half_rows = rows // 2

    def body(g_ref, out_ref, sib_buf, stage, landed, send_sems, recv_sems):
        x, y, c, chips = _place()
        sibling = (x, y, 1 - c)
        mine = 2 * x + y
        lo = pl.multiple_of(c * half_rows, 8)
        lo_sib = pl.multiple_of((1 - c) * half_rows, 8)

        def copy(k, src, dst, to):
            return pltpu.make_async_remote_copy(src_ref=src, dst_ref=dst, send_sem=send_sems.at[k],
                                                recv_sem=recv_sems.at[k], device_id=to, device_id_type=MESH)

        swap = copy(0, g_ref.at[:, pl.ds(lo_sib, half_rows), :], sib_buf, sibling)
        swap.start()
        swap.wait_recv()
        sends = []
        for k, (cx, cy) in enumerate(chips):
            theirs = 2 * cx + cy
            stage[k] = (g_ref[theirs, pl.ds(lo, half_rows), :] + sib_buf[theirs]).astype(BF16)
            cp = copy(1 + k, stage.at[k], landed.at[k], (cx, cy, c))
            cp.start()
            sends.append(cp)
        acc = g_ref[mine, pl.ds(lo, half_rows), :] + sib_buf[mine]
        for k, (cx, cy) in enumerate(chips):
            copy(1 + k, stage.at[k], landed.at[k], (cx, cy, c)).wait_recv()
            acc = acc + landed[k].astype(F32)
        out_ref[pl.ds(lo, half_rows), :] = acc
        done = copy(4, out_ref.at[pl.ds(lo, half_rows), :], out_ref.at[pl.ds(lo, half_rows), :], sibling)
        done.start()
        copy(4, out_ref.at[pl.ds(lo_sib, half_rows), :], out_ref.at[pl.ds(lo_sib, half_rows), :], sibling).wait_recv()
        for cp in [swap] + sends + [done]:
            cp.wait_send()

    return pl.pallas_call(
        body, name=name,
        in_specs=[pl.BlockSpec(memory_space=pltpu.VMEM)],
        out_specs=pl.BlockSpec(memory_space=pltpu.VMEM),
        out_shape=jax.ShapeDtypeStruct((rows, cols), F32),
        scratch_shapes=[pltpu.VMEM((4, half_rows, cols), F32), pltpu.VMEM((3, half_rows, cols), BF16),
                        pltpu.VMEM((3, half_rows, cols), BF16),
                        pltpu.SemaphoreType.DMA((5,)), pltpu.SemaphoreType.DMA((5,))],
        compiler_params=pltpu.CompilerParams(vmem_limit_bytes=VMEM_LIMIT),
    )(g4)


def _all_sum_small(v):
    shape = v.shape

    def body(v_ref, out_ref, buf, send_sems, recv_sems):
        x, y, c, _ = _place()
        me = 4 * x + 2 * y + c
        buf[me] = v_ref[...]
        flips = [(dx, dy, dc) for dx in (0, 1) for dy in (0, 1) for dc in (0, 1)][1:]

        def copy(k, slot, flip):
            dx, dy, dc = flip
            to = (1 - x if dx else x, 1 - y if dy else y, 1 - c if dc else c)
            return pltpu.make_async_remote_copy(src_ref=buf.at[slot], dst_ref=buf.at[slot], send_sem=send_sems.at[k],
                                                recv_sem=recv_sems.at[k], device_id=to, device_id_type=MESH)

        sends = [copy(k, me, flip) for k, flip in enumerate(flips)]
        for cp in sends:
            cp.start()
        for k, (dx, dy, dc) in enumerate(flips):
            sender = 4 * (1 - x if dx else x) + 2 * (1 - y if dy else y) + (1 - c if dc else c)
            copy(k, sender, (dx, dy, dc)).wait_recv()
        for cp in sends:
            cp.wait_send()
        total = buf[0]
        for i in range(1, 8):
            total = total + buf[i]
        out_ref[...] = total

    return pl.pallas_call(
        body, name="all_sum_small",
        in_specs=[pl.BlockSpec(memory_space=pltpu.VMEM)],
        out_specs=pl.BlockSpec(memory_space=pltpu.VMEM),
        out_shape=jax.ShapeDtypeStruct(shape, F32),
        scratch_shapes=[pltpu.VMEM((8,) + shape, F32), pltpu.SemaphoreType.DMA((7,)), pltpu.SemaphoreType.DMA((7,))],
    )(v)


SMALL = (("g_mix", 1024), ("g_ffn", 1024), ("g_out_fox", 512), ("g_out_dil", 512), ("g_q_fox", 64),
         ("g_k_fox", 64), ("g_q_dil", 64), ("g_k_dil", 64), ("b_forget", 8))
SMALL_PACKED = (32, LANES)


def _local_grads(x, target, gains, w1, wft, w_out, w_gate, w_up, w_down, nb, seq):
    tile2 = lambda g: jnp.tile(g, (1, 2))
    gq_f, gk_f, gq_d, gk_d = (tile2(gains[n]) for n in ("g_q_fox", "g_k_fox", "g_q_dil", "g_k_dil"))
    b_col = gains["b_forget"].reshape(N_FOX_HEADS, 1)
    cos, up, dn = _rope_tables(seq)
    npair = N_FOX_HEADS // 2

    proj, fa_row, h1 = _in_proj(x, gains["g_mix"], w1, wft)
    c_row = _gate_fwd(fa_row, b_col, seq)
    c3 = c_row.reshape(npair, 2, nb * seq)
    o_fox, lse_fox = _fox_fwd(proj, c3, gq_f, gk_f, nb, seq)
    o_dil, lse_dil = _dil_fwd(proj, gq_d, gk_d, cos, up, dn, nb, seq)
    x1, o_n = _attn_out(o_fox, o_dil, x, gains["g_out_fox"], gains["g_out_dil"], w_out)
    a, u, dy, loss_parts = _ffn_fwd(x1, target, gains["g_ffn"], w_gate, w_up, w_down)
    loss = jnp.sum(loss_parts[:, 0, 0])

    dx1, s, da, du, h2, dg_ffn = _ffn_bwd(dy, a, u, x1, gains["g_ffn"], w_gate, w_up, w_down)
    d_w_down = _tn_matmul(s, dy, "dw_down")
    d_w_gate = _tn_matmul(h2, da, "dw_gate")
    d_w_up = _tn_matmul(h2, du, "dw_up")
    d_w_out = _tn_matmul(o_n, dx1, "dw_out")
    do_fox, do_dil, dg_of, dg_od = _attn_out_bwd(dx1, o_fox, o_dil, gains["g_out_fox"], gains["g_out_dil"], w_out)
    dq_f, dk_f, dv_f, dc3, dg_fox = _fox_bwd(proj, c3, gq_f, gk_f, do_fox, o_fox, lse_fox, nb, seq)
    dq_d, dk_d, dv_d, dg_dil = _dil_bwd(proj, gq_d, gk_d, cos, up, dn, do_dil, o_dil, lse_dil, nb, seq)
    dfa_row, db = _gate_bwd(dc3.reshape(N_FOX_HEADS, nb * seq), fa_row, b_col, seq)
    dparts = [dq_f, dk_f, dv_f, dq_d, dk_d, dv_d]
    grad_x, dg_mix = _in_proj_bwd(dparts, dfa_row, w1, wft, x, gains["g_mix"], dx1)
    d_w1 = jnp.concatenate([_tn_matmul(h1, dp, "dw_in_%d" % j) for j, dp in enumerate(dparts)], axis=1)
    d_wf = _row_matmul(dfa_row, h1, "dw_forget")

    fold = lambda g2: (g2[:, :HEAD_DIM] + g2[:, HEAD_DIM:])
    small = {
        "g_mix": dg_mix[0:1], "g_ffn": dg_ffn[0:1], "g_out_fox": dg_of[0:1], "g_out_dil": dg_od[0:1],
        "g_q_fox": fold(dg_fox[0:1]), "g_k_fox": fold(dg_fox[1:2]),
        "g_q_dil": fold(dg_dil[0:1]), "g_k_dil": fold(dg_dil[1:2]),
        "b_forget": db[:, 0].reshape(1, N_FOX_HEADS),
    }
    big = {"w1": d_w1, "wf": d_wf, "w_out": d_w_out, "w_gate": d_w_gate, "w_up": d_w_up, "w_down": d_w_down}
    return loss, grad_x, big, small


def _shards_of_columns(full, n=4):
    r, nc = full.shape
    return full.reshape(r, n, nc // n).transpose(1, 0, 2)


def _columns_of_shards(slabs):
    n, r, c = slabs.shape
    return slabs.transpose(1, 0, 2).reshape(r, n * c)


def kernel(x, g_mix, w_in, b_forget, g_q_fox, g_k_fox, g_q_dil, g_k_dil, g_out_fox, g_out_dil, w_out, g_ffn, w_gate, w_up, w_down, loss_target, m_g_mix, m_w_in, m_b_forget, m_g_q_fox, m_g_k_fox, m_g_q_dil, m_g_k_dil, m_g_out_fox, m_g_out_dil, m_w_out, m_g_ffn, m_w_gate, m_w_up, m_w_down, v_g_mix, v_w_in, v_b_forget, v_g_q_fox, v_g_k_fox, v_g_q_dil, v_g_k_dil, v_g_out_fox, v_g_out_dil, v_w_out, v_g_ffn, v_w_gate, v_w_up, v_w_down):
    nb, seq, d = x.shape
    weights = dict(g_mix=g_mix, w_in=w_in, b_forget=b_forget, g_q_fox=g_q_fox, g_k_fox=g_k_fox, g_q_dil=g_q_dil,
                   g_k_dil=g_k_dil, g_out_fox=g_out_fox, g_out_dil=g_out_dil, w_out=w_out, g_ffn=g_ffn,
                   w_gate=w_gate, w_up=w_up, w_down=w_down)
    m_in = dict(g_mix=m_g_mix, w_in=m_w_in, b_forget=m_b_forget, g_q_fox=m_g_q_fox, g_k_fox=m_g_k_fox,
                g_q_dil=m_g_q_dil, g_k_dil=m_g_k_dil, g_out_fox=m_g_out_fox, g_out_dil=m_g_out_dil, w_out=m_w_out,
                g_ffn=m_g_ffn, w_gate=m_w_gate, w_up=m_w_up, w_down=m_w_down)
    v_in = dict(g_mix=v_g_mix, w_in=v_w_in, b_forget=v_b_forget, g_q_fox=v_g_q_fox, g_k_fox=v_g_k_fox,
                g_q_dil=v_g_q_dil, g_k_dil=v_g_k_dil, g_out_fox=v_g_out_fox, g_out_dil=v_g_out_dil, w_out=v_w_out,
                g_ffn=v_g_ffn, w_gate=v_w_gate, w_up=v_w_up, w_down=v_w_down)
    order = ["g_mix", "w_in", "b_forget", "g_q_fox", "g_k_fox", "g_q_dil", "g_k_dil", "g_out_fox", "g_out_dil",
             "w_out", "g_ffn", "w_gate", "w_up", "w_down"]

    w_in_full = _columns_of_shards(_gather_weight(w_in[0], "gather_w_in"))
    fox_w = 3 * W_GROUP
    w1 = jnp.concatenate([w_in_full[:, :fox_w], w_in_full[:, fox_w + N_FOX_HEADS:]], axis=1)
    wft = w_in_full[:, fox_w:fox_w + N_FOX_HEADS].T
    w_out_full = _gather_weight(w_out[0], "gather_w_out").reshape(d, d)
    w_gate_full = _columns_of_shards(_gather_weight(w_gate[0], "gather_w_gate"))
    w_up_full = _columns_of_shards(_gather_weight(w_up[0], "gather_w_up"))
    w_down_full = _gather_weight(w_down[0], "gather_w_down").reshape(-1, d)

    gains = {n: weights[n] for n, _ in SMALL}
    loss, grad_x, big, small = _local_grads(
        x.reshape(nb * seq, d), loss_target.reshape(nb * seq, d), gains,
        w1, wft, w_out_full, w_gate_full, w_up_full, w_down_full, nb, seq)
    loss = lax.psum(loss, ("x", "y", "c"))

    d_w_in_full = jnp.concatenate([big["w1"][:, :fox_w], big["wf"].T, big["w1"][:, fox_w:]], axis=1)
    grads = {
        "w_in": _reduce_scatter_weight(_shards_of_columns(d_w_in_full), "reduce_w_in"),
        "w_out": _reduce_scatter_weight(big["w_out"].reshape(4, d // 4, d), "reduce_w_out"),
        "w_gate": _reduce_scatter_weight(_shards_of_columns(big["w_gate"]), "reduce_w_gate"),
        "w_up": _reduce_scatter_weight(_shards_of_columns(big["w_up"]), "reduce_w_up"),
        "w_down": _reduce_scatter_weight(big["w_down"].reshape(4, -1, d), "reduce_w_down"),
    }
    packed = jnp.concatenate([small[n].reshape(-1) for n, _ in SMALL])
    packed = jnp.pad(packed, (0, SMALL_PACKED[0] * SMALL_PACKED[1] - packed.shape[0])).reshape(SMALL_PACKED)
    summed = _all_sum_small(packed).reshape(-1)
    pos = 0
    for n, size in SMALL:
        grads[n] = summed[pos:pos + size].reshape(1, size)
        pos += size

    deltas, new_m, new_v, grad_out = {}, {}, {}, {}
    for n in order:
        w2, m2, v2 = weights[n], m_in[n], v_in[n]
        shape = w2.shape
        flat = (lambda a: a.reshape(shape[-2], shape[-1])) if w2.ndim == 3 else (lambda a: a)
        g2 = grads[n]
        dl, nm, nv = _adamw(flat(w2), g2, flat(m2), flat(v2), "adamw_" + n)
        grad_out[n] = g2.reshape(shape)
        deltas[n], new_m[n], new_v[n] = dl.reshape(shape), nm.reshape(shape), nv.reshape(shape)

    return (loss, grad_x.reshape(nb, seq, d), *[grad_out[n] for n in order], *[deltas[n] for n in order],
            *[new_m[n] for n in order], *[new_v[n] for n in order])
```

```python
import functools
import math

import numpy as np
import jax
import jax.numpy as jnp
from jax import lax
from jax.experimental import pallas as pl
from jax.experimental.pallas import tpu as pltpu

F32, BF16 = jnp.float32, jnp.bfloat16
MESH = pl.DeviceIdType.MESH

EPS = 1e-6
NEG = -1e30
HEAD_DIM = 64
SCALE = HEAD_DIM ** -0.5
LOG2E = math.log2(math.e)
LN2 = math.log(2.0)
ROPE_THETA = 500000.0
ROPE_DIM = HEAD_DIM // 4
LANES = 128
W_GROUP = 512
N_FOX_HEADS = 8
VMEM_LIMIT = 56 * 1024 * 1024
DILATIONS = (1, 4, 16)
BAND = 128

ADAM_LR, ADAM_B1, ADAM_B2, ADAM_EPS, ADAM_WD, ADAM_STEP = 0.001, 0.9, 0.999, 1e-08, 0.01, 10

NT = (((1,), (1,)), ((), ()))
TN = (((0,), (0,)), ((), ()))
BATCH_NT = (((2,), (2,)), ((0,), (0,)))
BATCH_NN = (((2,), (1,)), ((0,), (0,)))
BATCH_TN = (((1,), (1,)), ((0,), (0,)))


def _params(sem=None):
    return pltpu.CompilerParams(dimension_semantics=sem, vmem_limit_bytes=VMEM_LIMIT)


def _dot(a, b, dims=None):
    if dims is None:
        return jnp.dot(a, b, preferred_element_type=F32)
    return lax.dot_general(a, b, dims, preferred_element_type=F32)


def _group_ones():
    i = lax.broadcasted_iota(jnp.int32, (LANES, LANES), 0) >> 6
    j = lax.broadcasted_iota(jnp.int32, (LANES, LANES), 1) >> 6
    return (i == j).astype(BF16)


def _split3(x):
    a = x.astype(BF16)
    r = x - a.astype(F32)
    b = r.astype(BF16)
    c = (r - b.astype(F32)).astype(BF16)
    return a, b, c


def _groupsum(x, ones):
    a, b, c = _split3(x)
    return _dot(a, ones) + _dot(b, ones) + _dot(c, ones)


def _head_masks():
    lane = lax.broadcasted_iota(jnp.int32, (1, LANES), 1)
    return [(lane < HEAD_DIM).astype(F32), (lane >= HEAD_DIM).astype(F32)]


def _head_norm(raw, gain, ones):
    r = lax.rsqrt(_groupsum(raw * raw, ones) * (1.0 / HEAD_DIM) + EPS)
    return raw * r, r


def _head_norm_bwd(dy, xhat, r, gain, ones):
    u = dy * gain
    dgain = jnp.sum(dy * xhat, axis=0, keepdims=True)
    draw = r * (u - xhat * (_groupsum(u * xhat, ones) * (1.0 / HEAD_DIM)))
    return draw, dgain


def _rope(x, cos, s_up, s_dn):
    return x * cos + pltpu.roll(x, LANES - 8, 1) * s_up + pltpu.roll(x, 8, 1) * s_dn


def _rope_bwd(dy, cos, s_up, s_dn):
    return dy * cos + pltpu.roll(dy * s_up, 8, 1) + pltpu.roll(dy * s_dn, LANES - 8, 1)


def _rope_tables(seq):
    half = ROPE_DIM // 2
    inv_freq = jnp.power(jnp.float32(ROPE_THETA), -jnp.arange(half, dtype=F32) * 2.0 / ROPE_DIM)
    ang = jnp.arange(seq).astype(F32)[:, None] * inv_freq[None, :]
    cos, sin = jnp.cos(ang), jnp.sin(ang)
    one = jnp.ones((seq, HEAD_DIM - ROPE_DIM), F32)
    zero_h = jnp.zeros((seq, half), F32)
    zero_r = jnp.zeros((seq, HEAD_DIM - ROPE_DIM), F32)
    c = jnp.concatenate([cos, cos, one], axis=1)
    up = jnp.concatenate([-sin, zero_h, zero_r], axis=1)
    dn = jnp.concatenate([zero_h, sin, zero_r], axis=1)
    return jnp.tile(c, (1, 2)), jnp.tile(up, (1, 2)), jnp.tile(dn, (1, 2))


def _row_tile(rows, cap=256):
    best = rows
    for t in range(8, min(rows, cap) + 1, 8):
        if rows % t == 0:
            best = t
    return best


def _in_proj(x, g_mix, w1, wft):
    t, d = x.shape
    n = w1.shape[1]
    tt = 512

    def body(x_ref, g_ref, w_ref, wf_ref, p_ref, fa_ref, h_ref, ht_ref):
        xx = x_ref[...]
        r = lax.rsqrt(jnp.mean(xx * xx, axis=-1, keepdims=True) + EPS)
        h = (xx * r * g_ref[...]).astype(BF16)
        h_ref[...] = h
        ht_ref[...] = h.T
        for j in range(n // W_GROUP):
            cols = slice(j * W_GROUP, (j + 1) * W_GROUP)
            p_ref[:, cols] = _dot(h, w_ref[:, cols]).astype(BF16)
        fa_ref[...] = _dot(wf_ref[...], h, NT)

    return pl.pallas_call(
        body, name="in_proj", grid=(t // tt,),
        in_specs=[pl.BlockSpec((tt, d), lambda i: (i, 0)), pl.BlockSpec((1, d), lambda i: (0, 0)),
                  pl.BlockSpec(memory_space=pltpu.VMEM), pl.BlockSpec(memory_space=pltpu.VMEM)],
        out_specs=[pl.BlockSpec((tt, n), lambda i: (i, 0)), pl.BlockSpec((8, tt), lambda i: (0, i)),
                   pl.BlockSpec((tt, d), lambda i: (i, 0)), pl.BlockSpec((d, tt), lambda i: (0, i))],
        out_shape=[jax.ShapeDtypeStruct((t, n), BF16), jax.ShapeDtypeStruct((8, t), F32),
                   jax.ShapeDtypeStruct((t, d), BF16), jax.ShapeDtypeStruct((d, t), BF16)],
        compiler_params=_params(("arbitrary",)),
    )(x, g_mix, w1, wft)


def _tri(n, upper):
    i = lax.broadcasted_iota(jnp.int32, (n, n), 0)
    j = lax.broadcasted_iota(jnp.int32, (n, n), 1)
    return ((i <= j) if upper else (i >= j)).astype(BF16)


def _gate_fwd(fa_row, b_col, seq):
    t = fa_row.shape[1]
    cb = 256

    def body(fa_ref, b_ref, c_ref):
        tri = _tri(cb, True)
        carry = jnp.zeros((8, 1), F32)
        for k in range(seq // cb):
            z = fa_ref[:, k * cb:(k + 1) * cb] + b_ref[...]
            lf = jnp.minimum(z, 0.0) - jnp.log(1.0 + jnp.exp(-jnp.abs(z)))
            a, b, c = _split3(lf)
            blk = _dot(a, tri) + _dot(b, tri) + _dot(c, tri) + carry
            c_ref[:, k * cb:(k + 1) * cb] = blk
            carry = blk[:, cb - 1:cb]

    return pl.pallas_call(
        body, name="gate_fwd", grid=(t // seq,),
        in_specs=[pl.BlockSpec((8, seq), lambda i: (0, i)), pl.BlockSpec((8, 1), lambda i: (0, 0))],
        out_specs=pl.BlockSpec((8, seq), lambda i: (0, i)),
        out_shape=jax.ShapeDtypeStruct((8, t), F32),
        compiler_params=_params(("arbitrary",)),
    )(fa_row, b_col)


def _gate_bwd(dc_row, fa_row, b_col, seq):
    t = fa_row.shape[1]
    cb = 256

    def body(dc_ref, fa_ref, b_ref, dfa_ref, db_ref):
        @pl.when(pl.program_id(0) == 0)
        def _():
            db_ref[...] = jnp.zeros_like(db_ref)

        tri = _tri(cb, False)
        carry = jnp.zeros((8, 1), F32)
        dbs = jnp.zeros((8, 1), F32)
        for k in reversed(range(seq // cb)):
            a, b, c = _split3(dc_ref[:, k * cb:(k + 1) * cb])
            dlf = _dot(a, tri) + _dot(b, tri) + _dot(c, tri) + carry
            carry = dlf[:, 0:1]
            z = fa_ref[:, k * cb:(k + 1) * cb] + b_ref[...]
            dfa = dlf / (1.0 + jnp.exp(z))
            dfa_ref[:, k * cb:(k + 1) * cb] = dfa
            dbs = dbs + jnp.sum(dfa, axis=1, keepdims=True)
        db_ref[...] += jnp.broadcast_to(dbs, (8, LANES))

    return pl.pallas_call(
        body, name="gate_bwd", grid=(t // seq,),
        in_specs=[pl.BlockSpec((8, seq), lambda i: (0, i)), pl.BlockSpec((8, seq), lambda i: (0, i)),
                  pl.BlockSpec((8, 1), lambda i: (0, 0))],
        out_specs=[pl.BlockSpec((8, seq), lambda i: (0, i)), pl.BlockSpec((8, LANES), lambda i: (0, 0))],
        out_shape=[jax.ShapeDtypeStruct((8, t), F32), jax.ShapeDtypeStruct((8, LANES), F32)],
        compiler_params=_params(("arbitrary",)),
    )(dc_row, fa_row, b_col)


def _attn_out(o_fox, o_dil, x, g_fox, g_dil, w_out):
    t, d = x.shape
    w = o_fox.shape[1]
    tt = 512

    def body(of_ref, od_ref, x_ref, gf_ref, gd_ref, w_ref, x1_ref, ont_ref):
        acc = x_ref[...]
        for k, (o_ref, g_ref) in enumerate(((of_ref, gf_ref), (od_ref, gd_ref))):
            o = o_ref[...]
            r = lax.rsqrt(jnp.mean(o * o, axis=-1, keepdims=True) + EPS)
            on = (o * r * g_ref[...]).astype(BF16)
            ont_ref[k * w:(k + 1) * w, :] = on.T
            acc = acc + _dot(on, w_ref[k * w:(k + 1) * w, :])
        x1_ref[...] = acc

    return pl.pallas_call(
        body, name="attn_out", grid=(t // tt,),
        in_specs=[pl.BlockSpec((tt, w), lambda i: (i, 0)), pl.BlockSpec((tt, w), lambda i: (i, 0)),
                  pl.BlockSpec((tt, d), lambda i: (i, 0)), pl.BlockSpec((1, w), lambda i: (0, 0)),
                  pl.BlockSpec((1, w), lambda i: (0, 0)), pl.BlockSpec(memory_space=pltpu.VMEM)],
        out_specs=[pl.BlockSpec((tt, d), lambda i: (i, 0)), pl.BlockSpec((2 * w, tt), lambda i: (0, i))],
        out_shape=[jax.ShapeDtypeStruct((t, d), F32), jax.ShapeDtypeStruct((2 * w, t), BF16)],
        compiler_params=_params(("arbitrary",)),
    )(o_fox, o_dil, x, g_fox, g_dil, w_out)


def _attn_out_bwd(dx1, o_fox, o_dil, g_fox, g_dil, w_out):
    t, d = dx1.shape
    w = o_fox.shape[1]
    tt = 512

    def body(dx_ref, of_ref, od_ref, gf_ref, gd_ref, w_ref, dof_ref, dod_ref, dgf_ref, dgd_ref):
        @pl.when(pl.program_id(0) == 0)
        def _():
            dgf_ref[...] = jnp.zeros_like(dgf_ref)
            dgd_ref[...] = jnp.zeros_like(dgd_ref)

        dxb = dx_ref[...].astype(BF16)
        for k, (o_ref, g_ref, do_ref, dg_ref) in enumerate(
                ((of_ref, gf_ref, dof_ref, dgf_ref), (od_ref, gd_ref, dod_ref, dgd_ref))):
            don = _dot(dxb, w_ref[k * w:(k + 1) * w, :], NT)
            o = o_ref[...]
            r = lax.rsqrt(jnp.mean(o * o, axis=-1, keepdims=True) + EPS)
            xhat = o * r
            u = don * g_ref[...]
            do_ref[...] = r * (u - xhat * jnp.mean(u * xhat, axis=-1, keepdims=True))
            dg_ref[0:1, :] += jnp.sum(don * xhat, axis=0, keepdims=True)

    return pl.pallas_call(
        body, name="attn_out_bwd", grid=(t // tt,),
        in_specs=[pl.BlockSpec((tt, d), lambda i: (i, 0)), pl.BlockSpec((tt, w), lambda i: (i, 0)),
                  pl.BlockSpec((tt, w), lambda i: (i, 0)), pl.BlockSpec((1, w), lambda i: (0, 0)),
                  pl.BlockSpec((1, w), lambda i: (0, 0)), pl.BlockSpec(memory_space=pltpu.VMEM)],
        out_specs=[pl.BlockSpec((tt, w), lambda i: (i, 0)), pl.BlockSpec((tt, w), lambda i: (i, 0)),
                   pl.BlockSpec((8, w), lambda i: (0, 0)), pl.BlockSpec((8, w), lambda i: (0, 0))],
        out_shape=[jax.ShapeDtypeStruct((t, w), F32), jax.ShapeDtypeStruct((t, w), F32),
                   jax.ShapeDtypeStruct((8, w), F32), jax.ShapeDtypeStruct((8, w), F32)],
        compiler_params=_params(("arbitrary",)),
    )(dx1, o_fox, o_dil, g_fox, g_dil, w_out)


def _ffn_fwd(x1, target, g_ffn, w_gate, w_up, w_down):
    t, d = x1.shape
    f = w_gate.shape[1]
    tt = 256

    def body(x_ref, t_ref, g_ref, wg_ref, wu_ref, wd_ref, a_ref, u_ref, dy_ref, loss_ref):
        xx = x_ref[...]
        r = lax.rsqrt(jnp.mean(xx * xx, axis=-1, keepdims=True) + EPS)
        h = (xx * r * g_ref[...]).astype(BF16)
        a = _dot(h, wg_ref[...])
        u = _dot(h, wu_ref[...])
        a_ref[...] = a.astype(BF16)
        u_ref[...] = u.astype(BF16)
        s = (a / (1.0 + jnp.exp(-a)) * u).astype(BF16)
        y = xx + _dot(s, wd_ref[...])
        e = y - t_ref[...]
        dy_ref[...] = e * (1.0 / d)
        loss_ref[...] = jnp.broadcast_to(0.5 * jnp.sum(e * e) * (1.0 / d), (1, 8, LANES))

    return pl.pallas_call(
        body, name="ffn_fwd", grid=(t // tt,),
        in_specs=[pl.BlockSpec((tt, d), lambda i: (i, 0)), pl.BlockSpec((tt, d), lambda i: (i, 0)),
                  pl.BlockSpec((1, d), lambda i: (0, 0)), pl.BlockSpec(memory_space=pltpu.VMEM),
                  pl.BlockSpec(memory_space=pltpu.VMEM), pl.BlockSpec(memory_space=pltpu.VMEM)],
        out_specs=[pl.BlockSpec((tt, f), lambda i: (i, 0)), pl.BlockSpec((tt, f), lambda i: (i, 0)),
                   pl.BlockSpec((tt, d), lambda i: (i, 0)), pl.BlockSpec((1, 8, LANES), lambda i: (i, 0, 0))],
        out_shape=[jax.ShapeDtypeStruct((t, f), BF16), jax.ShapeDtypeStruct((t, f), BF16),
                   jax.ShapeDtypeStruct((t, d), F32), jax.ShapeDtypeStruct((t // tt, 8, LANES), F32)],
        compiler_params=_params(("arbitrary",)),
    )(x1, target, g_ffn, w_gate, w_up, w_down)


def _ffn_bwd(dy, a, u, x1, g_ffn, w_gate, w_up, w_down):
    t, d = x1.shape
    f = w_gate.shape[1]
    tt = 256

    def body(dy_ref, a_ref, u_ref, x_ref, g_ref, wg_ref, wu_ref, wd_ref,
             dx_ref, s_ref, da_ref, du_ref, h_ref, dg_ref):
        @pl.when(pl.program_id(0) == 0)
        def _():
            dg_ref[...] = jnp.zeros_like(dg_ref)

        dy_ = dy_ref[...]
        ds = _dot(dy_.astype(BF16), wd_ref[...], NT)
        a_ = a_ref[...].astype(F32)
        u_ = u_ref[...].astype(F32)
        sig = 1.0 / (1.0 + jnp.exp(-a_))
        silu = a_ * sig
        s_ref[...] = (silu * u_).astype(BF16).T
        da = (ds * u_ * (sig * (1.0 + a_ * (1.0 - sig)))).astype(BF16)
        du = (ds * silu).astype(BF16)
        da_ref[...] = da
        du_ref[...] = du
        dh = _dot(da, wg_ref[...], NT) + _dot(du, wu_ref[...], NT)
        xx = x_ref[...]
        r = lax.rsqrt(jnp.mean(xx * xx, axis=-1, keepdims=True) + EPS)
        xhat = xx * r
        g = g_ref[...]
        h_ref[...] = (xhat * g).astype(BF16).T
        uu = dh * g
        dx_ref[...] = dy_ + r * (uu - xhat * jnp.mean(uu * xhat, axis=-1, keepdims=True))
        dg_ref[0:1, :] += jnp.sum(dh * xhat, axis=0, keepdims=True)

    return pl.pallas_call(
        body, name="ffn_bwd", grid=(t // tt,),
        in_specs=[pl.BlockSpec((tt, d), lambda i: (i, 0)), pl.BlockSpec((tt, f), lambda i: (i, 0)),
                  pl.BlockSpec((tt, f), lambda i: (i, 0)), pl.BlockSpec((tt, d), lambda i: (i, 0)),
                  pl.BlockSpec((1, d), lambda i: (0, 0)), pl.BlockSpec(memory_space=pltpu.VMEM),
                  pl.BlockSpec(memory_space=pltpu.VMEM), pl.BlockSpec(memory_space=pltpu.VMEM)],
        out_specs=[pl.BlockSpec((tt, d), lambda i: (i, 0)), pl.BlockSpec((f, tt), lambda i: (0, i)),
                   pl.BlockSpec((tt, f), lambda i: (i, 0)), pl.BlockSpec((tt, f), lambda i: (i, 0)),
                   pl.BlockSpec((d, tt), lambda i: (0, i)), pl.BlockSpec((8, d), lambda i: (0, 0))],
        out_shape=[jax.ShapeDtypeStruct((t, d), F32), jax.ShapeDtypeStruct((f, t), BF16),
                   jax.ShapeDtypeStruct((t, f), BF16), jax.ShapeDtypeStruct((t, f), BF16),
                   jax.ShapeDtypeStruct((d, t), BF16), jax.ShapeDtypeStruct((8, d), F32)],
        compiler_params=_params(("arbitrary",)),
    )(dy, a, u, x1, g_ffn, w_gate, w_up, w_down)


def _in_proj_bwd(dparts, dfa_row, w1, wft, x, g_mix, dx1):
    t, d = x.shape
    tt = 512
    npart = len(dparts)

    def body(*refs):
        dp_refs = refs[:npart]
        dfa_ref, w_ref, wf_ref, x_ref, g_ref, dx1_ref, dx_ref, dg_ref = refs[npart:]

        @pl.when(pl.program_id(0) == 0)
        def _():
            dg_ref[...] = jnp.zeros_like(dg_ref)

        dh = _dot(dfa_ref[...].astype(BF16), wf_ref[...], TN)
        for j in range(npart):
            dh = dh + _dot(dp_refs[j][...], w_ref[:, j * W_GROUP:(j + 1) * W_GROUP], NT)
        xx = x_ref[...]
        r = lax.rsqrt(jnp.mean(xx * xx, axis=-1, keepdims=True) + EPS)
        xhat = xx * r
        uu = dh * g_ref[...]
        dx_ref[...] = dx1_ref[...] + r * (uu - xhat * jnp.mean(uu * xhat, axis=-1, keepdims=True))
        dg_ref[0:1, :] += jnp.sum(dh * xhat, axis=0, keepdims=True)

    return pl.pallas_call(
        body, name="in_proj_bwd", grid=(t // tt,),
        in_specs=[pl.BlockSpec((tt, W_GROUP), lambda i: (i, 0)) for _ in range(npart)]
        + [pl.BlockSpec((8, tt), lambda i: (0, i)), pl.BlockSpec(memory_space=pltpu.VMEM),
           pl.BlockSpec(memory_space=pltpu.VMEM), pl.BlockSpec((tt, d), lambda i: (i, 0)),
           pl.BlockSpec((1, d), lambda i: (0, 0)), pl.BlockSpec((tt, d), lambda i: (i, 0))],
        out_specs=[pl.BlockSpec((tt, d), lambda i: (i, 0)), pl.BlockSpec((8, d), lambda i: (0, 0))],
        out_shape=[jax.ShapeDtypeStruct((t, d), F32), jax.ShapeDtypeStruct((8, d), F32)],
        compiler_params=_params(("arbitrary",)),
    )(*dparts, dfa_row, w1, wft, x, g_mix, dx1)


def _token_matmul(at, b, name, tn):
    m, t = at.shape
    n = b.shape[1]
    tk = 1024

    def body(a_ref, b_ref, o_ref):
        @pl.when(pl.program_id(1) == 0)
        def _():
            o_ref[...] = jnp.zeros_like(o_ref)

        o_ref[...] += _dot(a_ref[...], b_ref[...].astype(BF16))

    return pl.pallas_call(
        body, name=name, grid=(n // tn, t // tk),
        in_specs=[pl.BlockSpec((m, tk), lambda j, k: (0, k)), pl.BlockSpec((tk, tn), lambda j, k: (k, j))],
        out_specs=pl.BlockSpec((m, tn), lambda j, k: (0, j)),
        out_shape=jax.ShapeDtypeStruct((m, n), F32),
        compiler_params=_params(("arbitrary", "arbitrary")),
    )(at, b)


def _row_matmul(a_row, b, name):
    t, n = b.shape
    tk = 1024
    nk = t // tk

    def body(a_ref, b_ref, o_ref):
        @pl.when(pl.program_id(0) == 0)
        def _():
            o_ref[...] = jnp.zeros_like(o_ref)

        o_ref[...] += _dot(a_ref[...].astype(BF16), b_ref[...])

    return pl.pallas_call(
        body, name=name, grid=(nk,),
        in_specs=[pl.BlockSpec((8, tk), lambda k: (0, k)), pl.BlockSpec((tk, n), lambda k: (k, 0))],
        out_specs=pl.BlockSpec((8, n), lambda k: (0, 0)),
        out_shape=jax.ShapeDtypeStruct((8, n), F32),
        compiler_params=_params(("arbitrary",)),
    )(a_row, b)


FOX_TQ = 256


def _fox_fwd(proj, c3, gq, gk, nb, seq):
    t = nb * seq
    tq = FOX_TQ
    nq = seq // tq
    npair = N_FOX_HEADS // 2

    def body(q_ref, k_ref, v_ref, c_ref, gq_ref, gk_ref, o_ref, lse_ref, qs, ks, vs):
        ones = _group_ones()
        masks = _head_masks()
        qhat, _ = _head_norm(q_ref[...].astype(F32), None, ones)
        khat, _ = _head_norm(k_ref[...].astype(F32), None, ones)
        qs[...] = (qhat * gq_ref[...] * (SCALE * LOG2E)).astype(BF16)
        kn = khat * gk_ref[...]
        for hd in range(2):
            ks[hd] = (kn * masks[hd]).astype(BF16)
        vs[...] = v_ref[...]
        row = lax.broadcasted_iota(jnp.int32, (tq, tq), 0)
        col = lax.broadcasted_iota(jnp.int32, (tq, tq), 1)
        causal = col <= row

        for qi in range(nq):
            q0 = qi * tq
            q_blk = qs[q0:q0 + tq, :]
            o_tot = jnp.zeros((tq, LANES), F32)
            lse_tot = jnp.zeros((tq, LANES), F32)
            for hd in range(2):
                crow = c_ref[0, hd:hd + 1, 0:q0 + tq] * LOG2E
                c0 = crow[:, q0:q0 + 1]
                s_d = _dot(q_blk, ks[hd, q0:q0 + tq, :], NT) + (c0 - crow[:, q0:q0 + tq])
                s_d = jnp.where(causal, s_d, NEG)
                m = jnp.max(s_d, axis=-1, keepdims=True)
                if qi > 0:
                    s_o = _dot(q_blk, ks[hd, 0:q0, :], NT) + (c0 - crow[:, 0:q0])
                    m = jnp.maximum(m, jnp.max(s_o, axis=-1, keepdims=True))
                p_d = jnp.exp2(s_d - m)
                l = jnp.sum(p_d, axis=-1, keepdims=True)
                acc = _dot(p_d.astype(BF16), vs[q0:q0 + tq, :])
                if qi > 0:
                    p_o = jnp.exp2(s_o - m)
                    l = l + jnp.sum(p_o, axis=-1, keepdims=True)
                    acc = acc + _dot(p_o.astype(BF16), vs[0:q0, :])
                o_tot = o_tot + (acc / l) * masks[hd]
                lse_tot = lse_tot + (m + jnp.log2(l) - c0) * masks[hd]
            o_ref[q0:q0 + tq, :] = o_tot
            lse_ref[q0:q0 + tq, :] = lse_tot

    blk = lambda off: pl.BlockSpec((seq, LANES), lambda b, p: (b, off + p))
    return pl.pallas_call(
        body, name="fox_fwd", grid=(nb, npair),
        in_specs=[blk(0), blk(npair), blk(2 * npair), pl.BlockSpec((1, 2, seq), lambda b, p: (p, 0, b)),
                  pl.BlockSpec((1, LANES), lambda b, p: (0, 0)), pl.BlockSpec((1, LANES), lambda b, p: (0, 0))],
        out_specs=[blk(0), blk(0)],
        out_shape=[jax.ShapeDtypeStruct((t, W_GROUP), F32), jax.ShapeDtypeStruct((t, W_GROUP), F32)],
        scratch_shapes=[pltpu.VMEM((seq, LANES), BF16), pltpu.VMEM((2, seq, LANES), BF16),
                        pltpu.VMEM((seq, LANES), BF16)],
        compiler_params=_params(("arbitrary", "arbitrary")),
    )(proj, proj, proj, c3, gq, gk)


def _fox_bwd(proj, c3, gq, gk, do, o, lse, nb, seq):
    t = nb * seq
    tq = FOX_TQ
    nq = seq // tq
    npair = N_FOX_HEADS // 2

    def body(q_ref, k_ref, v_ref, c_ref, gq_ref, gk_ref, do_ref, o_ref, lse_ref,
             dq_ref, dk_ref, dv_ref, dc_ref, dg_ref, qs, ks, vs, dos, delta, dq_acc, dk_acc, dv_acc, row_sum):
        @pl.when((pl.program_id(0) == 0) & (pl.program_id(1) == 0))
        def _():
            dg_ref[...] = jnp.zeros_like(dg_ref)

        ones = _group_ones()
        masks = _head_masks()
        qhat, rq = _head_norm(q_ref[...].astype(F32), None, ones)
        khat, rk = _head_norm(k_ref[...].astype(F32), None, ones)
        qs[...] = (qhat * gq_ref[...] * (SCALE * LOG2E)).astype(BF16)
        kn = khat * gk_ref[...]
        vv = v_ref[...].astype(F32)
        for hd in range(2):
            ks[hd] = (kn * masks[hd]).astype(BF16)
            vs[hd] = (vv * masks[hd]).astype(BF16)
        dof = do_ref[...]
        dos[...] = dof.astype(BF16)
        delta[...] = _groupsum(dof * o_ref[...], ones)
        dq_acc[...] = jnp.zeros_like(dq_acc)
        dk_acc[...] = jnp.zeros_like(dk_acc)
        dv_acc[...] = jnp.zeros_like(dv_acc)
        row_sum[...] = jnp.zeros_like(row_sum)
        row = lax.broadcasted_iota(jnp.int32, (tq, tq), 0)
        col = lax.broadcasted_iota(jnp.int32, (tq, tq), 1)
        causal = col <= row

        for hd in range(2):
            lane0 = hd * HEAD_DIM
            for kj in range(nq):
                k0 = kj * tq
                k_blk = ks[hd, k0:k0 + tq, :]
                v_blk = vs[hd, k0:k0 + tq, :]
                crow = c_ref[0, hd:hd + 1, k0:k0 + tq] * LOG2E
                ck0 = crow[:, 0:1]
                bias = ck0 - crow

                def rows_step(r0, r1, diag, hd=hd, lane0=lane0, k_blk=k_blk, v_blk=v_blk, bias=bias, ck0=ck0):
                    q_r = qs[r0:r1, :]
                    do_r = dos[r0:r1, :]
                    z = _dot(q_r, k_blk, NT) + bias
                    p = jnp.exp2(z - (lse_ref[r0:r1, lane0:lane0 + 1] + ck0))
                    if diag:
                        p = jnp.where(causal, p, 0.0)
                    dp = _dot(do_r, v_blk, NT)
                    ds = p * (dp - delta[r0:r1, lane0:lane0 + 1])
                    dsb = ds.astype(BF16)
                    dq_acc[r0:r1, :] += _dot(dsb, k_blk)
                    row_sum[r0:r1, :] += jnp.sum(ds, axis=1, keepdims=True) * masks[hd]
                    return _dot(dsb, q_r, TN), _dot(p.astype(BF16), do_r, TN), -jnp.sum(ds, axis=0, keepdims=True)

                dk_j, dv_j, dc_j = rows_step(k0, k0 + tq, True)
                if k0 + tq < seq:
                    dk_o, dv_o, dc_o = rows_step(k0 + tq, seq, False)
                    dk_j, dv_j, dc_j = dk_j + dk_o, dv_j + dv_o, dc_j + dc_o
                dk_acc[k0:k0 + tq, :] += dk_j * masks[hd]
                dv_acc[k0:k0 + tq, :] += dv_j * masks[hd]
                dc_ref[0, hd:hd + 1, k0:k0 + tq] = dc_j

        for qi in range(nq):
            q0 = qi * tq
            sums = row_sum[q0:q0 + tq, :].T
            for hd in range(2):
                dc_ref[0, hd:hd + 1, q0:q0 + tq] += sums[hd * HEAD_DIM:hd * HEAD_DIM + 1, :]

        dq_raw, dgq = _head_norm_bwd(dq_acc[...] * SCALE, qhat, rq, gq_ref[...], ones)
        dk_raw, dgk = _head_norm_bwd(dk_acc[...] * LN2, khat, rk, gk_ref[...], ones)
        dq_ref[...] = dq_raw.astype(BF16)
        dk_ref[...] = dk_raw.astype(BF16)
        dv_ref[...] = dv_acc[...].astype(BF16)
        dg_ref[0:1, :] += dgq
        dg_ref[1:2, :] += dgk

    blk = lambda off: pl.BlockSpec((seq, LANES), lambda b, p: (b, off + p))
    vec = pl.BlockSpec((1, LANES), lambda b, p: (0, 0))
    c_spec = pl.BlockSpec((1, 2, seq), lambda b, p: (p, 0, b))
    return pl.pallas_call(
        body, name="fox_bwd", grid=(nb, npair),
        in_specs=[blk(0), blk(npair), blk(2 * npair), c_spec, vec, vec, blk(0), blk(0), blk(0)],
        out_specs=[blk(0), blk(0), blk(0), c_spec, pl.BlockSpec((8, LANES), lambda b, p: (0, 0))],
        out_shape=[jax.ShapeDtypeStruct((t, W_GROUP), BF16), jax.ShapeDtypeStruct((t, W_GROUP), BF16),
                   jax.ShapeDtypeStruct((t, W_GROUP), BF16), jax.ShapeDtypeStruct((npair, 2, t), F32),
                   jax.ShapeDtypeStruct((8, LANES), F32)],
        scratch_shapes=[pltpu.VMEM((seq, LANES), BF16), pltpu.VMEM((2, seq, LANES), BF16),
                        pltpu.VMEM((2, seq, LANES), BF16), pltpu.VMEM((seq, LANES), BF16),
                        pltpu.VMEM((seq, LANES), F32), pltpu.VMEM((seq, LANES), F32),
                        pltpu.VMEM((seq, LANES), F32), pltpu.VMEM((seq, LANES), F32),
                        pltpu.VMEM((seq, LANES), F32)],
        compiler_params=_params(("arbitrary", "arbitrary")),
    )(proj, proj, proj, c3, gq, gk, do, o, lse)


def _dil_prep(q_ref, k_ref, gq_ref, gk_ref, cos_ref, up_ref, dn_ref, ones):
    qhat, rq = _head_norm(q_ref[...].astype(F32), None, ones)
    khat, rk = _head_norm(k_ref[...].astype(F32), None, ones)
    cos, up, dn = cos_ref[...], up_ref[...], dn_ref[...]
    qn = _rope(qhat * gq_ref[...], cos, up, dn) * (SCALE * LOG2E)
    kn = _rope(khat * gk_ref[...], cos, up, dn)
    return qhat, rq, khat, rk, qn, kn


def _dil_keys(d, seq, kp, vp, kw, vw):
    nblk = seq // BAND
    per_res = seq // (d * BAND)
    as_blocks = lambda ref, rows: ref[rows, :].reshape(-1, BAND, LANES)
    if per_res == 1:
        a = lax.broadcasted_iota(jnp.int32, (1, BAND, BAND), 1)
        j = lax.broadcasted_iota(jnp.int32, (1, BAND, BAND), 2)
        causal = jnp.where(j <= a, 0.0, NEG)
        return as_blocks(kp, slice(0, seq)), as_blocks(vp, slice(0, seq)), [causal]
    for src, dst in ((kp, kw), (vp, vw)):
        dst[:, BAND:, :] = as_blocks(src, slice(0, seq))
        dst[1:, :BAND, :] = as_blocks(src, slice(0, seq - BAND))
        dst[0:1, :BAND, :] = jnp.zeros((1, BAND, LANES), BF16)
    a = lax.broadcasted_iota(jnp.int32, (1, BAND, 2 * BAND), 1)
    j = lax.broadcasted_iota(jnp.int32, (1, BAND, 2 * BAND), 2)
    band = jnp.where(((j < BAND) & (j >= a)) | ((j >= BAND) & (j - BAND <= a)), 0.0, NEG)
    e = lax.broadcasted_iota(jnp.int32, (nblk, 1, 2 * BAND), 0)
    j = lax.broadcasted_iota(jnp.int32, (nblk, 1, 2 * BAND), 2)
    no_prev = jnp.where(((e & (per_res - 1)) == 0) & (j < BAND), NEG, 0.0)
    return kw[...], vw[...], [band, no_prev]


def _residues(d, seq):
    n = seq // d
    if d == 1:
        return [(slice(0, seq), slice(0, seq))]
    return [(pl.ds(r, n, stride=d), slice(r * n, (r + 1) * n)) for r in range(d)]


def _dil_fwd(proj, gq, gk, cos, up, dn, nb, seq):
    t = nb * seq
    npair = W_GROUP // LANES
    off = 3 * npair

    def body(q_ref, k_ref, v_ref, gq_ref, gk_ref, cos_ref, up_ref, dn_ref, o_ref, lse_ref,
             qs, ks, vs, qp, kp, vp, kw, vw, m_b, l_b, o_b, m_s, l_s, o_s):
        ones = _group_ones()
        masks = _head_masks()
        _, _, _, _, qn, kn = _dil_prep(q_ref, k_ref, gq_ref, gk_ref, cos_ref, up_ref, dn_ref, ones)
        qs[...] = qn
        ks[...] = kn
        vs[...] = v_ref[...].astype(F32)
        nblk = seq // BAND

        for d in DILATIONS:
            for tok, res in _residues(d, seq):
                qv = qs[tok, :]
                for hd in range(2):
                    qp[hd, res, :] = (qv * masks[hd]).astype(BF16)
                kp[res, :] = ks[tok, :].astype(BF16)
                vp[res, :] = vs[tok, :].astype(BF16)
            keys_k, keys_v, bias = _dil_keys(d, seq, kp, vp, kw, vw)
            m_t = jnp.zeros((nblk, BAND, LANES), F32)
            l_t = jnp.zeros((nblk, BAND, LANES), F32)
            o_t = jnp.zeros((nblk, BAND, LANES), F32)
            for hd in range(2):
                s = _dot(qp[hd].reshape(nblk, BAND, LANES), keys_k, BATCH_NT)
                for b_ in bias:
                    s = s + b_
                m = jnp.max(s, axis=-1, keepdims=True)
                p = jnp.exp2(s - m)
                m_t = m_t + m * masks[hd]
                l_t = l_t + jnp.sum(p, axis=-1, keepdims=True) * masks[hd]
                o_t = o_t + _dot(p.astype(BF16), keys_v, BATCH_NN) * masks[hd]
            m_b[...] = m_t.reshape(seq, LANES)
            l_b[...] = l_t.reshape(seq, LANES)
            o_b[...] = o_t.reshape(seq, LANES)
            for tok, res in _residues(d, seq):
                if d == DILATIONS[0]:
                    m_s[tok, :] = m_b[res, :]
                    l_s[tok, :] = l_b[res, :]
                    o_s[tok, :] = o_b[res, :]
                else:
                    m_old = m_s[tok, :]
                    m_new = jnp.maximum(m_old, m_b[res, :])
                    w_old = jnp.exp2(m_old - m_new)
                    w_new = jnp.exp2(m_b[res, :] - m_new)
                    l_s[tok, :] = l_s[tok, :] * w_old + l_b[res, :] * w_new
                    o_s[tok, :] = o_s[tok, :] * w_old + o_b[res, :] * w_new
                    m_s[tok, :] = m_new

        l = l_s[...]
        o_ref[...] = o_s[...] / l
        lse_ref[...] = m_s[...] + jnp.log2(l)

    blk = lambda o_: pl.BlockSpec((seq, LANES), lambda b, p: (b, o_ + p))
    vec = pl.BlockSpec((1, LANES), lambda b, p: (0, 0))
    tab = pl.BlockSpec((seq, LANES), lambda b, p: (0, 0))
    f32_buf = pltpu.VMEM((seq, LANES), F32)
    bf16_buf = pltpu.VMEM((seq, LANES), BF16)
    window_buf = pltpu.VMEM((seq // BAND, 2 * BAND, LANES), BF16)
    return pl.pallas_call(
        body, name="dil_fwd", grid=(nb, npair),
        in_specs=[blk(off), blk(off + npair), blk(off + 2 * npair), vec, vec, tab, tab, tab],
        out_specs=[blk(0), blk(0)],
        out_shape=[jax.ShapeDtypeStruct((t, W_GROUP), F32), jax.ShapeDtypeStruct((t, W_GROUP), F32)],
        scratch_shapes=[f32_buf, f32_buf, f32_buf, pltpu.VMEM((2, seq, LANES), BF16), bf16_buf, bf16_buf,
                        window_buf, window_buf, f32_buf, f32_buf, f32_buf, f32_buf, f32_buf, f32_buf],
        compiler_params=_params(("arbitrary", "arbitrary")),
    )(proj, proj, proj, gq, gk, cos, up, dn)


def _dil_bwd(proj, gq, gk, cos, up, dn, do, o, lse, nb, seq):
    t = nb * seq
    npair = W_GROUP // LANES
    off = 3 * npair

    def body(q_ref, k_ref, v_ref, gq_ref, gk_ref, cos_ref, up_ref, dn_ref, do_ref, o_ref, lse_ref,
             dq_ref, dk_ref, dv_ref, dg_ref, qs, ks, vs, delta, dq_s, dk_s, dv_s,
             qp, kp, vp, dop, kw, vw, lse_p, delta_p, dq_p, dk_p, dv_p):
        @pl.when((pl.program_id(0) == 0) & (pl.program_id(1) == 0))
        def _():
            dg_ref[...] = jnp.zeros_like(dg_ref)

        ones = _group_ones()
        masks = _head_masks()
        qhat, rq, khat, rk, qn, kn = _dil_prep(q_ref, k_ref, gq_ref, gk_ref, cos_ref, up_ref, dn_ref, ones)
        qs[...] = qn
        ks[...] = kn
        vs[...] = v_ref[...].astype(F32)
        delta[...] = _groupsum(do_ref[...] * o_ref[...], ones)
        nblk = seq // BAND

        for d in DILATIONS:
            for tok, res in _residues(d, seq):
                qv = qs[tok, :]
                dov = do_ref[tok, :]
                for hd in range(2):
                    qp[hd, res, :] = (qv * masks[hd]).astype(BF16)
                    dop[hd, res, :] = (dov * masks[hd]).astype(BF16)
                kp[res, :] = ks[tok, :].astype(BF16)
                vp[res, :] = vs[tok, :].astype(BF16)
                lse_p[res, :] = lse_ref[tok, :]
                delta_p[res, :] = delta[tok, :]
            keys_k, keys_v, bias = _dil_keys(d, seq, kp, vp, kw, vw)
            nk = keys_k.shape[1]
            dq_b = jnp.zeros((nblk, BAND, LANES), F32)
            dk_b = jnp.zeros((nblk, nk, LANES), F32)
            dv_b = jnp.zeros((nblk, nk, LANES), F32)
            for hd in range(2):
                lane0 = hd * HEAD_DIM
                q3 = qp[hd].reshape(nblk, BAND, LANES)
                do3 = dop[hd].reshape(nblk, BAND, LANES)
                z = _dot(q3, keys_k, BATCH_NT)
                for b_ in bias:
                    z = z + b_
                p = jnp.exp2(z - lse_p[...].reshape(nblk, BAND, LANES)[:, :, lane0:lane0 + 1])
                dp = _dot(do3, keys_v, BATCH_NT)
                ds = (p * (dp - delta_p[...].reshape(nblk, BAND, LANES)[:, :, lane0:lane0 + 1])).astype(BF16)
                dq_b = dq_b + _dot(ds, keys_k, BATCH_NN) * masks[hd]
                dk_b = dk_b + _dot(ds, q3, BATCH_TN)
                dv_b = dv_b + _dot(p.astype(BF16), do3, BATCH_TN)
            dq_p[...] = dq_b.reshape(seq, LANES)
            for acc, out in ((dk_b, dk_p), (dv_b, dv_p)):
                out[...] = acc[:, nk - BAND:, :].reshape(seq, LANES)
                if nk > BAND:
                    out[0:seq - BAND, :] += acc[1:, :BAND, :].reshape(seq - BAND, LANES)
            for tok, res in _residues(d, seq):
                if d == DILATIONS[0]:
                    dq_s[tok, :] = dq_p[res, :]
                    dk_s[tok, :] = dk_p[res, :]
                    dv_s[tok, :] = dv_p[res, :]
                else:
                    dq_s[tok, :] += dq_p[res, :]
                    dk_s[tok, :] += dk_p[res, :]
                    dv_s[tok, :] += dv_p[res, :]

        cos, up, dn = cos_ref[...], up_ref[...], dn_ref[...]
        dq_raw, dgq = _head_norm_bwd(_rope_bwd(dq_s[...] * SCALE, cos, up, dn), qhat, rq, gq_ref[...], ones)
        dk_raw, dgk = _head_norm_bwd(_rope_bwd(dk_s[...] * LN2, cos, up, dn), khat, rk, gk_ref[...], ones)
        dq_ref[...] = dq_raw.astype(BF16)
        dk_ref[...] = dk_raw.astype(BF16)
        dv_ref[...] = dv_s[...].astype(BF16)
        dg_ref[0:1, :] += dgq
        dg_ref[1:2, :] += dgk

    blk = lambda o_: pl.BlockSpec((seq, LANES), lambda b, p: (b, o_ + p))
    vec = pl.BlockSpec((1, LANES), lambda b, p: (0, 0))
    tab = pl.BlockSpec((seq, LANES), lambda b, p: (0, 0))
    f32_buf = pltpu.VMEM((seq, LANES), F32)
    bf16_buf = pltpu.VMEM((seq, LANES), BF16)
    window_buf = pltpu.VMEM((seq // BAND, 2 * BAND, LANES), BF16)
    bf16_pair = pltpu.VMEM((2, seq, LANES), BF16)
    return pl.pallas_call(
        body, name="dil_bwd", grid=(nb, npair),
        in_specs=[blk(off), blk(off + npair), blk(off + 2 * npair), vec, vec, tab, tab, tab,
                  blk(0), blk(0), blk(0)],
        out_specs=[blk(0), blk(0), blk(0), pl.BlockSpec((8, LANES), lambda b, p: (0, 0))],
        out_shape=[jax.ShapeDtypeStruct((t, W_GROUP), BF16), jax.ShapeDtypeStruct((t, W_GROUP), BF16),
                   jax.ShapeDtypeStruct((t, W_GROUP), BF16), jax.ShapeDtypeStruct((8, LANES), F32)],
        scratch_shapes=[f32_buf] * 7 + [bf16_pair, bf16_buf, bf16_buf, bf16_pair, window_buf, window_buf]
        + [f32_buf] * 5,
        compiler_params=_params(("arbitrary", "arbitrary")),
    )(proj, proj, proj, gq, gk, cos, up, dn, do, o, lse)


def _adamw(w, g, m, v, name):
    rows, cols = w.shape[-2:]
    tr = _row_tile(rows) if rows >= 8 else rows
    c1 = 1.0 - ADAM_B1 ** ADAM_STEP
    c2 = 1.0 - ADAM_B2 ** ADAM_STEP

    def body(w_ref, g_ref, m_ref, v_ref, d_ref, nm_ref, nv_ref):
        g_ = g_ref[...]
        nm = ADAM_B1 * m_ref[...] + (1.0 - ADAM_B1) * g_
        nv = ADAM_B2 * v_ref[...] + (1.0 - ADAM_B2) * (g_ * g_)
        nm_ref[...] = nm
        nv_ref[...] = nv
        d_ref[...] = -ADAM_LR * ((nm / c1) / (jnp.sqrt(nv / c2) + ADAM_EPS) + ADAM_WD * w_ref[...])

    if w.ndim == 3:
        spec = pl.BlockSpec((1, tr, cols), lambda i: (0, i, 0))
    else:
        spec = pl.BlockSpec((tr, cols), lambda i: (i, 0))
    shape = jax.ShapeDtypeStruct(w.shape, F32)
    return pl.pallas_call(
        body, name=name, grid=(rows // tr,), in_specs=[spec] * 4, out_specs=[spec] * 3,
        out_shape=[shape] * 3, compiler_params=_params(("arbitrary",)),
    )(w, g, m, v)


def _place():
    x, y, c = lax.axis_index("x"), lax.axis_index("y"), lax.axis_index("c")
    chips = [(1 - x, y), (x, 1 - y), (1 - x, 1 - y)]
    return x, y, c, chips


def _gather_weight(w, name):
    _, rows, cols = w.shape
    half_rows = rows // 2

    def body(w_ref, out_ref, send_sems, recv_sems):
        x, y, c, chips = _place()
        sibling = (x, y, 1 - c)
        mine = 2 * x + y
        lo = pl.multiple_of(c * half_rows, 16)
        lo_sib = pl.multiple_of((1 - c) * half_rows, 16)
        out_ref[mine] = w_ref[0].astype(BF16)

        def copy(k, shard, first_row, to):
            ref = out_ref.at[shard, pl.ds(first_row, half_rows), :]
            return pltpu.make_async_remote_copy(src_ref=ref, dst_ref=ref, send_sem=send_sems.at[k],
                                                recv_sem=recv_sems.at[k], device_id=to, device_id_type=MESH)

        sends = [copy(k, mine, lo, (cx, cy, c)) for k, (cx, cy) in enumerate(chips)]
        for cp in sends:
            cp.start()
        passed = []
        for k, (cx, cy) in enumerate(chips):
            theirs = 2 * cx + cy
            copy(k, theirs, lo, (cx, cy, c)).wait_recv()
            fw = copy(3 + k, theirs, lo, sibling)
            fw.start()
            passed.append(fw)
        for k, (cx, cy) in enumerate(chips):
            copy(3 + k, 2 * cx + cy, lo_sib, sibling).wait_recv()
        for cp in sends + passed:
            cp.wait_send()

    return pl.pallas_call(
        body, name=name,
        in_specs=[pl.BlockSpec(memory_space=pltpu.VMEM)],
        out_specs=pl.BlockSpec(memory_space=pltpu.VMEM),
        out_shape=jax.ShapeDtypeStruct((4, rows, cols), BF16),
        scratch_shapes=[pltpu.SemaphoreType.DMA((6,)), pltpu.SemaphoreType.DMA((6,))],
        compiler_params=pltpu.CompilerParams(vmem_limit_bytes=VMEM_LIMIT),
    )(w)


def _reduce_scatter_weight(g4, name):
    _, rows, cols = g4.shape
    half_rows = rows // 2

    def body(g_ref, out_ref, sib_buf, stage, landed, send_sems, recv_sems):
        x, y, c, chips = _place()
        sibling = (x, y, 1 - c)
        mine = 2 * x + y
        lo = pl.multiple_of(c * half_rows, 8)
        lo_sib = pl.multiple_of((1 - c) * half_rows, 8)

        def copy(k, src, dst, to):
            return pltpu.make_async_remote_copy(src_ref=src, dst_ref=dst, send_sem=send_sems.at[k],
                                                recv_sem=recv_sems.at[k], device_id=to, device_id_type=MESH)

        swap = copy(0, g_ref.at[:, pl.ds(lo_sib, half_rows), :], sib_buf, sibling)
        swap.start()
        swap.wait_recv()
        sends = []
        for k, (cx, cy) in enumerate(chips):
            theirs = 2 * cx + cy
            stage[k] = (g_ref[theirs, pl.ds(lo, half_rows), :] + sib_buf[theirs]).astype(BF16)
            cp = copy(1 + k, stage.at[k], landed.at[k], (cx, cy, c))
            cp.start()
            sends.append(cp)
        acc = g_ref[mine, pl.ds(lo, half_rows), :] + sib_buf[mine]
        for k, (cx, cy) in enumerate(chips):
            copy(1 + k, stage.at[k], landed.at[k], (cx, cy, c)).wait_recv()
            acc = acc + landed[k].astype(F32)
        out_ref[0, pl.ds(lo, half_rows), :] = acc
        own_half = out_ref.at[0, pl.ds(lo, half_rows), :]
        other_half = out_ref.at[0, pl.ds(lo_sib, half_rows), :]
        done = copy(4, own_half, own_half, sibling)
        done.start()
        copy(4, other_half, other_half, sibling).wait_recv()
        for cp in [swap] + sends + [done]:
            cp.wait_send()

    return pl.pallas_call(
        body, name=name,
        in_specs=[pl.BlockSpec(memory_space=pltpu.VMEM)],
        out_specs=pl.BlockSpec(memory_space=pltpu.VMEM),
        out_shape=jax.ShapeDtypeStruct((1, rows, cols), F32),
        scratch_shapes=[pltpu.VMEM((4, half_rows, cols), F32), pltpu.VMEM((3, half_rows, cols), BF16),
                        pltpu.VMEM((3, half_rows, cols), BF16),
                        pltpu.SemaphoreType.DMA((5,)), pltpu.SemaphoreType.DMA((5,))],
        compiler_params=pltpu.CompilerParams(vmem_limit_bytes=VMEM_LIMIT),
    )(g4)


def _all_sum_small(v):
    shape = v.shape

    def body(v_ref, out_ref, buf, send_sems, recv_sems):
        x, y, c, _ = _place()
        me = 4 * x + 2 * y + c
        buf[me] = v_ref[...]
        flips = [(dx, dy, dc) for dx in (0, 1) for dy in (0, 1) for dc in (0, 1)][1:]

        def copy(k, slot, flip):
            dx, dy, dc = flip
            to = (1 - x if dx else x, 1 - y if dy else y, 1 - c if dc else c)
            return pltpu.make_async_remote_copy(src_ref=buf.at[slot], dst_ref=buf.at[slot], send_sem=send_sems.at[k],
                                                recv_sem=recv_sems.at[k], device_id=to, device_id_type=MESH)

        sends = [copy(k, me, flip) for k, flip in enumerate(flips)]
        for cp in sends:
            cp.start()
        for k, (dx, dy, dc) in enumerate(flips):
            sender = 4 * (1 - x if dx else x) + 2 * (1 - y if dy else y) + (1 - c if dc else c)
            copy(k, sender, (dx, dy, dc)).wait_recv()
        for cp in sends:
            cp.wait_send()
        total = buf[0]
        for i in range(1, 8):
            total = total + buf[i]
        out_ref[...] = total

    return pl.pallas_call(
        body, name="all_sum_small",
        in_specs=[pl.BlockSpec(memory_space=pltpu.VMEM)],
        out_specs=pl.BlockSpec(memory_space=pltpu.VMEM),
        out_shape=jax.ShapeDtypeStruct(shape, F32),
        scratch_shapes=[pltpu.VMEM((8,) + shape, F32), pltpu.SemaphoreType.DMA((7,)), pltpu.SemaphoreType.DMA((7,))],
    )(v)


SMALL = (("g_mix", 1024), ("g_ffn", 1024), ("g_out_fox", 512), ("g_out_dil", 512), ("g_q_fox", 64),
         ("g_k_fox", 64), ("g_q_dil", 64), ("g_k_dil", 64), ("b_forget", 8))
SMALL_PACKED = (32, LANES)


def _local_grads(x, target, gains, w1, wft, w_out, w_gate, w_up, w_down, nb, seq):
    tile2 = lambda g: jnp.tile(g, (1, 2))
    gq_f, gk_f, gq_d, gk_d = (tile2(gains[n]) for n in ("g_q_fox", "g_k_fox", "g_q_dil", "g_k_dil"))
    b_col = gains["b_forget"].reshape(N_FOX_HEADS, 1)
    cos, up, dn = _rope_tables(seq)
    npair = N_FOX_HEADS // 2

    proj, fa_row, h1, h1_t = _in_proj(x, gains["g_mix"], w1, wft)
    c_row = _gate_fwd(fa_row, b_col, seq)
    c3 = c_row.reshape(npair, 2, nb * seq)
    o_fox, lse_fox = _fox_fwd(proj, c3, gq_f, gk_f, nb, seq)
    o_dil, lse_dil = _dil_fwd(proj, gq_d, gk_d, cos, up, dn, nb, seq)
    x1, o_n_t = _attn_out(o_fox, o_dil, x, gains["g_out_fox"], gains["g_out_dil"], w_out)
    a, u, dy, loss_parts = _ffn_fwd(x1, target, gains["g_ffn"], w_gate, w_up, w_down)
    loss = jnp.sum(loss_parts[:, 0, 0])

    dx1, s_t, da, du, h2_t, dg_ffn = _ffn_bwd(dy, a, u, x1, gains["g_ffn"], w_gate, w_up, w_down)
    d_w_down = _token_matmul(s_t, dy, "dw_down", 512)
    d_w_gate = _token_matmul(h2_t, da, "dw_gate", w_gate.shape[1] // 2)
    d_w_up = _token_matmul(h2_t, du, "dw_up", w_up.shape[1] // 2)
    d_w_out = _token_matmul(o_n_t, dx1, "dw_out", 1024)
    do_fox, do_dil, dg_of, dg_od = _attn_out_bwd(dx1, o_fox, o_dil, gains["g_out_fox"], gains["g_out_dil"], w_out)
    dq_f, dk_f, dv_f, dc3, dg_fox = _fox_bwd(proj, c3, gq_f, gk_f, do_fox, o_fox, lse_fox, nb, seq)
    dq_d, dk_d, dv_d, dg_dil = _dil_bwd(proj, gq_d, gk_d, cos, up, dn, do_dil, o_dil, lse_dil, nb, seq)
    dfa_row, db = _gate_bwd(dc3.reshape(N_FOX_HEADS, nb * seq), fa_row, b_col, seq)
    dparts = [dq_f, dk_f, dv_f, dq_d, dk_d, dv_d]
    grad_x, dg_mix = _in_proj_bwd(dparts, dfa_row, w1, wft, x, gains["g_mix"], dx1)
    d_w1 = jnp.concatenate([_token_matmul(h1_t, dp, "dw_in_%d" % j, W_GROUP) for j, dp in enumerate(dparts)], axis=1)
    d_wf = _row_matmul(dfa_row, h1, "dw_forget")

    fold = lambda g2: (g2[:, :HEAD_DIM] + g2[:, HEAD_DIM:])
    small = {
        "g_mix": dg_mix[0:1], "g_ffn": dg_ffn[0:1], "g_out_fox": dg_of[0:1], "g_out_dil": dg_od[0:1],
        "g_q_fox": fold(dg_fox[0:1]), "g_k_fox": fold(dg_fox[1:2]),
        "g_q_dil": fold(dg_dil[0:1]), "g_k_dil": fold(dg_dil[1:2]),
        "b_forget": db[:, 0].reshape(1, N_FOX_HEADS),
    }
    big = {"w1": d_w1, "wf": d_wf, "w_out": d_w_out, "w_gate": d_w_gate, "w_up": d_w_up, "w_down": d_w_down}
    return loss, grad_x, big, small


def _shards_of_columns(full, n=4):
    r, nc = full.shape
    return full.reshape(r, n, nc // n).transpose(1, 0, 2)


def _columns_of_shards(slabs):
    n, r, c = slabs.shape
    return slabs.transpose(1, 0, 2).reshape(r, n * c)


def kernel(x, g_mix, w_in, b_forget, g_q_fox, g_k_fox, g_q_dil, g_k_dil, g_out_fox, g_out_dil, w_out, g_ffn, w_gate, w_up, w_down, loss_target, m_g_mix, m_w_in, m_b_forget, m_g_q_fox, m_g_k_fox, m_g_q_dil, m_g_k_dil, m_g_out_fox, m_g_out_dil, m_w_out, m_g_ffn, m_w_gate, m_w_up, m_w_down, v_g_mix, v_w_in, v_b_forget, v_g_q_fox, v_g_k_fox, v_g_q_dil, v_g_k_dil, v_g_out_fox, v_g_out_dil, v_w_out, v_g_ffn, v_w_gate, v_w_up, v_w_down):
    nb, seq, d = x.shape
    weights = dict(g_mix=g_mix, w_in=w_in, b_forget=b_forget, g_q_fox=g_q_fox, g_k_fox=g_k_fox, g_q_dil=g_q_dil,
                   g_k_dil=g_k_dil, g_out_fox=g_out_fox, g_out_dil=g_out_dil, w_out=w_out, g_ffn=g_ffn,
                   w_gate=w_gate, w_up=w_up, w_down=w_down)
    m_in = dict(g_mix=m_g_mix, w_in=m_w_in, b_forget=m_b_forget, g_q_fox=m_g_q_fox, g_k_fox=m_g_k_fox,
                g_q_dil=m_g_q_dil, g_k_dil=m_g_k_dil, g_out_fox=m_g_out_fox, g_out_dil=m_g_out_dil, w_out=m_w_out,
                g_ffn=m_g_ffn, w_gate=m_w_gate, w_up=m_w_up, w_down=m_w_down)
    v_in = dict(g_mix=v_g_mix, w_in=v_w_in, b_forget=v_b_forget, g_q_fox=v_g_q_fox, g_k_fox=v_g_k_fox,
                g_q_dil=v_g_q_dil, g_k_dil=v_g_k_dil, g_out_fox=v_g_out_fox, g_out_dil=v_g_out_dil, w_out=v_w_out,
                g_ffn=v_g_ffn, w_gate=v_w_gate, w_up=v_w_up, w_down=v_w_down)
    order = ["g_mix", "w_in", "b_forget", "g_q_fox", "g_k_fox", "g_q_dil", "g_k_dil", "g_out_fox", "g_out_dil",
             "w_out", "g_ffn", "w_gate", "w_up", "w_down"]

    w_in_full = _columns_of_shards(_gather_weight(w_in, "gather_w_in"))
    fox_w = 3 * W_GROUP
    w1 = jnp.concatenate([w_in_full[:, :fox_w], w_in_full[:, fox_w + N_FOX_HEADS:]], axis=1)
    wft = w_in_full[:, fox_w:fox_w + N_FOX_HEADS].T
    w_out_full = _gather_weight(w_out, "gather_w_out").reshape(d, d)
    w_gate_full = _columns_of_shards(_gather_weight(w_gate, "gather_w_gate"))
    w_up_full = _columns_of_shards(_gather_weight(w_up, "gather_w_up"))
    w_down_full = _gather_weight(w_down, "gather_w_down").reshape(-1, d)

    gains = {n: weights[n] for n, _ in SMALL}
    loss, grad_x, big, small = _local_grads(
        x.reshape(nb * seq, d), loss_target.reshape(nb * seq, d), gains,
        w1, wft, w_out_full, w_gate_full, w_up_full, w_down_full, nb, seq)
    loss = lax.psum(loss, ("x", "y", "c"))

    d_w_in_full = jnp.concatenate([big["w1"][:, :fox_w], big["wf"].T, big["w1"][:, fox_w:]], axis=1)
    grads = {
        "w_in": _reduce_scatter_weight(_shards_of_columns(d_w_in_full), "reduce_w_in"),
        "w_out": _reduce_scatter_weight(big["w_out"].reshape(4, d // 4, d), "reduce_w_out"),
        "w_gate": _reduce_scatter_weight(_shards_of_columns(big["w_gate"]), "reduce_w_gate"),
        "w_up": _reduce_scatter_weight(_shards_of_columns(big["w_up"]), "reduce_w_up"),
        "w_down": _reduce_scatter_weight(big["w_down"].reshape(4, -1, d), "reduce_w_down"),
    }
    packed = jnp.concatenate([small[n].reshape(-1) for n, _ in SMALL])
    packed = jnp.pad(packed, (0, SMALL_PACKED[0] * SMALL_PACKED[1] - packed.shape[0])).reshape(SMALL_PACKED)
    summed = _all_sum_small(packed).reshape(-1)
    pos = 0
    for n, size in SMALL:
        grads[n] = summed[pos:pos + size].reshape(1, size)
        pos += size

    deltas, new_m, new_v, grad_out = {}, {}, {}, {}
    for n in order:
        grad_out[n] = grads[n]
        deltas[n], new_m[n], new_v[n] = _adamw(weights[n], grads[n], m_in[n], v_in[n], "adamw_" + n)

    return (loss, grad_x.reshape(nb, seq, d), *[grad_out[n] for n in order], *[deltas[n] for n in order],
            *[new_m[n] for n in order], *[new_v[n] for n in order])
```

```python
import functools
import math

import numpy as np
import jax
import jax.numpy as jnp
from jax import lax
from jax.experimental import pallas as pl
from jax.experimental.pallas import tpu as pltpu

F32, BF16 = jnp.float32, jnp.bfloat16
MESH = pl.DeviceIdType.MESH

EPS = 1e-6
NEG = -1e30
HEAD_DIM = 64
SCALE = HEAD_DIM ** -0.5
LOG2E = math.log2(math.e)
LN2 = math.log(2.0)
ROPE_THETA = 500000.0
ROPE_DIM = HEAD_DIM // 4
LANES = 128
W_GROUP = 512
N_FOX_HEADS = 8
VMEM_LIMIT = 56 * 1024 * 1024
DILATIONS = (1, 4, 16)
BAND = 128

ADAM_LR, ADAM_B1, ADAM_B2, ADAM_EPS, ADAM_WD, ADAM_STEP = 0.001, 0.9, 0.999, 1e-08, 0.01, 10

NT = (((1,), (1,)), ((), ()))
TN = (((0,), (0,)), ((), ()))
BATCH_NT = (((2,), (2,)), ((0,), (0,)))
BATCH_NN = (((2,), (1,)), ((0,), (0,)))
BATCH_TN = (((1,), (1,)), ((0,), (0,)))


def _params(sem=None):
    return pltpu.CompilerParams(dimension_semantics=sem, vmem_limit_bytes=VMEM_LIMIT)


def _dot(a, b, dims=None):
    if dims is None:
        return jnp.dot(a, b, preferred_element_type=F32)
    return lax.dot_general(a, b, dims, preferred_element_type=F32)


def _group_ones():
    i = lax.broadcasted_iota(jnp.int32, (LANES, LANES), 0) >> 6
    j = lax.broadcasted_iota(jnp.int32, (LANES, LANES), 1) >> 6
    return (i == j).astype(BF16)


def _split3(x):
    a = x.astype(BF16)
    r = x - a.astype(F32)
    b = r.astype(BF16)
    c = (r - b.astype(F32)).astype(BF16)
    return a, b, c


def _groupsum(x, ones):
    a, b, c = _split3(x)
    return _dot(a, ones) + _dot(b, ones) + _dot(c, ones)


def _head_masks():
    lane = lax.broadcasted_iota(jnp.int32, (1, LANES), 1)
    return [(lane < HEAD_DIM).astype(F32), (lane >= HEAD_DIM).astype(F32)]


def _head_norm(raw, gain, ones):
    r = lax.rsqrt(_groupsum(raw * raw, ones) * (1.0 / HEAD_DIM) + EPS)
    return raw * r, r


def _head_norm_bwd(dy, xhat, r, gain, ones):
    u = dy * gain
    dgain = jnp.sum(dy * xhat, axis=0, keepdims=True)
    draw = r * (u - xhat * (_groupsum(u * xhat, ones) * (1.0 / HEAD_DIM)))
    return draw, dgain


def _rope(x, cos, s_up, s_dn):
    return x * cos + pltpu.roll(x, LANES - 8, 1) * s_up + pltpu.roll(x, 8, 1) * s_dn


def _rope_bwd(dy, cos, s_up, s_dn):
    return dy * cos + pltpu.roll(dy * s_up, 8, 1) + pltpu.roll(dy * s_dn, LANES - 8, 1)


def _rope_tables(seq):
    half = ROPE_DIM // 2
    inv_freq = jnp.power(jnp.float32(ROPE_THETA), -jnp.arange(half, dtype=F32) * 2.0 / ROPE_DIM)
    ang = jnp.arange(seq).astype(F32)[:, None] * inv_freq[None, :]
    cos, sin = jnp.cos(ang), jnp.sin(ang)
    one = jnp.ones((seq, HEAD_DIM - ROPE_DIM), F32)
    zero_h = jnp.zeros((seq, half), F32)
    zero_r = jnp.zeros((seq, HEAD_DIM - ROPE_DIM), F32)
    c = jnp.concatenate([cos, cos, one], axis=1)
    up = jnp.concatenate([-sin, zero_h, zero_r], axis=1)
    dn = jnp.concatenate([zero_h, sin, zero_r], axis=1)
    return jnp.tile(c, (1, 2)), jnp.tile(up, (1, 2)), jnp.tile(dn, (1, 2))


def _row_tile(rows, cap=256):
    best = rows
    for t in range(8, min(rows, cap) + 1, 8):
        if rows % t == 0:
            best = t
    return best


class _Rider:
    def __init__(self, inputs, out_shapes, n_sems, start, finish, aliases=None):
        self.inputs, self.out_shapes, self.n_sems = list(inputs), list(out_shapes), n_sems
        self.start, self.finish, self.aliases = start, finish, dict(aliases or {})


def _host_call(body, rider, *, name, grid, in_specs, out_specs, out_shape, scratch_shapes, inputs, semantics):
    if rider is None:
        return pl.pallas_call(body, name=name, grid=grid, in_specs=in_specs, out_specs=out_specs,
                              out_shape=out_shape, scratch_shapes=scratch_shapes,
                              compiler_params=_params(semantics))(*inputs), []
    n_in, n_out, n_scr = len(in_specs), len(out_specs), len(scratch_shapes)
    r_in, r_out = len(rider.inputs), len(rider.out_shapes)

    def wrapped(*refs):
        ins, refs = refs[:n_in], refs[n_in:]
        r_ins, refs = refs[:r_in], refs[r_in:]
        outs, refs = refs[:n_out], refs[n_out:]
        r_outs, refs = refs[:r_out], refs[r_out:]
        scratch, (send_sems, recv_sems) = refs[:n_scr], refs[n_scr:]
        ids = [pl.program_id(a) for a in range(len(grid))]
        first = functools.reduce(lambda p, q: p & q, [i == 0 for i in ids])
        last = functools.reduce(lambda p, q: p & q, [i == g - 1 for i, g in zip(ids, grid)])

        @pl.when(first)
        def _():
            rider.start(r_ins, r_outs, send_sems, recv_sems)

        body(*ins, *outs, *scratch)

        @pl.when(last)
        def _():
            rider.finish(r_ins, r_outs, send_sems, recv_sems)

    hbm = pl.BlockSpec(memory_space=pl.ANY)
    res = pl.pallas_call(
        wrapped, name=name, grid=grid,
        in_specs=list(in_specs) + [hbm] * r_in, out_specs=list(out_specs) + [hbm] * r_out,
        out_shape=list(out_shape) + rider.out_shapes,
        scratch_shapes=list(scratch_shapes) + [pltpu.SemaphoreType.DMA((rider.n_sems,))] * 2,
        input_output_aliases={n_in + i: n_out + o for i, o in rider.aliases.items()},
        compiler_params=_params(semantics),
    )(*inputs, *rider.inputs)
    return res[:n_out], res[n_out:]


def _in_proj(x, g_mix, w1, wft):
    t, d = x.shape
    n = w1.shape[1]
    tt = 512

    def body(x_ref, g_ref, w_ref, wf_ref, p_ref, fa_ref, h_ref, ht_ref):
        xx = x_ref[...]
        r = lax.rsqrt(jnp.mean(xx * xx, axis=-1, keepdims=True) + EPS)
        h = (xx * r * g_ref[...]).astype(BF16)
        h_ref[...] = h
        ht_ref[...] = h.T
        for j in range(n // W_GROUP):
            cols = slice(j * W_GROUP, (j + 1) * W_GROUP)
            p_ref[:, cols] = _dot(h, w_ref[:, cols]).astype(BF16)
        fa_ref[...] = _dot(wf_ref[...], h, NT)

    return pl.pallas_call(
        body, name="in_proj", grid=(t // tt,),
        in_specs=[pl.BlockSpec((tt, d), lambda i: (i, 0)), pl.BlockSpec((1, d), lambda i: (0, 0)),
                  pl.BlockSpec(memory_space=pltpu.VMEM), pl.BlockSpec(memory_space=pltpu.VMEM)],
        out_specs=[pl.BlockSpec((tt, n), lambda i: (i, 0)), pl.BlockSpec((8, tt), lambda i: (0, i)),
                   pl.BlockSpec((tt, d), lambda i: (i, 0)), pl.BlockSpec((d, tt), lambda i: (0, i))],
        out_shape=[jax.ShapeDtypeStruct((t, n), BF16), jax.ShapeDtypeStruct((8, t), F32),
                   jax.ShapeDtypeStruct((t, d), BF16), jax.ShapeDtypeStruct((d, t), BF16)],
        compiler_params=_params(("arbitrary",)),
    )(x, g_mix, w1, wft)


def _tri(n, upper):
    i = lax.broadcasted_iota(jnp.int32, (n, n), 0)
    j = lax.broadcasted_iota(jnp.int32, (n, n), 1)
    return ((i <= j) if upper else (i >= j)).astype(BF16)


def _gate_fwd(fa_row, b_col, seq):
    t = fa_row.shape[1]
    cb = 256

    def body(fa_ref, b_ref, c_ref):
        tri = _tri(cb, True)
        carry = jnp.zeros((8, 1), F32)
        for k in range(seq // cb):
            z = fa_ref[:, k * cb:(k + 1) * cb] + b_ref[...]
            lf = jnp.minimum(z, 0.0) - jnp.log(1.0 + jnp.exp(-jnp.abs(z)))
            a, b, c = _split3(lf)
            blk = _dot(a, tri) + _dot(b, tri) + _dot(c, tri) + carry
            c_ref[:, k * cb:(k + 1) * cb] = blk
            carry = blk[:, cb - 1:cb]

    return pl.pallas_call(
        body, name="gate_fwd", grid=(t // seq,),
        in_specs=[pl.BlockSpec((8, seq), lambda i: (0, i)), pl.BlockSpec((8, 1), lambda i: (0, 0))],
        out_specs=pl.BlockSpec((8, seq), lambda i: (0, i)),
        out_shape=jax.ShapeDtypeStruct((8, t), F32),
        compiler_params=_params(("arbitrary",)),
    )(fa_row, b_col)


def _gate_bwd(dc_row, fa_row, b_col, seq):
    t = fa_row.shape[1]
    cb = 256

    def body(dc_ref, fa_ref, b_ref, dfa_ref, db_ref):
        @pl.when(pl.program_id(0) == 0)
        def _():
            db_ref[...] = jnp.zeros_like(db_ref)

        tri = _tri(cb, False)
        carry = jnp.zeros((8, 1), F32)
        dbs = jnp.zeros((8, 1), F32)
        for k in reversed(range(seq // cb)):
            a, b, c = _split3(dc_ref[:, k * cb:(k + 1) * cb])
            dlf = _dot(a, tri) + _dot(b, tri) + _dot(c, tri) + carry
            carry = dlf[:, 0:1]
            z = fa_ref[:, k * cb:(k + 1) * cb] + b_ref[...]
            dfa = dlf / (1.0 + jnp.exp(z))
            dfa_ref[:, k * cb:(k + 1) * cb] = dfa
            dbs = dbs + jnp.sum(dfa, axis=1, keepdims=True)
        db_ref[...] += jnp.broadcast_to(dbs, (8, LANES))

    return pl.pallas_call(
        body, name="gate_bwd", grid=(t // seq,),
        in_specs=[pl.BlockSpec((8, seq), lambda i: (0, i)), pl.BlockSpec((8, seq), lambda i: (0, i)),
                  pl.BlockSpec((8, 1), lambda i: (0, 0))],
        out_specs=[pl.BlockSpec((8, seq), lambda i: (0, i)), pl.BlockSpec((8, LANES), lambda i: (0, 0))],
        out_shape=[jax.ShapeDtypeStruct((8, t), F32), jax.ShapeDtypeStruct((8, LANES), F32)],
        compiler_params=_params(("arbitrary",)),
    )(dc_row, fa_row, b_col)


def _attn_out(o_fox, o_dil, x, g_fox, g_dil, w_out):
    t, d = x.shape
    w = o_fox.shape[1]
    tt = 512

    def body(of_ref, od_ref, x_ref, gf_ref, gd_ref, w_ref, x1_ref, ont_ref):
        acc = x_ref[...]
        for k, (o_ref, g_ref) in enumerate(((of_ref, gf_ref), (od_ref, gd_ref))):
            o = o_ref[...]
            r = lax.rsqrt(jnp.mean(o * o, axis=-1, keepdims=True) + EPS)
            on = (o * r * g_ref[...]).astype(BF16)
            ont_ref[k * w:(k + 1) * w, :] = on.T
            acc = acc + _dot(on, w_ref[k * w:(k + 1) * w, :])
        x1_ref[...] = acc

    return pl.pallas_call(
        body, name="attn_out", grid=(t // tt,),
        in_specs=[pl.BlockSpec((tt, w), lambda i: (i, 0)), pl.BlockSpec((tt, w), lambda i: (i, 0)),
                  pl.BlockSpec((tt, d), lambda i: (i, 0)), pl.BlockSpec((1, w), lambda i: (0, 0)),
                  pl.BlockSpec((1, w), lambda i: (0, 0)), pl.BlockSpec(memory_space=pltpu.VMEM)],
        out_specs=[pl.BlockSpec((tt, d), lambda i: (i, 0)), pl.BlockSpec((2 * w, tt), lambda i: (0, i))],
        out_shape=[jax.ShapeDtypeStruct((t, d), F32), jax.ShapeDtypeStruct((2 * w, t), BF16)],
        compiler_params=_params(("arbitrary",)),
    )(o_fox, o_dil, x, g_fox, g_dil, w_out)


def _attn_out_bwd(dx1, o_fox, o_dil, g_fox, g_dil, w_out, rider=None):
    t, d = dx1.shape
    w = o_fox.shape[1]
    tt = 512

    def body(dx_ref, of_ref, od_ref, gf_ref, gd_ref, w_ref, dof_ref, dod_ref, dgf_ref, dgd_ref):
        @pl.when(pl.program_id(0) == 0)
        def _():
            dgf_ref[...] = jnp.zeros_like(dgf_ref)
            dgd_ref[...] = jnp.zeros_like(dgd_ref)

        dxb = dx_ref[...].astype(BF16)
        for k, (o_ref, g_ref, do_ref, dg_ref) in enumerate(
                ((of_ref, gf_ref, dof_ref, dgf_ref), (od_ref, gd_ref, dod_ref, dgd_ref))):
            don = _dot(dxb, w_ref[k * w:(k + 1) * w, :], NT)
            o = o_ref[...]
            r = lax.rsqrt(jnp.mean(o * o, axis=-1, keepdims=True) + EPS)
            xhat = o * r
            u = don * g_ref[...]
            do_ref[...] = r * (u - xhat * jnp.mean(u * xhat, axis=-1, keepdims=True))
            dg_ref[0:1, :] += jnp.sum(don * xhat, axis=0, keepdims=True)

    return _host_call(
        body, rider, name="attn_out_bwd", grid=(t // tt,),
        in_specs=[pl.BlockSpec((tt, d), lambda i: (i, 0)), pl.BlockSpec((tt, w), lambda i: (i, 0)),
                  pl.BlockSpec((tt, w), lambda i: (i, 0)), pl.BlockSpec((1, w), lambda i: (0, 0)),
                  pl.BlockSpec((1, w), lambda i: (0, 0)), pl.BlockSpec(memory_space=pltpu.VMEM)],
        out_specs=[pl.BlockSpec((tt, w), lambda i: (i, 0)), pl.BlockSpec((tt, w), lambda i: (i, 0)),
                   pl.BlockSpec((8, w), lambda i: (0, 0)), pl.BlockSpec((8, w), lambda i: (0, 0))],
        out_shape=[jax.ShapeDtypeStruct((t, w), F32), jax.ShapeDtypeStruct((t, w), F32),
                   jax.ShapeDtypeStruct((8, w), F32), jax.ShapeDtypeStruct((8, w), F32)],
        scratch_shapes=[], inputs=(dx1, o_fox, o_dil, g_fox, g_dil, w_out), semantics=("arbitrary",))


def _ffn_fwd(x1, target, g_ffn, w_gate, w_up, w_down):
    t, d = x1.shape
    f = w_gate.shape[0]
    tt = 256

    def body(x_ref, t_ref, g_ref, wg_ref, wu_ref, wd_ref, a_ref, u_ref, dy_ref, loss_ref):
        xx = x_ref[...]
        r = lax.rsqrt(jnp.mean(xx * xx, axis=-1, keepdims=True) + EPS)
        h = (xx * r * g_ref[...]).astype(BF16)
        a = _dot(h, wg_ref[...], NT)
        u = _dot(h, wu_ref[...], NT)
        a_ref[...] = a.astype(BF16)
        u_ref[...] = u.astype(BF16)
        s = (a / (1.0 + jnp.exp(-a)) * u).astype(BF16)
        y = xx + _dot(s, wd_ref[...])
        e = y - t_ref[...]
        dy_ref[...] = e * (1.0 / d)
        loss_ref[...] = jnp.broadcast_to(0.5 * jnp.sum(e * e) * (1.0 / d), (1, 8, LANES))

    return pl.pallas_call(
        body, name="ffn_fwd", grid=(t // tt,),
        in_specs=[pl.BlockSpec((tt, d), lambda i: (i, 0)), pl.BlockSpec((tt, d), lambda i: (i, 0)),
                  pl.BlockSpec((1, d), lambda i: (0, 0)), pl.BlockSpec(memory_space=pltpu.VMEM),
                  pl.BlockSpec(memory_space=pltpu.VMEM), pl.BlockSpec(memory_space=pltpu.VMEM)],
        out_specs=[pl.BlockSpec((tt, f), lambda i: (i, 0)), pl.BlockSpec((tt, f), lambda i: (i, 0)),
                   pl.BlockSpec((tt, d), lambda i: (i, 0)), pl.BlockSpec((1, 8, LANES), lambda i: (i, 0, 0))],
        out_shape=[jax.ShapeDtypeStruct((t, f), BF16), jax.ShapeDtypeStruct((t, f), BF16),
                   jax.ShapeDtypeStruct((t, d), F32), jax.ShapeDtypeStruct((t // tt, 8, LANES), F32)],
        compiler_params=_params(("arbitrary",)),
    )(x1, target, g_ffn, w_gate, w_up, w_down)


def _ffn_bwd(dy, a, u, x1, g_ffn, w_gate, w_up, w_down):
    t, d = x1.shape
    f = w_gate.shape[0]
    tt = 256

    def body(dy_ref, a_ref, u_ref, x_ref, g_ref, wg_ref, wu_ref, wd_ref,
             dx_ref, s_ref, da_ref, du_ref, h_ref, dg_ref):
        @pl.when(pl.program_id(0) == 0)
        def _():
            dg_ref[...] = jnp.zeros_like(dg_ref)

        dy_ = dy_ref[...]
        ds = _dot(dy_.astype(BF16), wd_ref[...], NT)
        a_ = a_ref[...].astype(F32)
        u_ = u_ref[...].astype(F32)
        sig = 1.0 / (1.0 + jnp.exp(-a_))
        silu = a_ * sig
        s_ref[...] = (silu * u_).astype(BF16).T
        da = (ds * u_ * (sig * (1.0 + a_ * (1.0 - sig)))).astype(BF16)
        du = (ds * silu).astype(BF16)
        da_ref[...] = da.T
        du_ref[...] = du.T
        dh = _dot(da, wg_ref[...]) + _dot(du, wu_ref[...])
        xx = x_ref[...]
        r = lax.rsqrt(jnp.mean(xx * xx, axis=-1, keepdims=True) + EPS)
        xhat = xx * r
        g = g_ref[...]
        h_ref[...] = (xhat * g).astype(BF16)
        uu = dh * g
        dx_ref[...] = dy_ + r * (uu - xhat * jnp.mean(uu * xhat, axis=-1, keepdims=True))
        dg_ref[0:1, :] += jnp.sum(dh * xhat, axis=0, keepdims=True)

    return pl.pallas_call(
        body, name="ffn_bwd", grid=(t // tt,),
        in_specs=[pl.BlockSpec((tt, d), lambda i: (i, 0)), pl.BlockSpec((tt, f), lambda i: (i, 0)),
                  pl.BlockSpec((tt, f), lambda i: (i, 0)), pl.BlockSpec((tt, d), lambda i: (i, 0)),
                  pl.BlockSpec((1, d), lambda i: (0, 0)), pl.BlockSpec(memory_space=pltpu.VMEM),
                  pl.BlockSpec(memory_space=pltpu.VMEM), pl.BlockSpec(memory_space=pltpu.VMEM)],
        out_specs=[pl.BlockSpec((tt, d), lambda i: (i, 0)), pl.BlockSpec((f, tt), lambda i: (0, i)),
                   pl.BlockSpec((f, tt), lambda i: (0, i)), pl.BlockSpec((f, tt), lambda i: (0, i)),
                   pl.BlockSpec((tt, d), lambda i: (i, 0)), pl.BlockSpec((8, d), lambda i: (0, 0))],
        out_shape=[jax.ShapeDtypeStruct((t, d), F32), jax.ShapeDtypeStruct((f, t), BF16),
                   jax.ShapeDtypeStruct((f, t), BF16), jax.ShapeDtypeStruct((f, t), BF16),
                   jax.ShapeDtypeStruct((t, d), BF16), jax.ShapeDtypeStruct((8, d), F32)],
        compiler_params=_params(("arbitrary",)),
    )(dy, a, u, x1, g_ffn, w_gate, w_up, w_down)


def _in_proj_bwd(dparts, dfa_row, w1, wft, x, g_mix, dx1):
    t, d = x.shape
    tt = 512
    npart = len(dparts)

    def body(*refs):
        dp_refs = refs[:npart]
        dfa_ref, w_ref, wf_ref, x_ref, g_ref, dx1_ref, dx_ref, dg_ref = refs[npart:]

        @pl.when(pl.program_id(0) == 0)
        def _():
            dg_ref[...] = jnp.zeros_like(dg_ref)

        dh = _dot(dfa_ref[...].astype(BF16), wf_ref[...], TN)
        for j in range(npart):
            dh = dh + _dot(dp_refs[j][...], w_ref[:, j * W_GROUP:(j + 1) * W_GROUP], NT)
        xx = x_ref[...]
        r = lax.rsqrt(jnp.mean(xx * xx, axis=-1, keepdims=True) + EPS)
        xhat = xx * r
        uu = dh * g_ref[...]
        dx_ref[...] = dx1_ref[...] + r * (uu - xhat * jnp.mean(uu * xhat, axis=-1, keepdims=True))
        dg_ref[0:1, :] += jnp.sum(dh * xhat, axis=0, keepdims=True)

    return pl.pallas_call(
        body, name="in_proj_bwd", grid=(t // tt,),
        in_specs=[pl.BlockSpec((tt, W_GROUP), lambda i: (i, 0)) for _ in range(npart)]
        + [pl.BlockSpec((8, tt), lambda i: (0, i)), pl.BlockSpec(memory_space=pltpu.VMEM),
           pl.BlockSpec(memory_space=pltpu.VMEM), pl.BlockSpec((tt, d), lambda i: (i, 0)),
           pl.BlockSpec((1, d), lambda i: (0, 0)), pl.BlockSpec((tt, d), lambda i: (i, 0))],
        out_specs=[pl.BlockSpec((tt, d), lambda i: (i, 0)), pl.BlockSpec((8, d), lambda i: (0, 0))],
        out_shape=[jax.ShapeDtypeStruct((t, d), F32), jax.ShapeDtypeStruct((8, d), F32)],
        compiler_params=_params(("arbitrary",)),
    )(*dparts, dfa_row, w1, wft, x, g_mix, dx1)


def _token_matmul(at, b, name, tn):
    m, t = at.shape
    n = b.shape[1]
    tk = 1024

    def body(a_ref, b_ref, o_ref):
        @pl.when(pl.program_id(1) == 0)
        def _():
            o_ref[...] = jnp.zeros_like(o_ref)

        o_ref[...] += _dot(a_ref[...], b_ref[...].astype(BF16))

    return pl.pallas_call(
        body, name=name, grid=(n // tn, t // tk),
        in_specs=[pl.BlockSpec((m, tk), lambda j, k: (0, k)), pl.BlockSpec((tk, tn), lambda j, k: (k, j))],
        out_specs=pl.BlockSpec((m, tn), lambda j, k: (0, j)),
        out_shape=jax.ShapeDtypeStruct((m, n), F32),
        compiler_params=_params(("arbitrary", "arbitrary")),
    )(at, b)


def _row_matmul(a_row, b, name):
    t, n = b.shape
    tk = 1024
    nk = t // tk

    def body(a_ref, b_ref, o_ref):
        @pl.when(pl.program_id(0) == 0)
        def _():
            o_ref[...] = jnp.zeros_like(o_ref)

        o_ref[...] += _dot(a_ref[...].astype(BF16), b_ref[...])

    return pl.pallas_call(
        body, name=name, grid=(nk,),
        in_specs=[pl.BlockSpec((8, tk), lambda k: (0, k)), pl.BlockSpec((tk, n), lambda k: (k, 0))],
        out_specs=pl.BlockSpec((8, n), lambda k: (0, 0)),
        out_shape=jax.ShapeDtypeStruct((8, n), F32),
        compiler_params=_params(("arbitrary",)),
    )(a_row, b)


FOX_TQ = 256


def _fox_fwd(proj, c3, gq, gk, nb, seq, rider=None):
    t = nb * seq
    tq = FOX_TQ
    nq = seq // tq
    npair = N_FOX_HEADS // 2

    def body(q_ref, k_ref, v_ref, c_ref, gq_ref, gk_ref, o_ref, lse_ref, qs, ks, vs):
        ones = _group_ones()
        masks = _head_masks()
        qhat, _ = _head_norm(q_ref[...].astype(F32), None, ones)
        khat, _ = _head_norm(k_ref[...].astype(F32), None, ones)
        qs[...] = (qhat * gq_ref[...] * (SCALE * LOG2E)).astype(BF16)
        kn = khat * gk_ref[...]
        for hd in range(2):
            ks[hd] = (kn * masks[hd]).astype(BF16)
        vs[...] = v_ref[...]
        row = lax.broadcasted_iota(jnp.int32, (tq, tq), 0)
        col = lax.broadcasted_iota(jnp.int32, (tq, tq), 1)
        causal = col <= row

        for qi in range(nq):
            q0 = qi * tq
            q_blk = qs[q0:q0 + tq, :]
            o_tot = jnp.zeros((tq, LANES), F32)
            lse_tot = jnp.zeros((tq, LANES), F32)
            for hd in range(2):
                crow = c_ref[0, hd:hd + 1, 0:q0 + tq] * LOG2E
                c0 = crow[:, q0:q0 + 1]
                s_d = _dot(q_blk, ks[hd, q0:q0 + tq, :], NT) + (c0 - crow[:, q0:q0 + tq])
                s_d = jnp.where(causal, s_d, NEG)
                m = jnp.max(s_d, axis=-1, keepdims=True)
                if qi > 0:
                    s_o = _dot(q_blk, ks[hd, 0:q0, :], NT) + (c0 - crow[:, 0:q0])
                    m = jnp.maximum(m, jnp.max(s_o, axis=-1, keepdims=True))
                p_d = jnp.exp2(s_d - m)
                l = jnp.sum(p_d, axis=-1, keepdims=True)
                acc = _dot(p_d.astype(BF16), vs[q0:q0 + tq, :])
                if qi > 0:
                    p_o = jnp.exp2(s_o - m)
                    l = l + jnp.sum(p_o, axis=-1, keepdims=True)
                    acc = acc + _dot(p_o.astype(BF16), vs[0:q0, :])
                o_tot = o_tot + (acc / l) * masks[hd]
                lse_tot = lse_tot + (m + jnp.log2(l) - c0) * masks[hd]
            o_ref[q0:q0 + tq, :] = o_tot
            lse_ref[q0:q0 + tq, :] = lse_tot

    blk = lambda off: pl.BlockSpec((seq, LANES), lambda b, p: (b, off + p))
    return _host_call(
        body, rider, name="fox_fwd", grid=(nb, npair),
        in_specs=[blk(0), blk(npair), blk(2 * npair), pl.BlockSpec((1, 2, seq), lambda b, p: (p, 0, b)),
                  pl.BlockSpec((1, LANES), lambda b, p: (0, 0)), pl.BlockSpec((1, LANES), lambda b, p: (0, 0))],
        out_specs=[blk(0), blk(0)],
        out_shape=[jax.ShapeDtypeStruct((t, W_GROUP), F32), jax.ShapeDtypeStruct((t, W_GROUP), F32)],
        scratch_shapes=[pltpu.VMEM((seq, LANES), BF16), pltpu.VMEM((2, seq, LANES), BF16),
                        pltpu.VMEM((seq, LANES), BF16)],
        inputs=(proj, proj, proj, c3, gq, gk), semantics=("arbitrary", "arbitrary"))


def _fox_bwd(proj, c3, gq, gk, do, o, lse, nb, seq, rider=None):
    t = nb * seq
    tq = FOX_TQ
    nq = seq // tq
    npair = N_FOX_HEADS // 2

    def body(q_ref, k_ref, v_ref, c_ref, gq_ref, gk_ref, do_ref, o_ref, lse_ref,
             dq_ref, dk_ref, dv_ref, dc_ref, dg_ref, qs, ks, vs, dos, delta, dq_acc, dk_acc, dv_acc, row_sum):
        @pl.when((pl.program_id(0) == 0) & (pl.program_id(1) == 0))
        def _():
            dg_ref[...] = jnp.zeros_like(dg_ref)

        ones = _group_ones()
        masks = _head_masks()
        qhat, rq = _head_norm(q_ref[...].astype(F32), None, ones)
        khat, rk = _head_norm(k_ref[...].astype(F32), None, ones)
        qs[...] = (qhat * gq_ref[...] * (SCALE * LOG2E)).astype(BF16)
        kn = khat * gk_ref[...]
        vv = v_ref[...].astype(F32)
        for hd in range(2):
            ks[hd] = (kn * masks[hd]).astype(BF16)
            vs[hd] = (vv * masks[hd]).astype(BF16)
        dof = do_ref[...]
        dos[...] = dof.astype(BF16)
        delta[...] = _groupsum(dof * o_ref[...], ones)
        dq_acc[...] = jnp.zeros_like(dq_acc)
        dk_acc[...] = jnp.zeros_like(dk_acc)
        dv_acc[...] = jnp.zeros_like(dv_acc)
        row_sum[...] = jnp.zeros_like(row_sum)
        row = lax.broadcasted_iota(jnp.int32, (tq, tq), 0)
        col = lax.broadcasted_iota(jnp.int32, (tq, tq), 1)
        causal = col <= row

        for hd in range(2):
            lane0 = hd * HEAD_DIM
            for kj in range(nq):
                k0 = kj * tq
                k_blk = ks[hd, k0:k0 + tq, :]
                v_blk = vs[hd, k0:k0 + tq, :]
                crow = c_ref[0, hd:hd + 1, k0:k0 + tq] * LOG2E
                ck0 = crow[:, 0:1]
                bias = ck0 - crow

                def rows_step(r0, r1, diag, hd=hd, lane0=lane0, k_blk=k_blk, v_blk=v_blk, bias=bias, ck0=ck0):
                    q_r = qs[r0:r1, :]
                    do_r = dos[r0:r1, :]
                    z = _dot(q_r, k_blk, NT) + bias
                    p = jnp.exp2(z - (lse_ref[r0:r1, lane0:lane0 + 1] + ck0))
                    if diag:
                        p = jnp.where(causal, p, 0.0)
                    dp = _dot(do_r, v_blk, NT)
                    ds = p * (dp - delta[r0:r1, lane0:lane0 + 1])
                    dsb = ds.astype(BF16)
                    dq_acc[r0:r1, :] += _dot(dsb, k_blk)
                    row_sum[r0:r1, :] += jnp.sum(ds, axis=1, keepdims=True) * masks[hd]
                    return _dot(dsb, q_r, TN), _dot(p.astype(BF16), do_r, TN), -jnp.sum(ds, axis=0, keepdims=True)

                dk_j, dv_j, dc_j = rows_step(k0, k0 + tq, True)
                if k0 + tq < seq:
                    dk_o, dv_o, dc_o = rows_step(k0 + tq, seq, False)
                    dk_j, dv_j, dc_j = dk_j + dk_o, dv_j + dv_o, dc_j + dc_o
                dk_acc[k0:k0 + tq, :] += dk_j * masks[hd]
                dv_acc[k0:k0 + tq, :] += dv_j * masks[hd]
                dc_ref[0, hd:hd + 1, k0:k0 + tq] = dc_j

        for qi in range(nq):
            q0 = qi * tq
            sums = row_sum[q0:q0 + tq, :].T
            for hd in range(2):
                dc_ref[0, hd:hd + 1, q0:q0 + tq] += sums[hd * HEAD_DIM:hd * HEAD_DIM + 1, :]

        dq_raw, dgq = _head_norm_bwd(dq_acc[...] * SCALE, qhat, rq, gq_ref[...], ones)
        dk_raw, dgk = _head_norm_bwd(dk_acc[...] * LN2, khat, rk, gk_ref[...], ones)
        dq_ref[...] = dq_raw.astype(BF16)
        dk_ref[...] = dk_raw.astype(BF16)
        dv_ref[...] = dv_acc[...].astype(BF16)
        dg_ref[0:1, :] += dgq
        dg_ref[1:2, :] += dgk

    blk = lambda off: pl.BlockSpec((seq, LANES), lambda b, p: (b, off + p))
    vec = pl.BlockSpec((1, LANES), lambda b, p: (0, 0))
    c_spec = pl.BlockSpec((1, 2, seq), lambda b, p: (p, 0, b))
    return _host_call(
        body, rider, name="fox_bwd", grid=(nb, npair),
        in_specs=[blk(0), blk(npair), blk(2 * npair), c_spec, vec, vec, blk(0), blk(0), blk(0)],
        out_specs=[blk(0), blk(0), blk(0), c_spec, pl.BlockSpec((8, LANES), lambda b, p: (0, 0))],
        out_shape=[jax.ShapeDtypeStruct((t, W_GROUP), BF16), jax.ShapeDtypeStruct((t, W_GROUP), BF16),
                   jax.ShapeDtypeStruct((t, W_GROUP), BF16), jax.ShapeDtypeStruct((npair, 2, t), F32),
                   jax.ShapeDtypeStruct((8, LANES), F32)],
        scratch_shapes=[pltpu.VMEM((seq, LANES), BF16), pltpu.VMEM((2, seq, LANES), BF16),
                        pltpu.VMEM((2, seq, LANES), BF16), pltpu.VMEM((seq, LANES), BF16),
                        pltpu.VMEM((seq, LANES), F32), pltpu.VMEM((seq, LANES), F32),
                        pltpu.VMEM((seq, LANES), F32), pltpu.VMEM((seq, LANES), F32),
                        pltpu.VMEM((seq, LANES), F32)],
        inputs=(proj, proj, proj, c3, gq, gk, do, o, lse), semantics=("arbitrary", "arbitrary"))


def _dil_prep(q_ref, k_ref, gq_ref, gk_ref, cos_ref, up_ref, dn_ref, ones):
    qhat, rq = _head_norm(q_ref[...].astype(F32), None, ones)
    khat, rk = _head_norm(k_ref[...].astype(F32), None, ones)
    cos, up, dn = cos_ref[...], up_ref[...], dn_ref[...]
    qn = _rope(qhat * gq_ref[...], cos, up, dn) * (SCALE * LOG2E)
    kn = _rope(khat * gk_ref[...], cos, up, dn)
    return qhat, rq, khat, rk, qn, kn


def _dil_keys(d, seq, kp, vp, kw, vw):
    nblk = seq // BAND
    per_res = seq // (d * BAND)
    as_blocks = lambda ref, rows: ref[rows, :].reshape(-1, BAND, LANES)
    if per_res == 1:
        a = lax.broadcasted_iota(jnp.int32, (1, BAND, BAND), 1)
        j = lax.broadcasted_iota(jnp.int32, (1, BAND, BAND), 2)
        causal = jnp.where(j <= a, 0.0, NEG)
        return as_blocks(kp, slice(0, seq)), as_blocks(vp, slice(0, seq)), [causal]
    for src, dst in ((kp, kw), (vp, vw)):
        dst[:, BAND:, :] = as_blocks(src, slice(0, seq))
        dst[1:, :BAND, :] = as_blocks(src, slice(0, seq - BAND))
        dst[0:1, :BAND, :] = jnp.zeros((1, BAND, LANES), BF16)
    a = lax.broadcasted_iota(jnp.int32, (1, BAND, 2 * BAND), 1)
    j = lax.broadcasted_iota(jnp.int32, (1, BAND, 2 * BAND), 2)
    band = jnp.where(((j < BAND) & (j >= a)) | ((j >= BAND) & (j - BAND <= a)), 0.0, NEG)
    e = lax.broadcasted_iota(jnp.int32, (nblk, 1, 2 * BAND), 0)
    j = lax.broadcasted_iota(jnp.int32, (nblk, 1, 2 * BAND), 2)
    no_prev = jnp.where(((e & (per_res - 1)) == 0) & (j < BAND), NEG, 0.0)
    return kw[...], vw[...], [band, no_prev]


def _residues(d, seq):
    n = seq // d
    if d == 1:
        return [(slice(0, seq), slice(0, seq))]
    return [(pl.ds(r, n, stride=d), slice(r * n, (r + 1) * n)) for r in range(d)]


def _dil_fwd(proj, gq, gk, cos, up, dn, nb, seq):
    t = nb * seq
    npair = W_GROUP // LANES
    off = 3 * npair

    def body(q_ref, k_ref, v_ref, gq_ref, gk_ref, cos_ref, up_ref, dn_ref, o_ref, lse_ref,
             qs, ks, vs, qp, kp, vp, kw, vw, m_b, l_b, o_b, m_s, l_s, o_s):
        ones = _group_ones()
        masks = _head_masks()
        _, _, _, _, qn, kn = _dil_prep(q_ref, k_ref, gq_ref, gk_ref, cos_ref, up_ref, dn_ref, ones)
        qs[...] = qn
        ks[...] = kn
        vs[...] = v_ref[...].astype(F32)
        nblk = seq // BAND

        for d in DILATIONS:
            for tok, res in _residues(d, seq):
                qv = qs[tok, :]
                for hd in range(2):
                    qp[hd, res, :] = (qv * masks[hd]).astype(BF16)
                kp[res, :] = ks[tok, :].astype(BF16)
                vp[res, :] = vs[tok, :].astype(BF16)
            keys_k, keys_v, bias = _dil_keys(d, seq, kp, vp, kw, vw)
            m_t = jnp.zeros((nblk, BAND, LANES), F32)
            l_t = jnp.zeros((nblk, BAND, LANES), F32)
            o_t = jnp.zeros((nblk, BAND, LANES), F32)
            for hd in range(2):
                s = _dot(qp[hd].reshape(nblk, BAND, LANES), keys_k, BATCH_NT)
                for b_ in bias:
                    s = s + b_
                m = jnp.max(s, axis=-1, keepdims=True)
                p = jnp.exp2(s - m)
                m_t = m_t + m * masks[hd]
                l_t = l_t + jnp.sum(p, axis=-1, keepdims=True) * masks[hd]
                o_t = o_t + _dot(p.astype(BF16), keys_v, BATCH_NN) * masks[hd]
            m_b[...] = m_t.reshape(seq, LANES)
            l_b[...] = l_t.reshape(seq, LANES)
            o_b[...] = o_t.reshape(seq, LANES)
            for tok, res in _residues(d, seq):
                if d == DILATIONS[0]:
                    m_s[tok, :] = m_b[res, :]
                    l_s[tok, :] = l_b[res, :]
                    o_s[tok, :] = o_b[res, :]
                else:
                    m_old = m_s[tok, :]
                    m_new = jnp.maximum(m_old, m_b[res, :])
                    w_old = jnp.exp2(m_old - m_new)
                    w_new = jnp.exp2(m_b[res, :] - m_new)
                    l_s[tok, :] = l_s[tok, :] * w_old + l_b[res, :] * w_new
                    o_s[tok, :] = o_s[tok, :] * w_old + o_b[res, :] * w_new
                    m_s[tok, :] = m_new

        l = l_s[...]
        o_ref[...] = o_s[...] / l
        lse_ref[...] = m_s[...] + jnp.log2(l)

    blk = lambda o_: pl.BlockSpec((seq, LANES), lambda b, p: (b, o_ + p))
    vec = pl.BlockSpec((1, LANES), lambda b, p: (0, 0))
    tab = pl.BlockSpec((seq, LANES), lambda b, p: (0, 0))
    f32_buf = pltpu.VMEM((seq, LANES), F32)
    bf16_buf = pltpu.VMEM((seq, LANES), BF16)
    window_buf = pltpu.VMEM((seq // BAND, 2 * BAND, LANES), BF16)
    return pl.pallas_call(
        body, name="dil_fwd", grid=(nb, npair),
        in_specs=[blk(off), blk(off + npair), blk(off + 2 * npair), vec, vec, tab, tab, tab],
        out_specs=[blk(0), blk(0)],
        out_shape=[jax.ShapeDtypeStruct((t, W_GROUP), F32), jax.ShapeDtypeStruct((t, W_GROUP), F32)],
        scratch_shapes=[f32_buf, f32_buf, f32_buf, pltpu.VMEM((2, seq, LANES), BF16), bf16_buf, bf16_buf,
                        window_buf, window_buf, f32_buf, f32_buf, f32_buf, f32_buf, f32_buf, f32_buf],
        compiler_params=_params(("arbitrary", "arbitrary")),
    )(proj, proj, proj, gq, gk, cos, up, dn)


def _dil_bwd(proj, gq, gk, cos, up, dn, do, o, lse, nb, seq, rider=None):
    t = nb * seq
    npair = W_GROUP // LANES
    off = 3 * npair

    def body(q_ref, k_ref, v_ref, gq_ref, gk_ref, cos_ref, up_ref, dn_ref, do_ref, o_ref, lse_ref,
             dq_ref, dk_ref, dv_ref, dg_ref, qs, ks, vs, delta, dq_s, dk_s, dv_s,
             qp, kp, vp, dop, kw, vw, lse_p, delta_p, dq_p, dk_p, dv_p):
        @pl.when((pl.program_id(0) == 0) & (pl.program_id(1) == 0))
        def _():
            dg_ref[...] = jnp.zeros_like(dg_ref)

        ones = _group_ones()
        masks = _head_masks()
        qhat, rq, khat, rk, qn, kn = _dil_prep(q_ref, k_ref, gq_ref, gk_ref, cos_ref, up_ref, dn_ref, ones)
        qs[...] = qn
        ks[...] = kn
        vs[...] = v_ref[...].astype(F32)
        delta[...] = _groupsum(do_ref[...] * o_ref[...], ones)
        nblk = seq // BAND

        for d in DILATIONS:
            for tok, res in _residues(d, seq):
                qv = qs[tok, :]
                dov = do_ref[tok, :]
                for hd in range(2):
                    qp[hd, res, :] = (qv * masks[hd]).astype(BF16)
                    dop[hd, res, :] = (dov * masks[hd]).astype(BF16)
                kp[res, :] = ks[tok, :].astype(BF16)
                vp[res, :] = vs[tok, :].astype(BF16)
                lse_p[res, :] = lse_ref[tok, :]
                delta_p[res, :] = delta[tok, :]
            keys_k, keys_v, bias = _dil_keys(d, seq, kp, vp, kw, vw)
            nk = keys_k.shape[1]
            dq_b = jnp.zeros((nblk, BAND, LANES), F32)
            dk_b = jnp.zeros((nblk, nk, LANES), F32)
            dv_b = jnp.zeros((nblk, nk, LANES), F32)
            for hd in range(2):
                lane0 = hd * HEAD_DIM
                q3 = qp[hd].reshape(nblk, BAND, LANES)
                do3 = dop[hd].reshape(nblk, BAND, LANES)
                z = _dot(q3, keys_k, BATCH_NT)
                for b_ in bias:
                    z = z + b_
                p = jnp.exp2(z - lse_p[...].reshape(nblk, BAND, LANES)[:, :, lane0:lane0 + 1])
                dp = _dot(do3, keys_v, BATCH_NT)
                ds = (p * (dp - delta_p[...].reshape(nblk, BAND, LANES)[:, :, lane0:lane0 + 1])).astype(BF16)
                dq_b = dq_b + _dot(ds, keys_k, BATCH_NN) * masks[hd]
                dk_b = dk_b + _dot(ds, q3, BATCH_TN)
                dv_b = dv_b + _dot(p.astype(BF16), do3, BATCH_TN)
            dq_p[...] = dq_b.reshape(seq, LANES)
            for acc, out in ((dk_b, dk_p), (dv_b, dv_p)):
                out[...] = acc[:, nk - BAND:, :].reshape(seq, LANES)
                if nk > BAND:
                    out[0:seq - BAND, :] += acc[1:, :BAND, :].reshape(seq - BAND, LANES)
            for tok, res in _residues(d, seq):
                if d == DILATIONS[0]:
                    dq_s[tok, :] = dq_p[res, :]
                    dk_s[tok, :] = dk_p[res, :]
                    dv_s[tok, :] = dv_p[res, :]
                else:
                    dq_s[tok, :] += dq_p[res, :]
                    dk_s[tok, :] += dk_p[res, :]
                    dv_s[tok, :] += dv_p[res, :]

        cos, up, dn = cos_ref[...], up_ref[...], dn_ref[...]
        dq_raw, dgq = _head_norm_bwd(_rope_bwd(dq_s[...] * SCALE, cos, up, dn), qhat, rq, gq_ref[...], ones)
        dk_raw, dgk = _head_norm_bwd(_rope_bwd(dk_s[...] * LN2, cos, up, dn), khat, rk, gk_ref[...], ones)
        dq_ref[...] = dq_raw.astype(BF16)
        dk_ref[...] = dk_raw.astype(BF16)
        dv_ref[...] = dv_s[...].astype(BF16)
        dg_ref[0:1, :] += dgq
        dg_ref[1:2, :] += dgk

    blk = lambda o_: pl.BlockSpec((seq, LANES), lambda b, p: (b, o_ + p))
    vec = pl.BlockSpec((1, LANES), lambda b, p: (0, 0))
    tab = pl.BlockSpec((seq, LANES), lambda b, p: (0, 0))
    f32_buf = pltpu.VMEM((seq, LANES), F32)
    bf16_buf = pltpu.VMEM((seq, LANES), BF16)
    window_buf = pltpu.VMEM((seq // BAND, 2 * BAND, LANES), BF16)
    bf16_pair = pltpu.VMEM((2, seq, LANES), BF16)
    return _host_call(
        body, rider, name="dil_bwd", grid=(nb, npair),
        in_specs=[blk(off), blk(off + npair), blk(off + 2 * npair), vec, vec, tab, tab, tab,
                  blk(0), blk(0), blk(0)],
        out_specs=[blk(0), blk(0), blk(0), pl.BlockSpec((8, LANES), lambda b, p: (0, 0))],
        out_shape=[jax.ShapeDtypeStruct((t, W_GROUP), BF16), jax.ShapeDtypeStruct((t, W_GROUP), BF16),
                   jax.ShapeDtypeStruct((t, W_GROUP), BF16), jax.ShapeDtypeStruct((8, LANES), F32)],
        scratch_shapes=[f32_buf] * 7 + [bf16_pair, bf16_buf, bf16_buf, bf16_pair, window_buf, window_buf]
        + [f32_buf] * 5,
        inputs=(proj, proj, proj, gq, gk, cos, up, dn, do, o, lse), semantics=("arbitrary", "arbitrary"))


def _adamw(w, g, m, v, name):
    rows, cols = w.shape[-2:]
    tr = _row_tile(rows) if rows >= 8 else rows
    c1 = 1.0 - ADAM_B1 ** ADAM_STEP
    c2 = 1.0 - ADAM_B2 ** ADAM_STEP

    def body(w_ref, g_ref, m_ref, v_ref, d_ref, nm_ref, nv_ref):
        g_ = g_ref[...]
        nm = ADAM_B1 * m_ref[...] + (1.0 - ADAM_B1) * g_
        nv = ADAM_B2 * v_ref[...] + (1.0 - ADAM_B2) * (g_ * g_)
        nm_ref[...] = nm
        nv_ref[...] = nv
        d_ref[...] = -ADAM_LR * ((nm / c1) / (jnp.sqrt(nv / c2) + ADAM_EPS) + ADAM_WD * w_ref[...])

    if w.ndim == 3:
        spec = pl.BlockSpec((1, tr, cols), lambda i: (0, i, 0))
    else:
        spec = pl.BlockSpec((tr, cols), lambda i: (i, 0))
    shape = jax.ShapeDtypeStruct(w.shape, F32)
    return pl.pallas_call(
        body, name=name, grid=(rows // tr,), in_specs=[spec] * 4, out_specs=[spec] * 3,
        out_shape=[shape] * 3, compiler_params=_params(("arbitrary",)),
    )(w, g, m, v)


def _place():
    x, y, c = lax.axis_index("x"), lax.axis_index("y"), lax.axis_index("c")
    chips = [(1 - x, y), (x, 1 - y), (1 - x, 1 - y)]
    return x, y, c, chips


def _gather_weight(w, name):
    _, rows, cols = w.shape
    half_rows = rows // 2

    def body(w_ref, out_ref, send_sems, recv_sems):
        x, y, c, chips = _place()
        sibling = (x, y, 1 - c)
        mine = 2 * x + y
        lo = pl.multiple_of(c * half_rows, 16)
        lo_sib = pl.multiple_of((1 - c) * half_rows, 16)
        out_ref[mine] = w_ref[0].astype(BF16)

        def copy(k, shard, first_row, to):
            ref = out_ref.at[shard, pl.ds(first_row, half_rows), :]
            return pltpu.make_async_remote_copy(src_ref=ref, dst_ref=ref, send_sem=send_sems.at[k],
                                                recv_sem=recv_sems.at[k], device_id=to, device_id_type=MESH)

        sends = [copy(k, mine, lo, (cx, cy, c)) for k, (cx, cy) in enumerate(chips)]
        for cp in sends:
            cp.start()
        passed = []
        for k, (cx, cy) in enumerate(chips):
            theirs = 2 * cx + cy
            copy(k, theirs, lo, (cx, cy, c)).wait_recv()
            fw = copy(3 + k, theirs, lo, sibling)
            fw.start()
            passed.append(fw)
        for k, (cx, cy) in enumerate(chips):
            copy(3 + k, 2 * cx + cy, lo_sib, sibling).wait_recv()
        for cp in sends + passed:
            cp.wait_send()

    return pl.pallas_call(
        body, name=name,
        in_specs=[pl.BlockSpec(memory_space=pltpu.VMEM)],
        out_specs=pl.BlockSpec(memory_space=pltpu.VMEM),
        out_shape=jax.ShapeDtypeStruct((4, rows, cols), BF16),
        scratch_shapes=[pltpu.SemaphoreType.DMA((6,)), pltpu.SemaphoreType.DMA((6,))],
        compiler_params=pltpu.CompilerParams(vmem_limit_bytes=VMEM_LIMIT),
    )(w)


def _reduce_scatter_weight(g4, name):
    _, rows, cols = g4.shape
    half_rows = rows // 2

    def body(g_ref, out_ref, sib_buf, stage, landed, send_sems, recv_sems):
        x, y, c, chips = _place()
        sibling = (x, y, 1 - c)
        mine = 2 * x + y
        lo = pl.multiple_of(c * half_rows, 8)
        lo_sib = pl.multiple_of((1 - c) * half_rows, 8)

        def copy(k, src, dst, to):
            return pltpu.make_async_remote_copy(src_ref=src, dst_ref=dst, send_sem=send_sems.at[k],
                                                recv_sem=recv_sems.at[k], device_id=to, device_id_type=MESH)

        swap = copy(0, g_ref.at[:, pl.ds(lo_sib, half_rows), :], sib_buf, sibling)
        swap.start()
        swap.wait_recv()
        sends = []
        for k, (cx, cy) in enumerate(chips):
            theirs = 2 * cx + cy
            stage[k] = (g_ref[theirs, pl.ds(lo, half_rows), :] + sib_buf[theirs]).astype(BF16)
            cp = copy(1 + k, stage.at[k], landed.at[k], (cx, cy, c))
            cp.start()
            sends.append(cp)
        acc = g_ref[mine, pl.ds(lo, half_rows), :] + sib_buf[mine]
        for k, (cx, cy) in enumerate(chips):
            copy(1 + k, stage.at[k], landed.at[k], (cx, cy, c)).wait_recv()
            acc = acc + landed[k].astype(F32)
        out_ref[0, pl.ds(lo, half_rows), :] = acc
        own_half = out_ref.at[0, pl.ds(lo, half_rows), :]
        other_half = out_ref.at[0, pl.ds(lo_sib, half_rows), :]
        done = copy(4, own_half, own_half, sibling)
        done.start()
        copy(4, other_half, other_half, sibling).wait_recv()
        for cp in [swap] + sends + [done]:
            cp.wait_send()

    return pl.pallas_call(
        body, name=name,
        in_specs=[pl.BlockSpec(memory_space=pltpu.VMEM)],
        out_specs=pl.BlockSpec(memory_space=pltpu.VMEM),
        out_shape=jax.ShapeDtypeStruct((1, rows, cols), F32),
        scratch_shapes=[pltpu.VMEM((4, half_rows, cols), F32), pltpu.VMEM((3, half_rows, cols), BF16),
                        pltpu.VMEM((3, half_rows, cols), BF16),
                        pltpu.SemaphoreType.DMA((5,)), pltpu.SemaphoreType.DMA((5,))],
        compiler_params=pltpu.CompilerParams(vmem_limit_bytes=VMEM_LIMIT),
    )(g4)


def _remote(src, dst, sems, k, to):
    send_sems, recv_sems = sems
    return pltpu.make_async_remote_copy(src_ref=src, dst_ref=dst, send_sem=send_sems.at[k], recv_sem=recv_sems.at[k],
                                        device_id=to, device_id_type=MESH)


def _pack_bf16(parts, name):
    rows = [p.shape[1] for p in parts]
    cols = parts[0].shape[2]

    def body(*refs):
        out_ref, first = refs[-1], 0
        for ref, r in zip(refs[:-1], rows):
            out_ref[first:first + r, :] = ref[0].astype(BF16)
            first += r

    return pl.pallas_call(
        body, name=name, in_specs=[pl.BlockSpec(memory_space=pltpu.VMEM)] * len(parts),
        out_specs=pl.BlockSpec(memory_space=pltpu.VMEM),
        out_shape=jax.ShapeDtypeStruct((sum(rows), cols), BF16),
        compiler_params=pltpu.CompilerParams(vmem_limit_bytes=VMEM_LIMIT),
    )(*parts)


def _gather_rider(packed):
    rows, cols = packed.shape
    half = rows // 2

    def copies(ins, outs, sems, finishing):
        p_ref, g_ref = ins[0], outs[0]
        x, y, c, chips = _place()
        sibling = (x, y, 1 - c)
        mine = 2 * x + y
        lo = pl.multiple_of(c * half, 16)
        lo_sib = pl.multiple_of((1 - c) * half, 16)
        spot = lambda shard, first: g_ref.at[shard, pl.ds(first, half), :]
        own = pltpu.make_async_copy(p_ref, g_ref.at[mine], sems[0].at[6])
        sends = [_remote(p_ref.at[pl.ds(lo, half), :], spot(mine, lo), sems, k, (cx, cy, c))
                 for k, (cx, cy) in enumerate(chips)]
        if not finishing:
            return own, sends
        arrivals = [_remote(spot(2 * cx + cy, lo), spot(2 * cx + cy, lo), sems, k, (cx, cy, c))
                    for k, (cx, cy) in enumerate(chips)]
        passes = [_remote(spot(2 * cx + cy, lo), spot(2 * cx + cy, lo), sems, 3 + k, sibling)
                  for k, (cx, cy) in enumerate(chips)]
        from_sibling = [_remote(spot(2 * cx + cy, lo_sib), spot(2 * cx + cy, lo_sib), sems, 3 + k, sibling)
                        for k, (cx, cy) in enumerate(chips)]
        return own, sends, arrivals, passes, from_sibling

    def start(ins, outs, send_sems, recv_sems):
        own, sends = copies(ins, outs, (send_sems, recv_sems), False)
        own.start()
        for cp in sends:
            cp.start()

    def finish(ins, outs, send_sems, recv_sems):
        own, sends, arrivals, passes, from_sibling = copies(ins, outs, (send_sems, recv_sems), True)
        for landed, onward in zip(arrivals, passes):
            landed.wait_recv()
            onward.start()
        for cp in from_sibling:
            cp.wait_recv()
        for cp in sends + passes:
            cp.wait_send()
        own.wait()

    return _Rider([packed], [jax.ShapeDtypeStruct((4, rows, cols), BF16)], 7, start, finish)


def _exchange_rider(inputs, out_shapes, n_sems, copies, aliases=None):
    def start(ins, outs, send_sems, recv_sems):
        for cp in copies(ins, outs, (send_sems, recv_sems)):
            cp.start()

    def finish(ins, outs, send_sems, recv_sems):
        for cp in copies(ins, outs, (send_sems, recv_sems)):
            cp.wait()

    return _Rider(inputs, out_shapes, n_sems, start, finish, aliases)


def _swap_rider(grads4):
    halves = [g.shape[1] // 2 for g in grads4]

    def copies(ins, outs, sems):
        x, y, c, _ = _place()
        return [_remote(g.at[:, pl.ds(pl.multiple_of((1 - c) * h, 8), h), :], a, sems, i, (x, y, 1 - c))
                for i, (g, a, h) in enumerate(zip(ins, outs, halves))]

    shapes = [jax.ShapeDtypeStruct((4, h, g.shape[2]), F32) for g, h in zip(grads4, halves)]
    return _exchange_rider(grads4, shapes, len(grads4), copies)


def _chip_sum(g4, from_sibling, name):
    _, rows, cols = g4.shape
    half = rows // 2

    def body(g_ref, s_ref, stage_ref, own_ref):
        x, y, c, chips = _place()
        lo = pl.multiple_of(c * half, 8)
        for k, (cx, cy) in enumerate(chips):
            theirs = 2 * cx + cy
            stage_ref[k] = (g_ref[theirs, pl.ds(lo, half), :] + s_ref[theirs]).astype(BF16)
        mine = 2 * x + y
        own_ref[...] = g_ref[mine, pl.ds(lo, half), :] + s_ref[mine]

    return pl.pallas_call(
        body, name=name, in_specs=[pl.BlockSpec(memory_space=pltpu.VMEM)] * 2,
        out_specs=[pl.BlockSpec(memory_space=pltpu.VMEM)] * 2,
        out_shape=[jax.ShapeDtypeStruct((3, half, cols), BF16), jax.ShapeDtypeStruct((half, cols), F32)],
        compiler_params=pltpu.CompilerParams(vmem_limit_bytes=VMEM_LIMIT),
    )(g4, from_sibling)


def _spread_rider(stages):
    def copies(ins, outs, sems):
        _, _, c, chips = _place()
        return [_remote(st.at[k], ld.at[k], sems, 3 * i + k, (cx, cy, c))
                for i, (st, ld) in enumerate(zip(ins, outs)) for k, (cx, cy) in enumerate(chips)]

    shapes = [jax.ShapeDtypeStruct(s.shape, s.dtype) for s in stages]
    return _exchange_rider(stages, shapes, 3 * len(stages), copies)


def _finish_half(own, landed, name):
    half, cols = own.shape

    def body(own_ref, landed_ref, out_ref):
        c = lax.axis_index("c")
        acc = own_ref[...]
        for k in range(3):
            acc = acc + landed_ref[k].astype(F32)
        out_ref[pl.ds(pl.multiple_of(c * half, 8), half), :] = acc

    return pl.pallas_call(
        body, name=name, in_specs=[pl.BlockSpec(memory_space=pltpu.VMEM)] * 2,
        out_specs=pl.BlockSpec(memory_space=pltpu.VMEM),
        out_shape=jax.ShapeDtypeStruct((2 * half, cols), F32),
        compiler_params=pltpu.CompilerParams(vmem_limit_bytes=VMEM_LIMIT),
    )(own, landed)


def _share_rider(fulls):
    def copies(ins, outs, sems):
        x, y, c, _ = _place()
        out = []
        for i, full in enumerate(outs):
            half = full.shape[0] // 2
            rows = full.at[pl.ds(pl.multiple_of(c * half, 8), half), :]
            out.append(_remote(rows, rows, sems, i, (x, y, 1 - c)))
        return out

    def finish_copies(ins, outs, sems):
        x, y, c, _ = _place()
        out = []
        for i, full in enumerate(outs):
            half = full.shape[0] // 2
            mine = full.at[pl.ds(pl.multiple_of(c * half, 8), half), :]
            theirs = full.at[pl.ds(pl.multiple_of((1 - c) * half, 8), half), :]
            out.append((_remote(mine, mine, sems, i, (x, y, 1 - c)), _remote(theirs, theirs, sems, i, (x, y, 1 - c))))
        return out

    def start(ins, outs, send_sems, recv_sems):
        for cp in copies(ins, outs, (send_sems, recv_sems)):
            cp.start()

    def finish(ins, outs, send_sems, recv_sems):
        for sent, landed in finish_copies(ins, outs, (send_sems, recv_sems)):
            sent.wait_send()
            landed.wait_recv()

    shapes = [jax.ShapeDtypeStruct(f.shape, f.dtype) for f in fulls]
    return _Rider(fulls, shapes, len(fulls), start, finish, aliases={i: i for i in range(len(fulls))})


def _all_sum_small(v):
    shape = v.shape

    def body(v_ref, out_ref, buf, send_sems, recv_sems):
        x, y, c, _ = _place()
        me = 4 * x + 2 * y + c
        buf[me] = v_ref[...]
        flips = [(dx, dy, dc) for dx in (0, 1) for dy in (0, 1) for dc in (0, 1)][1:]

        def copy(k, slot, flip):
            dx, dy, dc = flip
            to = (1 - x if dx else x, 1 - y if dy else y, 1 - c if dc else c)
            return pltpu.make_async_remote_copy(src_ref=buf.at[slot], dst_ref=buf.at[slot], send_sem=send_sems.at[k],
                                                recv_sem=recv_sems.at[k], device_id=to, device_id_type=MESH)

        sends = [copy(k, me, flip) for k, flip in enumerate(flips)]
        for cp in sends:
            cp.start()
        for k, (dx, dy, dc) in enumerate(flips):
            sender = 4 * (1 - x if dx else x) + 2 * (1 - y if dy else y) + (1 - c if dc else c)
            copy(k, sender, (dx, dy, dc)).wait_recv()
        for cp in sends:
            cp.wait_send()
        total = buf[0]
        for i in range(1, 8):
            total = total + buf[i]
        out_ref[...] = total

    return pl.pallas_call(
        body, name="all_sum_small",
        in_specs=[pl.BlockSpec(memory_space=pltpu.VMEM)],
        out_specs=pl.BlockSpec(memory_space=pltpu.VMEM),
        out_shape=jax.ShapeDtypeStruct(shape, F32),
        scratch_shapes=[pltpu.VMEM((8,) + shape, F32), pltpu.SemaphoreType.DMA((7,)), pltpu.SemaphoreType.DMA((7,))],
    )(v)


SMALL = (("g_mix", 1024), ("g_ffn", 1024), ("g_out_fox", 512), ("g_out_dil", 512), ("g_q_fox", 64),
         ("g_k_fox", 64), ("g_q_dil", 64), ("g_k_dil", 64), ("b_forget", 8))
SMALL_PACKED = (32, LANES)


PACKED_ROWS = (256, 704, 704, 704)


def _local_grads(x, target, gains, w1, wft, dense, packed, nb, seq):
    tile2 = lambda g: jnp.tile(g, (1, 2))
    gq_f, gk_f, gq_d, gk_d = (tile2(gains[n]) for n in ("g_q_fox", "g_k_fox", "g_q_dil", "g_k_dil"))
    b_col = gains["b_forget"].reshape(N_FOX_HEADS, 1)
    cos, up, dn = _rope_tables(seq)
    npair = N_FOX_HEADS // 2

    proj, fa_row, h1, h1_t = _in_proj(x, gains["g_mix"], w1, wft)
    c_row = _gate_fwd(fa_row, b_col, seq)
    c3 = c_row.reshape(npair, 2, nb * seq)
    (o_fox, lse_fox), gathered = _fox_fwd(proj, c3, gq_f, gk_f, nb, seq,
                                          rider=None if packed is None else _gather_rider(packed))
    if packed is not None:
        first, dense = 0, []
        for r in PACKED_ROWS:
            dense.append(gathered[0][:, first:first + r, :].reshape(4 * r, -1))
            first += r
    w_out, w_gate, w_up, w_down = dense
    o_dil, lse_dil = _dil_fwd(proj, gq_d, gk_d, cos, up, dn, nb, seq)
    x1, o_n_t = _attn_out(o_fox, o_dil, x, gains["g_out_fox"], gains["g_out_dil"], w_out)
    a, u, dy, loss_parts = _ffn_fwd(x1, target, gains["g_ffn"], w_gate, w_up, w_down)
    loss = jnp.sum(loss_parts[:, 0, 0])

    dx1, s_t, da_t, du_t, h2, dg_ffn = _ffn_bwd(dy, a, u, x1, gains["g_ffn"], w_gate, w_up, w_down)
    d_w_down = _token_matmul(s_t, dy, "dw_down", 512)
    d_w_gate = _token_matmul(da_t, h2, "dw_gate", 512)
    d_w_up = _token_matmul(du_t, h2, "dw_up", 512)
    d_w_out = _token_matmul(o_n_t, dx1, "dw_out", 1024)
    names = ("w_out", "w_gate", "w_up", "w_down")
    grads4 = [g.reshape(4, -1, g.shape[1]) for g in (d_w_out, d_w_gate, d_w_up, d_w_down)]
    exchange = packed is not None
    (do_fox, do_dil, dg_of, dg_od), from_sibling = _attn_out_bwd(
        dx1, o_fox, o_dil, gains["g_out_fox"], gains["g_out_dil"], w_out,
        rider=_swap_rider(grads4) if exchange else None)
    if exchange:
        sums = [_chip_sum(g, s, "chip_sum_" + n) for g, s, n in zip(grads4, from_sibling, names)]
    (dq_f, dk_f, dv_f, dc3, dg_fox), landed = _fox_bwd(
        proj, c3, gq_f, gk_f, do_fox, o_fox, lse_fox, nb, seq,
        rider=_spread_rider([st for st, _ in sums]) if exchange else None)
    if exchange:
        halves = [_finish_half(own, ld, "finish_half_" + n) for (_, own), ld, n in zip(sums, landed, names)]
    (dq_d, dk_d, dv_d, dg_dil), reduced = _dil_bwd(
        proj, gq_d, gk_d, cos, up, dn, do_dil, o_dil, lse_dil, nb, seq,
        rider=_share_rider(halves) if exchange else None)
    if exchange:
        d_w_out, d_w_gate, d_w_up, d_w_down = reduced
    dfa_row, db = _gate_bwd(dc3.reshape(N_FOX_HEADS, nb * seq), fa_row, b_col, seq)
    dparts = [dq_f, dk_f, dv_f, dq_d, dk_d, dv_d]
    grad_x, dg_mix = _in_proj_bwd(dparts, dfa_row, w1, wft, x, gains["g_mix"], dx1)
    d_w1 = jnp.concatenate([_token_matmul(h1_t, dp, "dw_in_%d" % j, W_GROUP) for j, dp in enumerate(dparts)], axis=1)
    d_wf = _row_matmul(dfa_row, h1, "dw_forget")

    fold = lambda g2: (g2[:, :HEAD_DIM] + g2[:, HEAD_DIM:])
    small = {
        "g_mix": dg_mix[0:1], "g_ffn": dg_ffn[0:1], "g_out_fox": dg_of[0:1], "g_out_dil": dg_od[0:1],
        "g_q_fox": fold(dg_fox[0:1]), "g_k_fox": fold(dg_fox[1:2]),
        "g_q_dil": fold(dg_dil[0:1]), "g_k_dil": fold(dg_dil[1:2]),
        "b_forget": db[:, 0].reshape(1, N_FOX_HEADS),
    }
    big = {"w1": d_w1, "wf": d_wf, "w_out": d_w_out, "w_gate": d_w_gate, "w_up": d_w_up, "w_down": d_w_down}
    return loss, grad_x, big, small


def _shards_of_columns(full, n=4):
    r, nc = full.shape
    return full.reshape(r, n, nc // n).transpose(1, 0, 2)


def _columns_of_shards(slabs):
    n, r, c = slabs.shape
    return slabs.transpose(1, 0, 2).reshape(r, n * c)


def kernel(x, g_mix, w_in, b_forget, g_q_fox, g_k_fox, g_q_dil, g_k_dil, g_out_fox, g_out_dil, w_out, g_ffn, w_gate, w_up, w_down, loss_target, m_g_mix, m_w_in, m_b_forget, m_g_q_fox, m_g_k_fox, m_g_q_dil, m_g_k_dil, m_g_out_fox, m_g_out_dil, m_w_out, m_g_ffn, m_w_gate, m_w_up, m_w_down, v_g_mix, v_w_in, v_b_forget, v_g_q_fox, v_g_k_fox, v_g_q_dil, v_g_k_dil, v_g_out_fox, v_g_out_dil, v_w_out, v_g_ffn, v_w_gate, v_w_up, v_w_down):
    nb, seq, d = x.shape
    weights = dict(g_mix=g_mix, w_in=w_in, b_forget=b_forget, g_q_fox=g_q_fox, g_k_fox=g_k_fox, g_q_dil=g_q_dil,
                   g_k_dil=g_k_dil, g_out_fox=g_out_fox, g_out_dil=g_out_dil, w_out=w_out, g_ffn=g_ffn,
                   w_gate=w_gate, w_up=w_up, w_down=w_down)
    m_in = dict(g_mix=m_g_mix, w_in=m_w_in, b_forget=m_b_forget, g_q_fox=m_g_q_fox, g_k_fox=m_g_k_fox,
                g_q_dil=m_g_q_dil, g_k_dil=m_g_k_dil, g_out_fox=m_g_out_fox, g_out_dil=m_g_out_dil, w_out=m_w_out,
                g_ffn=m_g_ffn, w_gate=m_w_gate, w_up=m_w_up, w_down=m_w_down)
    v_in = dict(g_mix=v_g_mix, w_in=v_w_in, b_forget=v_b_forget, g_q_fox=v_g_q_fox, g_k_fox=v_g_k_fox,
                g_q_dil=v_g_q_dil, g_k_dil=v_g_k_dil, g_out_fox=v_g_out_fox, g_out_dil=v_g_out_dil, w_out=v_w_out,
                g_ffn=v_g_ffn, w_gate=v_w_gate, w_up=v_w_up, w_down=v_w_down)
    order = ["g_mix", "w_in", "b_forget", "g_q_fox", "g_k_fox", "g_q_dil", "g_k_dil", "g_out_fox", "g_out_dil",
             "w_out", "g_ffn", "w_gate", "w_up", "w_down"]

    w_in_full = _columns_of_shards(_gather_weight(w_in, "gather_w_in"))
    fox_w = 3 * W_GROUP
    w1 = jnp.concatenate([w_in_full[:, :fox_w], w_in_full[:, fox_w + N_FOX_HEADS:]], axis=1)
    wft = w_in_full[:, fox_w:fox_w + N_FOX_HEADS].T
    swap = lambda a: jnp.transpose(a, (0, 2, 1))
    for n in ("w_gate", "w_up"):
        weights[n], m_in[n], v_in[n] = swap(weights[n]), swap(m_in[n]), swap(v_in[n])
    shards = _pack_bf16([weights[n] for n in ("w_out", "w_gate", "w_up", "w_down")], "pack_shards")

    gains = {n: weights[n] for n, _ in SMALL}
    loss, grad_x, big, small = _local_grads(
        x.reshape(nb * seq, d), loss_target.reshape(nb * seq, d), gains, w1, wft, None, shards, nb, seq)
    loss = lax.psum(loss, ("x", "y", "c"))

    d_w_in_full = jnp.concatenate([big["w1"][:, :fox_w], big["wf"].T, big["w1"][:, fox_w:]], axis=1)
    grads = {"w_in": _reduce_scatter_weight(_shards_of_columns(d_w_in_full), "reduce_w_in")}
    for n in ("w_out", "w_gate", "w_up", "w_down"):
        grads[n] = big[n][None]
    packed = jnp.concatenate([small[n].reshape(-1) for n, _ in SMALL])
    packed = jnp.pad(packed, (0, SMALL_PACKED[0] * SMALL_PACKED[1] - packed.shape[0])).reshape(SMALL_PACKED)
    summed = _all_sum_small(packed).reshape(-1)
    pos = 0
    for n, size in SMALL:
        grads[n] = summed[pos:pos + size].reshape(1, size)
        pos += size

    deltas, new_m, new_v, grad_out = {}, {}, {}, {}
    for n in order:
        grad_out[n] = grads[n]
        deltas[n], new_m[n], new_v[n] = _adamw(weights[n], grads[n], m_in[n], v_in[n], "adamw_" + n)
    for n in ("w_gate", "w_up"):
        grad_out[n], deltas[n], new_m[n], new_v[n] = (swap(a) for a in (grad_out[n], deltas[n], new_m[n], new_v[n]))

    return (loss, grad_x.reshape(nb, seq, d), *[grad_out[n] for n in order], *[deltas[n] for n in order],
            *[new_m[n] for n in order], *[new_v[n] for n in order])
```

```python
import functools
import math

import numpy as np
import jax
import jax.numpy as jnp
from jax import lax
from jax.experimental import pallas as pl
from jax.experimental.pallas import tpu as pltpu

F32, BF16 = jnp.float32, jnp.bfloat16
MESH = pl.DeviceIdType.MESH

EPS = 1e-6
NEG = -1e30
HEAD_DIM = 64
SCALE = HEAD_DIM ** -0.5
LOG2E = math.log2(math.e)
LN2 = math.log(2.0)
ROPE_THETA = 500000.0
ROPE_DIM = HEAD_DIM // 4
LANES = 128
W_GROUP = 512
N_FOX_HEADS = 8
VMEM_LIMIT = 56 * 1024 * 1024
DILATIONS = (1, 4, 16)
BAND = 128

ADAM_LR, ADAM_B1, ADAM_B2, ADAM_EPS, ADAM_WD, ADAM_STEP = 0.001, 0.9, 0.999, 1e-08, 0.01, 10

NT = (((1,), (1,)), ((), ()))
TN = (((0,), (0,)), ((), ()))
BATCH_NT = (((2,), (2,)), ((0,), (0,)))
BATCH_NN = (((2,), (1,)), ((0,), (0,)))
BATCH_TN = (((1,), (1,)), ((0,), (0,)))


def _params(sem=None):
    return pltpu.CompilerParams(dimension_semantics=sem, vmem_limit_bytes=VMEM_LIMIT)


def _dot(a, b, dims=None):
    if dims is None:
        return jnp.dot(a, b, preferred_element_type=F32)
    return lax.dot_general(a, b, dims, preferred_element_type=F32)


def _group_ones():
    i = lax.broadcasted_iota(jnp.int32, (LANES, LANES), 0) >> 6
    j = lax.broadcasted_iota(jnp.int32, (LANES, LANES), 1) >> 6
    return (i == j).astype(BF16)


def _split3(x):
    a = x.astype(BF16)
    r = x - a.astype(F32)
    b = r.astype(BF16)
    c = (r - b.astype(F32)).astype(BF16)
    return a, b, c


def _groupsum(x, ones):
    a, b, c = _split3(x)
    return _dot(a, ones) + _dot(b, ones) + _dot(c, ones)


def _head_masks():
    lane = lax.broadcasted_iota(jnp.int32, (1, LANES), 1)
    return [(lane < HEAD_DIM).astype(F32), (lane >= HEAD_DIM).astype(F32)]


def _head_norm(raw, gain, ones):
    r = lax.rsqrt(_groupsum(raw * raw, ones) * (1.0 / HEAD_DIM) + EPS)
    return raw * r, r


def _head_norm_bwd(dy, xhat, r, gain, ones):
    u = dy * gain
    dgain = jnp.sum(dy * xhat, axis=0, keepdims=True)
    draw = r * (u - xhat * (_groupsum(u * xhat, ones) * (1.0 / HEAD_DIM)))
    return draw, dgain


def _rope(x, cos, s_up, s_dn):
    return x * cos + pltpu.roll(x, LANES - 8, 1) * s_up + pltpu.roll(x, 8, 1) * s_dn


def _rope_bwd(dy, cos, s_up, s_dn):
    return dy * cos + pltpu.roll(dy * s_up, 8, 1) + pltpu.roll(dy * s_dn, LANES - 8, 1)


def _rope_tables(seq):
    half = ROPE_DIM // 2
    inv_freq = jnp.power(jnp.float32(ROPE_THETA), -jnp.arange(half, dtype=F32) * 2.0 / ROPE_DIM)
    ang = jnp.arange(seq).astype(F32)[:, None] * inv_freq[None, :]
    cos, sin = jnp.cos(ang), jnp.sin(ang)
    one = jnp.ones((seq, HEAD_DIM - ROPE_DIM), F32)
    zero_h = jnp.zeros((seq, half), F32)
    zero_r = jnp.zeros((seq, HEAD_DIM - ROPE_DIM), F32)
    c = jnp.concatenate([cos, cos, one], axis=1)
    up = jnp.concatenate([-sin, zero_h, zero_r], axis=1)
    dn = jnp.concatenate([zero_h, sin, zero_r], axis=1)
    return jnp.tile(c, (1, 2)), jnp.tile(up, (1, 2)), jnp.tile(dn, (1, 2))


def _row_tile(rows, cap=256):
    best = rows
    for t in range(8, min(rows, cap) + 1, 8):
        if rows % t == 0:
            best = t
    return best


class _Rider:
    def __init__(self, inputs, out_shapes, n_sems, start, finish, aliases=None):
        self.inputs, self.out_shapes, self.n_sems = list(inputs), list(out_shapes), n_sems
        self.start, self.finish, self.aliases = start, finish, dict(aliases or {})


def _host_call(body, rider, *, name, grid, in_specs, out_specs, out_shape, scratch_shapes, inputs, semantics):
    if rider is None:
        return pl.pallas_call(body, name=name, grid=grid, in_specs=in_specs, out_specs=out_specs,
                              out_shape=out_shape, scratch_shapes=scratch_shapes,
                              compiler_params=_params(semantics))(*inputs), []
    n_in, n_out, n_scr = len(in_specs), len(out_specs), len(scratch_shapes)
    r_in, r_out = len(rider.inputs), len(rider.out_shapes)

    def wrapped(*refs):
        ins, refs = refs[:n_in], refs[n_in:]
        r_ins, refs = refs[:r_in], refs[r_in:]
        outs, refs = refs[:n_out], refs[n_out:]
        r_outs, refs = refs[:r_out], refs[r_out:]
        scratch, (send_sems, recv_sems) = refs[:n_scr], refs[n_scr:]
        ids = [pl.program_id(a) for a in range(len(grid))]
        first = functools.reduce(lambda p, q: p & q, [i == 0 for i in ids])
        last = functools.reduce(lambda p, q: p & q, [i == g - 1 for i, g in zip(ids, grid)])

        @pl.when(first)
        def _():
            rider.start(r_ins, r_outs, send_sems, recv_sems)

        body(*ins, *outs, *scratch)

        @pl.when(last)
        def _():
            rider.finish(r_ins, r_outs, send_sems, recv_sems)

    hbm = pl.BlockSpec(memory_space=pl.ANY)
    res = pl.pallas_call(
        wrapped, name=name, grid=grid,
        in_specs=list(in_specs) + [hbm] * r_in, out_specs=list(out_specs) + [hbm] * r_out,
        out_shape=list(out_shape) + rider.out_shapes,
        scratch_shapes=list(scratch_shapes) + [pltpu.SemaphoreType.DMA((rider.n_sems,))] * 2,
        input_output_aliases={n_in + i: n_out + o for i, o in rider.aliases.items()},
        compiler_params=_params(semantics),
    )(*inputs, *rider.inputs)
    return res[:n_out], res[n_out:]


def _idle_host(rider, name):
    def body(o_ref):
        o_ref[...] = jnp.zeros_like(o_ref)

    return _host_call(body, rider, name=name, grid=(1,), in_specs=[],
                      out_specs=[pl.BlockSpec((8, LANES), lambda i: (0, 0))],
                      out_shape=[jax.ShapeDtypeStruct((8, LANES), F32)], scratch_shapes=[], inputs=(),
                      semantics=("arbitrary",))


def _in_proj(x, g_mix, w1, wft):
    t, d = x.shape
    n = w1.shape[1]
    tt = 512

    def body(x_ref, g_ref, w_ref, wf_ref, p_ref, fa_ref, h_ref, ht_ref):
        xx = x_ref[...]
        r = lax.rsqrt(jnp.mean(xx * xx, axis=-1, keepdims=True) + EPS)
        h = (xx * r * g_ref[...]).astype(BF16)
        h_ref[...] = h
        ht_ref[...] = h.T
        for j in range(n // W_GROUP):
            cols = slice(j * W_GROUP, (j + 1) * W_GROUP)
            p_ref[:, cols] = _dot(h, w_ref[:, cols]).astype(BF16)
        fa_ref[...] = _dot(wf_ref[...], h, NT)

    return pl.pallas_call(
        body, name="in_proj", grid=(t // tt,),
        in_specs=[pl.BlockSpec((tt, d), lambda i: (i, 0)), pl.BlockSpec((1, d), lambda i: (0, 0)),
                  pl.BlockSpec(memory_space=pltpu.VMEM), pl.BlockSpec(memory_space=pltpu.VMEM)],
        out_specs=[pl.BlockSpec((tt, n), lambda i: (i, 0)), pl.BlockSpec((8, tt), lambda i: (0, i)),
                   pl.BlockSpec((tt, d), lambda i: (i, 0)), pl.BlockSpec((d, tt), lambda i: (0, i))],
        out_shape=[jax.ShapeDtypeStruct((t, n), BF16), jax.ShapeDtypeStruct((8, t), F32),
                   jax.ShapeDtypeStruct((t, d), BF16), jax.ShapeDtypeStruct((d, t), BF16)],
        compiler_params=_params(("arbitrary",)),
    )(x, g_mix, w1, wft)


def _tri(n, upper):
    i = lax.broadcasted_iota(jnp.int32, (n, n), 0)
    j = lax.broadcasted_iota(jnp.int32, (n, n), 1)
    return ((i <= j) if upper else (i >= j)).astype(BF16)


def _gate_fwd(fa_row, b_col, seq):
    t = fa_row.shape[1]
    cb = 256

    def body(fa_ref, b_ref, c_ref):
        tri = _tri(cb, True)
        carry = jnp.zeros((8, 1), F32)
        for k in range(seq // cb):
            z = fa_ref[:, k * cb:(k + 1) * cb] + b_ref[...]
            lf = jnp.minimum(z, 0.0) - jnp.log(1.0 + jnp.exp(-jnp.abs(z)))
            a, b, c = _split3(lf)
            blk = _dot(a, tri) + _dot(b, tri) + _dot(c, tri) + carry
            c_ref[:, k * cb:(k + 1) * cb] = blk
            carry = blk[:, cb - 1:cb]

    return pl.pallas_call(
        body, name="gate_fwd", grid=(t // seq,),
        in_specs=[pl.BlockSpec((8, seq), lambda i: (0, i)), pl.BlockSpec((8, 1), lambda i: (0, 0))],
        out_specs=pl.BlockSpec((8, seq), lambda i: (0, i)),
        out_shape=jax.ShapeDtypeStruct((8, t), F32),
        compiler_params=_params(("arbitrary",)),
    )(fa_row, b_col)


def _gate_bwd(dc_row, fa_row, b_col, seq):
    t = fa_row.shape[1]
    cb = 256

    def body(dc_ref, fa_ref, b_ref, dfa_ref, db_ref):
        @pl.when(pl.program_id(0) == 0)
        def _():
            db_ref[...] = jnp.zeros_like(db_ref)

        tri = _tri(cb, False)
        carry = jnp.zeros((8, 1), F32)
        dbs = jnp.zeros((8, 1), F32)
        for k in reversed(range(seq // cb)):
            a, b, c = _split3(dc_ref[:, k * cb:(k + 1) * cb])
            dlf = _dot(a, tri) + _dot(b, tri) + _dot(c, tri) + carry
            carry = dlf[:, 0:1]
            z = fa_ref[:, k * cb:(k + 1) * cb] + b_ref[...]
            dfa = dlf / (1.0 + jnp.exp(z))
            dfa_ref[:, k * cb:(k + 1) * cb] = dfa
            dbs = dbs + jnp.sum(dfa, axis=1, keepdims=True)
        db_ref[...] += jnp.broadcast_to(dbs, (8, LANES))

    return pl.pallas_call(
        body, name="gate_bwd", grid=(t // seq,),
        in_specs=[pl.BlockSpec((8, seq), lambda i: (0, i)), pl.BlockSpec((8, seq), lambda i: (0, i)),
                  pl.BlockSpec((8, 1), lambda i: (0, 0))],
        out_specs=[pl.BlockSpec((8, seq), lambda i: (0, i)), pl.BlockSpec((8, LANES), lambda i: (0, 0))],
        out_shape=[jax.ShapeDtypeStruct((8, t), F32), jax.ShapeDtypeStruct((8, LANES), F32)],
        compiler_params=_params(("arbitrary",)),
    )(dc_row, fa_row, b_col)


def _attn_out(o_fox, o_dil, x, g_fox, g_dil, w_out):
    t, d = x.shape
    w = o_fox.shape[1]
    tt = 512

    def body(of_ref, od_ref, x_ref, gf_ref, gd_ref, w_ref, x1_ref, ont_ref):
        acc = x_ref[...]
        for k, (o_ref, g_ref) in enumerate(((of_ref, gf_ref), (od_ref, gd_ref))):
            o = o_ref[...]
            r = lax.rsqrt(jnp.mean(o * o, axis=-1, keepdims=True) + EPS)
            on = (o * r * g_ref[...]).astype(BF16)
            ont_ref[k * w:(k + 1) * w, :] = on.T
            acc = acc + _dot(on, w_ref[k * w:(k + 1) * w, :])
        x1_ref[...] = acc

    return pl.pallas_call(
        body, name="attn_out", grid=(t // tt,),
        in_specs=[pl.BlockSpec((tt, w), lambda i: (i, 0)), pl.BlockSpec((tt, w), lambda i: (i, 0)),
                  pl.BlockSpec((tt, d), lambda i: (i, 0)), pl.BlockSpec((1, w), lambda i: (0, 0)),
                  pl.BlockSpec((1, w), lambda i: (0, 0)), pl.BlockSpec(memory_space=pltpu.VMEM)],
        out_specs=[pl.BlockSpec((tt, d), lambda i: (i, 0)), pl.BlockSpec((2 * w, tt), lambda i: (0, i))],
        out_shape=[jax.ShapeDtypeStruct((t, d), F32), jax.ShapeDtypeStruct((2 * w, t), BF16)],
        compiler_params=_params(("arbitrary",)),
    )(o_fox, o_dil, x, g_fox, g_dil, w_out)


def _attn_out_bwd(dx1, o_fox, o_dil, g_fox, g_dil, w_out, rider=None):
    t, d = dx1.shape
    w = o_fox.shape[1]
    tt = 512

    def body(dx_ref, of_ref, od_ref, gf_ref, gd_ref, w_ref, dof_ref, dod_ref, dgf_ref, dgd_ref):
        @pl.when(pl.program_id(0) == 0)
        def _():
            dgf_ref[...] = jnp.zeros_like(dgf_ref)
            dgd_ref[...] = jnp.zeros_like(dgd_ref)

        dxb = dx_ref[...].astype(BF16)
        for k, (o_ref, g_ref, do_ref, dg_ref) in enumerate(
                ((of_ref, gf_ref, dof_ref, dgf_ref), (od_ref, gd_ref, dod_ref, dgd_ref))):
            don = _dot(dxb, w_ref[k * w:(k + 1) * w, :], NT)
            o = o_ref[...]
            r = lax.rsqrt(jnp.mean(o * o, axis=-1, keepdims=True) + EPS)
            xhat = o * r
            u = don * g_ref[...]
            do_ref[...] = r * (u - xhat * jnp.mean(u * xhat, axis=-1, keepdims=True))
            dg_ref[0:1, :] += jnp.sum(don * xhat, axis=0, keepdims=True)

    return _host_call(
        body, rider, name="attn_out_bwd", grid=(t // tt,),
        in_specs=[pl.BlockSpec((tt, d), lambda i: (i, 0)), pl.BlockSpec((tt, w), lambda i: (i, 0)),
                  pl.BlockSpec((tt, w), lambda i: (i, 0)), pl.BlockSpec((1, w), lambda i: (0, 0)),
                  pl.BlockSpec((1, w), lambda i: (0, 0)), pl.BlockSpec(memory_space=pltpu.VMEM)],
        out_specs=[pl.BlockSpec((tt, w), lambda i: (i, 0)), pl.BlockSpec((tt, w), lambda i: (i, 0)),
                   pl.BlockSpec((8, w), lambda i: (0, 0)), pl.BlockSpec((8, w), lambda i: (0, 0))],
        out_shape=[jax.ShapeDtypeStruct((t, w), F32), jax.ShapeDtypeStruct((t, w), F32),
                   jax.ShapeDtypeStruct((8, w), F32), jax.ShapeDtypeStruct((8, w), F32)],
        scratch_shapes=[], inputs=(dx1, o_fox, o_dil, g_fox, g_dil, w_out), semantics=("arbitrary",))


def _ffn_fwd(x1, target, g_ffn, w_gate, w_up, w_down):
    t, d = x1.shape
    f = w_gate.shape[0]
    tt = 256

    def body(x_ref, t_ref, g_ref, wg_ref, wu_ref, wd_ref, a_ref, u_ref, dy_ref, loss_ref):
        xx = x_ref[...]
        r = lax.rsqrt(jnp.mean(xx * xx, axis=-1, keepdims=True) + EPS)
        h = (xx * r * g_ref[...]).astype(BF16)
        a = _dot(h, wg_ref[...], NT)
        u = _dot(h, wu_ref[...], NT)
        a_ref[...] = a.astype(BF16)
        u_ref[...] = u.astype(BF16)
        s = (a / (1.0 + jnp.exp(-a)) * u).astype(BF16)
        y = xx + _dot(s, wd_ref[...])
        e = y - t_ref[...]
        dy_ref[...] = e * (1.0 / d)
        loss_ref[...] = jnp.broadcast_to(0.5 * jnp.sum(e * e) * (1.0 / d), (1, 8, LANES))

    return pl.pallas_call(
        body, name="ffn_fwd", grid=(t // tt,),
        in_specs=[pl.BlockSpec((tt, d), lambda i: (i, 0)), pl.BlockSpec((tt, d), lambda i: (i, 0)),
                  pl.BlockSpec((1, d), lambda i: (0, 0)), pl.BlockSpec(memory_space=pltpu.VMEM),
                  pl.BlockSpec(memory_space=pltpu.VMEM), pl.BlockSpec(memory_space=pltpu.VMEM)],
        out_specs=[pl.BlockSpec((tt, f), lambda i: (i, 0)), pl.BlockSpec((tt, f), lambda i: (i, 0)),
                   pl.BlockSpec((tt, d), lambda i: (i, 0)), pl.BlockSpec((1, 8, LANES), lambda i: (i, 0, 0))],
        out_shape=[jax.ShapeDtypeStruct((t, f), BF16), jax.ShapeDtypeStruct((t, f), BF16),
                   jax.ShapeDtypeStruct((t, d), F32), jax.ShapeDtypeStruct((t // tt, 8, LANES), F32)],
        compiler_params=_params(("arbitrary",)),
    )(x1, target, g_ffn, w_gate, w_up, w_down)


def _ffn_bwd(dy, a, u, x1, g_ffn, w_gate, w_up, w_down):
    t, d = x1.shape
    f = w_gate.shape[0]
    tt = 256

    def body(dy_ref, a_ref, u_ref, x_ref, g_ref, wg_ref, wu_ref, wd_ref,
             dx_ref, s_ref, da_ref, du_ref, h_ref, dg_ref):
        @pl.when(pl.program_id(0) == 0)
        def _():
            dg_ref[...] = jnp.zeros_like(dg_ref)

        dy_ = dy_ref[...]
        ds = _dot(dy_.astype(BF16), wd_ref[...], NT)
        a_ = a_ref[...].astype(F32)
        u_ = u_ref[...].astype(F32)
        sig = 1.0 / (1.0 + jnp.exp(-a_))
        silu = a_ * sig
        s_ref[...] = (silu * u_).astype(BF16)
        da = (ds * u_ * (sig * (1.0 + a_ * (1.0 - sig)))).astype(BF16)
        du = (ds * silu).astype(BF16)
        da_ref[...] = da
        du_ref[...] = du
        dh = _dot(da, wg_ref[...]) + _dot(du, wu_ref[...])
        xx = x_ref[...]
        r = lax.rsqrt(jnp.mean(xx * xx, axis=-1, keepdims=True) + EPS)
        xhat = xx * r
        g = g_ref[...]
        h_ref[...] = (xhat * g).astype(BF16)
        uu = dh * g
        dx_ref[...] = dy_ + r * (uu - xhat * jnp.mean(uu * xhat, axis=-1, keepdims=True))
        dg_ref[0:1, :] += jnp.sum(dh * xhat, axis=0, keepdims=True)

    return pl.pallas_call(
        body, name="ffn_bwd", grid=(t // tt,),
        in_specs=[pl.BlockSpec((tt, d), lambda i: (i, 0)), pl.BlockSpec((tt, f), lambda i: (i, 0)),
                  pl.BlockSpec((tt, f), lambda i: (i, 0)), pl.BlockSpec((tt, d), lambda i: (i, 0)),
                  pl.BlockSpec((1, d), lambda i: (0, 0)), pl.BlockSpec(memory_space=pltpu.VMEM),
                  pl.BlockSpec(memory_space=pltpu.VMEM), pl.BlockSpec(memory_space=pltpu.VMEM)],
        out_specs=[pl.BlockSpec((tt, d), lambda i: (i, 0)), pl.BlockSpec((tt, f), lambda i: (i, 0)),
                   pl.BlockSpec((tt, f), lambda i: (i, 0)), pl.BlockSpec((tt, f), lambda i: (i, 0)),
                   pl.BlockSpec((tt, d), lambda i: (i, 0)), pl.BlockSpec((8, d), lambda i: (0, 0))],
        out_shape=[jax.ShapeDtypeStruct((t, d), F32), jax.ShapeDtypeStruct((t, f), BF16),
                   jax.ShapeDtypeStruct((t, f), BF16), jax.ShapeDtypeStruct((t, f), BF16),
                   jax.ShapeDtypeStruct((t, d), BF16), jax.ShapeDtypeStruct((8, d), F32)],
        compiler_params=_params(("arbitrary",)),
    )(dy, a, u, x1, g_ffn, w_gate, w_up, w_down)


def _in_proj_bwd(dparts, dfa_row, w1, wft, x, g_mix, dx1, rider=None):
    t, d = x.shape
    tt = 512
    npart = len(dparts)

    def body(*refs):
        dp_refs = refs[:npart]
        dfa_ref, w_ref, wf_ref, x_ref, g_ref, dx1_ref, dx_ref, dg_ref = refs[npart:]

        @pl.when(pl.program_id(0) == 0)
        def _():
            dg_ref[...] = jnp.zeros_like(dg_ref)

        dh = _dot(dfa_ref[...].astype(BF16), wf_ref[...], TN)
        for j in range(npart):
            dh = dh + _dot(dp_refs[j][...], w_ref[:, j * W_GROUP:(j + 1) * W_GROUP], NT)
        xx = x_ref[...]
        r = lax.rsqrt(jnp.mean(xx * xx, axis=-1, keepdims=True) + EPS)
        xhat = xx * r
        uu = dh * g_ref[...]
        dx_ref[...] = dx1_ref[...] + r * (uu - xhat * jnp.mean(uu * xhat, axis=-1, keepdims=True))
        dg_ref[0:1, :] += jnp.sum(dh * xhat, axis=0, keepdims=True)

    return _host_call(
        body, rider, name="in_proj_bwd", grid=(t // tt,),
        in_specs=[pl.BlockSpec((tt, W_GROUP), lambda i: (i, 0)) for _ in range(npart)]
        + [pl.BlockSpec((8, tt), lambda i: (0, i)), pl.BlockSpec(memory_space=pltpu.VMEM),
           pl.BlockSpec(memory_space=pltpu.VMEM), pl.BlockSpec((tt, d), lambda i: (i, 0)),
           pl.BlockSpec((1, d), lambda i: (0, 0)), pl.BlockSpec((tt, d), lambda i: (i, 0))],
        out_specs=[pl.BlockSpec((tt, d), lambda i: (i, 0)), pl.BlockSpec((8, d), lambda i: (0, 0))],
        out_shape=[jax.ShapeDtypeStruct((t, d), F32), jax.ShapeDtypeStruct((8, d), F32)],
        scratch_shapes=[], inputs=(*dparts, dfa_row, w1, wft, x, g_mix, dx1), semantics=("arbitrary",))


def _token_matmul(a, b, name, tn, a_is_transposed=True):
    m, t = a.shape if a_is_transposed else a.shape[::-1]
    n = b.shape[1]
    tk = 1024

    def body(a_ref, b_ref, o_ref):
        @pl.when(pl.program_id(1) == 0)
        def _():
            o_ref[...] = jnp.zeros_like(o_ref)

        o_ref[...] += _dot(a_ref[...], b_ref[...].astype(BF16), None if a_is_transposed else TN)

    a_spec = pl.BlockSpec((m, tk), lambda j, k: (0, k)) if a_is_transposed else pl.BlockSpec((tk, m), lambda j, k: (k, 0))
    return pl.pallas_call(
        body, name=name, grid=(n // tn, t // tk),
        in_specs=[a_spec, pl.BlockSpec((tk, tn), lambda j, k: (k, j))],
        out_specs=pl.BlockSpec((m, tn), lambda j, k: (0, j)),
        out_shape=jax.ShapeDtypeStruct((m, n), F32),
        compiler_params=_params(("arbitrary", "arbitrary")),
    )(a, b)


def _token_matmul_parts(at, parts, name):
    m, t = at.shape
    widths = [p.shape[1] for p in parts]
    tk = 1024

    def body(a_ref, *refs):
        o_ref = refs[-1]

        @pl.when(pl.program_id(0) == 0)
        def _():
            o_ref[...] = jnp.zeros_like(o_ref)

        a, first = a_ref[...], 0
        for b_ref, w in zip(refs[:-1], widths):
            o_ref[:, first:first + w] += _dot(a, b_ref[...])
            first += w

    return pl.pallas_call(
        body, name=name, grid=(t // tk,),
        in_specs=[pl.BlockSpec((m, tk), lambda k: (0, k))] + [pl.BlockSpec((tk, w), lambda k: (k, 0)) for w in widths],
        out_specs=pl.BlockSpec((m, sum(widths)), lambda k: (0, 0)),
        out_shape=jax.ShapeDtypeStruct((m, sum(widths)), F32),
        compiler_params=_params(("arbitrary",)),
    )(at, *parts)


def _row_matmul(a_row, b, name):
    t, n = b.shape
    tk = 1024
    nk = t // tk

    def body(a_ref, b_ref, o_ref):
        @pl.when(pl.program_id(0) == 0)
        def _():
            o_ref[...] = jnp.zeros_like(o_ref)

        o_ref[...] += _dot(a_ref[...].astype(BF16), b_ref[...])

    return pl.pallas_call(
        body, name=name, grid=(nk,),
        in_specs=[pl.BlockSpec((8, tk), lambda k: (0, k)), pl.BlockSpec((tk, n), lambda k: (k, 0))],
        out_specs=pl.BlockSpec((8, n), lambda k: (0, 0)),
        out_shape=jax.ShapeDtypeStruct((8, n), F32),
        compiler_params=_params(("arbitrary",)),
    )(a_row, b)


FOX_TQ = 256


def _fox_fwd(proj, c3, gq, gk, nb, seq, rider=None):
    t = nb * seq
    tq = FOX_TQ
    nq = seq // tq
    npair = N_FOX_HEADS // 2

    def body(q_ref, k_ref, v_ref, c_ref, gq_ref, gk_ref, o_ref, lse_ref, qs, ks, vs):
        ones = _group_ones()
        masks = _head_masks()
        qhat, _ = _head_norm(q_ref[...].astype(F32), None, ones)
        khat, _ = _head_norm(k_ref[...].astype(F32), None, ones)
        qs[...] = (qhat * gq_ref[...] * (SCALE * LOG2E)).astype(BF16)
        kn = khat * gk_ref[...]
        for hd in range(2):
            ks[hd] = (kn * masks[hd]).astype(BF16)
        vs[...] = v_ref[...]
        row = lax.broadcasted_iota(jnp.int32, (tq, tq), 0)
        col = lax.broadcasted_iota(jnp.int32, (tq, tq), 1)
        causal = col <= row

        for qi in range(nq):
            q0 = qi * tq
            q_blk = qs[q0:q0 + tq, :]
            o_tot = jnp.zeros((tq, LANES), F32)
            lse_tot = jnp.zeros((tq, LANES), F32)
            for hd in range(2):
                crow = c_ref[0, hd:hd + 1, 0:q0 + tq] * LOG2E
                c0 = crow[:, q0:q0 + 1]
                s_d = _dot(q_blk, ks[hd, q0:q0 + tq, :], NT) + (c0 - crow[:, q0:q0 + tq])
                s_d = jnp.where(causal, s_d, NEG)
                m = jnp.max(s_d, axis=-1, keepdims=True)
                if qi > 0:
                    s_o = _dot(q_blk, ks[hd, 0:q0, :], NT) + (c0 - crow[:, 0:q0])
                    m = jnp.maximum(m, jnp.max(s_o, axis=-1, keepdims=True))
                p_d = jnp.exp2(s_d - m)
                l = jnp.sum(p_d, axis=-1, keepdims=True)
                acc = _dot(p_d.astype(BF16), vs[q0:q0 + tq, :])
                if qi > 0:
                    p_o = jnp.exp2(s_o - m)
                    l = l + jnp.sum(p_o, axis=-1, keepdims=True)
                    acc = acc + _dot(p_o.astype(BF16), vs[0:q0, :])
                o_tot = o_tot + (acc / l) * masks[hd]
                lse_tot = lse_tot + (m + jnp.log2(l) - c0) * masks[hd]
            o_ref[q0:q0 + tq, :] = o_tot
            lse_ref[q0:q0 + tq, :] = lse_tot

    blk = lambda off: pl.BlockSpec((seq, LANES), lambda b, p: (b, off + p))
    return _host_call(
        body, rider, name="fox_fwd", grid=(nb, npair),
        in_specs=[blk(0), blk(npair), blk(2 * npair), pl.BlockSpec((1, 2, seq), lambda b, p: (p, 0, b)),
                  pl.BlockSpec((1, LANES), lambda b, p: (0, 0)), pl.BlockSpec((1, LANES), lambda b, p: (0, 0))],
        out_specs=[blk(0), blk(0)],
        out_shape=[jax.ShapeDtypeStruct((t, W_GROUP), F32), jax.ShapeDtypeStruct((t, W_GROUP), F32)],
        scratch_shapes=[pltpu.VMEM((seq, LANES), BF16), pltpu.VMEM((2, seq, LANES), BF16),
                        pltpu.VMEM((seq, LANES), BF16)],
        inputs=(proj, proj, proj, c3, gq, gk), semantics=("arbitrary", "arbitrary"))


def _fox_bwd(proj, c3, gq, gk, do, o, lse, nb, seq, rider=None):
    t = nb * seq
    tq = FOX_TQ
    nq = seq // tq
    npair = N_FOX_HEADS // 2

    def body(q_ref, k_ref, v_ref, c_ref, gq_ref, gk_ref, do_ref, o_ref, lse_ref,
             dq_ref, dk_ref, dv_ref, dc_ref, dg_ref, qs, ks, vs, dos, delta, dq_acc, dk_acc, dv_acc, row_sum):
        @pl.when((pl.program_id(0) == 0) & (pl.program_id(1) == 0))
        def _():
            dg_ref[...] = jnp.zeros_like(dg_ref)

        ones = _group_ones()
        masks = _head_masks()
        qhat, rq = _head_norm(q_ref[...].astype(F32), None, ones)
        khat, rk = _head_norm(k_ref[...].astype(F32), None, ones)
        qs[...] = (qhat * gq_ref[...] * (SCALE * LOG2E)).astype(BF16)
        kn = khat * gk_ref[...]
        vv = v_ref[...].astype(F32)
        for hd in range(2):
            ks[hd] = (kn * masks[hd]).astype(BF16)
            vs[hd] = (vv * masks[hd]).astype(BF16)
        dof = do_ref[...]
        dos[...] = dof.astype(BF16)
        delta[...] = _groupsum(dof * o_ref[...], ones)
        dq_acc[...] = jnp.zeros_like(dq_acc)
        dk_acc[...] = jnp.zeros_like(dk_acc)
        dv_acc[...] = jnp.zeros_like(dv_acc)
        row_sum[...] = jnp.zeros_like(row_sum)
        row = lax.broadcasted_iota(jnp.int32, (tq, tq), 0)
        col = lax.broadcasted_iota(jnp.int32, (tq, tq), 1)
        causal = col <= row

        for hd in range(2):
            lane0 = hd * HEAD_DIM
            for kj in range(nq):
                k0 = kj * tq
                k_blk = ks[hd, k0:k0 + tq, :]
                v_blk = vs[hd, k0:k0 + tq, :]
                crow = c_ref[0, hd:hd + 1, k0:k0 + tq] * LOG2E
                ck0 = crow[:, 0:1]
                bias = ck0 - crow

                def rows_step(r0, r1, diag, hd=hd, lane0=lane0, k_blk=k_blk, v_blk=v_blk, bias=bias, ck0=ck0):
                    q_r = qs[r0:r1, :]
                    do_r = dos[r0:r1, :]
                    z = _dot(q_r, k_blk, NT) + bias
                    p = jnp.exp2(z - (lse_ref[r0:r1, lane0:lane0 + 1] + ck0))
                    if diag:
                        p = jnp.where(causal, p, 0.0)
                    dp = _dot(do_r, v_blk, NT)
                    ds = p * (dp - delta[r0:r1, lane0:lane0 + 1])
                    dsb = ds.astype(BF16)
                    dq_acc[r0:r1, :] += _dot(dsb, k_blk)
                    row_sum[r0:r1, :] += jnp.sum(ds, axis=1, keepdims=True) * masks[hd]
                    return _dot(dsb, q_r, TN), _dot(p.astype(BF16), do_r, TN), -jnp.sum(ds, axis=0, keepdims=True)

                dk_j, dv_j, dc_j = rows_step(k0, k0 + tq, True)
                if k0 + tq < seq:
                    dk_o, dv_o, dc_o = rows_step(k0 + tq, seq, False)
                    dk_j, dv_j, dc_j = dk_j + dk_o, dv_j + dv_o, dc_j + dc_o
                dk_acc[k0:k0 + tq, :] += dk_j * masks[hd]
                dv_acc[k0:k0 + tq, :] += dv_j * masks[hd]
                dc_ref[0, hd:hd + 1, k0:k0 + tq] = dc_j

        for qi in range(nq):
            q0 = qi * tq
            sums = row_sum[q0:q0 + tq, :].T
            for hd in range(2):
                dc_ref[0, hd:hd + 1, q0:q0 + tq] += sums[hd * HEAD_DIM:hd * HEAD_DIM + 1, :]

        dq_raw, dgq = _head_norm_bwd(dq_acc[...] * SCALE, qhat, rq, gq_ref[...], ones)
        dk_raw, dgk = _head_norm_bwd(dk_acc[...] * LN2, khat, rk, gk_ref[...], ones)
        dq_ref[...] = dq_raw.astype(BF16)
        dk_ref[...] = dk_raw.astype(BF16)
        dv_ref[...] = dv_acc[...].astype(BF16)
        dg_ref[0:1, :] += dgq
        dg_ref[1:2, :] += dgk

    blk = lambda off: pl.BlockSpec((seq, LANES), lambda b, p: (b, off + p))
    vec = pl.BlockSpec((1, LANES), lambda b, p: (0, 0))
    c_spec = pl.BlockSpec((1, 2, seq), lambda b, p: (p, 0, b))
    return _host_call(
        body, rider, name="fox_bwd", grid=(nb, npair),
        in_specs=[blk(0), blk(npair), blk(2 * npair), c_spec, vec, vec, blk(0), blk(0), blk(0)],
        out_specs=[blk(0), blk(0), blk(0), c_spec, pl.BlockSpec((8, LANES), lambda b, p: (0, 0))],
        out_shape=[jax.ShapeDtypeStruct((t, W_GROUP), BF16), jax.ShapeDtypeStruct((t, W_GROUP), BF16),
                   jax.ShapeDtypeStruct((t, W_GROUP), BF16), jax.ShapeDtypeStruct((npair, 2, t), F32),
                   jax.ShapeDtypeStruct((8, LANES), F32)],
        scratch_shapes=[pltpu.VMEM((seq, LANES), BF16), pltpu.VMEM((2, seq, LANES), BF16),
                        pltpu.VMEM((2, seq, LANES), BF16), pltpu.VMEM((seq, LANES), BF16),
                        pltpu.VMEM((seq, LANES), F32), pltpu.VMEM((seq, LANES), F32),
                        pltpu.VMEM((seq, LANES), F32), pltpu.VMEM((seq, LANES), F32),
                        pltpu.VMEM((seq, LANES), F32)],
        inputs=(proj, proj, proj, c3, gq, gk, do, o, lse), semantics=("arbitrary", "arbitrary"))


def _dil_prep(q_ref, k_ref, gq_ref, gk_ref, cos_ref, up_ref, dn_ref, ones):
    qhat, rq = _head_norm(q_ref[...].astype(F32), None, ones)
    khat, rk = _head_norm(k_ref[...].astype(F32), None, ones)
    cos, up, dn = cos_ref[...], up_ref[...], dn_ref[...]
    qn = _rope(qhat * gq_ref[...], cos, up, dn) * (SCALE * LOG2E)
    kn = _rope(khat * gk_ref[...], cos, up, dn)
    return qhat, rq, khat, rk, qn, kn


def _dil_keys(d, seq, kp, vp, kw, vw):
    nblk = seq // BAND
    per_res = seq // (d * BAND)
    as_blocks = lambda ref, rows: ref[rows, :].reshape(-1, BAND, LANES)
    if per_res == 1:
        a = lax.broadcasted_iota(jnp.int32, (1, BAND, BAND), 1)
        j = lax.broadcasted_iota(jnp.int32, (1, BAND, BAND), 2)
        causal = jnp.where(j <= a, 0.0, NEG)
        return as_blocks(kp, slice(0, seq)), as_blocks(vp, slice(0, seq)), [causal]
    for src, dst in ((kp, kw), (vp, vw)):
        dst[:, BAND:, :] = as_blocks(src, slice(0, seq))
        dst[1:, :BAND, :] = as_blocks(src, slice(0, seq - BAND))
        dst[0:1, :BAND, :] = jnp.zeros((1, BAND, LANES), BF16)
    a = lax.broadcasted_iota(jnp.int32, (1, BAND, 2 * BAND), 1)
    j = lax.broadcasted_iota(jnp.int32, (1, BAND, 2 * BAND), 2)
    band = jnp.where(((j < BAND) & (j >= a)) | ((j >= BAND) & (j - BAND <= a)), 0.0, NEG)
    e = lax.broadcasted_iota(jnp.int32, (nblk, 1, 2 * BAND), 0)
    j = lax.broadcasted_iota(jnp.int32, (nblk, 1, 2 * BAND), 2)
    no_prev = jnp.where(((e & (per_res - 1)) == 0) & (j < BAND), NEG, 0.0)
    return kw[...], vw[...], [band, no_prev]


def _residues(d, seq):
    n = seq // d
    if d == 1:
        return [(slice(0, seq), slice(0, seq))]
    return [(pl.ds(r, n, stride=d), slice(r * n, (r + 1) * n)) for r in range(d)]


def _dil_fwd(proj, gq, gk, cos, up, dn, nb, seq):
    t = nb * seq
    npair = W_GROUP // LANES
    off = 3 * npair

    def body(q_ref, k_ref, v_ref, gq_ref, gk_ref, cos_ref, up_ref, dn_ref, o_ref, lse_ref,
             qs, ks, vs, qp, kp, vp, kw, vw, m_b, l_b, o_b, m_s, l_s, o_s):
        ones = _group_ones()
        masks = _head_masks()
        _, _, _, _, qn, kn = _dil_prep(q_ref, k_ref, gq_ref, gk_ref, cos_ref, up_ref, dn_ref, ones)
        qs[...] = qn
        ks[...] = kn
        vs[...] = v_ref[...].astype(F32)
        nblk = seq // BAND

        for d in DILATIONS:
            for tok, res in _residues(d, seq):
                qv = qs[tok, :]
                for hd in range(2):
                    qp[hd, res, :] = (qv * masks[hd]).astype(BF16)
                kp[res, :] = ks[tok, :].astype(BF16)
                vp[res, :] = vs[tok, :].astype(BF16)
            keys_k, keys_v, bias = _dil_keys(d, seq, kp, vp, kw, vw)
            m_t = jnp.zeros((nblk, BAND, LANES), F32)
            l_t = jnp.zeros((nblk, BAND, LANES), F32)
            o_t = jnp.zeros((nblk, BAND, LANES), F32)
            for hd in range(2):
                s = _dot(qp[hd].reshape(nblk, BAND, LANES), keys_k, BATCH_NT)
                for b_ in bias:
                    s = s + b_
                m = jnp.max(s, axis=-1, keepdims=True)
                p = jnp.exp2(s - m)
                m_t = m_t + m * masks[hd]
                l_t = l_t + jnp.sum(p, axis=-1, keepdims=True) * masks[hd]
                o_t = o_t + _dot(p.astype(BF16), keys_v, BATCH_NN) * masks[hd]
            m_b[...] = m_t.reshape(seq, LANES)
            l_b[...] = l_t.reshape(seq, LANES)
            o_b[...] = o_t.reshape(seq, LANES)
            for tok, res in _residues(d, seq):
                if d == DILATIONS[0]:
                    m_s[tok, :] = m_b[res, :]
                    l_s[tok, :] = l_b[res, :]
                    o_s[tok, :] = o_b[res, :]
                else:
                    m_old = m_s[tok, :]
                    m_new = jnp.maximum(m_old, m_b[res, :])
                    w_old = jnp.exp2(m_old - m_new)
                    w_new = jnp.exp2(m_b[res, :] - m_new)
                    l_s[tok, :] = l_s[tok, :] * w_old + l_b[res, :] * w_new
                    o_s[tok, :] = o_s[tok, :] * w_old + o_b[res, :] * w_new
                    m_s[tok, :] = m_new

        l = l_s[...]
        o_ref[...] = o_s[...] / l
        lse_ref[...] = m_s[...] + jnp.log2(l)

    blk = lambda o_: pl.BlockSpec((seq, LANES), lambda b, p: (b, o_ + p))
    vec = pl.BlockSpec((1, LANES), lambda b, p: (0, 0))
    tab = pl.BlockSpec((seq, LANES), lambda b, p: (0, 0))
    f32_buf = pltpu.VMEM((seq, LANES), F32)
    bf16_buf = pltpu.VMEM((seq, LANES), BF16)
    window_buf = pltpu.VMEM((seq // BAND, 2 * BAND, LANES), BF16)
    return pl.pallas_call(
        body, name="dil_fwd", grid=(nb, npair),
        in_specs=[blk(off), blk(off + npair), blk(off + 2 * npair), vec, vec, tab, tab, tab],
        out_specs=[blk(0), blk(0)],
        out_shape=[jax.ShapeDtypeStruct((t, W_GROUP), F32), jax.ShapeDtypeStruct((t, W_GROUP), F32)],
        scratch_shapes=[f32_buf, f32_buf, f32_buf, pltpu.VMEM((2, seq, LANES), BF16), bf16_buf, bf16_buf,
                        window_buf, window_buf, f32_buf, f32_buf, f32_buf, f32_buf, f32_buf, f32_buf],
        compiler_params=_params(("arbitrary", "arbitrary")),
    )(proj, proj, proj, gq, gk, cos, up, dn)


def _dil_bwd(proj, gq, gk, cos, up, dn, do, o, lse, nb, seq, rider=None):
    t = nb * seq
    npair = W_GROUP // LANES
    off = 3 * npair

    def body(q_ref, k_ref, v_ref, gq_ref, gk_ref, cos_ref, up_ref, dn_ref, do_ref, o_ref, lse_ref,
             dq_ref, dk_ref, dv_ref, dg_ref, qs, ks, vs, delta, dq_s, dk_s, dv_s,
             qp, kp, vp, dop, kw, vw, lse_p, delta_p, dq_p, dk_p, dv_p):
        @pl.when((pl.program_id(0) == 0) & (pl.program_id(1) == 0))
        def _():
            dg_ref[...] = jnp.zeros_like(dg_ref)

        ones = _group_ones()
        masks = _head_masks()
        qhat, rq, khat, rk, qn, kn = _dil_prep(q_ref, k_ref, gq_ref, gk_ref, cos_ref, up_ref, dn_ref, ones)
        qs[...] = qn
        ks[...] = kn
        vs[...] = v_ref[...].astype(F32)
        delta[...] = _groupsum(do_ref[...] * o_ref[...], ones)
        nblk = seq // BAND

        for d in DILATIONS:
            for tok, res in _residues(d, seq):
                qv = qs[tok, :]
                dov = do_ref[tok, :]
                for hd in range(2):
                    qp[hd, res, :] = (qv * masks[hd]).astype(BF16)
                    dop[hd, res, :] = (dov * masks[hd]).astype(BF16)
                kp[res, :] = ks[tok, :].astype(BF16)
                vp[res, :] = vs[tok, :].astype(BF16)
                lse_p[res, :] = lse_ref[tok, :]
                delta_p[res, :] = delta[tok, :]
            keys_k, keys_v, bias = _dil_keys(d, seq, kp, vp, kw, vw)
            nk = keys_k.shape[1]
            dq_b = jnp.zeros((nblk, BAND, LANES), F32)
            dk_b = jnp.zeros((nblk, nk, LANES), F32)
            dv_b = jnp.zeros((nblk, nk, LANES), F32)
            for hd in range(2):
                lane0 = hd * HEAD_DIM
                q3 = qp[hd].reshape(nblk, BAND, LANES)
                do3 = dop[hd].reshape(nblk, BAND, LANES)
                z = _dot(q3, keys_k, BATCH_NT)
                for b_ in bias:
                    z = z + b_
                p = jnp.exp2(z - lse_p[...].reshape(nblk, BAND, LANES)[:, :, lane0:lane0 + 1])
                dp = _dot(do3, keys_v, BATCH_NT)
                ds = (p * (dp - delta_p[...].reshape(nblk, BAND, LANES)[:, :, lane0:lane0 + 1])).astype(BF16)
                dq_b = dq_b + _dot(ds, keys_k, BATCH_NN) * masks[hd]
                dk_b = dk_b + _dot(ds, q3, BATCH_TN)
                dv_b = dv_b + _dot(p.astype(BF16), do3, BATCH_TN)
            dq_p[...] = dq_b.reshape(seq, LANES)
            for acc, out in ((dk_b, dk_p), (dv_b, dv_p)):
                out[...] = acc[:, nk - BAND:, :].reshape(seq, LANES)
                if nk > BAND:
                    out[0:seq - BAND, :] += acc[1:, :BAND, :].reshape(seq - BAND, LANES)
            for tok, res in _residues(d, seq):
                if d == DILATIONS[0]:
                    dq_s[tok, :] = dq_p[res, :]
                    dk_s[tok, :] = dk_p[res, :]
                    dv_s[tok, :] = dv_p[res, :]
                else:
                    dq_s[tok, :] += dq_p[res, :]
                    dk_s[tok, :] += dk_p[res, :]
                    dv_s[tok, :] += dv_p[res, :]

        cos, up, dn = cos_ref[...], up_ref[...], dn_ref[...]
        dq_raw, dgq = _head_norm_bwd(_rope_bwd(dq_s[...] * SCALE, cos, up, dn), qhat, rq, gq_ref[...], ones)
        dk_raw, dgk = _head_norm_bwd(_rope_bwd(dk_s[...] * LN2, cos, up, dn), khat, rk, gk_ref[...], ones)
        dq_ref[...] = dq_raw.astype(BF16)
        dk_ref[...] = dk_raw.astype(BF16)
        dv_ref[...] = dv_s[...].astype(BF16)
        dg_ref[0:1, :] += dgq
        dg_ref[1:2, :] += dgk

    blk = lambda o_: pl.BlockSpec((seq, LANES), lambda b, p: (b, o_ + p))
    vec = pl.BlockSpec((1, LANES), lambda b, p: (0, 0))
    tab = pl.BlockSpec((seq, LANES), lambda b, p: (0, 0))
    f32_buf = pltpu.VMEM((seq, LANES), F32)
    bf16_buf = pltpu.VMEM((seq, LANES), BF16)
    window_buf = pltpu.VMEM((seq // BAND, 2 * BAND, LANES), BF16)
    bf16_pair = pltpu.VMEM((2, seq, LANES), BF16)
    return _host_call(
        body, rider, name="dil_bwd", grid=(nb, npair),
        in_specs=[blk(off), blk(off + npair), blk(off + 2 * npair), vec, vec, tab, tab, tab,
                  blk(0), blk(0), blk(0)],
        out_specs=[blk(0), blk(0), blk(0), pl.BlockSpec((8, LANES), lambda b, p: (0, 0))],
        out_shape=[jax.ShapeDtypeStruct((t, W_GROUP), BF16), jax.ShapeDtypeStruct((t, W_GROUP), BF16),
                   jax.ShapeDtypeStruct((t, W_GROUP), BF16), jax.ShapeDtypeStruct((8, LANES), F32)],
        scratch_shapes=[f32_buf] * 7 + [bf16_pair, bf16_buf, bf16_buf, bf16_pair, window_buf, window_buf]
        + [f32_buf] * 5,
        inputs=(proj, proj, proj, gq, gk, cos, up, dn, do, o, lse), semantics=("arbitrary", "arbitrary"))


def _adamw(w, g, m, v, name, rider=None):
    rows, cols = w.shape[-2:]
    tr = _row_tile(rows) if rows >= 8 else rows
    c1 = 1.0 - ADAM_B1 ** ADAM_STEP
    c2 = 1.0 - ADAM_B2 ** ADAM_STEP

    def body(w_ref, g_ref, m_ref, v_ref, d_ref, nm_ref, nv_ref):
        g_ = g_ref[...]
        nm = ADAM_B1 * m_ref[...] + (1.0 - ADAM_B1) * g_
        nv = ADAM_B2 * v_ref[...] + (1.0 - ADAM_B2) * (g_ * g_)
        nm_ref[...] = nm
        nv_ref[...] = nv
        d_ref[...] = -ADAM_LR * ((nm / c1) / (jnp.sqrt(nv / c2) + ADAM_EPS) + ADAM_WD * w_ref[...])

    if w.ndim == 3:
        spec = pl.BlockSpec((1, tr, cols), lambda i: (0, i, 0))
    else:
        spec = pl.BlockSpec((tr, cols), lambda i: (i, 0))
    shape = jax.ShapeDtypeStruct(w.shape, F32)
    return _host_call(
        body, rider, name=name, grid=(rows // tr,), in_specs=[spec] * 4, out_specs=[spec] * 3,
        out_shape=[shape] * 3, scratch_shapes=[], inputs=(w, g, m, v), semantics=("arbitrary",))


def _place():
    x, y, c = lax.axis_index("x"), lax.axis_index("y"), lax.axis_index("c")
    chips = [(1 - x, y), (x, 1 - y), (1 - x, 1 - y)]
    return x, y, c, chips


def _gather_weight(w, name):
    _, rows, cols = w.shape
    half_rows = rows // 2

    def body(w_ref, out_ref, send_sems, recv_sems):
        x, y, c, chips = _place()
        sibling = (x, y, 1 - c)
        mine = 2 * x + y
        lo = pl.multiple_of(c * half_rows, 16)
        lo_sib = pl.multiple_of((1 - c) * half_rows, 16)
        out_ref[mine] = w_ref[0].astype(BF16)

        def copy(k, shard, first_row, to):
            ref = out_ref.at[shard, pl.ds(first_row, half_rows), :]
            return pltpu.make_async_remote_copy(src_ref=ref, dst_ref=ref, send_sem=send_sems.at[k],
                                                recv_sem=recv_sems.at[k], device_id=to, device_id_type=MESH)

        sends = [copy(k, mine, lo, (cx, cy, c)) for k, (cx, cy) in enumerate(chips)]
        for cp in sends:
            cp.start()
        passed = []
        for k, (cx, cy) in enumerate(chips):
            theirs = 2 * cx + cy
            copy(k, theirs, lo, (cx, cy, c)).wait_recv()
            fw = copy(3 + k, theirs, lo, sibling)
            fw.start()
            passed.append(fw)
        for k, (cx, cy) in enumerate(chips):
            copy(3 + k, 2 * cx + cy, lo_sib, sibling).wait_recv()
        for cp in sends + passed:
            cp.wait_send()

    return pl.pallas_call(
        body, name=name,
        in_specs=[pl.BlockSpec(memory_space=pltpu.VMEM)],
        out_specs=pl.BlockSpec(memory_space=pltpu.VMEM),
        out_shape=jax.ShapeDtypeStruct((4, rows, cols), BF16),
        scratch_shapes=[pltpu.SemaphoreType.DMA((6,)), pltpu.SemaphoreType.DMA((6,))],
        compiler_params=pltpu.CompilerParams(vmem_limit_bytes=VMEM_LIMIT),
    )(w)


def _remote(src, dst, sems, k, to):
    send_sems, recv_sems = sems
    return pltpu.make_async_remote_copy(src_ref=src, dst_ref=dst, send_sem=send_sems.at[k], recv_sem=recv_sems.at[k],
                                        device_id=to, device_id_type=MESH)


def _pack_bf16(parts, name):
    rows = [p.shape[1] for p in parts]
    cols = parts[0].shape[2]

    def body(*refs):
        out_ref, first = refs[-1], 0
        for ref, r in zip(refs[:-1], rows):
            out_ref[first:first + r, :] = ref[0].astype(BF16)
            first += r

    return pl.pallas_call(
        body, name=name, in_specs=[pl.BlockSpec(memory_space=pltpu.VMEM)] * len(parts),
        out_specs=pl.BlockSpec(memory_space=pltpu.VMEM),
        out_shape=jax.ShapeDtypeStruct((sum(rows), cols), BF16),
        compiler_params=pltpu.CompilerParams(vmem_limit_bytes=VMEM_LIMIT),
    )(*parts)


def _gather_rider(packed):
    rows, cols = packed.shape
    half = rows // 2

    def copies(ins, outs, sems, finishing):
        p_ref, g_ref = ins[0], outs[0]
        x, y, c, chips = _place()
        sibling = (x, y, 1 - c)
        mine = 2 * x + y
        lo = pl.multiple_of(c * half, 16)
        lo_sib = pl.multiple_of((1 - c) * half, 16)
        spot = lambda shard, first: g_ref.at[shard, pl.ds(first, half), :]
        own = pltpu.make_async_copy(p_ref, g_ref.at[mine], sems[0].at[6])
        sends = [_remote(p_ref.at[pl.ds(lo, half), :], spot(mine, lo), sems, k, (cx, cy, c))
                 for k, (cx, cy) in enumerate(chips)]
        if not finishing:
            return own, sends
        arrivals = [_remote(spot(2 * cx + cy, lo), spot(2 * cx + cy, lo), sems, k, (cx, cy, c))
                    for k, (cx, cy) in enumerate(chips)]
        passes = [_remote(spot(2 * cx + cy, lo), spot(2 * cx + cy, lo), sems, 3 + k, sibling)
                  for k, (cx, cy) in enumerate(chips)]
        from_sibling = [_remote(spot(2 * cx + cy, lo_sib), spot(2 * cx + cy, lo_sib), sems, 3 + k, sibling)
                        for k, (cx, cy) in enumerate(chips)]
        return own, sends, arrivals, passes, from_sibling

    def start(ins, outs, send_sems, recv_sems):
        own, sends = copies(ins, outs, (send_sems, recv_sems), False)
        own.start()
        for cp in sends:
            cp.start()

    def finish(ins, outs, send_sems, recv_sems):
        own, sends, arrivals, passes, from_sibling = copies(ins, outs, (send_sems, recv_sems), True)
        for landed, onward in zip(arrivals, passes):
            landed.wait_recv()
            onward.start()
        for cp in from_sibling:
            cp.wait_recv()
        for cp in sends + passes:
            cp.wait_send()
        own.wait()

    return _Rider([packed], [jax.ShapeDtypeStruct((4, rows, cols), BF16)], 7, start, finish)


def _exchange_rider(inputs, out_shapes, n_sems, copies, aliases=None):
    def start(ins, outs, send_sems, recv_sems):
        for cp in copies(ins, outs, (send_sems, recv_sems)):
            cp.start()

    def finish(ins, outs, send_sems, recv_sems):
        for cp in copies(ins, outs, (send_sems, recv_sems)):
            cp.wait()

    return _Rider(inputs, out_shapes, n_sems, start, finish, aliases)


def _swap_rider(grads4):
    halves = [g.shape[1] // 2 for g in grads4]

    def copies(ins, outs, sems):
        x, y, c, _ = _place()
        return [_remote(g.at[:, pl.ds(pl.multiple_of((1 - c) * h, 8), h), :], a, sems, i, (x, y, 1 - c))
                for i, (g, a, h) in enumerate(zip(ins, outs, halves))]

    shapes = [jax.ShapeDtypeStruct((4, h, g.shape[2]), F32) for g, h in zip(grads4, halves)]
    return _exchange_rider(grads4, shapes, len(grads4), copies)


def _chip_sum(g4, from_sibling, name):
    _, rows, cols = g4.shape
    half = rows // 2

    def body(g_ref, s_ref, stage_ref, own_ref):
        x, y, c, chips = _place()
        lo = pl.multiple_of(c * half, 8)
        for k, (cx, cy) in enumerate(chips):
            theirs = 2 * cx + cy
            stage_ref[k] = (g_ref[theirs, pl.ds(lo, half), :] + s_ref[theirs]).astype(BF16)
        mine = 2 * x + y
        own_ref[...] = g_ref[mine, pl.ds(lo, half), :] + s_ref[mine]

    return pl.pallas_call(
        body, name=name, in_specs=[pl.BlockSpec(memory_space=pltpu.VMEM)] * 2,
        out_specs=[pl.BlockSpec(memory_space=pltpu.VMEM)] * 2,
        out_shape=[jax.ShapeDtypeStruct((3, half, cols), BF16), jax.ShapeDtypeStruct((half, cols), F32)],
        compiler_params=pltpu.CompilerParams(vmem_limit_bytes=VMEM_LIMIT),
    )(g4, from_sibling)


def _spread_rider(stages):
    def copies(ins, outs, sems):
        _, _, c, chips = _place()
        return [_remote(st.at[k], ld.at[k], sems, 3 * i + k, (cx, cy, c))
                for i, (st, ld) in enumerate(zip(ins, outs)) for k, (cx, cy) in enumerate(chips)]

    shapes = [jax.ShapeDtypeStruct(s.shape, s.dtype) for s in stages]
    return _exchange_rider(stages, shapes, 3 * len(stages), copies)


def _finish_half(own, landed, name):
    half, cols = own.shape

    def body(own_ref, landed_ref, out_ref):
        c = lax.axis_index("c")
        acc = own_ref[...]
        for k in range(3):
            acc = acc + landed_ref[k].astype(F32)
        out_ref[pl.ds(pl.multiple_of(c * half, 8), half), :] = acc

    return pl.pallas_call(
        body, name=name, in_specs=[pl.BlockSpec(memory_space=pltpu.VMEM)] * 2,
        out_specs=pl.BlockSpec(memory_space=pltpu.VMEM),
        out_shape=jax.ShapeDtypeStruct((2 * half, cols), F32),
        compiler_params=pltpu.CompilerParams(vmem_limit_bytes=VMEM_LIMIT),
    )(own, landed)


def _share_rider(fulls):
    def copies(ins, outs, sems):
        x, y, c, _ = _place()
        out = []
        for i, full in enumerate(outs):
            half = full.shape[0] // 2
            rows = full.at[pl.ds(pl.multiple_of(c * half, 8), half), :]
            out.append(_remote(rows, rows, sems, i, (x, y, 1 - c)))
        return out

    def finish_copies(ins, outs, sems):
        x, y, c, _ = _place()
        out = []
        for i, full in enumerate(outs):
            half = full.shape[0] // 2
            mine = full.at[pl.ds(pl.multiple_of(c * half, 8), half), :]
            theirs = full.at[pl.ds(pl.multiple_of((1 - c) * half, 8), half), :]
            out.append((_remote(mine, mine, sems, i, (x, y, 1 - c)), _remote(theirs, theirs, sems, i, (x, y, 1 - c))))
        return out

    def start(ins, outs, send_sems, recv_sems):
        for cp in copies(ins, outs, (send_sems, recv_sems)):
            cp.start()

    def finish(ins, outs, send_sems, recv_sems):
        for sent, landed in finish_copies(ins, outs, (send_sems, recv_sems)):
            sent.wait_send()
            landed.wait_recv()

    shapes = [jax.ShapeDtypeStruct(f.shape, f.dtype) for f in fulls]
    return _Rider(fulls, shapes, len(fulls), start, finish, aliases={i: i for i in range(len(fulls))})


def _all_sum_small(v):
    shape = v.shape

    def body(v_ref, out_ref, buf, send_sems, recv_sems):
        x, y, c, _ = _place()
        me = 4 * x + 2 * y + c
        buf[me] = v_ref[...]
        flips = [(dx, dy, dc) for dx in (0, 1) for dy in (0, 1) for dc in (0, 1)][1:]

        def copy(k, slot, flip):
            dx, dy, dc = flip
            to = (1 - x if dx else x, 1 - y if dy else y, 1 - c if dc else c)
            return pltpu.make_async_remote_copy(src_ref=buf.at[slot], dst_ref=buf.at[slot], send_sem=send_sems.at[k],
                                                recv_sem=recv_sems.at[k], device_id=to, device_id_type=MESH)

        sends = [copy(k, me, flip) for k, flip in enumerate(flips)]
        for cp in sends:
            cp.start()
        for k, (dx, dy, dc) in enumerate(flips):
            sender = 4 * (1 - x if dx else x) + 2 * (1 - y if dy else y) + (1 - c if dc else c)
            copy(k, sender, (dx, dy, dc)).wait_recv()
        for cp in sends:
            cp.wait_send()
        total = buf[0]
        for i in range(1, 8):
            total = total + buf[i]
        out_ref[...] = total

    return pl.pallas_call(
        body, name="all_sum_small",
        in_specs=[pl.BlockSpec(memory_space=pltpu.VMEM)],
        out_specs=pl.BlockSpec(memory_space=pltpu.VMEM),
        out_shape=jax.ShapeDtypeStruct(shape, F32),
        scratch_shapes=[pltpu.VMEM((8,) + shape, F32), pltpu.SemaphoreType.DMA((7,)), pltpu.SemaphoreType.DMA((7,))],
    )(v)


SMALL = (("g_mix", 1024), ("g_ffn", 1024), ("g_out_fox", 512), ("g_out_dil", 512), ("g_q_fox", 64),
         ("g_k_fox", 64), ("g_q_dil", 64), ("g_k_dil", 64), ("b_forget", 8))
SMALL_PACKED = (32, LANES)


PACKED_ROWS = (256, 704, 704, 704)


def _local_grads(x, target, gains, w1, wft, dense, packed, nb, seq):
    tile2 = lambda g: jnp.tile(g, (1, 2))
    gq_f, gk_f, gq_d, gk_d = (tile2(gains[n]) for n in ("g_q_fox", "g_k_fox", "g_q_dil", "g_k_dil"))
    b_col = gains["b_forget"].reshape(N_FOX_HEADS, 1)
    cos, up, dn = _rope_tables(seq)
    npair = N_FOX_HEADS // 2

    proj, fa_row, h1, h1_t = _in_proj(x, gains["g_mix"], w1, wft)
    c_row = _gate_fwd(fa_row, b_col, seq)
    c3 = c_row.reshape(npair, 2, nb * seq)
    (o_fox, lse_fox), gathered = _fox_fwd(proj, c3, gq_f, gk_f, nb, seq,
                                          rider=None if packed is None else _gather_rider(packed))
    if packed is not None:
        first, dense = 0, []
        for r in PACKED_ROWS:
            dense.append(gathered[0][:, first:first + r, :].reshape(4 * r, -1))
            first += r
    w_out, w_gate, w_up, w_down = dense
    o_dil, lse_dil = _dil_fwd(proj, gq_d, gk_d, cos, up, dn, nb, seq)
    x1, o_n_t = _attn_out(o_fox, o_dil, x, gains["g_out_fox"], gains["g_out_dil"], w_out)
    a, u, dy, loss_parts = _ffn_fwd(x1, target, gains["g_ffn"], w_gate, w_up, w_down)
    loss = jnp.sum(loss_parts[:, 0, 0])

    dx1, s, da, du, h2, dg_ffn = _ffn_bwd(dy, a, u, x1, gains["g_ffn"], w_gate, w_up, w_down)
    d_w_down = _token_matmul(s, dy, "dw_down", 512, False)
    d_w_gate = _token_matmul(da, h2, "dw_gate", 512, False)
    d_w_up = _token_matmul(du, h2, "dw_up", 512, False)
    d_w_out = _token_matmul(o_n_t, dx1, "dw_out", 1024)
    names = ("w_out", "w_gate", "w_up", "w_down")
    grads4 = [g.reshape(4, -1, g.shape[1]) for g in (d_w_out, d_w_gate, d_w_up, d_w_down)]
    exchange = packed is not None
    (do_fox, do_dil, dg_of, dg_od), from_sibling = _attn_out_bwd(
        dx1, o_fox, o_dil, gains["g_out_fox"], gains["g_out_dil"], w_out,
        rider=_swap_rider(grads4) if exchange else None)
    if exchange:
        sums = [_chip_sum(g, s, "chip_sum_" + n) for g, s, n in zip(grads4, from_sibling, names)]
    (dq_f, dk_f, dv_f, dc3, dg_fox), landed = _fox_bwd(
        proj, c3, gq_f, gk_f, do_fox, o_fox, lse_fox, nb, seq,
        rider=_spread_rider([st for st, _ in sums]) if exchange else None)
    if exchange:
        halves = [_finish_half(own, ld, "finish_half_" + n) for (_, own), ld, n in zip(sums, landed, names)]
    (dq_d, dk_d, dv_d, dg_dil), reduced = _dil_bwd(
        proj, gq_d, gk_d, cos, up, dn, do_dil, o_dil, lse_dil, nb, seq,
        rider=_share_rider(halves) if exchange else None)
    if exchange:
        d_w_out, d_w_gate, d_w_up, d_w_down = reduced
    dfa_row, db = _gate_bwd(dc3.reshape(N_FOX_HEADS, nb * seq), fa_row, b_col, seq)
    dparts = [dq_f, dk_f, dv_f, dq_d, dk_d, dv_d]
    d_w1 = _token_matmul_parts(h1_t, dparts, "dw_in")
    d_wf = _row_matmul(dfa_row, h1, "dw_forget")
    fox_w = 3 * W_GROUP
    d_w_in = jnp.concatenate([d_w1[:, :fox_w], d_wf.T, d_w1[:, fox_w:]], axis=1)
    if exchange:
        shards = [_shards_of_columns(d_w_in)]
        _, from_sibling = _idle_host(_swap_rider(shards), "swap_w_in")
        stage, own = _chip_sum(shards[0], from_sibling[0], "chip_sum_w_in")
    (grad_x, dg_mix), landed = _in_proj_bwd(dparts, dfa_row, w1, wft, x, gains["g_mix"], dx1,
                                            rider=_spread_rider([stage]) if exchange else None)
    if exchange:
        d_w_in = _finish_half(own, landed[0], "finish_half_w_in")

    fold = lambda g2: (g2[:, :HEAD_DIM] + g2[:, HEAD_DIM:])
    small = {
        "g_mix": dg_mix[0:1], "g_ffn": dg_ffn[0:1], "g_out_fox": dg_of[0:1], "g_out_dil": dg_od[0:1],
        "g_q_fox": fold(dg_fox[0:1]), "g_k_fox": fold(dg_fox[1:2]),
        "g_q_dil": fold(dg_dil[0:1]), "g_k_dil": fold(dg_dil[1:2]),
        "b_forget": db[:, 0].reshape(1, N_FOX_HEADS),
    }
    big = {"w_in": d_w_in, "w_out": d_w_out, "w_gate": d_w_gate, "w_up": d_w_up, "w_down": d_w_down}
    return loss, grad_x, big, small


def _shards_of_columns(full, n=4):
    r, nc = full.shape
    return full.reshape(r, n, nc // n).transpose(1, 0, 2)


def _columns_of_shards(slabs):
    n, r, c = slabs.shape
    return slabs.transpose(1, 0, 2).reshape(r, n * c)


def kernel(x, g_mix, w_in, b_forget, g_q_fox, g_k_fox, g_q_dil, g_k_dil, g_out_fox, g_out_dil, w_out, g_ffn, w_gate, w_up, w_down, loss_target, m_g_mix, m_w_in, m_b_forget, m_g_q_fox, m_g_k_fox, m_g_q_dil, m_g_k_dil, m_g_out_fox, m_g_out_dil, m_w_out, m_g_ffn, m_w_gate, m_w_up, m_w_down, v_g_mix, v_w_in, v_b_forget, v_g_q_fox, v_g_k_fox, v_g_q_dil, v_g_k_dil, v_g_out_fox, v_g_out_dil, v_w_out, v_g_ffn, v_w_gate, v_w_up, v_w_down):
    nb, seq, d = x.shape
    weights = dict(g_mix=g_mix, w_in=w_in, b_forget=b_forget, g_q_fox=g_q_fox, g_k_fox=g_k_fox, g_q_dil=g_q_dil,
                   g_k_dil=g_k_dil, g_out_fox=g_out_fox, g_out_dil=g_out_dil, w_out=w_out, g_ffn=g_ffn,
                   w_gate=w_gate, w_up=w_up, w_down=w_down)
    m_in = dict(g_mix=m_g_mix, w_in=m_w_in, b_forget=m_b_forget, g_q_fox=m_g_q_fox, g_k_fox=m_g_k_fox,
                g_q_dil=m_g_q_dil, g_k_dil=m_g_k_dil, g_out_fox=m_g_out_fox, g_out_dil=m_g_out_dil, w_out=m_w_out,
                g_ffn=m_g_ffn, w_gate=m_w_gate, w_up=m_w_up, w_down=m_w_down)
    v_in = dict(g_mix=v_g_mix, w_in=v_w_in, b_forget=v_b_forget, g_q_fox=v_g_q_fox, g_k_fox=v_g_k_fox,
                g_q_dil=v_g_q_dil, g_k_dil=v_g_k_dil, g_out_fox=v_g_out_fox, g_out_dil=v_g_out_dil, w_out=v_w_out,
                g_ffn=v_g_ffn, w_gate=v_w_gate, w_up=v_w_up, w_down=v_w_down)
    order = ["g_mix", "w_in", "b_forget", "g_q_fox", "g_k_fox", "g_q_dil", "g_k_dil", "g_out_fox", "g_out_dil",
             "w_out", "g_ffn", "w_gate", "w_up", "w_down"]

    w_in_full = _columns_of_shards(_gather_weight(w_in, "gather_w_in"))
    fox_w = 3 * W_GROUP
    w1 = jnp.concatenate([w_in_full[:, :fox_w], w_in_full[:, fox_w + N_FOX_HEADS:]], axis=1)
    wft = w_in_full[:, fox_w:fox_w + N_FOX_HEADS].T
    swap = lambda a: jnp.transpose(a, (0, 2, 1))
    for n in ("w_gate", "w_up"):
        weights[n], m_in[n], v_in[n] = swap(weights[n]), swap(m_in[n]), swap(v_in[n])
    shards = _pack_bf16([weights[n] for n in ("w_out", "w_gate", "w_up", "w_down")], "pack_shards")

    gains = {n: weights[n] for n, _ in SMALL}
    loss, grad_x, big, small = _local_grads(
        x.reshape(nb * seq, d), loss_target.reshape(nb * seq, d), gains, w1, wft, None, shards, nb, seq)

    grads = {n: big[n][None] for n in ("w_out", "w_gate", "w_up", "w_down")}
    packed = jnp.concatenate([small[n].reshape(-1) for n, _ in SMALL] + [loss.reshape(1)])
    packed = jnp.pad(packed, (0, SMALL_PACKED[0] * SMALL_PACKED[1] - packed.shape[0])).reshape(SMALL_PACKED)
    summed = _all_sum_small(packed).reshape(-1)
    pos = 0
    for n, size in SMALL:
        grads[n] = summed[pos:pos + size].reshape(1, size)
        pos += size
    loss = summed[pos]

    deltas, new_m, new_v, grad_out = {}, {}, {}, {}
    for n in ["w_down"] + [n for n in order if n != "w_down"]:
        rider = _share_rider([big["w_in"]]) if n == "w_down" else None
        (deltas[n], new_m[n], new_v[n]), shared = _adamw(weights[n], grads[n], m_in[n], v_in[n], "adamw_" + n, rider)
        if rider is not None:
            grads["w_in"] = shared[0][None]
        grad_out[n] = grads[n]
    for n in ("w_gate", "w_up"):
        grad_out[n], deltas[n], new_m[n], new_v[n] = (swap(a) for a in (grad_out[n], deltas[n], new_m[n], new_v[n]))

    return (loss, grad_x.reshape(nb, seq, d), *[grad_out[n] for n in order], *[deltas[n] for n in order],
            *[new_m[n] for n in order], *[new_v[n] for n in order])
```

```python
import functools
import math

import numpy as np
import jax
import jax.numpy as jnp
from jax import lax
from jax.experimental import pallas as pl
from jax.experimental.pallas import tpu as pltpu

F32, BF16 = jnp.float32, jnp.bfloat16
MESH = pl.DeviceIdType.MESH

EPS = 1e-6
NEG = -1e30
HEAD_DIM = 64
SCALE = HEAD_DIM ** -0.5
LOG2E = math.log2(math.e)
LN2 = math.log(2.0)
ROPE_THETA = 500000.0
ROPE_DIM = HEAD_DIM // 4
LANES = 128
W_GROUP = 512
N_FOX_HEADS = 8
VMEM_LIMIT = 56 * 1024 * 1024
DILATIONS = (1, 4, 16)
BAND = 128

ADAM_LR, ADAM_B1, ADAM_B2, ADAM_EPS, ADAM_WD, ADAM_STEP = 0.001, 0.9, 0.999, 1e-08, 0.01, 10

NT = (((1,), (1,)), ((), ()))
TN = (((0,), (0,)), ((), ()))
BATCH_NT = (((2,), (2,)), ((0,), (0,)))
BATCH_NN = (((2,), (1,)), ((0,), (0,)))
BATCH_TN = (((1,), (1,)), ((0,), (0,)))


def _params(sem=None):
    return pltpu.CompilerParams(dimension_semantics=sem, vmem_limit_bytes=VMEM_LIMIT)


def _dot(a, b, dims=None):
    if dims is None:
        return jnp.dot(a, b, preferred_element_type=F32)
    return lax.dot_general(a, b, dims, preferred_element_type=F32)


def _group_ones():
    i = lax.broadcasted_iota(jnp.int32, (LANES, LANES), 0) >> 6
    j = lax.broadcasted_iota(jnp.int32, (LANES, LANES), 1) >> 6
    return (i == j).astype(BF16)


def _split3(x):
    a = x.astype(BF16)
    r = x - a.astype(F32)
    b = r.astype(BF16)
    c = (r - b.astype(F32)).astype(BF16)
    return a, b, c


def _groupsum(x, ones):
    a, b, c = _split3(x)
    return _dot(a, ones) + _dot(b, ones) + _dot(c, ones)


def _head_masks():
    lane = lax.broadcasted_iota(jnp.int32, (1, LANES), 1)
    return [(lane < HEAD_DIM).astype(F32), (lane >= HEAD_DIM).astype(F32)]


def _head_norm(raw, gain, ones):
    r = lax.rsqrt(_groupsum(raw * raw, ones) * (1.0 / HEAD_DIM) + EPS)
    return raw * r, r


def _head_norm_bwd(dy, xhat, r, gain, ones):
    u = dy * gain
    dgain = jnp.sum(dy * xhat, axis=0, keepdims=True)
    draw = r * (u - xhat * (_groupsum(u * xhat, ones) * (1.0 / HEAD_DIM)))
    return draw, dgain


def _rope(x, cos, s_up, s_dn):
    return x * cos + pltpu.roll(x, LANES - 8, 1) * s_up + pltpu.roll(x, 8, 1) * s_dn


def _rope_bwd(dy, cos, s_up, s_dn):
    return dy * cos + pltpu.roll(dy * s_up, 8, 1) + pltpu.roll(dy * s_dn, LANES - 8, 1)


def _rope_tables(seq):
    half = ROPE_DIM // 2
    inv_freq = jnp.power(jnp.float32(ROPE_THETA), -jnp.arange(half, dtype=F32) * 2.0 / ROPE_DIM)
    ang = jnp.arange(seq).astype(F32)[:, None] * inv_freq[None, :]
    cos, sin = jnp.cos(ang), jnp.sin(ang)
    one = jnp.ones((seq, HEAD_DIM - ROPE_DIM), F32)
    zero_h = jnp.zeros((seq, half), F32)
    zero_r = jnp.zeros((seq, HEAD_DIM - ROPE_DIM), F32)
    c = jnp.concatenate([cos, cos, one], axis=1)
    up = jnp.concatenate([-sin, zero_h, zero_r], axis=1)
    dn = jnp.concatenate([zero_h, sin, zero_r], axis=1)
    return jnp.tile(c, (1, 2)), jnp.tile(up, (1, 2)), jnp.tile(dn, (1, 2))


def _row_tile(rows, cap=256):
    best = rows
    for t in range(8, min(rows, cap) + 1, 8):
        if rows % t == 0:
            best = t
    return best


class _Rider:
    def __init__(self, inputs, out_shapes, n_sems, start, finish, aliases=None, middle=None):
        self.inputs, self.out_shapes, self.n_sems = list(inputs), list(out_shapes), n_sems
        self.start, self.finish, self.middle, self.aliases = start, finish, middle, dict(aliases or {})


def _host_call(body, rider, *, name, grid, in_specs, out_specs, out_shape, scratch_shapes, inputs, semantics):
    if rider is None:
        return pl.pallas_call(body, name=name, grid=grid, in_specs=in_specs, out_specs=out_specs,
                              out_shape=out_shape, scratch_shapes=scratch_shapes,
                              compiler_params=_params(semantics))(*inputs), []
    n_in, n_out, n_scr = len(in_specs), len(out_specs), len(scratch_shapes)
    r_in, r_out = len(rider.inputs), len(rider.out_shapes)

    def wrapped(*refs):
        ins, refs = refs[:n_in], refs[n_in:]
        r_ins, refs = refs[:r_in], refs[r_in:]
        outs, refs = refs[:n_out], refs[n_out:]
        r_outs, refs = refs[:r_out], refs[r_out:]
        scratch, (send_sems, recv_sems) = refs[:n_scr], refs[n_scr:]
        ids = [pl.program_id(a) for a in range(len(grid))]
        first = functools.reduce(lambda p, q: p & q, [i == 0 for i in ids])
        last = functools.reduce(lambda p, q: p & q, [i == g - 1 for i, g in zip(ids, grid)])

        @pl.when(first)
        def _():
            rider.start(r_ins, r_outs, send_sems, recv_sems)

        body(*ins, *outs, *scratch)

        if rider.middle is not None:
            step, steps = ids[0], grid[0]
            for i, g in zip(ids[1:], grid[1:]):
                step, steps = step * g + i, steps * g

            @pl.when(step == (3 * steps) // 4)
            def _():
                rider.middle(r_ins, r_outs, send_sems, recv_sems)

        @pl.when(last)
        def _():
            rider.finish(r_ins, r_outs, send_sems, recv_sems)

    hbm = pl.BlockSpec(memory_space=pl.ANY)
    res = pl.pallas_call(
        wrapped, name=name, grid=grid,
        in_specs=list(in_specs) + [hbm] * r_in, out_specs=list(out_specs) + [hbm] * r_out,
        out_shape=list(out_shape) + rider.out_shapes,
        scratch_shapes=list(scratch_shapes) + [pltpu.SemaphoreType.DMA((rider.n_sems,))] * 2,
        input_output_aliases={n_in + i: n_out + o for i, o in rider.aliases.items()},
        compiler_params=_params(semantics),
    )(*inputs, *rider.inputs)
    return res[:n_out], res[n_out:]


def _idle_host(rider, name):
    def body(o_ref):
        o_ref[...] = jnp.zeros_like(o_ref)

    return _host_call(body, rider, name=name, grid=(1,), in_specs=[],
                      out_specs=[pl.BlockSpec((8, LANES), lambda i: (0, 0))],
                      out_shape=[jax.ShapeDtypeStruct((8, LANES), F32)], scratch_shapes=[], inputs=(),
                      semantics=("arbitrary",))


def _in_proj(x, g_mix, w1, wft):
    t, d = x.shape
    n = w1.shape[1]
    tt = 512

    def body(x_ref, g_ref, w_ref, wf_ref, p_ref, fa_ref, h_ref, ht_ref):
        xx = x_ref[...]
        r = lax.rsqrt(jnp.mean(xx * xx, axis=-1, keepdims=True) + EPS)
        h = (xx * r * g_ref[...]).astype(BF16)
        h_ref[...] = h
        ht_ref[...] = h.T
        for j in range(n // W_GROUP):
            cols = slice(j * W_GROUP, (j + 1) * W_GROUP)
            p_ref[:, cols] = _dot(h, w_ref[:, cols]).astype(BF16)
        fa_ref[...] = _dot(wf_ref[...], h, NT)

    return pl.pallas_call(
        body, name="in_proj", grid=(t // tt,),
        in_specs=[pl.BlockSpec((tt, d), lambda i: (i, 0)), pl.BlockSpec((1, d), lambda i: (0, 0)),
                  pl.BlockSpec(memory_space=pltpu.VMEM), pl.BlockSpec(memory_space=pltpu.VMEM)],
        out_specs=[pl.BlockSpec((tt, n), lambda i: (i, 0)), pl.BlockSpec((8, tt), lambda i: (0, i)),
                   pl.BlockSpec((tt, d), lambda i: (i, 0)), pl.BlockSpec((d, tt), lambda i: (0, i))],
        out_shape=[jax.ShapeDtypeStruct((t, n), BF16), jax.ShapeDtypeStruct((8, t), F32),
                   jax.ShapeDtypeStruct((t, d), BF16), jax.ShapeDtypeStruct((d, t), BF16)],
        compiler_params=_params(("arbitrary",)),
    )(x, g_mix, w1, wft)


def _tri(n, upper):
    i = lax.broadcasted_iota(jnp.int32, (n, n), 0)
    j = lax.broadcasted_iota(jnp.int32, (n, n), 1)
    return ((i <= j) if upper else (i >= j)).astype(BF16)


def _gate_fwd(fa_row, b_col, seq):
    t = fa_row.shape[1]
    cb = 256

    def body(fa_ref, b_ref, c_ref):
        tri = _tri(cb, True)
        carry = jnp.zeros((8, 1), F32)
        for k in range(seq // cb):
            z = fa_ref[:, k * cb:(k + 1) * cb] + b_ref[...]
            lf = jnp.minimum(z, 0.0) - jnp.log(1.0 + jnp.exp(-jnp.abs(z)))
            a, b, c = _split3(lf)
            blk = _dot(a, tri) + _dot(b, tri) + _dot(c, tri) + carry
            c_ref[:, k * cb:(k + 1) * cb] = blk
            carry = blk[:, cb - 1:cb]

    return pl.pallas_call(
        body, name="gate_fwd", grid=(t // seq,),
        in_specs=[pl.BlockSpec((8, seq), lambda i: (0, i)), pl.BlockSpec((8, 1), lambda i: (0, 0))],
        out_specs=pl.BlockSpec((8, seq), lambda i: (0, i)),
        out_shape=jax.ShapeDtypeStruct((8, t), F32),
        compiler_params=_params(("arbitrary",)),
    )(fa_row, b_col)


def _gate_bwd(dc_row, fa_row, b_col, seq):
    t = fa_row.shape[1]
    cb = 256

    def body(dc_ref, fa_ref, b_ref, dfa_ref, db_ref):
        @pl.when(pl.program_id(0) == 0)
        def _():
            db_ref[...] = jnp.zeros_like(db_ref)

        tri = _tri(cb, False)
        carry = jnp.zeros((8, 1), F32)
        dbs = jnp.zeros((8, 1), F32)
        for k in reversed(range(seq // cb)):
            a, b, c = _split3(dc_ref[:, k * cb:(k + 1) * cb])
            dlf = _dot(a, tri) + _dot(b, tri) + _dot(c, tri) + carry
            carry = dlf[:, 0:1]
            z = fa_ref[:, k * cb:(k + 1) * cb] + b_ref[...]
            dfa = dlf / (1.0 + jnp.exp(z))
            dfa_ref[:, k * cb:(k + 1) * cb] = dfa
            dbs = dbs + jnp.sum(dfa, axis=1, keepdims=True)
        db_ref[...] += jnp.broadcast_to(dbs, (8, LANES))

    return pl.pallas_call(
        body, name="gate_bwd", grid=(t // seq,),
        in_specs=[pl.BlockSpec((8, seq), lambda i: (0, i)), pl.BlockSpec((8, seq), lambda i: (0, i)),
                  pl.BlockSpec((8, 1), lambda i: (0, 0))],
        out_specs=[pl.BlockSpec((8, seq), lambda i: (0, i)), pl.BlockSpec((8, LANES), lambda i: (0, 0))],
        out_shape=[jax.ShapeDtypeStruct((8, t), F32), jax.ShapeDtypeStruct((8, LANES), F32)],
        compiler_params=_params(("arbitrary",)),
    )(dc_row, fa_row, b_col)


def _attn_out(o_fox, o_dil, x, g_fox, g_dil, w_out):
    t, d = x.shape
    w = o_fox.shape[1]
    tt = 512

    def body(of_ref, od_ref, x_ref, gf_ref, gd_ref, w_ref, x1_ref, ont_ref):
        acc = x_ref[...]
        for k, (o_ref, g_ref) in enumerate(((of_ref, gf_ref), (od_ref, gd_ref))):
            o = o_ref[...]
            r = lax.rsqrt(jnp.mean(o * o, axis=-1, keepdims=True) + EPS)
            on = (o * r * g_ref[...]).astype(BF16)
            ont_ref[k * w:(k + 1) * w, :] = on.T
            acc = acc + _dot(on, w_ref[k * w:(k + 1) * w, :])
        x1_ref[...] = acc

    return pl.pallas_call(
        body, name="attn_out", grid=(t // tt,),
        in_specs=[pl.BlockSpec((tt, w), lambda i: (i, 0)), pl.BlockSpec((tt, w), lambda i: (i, 0)),
                  pl.BlockSpec((tt, d), lambda i: (i, 0)), pl.BlockSpec((1, w), lambda i: (0, 0)),
                  pl.BlockSpec((1, w), lambda i: (0, 0)), pl.BlockSpec(memory_space=pltpu.VMEM)],
        out_specs=[pl.BlockSpec((tt, d), lambda i: (i, 0)), pl.BlockSpec((2 * w, tt), lambda i: (0, i))],
        out_shape=[jax.ShapeDtypeStruct((t, d), F32), jax.ShapeDtypeStruct((2 * w, t), BF16)],
        compiler_params=_params(("arbitrary",)),
    )(o_fox, o_dil, x, g_fox, g_dil, w_out)


def _attn_out_bwd(dx1, o_fox, o_dil, g_fox, g_dil, w_out, rider=None):
    t, d = dx1.shape
    w = o_fox.shape[1]
    tt = 512

    def body(dx_ref, of_ref, od_ref, gf_ref, gd_ref, w_ref, dof_ref, dod_ref, dgf_ref, dgd_ref):
        @pl.when(pl.program_id(0) == 0)
        def _():
            dgf_ref[...] = jnp.zeros_like(dgf_ref)
            dgd_ref[...] = jnp.zeros_like(dgd_ref)

        dxb = dx_ref[...].astype(BF16)
        for k, (o_ref, g_ref, do_ref, dg_ref) in enumerate(
                ((of_ref, gf_ref, dof_ref, dgf_ref), (od_ref, gd_ref, dod_ref, dgd_ref))):
            don = _dot(dxb, w_ref[k * w:(k + 1) * w, :], NT)
            o = o_ref[...]
            r = lax.rsqrt(jnp.mean(o * o, axis=-1, keepdims=True) + EPS)
            xhat = o * r
            u = don * g_ref[...]
            do_ref[...] = r * (u - xhat * jnp.mean(u * xhat, axis=-1, keepdims=True))
            dg_ref[0:1, :] += jnp.sum(don * xhat, axis=0, keepdims=True)

    return _host_call(
        body, rider, name="attn_out_bwd", grid=(t // tt,),
        in_specs=[pl.BlockSpec((tt, d), lambda i: (i, 0)), pl.BlockSpec((tt, w), lambda i: (i, 0)),
                  pl.BlockSpec((tt, w), lambda i: (i, 0)), pl.BlockSpec((1, w), lambda i: (0, 0)),
                  pl.BlockSpec((1, w), lambda i: (0, 0)), pl.BlockSpec(memory_space=pltpu.VMEM)],
        out_specs=[pl.BlockSpec((tt, w), lambda i: (i, 0)), pl.BlockSpec((tt, w), lambda i: (i, 0)),
                   pl.BlockSpec((8, w), lambda i: (0, 0)), pl.BlockSpec((8, w), lambda i: (0, 0))],
        out_shape=[jax.ShapeDtypeStruct((t, w), F32), jax.ShapeDtypeStruct((t, w), F32),
                   jax.ShapeDtypeStruct((8, w), F32), jax.ShapeDtypeStruct((8, w), F32)],
        scratch_shapes=[], inputs=(dx1, o_fox, o_dil, g_fox, g_dil, w_out), semantics=("arbitrary",))


def _ffn_fwd(x1, target, g_ffn, w_gate, w_up, w_down):
    t, d = x1.shape
    f = w_gate.shape[0]
    tt = 256

    def body(x_ref, t_ref, g_ref, wg_ref, wu_ref, wd_ref, a_ref, u_ref, dy_ref, loss_ref):
        xx = x_ref[...]
        r = lax.rsqrt(jnp.mean(xx * xx, axis=-1, keepdims=True) + EPS)
        h = (xx * r * g_ref[...]).astype(BF16)
        a = _dot(h, wg_ref[...], NT)
        u = _dot(h, wu_ref[...], NT)
        a_ref[...] = a.astype(BF16)
        u_ref[...] = u.astype(BF16)
        s = (a / (1.0 + jnp.exp(-a)) * u).astype(BF16)
        y = xx + _dot(s, wd_ref[...])
        e = y - t_ref[...]
        dy_ref[...] = e * (1.0 / d)
        loss_ref[...] = jnp.broadcast_to(0.5 * jnp.sum(e * e) * (1.0 / d), (1, 8, LANES))

    return pl.pallas_call(
        body, name="ffn_fwd", grid=(t // tt,),
        in_specs=[pl.BlockSpec((tt, d), lambda i: (i, 0)), pl.BlockSpec((tt, d), lambda i: (i, 0)),
                  pl.BlockSpec((1, d), lambda i: (0, 0)), pl.BlockSpec(memory_space=pltpu.VMEM),
                  pl.BlockSpec(memory_space=pltpu.VMEM), pl.BlockSpec(memory_space=pltpu.VMEM)],
        out_specs=[pl.BlockSpec((tt, f), lambda i: (i, 0)), pl.BlockSpec((tt, f), lambda i: (i, 0)),
                   pl.BlockSpec((tt, d), lambda i: (i, 0)), pl.BlockSpec((1, 8, LANES), lambda i: (i, 0, 0))],
        out_shape=[jax.ShapeDtypeStruct((t, f), BF16), jax.ShapeDtypeStruct((t, f), BF16),
                   jax.ShapeDtypeStruct((t, d), F32), jax.ShapeDtypeStruct((t // tt, 8, LANES), F32)],
        compiler_params=_params(("arbitrary",)),
    )(x1, target, g_ffn, w_gate, w_up, w_down)


def _ffn_bwd(dy, a, u, x1, g_ffn, w_gate, w_up, w_down):
    t, d = x1.shape
    f = w_gate.shape[0]
    tt = 256

    def body(dy_ref, a_ref, u_ref, x_ref, g_ref, wg_ref, wu_ref, wd_ref,
             dx_ref, s_ref, da_ref, du_ref, h_ref, dg_ref):
        @pl.when(pl.program_id(0) == 0)
        def _():
            dg_ref[...] = jnp.zeros_like(dg_ref)

        dy_ = dy_ref[...]
        ds = _dot(dy_.astype(BF16), wd_ref[...], NT)
        a_ = a_ref[...].astype(F32)
        u_ = u_ref[...].astype(F32)
        sig = 1.0 / (1.0 + jnp.exp(-a_))
        silu = a_ * sig
        s_ref[...] = (silu * u_).astype(BF16)
        da = (ds * u_ * (sig * (1.0 + a_ * (1.0 - sig)))).astype(BF16)
        du = (ds * silu).astype(BF16)
        da_ref[...] = da
        du_ref[...] = du
        dh = _dot(da, wg_ref[...]) + _dot(du, wu_ref[...])
        xx = x_ref[...]
        r = lax.rsqrt(jnp.mean(xx * xx, axis=-1, keepdims=True) + EPS)
        xhat = xx * r
        g = g_ref[...]
        h_ref[...] = (xhat * g).astype(BF16)
        uu = dh * g
        dx_ref[...] = dy_ + r * (uu - xhat * jnp.mean(uu * xhat, axis=-1, keepdims=True))
        dg_ref[0:1, :] += jnp.sum(dh * xhat, axis=0, keepdims=True)

    return pl.pallas_call(
        body, name="ffn_bwd", grid=(t // tt,),
        in_specs=[pl.BlockSpec((tt, d), lambda i: (i, 0)), pl.BlockSpec((tt, f), lambda i: (i, 0)),
                  pl.BlockSpec((tt, f), lambda i: (i, 0)), pl.BlockSpec((tt, d), lambda i: (i, 0)),
                  pl.BlockSpec((1, d), lambda i: (0, 0)), pl.BlockSpec(memory_space=pltpu.VMEM),
                  pl.BlockSpec(memory_space=pltpu.VMEM), pl.BlockSpec(memory_space=pltpu.VMEM)],
        out_specs=[pl.BlockSpec((tt, d), lambda i: (i, 0)), pl.BlockSpec((tt, f), lambda i: (i, 0)),
                   pl.BlockSpec((tt, f), lambda i: (i, 0)), pl.BlockSpec((tt, f), lambda i: (i, 0)),
                   pl.BlockSpec((tt, d), lambda i: (i, 0)), pl.BlockSpec((8, d), lambda i: (0, 0))],
        out_shape=[jax.ShapeDtypeStruct((t, d), F32), jax.ShapeDtypeStruct((t, f), BF16),
                   jax.ShapeDtypeStruct((t, f), BF16), jax.ShapeDtypeStruct((t, f), BF16),
                   jax.ShapeDtypeStruct((t, d), BF16), jax.ShapeDtypeStruct((8, d), F32)],
        compiler_params=_params(("arbitrary",)),
    )(dy, a, u, x1, g_ffn, w_gate, w_up, w_down)


def _in_proj_bwd(dparts, dfa_row, w1, wft, x, g_mix, dx1, rider=None):
    t, d = x.shape
    tt = 512
    npart = len(dparts)

    def body(*refs):
        dp_refs = refs[:npart]
        dfa_ref, w_ref, wf_ref, x_ref, g_ref, dx1_ref, dx_ref, dg_ref = refs[npart:]

        @pl.when(pl.program_id(0) == 0)
        def _():
            dg_ref[...] = jnp.zeros_like(dg_ref)

        dh = _dot(dfa_ref[...].astype(BF16), wf_ref[...], TN)
        for j in range(npart):
            dh = dh + _dot(dp_refs[j][...], w_ref[:, j * W_GROUP:(j + 1) * W_GROUP], NT)
        xx = x_ref[...]
        r = lax.rsqrt(jnp.mean(xx * xx, axis=-1, keepdims=True) + EPS)
        xhat = xx * r
        uu = dh * g_ref[...]
        dx_ref[...] = dx1_ref[...] + r * (uu - xhat * jnp.mean(uu * xhat, axis=-1, keepdims=True))
        dg_ref[0:1, :] += jnp.sum(dh * xhat, axis=0, keepdims=True)

    return _host_call(
        body, rider, name="in_proj_bwd", grid=(t // tt,),
        in_specs=[pl.BlockSpec((tt, W_GROUP), lambda i: (i, 0)) for _ in range(npart)]
        + [pl.BlockSpec((8, tt), lambda i: (0, i)), pl.BlockSpec(memory_space=pltpu.VMEM),
           pl.BlockSpec(memory_space=pltpu.VMEM), pl.BlockSpec((tt, d), lambda i: (i, 0)),
           pl.BlockSpec((1, d), lambda i: (0, 0)), pl.BlockSpec((tt, d), lambda i: (i, 0))],
        out_specs=[pl.BlockSpec((tt, d), lambda i: (i, 0)), pl.BlockSpec((8, d), lambda i: (0, 0))],
        out_shape=[jax.ShapeDtypeStruct((t, d), F32), jax.ShapeDtypeStruct((8, d), F32)],
        scratch_shapes=[], inputs=(*dparts, dfa_row, w1, wft, x, g_mix, dx1), semantics=("arbitrary",))


def _token_matmul(a, b, name, tn, a_is_transposed=True):
    m, t = a.shape if a_is_transposed else a.shape[::-1]
    n = b.shape[1]
    tk = 1024

    def body(a_ref, b_ref, o_ref):
        @pl.when(pl.program_id(1) == 0)
        def _():
            o_ref[...] = jnp.zeros_like(o_ref)

        o_ref[...] += _dot(a_ref[...], b_ref[...].astype(BF16), None if a_is_transposed else TN)

    a_spec = pl.BlockSpec((m, tk), lambda j, k: (0, k)) if a_is_transposed else pl.BlockSpec((tk, m), lambda j, k: (k, 0))
    return pl.pallas_call(
        body, name=name, grid=(n // tn, t // tk),
        in_specs=[a_spec, pl.BlockSpec((tk, tn), lambda j, k: (k, j))],
        out_specs=pl.BlockSpec((m, tn), lambda j, k: (0, j)),
        out_shape=jax.ShapeDtypeStruct((m, n), F32),
        compiler_params=_params(("arbitrary", "arbitrary")),
    )(a, b)


def _token_matmul_parts(at, parts, name):
    m, t = at.shape
    widths = [p.shape[1] for p in parts]
    tk = 1024

    def body(a_ref, *refs):
        o_ref = refs[-1]

        @pl.when(pl.program_id(0) == 0)
        def _():
            o_ref[...] = jnp.zeros_like(o_ref)

        a, first = a_ref[...], 0
        for b_ref, w in zip(refs[:-1], widths):
            o_ref[:, first:first + w] += _dot(a, b_ref[...])
            first += w

    return pl.pallas_call(
        body, name=name, grid=(t // tk,),
        in_specs=[pl.BlockSpec((m, tk), lambda k: (0, k))] + [pl.BlockSpec((tk, w), lambda k: (k, 0)) for w in widths],
        out_specs=pl.BlockSpec((m, sum(widths)), lambda k: (0, 0)),
        out_shape=jax.ShapeDtypeStruct((m, sum(widths)), F32),
        compiler_params=_params(("arbitrary",)),
    )(at, *parts)


def _row_matmul(a_row, b, name):
    t, n = b.shape
    tk = 1024
    nk = t // tk

    def body(a_ref, b_ref, o_ref):
        @pl.when(pl.program_id(0) == 0)
        def _():
            o_ref[...] = jnp.zeros_like(o_ref)

        o_ref[...] += _dot(a_ref[...].astype(BF16), b_ref[...])

    return pl.pallas_call(
        body, name=name, grid=(nk,),
        in_specs=[pl.BlockSpec((8, tk), lambda k: (0, k)), pl.BlockSpec((tk, n), lambda k: (k, 0))],
        out_specs=pl.BlockSpec((8, n), lambda k: (0, 0)),
        out_shape=jax.ShapeDtypeStruct((8, n), F32),
        compiler_params=_params(("arbitrary",)),
    )(a_row, b)


FOX_TQ = 256


def _fox_fwd(proj, c3, gq, gk, nb, seq, rider=None):
    t = nb * seq
    tq = FOX_TQ
    nq = seq // tq
    npair = N_FOX_HEADS // 2

    def body(q_ref, k_ref, v_ref, c_ref, gq_ref, gk_ref, o_ref, lse_ref, qs, ks, vs):
        ones = _group_ones()
        masks = _head_masks()
        qhat, _ = _head_norm(q_ref[...].astype(F32), None, ones)
        khat, _ = _head_norm(k_ref[...].astype(F32), None, ones)
        qs[...] = (qhat * gq_ref[...] * (SCALE * LOG2E)).astype(BF16)
        kn = khat * gk_ref[...]
        for hd in range(2):
            ks[hd] = (kn * masks[hd]).astype(BF16)
        vs[...] = v_ref[...]
        row = lax.broadcasted_iota(jnp.int32, (tq, tq), 0)
        col = lax.broadcasted_iota(jnp.int32, (tq, tq), 1)
        causal = col <= row

        for qi in range(nq):
            q0 = qi * tq
            q_blk = qs[q0:q0 + tq, :]
            o_tot = jnp.zeros((tq, LANES), F32)
            lse_tot = jnp.zeros((tq, LANES), F32)
            for hd in range(2):
                crow = c_ref[0, hd:hd + 1, 0:q0 + tq] * LOG2E
                c0 = crow[:, q0:q0 + 1]
                s_d = _dot(q_blk, ks[hd, q0:q0 + tq, :], NT) + (c0 - crow[:, q0:q0 + tq])
                s_d = jnp.where(causal, s_d, NEG)
                m = jnp.max(s_d, axis=-1, keepdims=True)
                if qi > 0:
                    s_o = _dot(q_blk, ks[hd, 0:q0, :], NT) + (c0 - crow[:, 0:q0])
                    m = jnp.maximum(m, jnp.max(s_o, axis=-1, keepdims=True))
                p_d = jnp.exp2(s_d - m)
                l = jnp.sum(p_d, axis=-1, keepdims=True)
                acc = _dot(p_d.astype(BF16), vs[q0:q0 + tq, :])
                if qi > 0:
                    p_o = jnp.exp2(s_o - m)
                    l = l + jnp.sum(p_o, axis=-1, keepdims=True)
                    acc = acc + _dot(p_o.astype(BF16), vs[0:q0, :])
                o_tot = o_tot + (acc / l) * masks[hd]
                lse_tot = lse_tot + (m + jnp.log2(l) - c0) * masks[hd]
            o_ref[q0:q0 + tq, :] = o_tot
            lse_ref[q0:q0 + tq, :] = lse_tot

    blk = lambda off: pl.BlockSpec((seq, LANES), lambda b, p: (b, off + p))
    return _host_call(
        body, rider, name="fox_fwd", grid=(nb, npair),
        in_specs=[blk(0), blk(npair), blk(2 * npair), pl.BlockSpec((1, 2, seq), lambda b, p: (p, 0, b)),
                  pl.BlockSpec((1, LANES), lambda b, p: (0, 0)), pl.BlockSpec((1, LANES), lambda b, p: (0, 0))],
        out_specs=[blk(0), blk(0)],
        out_shape=[jax.ShapeDtypeStruct((t, W_GROUP), F32), jax.ShapeDtypeStruct((t, W_GROUP), F32)],
        scratch_shapes=[pltpu.VMEM((seq, LANES), BF16), pltpu.VMEM((2, seq, LANES), BF16),
                        pltpu.VMEM((seq, LANES), BF16)],
        inputs=(proj, proj, proj, c3, gq, gk), semantics=("arbitrary", "arbitrary"))


def _fox_bwd(proj, c3, gq, gk, do, o, lse, nb, seq, rider=None):
    t = nb * seq
    tq = FOX_TQ
    nq = seq // tq
    npair = N_FOX_HEADS // 2

    def body(q_ref, k_ref, v_ref, c_ref, gq_ref, gk_ref, do_ref, o_ref, lse_ref,
             dq_ref, dk_ref, dv_ref, dc_ref, dg_ref, qs, ks, vs, kts, dos, lse_t, delta_t, dqt_acc, dk_acc, dv_acc,
             row_sum):
        @pl.when((pl.program_id(0) == 0) & (pl.program_id(1) == 0))
        def _():
            dg_ref[...] = jnp.zeros_like(dg_ref)

        ones = _group_ones()
        masks = _head_masks()
        qhat, rq = _head_norm(q_ref[...].astype(F32), None, ones)
        khat, rk = _head_norm(k_ref[...].astype(F32), None, ones)
        qs[...] = (qhat * gq_ref[...] * (SCALE * LOG2E)).astype(BF16)
        kn = khat * gk_ref[...]
        vv = v_ref[...].astype(F32)
        for hd in range(2):
            ks[hd] = (kn * masks[hd]).astype(BF16)
            vs[hd] = (vv * masks[hd]).astype(BF16)
            kts[hd] = ks[hd].T
        dof = do_ref[...]
        dos[...] = dof.astype(BF16)
        lse_t[...] = lse_ref[...].T
        delta_t[...] = _groupsum(dof * o_ref[...], ones).T
        dqt_acc[...] = jnp.zeros_like(dqt_acc)
        dk_acc[...] = jnp.zeros_like(dk_acc)
        dv_acc[...] = jnp.zeros_like(dv_acc)
        row_sum[...] = jnp.zeros_like(row_sum)
        key = lax.broadcasted_iota(jnp.int32, (tq, tq), 0)
        qry = lax.broadcasted_iota(jnp.int32, (tq, tq), 1)
        causal = key <= qry

        for hd in range(2):
            lane0 = hd * HEAD_DIM
            for kj in range(nq):
                k0 = kj * tq
                k_blk = ks[hd, k0:k0 + tq, :]
                v_blk = vs[hd, k0:k0 + tq, :]
                kt_blk = kts[hd, :, k0:k0 + tq]
                crow = c_ref[0, hd:hd + 1, k0:k0 + tq] * LOG2E
                ck0 = crow[:, 0:1]
                bias = jnp.broadcast_to(ck0 - crow, (LANES, tq)).T[:, 0:1]

                def queries_step(r0, r1, diag, hd=hd, lane0=lane0, k_blk=k_blk, v_blk=v_blk, kt_blk=kt_blk,
                                 bias=bias, ck0=ck0):
                    q_r = qs[r0:r1, :]
                    do_r = dos[r0:r1, :]
                    z = _dot(k_blk, q_r, NT) + bias
                    p = jnp.exp2(z - (lse_t[lane0:lane0 + 1, r0:r1] + ck0))
                    if diag:
                        p = jnp.where(causal, p, 0.0)
                    dp = _dot(v_blk, do_r, NT)
                    ds = p * (dp - delta_t[lane0:lane0 + 1, r0:r1])
                    dsb = ds.astype(BF16)
                    dqt_acc[:, r0:r1] += _dot(kt_blk, dsb)
                    row_sum[hd:hd + 1, r0:r1] += jnp.sum(ds, axis=0, keepdims=True)
                    return _dot(dsb, q_r), _dot(p.astype(BF16), do_r), -jnp.sum(ds, axis=1, keepdims=True)

                dk_j, dv_j, dc_j = queries_step(k0, k0 + tq, True)
                if k0 + tq < seq:
                    dk_o, dv_o, dc_o = queries_step(k0 + tq, seq, False)
                    dk_j, dv_j, dc_j = dk_j + dk_o, dv_j + dv_o, dc_j + dc_o
                dk_acc[k0:k0 + tq, :] += dk_j * masks[hd]
                dv_acc[k0:k0 + tq, :] += dv_j * masks[hd]
                dc_ref[0, hd:hd + 1, k0:k0 + tq] = jnp.broadcast_to(dc_j, (tq, LANES)).T[0:1, :]

        dc_ref[0] += row_sum[0:2, :]

        dq_raw, dgq = _head_norm_bwd(dqt_acc[...].T * SCALE, qhat, rq, gq_ref[...], ones)
        dk_raw, dgk = _head_norm_bwd(dk_acc[...] * LN2, khat, rk, gk_ref[...], ones)
        dq_ref[...] = dq_raw.astype(BF16)
        dk_ref[...] = dk_raw.astype(BF16)
        dv_ref[...] = dv_acc[...].astype(BF16)
        dg_ref[0:1, :] += dgq
        dg_ref[1:2, :] += dgk

    blk = lambda off: pl.BlockSpec((seq, LANES), lambda b, p: (b, off + p))
    vec = pl.BlockSpec((1, LANES), lambda b, p: (0, 0))
    c_spec = pl.BlockSpec((1, 2, seq), lambda b, p: (p, 0, b))
    return _host_call(
        body, rider, name="fox_bwd", grid=(nb, npair),
        in_specs=[blk(0), blk(npair), blk(2 * npair), c_spec, vec, vec, blk(0), blk(0), blk(0)],
        out_specs=[blk(0), blk(0), blk(0), c_spec, pl.BlockSpec((8, LANES), lambda b, p: (0, 0))],
        out_shape=[jax.ShapeDtypeStruct((t, W_GROUP), BF16), jax.ShapeDtypeStruct((t, W_GROUP), BF16),
                   jax.ShapeDtypeStruct((t, W_GROUP), BF16), jax.ShapeDtypeStruct((npair, 2, t), F32),
                   jax.ShapeDtypeStruct((8, LANES), F32)],
        scratch_shapes=[pltpu.VMEM((seq, LANES), BF16), pltpu.VMEM((2, seq, LANES), BF16),
                        pltpu.VMEM((2, seq, LANES), BF16), pltpu.VMEM((2, LANES, seq), BF16),
                        pltpu.VMEM((seq, LANES), BF16), pltpu.VMEM((LANES, seq), F32),
                        pltpu.VMEM((LANES, seq), F32), pltpu.VMEM((LANES, seq), F32),
                        pltpu.VMEM((seq, LANES), F32), pltpu.VMEM((seq, LANES), F32),
                        pltpu.VMEM((8, seq), F32)],
        inputs=(proj, proj, proj, c3, gq, gk, do, o, lse), semantics=("arbitrary", "arbitrary"))


def _dil_prep(q_ref, k_ref, gq_ref, gk_ref, cos_ref, up_ref, dn_ref, ones):
    qhat, rq = _head_norm(q_ref[...].astype(F32), None, ones)
    khat, rk = _head_norm(k_ref[...].astype(F32), None, ones)
    cos, up, dn = cos_ref[...], up_ref[...], dn_ref[...]
    qn = _rope(qhat * gq_ref[...], cos, up, dn) * (SCALE * LOG2E)
    kn = _rope(khat * gk_ref[...], cos, up, dn)
    return qhat, rq, khat, rk, qn, kn


def _dil_keys(d, seq, kp, vp, kw, vw):
    nblk = seq // BAND
    per_res = seq // (d * BAND)
    as_blocks = lambda ref, rows: ref[rows, :].reshape(-1, BAND, LANES)
    if per_res == 1:
        a = lax.broadcasted_iota(jnp.int32, (1, BAND, BAND), 1)
        j = lax.broadcasted_iota(jnp.int32, (1, BAND, BAND), 2)
        causal = jnp.where(j <= a, 0.0, NEG)
        return as_blocks(kp, slice(0, seq)), as_blocks(vp, slice(0, seq)), [causal]
    for src, dst in ((kp, kw), (vp, vw)):
        dst[:, BAND:, :] = as_blocks(src, slice(0, seq))
        dst[1:, :BAND, :] = as_blocks(src, slice(0, seq - BAND))
        dst[0:1, :BAND, :] = jnp.zeros((1, BAND, LANES), BF16)
    a = lax.broadcasted_iota(jnp.int32, (1, BAND, 2 * BAND), 1)
    j = lax.broadcasted_iota(jnp.int32, (1, BAND, 2 * BAND), 2)
    band = jnp.where(((j < BAND) & (j >= a)) | ((j >= BAND) & (j - BAND <= a)), 0.0, NEG)
    e = lax.broadcasted_iota(jnp.int32, (nblk, 1, 2 * BAND), 0)
    j = lax.broadcasted_iota(jnp.int32, (nblk, 1, 2 * BAND), 2)
    no_prev = jnp.where(((e & (per_res - 1)) == 0) & (j < BAND), NEG, 0.0)
    return kw[...], vw[...], [band, no_prev]


def _residues(d, seq):
    n = seq // d
    if d == 1:
        return [(slice(0, seq), slice(0, seq))]
    return [(pl.ds(r, n, stride=d), slice(r * n, (r + 1) * n)) for r in range(d)]


def _dil_fwd(proj, gq, gk, cos, up, dn, nb, seq):
    t = nb * seq
    npair = W_GROUP // LANES
    off = 3 * npair

    def body(q_ref, k_ref, v_ref, gq_ref, gk_ref, cos_ref, up_ref, dn_ref, o_ref, lse_ref,
             qs, ks, vs, qp, kp, vp, kw, vw, m_b, l_b, o_b, m_s, l_s, o_s):
        ones = _group_ones()
        masks = _head_masks()
        _, _, _, _, qn, kn = _dil_prep(q_ref, k_ref, gq_ref, gk_ref, cos_ref, up_ref, dn_ref, ones)
        qs[...] = qn
        ks[...] = kn
        vs[...] = v_ref[...].astype(F32)
        nblk = seq // BAND

        for d in DILATIONS:
            for tok, res in _residues(d, seq):
                qv = qs[tok, :]
                for hd in range(2):
                    qp[hd, res, :] = (qv * masks[hd]).astype(BF16)
                kp[res, :] = ks[tok, :].astype(BF16)
                vp[res, :] = vs[tok, :].astype(BF16)
            keys_k, keys_v, bias = _dil_keys(d, seq, kp, vp, kw, vw)
            m_t = jnp.zeros((nblk, BAND, LANES), F32)
            l_t = jnp.zeros((nblk, BAND, LANES), F32)
            o_t = jnp.zeros((nblk, BAND, LANES), F32)
            for hd in range(2):
                s = _dot(qp[hd].reshape(nblk, BAND, LANES), keys_k, BATCH_NT)
                for b_ in bias:
                    s = s + b_
                m = jnp.max(s, axis=-1, keepdims=True)
                p = jnp.exp2(s - m)
                m_t = m_t + m * masks[hd]
                l_t = l_t + jnp.sum(p, axis=-1, keepdims=True) * masks[hd]
                o_t = o_t + _dot(p.astype(BF16), keys_v, BATCH_NN) * masks[hd]
            m_b[...] = m_t.reshape(seq, LANES)
            l_b[...] = l_t.reshape(seq, LANES)
            o_b[...] = o_t.reshape(seq, LANES)
            for tok, res in _residues(d, seq):
                if d == DILATIONS[0]:
                    m_s[tok, :] = m_b[res, :]
                    l_s[tok, :] = l_b[res, :]
                    o_s[tok, :] = o_b[res, :]
                else:
                    m_old = m_s[tok, :]
                    m_new = jnp.maximum(m_old, m_b[res, :])
                    w_old = jnp.exp2(m_old - m_new)
                    w_new = jnp.exp2(m_b[res, :] - m_new)
                    l_s[tok, :] = l_s[tok, :] * w_old + l_b[res, :] * w_new
                    o_s[tok, :] = o_s[tok, :] * w_old + o_b[res, :] * w_new
                    m_s[tok, :] = m_new

        l = l_s[...]
        o_ref[...] = o_s[...] / l
        lse_ref[...] = m_s[...] + jnp.log2(l)

    blk = lambda o_: pl.BlockSpec((seq, LANES), lambda b, p: (b, o_ + p))
    vec = pl.BlockSpec((1, LANES), lambda b, p: (0, 0))
    tab = pl.BlockSpec((seq, LANES), lambda b, p: (0, 0))
    f32_buf = pltpu.VMEM((seq, LANES), F32)
    bf16_buf = pltpu.VMEM((seq, LANES), BF16)
    window_buf = pltpu.VMEM((seq // BAND, 2 * BAND, LANES), BF16)
    return pl.pallas_call(
        body, name="dil_fwd", grid=(nb, npair),
        in_specs=[blk(off), blk(off + npair), blk(off + 2 * npair), vec, vec, tab, tab, tab],
        out_specs=[blk(0), blk(0)],
        out_shape=[jax.ShapeDtypeStruct((t, W_GROUP), F32), jax.ShapeDtypeStruct((t, W_GROUP), F32)],
        scratch_shapes=[f32_buf, f32_buf, f32_buf, pltpu.VMEM((2, seq, LANES), BF16), bf16_buf, bf16_buf,
                        window_buf, window_buf, f32_buf, f32_buf, f32_buf, f32_buf, f32_buf, f32_buf],
        compiler_params=_params(("arbitrary", "arbitrary")),
    )(proj, proj, proj, gq, gk, cos, up, dn)


def _dil_bwd(proj, gq, gk, cos, up, dn, do, o, lse, nb, seq, rider=None):
    t = nb * seq
    npair = W_GROUP // LANES
    off = 3 * npair

    def body(q_ref, k_ref, v_ref, gq_ref, gk_ref, cos_ref, up_ref, dn_ref, do_ref, o_ref, lse_ref,
             dq_ref, dk_ref, dv_ref, dg_ref, qs, ks, vs, delta, dq_s, dk_s, dv_s,
             qp, kp, vp, dop, kw, vw, lse_p, delta_p, dq_p, dk_p, dv_p):
        @pl.when((pl.program_id(0) == 0) & (pl.program_id(1) == 0))
        def _():
            dg_ref[...] = jnp.zeros_like(dg_ref)

        ones = _group_ones()
        masks = _head_masks()
        qhat, rq, khat, rk, qn, kn = _dil_prep(q_ref, k_ref, gq_ref, gk_ref, cos_ref, up_ref, dn_ref, ones)
        qs[...] = qn
        ks[...] = kn
        vs[...] = v_ref[...].astype(F32)
        delta[...] = _groupsum(do_ref[...] * o_ref[...], ones)
        nblk = seq // BAND

        for d in DILATIONS:
            for tok, res in _residues(d, seq):
                qv = qs[tok, :]
                dov = do_ref[tok, :]
                for hd in range(2):
                    qp[hd, res, :] = (qv * masks[hd]).astype(BF16)
                    dop[hd, res, :] = (dov * masks[hd]).astype(BF16)
                kp[res, :] = ks[tok, :].astype(BF16)
                vp[res, :] = vs[tok, :].astype(BF16)
                lse_p[res, :] = lse_ref[tok, :]
                delta_p[res, :] = delta[tok, :]
            keys_k, keys_v, bias = _dil_keys(d, seq, kp, vp, kw, vw)
            nk = keys_k.shape[1]
            dq_b = jnp.zeros((nblk, BAND, LANES), F32)
            dk_b = jnp.zeros((nblk, nk, LANES), F32)
            dv_b = jnp.zeros((nblk, nk, LANES), F32)
            for hd in range(2):
                lane0 = hd * HEAD_DIM
                q3 = qp[hd].reshape(nblk, BAND, LANES)
                do3 = dop[hd].reshape(nblk, BAND, LANES)
                z = _dot(q3, keys_k, BATCH_NT)
                for b_ in bias:
                    z = z + b_
                p = jnp.exp2(z - lse_p[...].reshape(nblk, BAND, LANES)[:, :, lane0:lane0 + 1])
                dp = _dot(do3, keys_v, BATCH_NT)
                ds = (p * (dp - delta_p[...].reshape(nblk, BAND, LANES)[:, :, lane0:lane0 + 1])).astype(BF16)
                dq_b = dq_b + _dot(ds, keys_k, BATCH_NN) * masks[hd]
                dk_b = dk_b + _dot(ds, q3, BATCH_TN)
                dv_b = dv_b + _dot(p.astype(BF16), do3, BATCH_TN)
            dq_p[...] = dq_b.reshape(seq, LANES)
            for acc, out in ((dk_b, dk_p), (dv_b, dv_p)):
                out[...] = acc[:, nk - BAND:, :].reshape(seq, LANES)
                if nk > BAND:
                    out[0:seq - BAND, :] += acc[1:, :BAND, :].reshape(seq - BAND, LANES)
            for tok, res in _residues(d, seq):
                if d == DILATIONS[0]:
                    dq_s[tok, :] = dq_p[res, :]
                    dk_s[tok, :] = dk_p[res, :]
                    dv_s[tok, :] = dv_p[res, :]
                else:
                    dq_s[tok, :] += dq_p[res, :]
                    dk_s[tok, :] += dk_p[res, :]
                    dv_s[tok, :] += dv_p[res, :]

        cos, up, dn = cos_ref[...], up_ref[...], dn_ref[...]
        dq_raw, dgq = _head_norm_bwd(_rope_bwd(dq_s[...] * SCALE, cos, up, dn), qhat, rq, gq_ref[...], ones)
        dk_raw, dgk = _head_norm_bwd(_rope_bwd(dk_s[...] * LN2, cos, up, dn), khat, rk, gk_ref[...], ones)
        dq_ref[...] = dq_raw.astype(BF16)
        dk_ref[...] = dk_raw.astype(BF16)
        dv_ref[...] = dv_s[...].astype(BF16)
        dg_ref[0:1, :] += dgq
        dg_ref[1:2, :] += dgk

    blk = lambda o_: pl.BlockSpec((seq, LANES), lambda b, p: (b, o_ + p))
    vec = pl.BlockSpec((1, LANES), lambda b, p: (0, 0))
    tab = pl.BlockSpec((seq, LANES), lambda b, p: (0, 0))
    f32_buf = pltpu.VMEM((seq, LANES), F32)
    bf16_buf = pltpu.VMEM((seq, LANES), BF16)
    window_buf = pltpu.VMEM((seq // BAND, 2 * BAND, LANES), BF16)
    bf16_pair = pltpu.VMEM((2, seq, LANES), BF16)
    return _host_call(
        body, rider, name="dil_bwd", grid=(nb, npair),
        in_specs=[blk(off), blk(off + npair), blk(off + 2 * npair), vec, vec, tab, tab, tab,
                  blk(0), blk(0), blk(0)],
        out_specs=[blk(0), blk(0), blk(0), pl.BlockSpec((8, LANES), lambda b, p: (0, 0))],
        out_shape=[jax.ShapeDtypeStruct((t, W_GROUP), BF16), jax.ShapeDtypeStruct((t, W_GROUP), BF16),
                   jax.ShapeDtypeStruct((t, W_GROUP), BF16), jax.ShapeDtypeStruct((8, LANES), F32)],
        scratch_shapes=[f32_buf] * 7 + [bf16_pair, bf16_buf, bf16_buf, bf16_pair, window_buf, window_buf]
        + [f32_buf] * 5,
        inputs=(proj, proj, proj, gq, gk, cos, up, dn, do, o, lse), semantics=("arbitrary", "arbitrary"))


def _adamw(w, g, m, v, name, rider=None):
    rows, cols = w.shape[-2:]
    tr = _row_tile(rows) if rows >= 8 else rows
    c1 = 1.0 - ADAM_B1 ** ADAM_STEP
    c2 = 1.0 - ADAM_B2 ** ADAM_STEP

    def body(w_ref, g_ref, m_ref, v_ref, d_ref, nm_ref, nv_ref):
        g_ = g_ref[...]
        nm = ADAM_B1 * m_ref[...] + (1.0 - ADAM_B1) * g_
        nv = ADAM_B2 * v_ref[...] + (1.0 - ADAM_B2) * (g_ * g_)
        nm_ref[...] = nm
        nv_ref[...] = nv
        d_ref[...] = -ADAM_LR * ((nm / c1) / (jnp.sqrt(nv / c2) + ADAM_EPS) + ADAM_WD * w_ref[...])

    if w.ndim == 3:
        spec = pl.BlockSpec((1, tr, cols), lambda i: (0, i, 0))
    else:
        spec = pl.BlockSpec((tr, cols), lambda i: (i, 0))
    shape = jax.ShapeDtypeStruct(w.shape, F32)
    return _host_call(
        body, rider, name=name, grid=(rows // tr,), in_specs=[spec] * 4, out_specs=[spec] * 3,
        out_shape=[shape] * 3, scratch_shapes=[], inputs=(w, g, m, v), semantics=("arbitrary",))


def _place():
    x, y, c = lax.axis_index("x"), lax.axis_index("y"), lax.axis_index("c")
    chips = [(1 - x, y), (x, 1 - y), (1 - x, 1 - y)]
    return x, y, c, chips


def _gather_weight(w, name):
    _, rows, cols = w.shape
    half_rows = rows // 2

    def body(w_ref, out_ref, send_sems, recv_sems):
        x, y, c, chips = _place()
        sibling = (x, y, 1 - c)
        mine = 2 * x + y
        lo = pl.multiple_of(c * half_rows, 16)
        lo_sib = pl.multiple_of((1 - c) * half_rows, 16)
        out_ref[mine] = w_ref[0].astype(BF16)

        def copy(k, shard, first_row, to):
            ref = out_ref.at[shard, pl.ds(first_row, half_rows), :]
            return pltpu.make_async_remote_copy(src_ref=ref, dst_ref=ref, send_sem=send_sems.at[k],
                                                recv_sem=recv_sems.at[k], device_id=to, device_id_type=MESH)

        sends = [copy(k, mine, lo, (cx, cy, c)) for k, (cx, cy) in enumerate(chips)]
        for cp in sends:
            cp.start()
        passed = []
        for k, (cx, cy) in enumerate(chips):
            theirs = 2 * cx + cy
            copy(k, theirs, lo, (cx, cy, c)).wait_recv()
            fw = copy(3 + k, theirs, lo, sibling)
            fw.start()
            passed.append(fw)
        for k, (cx, cy) in enumerate(chips):
            copy(3 + k, 2 * cx + cy, lo_sib, sibling).wait_recv()
        for cp in sends + passed:
            cp.wait_send()

    return pl.pallas_call(
        body, name=name,
        in_specs=[pl.BlockSpec(memory_space=pltpu.VMEM)],
        out_specs=pl.BlockSpec(memory_space=pltpu.VMEM),
        out_shape=jax.ShapeDtypeStruct((4, rows, cols), BF16),
        scratch_shapes=[pltpu.SemaphoreType.DMA((6,)), pltpu.SemaphoreType.DMA((6,))],
        compiler_params=pltpu.CompilerParams(vmem_limit_bytes=VMEM_LIMIT),
    )(w)


def _remote(src, dst, sems, k, to):
    send_sems, recv_sems = sems
    return pltpu.make_async_remote_copy(src_ref=src, dst_ref=dst, send_sem=send_sems.at[k], recv_sem=recv_sems.at[k],
                                        device_id=to, device_id_type=MESH)


def _pack_bf16(parts, name):
    rows = [p.shape[1] for p in parts]
    cols = parts[0].shape[2]

    def body(*refs):
        out_ref, first = refs[-1], 0
        for ref, r in zip(refs[:-1], rows):
            out_ref[first:first + r, :] = ref[0].astype(BF16)
            first += r

    return pl.pallas_call(
        body, name=name, in_specs=[pl.BlockSpec(memory_space=pltpu.VMEM)] * len(parts),
        out_specs=pl.BlockSpec(memory_space=pltpu.VMEM),
        out_shape=jax.ShapeDtypeStruct((sum(rows), cols), BF16),
        compiler_params=pltpu.CompilerParams(vmem_limit_bytes=VMEM_LIMIT),
    )(*parts)


def _gather_rider(packed):
    rows, cols = packed.shape
    half = rows // 2

    def copies(ins, outs, sems, which):
        p_ref, g_ref = ins[0], outs[0]
        x, y, c, chips = _place()
        sibling = (x, y, 1 - c)
        mine = 2 * x + y
        lo = pl.multiple_of(c * half, 16)
        lo_sib = pl.multiple_of((1 - c) * half, 16)
        spot = lambda shard, first: g_ref.at[shard, pl.ds(first, half), :]
        groups = {
            "own": lambda: [pltpu.make_async_copy(p_ref, g_ref.at[mine], sems[0].at[6])],
            "sends": lambda: [_remote(p_ref.at[pl.ds(lo, half), :], spot(mine, lo), sems, k, (cx, cy, c))
                              for k, (cx, cy) in enumerate(chips)],
            "arrivals": lambda: [_remote(spot(2 * cx + cy, lo), spot(2 * cx + cy, lo), sems, k, (cx, cy, c))
                                 for k, (cx, cy) in enumerate(chips)],
            "passes": lambda: [_remote(spot(2 * cx + cy, lo), spot(2 * cx + cy, lo), sems, 3 + k, sibling)
                               for k, (cx, cy) in enumerate(chips)],
            "from_sibling": lambda: [_remote(spot(2 * cx + cy, lo_sib), spot(2 * cx + cy, lo_sib), sems, 3 + k, sibling)
                                     for k, (cx, cy) in enumerate(chips)],
        }
        return [groups[name]() for name in which]

    def start(ins, outs, send_sems, recv_sems):
        own, sends = copies(ins, outs, (send_sems, recv_sems), ("own", "sends"))
        for cp in own + sends:
            cp.start()

    def middle(ins, outs, send_sems, recv_sems):
        arrivals, passes = copies(ins, outs, (send_sems, recv_sems), ("arrivals", "passes"))
        for landed, onward in zip(arrivals, passes):
            landed.wait_recv()
            onward.start()

    def finish(ins, outs, send_sems, recv_sems):
        own, sends, passes, from_sibling = copies(ins, outs, (send_sems, recv_sems),
                                                  ("own", "sends", "passes", "from_sibling"))
        for cp in from_sibling:
            cp.wait_recv()
        for cp in sends + passes:
            cp.wait_send()
        own[0].wait()

    return _Rider([packed], [jax.ShapeDtypeStruct((4, rows, cols), BF16)], 7, start, finish, middle=middle)


def _exchange_rider(inputs, out_shapes, n_sems, copies, aliases=None):
    def start(ins, outs, send_sems, recv_sems):
        for cp in copies(ins, outs, (send_sems, recv_sems)):
            cp.start()

    def finish(ins, outs, send_sems, recv_sems):
        for cp in copies(ins, outs, (send_sems, recv_sems)):
            cp.wait()

    return _Rider(inputs, out_shapes, n_sems, start, finish, aliases)


def _swap_rider(grads4):
    halves = [g.shape[1] // 2 for g in grads4]

    def copies(ins, outs, sems):
        x, y, c, _ = _place()
        return [_remote(g.at[:, pl.ds(pl.multiple_of((1 - c) * h, 8), h), :], a, sems, i, (x, y, 1 - c))
                for i, (g, a, h) in enumerate(zip(ins, outs, halves))]

    shapes = [jax.ShapeDtypeStruct((4, h, g.shape[2]), F32) for g, h in zip(grads4, halves)]
    return _exchange_rider(grads4, shapes, len(grads4), copies)


def _chip_sum(g4, from_sibling, name):
    _, rows, cols = g4.shape
    half = rows // 2

    def body(g_ref, s_ref, stage_ref, own_ref):
        x, y, c, chips = _place()
        lo = pl.multiple_of(c * half, 8)
        for k, (cx, cy) in enumerate(chips):
            theirs = 2 * cx + cy
            stage_ref[k] = (g_ref[theirs, pl.ds(lo, half), :] + s_ref[theirs]).astype(BF16)
        mine = 2 * x + y
        own_ref[...] = g_ref[mine, pl.ds(lo, half), :] + s_ref[mine]

    return pl.pallas_call(
        body, name=name, in_specs=[pl.BlockSpec(memory_space=pltpu.VMEM)] * 2,
        out_specs=[pl.BlockSpec(memory_space=pltpu.VMEM)] * 2,
        out_shape=[jax.ShapeDtypeStruct((3, half, cols), BF16), jax.ShapeDtypeStruct((half, cols), F32)],
        compiler_params=pltpu.CompilerParams(vmem_limit_bytes=VMEM_LIMIT),
    )(g4, from_sibling)


def _spread_rider(stages):
    def copies(ins, outs, sems):
        _, _, c, chips = _place()
        return [_remote(st.at[k], ld.at[k], sems, 3 * i + k, (cx, cy, c))
                for i, (st, ld) in enumerate(zip(ins, outs)) for k, (cx, cy) in enumerate(chips)]

    shapes = [jax.ShapeDtypeStruct(s.shape, s.dtype) for s in stages]
    return _exchange_rider(stages, shapes, 3 * len(stages), copies)


def _finish_half(own, landed, name):
    half, cols = own.shape

    def body(own_ref, landed_ref, out_ref):
        c = lax.axis_index("c")
        acc = own_ref[...]
        for k in range(3):
            acc = acc + landed_ref[k].astype(F32)
        out_ref[pl.ds(pl.multiple_of(c * half, 8), half), :] = acc

    return pl.pallas_call(
        body, name=name, in_specs=[pl.BlockSpec(memory_space=pltpu.VMEM)] * 2,
        out_specs=pl.BlockSpec(memory_space=pltpu.VMEM),
        out_shape=jax.ShapeDtypeStruct((2 * half, cols), F32),
        compiler_params=pltpu.CompilerParams(vmem_limit_bytes=VMEM_LIMIT),
    )(own, landed)


def _share_rider(fulls):
    def copies(ins, outs, sems):
        x, y, c, _ = _place()
        out = []
        for i, full in enumerate(outs):
            half = full.shape[0] // 2
            rows = full.at[pl.ds(pl.multiple_of(c * half, 8), half), :]
            out.append(_remote(rows, rows, sems, i, (x, y, 1 - c)))
        return out

    def finish_copies(ins, outs, sems):
        x, y, c, _ = _place()
        out = []
        for i, full in enumerate(outs):
            half = full.shape[0] // 2
            mine = full.at[pl.ds(pl.multiple_of(c * half, 8), half), :]
            theirs = full.at[pl.ds(pl.multiple_of((1 - c) * half, 8), half), :]
            out.append((_remote(mine, mine, sems, i, (x, y, 1 - c)), _remote(theirs, theirs, sems, i, (x, y, 1 - c))))
        return out

    def start(ins, outs, send_sems, recv_sems):
        for cp in copies(ins, outs, (send_sems, recv_sems)):
            cp.start()

    def finish(ins, outs, send_sems, recv_sems):
        for sent, landed in finish_copies(ins, outs, (send_sems, recv_sems)):
            sent.wait_send()
            landed.wait_recv()

    shapes = [jax.ShapeDtypeStruct(f.shape, f.dtype) for f in fulls]
    return _Rider(fulls, shapes, len(fulls), start, finish, aliases={i: i for i in range(len(fulls))})


def _all_sum_small(v):
    shape = v.shape

    def body(v_ref, out_ref, buf, send_sems, recv_sems):
        x, y, c, _ = _place()
        me = 4 * x + 2 * y + c
        buf[me] = v_ref[...]
        flips = [(dx, dy, dc) for dx in (0, 1) for dy in (0, 1) for dc in (0, 1)][1:]

        def copy(k, slot, flip):
            dx, dy, dc = flip
            to = (1 - x if dx else x, 1 - y if dy else y, 1 - c if dc else c)
            return pltpu.make_async_remote_copy(src_ref=buf.at[slot], dst_ref=buf.at[slot], send_sem=send_sems.at[k],
                                                recv_sem=recv_sems.at[k], device_id=to, device_id_type=MESH)

        sends = [copy(k, me, flip) for k, flip in enumerate(flips)]
        for cp in sends:
            cp.start()
        for k, (dx, dy, dc) in enumerate(flips):
            sender = 4 * (1 - x if dx else x) + 2 * (1 - y if dy else y) + (1 - c if dc else c)
            copy(k, sender, (dx, dy, dc)).wait_recv()
        for cp in sends:
            cp.wait_send()
        total = buf[0]
        for i in range(1, 8):
            total = total + buf[i]
        out_ref[...] = total

    return pl.pallas_call(
        body, name="all_sum_small",
        in_specs=[pl.BlockSpec(memory_space=pltpu.VMEM)],
        out_specs=pl.BlockSpec(memory_space=pltpu.VMEM),
        out_shape=jax.ShapeDtypeStruct(shape, F32),
        scratch_shapes=[pltpu.VMEM((8,) + shape, F32), pltpu.SemaphoreType.DMA((7,)), pltpu.SemaphoreType.DMA((7,))],
    )(v)


SMALL = (("g_mix", 1024), ("g_ffn", 1024), ("g_out_fox", 512), ("g_out_dil", 512), ("g_q_fox", 64),
         ("g_k_fox", 64), ("g_q_dil", 64), ("g_k_dil", 64), ("b_forget", 8))
SMALL_PACKED = (32, LANES)


PACKED_ROWS = (256, 704, 704, 704)


def _local_grads(x, target, gains, w1, wft, dense, packed, nb, seq):
    tile2 = lambda g: jnp.tile(g, (1, 2))
    gq_f, gk_f, gq_d, gk_d = (tile2(gains[n]) for n in ("g_q_fox", "g_k_fox", "g_q_dil", "g_k_dil"))
    b_col = gains["b_forget"].reshape(N_FOX_HEADS, 1)
    cos, up, dn = _rope_tables(seq)
    npair = N_FOX_HEADS // 2

    proj, fa_row, h1, h1_t = _in_proj(x, gains["g_mix"], w1, wft)
    c_row = _gate_fwd(fa_row, b_col, seq)
    c3 = c_row.reshape(npair, 2, nb * seq)
    (o_fox, lse_fox), gathered = _fox_fwd(proj, c3, gq_f, gk_f, nb, seq,
                                          rider=None if packed is None else _gather_rider(packed))
    if packed is not None:
        first, dense = 0, []
        for r in PACKED_ROWS:
            dense.append(gathered[0][:, first:first + r, :].reshape(4 * r, -1))
            first += r
    w_out, w_gate, w_up, w_down = dense
    o_dil, lse_dil = _dil_fwd(proj, gq_d, gk_d, cos, up, dn, nb, seq)
    x1, o_n_t = _attn_out(o_fox, o_dil, x, gains["g_out_fox"], gains["g_out_dil"], w_out)
    a, u, dy, loss_parts = _ffn_fwd(x1, target, gains["g_ffn"], w_gate, w_up, w_down)
    loss = jnp.sum(loss_parts[:, 0, 0])

    dx1, s, da, du, h2, dg_ffn = _ffn_bwd(dy, a, u, x1, gains["g_ffn"], w_gate, w_up, w_down)
    d_w_down = _token_matmul(s, dy, "dw_down", 512, False)
    d_w_gate = _token_matmul(da, h2, "dw_gate", 512, False)
    d_w_up = _token_matmul(du, h2, "dw_up", 512, False)
    d_w_out = _token_matmul(o_n_t, dx1, "dw_out", 1024)
    names = ("w_out", "w_gate", "w_up", "w_down")
    grads4 = [g.reshape(4, -1, g.shape[1]) for g in (d_w_out, d_w_gate, d_w_up, d_w_down)]
    exchange = packed is not None
    (do_fox, do_dil, dg_of, dg_od), from_sibling = _attn_out_bwd(
        dx1, o_fox, o_dil, gains["g_out_fox"], gains["g_out_dil"], w_out,
        rider=_swap_rider(grads4) if exchange else None)
    if exchange:
        sums = [_chip_sum(g, s, "chip_sum_" + n) for g, s, n in zip(grads4, from_sibling, names)]
    (dq_f, dk_f, dv_f, dc3, dg_fox), landed = _fox_bwd(
        proj, c3, gq_f, gk_f, do_fox, o_fox, lse_fox, nb, seq,
        rider=_spread_rider([st for st, _ in sums]) if exchange else None)
    if exchange:
        halves = [_finish_half(own, ld, "finish_half_" + n) for (_, own), ld, n in zip(sums, landed, names)]
    (dq_d, dk_d, dv_d, dg_dil), reduced = _dil_bwd(
        proj, gq_d, gk_d, cos, up, dn, do_dil, o_dil, lse_dil, nb, seq,
        rider=_share_rider(halves) if exchange else None)
    if exchange:
        d_w_out, d_w_gate, d_w_up, d_w_down = reduced
    dfa_row, db = _gate_bwd(dc3.reshape(N_FOX_HEADS, nb * seq), fa_row, b_col, seq)
    dparts = [dq_f, dk_f, dv_f, dq_d, dk_d, dv_d]
    d_w1 = _token_matmul_parts(h1_t, dparts, "dw_in")
    d_wf = _row_matmul(dfa_row, h1, "dw_forget")
    fox_w = 3 * W_GROUP
    d_w_in = jnp.concatenate([d_w1[:, :fox_w], d_wf.T, d_w1[:, fox_w:]], axis=1)
    if exchange:
        shards = [_shards_of_columns(d_w_in)]
        _, from_sibling = _idle_host(_swap_rider(shards), "swap_w_in")
        stage, own = _chip_sum(shards[0], from_sibling[0], "chip_sum_w_in")
    (grad_x, dg_mix), landed = _in_proj_bwd(dparts, dfa_row, w1, wft, x, gains["g_mix"], dx1,
                                            rider=_spread_rider([stage]) if exchange else None)
    if exchange:
        d_w_in = _finish_half(own, landed[0], "finish_half_w_in")

    fold = lambda g2: (g2[:, :HEAD_DIM] + g2[:, HEAD_DIM:])
    small = {
        "g_mix": dg_mix[0:1], "g_ffn": dg_ffn[0:1], "g_out_fox": dg_of[0:1], "g_out_dil": dg_od[0:1],
        "g_q_fox": fold(dg_fox[0:1]), "g_k_fox": fold(dg_fox[1:2]),
        "g_q_dil": fold(dg_dil[0:1]), "g_k_dil": fold(dg_dil[1:2]),
        "b_forget": db[:, 0].reshape(1, N_FOX_HEADS),
    }
    big = {"w_in": d_w_in, "w_out": d_w_out, "w_gate": d_w_gate, "w_up": d_w_up, "w_down": d_w_down}
    return loss, grad_x, big, small


def _shards_of_columns(full, n=4):
    r, nc = full.shape
    return full.reshape(r, n, nc // n).transpose(1, 0, 2)


def _columns_of_shards(slabs):
    n, r, c = slabs.shape
    return slabs.transpose(1, 0, 2).reshape(r, n * c)


def kernel(x, g_mix, w_in, b_forget, g_q_fox, g_k_fox, g_q_dil, g_k_dil, g_out_fox, g_out_dil, w_out, g_ffn, w_gate, w_up, w_down, loss_target, m_g_mix, m_w_in, m_b_forget, m_g_q_fox, m_g_k_fox, m_g_q_dil, m_g_k_dil, m_g_out_fox, m_g_out_dil, m_w_out, m_g_ffn, m_w_gate, m_w_up, m_w_down, v_g_mix, v_w_in, v_b_forget, v_g_q_fox, v_g_k_fox, v_g_q_dil, v_g_k_dil, v_g_out_fox, v_g_out_dil, v_w_out, v_g_ffn, v_w_gate, v_w_up, v_w_down):
    nb, seq, d = x.shape
    weights = dict(g_mix=g_mix, w_in=w_in, b_forget=b_forget, g_q_fox=g_q_fox, g_k_fox=g_k_fox, g_q_dil=g_q_dil,
                   g_k_dil=g_k_dil, g_out_fox=g_out_fox, g_out_dil=g_out_dil, w_out=w_out, g_ffn=g_ffn,
                   w_gate=w_gate, w_up=w_up, w_down=w_down)
    m_in = dict(g_mix=m_g_mix, w_in=m_w_in, b_forget=m_b_forget, g_q_fox=m_g_q_fox, g_k_fox=m_g_k_fox,
                g_q_dil=m_g_q_dil, g_k_dil=m_g_k_dil, g_out_fox=m_g_out_fox, g_out_dil=m_g_out_dil, w_out=m_w_out,
                g_ffn=m_g_ffn, w_gate=m_w_gate, w_up=m_w_up, w_down=m_w_down)
    v_in = dict(g_mix=v_g_mix, w_in=v_w_in, b_forget=v_b_forget, g_q_fox=v_g_q_fox, g_k_fox=v_g_k_fox,
                g_q_dil=v_g_q_dil, g_k_dil=v_g_k_dil, g_out_fox=v_g_out_fox, g_out_dil=v_g_out_dil, w_out=v_w_out,
                g_ffn=v_g_ffn, w_gate=v_w_gate, w_up=v_w_up, w_down=v_w_down)
    order = ["g_mix", "w_in", "b_forget", "g_q_fox", "g_k_fox", "g_q_dil", "g_k_dil", "g_out_fox", "g_out_dil",
             "w_out", "g_ffn", "w_gate", "w_up", "w_down"]

    w_in_full = _columns_of_shards(_gather_weight(w_in, "gather_w_in"))
    fox_w = 3 * W_GROUP
    w1 = jnp.concatenate([w_in_full[:, :fox_w], w_in_full[:, fox_w + N_FOX_HEADS:]], axis=1)
    wft = w_in_full[:, fox_w:fox_w + N_FOX_HEADS].T
    swap = lambda a: jnp.transpose(a, (0, 2, 1))
    for n in ("w_gate", "w_up"):
        weights[n], m_in[n], v_in[n] = swap(weights[n]), swap(m_in[n]), swap(v_in[n])
    shards = _pack_bf16([weights[n] for n in ("w_out", "w_gate", "w_up", "w_down")], "pack_shards")

    gains = {n: weights[n] for n, _ in SMALL}
    loss, grad_x, big, small = _local_grads(
        x.reshape(nb * seq, d), loss_target.reshape(nb * seq, d), gains, w1, wft, None, shards, nb, seq)

    grads = {n: big[n][None] for n in ("w_out", "w_gate", "w_up", "w_down")}
    packed = jnp.concatenate([small[n].reshape(-1) for n, _ in SMALL] + [loss.reshape(1)])
    packed = jnp.pad(packed, (0, SMALL_PACKED[0] * SMALL_PACKED[1] - packed.shape[0])).reshape(SMALL_PACKED)
    summed = _all_sum_small(packed).reshape(-1)
    pos = 0
    for n, size in SMALL:
        grads[n] = summed[pos:pos + size].reshape(1, size)
        pos += size
    loss = summed[pos]

    deltas, new_m, new_v, grad_out = {}, {}, {}, {}
    for n in ["w_down"] + [n for n in order if n != "w_down"]:
        rider = _share_rider([big["w_in"]]) if n == "w_down" else None
        (deltas[n], new_m[n], new_v[n]), shared = _adamw(weights[n], grads[n], m_in[n], v_in[n], "adamw_" + n, rider)
        if rider is not None:
            grads["w_in"] = shared[0][None]
        grad_out[n] = grads[n]
    for n in ("w_gate", "w_up"):
        grad_out[n], deltas[n], new_m[n], new_v[n] = (swap(a) for a in (grad_out[n], deltas[n], new_m[n], new_v[n]))

    return (loss, grad_x.reshape(nb, seq, d), *[grad_out[n] for n in order], *[deltas[n] for n in order],
            *[new_m[n] for n in order], *[new_v[n] for n in order])
```

```python
import functools
import math

import numpy as np
import jax
import jax.numpy as jnp
from jax import lax
from jax.experimental import pallas as pl
from jax.experimental.pallas import tpu as pltpu

F32, BF16 = jnp.float32, jnp.bfloat16
MESH = pl.DeviceIdType.MESH

EPS = 1e-6
NEG = -1e30
HEAD_DIM = 64
SCALE = HEAD_DIM ** -0.5
LOG2E = math.log2(math.e)
LN2 = math.log(2.0)
ROPE_THETA = 500000.0
ROPE_DIM = HEAD_DIM // 4
LANES = 128
W_GROUP = 512
N_FOX_HEADS = 8
VMEM_LIMIT = 56 * 1024 * 1024
DILATIONS = (1, 4, 16)
BAND = 128

ADAM_LR, ADAM_B1, ADAM_B2, ADAM_EPS, ADAM_WD, ADAM_STEP = 0.001, 0.9, 0.999, 1e-08, 0.01, 10

NT = (((1,), (1,)), ((), ()))
TN = (((0,), (0,)), ((), ()))
BATCH_NT = (((2,), (2,)), ((0,), (0,)))
BATCH_NN = (((2,), (1,)), ((0,), (0,)))
BATCH_TN = (((1,), (1,)), ((0,), (0,)))


def _params(sem=None):
    return pltpu.CompilerParams(dimension_semantics=sem, vmem_limit_bytes=VMEM_LIMIT)


def _dot(a, b, dims=None):
    if dims is None:
        return jnp.dot(a, b, preferred_element_type=F32)
    return lax.dot_general(a, b, dims, preferred_element_type=F32)


def _group_ones():
    i = lax.broadcasted_iota(jnp.int32, (LANES, LANES), 0) >> 6
    j = lax.broadcasted_iota(jnp.int32, (LANES, LANES), 1) >> 6
    return (i == j).astype(BF16)


def _split3(x):
    a = x.astype(BF16)
    r = x - a.astype(F32)
    b = r.astype(BF16)
    c = (r - b.astype(F32)).astype(BF16)
    return a, b, c


def _groupsum(x, ones):
    a, b, c = _split3(x)
    return _dot(a, ones) + _dot(b, ones) + _dot(c, ones)


def _head_masks():
    lane = lax.broadcasted_iota(jnp.int32, (1, LANES), 1)
    return [(lane < HEAD_DIM).astype(F32), (lane >= HEAD_DIM).astype(F32)]


def _head_norm(raw, gain, ones):
    r = lax.rsqrt(_groupsum(raw * raw, ones) * (1.0 / HEAD_DIM) + EPS)
    return raw * r, r


def _head_norm_bwd(dy, xhat, r, gain, ones):
    u = dy * gain
    dgain = jnp.sum(dy * xhat, axis=0, keepdims=True)
    draw = r * (u - xhat * (_groupsum(u * xhat, ones) * (1.0 / HEAD_DIM)))
    return draw, dgain


def _rope(x, cos, s_up, s_dn):
    return x * cos + pltpu.roll(x, LANES - 8, 1) * s_up + pltpu.roll(x, 8, 1) * s_dn


def _rope_bwd(dy, cos, s_up, s_dn):
    return dy * cos + pltpu.roll(dy * s_up, 8, 1) + pltpu.roll(dy * s_dn, LANES - 8, 1)


def _rope_tables(seq):
    half = ROPE_DIM // 2
    inv_freq = jnp.power(jnp.float32(ROPE_THETA), -jnp.arange(half, dtype=F32) * 2.0 / ROPE_DIM)
    ang = jnp.arange(seq).astype(F32)[:, None] * inv_freq[None, :]
    cos, sin = jnp.cos(ang), jnp.sin(ang)
    one = jnp.ones((seq, HEAD_DIM - ROPE_DIM), F32)
    zero_h = jnp.zeros((seq, half), F32)
    zero_r = jnp.zeros((seq, HEAD_DIM - ROPE_DIM), F32)
    c = jnp.concatenate([cos, cos, one], axis=1)
    up = jnp.concatenate([-sin, zero_h, zero_r], axis=1)
    dn = jnp.concatenate([zero_h, sin, zero_r], axis=1)
    return jnp.tile(c, (1, 2)), jnp.tile(up, (1, 2)), jnp.tile(dn, (1, 2))


def _row_tile(rows, cap=256):
    best = rows
    for t in range(8, min(rows, cap) + 1, 8):
        if rows % t == 0:
            best = t
    return best


class _Rider:
    def __init__(self, inputs, out_shapes, n_sems, start, finish, aliases=None, middle=None):
        self.inputs, self.out_shapes, self.n_sems = list(inputs), list(out_shapes), n_sems
        self.start, self.finish, self.middle, self.aliases = start, finish, middle, dict(aliases or {})


def _host_call(body, rider, *, name, grid, in_specs, out_specs, out_shape, scratch_shapes, inputs, semantics):
    if rider is None:
        return pl.pallas_call(body, name=name, grid=grid, in_specs=in_specs, out_specs=out_specs,
                              out_shape=out_shape, scratch_shapes=scratch_shapes,
                              compiler_params=_params(semantics))(*inputs), []
    n_in, n_out, n_scr = len(in_specs), len(out_specs), len(scratch_shapes)
    r_in, r_out = len(rider.inputs), len(rider.out_shapes)

    def wrapped(*refs):
        ins, refs = refs[:n_in], refs[n_in:]
        r_ins, refs = refs[:r_in], refs[r_in:]
        outs, refs = refs[:n_out], refs[n_out:]
        r_outs, refs = refs[:r_out], refs[r_out:]
        scratch, (send_sems, recv_sems) = refs[:n_scr], refs[n_scr:]
        ids = [pl.program_id(a) for a in range(len(grid))]
        first = functools.reduce(lambda p, q: p & q, [i == 0 for i in ids])
        last = functools.reduce(lambda p, q: p & q, [i == g - 1 for i, g in zip(ids, grid)])

        @pl.when(first)
        def _():
            rider.start(r_ins, r_outs, send_sems, recv_sems)

        body(*ins, *outs, *scratch)

        if rider.middle is not None:
            step, steps = ids[0], grid[0]
            for i, g in zip(ids[1:], grid[1:]):
                step, steps = step * g + i, steps * g

            @pl.when(step == (3 * steps) // 4)
            def _():
                rider.middle(r_ins, r_outs, send_sems, recv_sems)

        @pl.when(last)
        def _():
            rider.finish(r_ins, r_outs, send_sems, recv_sems)

    hbm = pl.BlockSpec(memory_space=pl.ANY)
    res = pl.pallas_call(
        wrapped, name=name, grid=grid,
        in_specs=list(in_specs) + [hbm] * r_in, out_specs=list(out_specs) + [hbm] * r_out,
        out_shape=list(out_shape) + rider.out_shapes,
        scratch_shapes=list(scratch_shapes) + [pltpu.SemaphoreType.DMA((rider.n_sems,))] * 2,
        input_output_aliases={n_in + i: n_out + o for i, o in rider.aliases.items()},
        compiler_params=_params(semantics),
    )(*inputs, *rider.inputs)
    return res[:n_out], res[n_out:]


def _idle_host(rider, name):
    def body(o_ref):
        o_ref[...] = jnp.zeros_like(o_ref)

    return _host_call(body, rider, name=name, grid=(1,), in_specs=[],
                      out_specs=[pl.BlockSpec((8, LANES), lambda i: (0, 0))],
                      out_shape=[jax.ShapeDtypeStruct((8, LANES), F32)], scratch_shapes=[], inputs=(),
                      semantics=("arbitrary",))


def _in_proj(x, g_mix, w1, wft):
    t, d = x.shape
    n = w1.shape[1]
    tt = 512

    def body(x_ref, g_ref, w_ref, wf_ref, p_ref, fa_ref, h_ref, ht_ref):
        xx = x_ref[...]
        r = lax.rsqrt(jnp.mean(xx * xx, axis=-1, keepdims=True) + EPS)
        h = (xx * r * g_ref[...]).astype(BF16)
        h_ref[...] = h
        ht_ref[...] = h.T
        for j in range(n // W_GROUP):
            cols = slice(j * W_GROUP, (j + 1) * W_GROUP)
            p_ref[:, cols] = _dot(h, w_ref[:, cols]).astype(BF16)
        fa_ref[...] = _dot(wf_ref[...], h, NT)

    return pl.pallas_call(
        body, name="in_proj", grid=(t // tt,),
        in_specs=[pl.BlockSpec((tt, d), lambda i: (i, 0)), pl.BlockSpec((1, d), lambda i: (0, 0)),
                  pl.BlockSpec(memory_space=pltpu.VMEM), pl.BlockSpec(memory_space=pltpu.VMEM)],
        out_specs=[pl.BlockSpec((tt, n), lambda i: (i, 0)), pl.BlockSpec((8, tt), lambda i: (0, i)),
                   pl.BlockSpec((tt, d), lambda i: (i, 0)), pl.BlockSpec((d, tt), lambda i: (0, i))],
        out_shape=[jax.ShapeDtypeStruct((t, n), BF16), jax.ShapeDtypeStruct((8, t), F32),
                   jax.ShapeDtypeStruct((t, d), BF16), jax.ShapeDtypeStruct((d, t), BF16)],
        compiler_params=_params(("arbitrary",)),
    )(x, g_mix, w1, wft)


def _tri(n, upper):
    i = lax.broadcasted_iota(jnp.int32, (n, n), 0)
    j = lax.broadcasted_iota(jnp.int32, (n, n), 1)
    return ((i <= j) if upper else (i >= j)).astype(BF16)


def _gate_fwd(fa_row, b_col, seq):
    t = fa_row.shape[1]
    cb = 256

    def body(fa_ref, b_ref, c_ref):
        tri = _tri(cb, True)
        carry = jnp.zeros((8, 1), F32)
        for k in range(seq // cb):
            z = fa_ref[:, k * cb:(k + 1) * cb] + b_ref[...]
            lf = jnp.minimum(z, 0.0) - jnp.log(1.0 + jnp.exp(-jnp.abs(z)))
            a, b, c = _split3(lf)
            blk = _dot(a, tri) + _dot(b, tri) + _dot(c, tri) + carry
            c_ref[:, k * cb:(k + 1) * cb] = blk
            carry = blk[:, cb - 1:cb]

    return pl.pallas_call(
        body, name="gate_fwd", grid=(t // seq,),
        in_specs=[pl.BlockSpec((8, seq), lambda i: (0, i)), pl.BlockSpec((8, 1), lambda i: (0, 0))],
        out_specs=pl.BlockSpec((8, seq), lambda i: (0, i)),
        out_shape=jax.ShapeDtypeStruct((8, t), F32),
        compiler_params=_params(("arbitrary",)),
    )(fa_row, b_col)


def _gate_bwd(dc_row, fa_row, b_col, seq):
    t = fa_row.shape[1]
    cb = 256

    def body(dc_ref, fa_ref, b_ref, dfa_ref, db_ref):
        @pl.when(pl.program_id(0) == 0)
        def _():
            db_ref[...] = jnp.zeros_like(db_ref)

        tri = _tri(cb, False)
        carry = jnp.zeros((8, 1), F32)
        dbs = jnp.zeros((8, 1), F32)
        for k in reversed(range(seq // cb)):
            a, b, c = _split3(dc_ref[:, k * cb:(k + 1) * cb])
            dlf = _dot(a, tri) + _dot(b, tri) + _dot(c, tri) + carry
            carry = dlf[:, 0:1]
            z = fa_ref[:, k * cb:(k + 1) * cb] + b_ref[...]
            dfa = dlf / (1.0 + jnp.exp(z))
            dfa_ref[:, k * cb:(k + 1) * cb] = dfa
            dbs = dbs + jnp.sum(dfa, axis=1, keepdims=True)
        db_ref[...] += jnp.broadcast_to(dbs, (8, LANES))

    return pl.pallas_call(
        body, name="gate_bwd", grid=(t // seq,),
        in_specs=[pl.BlockSpec((8, seq), lambda i: (0, i)), pl.BlockSpec((8, seq), lambda i: (0, i)),
                  pl.BlockSpec((8, 1), lambda i: (0, 0))],
        out_specs=[pl.BlockSpec((8, seq), lambda i: (0, i)), pl.BlockSpec((8, LANES), lambda i: (0, 0))],
        out_shape=[jax.ShapeDtypeStruct((8, t), F32), jax.ShapeDtypeStruct((8, LANES), F32)],
        compiler_params=_params(("arbitrary",)),
    )(dc_row, fa_row, b_col)


def _attn_out(o_fox, o_dil, x, g_fox, g_dil, w_out):
    t, d = x.shape
    w = o_fox.shape[1]
    tt = 512

    def body(of_ref, od_ref, x_ref, gf_ref, gd_ref, w_ref, x1_ref, ont_ref):
        acc = x_ref[...]
        for k, (o_ref, g_ref) in enumerate(((of_ref, gf_ref), (od_ref, gd_ref))):
            o = o_ref[...]
            r = lax.rsqrt(jnp.mean(o * o, axis=-1, keepdims=True) + EPS)
            on = (o * r * g_ref[...]).astype(BF16)
            ont_ref[k * w:(k + 1) * w, :] = on.T
            acc = acc + _dot(on, w_ref[k * w:(k + 1) * w, :])
        x1_ref[...] = acc

    return pl.pallas_call(
        body, name="attn_out", grid=(t // tt,),
        in_specs=[pl.BlockSpec((tt, w), lambda i: (i, 0)), pl.BlockSpec((tt, w), lambda i: (i, 0)),
                  pl.BlockSpec((tt, d), lambda i: (i, 0)), pl.BlockSpec((1, w), lambda i: (0, 0)),
                  pl.BlockSpec((1, w), lambda i: (0, 0)), pl.BlockSpec(memory_space=pltpu.VMEM)],
        out_specs=[pl.BlockSpec((tt, d), lambda i: (i, 0)), pl.BlockSpec((2 * w, tt), lambda i: (0, i))],
        out_shape=[jax.ShapeDtypeStruct((t, d), F32), jax.ShapeDtypeStruct((2 * w, t), BF16)],
        compiler_params=_params(("arbitrary",)),
    )(o_fox, o_dil, x, g_fox, g_dil, w_out)


def _attn_out_bwd(dx1, o_fox, o_dil, g_fox, g_dil, w_out, rider=None):
    t, d = dx1.shape
    w = o_fox.shape[1]
    tt = 512

    def body(dx_ref, of_ref, od_ref, gf_ref, gd_ref, w_ref, dof_ref, dod_ref, dgf_ref, dgd_ref):
        @pl.when(pl.program_id(0) == 0)
        def _():
            dgf_ref[...] = jnp.zeros_like(dgf_ref)
            dgd_ref[...] = jnp.zeros_like(dgd_ref)

        dxb = dx_ref[...].astype(BF16)
        for k, (o_ref, g_ref, do_ref, dg_ref) in enumerate(
                ((of_ref, gf_ref, dof_ref, dgf_ref), (od_ref, gd_ref, dod_ref, dgd_ref))):
            don = _dot(dxb, w_ref[k * w:(k + 1) * w, :], NT)
            o = o_ref[...]
            r = lax.rsqrt(jnp.mean(o * o, axis=-1, keepdims=True) + EPS)
            xhat = o * r
            u = don * g_ref[...]
            do_ref[...] = r * (u - xhat * jnp.mean(u * xhat, axis=-1, keepdims=True))
            dg_ref[0:1, :] += jnp.sum(don * xhat, axis=0, keepdims=True)

    return _host_call(
        body, rider, name="attn_out_bwd", grid=(t // tt,),
        in_specs=[pl.BlockSpec((tt, d), lambda i: (i, 0)), pl.BlockSpec((tt, w), lambda i: (i, 0)),
                  pl.BlockSpec((tt, w), lambda i: (i, 0)), pl.BlockSpec((1, w), lambda i: (0, 0)),
                  pl.BlockSpec((1, w), lambda i: (0, 0)), pl.BlockSpec(memory_space=pltpu.VMEM)],
        out_specs=[pl.BlockSpec((tt, w), lambda i: (i, 0)), pl.BlockSpec((tt, w), lambda i: (i, 0)),
                   pl.BlockSpec((8, w), lambda i: (0, 0)), pl.BlockSpec((8, w), lambda i: (0, 0))],
        out_shape=[jax.ShapeDtypeStruct((t, w), F32), jax.ShapeDtypeStruct((t, w), F32),
                   jax.ShapeDtypeStruct((8, w), F32), jax.ShapeDtypeStruct((8, w), F32)],
        scratch_shapes=[], inputs=(dx1, o_fox, o_dil, g_fox, g_dil, w_out), semantics=("arbitrary",))


def _ffn_fwd(x1, target, g_ffn, w_gate, w_up, w_down):
    t, d = x1.shape
    f = w_gate.shape[0]
    tt = 256

    def body(x_ref, t_ref, g_ref, wg_ref, wu_ref, wd_ref, a_ref, u_ref, dy_ref, loss_ref):
        xx = x_ref[...]
        r = lax.rsqrt(jnp.mean(xx * xx, axis=-1, keepdims=True) + EPS)
        h = (xx * r * g_ref[...]).astype(BF16)
        a = _dot(h, wg_ref[...], NT)
        u = _dot(h, wu_ref[...], NT)
        a_ref[...] = a.astype(BF16)
        u_ref[...] = u.astype(BF16)
        s = (a / (1.0 + jnp.exp(-a)) * u).astype(BF16)
        y = xx + _dot(s, wd_ref[...])
        e = y - t_ref[...]
        dy_ref[...] = e * (1.0 / d)
        loss_ref[...] = jnp.broadcast_to(0.5 * jnp.sum(e * e) * (1.0 / d), (1, 8, LANES))

    return pl.pallas_call(
        body, name="ffn_fwd", grid=(t // tt,),
        in_specs=[pl.BlockSpec((tt, d), lambda i: (i, 0)), pl.BlockSpec((tt, d), lambda i: (i, 0)),
                  pl.BlockSpec((1, d), lambda i: (0, 0)), pl.BlockSpec(memory_space=pltpu.VMEM),
                  pl.BlockSpec(memory_space=pltpu.VMEM), pl.BlockSpec(memory_space=pltpu.VMEM)],
        out_specs=[pl.BlockSpec((tt, f), lambda i: (i, 0)), pl.BlockSpec((tt, f), lambda i: (i, 0)),
                   pl.BlockSpec((tt, d), lambda i: (i, 0)), pl.BlockSpec((1, 8, LANES), lambda i: (i, 0, 0))],
        out_shape=[jax.ShapeDtypeStruct((t, f), BF16), jax.ShapeDtypeStruct((t, f), BF16),
                   jax.ShapeDtypeStruct((t, d), F32), jax.ShapeDtypeStruct((t // tt, 8, LANES), F32)],
        compiler_params=_params(("arbitrary",)),
    )(x1, target, g_ffn, w_gate, w_up, w_down)


def _ffn_bwd(dy, a, u, x1, g_ffn, w_gate, w_up, w_down):
    t, d = x1.shape
    f = w_gate.shape[0]
    tt = 256

    def body(dy_ref, a_ref, u_ref, x_ref, g_ref, wg_ref, wu_ref, wd_ref,
             dx_ref, s_ref, da_ref, du_ref, h_ref, dg_ref):
        @pl.when(pl.program_id(0) == 0)
        def _():
            dg_ref[...] = jnp.zeros_like(dg_ref)

        dy_ = dy_ref[...]
        ds = _dot(dy_.astype(BF16), wd_ref[...], NT)
        a_ = a_ref[...].astype(F32)
        u_ = u_ref[...].astype(F32)
        sig = 1.0 / (1.0 + jnp.exp(-a_))
        silu = a_ * sig
        s_ref[...] = (silu * u_).astype(BF16)
        da = (ds * u_ * (sig * (1.0 + a_ * (1.0 - sig)))).astype(BF16)
        du = (ds * silu).astype(BF16)
        da_ref[...] = da
        du_ref[...] = du
        dh = _dot(da, wg_ref[...]) + _dot(du, wu_ref[...])
        xx = x_ref[...]
        r = lax.rsqrt(jnp.mean(xx * xx, axis=-1, keepdims=True) + EPS)
        xhat = xx * r
        g = g_ref[...]
        h_ref[...] = (xhat * g).astype(BF16)
        uu = dh * g
        dx_ref[...] = dy_ + r * (uu - xhat * jnp.mean(uu * xhat, axis=-1, keepdims=True))
        dg_ref[0:1, :] += jnp.sum(dh * xhat, axis=0, keepdims=True)

    return pl.pallas_call(
        body, name="ffn_bwd", grid=(t // tt,),
        in_specs=[pl.BlockSpec((tt, d), lambda i: (i, 0)), pl.BlockSpec((tt, f), lambda i: (i, 0)),
                  pl.BlockSpec((tt, f), lambda i: (i, 0)), pl.BlockSpec((tt, d), lambda i: (i, 0)),
                  pl.BlockSpec((1, d), lambda i: (0, 0)), pl.BlockSpec(memory_space=pltpu.VMEM),
                  pl.BlockSpec(memory_space=pltpu.VMEM), pl.BlockSpec(memory_space=pltpu.VMEM)],
        out_specs=[pl.BlockSpec((tt, d), lambda i: (i, 0)), pl.BlockSpec((tt, f), lambda i: (i, 0)),
                   pl.BlockSpec((tt, f), lambda i: (i, 0)), pl.BlockSpec((tt, f), lambda i: (i, 0)),
                   pl.BlockSpec((tt, d), lambda i: (i, 0)), pl.BlockSpec((8, d), lambda i: (0, 0))],
        out_shape=[jax.ShapeDtypeStruct((t, d), F32), jax.ShapeDtypeStruct((t, f), BF16),
                   jax.ShapeDtypeStruct((t, f), BF16), jax.ShapeDtypeStruct((t, f), BF16),
                   jax.ShapeDtypeStruct((t, d), BF16), jax.ShapeDtypeStruct((8, d), F32)],
        compiler_params=_params(("arbitrary",)),
    )(dy, a, u, x1, g_ffn, w_gate, w_up, w_down)


def _in_proj_bwd(dparts, dfa_row, w1, wft, x, g_mix, dx1, rider=None):
    t, d = x.shape
    tt = 512
    npart = len(dparts)

    def body(*refs):
        dp_refs = refs[:npart]
        dfa_ref, w_ref, wf_ref, x_ref, g_ref, dx1_ref, dx_ref, dg_ref = refs[npart:]

        @pl.when(pl.program_id(0) == 0)
        def _():
            dg_ref[...] = jnp.zeros_like(dg_ref)

        dh = _dot(dfa_ref[...].astype(BF16), wf_ref[...], TN)
        for j in range(npart):
            dh = dh + _dot(dp_refs[j][...], w_ref[:, j * W_GROUP:(j + 1) * W_GROUP], NT)
        xx = x_ref[...]
        r = lax.rsqrt(jnp.mean(xx * xx, axis=-1, keepdims=True) + EPS)
        xhat = xx * r
        uu = dh * g_ref[...]
        dx_ref[...] = dx1_ref[...] + r * (uu - xhat * jnp.mean(uu * xhat, axis=-1, keepdims=True))
        dg_ref[0:1, :] += jnp.sum(dh * xhat, axis=0, keepdims=True)

    return _host_call(
        body, rider, name="in_proj_bwd", grid=(t // tt,),
        in_specs=[pl.BlockSpec((tt, W_GROUP), lambda i: (i, 0)) for _ in range(npart)]
        + [pl.BlockSpec((8, tt), lambda i: (0, i)), pl.BlockSpec(memory_space=pltpu.VMEM),
           pl.BlockSpec(memory_space=pltpu.VMEM), pl.BlockSpec((tt, d), lambda i: (i, 0)),
           pl.BlockSpec((1, d), lambda i: (0, 0)), pl.BlockSpec((tt, d), lambda i: (i, 0))],
        out_specs=[pl.BlockSpec((tt, d), lambda i: (i, 0)), pl.BlockSpec((8, d), lambda i: (0, 0))],
        out_shape=[jax.ShapeDtypeStruct((t, d), F32), jax.ShapeDtypeStruct((8, d), F32)],
        scratch_shapes=[], inputs=(*dparts, dfa_row, w1, wft, x, g_mix, dx1), semantics=("arbitrary",))


def _token_matmul(a, b, name, tn, a_is_transposed=True):
    m, t = a.shape if a_is_transposed else a.shape[::-1]
    n = b.shape[1]
    tk = 1024

    def body(a_ref, b_ref, o_ref):
        @pl.when(pl.program_id(1) == 0)
        def _():
            o_ref[...] = jnp.zeros_like(o_ref)

        o_ref[...] += _dot(a_ref[...], b_ref[...].astype(BF16), None if a_is_transposed else TN)

    a_spec = pl.BlockSpec((m, tk), lambda j, k: (0, k)) if a_is_transposed else pl.BlockSpec((tk, m), lambda j, k: (k, 0))
    return pl.pallas_call(
        body, name=name, grid=(n // tn, t // tk),
        in_specs=[a_spec, pl.BlockSpec((tk, tn), lambda j, k: (k, j))],
        out_specs=pl.BlockSpec((m, tn), lambda j, k: (0, j)),
        out_shape=jax.ShapeDtypeStruct((m, n), F32),
        compiler_params=_params(("arbitrary", "arbitrary")),
    )(a, b)


def _token_matmul_parts(at, parts, name):
    m, t = at.shape
    widths = [p.shape[1] for p in parts]
    tk = 1024

    def body(a_ref, *refs):
        o_ref = refs[-1]

        @pl.when(pl.program_id(0) == 0)
        def _():
            o_ref[...] = jnp.zeros_like(o_ref)

        a, first = a_ref[...], 0
        for b_ref, w in zip(refs[:-1], widths):
            o_ref[:, first:first + w] += _dot(a, b_ref[...])
            first += w

    return pl.pallas_call(
        body, name=name, grid=(t // tk,),
        in_specs=[pl.BlockSpec((m, tk), lambda k: (0, k))] + [pl.BlockSpec((tk, w), lambda k: (k, 0)) for w in widths],
        out_specs=pl.BlockSpec((m, sum(widths)), lambda k: (0, 0)),
        out_shape=jax.ShapeDtypeStruct((m, sum(widths)), F32),
        compiler_params=_params(("arbitrary",)),
    )(at, *parts)


def _row_matmul(a_row, b, name):
    t, n = b.shape
    tk = 1024
    nk = t // tk

    def body(a_ref, b_ref, o_ref):
        @pl.when(pl.program_id(0) == 0)
        def _():
            o_ref[...] = jnp.zeros_like(o_ref)

        o_ref[...] += _dot(a_ref[...].astype(BF16), b_ref[...])

    return pl.pallas_call(
        body, name=name, grid=(nk,),
        in_specs=[pl.BlockSpec((8, tk), lambda k: (0, k)), pl.BlockSpec((tk, n), lambda k: (k, 0))],
        out_specs=pl.BlockSpec((8, n), lambda k: (0, 0)),
        out_shape=jax.ShapeDtypeStruct((8, n), F32),
        compiler_params=_params(("arbitrary",)),
    )(a_row, b)


FOX_TQ = 256
SUM_LANE = (HEAD_DIM, 0)


def _fox_fwd(proj, c3, gq, gk, nb, seq, rider=None):
    t = nb * seq
    tq = FOX_TQ
    nq = seq // tq
    npair = N_FOX_HEADS // 2

    def body(q_ref, k_ref, v_ref, c_ref, gq_ref, gk_ref, o_ref, lse_ref, qs, ks, vs):
        ones = _group_ones()
        masks = _head_masks()
        qhat, _ = _head_norm(q_ref[...].astype(F32), None, ones)
        khat, _ = _head_norm(k_ref[...].astype(F32), None, ones)
        qs[...] = (qhat * gq_ref[...] * (SCALE * LOG2E)).astype(BF16)
        kn = khat * gk_ref[...]
        vv = v_ref[...].astype(F32)
        lane = lax.broadcasted_iota(jnp.int32, (1, LANES), 1)
        for hd in range(2):
            ks[hd] = (kn * masks[hd]).astype(BF16)
            vs[hd] = (vv * masks[hd] + (lane == SUM_LANE[hd]).astype(F32)).astype(BF16)
        row = lax.broadcasted_iota(jnp.int32, (tq, tq), 0)
        col = lax.broadcasted_iota(jnp.int32, (tq, tq), 1)
        causal = col <= row

        for qi in range(nq):
            q0 = qi * tq
            q_blk = qs[q0:q0 + tq, :]
            o_tot = jnp.zeros((tq, LANES), F32)
            lse_tot = jnp.zeros((tq, LANES), F32)
            for hd in range(2):
                crow = c_ref[0, hd:hd + 1, 0:q0 + tq] * LOG2E
                c0 = crow[:, q0:q0 + 1]
                s_d = _dot(q_blk, ks[hd, q0:q0 + tq, :], NT) + (c0 - crow[:, q0:q0 + tq])
                s_d = jnp.where(causal, s_d, NEG)
                m = jnp.max(s_d, axis=-1, keepdims=True)
                if qi > 0:
                    s_o = _dot(q_blk, ks[hd, 0:q0, :], NT) + (c0 - crow[:, 0:q0])
                    m = jnp.maximum(m, jnp.max(s_o, axis=-1, keepdims=True))
                acc = _dot(jnp.exp2(s_d - m).astype(BF16), vs[hd, q0:q0 + tq, :])
                if qi > 0:
                    acc = acc + _dot(jnp.exp2(s_o - m).astype(BF16), vs[hd, 0:q0, :])
                l = acc[:, SUM_LANE[hd]:SUM_LANE[hd] + 1]
                o_tot = o_tot + (acc / l) * masks[hd]
                lse_tot = lse_tot + (m + jnp.log2(l) - c0) * masks[hd]
            o_ref[q0:q0 + tq, :] = o_tot
            lse_ref[q0:q0 + tq, :] = lse_tot

    blk = lambda off: pl.BlockSpec((seq, LANES), lambda b, p: (b, off + p))
    return _host_call(
        body, rider, name="fox_fwd", grid=(nb, npair),
        in_specs=[blk(0), blk(npair), blk(2 * npair), pl.BlockSpec((1, 2, seq), lambda b, p: (p, 0, b)),
                  pl.BlockSpec((1, LANES), lambda b, p: (0, 0)), pl.BlockSpec((1, LANES), lambda b, p: (0, 0))],
        out_specs=[blk(0), blk(0)],
        out_shape=[jax.ShapeDtypeStruct((t, W_GROUP), F32), jax.ShapeDtypeStruct((t, W_GROUP), F32)],
        scratch_shapes=[pltpu.VMEM((seq, LANES), BF16), pltpu.VMEM((2, seq, LANES), BF16),
                        pltpu.VMEM((2, seq, LANES), BF16)],
        inputs=(proj, proj, proj, c3, gq, gk), semantics=("arbitrary", "arbitrary"))


def _fox_bwd(proj, c3, gq, gk, do, o, lse, nb, seq, rider=None):
    t = nb * seq
    tq = FOX_TQ
    nq = seq // tq
    npair = N_FOX_HEADS // 2

    def body(q_ref, k_ref, v_ref, c_ref, gq_ref, gk_ref, do_ref, o_ref, lse_ref,
             dq_ref, dk_ref, dv_ref, dc_ref, dg_ref, qs, ks, vs, kts, dos, lse_t, delta_t, dqt_acc, dk_acc, dv_acc,
             row_sum):
        @pl.when((pl.program_id(0) == 0) & (pl.program_id(1) == 0))
        def _():
            dg_ref[...] = jnp.zeros_like(dg_ref)

        ones = _group_ones()
        masks = _head_masks()
        qhat, rq = _head_norm(q_ref[...].astype(F32), None, ones)
        khat, rk = _head_norm(k_ref[...].astype(F32), None, ones)
        qs[...] = (qhat * gq_ref[...] * (SCALE * LOG2E)).astype(BF16)
        kn = khat * gk_ref[...]
        vv = v_ref[...].astype(F32)
        for hd in range(2):
            ks[hd] = (kn * masks[hd]).astype(BF16)
            vs[hd] = (vv * masks[hd]).astype(BF16)
            kts[hd] = ks[hd].T
        dof = do_ref[...]
        dos[...] = dof.astype(BF16)
        lse_t[...] = lse_ref[...].T
        delta_t[...] = _groupsum(dof * o_ref[...], ones).T
        dqt_acc[...] = jnp.zeros_like(dqt_acc)
        dk_acc[...] = jnp.zeros_like(dk_acc)
        dv_acc[...] = jnp.zeros_like(dv_acc)
        row_sum[...] = jnp.zeros_like(row_sum)
        key = lax.broadcasted_iota(jnp.int32, (tq, tq), 0)
        qry = lax.broadcasted_iota(jnp.int32, (tq, tq), 1)
        causal = key <= qry

        for hd in range(2):
            lane0 = hd * HEAD_DIM
            for kj in range(nq):
                k0 = kj * tq
                k_blk = ks[hd, k0:k0 + tq, :]
                v_blk = vs[hd, k0:k0 + tq, :]
                kt_blk = kts[hd, :, k0:k0 + tq]
                crow = c_ref[0, hd:hd + 1, k0:k0 + tq] * LOG2E
                ck0 = crow[:, 0:1]
                bias = jnp.broadcast_to(ck0 - crow, (LANES, tq)).T[:, 0:1]

                def queries_step(r0, r1, diag, hd=hd, lane0=lane0, k_blk=k_blk, v_blk=v_blk, kt_blk=kt_blk,
                                 bias=bias, ck0=ck0):
                    q_r = qs[r0:r1, :]
                    do_r = dos[r0:r1, :]
                    z = _dot(k_blk, q_r, NT) + bias
                    p = jnp.exp2(z - (lse_t[lane0:lane0 + 1, r0:r1] + ck0))
                    if diag:
                        p = jnp.where(causal, p, 0.0)
                    dp = _dot(v_blk, do_r, NT)
                    ds = p * (dp - delta_t[lane0:lane0 + 1, r0:r1])
                    dsb = ds.astype(BF16)
                    dqt_acc[:, r0:r1] += _dot(kt_blk, dsb)
                    row_sum[hd:hd + 1, r0:r1] += jnp.sum(ds, axis=0, keepdims=True)
                    return _dot(dsb, q_r), _dot(p.astype(BF16), do_r), -jnp.sum(ds, axis=1, keepdims=True)

                dk_j, dv_j, dc_j = queries_step(k0, k0 + tq, True)
                if k0 + tq < seq:
                    dk_o, dv_o, dc_o = queries_step(k0 + tq, seq, False)
                    dk_j, dv_j, dc_j = dk_j + dk_o, dv_j + dv_o, dc_j + dc_o
                dk_acc[k0:k0 + tq, :] += dk_j * masks[hd]
                dv_acc[k0:k0 + tq, :] += dv_j * masks[hd]
                dc_ref[0, hd:hd + 1, k0:k0 + tq] = jnp.broadcast_to(dc_j, (tq, LANES)).T[0:1, :]

        dc_ref[0] += row_sum[0:2, :]

        dq_raw, dgq = _head_norm_bwd(dqt_acc[...].T * SCALE, qhat, rq, gq_ref[...], ones)
        dk_raw, dgk = _head_norm_bwd(dk_acc[...] * LN2, khat, rk, gk_ref[...], ones)
        dq_ref[...] = dq_raw.astype(BF16)
        dk_ref[...] = dk_raw.astype(BF16)
        dv_ref[...] = dv_acc[...].astype(BF16)
        dg_ref[0:1, :] += dgq
        dg_ref[1:2, :] += dgk

    blk = lambda off: pl.BlockSpec((seq, LANES), lambda b, p: (b, off + p))
    vec = pl.BlockSpec((1, LANES), lambda b, p: (0, 0))
    c_spec = pl.BlockSpec((1, 2, seq), lambda b, p: (p, 0, b))
    return _host_call(
        body, rider, name="fox_bwd", grid=(nb, npair),
        in_specs=[blk(0), blk(npair), blk(2 * npair), c_spec, vec, vec, blk(0), blk(0), blk(0)],
        out_specs=[blk(0), blk(0), blk(0), c_spec, pl.BlockSpec((8, LANES), lambda b, p: (0, 0))],
        out_shape=[jax.ShapeDtypeStruct((t, W_GROUP), BF16), jax.ShapeDtypeStruct((t, W_GROUP), BF16),
                   jax.ShapeDtypeStruct((t, W_GROUP), BF16), jax.ShapeDtypeStruct((npair, 2, t), F32),
                   jax.ShapeDtypeStruct((8, LANES), F32)],
        scratch_shapes=[pltpu.VMEM((seq, LANES), BF16), pltpu.VMEM((2, seq, LANES), BF16),
                        pltpu.VMEM((2, seq, LANES), BF16), pltpu.VMEM((2, LANES, seq), BF16),
                        pltpu.VMEM((seq, LANES), BF16), pltpu.VMEM((LANES, seq), F32),
                        pltpu.VMEM((LANES, seq), F32), pltpu.VMEM((LANES, seq), F32),
                        pltpu.VMEM((seq, LANES), F32), pltpu.VMEM((seq, LANES), F32),
                        pltpu.VMEM((8, seq), F32)],
        inputs=(proj, proj, proj, c3, gq, gk, do, o, lse), semantics=("arbitrary", "arbitrary"))


def _dil_prep(q_ref, k_ref, gq_ref, gk_ref, cos_ref, up_ref, dn_ref, ones):
    qhat, rq = _head_norm(q_ref[...].astype(F32), None, ones)
    khat, rk = _head_norm(k_ref[...].astype(F32), None, ones)
    cos, up, dn = cos_ref[...], up_ref[...], dn_ref[...]
    qn = _rope(qhat * gq_ref[...], cos, up, dn) * (SCALE * LOG2E)
    kn = _rope(khat * gk_ref[...], cos, up, dn)
    return qhat, rq, khat, rk, qn, kn


def _dil_keys(d, seq, kp, vp, kw, vw):
    nblk = seq // BAND
    per_res = seq // (d * BAND)
    as_blocks = lambda ref, rows: ref[rows, :].reshape(-1, BAND, LANES)
    if per_res == 1:
        a = lax.broadcasted_iota(jnp.int32, (1, BAND, BAND), 1)
        j = lax.broadcasted_iota(jnp.int32, (1, BAND, BAND), 2)
        causal = jnp.where(j <= a, 0.0, NEG)
        return as_blocks(kp, slice(0, seq)), as_blocks(vp, slice(0, seq)), [causal]
    for src, dst in ((kp, kw), (vp, vw)):
        dst[:, BAND:, :] = as_blocks(src, slice(0, seq))
        dst[1:, :BAND, :] = as_blocks(src, slice(0, seq - BAND))
        dst[0:1, :BAND, :] = jnp.zeros((1, BAND, LANES), BF16)
    a = lax.broadcasted_iota(jnp.int32, (1, BAND, 2 * BAND), 1)
    j = lax.broadcasted_iota(jnp.int32, (1, BAND, 2 * BAND), 2)
    band = jnp.where(((j < BAND) & (j >= a)) | ((j >= BAND) & (j - BAND <= a)), 0.0, NEG)
    e = lax.broadcasted_iota(jnp.int32, (nblk, 1, 2 * BAND), 0)
    j = lax.broadcasted_iota(jnp.int32, (nblk, 1, 2 * BAND), 2)
    no_prev = jnp.where(((e & (per_res - 1)) == 0) & (j < BAND), NEG, 0.0)
    return kw[...], vw[...], [band + no_prev]


def _residues(d, seq):
    n = seq // d
    if d == 1:
        return [(slice(0, seq), slice(0, seq))]
    return [(pl.ds(r, n, stride=d), slice(r * n, (r + 1) * n)) for r in range(d)]


def _dil_fwd(proj, gq, gk, cos, up, dn, nb, seq):
    t = nb * seq
    npair = W_GROUP // LANES
    off = 3 * npair

    def body(q_ref, k_ref, v_ref, gq_ref, gk_ref, cos_ref, up_ref, dn_ref, o_ref, lse_ref,
             qs, ks, vs, qp, kp, vp, kw, vw, m_b, l_b, o_b, m_s, l_s, o_s):
        ones = _group_ones()
        masks = _head_masks()
        _, _, _, _, qn, kn = _dil_prep(q_ref, k_ref, gq_ref, gk_ref, cos_ref, up_ref, dn_ref, ones)
        qs[...] = qn
        ks[...] = kn
        vs[...] = v_ref[...].astype(F32)
        nblk = seq // BAND

        for d in DILATIONS:
            for tok, res in _residues(d, seq):
                qv = qs[tok, :]
                for hd in range(2):
                    qp[hd, res, :] = (qv * masks[hd]).astype(BF16)
                kp[res, :] = ks[tok, :].astype(BF16)
                vp[res, :] = vs[tok, :].astype(BF16)
            keys_k, keys_v, bias = _dil_keys(d, seq, kp, vp, kw, vw)
            m_t = jnp.zeros((nblk, BAND, LANES), F32)
            l_t = jnp.zeros((nblk, BAND, LANES), F32)
            o_t = jnp.zeros((nblk, BAND, LANES), F32)
            for hd in range(2):
                s = _dot(qp[hd].reshape(nblk, BAND, LANES), keys_k, BATCH_NT)
                for b_ in bias:
                    s = s + b_
                m = jnp.max(s, axis=-1, keepdims=True)
                p = jnp.exp2(s - m)
                m_t = m_t + m * masks[hd]
                l_t = l_t + jnp.sum(p, axis=-1, keepdims=True) * masks[hd]
                o_t = o_t + _dot(p.astype(BF16), keys_v, BATCH_NN) * masks[hd]
            m_b[...] = m_t.reshape(seq, LANES)
            l_b[...] = l_t.reshape(seq, LANES)
            o_b[...] = o_t.reshape(seq, LANES)
            for tok, res in _residues(d, seq):
                if d == DILATIONS[0]:
                    m_s[tok, :] = m_b[res, :]
                    l_s[tok, :] = l_b[res, :]
                    o_s[tok, :] = o_b[res, :]
                else:
                    m_old = m_s[tok, :]
                    m_new = jnp.maximum(m_old, m_b[res, :])
                    w_old = jnp.exp2(m_old - m_new)
                    w_new = jnp.exp2(m_b[res, :] - m_new)
                    l_s[tok, :] = l_s[tok, :] * w_old + l_b[res, :] * w_new
                    o_s[tok, :] = o_s[tok, :] * w_old + o_b[res, :] * w_new
                    m_s[tok, :] = m_new

        l = l_s[...]
        o_ref[...] = o_s[...] / l
        lse_ref[...] = m_s[...] + jnp.log2(l)

    blk = lambda o_: pl.BlockSpec((seq, LANES), lambda b, p: (b, o_ + p))
    vec = pl.BlockSpec((1, LANES), lambda b, p: (0, 0))
    tab = pl.BlockSpec((seq, LANES), lambda b, p: (0, 0))
    f32_buf = pltpu.VMEM((seq, LANES), F32)
    bf16_buf = pltpu.VMEM((seq, LANES), BF16)
    window_buf = pltpu.VMEM((seq // BAND, 2 * BAND, LANES), BF16)
    return pl.pallas_call(
        body, name="dil_fwd", grid=(nb, npair),
        in_specs=[blk(off), blk(off + npair), blk(off + 2 * npair), vec, vec, tab, tab, tab],
        out_specs=[blk(0), blk(0)],
        out_shape=[jax.ShapeDtypeStruct((t, W_GROUP), F32), jax.ShapeDtypeStruct((t, W_GROUP), F32)],
        scratch_shapes=[f32_buf, f32_buf, f32_buf, pltpu.VMEM((2, seq, LANES), BF16), bf16_buf, bf16_buf,
                        window_buf, window_buf, f32_buf, f32_buf, f32_buf, f32_buf, f32_buf, f32_buf],
        compiler_params=_params(("arbitrary", "arbitrary")),
    )(proj, proj, proj, gq, gk, cos, up, dn)


def _dil_bwd(proj, gq, gk, cos, up, dn, do, o, lse, nb, seq, rider=None):
    t = nb * seq
    npair = W_GROUP // LANES
    off = 3 * npair

    def body(q_ref, k_ref, v_ref, gq_ref, gk_ref, cos_ref, up_ref, dn_ref, do_ref, o_ref, lse_ref,
             dq_ref, dk_ref, dv_ref, dg_ref, qs, ks, vs, delta, dq_s, dk_s, dv_s,
             qp, kp, vp, dop, kw, vw, lse_p, delta_p, dq_p, dk_p, dv_p):
        @pl.when((pl.program_id(0) == 0) & (pl.program_id(1) == 0))
        def _():
            dg_ref[...] = jnp.zeros_like(dg_ref)

        ones = _group_ones()
        masks = _head_masks()
        qhat, rq, khat, rk, qn, kn = _dil_prep(q_ref, k_ref, gq_ref, gk_ref, cos_ref, up_ref, dn_ref, ones)
        qs[...] = qn
        ks[...] = kn
        vs[...] = v_ref[...].astype(F32)
        delta[...] = _groupsum(do_ref[...] * o_ref[...], ones)
        nblk = seq // BAND

        for d in DILATIONS:
            for tok, res in _residues(d, seq):
                qv = qs[tok, :]
                dov = do_ref[tok, :]
                for hd in range(2):
                    qp[hd, res, :] = (qv * masks[hd]).astype(BF16)
                    dop[hd, res, :] = (dov * masks[hd]).astype(BF16)
                kp[res, :] = ks[tok, :].astype(BF16)
                vp[res, :] = vs[tok, :].astype(BF16)
                lse_p[res, :] = lse_ref[tok, :]
                delta_p[res, :] = delta[tok, :]
            keys_k, keys_v, bias = _dil_keys(d, seq, kp, vp, kw, vw)
            nk = keys_k.shape[1]
            dq_b = jnp.zeros((nblk, BAND, LANES), F32)
            dk_b = jnp.zeros((nblk, nk, LANES), F32)
            dv_b = jnp.zeros((nblk, nk, LANES), F32)
            for hd in range(2):
                lane0 = hd * HEAD_DIM
                q3 = qp[hd].reshape(nblk, BAND, LANES)
                do3 = dop[hd].reshape(nblk, BAND, LANES)
                z = _dot(q3, keys_k, BATCH_NT)
                for b_ in bias:
                    z = z + b_
                p = jnp.exp2(z - lse_p[...].reshape(nblk, BAND, LANES)[:, :, lane0:lane0 + 1])
                dp = _dot(do3, keys_v, BATCH_NT)
                ds = (p * (dp - delta_p[...].reshape(nblk, BAND, LANES)[:, :, lane0:lane0 + 1])).astype(BF16)
                dq_b = dq_b + _dot(ds, keys_k, BATCH_NN) * masks[hd]
                dk_b = dk_b + _dot(ds, q3, BATCH_TN)
                dv_b = dv_b + _dot(p.astype(BF16), do3, BATCH_TN)
            dq_p[...] = dq_b.reshape(seq, LANES)
            for acc, out in ((dk_b, dk_p), (dv_b, dv_p)):
                out[...] = acc[:, nk - BAND:, :].reshape(seq, LANES)
                if nk > BAND:
                    out[0:seq - BAND, :] += acc[1:, :BAND, :].reshape(seq - BAND, LANES)
            for tok, res in _residues(d, seq):
                if d == DILATIONS[0]:
                    dq_s[tok, :] = dq_p[res, :]
                    dk_s[tok, :] = dk_p[res, :]
                    dv_s[tok, :] = dv_p[res, :]
                else:
                    dq_s[tok, :] += dq_p[res, :]
                    dk_s[tok, :] += dk_p[res, :]
                    dv_s[tok, :] += dv_p[res, :]

        cos, up, dn = cos_ref[...], up_ref[...], dn_ref[...]
        dq_raw, dgq = _head_norm_bwd(_rope_bwd(dq_s[...] * SCALE, cos, up, dn), qhat, rq, gq_ref[...], ones)
        dk_raw, dgk = _head_norm_bwd(_rope_bwd(dk_s[...] * LN2, cos, up, dn), khat, rk, gk_ref[...], ones)
        dq_ref[...] = dq_raw.astype(BF16)
        dk_ref[...] = dk_raw.astype(BF16)
        dv_ref[...] = dv_s[...].astype(BF16)
        dg_ref[0:1, :] += dgq
        dg_ref[1:2, :] += dgk

    blk = lambda o_: pl.BlockSpec((seq, LANES), lambda b, p: (b, o_ + p))
    vec = pl.BlockSpec((1, LANES), lambda b, p: (0, 0))
    tab = pl.BlockSpec((seq, LANES), lambda b, p: (0, 0))
    f32_buf = pltpu.VMEM((seq, LANES), F32)
    bf16_buf = pltpu.VMEM((seq, LANES), BF16)
    window_buf = pltpu.VMEM((seq // BAND, 2 * BAND, LANES), BF16)
    bf16_pair = pltpu.VMEM((2, seq, LANES), BF16)
    return _host_call(
        body, rider, name="dil_bwd", grid=(nb, npair),
        in_specs=[blk(off), blk(off + npair), blk(off + 2 * npair), vec, vec, tab, tab, tab,
                  blk(0), blk(0), blk(0)],
        out_specs=[blk(0), blk(0), blk(0), pl.BlockSpec((8, LANES), lambda b, p: (0, 0))],
        out_shape=[jax.ShapeDtypeStruct((t, W_GROUP), BF16), jax.ShapeDtypeStruct((t, W_GROUP), BF16),
                   jax.ShapeDtypeStruct((t, W_GROUP), BF16), jax.ShapeDtypeStruct((8, LANES), F32)],
        scratch_shapes=[f32_buf] * 7 + [bf16_pair, bf16_buf, bf16_buf, bf16_pair, window_buf, window_buf]
        + [f32_buf] * 5,
        inputs=(proj, proj, proj, gq, gk, cos, up, dn, do, o, lse), semantics=("arbitrary", "arbitrary"))


def _adamw(w, g, m, v, name, rider=None):
    rows, cols = w.shape[-2:]
    tr = _row_tile(rows) if rows >= 8 else rows
    c1 = 1.0 - ADAM_B1 ** ADAM_STEP
    c2 = 1.0 - ADAM_B2 ** ADAM_STEP

    def body(w_ref, g_ref, m_ref, v_ref, d_ref, nm_ref, nv_ref):
        g_ = g_ref[...]
        nm = ADAM_B1 * m_ref[...] + (1.0 - ADAM_B1) * g_
        nv = ADAM_B2 * v_ref[...] + (1.0 - ADAM_B2) * (g_ * g_)
        nm_ref[...] = nm
        nv_ref[...] = nv
        d_ref[...] = -ADAM_LR * ((nm / c1) / (jnp.sqrt(nv / c2) + ADAM_EPS) + ADAM_WD * w_ref[...])

    if w.ndim == 3:
        spec = pl.BlockSpec((1, tr, cols), lambda i: (0, i, 0))
    else:
        spec = pl.BlockSpec((tr, cols), lambda i: (i, 0))
    shape = jax.ShapeDtypeStruct(w.shape, F32)
    return _host_call(
        body, rider, name=name, grid=(rows // tr,), in_specs=[spec] * 4, out_specs=[spec] * 3,
        out_shape=[shape] * 3, scratch_shapes=[], inputs=(w, g, m, v), semantics=("arbitrary",))


def _place():
    x, y, c = lax.axis_index("x"), lax.axis_index("y"), lax.axis_index("c")
    chips = [(1 - x, y), (x, 1 - y), (1 - x, 1 - y)]
    return x, y, c, chips


def _gather_weight(w, name):
    _, rows, cols = w.shape
    half_rows = rows // 2

    def body(w_ref, out_ref, send_sems, recv_sems):
        x, y, c, chips = _place()
        sibling = (x, y, 1 - c)
        mine = 2 * x + y
        lo = pl.multiple_of(c * half_rows, 16)
        lo_sib = pl.multiple_of((1 - c) * half_rows, 16)
        out_ref[mine] = w_ref[0].astype(BF16)

        def copy(k, shard, first_row, to):
            ref = out_ref.at[shard, pl.ds(first_row, half_rows), :]
            return pltpu.make_async_remote_copy(src_ref=ref, dst_ref=ref, send_sem=send_sems.at[k],
                                                recv_sem=recv_sems.at[k], device_id=to, device_id_type=MESH)

        sends = [copy(k, mine, lo, (cx, cy, c)) for k, (cx, cy) in enumerate(chips)]
        for cp in sends:
            cp.start()
        passed = []
        for k, (cx, cy) in enumerate(chips):
            theirs = 2 * cx + cy
            copy(k, theirs, lo, (cx, cy, c)).wait_recv()
            fw = copy(3 + k, theirs, lo, sibling)
            fw.start()
            passed.append(fw)
        for k, (cx, cy) in enumerate(chips):
            copy(3 + k, 2 * cx + cy, lo_sib, sibling).wait_recv()
        for cp in sends + passed:
            cp.wait_send()

    return pl.pallas_call(
        body, name=name,
        in_specs=[pl.BlockSpec(memory_space=pltpu.VMEM)],
        out_specs=pl.BlockSpec(memory_space=pltpu.VMEM),
        out_shape=jax.ShapeDtypeStruct((4, rows, cols), BF16),
        scratch_shapes=[pltpu.SemaphoreType.DMA((6,)), pltpu.SemaphoreType.DMA((6,))],
        compiler_params=pltpu.CompilerParams(vmem_limit_bytes=VMEM_LIMIT),
    )(w)


def _remote(src, dst, sems, k, to):
    send_sems, recv_sems = sems
    return pltpu.make_async_remote_copy(src_ref=src, dst_ref=dst, send_sem=send_sems.at[k], recv_sem=recv_sems.at[k],
                                        device_id=to, device_id_type=MESH)


def _cast_bf16(parts, name):
    def body(*refs):
        for src, dst in zip(refs[:len(parts)], refs[len(parts):]):
            dst[...] = src[0].astype(BF16)

    return pl.pallas_call(
        body, name=name, in_specs=[pl.BlockSpec(memory_space=pltpu.VMEM)] * len(parts),
        out_specs=[pl.BlockSpec(memory_space=pltpu.VMEM)] * len(parts),
        out_shape=[jax.ShapeDtypeStruct(p.shape[1:], BF16) for p in parts],
        compiler_params=pltpu.CompilerParams(vmem_limit_bytes=VMEM_LIMIT),
    )(*parts)


def _gather_rider(shards):
    def copies(ins, outs, sems, which):
        x, y, c, chips = _place()
        sibling = (x, y, 1 - c)
        mine = 2 * x + y
        made = {name: [] for name in which}
        for i, (p_ref, g_ref) in enumerate(zip(ins, outs)):
            half = p_ref.shape[0] // 2
            lo = pl.multiple_of(c * half, 16)
            lo_sib = pl.multiple_of((1 - c) * half, 16)
            spot = lambda shard, first, g_ref=g_ref, half=half: g_ref.at[shard, pl.ds(first, half), :]
            groups = {
                "own": lambda: [pltpu.make_async_copy(p_ref, g_ref.at[mine], sems[0].at[7 * i + 6])],
                "sends": lambda: [_remote(p_ref.at[pl.ds(lo, half), :], spot(mine, lo), sems, 7 * i + k, (cx, cy, c))
                                  for k, (cx, cy) in enumerate(chips)],
                "arrivals": lambda: [_remote(spot(2 * cx + cy, lo), spot(2 * cx + cy, lo), sems, 7 * i + k, (cx, cy, c))
                                     for k, (cx, cy) in enumerate(chips)],
                "passes": lambda: [_remote(spot(2 * cx + cy, lo), spot(2 * cx + cy, lo), sems, 7 * i + 3 + k, sibling)
                                   for k, (cx, cy) in enumerate(chips)],
                "from_sibling": lambda: [_remote(spot(2 * cx + cy, lo_sib), spot(2 * cx + cy, lo_sib), sems,
                                                 7 * i + 3 + k, sibling) for k, (cx, cy) in enumerate(chips)],
            }
            for name in which:
                made[name] += groups[name]()
        return [made[name] for name in which]

    def start(ins, outs, send_sems, recv_sems):
        own, sends = copies(ins, outs, (send_sems, recv_sems), ("own", "sends"))
        for cp in own + sends:
            cp.start()

    def middle(ins, outs, send_sems, recv_sems):
        arrivals, passes = copies(ins, outs, (send_sems, recv_sems), ("arrivals", "passes"))
        for landed, onward in zip(arrivals, passes):
            landed.wait_recv()
            onward.start()

    def finish(ins, outs, send_sems, recv_sems):
        own, sends, passes, from_sibling = copies(ins, outs, (send_sems, recv_sems),
                                                  ("own", "sends", "passes", "from_sibling"))
        for cp in from_sibling:
            cp.wait_recv()
        for cp in sends + passes:
            cp.wait_send()
        for cp in own:
            cp.wait()

    shapes = [jax.ShapeDtypeStruct((4,) + s.shape, BF16) for s in shards]
    return _Rider(shards, shapes, 7 * len(shards), start, finish, middle=middle)


def _exchange_rider(inputs, out_shapes, n_sems, copies, aliases=None):
    def start(ins, outs, send_sems, recv_sems):
        for cp in copies(ins, outs, (send_sems, recv_sems)):
            cp.start()

    def finish(ins, outs, send_sems, recv_sems):
        for cp in copies(ins, outs, (send_sems, recv_sems)):
            cp.wait()

    return _Rider(inputs, out_shapes, n_sems, start, finish, aliases)


def _swap_rider(grads4):
    halves = [g.shape[1] // 2 for g in grads4]

    def copies(ins, outs, sems):
        x, y, c, _ = _place()
        return [_remote(g.at[:, pl.ds(pl.multiple_of((1 - c) * h, 8), h), :], a, sems, i, (x, y, 1 - c))
                for i, (g, a, h) in enumerate(zip(ins, outs, halves))]

    shapes = [jax.ShapeDtypeStruct((4, h, g.shape[2]), F32) for g, h in zip(grads4, halves)]
    return _exchange_rider(grads4, shapes, len(grads4), copies)


def _chip_sum(g4, from_sibling, name):
    _, rows, cols = g4.shape
    half = rows // 2

    def body(g_ref, s_ref, stage_ref, own_ref):
        x, y, c, chips = _place()
        lo = pl.multiple_of(c * half, 8)
        for k, (cx, cy) in enumerate(chips):
            theirs = 2 * cx + cy
            stage_ref[k] = (g_ref[theirs, pl.ds(lo, half), :] + s_ref[theirs]).astype(BF16)
        mine = 2 * x + y
        own_ref[...] = g_ref[mine, pl.ds(lo, half), :] + s_ref[mine]

    return pl.pallas_call(
        body, name=name, in_specs=[pl.BlockSpec(memory_space=pltpu.VMEM)] * 2,
        out_specs=[pl.BlockSpec(memory_space=pltpu.VMEM)] * 2,
        out_shape=[jax.ShapeDtypeStruct((3, half, cols), BF16), jax.ShapeDtypeStruct((half, cols), F32)],
        compiler_params=pltpu.CompilerParams(vmem_limit_bytes=VMEM_LIMIT),
    )(g4, from_sibling)


def _spread_rider(stages):
    def copies(ins, outs, sems):
        _, _, c, chips = _place()
        return [_remote(st.at[k], ld.at[k], sems, 3 * i + k, (cx, cy, c))
                for i, (st, ld) in enumerate(zip(ins, outs)) for k, (cx, cy) in enumerate(chips)]

    shapes = [jax.ShapeDtypeStruct(s.shape, s.dtype) for s in stages]
    return _exchange_rider(stages, shapes, 3 * len(stages), copies)


def _finish_half(own, landed, name):
    half, cols = own.shape

    def body(own_ref, landed_ref, out_ref):
        c = lax.axis_index("c")
        acc = own_ref[...]
        for k in range(3):
            acc = acc + landed_ref[k].astype(F32)
        out_ref[pl.ds(pl.multiple_of(c * half, 8), half), :] = acc

    return pl.pallas_call(
        body, name=name, in_specs=[pl.BlockSpec(memory_space=pltpu.VMEM)] * 2,
        out_specs=pl.BlockSpec(memory_space=pltpu.VMEM),
        out_shape=jax.ShapeDtypeStruct((2 * half, cols), F32),
        compiler_params=pltpu.CompilerParams(vmem_limit_bytes=VMEM_LIMIT),
    )(own, landed)


def _share_rider(fulls):
    def copies(ins, outs, sems):
        x, y, c, _ = _place()
        out = []
        for i, full in enumerate(outs):
            half = full.shape[0] // 2
            rows = full.at[pl.ds(pl.multiple_of(c * half, 8), half), :]
            out.append(_remote(rows, rows, sems, i, (x, y, 1 - c)))
        return out

    def finish_copies(ins, outs, sems):
        x, y, c, _ = _place()
        out = []
        for i, full in enumerate(outs):
            half = full.shape[0] // 2
            mine = full.at[pl.ds(pl.multiple_of(c * half, 8), half), :]
            theirs = full.at[pl.ds(pl.multiple_of((1 - c) * half, 8), half), :]
            out.append((_remote(mine, mine, sems, i, (x, y, 1 - c)), _remote(theirs, theirs, sems, i, (x, y, 1 - c))))
        return out

    def start(ins, outs, send_sems, recv_sems):
        for cp in copies(ins, outs, (send_sems, recv_sems)):
            cp.start()

    def finish(ins, outs, send_sems, recv_sems):
        for sent, landed in finish_copies(ins, outs, (send_sems, recv_sems)):
            sent.wait_send()
            landed.wait_recv()

    shapes = [jax.ShapeDtypeStruct(f.shape, f.dtype) for f in fulls]
    return _Rider(fulls, shapes, len(fulls), start, finish, aliases={i: i for i in range(len(fulls))})


def _all_sum_small(v):
    shape = v.shape

    def body(v_ref, out_ref, buf, send_sems, recv_sems):
        x, y, c, _ = _place()
        me = 4 * x + 2 * y + c
        buf[me] = v_ref[...]
        flips = [(dx, dy, dc) for dx in (0, 1) for dy in (0, 1) for dc in (0, 1)][1:]

        def copy(k, slot, flip):
            dx, dy, dc = flip
            to = (1 - x if dx else x, 1 - y if dy else y, 1 - c if dc else c)
            return pltpu.make_async_remote_copy(src_ref=buf.at[slot], dst_ref=buf.at[slot], send_sem=send_sems.at[k],
                                                recv_sem=recv_sems.at[k], device_id=to, device_id_type=MESH)

        sends = [copy(k, me, flip) for k, flip in enumerate(flips)]
        for cp in sends:
            cp.start()
        for k, (dx, dy, dc) in enumerate(flips):
            sender = 4 * (1 - x if dx else x) + 2 * (1 - y if dy else y) + (1 - c if dc else c)
            copy(k, sender, (dx, dy, dc)).wait_recv()
        for cp in sends:
            cp.wait_send()
        total = buf[0]
        for i in range(1, 8):
            total = total + buf[i]
        out_ref[...] = total

    return pl.pallas_call(
        body, name="all_sum_small",
        in_specs=[pl.BlockSpec(memory_space=pltpu.VMEM)],
        out_specs=pl.BlockSpec(memory_space=pltpu.VMEM),
        out_shape=jax.ShapeDtypeStruct(shape, F32),
        scratch_shapes=[pltpu.VMEM((8,) + shape, F32), pltpu.SemaphoreType.DMA((7,)), pltpu.SemaphoreType.DMA((7,))],
    )(v)


SMALL = (("g_mix", 1024), ("g_ffn", 1024), ("g_out_fox", 512), ("g_out_dil", 512), ("g_q_fox", 64),
         ("g_k_fox", 64), ("g_q_dil", 64), ("g_k_dil", 64), ("b_forget", 8))
SMALL_PACKED = (32, LANES)


def _local_grads(x, target, gains, w1, wft, dense, packed, nb, seq):
    tile2 = lambda g: jnp.tile(g, (1, 2))
    gq_f, gk_f, gq_d, gk_d = (tile2(gains[n]) for n in ("g_q_fox", "g_k_fox", "g_q_dil", "g_k_dil"))
    b_col = gains["b_forget"].reshape(N_FOX_HEADS, 1)
    cos, up, dn = _rope_tables(seq)
    npair = N_FOX_HEADS // 2

    proj, fa_row, h1, h1_t = _in_proj(x, gains["g_mix"], w1, wft)
    c_row = _gate_fwd(fa_row, b_col, seq)
    c3 = c_row.reshape(npair, 2, nb * seq)
    (o_fox, lse_fox), gathered = _fox_fwd(proj, c3, gq_f, gk_f, nb, seq,
                                          rider=None if packed is None else _gather_rider(packed))
    if packed is not None:
        dense = [g.reshape(-1, g.shape[2]) for g in gathered]
    w_out, w_gate, w_up, w_down = dense
    o_dil, lse_dil = _dil_fwd(proj, gq_d, gk_d, cos, up, dn, nb, seq)
    x1, o_n_t = _attn_out(o_fox, o_dil, x, gains["g_out_fox"], gains["g_out_dil"], w_out)
    a, u, dy, loss_parts = _ffn_fwd(x1, target, gains["g_ffn"], w_gate, w_up, w_down)
    loss = jnp.sum(loss_parts[:, 0, 0])

    dx1, s, da, du, h2, dg_ffn = _ffn_bwd(dy, a, u, x1, gains["g_ffn"], w_gate, w_up, w_down)
    d_w_down = _token_matmul(s, dy, "dw_down", 512, False)
    d_w_gate = _token_matmul(da, h2, "dw_gate", 512, False)
    d_w_up = _token_matmul(du, h2, "dw_up", 512, False)
    d_w_out = _token_matmul(o_n_t, dx1, "dw_out", 1024)
    names = ("w_out", "w_gate", "w_up", "w_down")
    grads4 = [g.reshape(4, -1, g.shape[1]) for g in (d_w_out, d_w_gate, d_w_up, d_w_down)]
    exchange = packed is not None
    (do_fox, do_dil, dg_of, dg_od), from_sibling = _attn_out_bwd(
        dx1, o_fox, o_dil, gains["g_out_fox"], gains["g_out_dil"], w_out,
        rider=_swap_rider(grads4) if exchange else None)
    if exchange:
        sums = [_chip_sum(g, s, "chip_sum_" + n) for g, s, n in zip(grads4, from_sibling, names)]
    (dq_f, dk_f, dv_f, dc3, dg_fox), landed = _fox_bwd(
        proj, c3, gq_f, gk_f, do_fox, o_fox, lse_fox, nb, seq,
        rider=_spread_rider([st for st, _ in sums]) if exchange else None)
    if exchange:
        halves = [_finish_half(own, ld, "finish_half_" + n) for (_, own), ld, n in zip(sums, landed, names)]
    (dq_d, dk_d, dv_d, dg_dil), reduced = _dil_bwd(
        proj, gq_d, gk_d, cos, up, dn, do_dil, o_dil, lse_dil, nb, seq,
        rider=_share_rider(halves) if exchange else None)
    if exchange:
        d_w_out, d_w_gate, d_w_up, d_w_down = reduced
    dfa_row, db = _gate_bwd(dc3.reshape(N_FOX_HEADS, nb * seq), fa_row, b_col, seq)
    dparts = [dq_f, dk_f, dv_f, dq_d, dk_d, dv_d]
    d_w1 = _token_matmul_parts(h1_t, dparts, "dw_in")
    d_wf = _row_matmul(dfa_row, h1, "dw_forget")
    fox_w = 3 * W_GROUP
    d_w_in = jnp.concatenate([d_w1[:, :fox_w], d_wf.T, d_w1[:, fox_w:]], axis=1)
    if exchange:
        shards = [_shards_of_columns(d_w_in)]
        _, from_sibling = _idle_host(_swap_rider(shards), "swap_w_in")
        stage, own = _chip_sum(shards[0], from_sibling[0], "chip_sum_w_in")
    (grad_x, dg_mix), landed = _in_proj_bwd(dparts, dfa_row, w1, wft, x, gains["g_mix"], dx1,
                                            rider=_spread_rider([stage]) if exchange else None)
    if exchange:
        d_w_in = _finish_half(own, landed[0], "finish_half_w_in")

    fold = lambda g2: (g2[:, :HEAD_DIM] + g2[:, HEAD_DIM:])
    small = {
        "g_mix": dg_mix[0:1], "g_ffn": dg_ffn[0:1], "g_out_fox": dg_of[0:1], "g_out_dil": dg_od[0:1],
        "g_q_fox": fold(dg_fox[0:1]), "g_k_fox": fold(dg_fox[1:2]),
        "g_q_dil": fold(dg_dil[0:1]), "g_k_dil": fold(dg_dil[1:2]),
        "b_forget": db[:, 0].reshape(1, N_FOX_HEADS),
    }
    big = {"w_in": d_w_in, "w_out": d_w_out, "w_gate": d_w_gate, "w_up": d_w_up, "w_down": d_w_down}
    return loss, grad_x, big, small


def _shards_of_columns(full, n=4):
    r, nc = full.shape
    return full.reshape(r, n, nc // n).transpose(1, 0, 2)


def _columns_of_shards(slabs):
    n, r, c = slabs.shape
    return slabs.transpose(1, 0, 2).reshape(r, n * c)


def kernel(x, g_mix, w_in, b_forget, g_q_fox, g_k_fox, g_q_dil, g_k_dil, g_out_fox, g_out_dil, w_out, g_ffn, w_gate, w_up, w_down, loss_target, m_g_mix, m_w_in, m_b_forget, m_g_q_fox, m_g_k_fox, m_g_q_dil, m_g_k_dil, m_g_out_fox, m_g_out_dil, m_w_out, m_g_ffn, m_w_gate, m_w_up, m_w_down, v_g_mix, v_w_in, v_b_forget, v_g_q_fox, v_g_k_fox, v_g_q_dil, v_g_k_dil, v_g_out_fox, v_g_out_dil, v_w_out, v_g_ffn, v_w_gate, v_w_up, v_w_down):
    nb, seq, d = x.shape
    weights = dict(g_mix=g_mix, w_in=w_in, b_forget=b_forget, g_q_fox=g_q_fox, g_k_fox=g_k_fox, g_q_dil=g_q_dil,
                   g_k_dil=g_k_dil, g_out_fox=g_out_fox, g_out_dil=g_out_dil, w_out=w_out, g_ffn=g_ffn,
                   w_gate=w_gate, w_up=w_up, w_down=w_down)
    m_in = dict(g_mix=m_g_mix, w_in=m_w_in, b_forget=m_b_forget, g_q_fox=m_g_q_fox, g_k_fox=m_g_k_fox,
                g_q_dil=m_g_q_dil, g_k_dil=m_g_k_dil, g_out_fox=m_g_out_fox, g_out_dil=m_g_out_dil, w_out=m_w_out,
                g_ffn=m_g_ffn, w_gate=m_w_gate, w_up=m_w_up, w_down=m_w_down)
    v_in = dict(g_mix=v_g_mix, w_in=v_w_in, b_forget=v_b_forget, g_q_fox=v_g_q_fox, g_k_fox=v_g_k_fox,
                g_q_dil=v_g_q_dil, g_k_dil=v_g_k_dil, g_out_fox=v_g_out_fox, g_out_dil=v_g_out_dil, w_out=v_w_out,
                g_ffn=v_g_ffn, w_gate=v_w_gate, w_up=v_w_up, w_down=v_w_down)
    order = ["g_mix", "w_in", "b_forget", "g_q_fox", "g_k_fox", "g_q_dil", "g_k_dil", "g_out_fox", "g_out_dil",
             "w_out", "g_ffn", "w_gate", "w_up", "w_down"]

    w_in_full = _columns_of_shards(_gather_weight(w_in, "gather_w_in"))
    fox_w = 3 * W_GROUP
    w1 = jnp.concatenate([w_in_full[:, :fox_w], w_in_full[:, fox_w + N_FOX_HEADS:]], axis=1)
    wft = w_in_full[:, fox_w:fox_w + N_FOX_HEADS].T
    swap = lambda a: jnp.transpose(a, (0, 2, 1))
    for n in ("w_gate", "w_up"):
        weights[n], m_in[n], v_in[n] = swap(weights[n]), swap(m_in[n]), swap(v_in[n])
    shards = _cast_bf16([weights[n] for n in ("w_out", "w_gate", "w_up", "w_down")], "cast_shards")

    gains = {n: weights[n] for n, _ in SMALL}
    loss, grad_x, big, small = _local_grads(
        x.reshape(nb * seq, d), loss_target.reshape(nb * seq, d), gains, w1, wft, None, shards, nb, seq)

    grads = {n: big[n][None] for n in ("w_out", "w_gate", "w_up", "w_down")}
    packed = jnp.concatenate([small[n].reshape(-1) for n, _ in SMALL] + [loss.reshape(1)])
    packed = jnp.pad(packed, (0, SMALL_PACKED[0] * SMALL_PACKED[1] - packed.shape[0])).reshape(SMALL_PACKED)
    summed = _all_sum_small(packed).reshape(-1)
    pos = 0
    for n, size in SMALL:
        grads[n] = summed[pos:pos + size].reshape(1, size)
        pos += size
    loss = summed[pos]

    deltas, new_m, new_v, grad_out = {}, {}, {}, {}
    for n in ["w_down"] + [n for n in order if n != "w_down"]:
        rider = _share_rider([big["w_in"]]) if n == "w_down" else None
        (deltas[n], new_m[n], new_v[n]), shared = _adamw(weights[n], grads[n], m_in[n], v_in[n], "adamw_" + n, rider)
        if rider is not None:
            grads["w_in"] = shared[0][None]
        grad_out[n] = grads[n]
    for n in ("w_gate", "w_up"):
        grad_out[n], deltas[n], new_m[n], new_v[n] = (swap(a) for a in (grad_out[n], deltas[n], new_m[n], new_v[n]))

    return (loss, grad_x.reshape(nb, seq, d), *[grad_out[n] for n in order], *[deltas[n] for n in order],
            *[new_m[n] for n in order], *[new_v[n] for n in order])
```

```python
import functools
import math

import numpy as np
import jax
import jax.numpy as jnp
from jax import lax
from jax.experimental import pallas as pl
from jax.experimental.pallas import tpu as pltpu

F32, BF16 = jnp.float32, jnp.bfloat16
MESH = pl.DeviceIdType.MESH

EPS = 1e-6
NEG = -1e30
HEAD_DIM = 64
SCALE = HEAD_DIM ** -0.5
LOG2E = math.log2(math.e)
LN2 = math.log(2.0)
ROPE_THETA = 500000.0
ROPE_DIM = HEAD_DIM // 4
LANES = 128
W_GROUP = 512
N_FOX_HEADS = 8
VMEM_LIMIT = 56 * 1024 * 1024
DILATIONS = (1, 4, 16)
BAND = 128

ADAM_LR, ADAM_B1, ADAM_B2, ADAM_EPS, ADAM_WD, ADAM_STEP = 0.001, 0.9, 0.999, 1e-08, 0.01, 10

NT = (((1,), (1,)), ((), ()))
TN = (((0,), (0,)), ((), ()))
BATCH_NT = (((2,), (2,)), ((0,), (0,)))
BATCH_NN = (((2,), (1,)), ((0,), (0,)))
BATCH_TN = (((1,), (1,)), ((0,), (0,)))


def _params(sem=None):
    return pltpu.CompilerParams(dimension_semantics=sem, vmem_limit_bytes=VMEM_LIMIT)


def _dot(a, b, dims=None):
    if dims is None:
        return jnp.dot(a, b, preferred_element_type=F32)
    return lax.dot_general(a, b, dims, preferred_element_type=F32)


def _group_ones():
    i = lax.broadcasted_iota(jnp.int32, (LANES, LANES), 0) >> 6
    j = lax.broadcasted_iota(jnp.int32, (LANES, LANES), 1) >> 6
    return (i == j).astype(BF16)


def _split3(x):
    a = x.astype(BF16)
    r = x - a.astype(F32)
    b = r.astype(BF16)
    c = (r - b.astype(F32)).astype(BF16)
    return a, b, c


def _groupsum(x, ones):
    a, b, c = _split3(x)
    return _dot(a, ones) + _dot(b, ones) + _dot(c, ones)


def _head_masks():
    lane = lax.broadcasted_iota(jnp.int32, (1, LANES), 1)
    return [(lane < HEAD_DIM).astype(F32), (lane >= HEAD_DIM).astype(F32)]


def _head_norm(raw, gain, ones):
    r = lax.rsqrt(_groupsum(raw * raw, ones) * (1.0 / HEAD_DIM) + EPS)
    return raw * r, r


def _head_norm_bwd(dy, xhat, r, gain, ones):
    u = dy * gain
    dgain = jnp.sum(dy * xhat, axis=0, keepdims=True)
    draw = r * (u - xhat * (_groupsum(u * xhat, ones) * (1.0 / HEAD_DIM)))
    return draw, dgain


def _rope(x, cos, s_up, s_dn):
    return x * cos + pltpu.roll(x, LANES - 8, 1) * s_up + pltpu.roll(x, 8, 1) * s_dn


def _rope_bwd(dy, cos, s_up, s_dn):
    return dy * cos + pltpu.roll(dy * s_up, 8, 1) + pltpu.roll(dy * s_dn, LANES - 8, 1)


def _rope_tables(seq):
    half = ROPE_DIM // 2
    inv_freq = jnp.power(jnp.float32(ROPE_THETA), -jnp.arange(half, dtype=F32) * 2.0 / ROPE_DIM)
    ang = jnp.arange(seq).astype(F32)[:, None] * inv_freq[None, :]
    cos, sin = jnp.cos(ang), jnp.sin(ang)
    one = jnp.ones((seq, HEAD_DIM - ROPE_DIM), F32)
    zero_h = jnp.zeros((seq, half), F32)
    zero_r = jnp.zeros((seq, HEAD_DIM - ROPE_DIM), F32)
    c = jnp.concatenate([cos, cos, one], axis=1)
    up = jnp.concatenate([-sin, zero_h, zero_r], axis=1)
    dn = jnp.concatenate([zero_h, sin, zero_r], axis=1)
    return jnp.tile(c, (1, 2)), jnp.tile(up, (1, 2)), jnp.tile(dn, (1, 2))


def _row_tile(rows, cap=256):
    best = rows
    for t in range(8, min(rows, cap) + 1, 8):
        if rows % t == 0:
            best = t
    return best


class _Rider:
    def __init__(self, inputs, out_shapes, n_sems, start, finish, aliases=None, middle=None):
        self.inputs, self.out_shapes, self.n_sems = list(inputs), list(out_shapes), n_sems
        self.start, self.finish, self.middle, self.aliases = start, finish, middle, dict(aliases or {})


def _host_call(body, rider, *, name, grid, in_specs, out_specs, out_shape, scratch_shapes, inputs, semantics):
    if rider is None:
        return pl.pallas_call(body, name=name, grid=grid, in_specs=in_specs, out_specs=out_specs,
                              out_shape=out_shape, scratch_shapes=scratch_shapes,
                              compiler_params=_params(semantics))(*inputs), []
    n_in, n_out, n_scr = len(in_specs), len(out_specs), len(scratch_shapes)
    r_in, r_out = len(rider.inputs), len(rider.out_shapes)

    def wrapped(*refs):
        ins, refs = refs[:n_in], refs[n_in:]
        r_ins, refs = refs[:r_in], refs[r_in:]
        outs, refs = refs[:n_out], refs[n_out:]
        r_outs, refs = refs[:r_out], refs[r_out:]
        scratch, (send_sems, recv_sems) = refs[:n_scr], refs[n_scr:]
        ids = [pl.program_id(a) for a in range(len(grid))]
        first = functools.reduce(lambda p, q: p & q, [i == 0 for i in ids])
        last = functools.reduce(lambda p, q: p & q, [i == g - 1 for i, g in zip(ids, grid)])

        @pl.when(first)
        def _():
            rider.start(r_ins, r_outs, send_sems, recv_sems)

        body(*ins, *outs, *scratch)

        if rider.middle is not None:
            step, steps = ids[0], grid[0]
            for i, g in zip(ids[1:], grid[1:]):
                step, steps = step * g + i, steps * g

            @pl.when(step == (3 * steps) // 4)
            def _():
                rider.middle(r_ins, r_outs, send_sems, recv_sems)

        @pl.when(last)
        def _():
            rider.finish(r_ins, r_outs, send_sems, recv_sems)

    hbm = pl.BlockSpec(memory_space=pl.ANY)
    res = pl.pallas_call(
        wrapped, name=name, grid=grid,
        in_specs=list(in_specs) + [hbm] * r_in, out_specs=list(out_specs) + [hbm] * r_out,
        out_shape=list(out_shape) + rider.out_shapes,
        scratch_shapes=list(scratch_shapes) + [pltpu.SemaphoreType.DMA((rider.n_sems,))] * 2,
        input_output_aliases={n_in + i: n_out + o for i, o in rider.aliases.items()},
        compiler_params=_params(semantics),
    )(*inputs, *rider.inputs)
    return res[:n_out], res[n_out:]


def _idle_host(rider, name):
    def body(o_ref):
        o_ref[...] = jnp.zeros_like(o_ref)

    return _host_call(body, rider, name=name, grid=(1,), in_specs=[],
                      out_specs=[pl.BlockSpec((8, LANES), lambda i: (0, 0))],
                      out_shape=[jax.ShapeDtypeStruct((8, LANES), F32)], scratch_shapes=[], inputs=(),
                      semantics=("arbitrary",))


def _in_proj(x, g_mix, w1, wft):
    t, d = x.shape
    n = w1.shape[1]
    tt = 512

    def body(x_ref, g_ref, w_ref, wf_ref, p_ref, fa_ref, h_ref, ht_ref):
        xx = x_ref[...]
        r = lax.rsqrt(jnp.mean(xx * xx, axis=-1, keepdims=True) + EPS)
        h = (xx * r * g_ref[...]).astype(BF16)
        h_ref[...] = h
        ht_ref[...] = h.T
        for j in range(n // W_GROUP):
            cols = slice(j * W_GROUP, (j + 1) * W_GROUP)
            p_ref[:, cols] = _dot(h, w_ref[:, cols]).astype(BF16)
        fa_ref[...] = _dot(wf_ref[...], h, NT)

    return pl.pallas_call(
        body, name="in_proj", grid=(t // tt,),
        in_specs=[pl.BlockSpec((tt, d), lambda i: (i, 0)), pl.BlockSpec((1, d), lambda i: (0, 0)),
                  pl.BlockSpec(memory_space=pltpu.VMEM), pl.BlockSpec(memory_space=pltpu.VMEM)],
        out_specs=[pl.BlockSpec((tt, n), lambda i: (i, 0)), pl.BlockSpec((8, tt), lambda i: (0, i)),
                   pl.BlockSpec((tt, d), lambda i: (i, 0)), pl.BlockSpec((d, tt), lambda i: (0, i))],
        out_shape=[jax.ShapeDtypeStruct((t, n), BF16), jax.ShapeDtypeStruct((8, t), F32),
                   jax.ShapeDtypeStruct((t, d), BF16), jax.ShapeDtypeStruct((d, t), BF16)],
        compiler_params=_params(("arbitrary",)),
    )(x, g_mix, w1, wft)


def _tri(n, upper):
    i = lax.broadcasted_iota(jnp.int32, (n, n), 0)
    j = lax.broadcasted_iota(jnp.int32, (n, n), 1)
    return ((i <= j) if upper else (i >= j)).astype(BF16)


def _gate_fwd(fa_row, b_col, seq):
    t = fa_row.shape[1]
    cb = 256

    def body(fa_ref, b_ref, c_ref):
        tri = _tri(cb, True)
        carry = jnp.zeros((8, 1), F32)
        for k in range(seq // cb):
            z = fa_ref[:, k * cb:(k + 1) * cb] + b_ref[...]
            lf = jnp.minimum(z, 0.0) - jnp.log(1.0 + jnp.exp(-jnp.abs(z)))
            a, b, c = _split3(lf)
            blk = _dot(a, tri) + _dot(b, tri) + _dot(c, tri) + carry
            c_ref[:, k * cb:(k + 1) * cb] = blk
            carry = blk[:, cb - 1:cb]

    return pl.pallas_call(
        body, name="gate_fwd", grid=(t // seq,),
        in_specs=[pl.BlockSpec((8, seq), lambda i: (0, i)), pl.BlockSpec((8, 1), lambda i: (0, 0))],
        out_specs=pl.BlockSpec((8, seq), lambda i: (0, i)),
        out_shape=jax.ShapeDtypeStruct((8, t), F32),
        compiler_params=_params(("arbitrary",)),
    )(fa_row, b_col)


def _gate_bwd(dc_row, fa_row, b_col, seq):
    t = fa_row.shape[1]
    cb = 256

    def body(dc_ref, fa_ref, b_ref, dfa_ref, db_ref):
        @pl.when(pl.program_id(0) == 0)
        def _():
            db_ref[...] = jnp.zeros_like(db_ref)

        tri = _tri(cb, False)
        carry = jnp.zeros((8, 1), F32)
        dbs = jnp.zeros((8, 1), F32)
        for k in reversed(range(seq // cb)):
            a, b, c = _split3(dc_ref[:, k * cb:(k + 1) * cb])
            dlf = _dot(a, tri) + _dot(b, tri) + _dot(c, tri) + carry
            carry = dlf[:, 0:1]
            z = fa_ref[:, k * cb:(k + 1) * cb] + b_ref[...]
            dfa = dlf / (1.0 + jnp.exp(z))
            dfa_ref[:, k * cb:(k + 1) * cb] = dfa
            dbs = dbs + jnp.sum(dfa, axis=1, keepdims=True)
        db_ref[...] += jnp.broadcast_to(dbs, (8, LANES))

    return pl.pallas_call(
        body, name="gate_bwd", grid=(t // seq,),
        in_specs=[pl.BlockSpec((8, seq), lambda i: (0, i)), pl.BlockSpec((8, seq), lambda i: (0, i)),
                  pl.BlockSpec((8, 1), lambda i: (0, 0))],
        out_specs=[pl.BlockSpec((8, seq), lambda i: (0, i)), pl.BlockSpec((8, LANES), lambda i: (0, 0))],
        out_shape=[jax.ShapeDtypeStruct((8, t), F32), jax.ShapeDtypeStruct((8, LANES), F32)],
        compiler_params=_params(("arbitrary",)),
    )(dc_row, fa_row, b_col)


def _attn_out(o_fox, o_dil, x, g_fox, g_dil, w_out):
    t, d = x.shape
    w = o_fox.shape[1]
    tt = 512

    def body(of_ref, od_ref, x_ref, gf_ref, gd_ref, w_ref, x1_ref, ont_ref):
        acc = x_ref[...]
        for k, (o_ref, g_ref) in enumerate(((of_ref, gf_ref), (od_ref, gd_ref))):
            o = o_ref[...]
            r = lax.rsqrt(jnp.mean(o * o, axis=-1, keepdims=True) + EPS)
            on = (o * r * g_ref[...]).astype(BF16)
            ont_ref[k * w:(k + 1) * w, :] = on.T
            acc = acc + _dot(on, w_ref[k * w:(k + 1) * w, :])
        x1_ref[...] = acc

    return pl.pallas_call(
        body, name="attn_out", grid=(t // tt,),
        in_specs=[pl.BlockSpec((tt, w), lambda i: (i, 0)), pl.BlockSpec((tt, w), lambda i: (i, 0)),
                  pl.BlockSpec((tt, d), lambda i: (i, 0)), pl.BlockSpec((1, w), lambda i: (0, 0)),
                  pl.BlockSpec((1, w), lambda i: (0, 0)), pl.BlockSpec(memory_space=pltpu.VMEM)],
        out_specs=[pl.BlockSpec((tt, d), lambda i: (i, 0)), pl.BlockSpec((2 * w, tt), lambda i: (0, i))],
        out_shape=[jax.ShapeDtypeStruct((t, d), F32), jax.ShapeDtypeStruct((2 * w, t), BF16)],
        compiler_params=_params(("arbitrary",)),
    )(o_fox, o_dil, x, g_fox, g_dil, w_out)


def _attn_out_bwd(dx1, o_fox, o_dil, g_fox, g_dil, w_out, rider=None):
    t, d = dx1.shape
    w = o_fox.shape[1]
    tt = 512

    def body(dx_ref, of_ref, od_ref, gf_ref, gd_ref, w_ref, dof_ref, dod_ref, dgf_ref, dgd_ref):
        @pl.when(pl.program_id(0) == 0)
        def _():
            dgf_ref[...] = jnp.zeros_like(dgf_ref)
            dgd_ref[...] = jnp.zeros_like(dgd_ref)

        dxb = dx_ref[...].astype(BF16)
        for k, (o_ref, g_ref, do_ref, dg_ref) in enumerate(
                ((of_ref, gf_ref, dof_ref, dgf_ref), (od_ref, gd_ref, dod_ref, dgd_ref))):
            don = _dot(dxb, w_ref[k * w:(k + 1) * w, :], NT)
            o = o_ref[...]
            r = lax.rsqrt(jnp.mean(o * o, axis=-1, keepdims=True) + EPS)
            xhat = o * r
            u = don * g_ref[...]
            do_ref[...] = r * (u - xhat * jnp.mean(u * xhat, axis=-1, keepdims=True))
            dg_ref[0:1, :] += jnp.sum(don * xhat, axis=0, keepdims=True)

    return _host_call(
        body, rider, name="attn_out_bwd", grid=(t // tt,),
        in_specs=[pl.BlockSpec((tt, d), lambda i: (i, 0)), pl.BlockSpec((tt, w), lambda i: (i, 0)),
                  pl.BlockSpec((tt, w), lambda i: (i, 0)), pl.BlockSpec((1, w), lambda i: (0, 0)),
                  pl.BlockSpec((1, w), lambda i: (0, 0)), pl.BlockSpec(memory_space=pltpu.VMEM)],
        out_specs=[pl.BlockSpec((tt, w), lambda i: (i, 0)), pl.BlockSpec((tt, w), lambda i: (i, 0)),
                   pl.BlockSpec((8, w), lambda i: (0, 0)), pl.BlockSpec((8, w), lambda i: (0, 0))],
        out_shape=[jax.ShapeDtypeStruct((t, w), F32), jax.ShapeDtypeStruct((t, w), F32),
                   jax.ShapeDtypeStruct((8, w), F32), jax.ShapeDtypeStruct((8, w), F32)],
        scratch_shapes=[], inputs=(dx1, o_fox, o_dil, g_fox, g_dil, w_out), semantics=("arbitrary",))


def _ffn_fwd(x1, target, g_ffn, w_gate, w_up, w_down):
    t, d = x1.shape
    f = w_gate.shape[0]
    tt = 256

    def body(x_ref, t_ref, g_ref, wg_ref, wu_ref, wd_ref, a_ref, u_ref, dy_ref, loss_ref):
        xx = x_ref[...]
        r = lax.rsqrt(jnp.mean(xx * xx, axis=-1, keepdims=True) + EPS)
        h = (xx * r * g_ref[...]).astype(BF16)
        a = _dot(h, wg_ref[...], NT)
        u = _dot(h, wu_ref[...], NT)
        a_ref[...] = a.astype(BF16)
        u_ref[...] = u.astype(BF16)
        s = (a / (1.0 + jnp.exp(-a)) * u).astype(BF16)
        y = xx + _dot(s, wd_ref[...])
        e = y - t_ref[...]
        dy_ref[...] = e * (1.0 / d)
        loss_ref[...] = jnp.broadcast_to(0.5 * jnp.sum(e * e) * (1.0 / d), (1, 8, LANES))

    return pl.pallas_call(
        body, name="ffn_fwd", grid=(t // tt,),
        in_specs=[pl.BlockSpec((tt, d), lambda i: (i, 0)), pl.BlockSpec((tt, d), lambda i: (i, 0)),
                  pl.BlockSpec((1, d), lambda i: (0, 0)), pl.BlockSpec(memory_space=pltpu.VMEM),
                  pl.BlockSpec(memory_space=pltpu.VMEM), pl.BlockSpec(memory_space=pltpu.VMEM)],
        out_specs=[pl.BlockSpec((tt, f), lambda i: (i, 0)), pl.BlockSpec((tt, f), lambda i: (i, 0)),
                   pl.BlockSpec((tt, d), lambda i: (i, 0)), pl.BlockSpec((1, 8, LANES), lambda i: (i, 0, 0))],
        out_shape=[jax.ShapeDtypeStruct((t, f), BF16), jax.ShapeDtypeStruct((t, f), BF16),
                   jax.ShapeDtypeStruct((t, d), F32), jax.ShapeDtypeStruct((t // tt, 8, LANES), F32)],
        compiler_params=_params(("arbitrary",)),
    )(x1, target, g_ffn, w_gate, w_up, w_down)


def _ffn_bwd(dy, a, u, x1, g_ffn, w_gate, w_up, w_down):
    t, d = x1.shape
    f = w_gate.shape[0]
    tt = 256

    def body(dy_ref, a_ref, u_ref, x_ref, g_ref, wg_ref, wu_ref, wd_ref,
             dx_ref, s_ref, da_ref, du_ref, h_ref, dg_ref):
        @pl.when(pl.program_id(0) == 0)
        def _():
            dg_ref[...] = jnp.zeros_like(dg_ref)

        dy_ = dy_ref[...]
        ds = _dot(dy_.astype(BF16), wd_ref[...], NT)
        a_ = a_ref[...].astype(F32)
        u_ = u_ref[...].astype(F32)
        sig = 1.0 / (1.0 + jnp.exp(-a_))
        silu = a_ * sig
        s_ref[...] = (silu * u_).astype(BF16)
        da = (ds * u_ * (sig * (1.0 + a_ * (1.0 - sig)))).astype(BF16)
        du = (ds * silu).astype(BF16)
        da_ref[...] = da
        du_ref[...] = du
        dh = _dot(da, wg_ref[...]) + _dot(du, wu_ref[...])
        xx = x_ref[...]
        r = lax.rsqrt(jnp.mean(xx * xx, axis=-1, keepdims=True) + EPS)
        xhat = xx * r
        g = g_ref[...]
        h_ref[...] = (xhat * g).astype(BF16)
        uu = dh * g
        dx_ref[...] = dy_ + r * (uu - xhat * jnp.mean(uu * xhat, axis=-1, keepdims=True))
        dg_ref[0:1, :] += jnp.sum(dh * xhat, axis=0, keepdims=True)

    return pl.pallas_call(
        body, name="ffn_bwd", grid=(t // tt,),
        in_specs=[pl.BlockSpec((tt, d), lambda i: (i, 0)), pl.BlockSpec((tt, f), lambda i: (i, 0)),
                  pl.BlockSpec((tt, f), lambda i: (i, 0)), pl.BlockSpec((tt, d), lambda i: (i, 0)),
                  pl.BlockSpec((1, d), lambda i: (0, 0)), pl.BlockSpec(memory_space=pltpu.VMEM),
                  pl.BlockSpec(memory_space=pltpu.VMEM), pl.BlockSpec(memory_space=pltpu.VMEM)],
        out_specs=[pl.BlockSpec((tt, d), lambda i: (i, 0)), pl.BlockSpec((tt, f), lambda i: (i, 0)),
                   pl.BlockSpec((tt, f), lambda i: (i, 0)), pl.BlockSpec((tt, f), lambda i: (i, 0)),
                   pl.BlockSpec((tt, d), lambda i: (i, 0)), pl.BlockSpec((8, d), lambda i: (0, 0))],
        out_shape=[jax.ShapeDtypeStruct((t, d), F32), jax.ShapeDtypeStruct((t, f), BF16),
                   jax.ShapeDtypeStruct((t, f), BF16), jax.ShapeDtypeStruct((t, f), BF16),
                   jax.ShapeDtypeStruct((t, d), BF16), jax.ShapeDtypeStruct((8, d), F32)],
        compiler_params=_params(("arbitrary",)),
    )(dy, a, u, x1, g_ffn, w_gate, w_up, w_down)


def _in_proj_bwd(dparts, dfa_row, w1, wft, x, g_mix, dx1, rider=None):
    t, d = x.shape
    tt = 512
    npart = len(dparts)

    def body(*refs):
        dp_refs = refs[:npart]
        dfa_ref, w_ref, wf_ref, x_ref, g_ref, dx1_ref, dx_ref, dg_ref = refs[npart:]

        @pl.when(pl.program_id(0) == 0)
        def _():
            dg_ref[...] = jnp.zeros_like(dg_ref)

        dh = _dot(dfa_ref[...].astype(BF16), wf_ref[...], TN)
        for j in range(npart):
            dh = dh + _dot(dp_refs[j][...], w_ref[:, j * W_GROUP:(j + 1) * W_GROUP], NT)
        xx = x_ref[...]
        r = lax.rsqrt(jnp.mean(xx * xx, axis=-1, keepdims=True) + EPS)
        xhat = xx * r
        uu = dh * g_ref[...]
        dx_ref[...] = dx1_ref[...] + r * (uu - xhat * jnp.mean(uu * xhat, axis=-1, keepdims=True))
        dg_ref[0:1, :] += jnp.sum(dh * xhat, axis=0, keepdims=True)

    return _host_call(
        body, rider, name="in_proj_bwd", grid=(t // tt,),
        in_specs=[pl.BlockSpec((tt, W_GROUP), lambda i: (i, 0)) for _ in range(npart)]
        + [pl.BlockSpec((8, tt), lambda i: (0, i)), pl.BlockSpec(memory_space=pltpu.VMEM),
           pl.BlockSpec(memory_space=pltpu.VMEM), pl.BlockSpec((tt, d), lambda i: (i, 0)),
           pl.BlockSpec((1, d), lambda i: (0, 0)), pl.BlockSpec((tt, d), lambda i: (i, 0))],
        out_specs=[pl.BlockSpec((tt, d), lambda i: (i, 0)), pl.BlockSpec((8, d), lambda i: (0, 0))],
        out_shape=[jax.ShapeDtypeStruct((t, d), F32), jax.ShapeDtypeStruct((8, d), F32)],
        scratch_shapes=[], inputs=(*dparts, dfa_row, w1, wft, x, g_mix, dx1), semantics=("arbitrary",))


def _token_matmul(a, b, name, tn, a_is_transposed=True):
    m, t = a.shape if a_is_transposed else a.shape[::-1]
    n = b.shape[1]
    tk = 1024

    def body(a_ref, b_ref, o_ref):
        @pl.when(pl.program_id(1) == 0)
        def _():
            o_ref[...] = jnp.zeros_like(o_ref)

        o_ref[...] += _dot(a_ref[...], b_ref[...].astype(BF16), None if a_is_transposed else TN)

    a_spec = pl.BlockSpec((m, tk), lambda j, k: (0, k)) if a_is_transposed else pl.BlockSpec((tk, m), lambda j, k: (k, 0))
    return pl.pallas_call(
        body, name=name, grid=(n // tn, t // tk),
        in_specs=[a_spec, pl.BlockSpec((tk, tn), lambda j, k: (k, j))],
        out_specs=pl.BlockSpec((m, tn), lambda j, k: (0, j)),
        out_shape=jax.ShapeDtypeStruct((m, n), F32),
        compiler_params=_params(("arbitrary", "arbitrary")),
    )(a, b)


def _token_matmul_parts(at, parts, name):
    m, t = at.shape
    widths = [p.shape[1] for p in parts]
    tk = 1024

    def body(a_ref, *refs):
        o_ref = refs[-1]

        @pl.when(pl.program_id(0) == 0)
        def _():
            o_ref[...] = jnp.zeros_like(o_ref)

        a, first = a_ref[...], 0
        for b_ref, w in zip(refs[:-1], widths):
            o_ref[:, first:first + w] += _dot(a, b_ref[...])
            first += w

    return pl.pallas_call(
        body, name=name, grid=(t // tk,),
        in_specs=[pl.BlockSpec((m, tk), lambda k: (0, k))] + [pl.BlockSpec((tk, w), lambda k: (k, 0)) for w in widths],
        out_specs=pl.BlockSpec((m, sum(widths)), lambda k: (0, 0)),
        out_shape=jax.ShapeDtypeStruct((m, sum(widths)), F32),
        compiler_params=_params(("arbitrary",)),
    )(at, *parts)


def _row_matmul(a_row, b, name):
    t, n = b.shape
    tk = 1024
    nk = t // tk

    def body(a_ref, b_ref, o_ref):
        @pl.when(pl.program_id(0) == 0)
        def _():
            o_ref[...] = jnp.zeros_like(o_ref)

        o_ref[...] += _dot(a_ref[...].astype(BF16), b_ref[...])

    return pl.pallas_call(
        body, name=name, grid=(nk,),
        in_specs=[pl.BlockSpec((8, tk), lambda k: (0, k)), pl.BlockSpec((tk, n), lambda k: (k, 0))],
        out_specs=pl.BlockSpec((8, n), lambda k: (0, 0)),
        out_shape=jax.ShapeDtypeStruct((8, n), F32),
        compiler_params=_params(("arbitrary",)),
    )(a_row, b)


FOX_TQ = 256
SUM_LANE = (HEAD_DIM, 0)


def _fox_fwd(proj, c3, gq, gk, nb, seq, rider=None):
    t = nb * seq
    tq = FOX_TQ
    nq = seq // tq
    npair = N_FOX_HEADS // 2

    def body(q_ref, k_ref, v_ref, c_ref, gq_ref, gk_ref, o_ref, lse_ref, qs, ks, vs):
        ones = _group_ones()
        masks = _head_masks()
        qhat, _ = _head_norm(q_ref[...].astype(F32), None, ones)
        khat, _ = _head_norm(k_ref[...].astype(F32), None, ones)
        qs[...] = (qhat * gq_ref[...] * (SCALE * LOG2E)).astype(BF16)
        kn = khat * gk_ref[...]
        vv = v_ref[...].astype(F32)
        lane = lax.broadcasted_iota(jnp.int32, (1, LANES), 1)
        for hd in range(2):
            ks[hd] = (kn * masks[hd]).astype(BF16)
            vs[hd] = (vv * masks[hd] + (lane == SUM_LANE[hd]).astype(F32)).astype(BF16)
        row = lax.broadcasted_iota(jnp.int32, (tq, tq), 0)
        col = lax.broadcasted_iota(jnp.int32, (tq, tq), 1)
        causal = col <= row

        for qi in range(nq):
            q0 = qi * tq
            q_blk = qs[q0:q0 + tq, :]
            o_tot = jnp.zeros((tq, LANES), F32)
            lse_tot = jnp.zeros((tq, LANES), F32)
            for hd in range(2):
                crow = c_ref[0, hd:hd + 1, 0:q0 + tq] * LOG2E
                c0 = crow[:, q0:q0 + 1]
                s_d = _dot(q_blk, ks[hd, q0:q0 + tq, :], NT) + (c0 - crow[:, q0:q0 + tq])
                s_d = jnp.where(causal, s_d, NEG)
                m = jnp.max(s_d, axis=-1, keepdims=True)
                if qi > 0:
                    s_o = _dot(q_blk, ks[hd, 0:q0, :], NT) + (c0 - crow[:, 0:q0])
                    m = jnp.maximum(m, jnp.max(s_o, axis=-1, keepdims=True))
                acc = _dot(jnp.exp2(s_d - m).astype(BF16), vs[hd, q0:q0 + tq, :])
                if qi > 0:
                    acc = acc + _dot(jnp.exp2(s_o - m).astype(BF16), vs[hd, 0:q0, :])
                l = acc[:, SUM_LANE[hd]:SUM_LANE[hd] + 1]
                o_tot = o_tot + (acc / l) * masks[hd]
                lse_tot = lse_tot + (m + jnp.log2(l) - c0) * masks[hd]
            o_ref[q0:q0 + tq, :] = o_tot
            lse_ref[q0:q0 + tq, :] = lse_tot

    blk = lambda off: pl.BlockSpec((seq, LANES), lambda b, p: (b, off + p))
    return _host_call(
        body, rider, name="fox_fwd", grid=(nb, npair),
        in_specs=[blk(0), blk(npair), blk(2 * npair), pl.BlockSpec((1, 2, seq), lambda b, p: (p, 0, b)),
                  pl.BlockSpec((1, LANES), lambda b, p: (0, 0)), pl.BlockSpec((1, LANES), lambda b, p: (0, 0))],
        out_specs=[blk(0), blk(0)],
        out_shape=[jax.ShapeDtypeStruct((t, W_GROUP), F32), jax.ShapeDtypeStruct((t, W_GROUP), F32)],
        scratch_shapes=[pltpu.VMEM((seq, LANES), BF16), pltpu.VMEM((2, seq, LANES), BF16),
                        pltpu.VMEM((2, seq, LANES), BF16)],
        inputs=(proj, proj, proj, c3, gq, gk), semantics=("arbitrary", "arbitrary"))


def _fox_bwd(proj, c3, gq, gk, do, o, lse, nb, seq, rider=None):
    t = nb * seq
    tq = FOX_TQ
    nq = seq // tq
    npair = N_FOX_HEADS // 2

    def body(q_ref, k_ref, v_ref, c_ref, gq_ref, gk_ref, do_ref, o_ref, lse_ref,
             dq_ref, dk_ref, dv_ref, dc_ref, dg_ref, qs, ks, vs, kts, dos, lse_t, delta_t, dqt_acc, dk_acc, dv_acc,
             row_sum):
        @pl.when((pl.program_id(0) == 0) & (pl.program_id(1) == 0))
        def _():
            dg_ref[...] = jnp.zeros_like(dg_ref)

        ones = _group_ones()
        masks = _head_masks()
        qhat, rq = _head_norm(q_ref[...].astype(F32), None, ones)
        khat, rk = _head_norm(k_ref[...].astype(F32), None, ones)
        qs[...] = (qhat * gq_ref[...] * (SCALE * LOG2E)).astype(BF16)
        kn = khat * gk_ref[...]
        vv = v_ref[...].astype(F32)
        for hd in range(2):
            ks[hd] = (kn * masks[hd]).astype(BF16)
            vs[hd] = (vv * masks[hd]).astype(BF16)
            kts[hd] = ks[hd].T
        dof = do_ref[...]
        dos[...] = dof.astype(BF16)
        lse_t[...] = lse_ref[...].T
        delta_t[...] = _groupsum(dof * o_ref[...], ones).T
        dqt_acc[...] = jnp.zeros_like(dqt_acc)
        dk_acc[...] = jnp.zeros_like(dk_acc)
        dv_acc[...] = jnp.zeros_like(dv_acc)
        row_sum[...] = jnp.zeros_like(row_sum)
        key = lax.broadcasted_iota(jnp.int32, (tq, tq), 0)
        qry = lax.broadcasted_iota(jnp.int32, (tq, tq), 1)
        causal = key <= qry

        for hd in range(2):
            lane0 = hd * HEAD_DIM
            for kj in range(nq):
                k0 = kj * tq
                k_blk = ks[hd, k0:k0 + tq, :]
                v_blk = vs[hd, k0:k0 + tq, :]
                kt_blk = kts[hd, :, k0:k0 + tq]
                crow = c_ref[0, hd:hd + 1, k0:k0 + tq] * LOG2E
                ck0 = crow[:, 0:1]
                bias = jnp.broadcast_to(ck0 - crow, (LANES, tq)).T[:, 0:1]

                def queries_step(r0, r1, diag, hd=hd, lane0=lane0, k_blk=k_blk, v_blk=v_blk, kt_blk=kt_blk,
                                 bias=bias, ck0=ck0):
                    q_r = qs[r0:r1, :]
                    do_r = dos[r0:r1, :]
                    z = _dot(k_blk, q_r, NT) + bias
                    p = jnp.exp2(z - (lse_t[lane0:lane0 + 1, r0:r1] + ck0))
                    if diag:
                        p = jnp.where(causal, p, 0.0)
                    dp = _dot(v_blk, do_r, NT)
                    ds = p * (dp - delta_t[lane0:lane0 + 1, r0:r1])
                    dsb = ds.astype(BF16)
                    dqt_acc[:, r0:r1] += _dot(kt_blk, dsb)
                    row_sum[hd:hd + 1, r0:r1] += jnp.sum(ds, axis=0, keepdims=True)
                    return _dot(dsb, q_r), _dot(p.astype(BF16), do_r), -jnp.sum(ds, axis=1, keepdims=True)

                dk_j, dv_j, dc_j = queries_step(k0, k0 + tq, True)
                if k0 + tq < seq:
                    dk_o, dv_o, dc_o = queries_step(k0 + tq, seq, False)
                    dk_j, dv_j, dc_j = dk_j + dk_o, dv_j + dv_o, dc_j + dc_o
                dk_acc[k0:k0 + tq, :] += dk_j * masks[hd]
                dv_acc[k0:k0 + tq, :] += dv_j * masks[hd]
                dc_ref[0, hd:hd + 1, k0:k0 + tq] = jnp.broadcast_to(dc_j, (tq, LANES)).T[0:1, :]

        dc_ref[0] += row_sum[0:2, :]

        dq_raw, dgq = _head_norm_bwd(dqt_acc[...].T * SCALE, qhat, rq, gq_ref[...], ones)
        dk_raw, dgk = _head_norm_bwd(dk_acc[...] * LN2, khat, rk, gk_ref[...], ones)
        dq_ref[...] = dq_raw.astype(BF16)
        dk_ref[...] = dk_raw.astype(BF16)
        dv_ref[...] = dv_acc[...].astype(BF16)
        dg_ref[0:1, :] += dgq
        dg_ref[1:2, :] += dgk

    blk = lambda off: pl.BlockSpec((seq, LANES), lambda b, p: (b, off + p))
    vec = pl.BlockSpec((1, LANES), lambda b, p: (0, 0))
    c_spec = pl.BlockSpec((1, 2, seq), lambda b, p: (p, 0, b))
    return _host_call(
        body, rider, name="fox_bwd", grid=(nb, npair),
        in_specs=[blk(0), blk(npair), blk(2 * npair), c_spec, vec, vec, blk(0), blk(0), blk(0)],
        out_specs=[blk(0), blk(0), blk(0), c_spec, pl.BlockSpec((8, LANES), lambda b, p: (0, 0))],
        out_shape=[jax.ShapeDtypeStruct((t, W_GROUP), BF16), jax.ShapeDtypeStruct((t, W_GROUP), BF16),
                   jax.ShapeDtypeStruct((t, W_GROUP), BF16), jax.ShapeDtypeStruct((npair, 2, t), F32),
                   jax.ShapeDtypeStruct((8, LANES), F32)],
        scratch_shapes=[pltpu.VMEM((seq, LANES), BF16), pltpu.VMEM((2, seq, LANES), BF16),
                        pltpu.VMEM((2, seq, LANES), BF16), pltpu.VMEM((2, LANES, seq), BF16),
                        pltpu.VMEM((seq, LANES), BF16), pltpu.VMEM((LANES, seq), F32),
                        pltpu.VMEM((LANES, seq), F32), pltpu.VMEM((LANES, seq), F32),
                        pltpu.VMEM((seq, LANES), F32), pltpu.VMEM((seq, LANES), F32),
                        pltpu.VMEM((8, seq), F32)],
        inputs=(proj, proj, proj, c3, gq, gk, do, o, lse), semantics=("arbitrary", "arbitrary"))


def _dil_prep(q_ref, k_ref, gq_ref, gk_ref, cos_ref, up_ref, dn_ref, ones):
    qhat, rq = _head_norm(q_ref[...].astype(F32), None, ones)
    khat, rk = _head_norm(k_ref[...].astype(F32), None, ones)
    cos, up, dn = cos_ref[...], up_ref[...], dn_ref[...]
    qn = _rope(qhat * gq_ref[...], cos, up, dn) * (SCALE * LOG2E)
    kn = _rope(khat * gk_ref[...], cos, up, dn)
    return qhat, rq, khat, rk, qn, kn


def _dil_keys(d, seq, kp, vp, kw, vw):
    nblk = seq // BAND
    per_res = seq // (d * BAND)
    as_blocks = lambda ref, rows: ref[rows, :].reshape(-1, BAND, LANES)
    if per_res == 1:
        a = lax.broadcasted_iota(jnp.int32, (1, BAND, BAND), 1)
        j = lax.broadcasted_iota(jnp.int32, (1, BAND, BAND), 2)
        causal = jnp.where(j <= a, 0.0, NEG)
        return as_blocks(kp, slice(0, seq)), as_blocks(vp, slice(0, seq)), [causal]
    for src, dst in ((kp, kw), (vp, vw)):
        dst[:, BAND:, :] = as_blocks(src, slice(0, seq))
        dst[1:, :BAND, :] = as_blocks(src, slice(0, seq - BAND))
        dst[0:1, :BAND, :] = jnp.zeros((1, BAND, LANES), BF16)
    a = lax.broadcasted_iota(jnp.int32, (1, BAND, 2 * BAND), 1)
    j = lax.broadcasted_iota(jnp.int32, (1, BAND, 2 * BAND), 2)
    band = jnp.where(((j < BAND) & (j >= a)) | ((j >= BAND) & (j - BAND <= a)), 0.0, NEG)
    e = lax.broadcasted_iota(jnp.int32, (nblk, 1, 2 * BAND), 0)
    j = lax.broadcasted_iota(jnp.int32, (nblk, 1, 2 * BAND), 2)
    no_prev = jnp.where(((e & (per_res - 1)) == 0) & (j < BAND), NEG, 0.0)
    return kw[...], vw[...], [band + no_prev]


def _residues(d, seq):
    n = seq // d
    if d == 1:
        return [(slice(0, seq), slice(0, seq))]
    return [(pl.ds(r, n, stride=d), slice(r * n, (r + 1) * n)) for r in range(d)]


def _dil_fwd(proj, gq, gk, cos, up, dn, nb, seq):
    t = nb * seq
    npair = W_GROUP // LANES
    off = 3 * npair

    def body(q_ref, k_ref, v_ref, gq_ref, gk_ref, cos_ref, up_ref, dn_ref, o_ref, lse_ref,
             qs, ks, vs, qp, kp, vp, kw, vw, m_b, l_b, o_b, m_s, l_s, o_s):
        ones = _group_ones()
        masks = _head_masks()
        _, _, _, _, qn, kn = _dil_prep(q_ref, k_ref, gq_ref, gk_ref, cos_ref, up_ref, dn_ref, ones)
        qs[...] = qn
        ks[...] = kn
        vs[...] = v_ref[...].astype(F32)
        nblk = seq // BAND

        for d in DILATIONS:
            for tok, res in _residues(d, seq):
                qv = qs[tok, :]
                for hd in range(2):
                    qp[hd, res, :] = (qv * masks[hd]).astype(BF16)
                kp[res, :] = ks[tok, :].astype(BF16)
                vp[res, :] = vs[tok, :].astype(BF16)
            keys_k, keys_v, bias = _dil_keys(d, seq, kp, vp, kw, vw)
            m_t = jnp.zeros((nblk, BAND, LANES), F32)
            l_t = jnp.zeros((nblk, BAND, LANES), F32)
            o_t = jnp.zeros((nblk, BAND, LANES), F32)
            for hd in range(2):
                s = _dot(qp[hd].reshape(nblk, BAND, LANES), keys_k, BATCH_NT)
                for b_ in bias:
                    s = s + b_
                m = jnp.max(s, axis=-1, keepdims=True)
                p = jnp.exp2(s - m)
                m_t = m_t + m * masks[hd]
                l_t = l_t + jnp.sum(p, axis=-1, keepdims=True) * masks[hd]
                o_t = o_t + _dot(p.astype(BF16), keys_v, BATCH_NN) * masks[hd]
            m_b[...] = m_t.reshape(seq, LANES)
            l_b[...] = l_t.reshape(seq, LANES)
            o_b[...] = o_t.reshape(seq, LANES)
            for tok, res in _residues(d, seq):
                if d == DILATIONS[0]:
                    m_s[tok, :] = m_b[res, :]
                    l_s[tok, :] = l_b[res, :]
                    o_s[tok, :] = o_b[res, :]
                else:
                    m_old = m_s[tok, :]
                    m_new = jnp.maximum(m_old, m_b[res, :])
                    w_old = jnp.exp2(m_old - m_new)
                    w_new = jnp.exp2(m_b[res, :] - m_new)
                    l_s[tok, :] = l_s[tok, :] * w_old + l_b[res, :] * w_new
                    o_s[tok, :] = o_s[tok, :] * w_old + o_b[res, :] * w_new
                    m_s[tok, :] = m_new

        l = l_s[...]
        o_ref[...] = o_s[...] / l
        lse_ref[...] = m_s[...] + jnp.log2(l)

    blk = lambda o_: pl.BlockSpec((seq, LANES), lambda b, p: (b, o_ + p))
    vec = pl.BlockSpec((1, LANES), lambda b, p: (0, 0))
    tab = pl.BlockSpec((seq, LANES), lambda b, p: (0, 0))
    f32_buf = pltpu.VMEM((seq, LANES), F32)
    bf16_buf = pltpu.VMEM((seq, LANES), BF16)
    window_buf = pltpu.VMEM((seq // BAND, 2 * BAND, LANES), BF16)
    return pl.pallas_call(
        body, name="dil_fwd", grid=(nb, npair),
        in_specs=[blk(off), blk(off + npair), blk(off + 2 * npair), vec, vec, tab, tab, tab],
        out_specs=[blk(0), blk(0)],
        out_shape=[jax.ShapeDtypeStruct((t, W_GROUP), F32), jax.ShapeDtypeStruct((t, W_GROUP), F32)],
        scratch_shapes=[f32_buf, f32_buf, f32_buf, pltpu.VMEM((2, seq, LANES), BF16), bf16_buf, bf16_buf,
                        window_buf, window_buf, f32_buf, f32_buf, f32_buf, f32_buf, f32_buf, f32_buf],
        compiler_params=_params(("arbitrary", "arbitrary")),
    )(proj, proj, proj, gq, gk, cos, up, dn)


def _dil_bwd(proj, gq, gk, cos, up, dn, do, o, lse, nb, seq, rider=None):
    t = nb * seq
    npair = W_GROUP // LANES
    off = 3 * npair

    def body(q_ref, k_ref, v_ref, gq_ref, gk_ref, cos_ref, up_ref, dn_ref, do_ref, o_ref, lse_ref,
             dq_ref, dk_ref, dv_ref, dg_ref, qs, ks, vs, delta, dq_s, dk_s, dv_s,
             qp, kp, vp, dop, kw, vw, lse_p, delta_p, dq_p, dk_p, dv_p):
        @pl.when((pl.program_id(0) == 0) & (pl.program_id(1) == 0))
        def _():
            dg_ref[...] = jnp.zeros_like(dg_ref)

        ones = _group_ones()
        masks = _head_masks()
        qhat, rq, khat, rk, qn, kn = _dil_prep(q_ref, k_ref, gq_ref, gk_ref, cos_ref, up_ref, dn_ref, ones)
        qs[...] = qn
        ks[...] = kn
        vs[...] = v_ref[...].astype(F32)
        delta[...] = _groupsum(do_ref[...] * o_ref[...], ones)
        nblk = seq // BAND

        for d in DILATIONS:
            for tok, res in _residues(d, seq):
                qv = qs[tok, :]
                dov = do_ref[tok, :]
                for hd in range(2):
                    qp[hd, res, :] = (qv * masks[hd]).astype(BF16)
                    dop[hd, res, :] = (dov * masks[hd]).astype(BF16)
                kp[res, :] = ks[tok, :].astype(BF16)
                vp[res, :] = vs[tok, :].astype(BF16)
                lse_p[res, :] = lse_ref[tok, :]
                delta_p[res, :] = delta[tok, :]
            keys_k, keys_v, bias = _dil_keys(d, seq, kp, vp, kw, vw)
            nk = keys_k.shape[1]
            dq_b = jnp.zeros((nblk, BAND, LANES), F32)
            dk_b = jnp.zeros((nblk, nk, LANES), F32)
            dv_b = jnp.zeros((nblk, nk, LANES), F32)
            for hd in range(2):
                lane0 = hd * HEAD_DIM
                q3 = qp[hd].reshape(nblk, BAND, LANES)
                do3 = dop[hd].reshape(nblk, BAND, LANES)
                z = _dot(q3, keys_k, BATCH_NT)
                for b_ in bias:
                    z = z + b_
                p = jnp.exp2(z - lse_p[...].reshape(nblk, BAND, LANES)[:, :, lane0:lane0 + 1])
                dp = _dot(do3, keys_v, BATCH_NT)
                ds = (p * (dp - delta_p[...].reshape(nblk, BAND, LANES)[:, :, lane0:lane0 + 1])).astype(BF16)
                dq_b = dq_b + _dot(ds, keys_k, BATCH_NN) * masks[hd]
                dk_b = dk_b + _dot(ds, q3, BATCH_TN)
                dv_b = dv_b + _dot(p.astype(BF16), do3, BATCH_TN)
            dq_p[...] = dq_b.reshape(seq, LANES)
            for acc, out in ((dk_b, dk_p), (dv_b, dv_p)):
                out[...] = acc[:, nk - BAND:, :].reshape(seq, LANES)
                if nk > BAND:
                    out[0:seq - BAND, :] += acc[1:, :BAND, :].reshape(seq - BAND, LANES)
            for tok, res in _residues(d, seq):
                if d == DILATIONS[0]:
                    dq_s[tok, :] = dq_p[res, :]
                    dk_s[tok, :] = dk_p[res, :]
                    dv_s[tok, :] = dv_p[res, :]
                else:
                    dq_s[tok, :] += dq_p[res, :]
                    dk_s[tok, :] += dk_p[res, :]
                    dv_s[tok, :] += dv_p[res, :]

        cos, up, dn = cos_ref[...], up_ref[...], dn_ref[...]
        dq_raw, dgq = _head_norm_bwd(_rope_bwd(dq_s[...] * SCALE, cos, up, dn), qhat, rq, gq_ref[...], ones)
        dk_raw, dgk = _head_norm_bwd(_rope_bwd(dk_s[...] * LN2, cos, up, dn), khat, rk, gk_ref[...], ones)
        dq_ref[...] = dq_raw.astype(BF16)
        dk_ref[...] = dk_raw.astype(BF16)
        dv_ref[...] = dv_s[...].astype(BF16)
        dg_ref[0:1, :] += dgq
        dg_ref[1:2, :] += dgk

    blk = lambda o_: pl.BlockSpec((seq, LANES), lambda b, p: (b, o_ + p))
    vec = pl.BlockSpec((1, LANES), lambda b, p: (0, 0))
    tab = pl.BlockSpec((seq, LANES), lambda b, p: (0, 0))
    f32_buf = pltpu.VMEM((seq, LANES), F32)
    bf16_buf = pltpu.VMEM((seq, LANES), BF16)
    window_buf = pltpu.VMEM((seq // BAND, 2 * BAND, LANES), BF16)
    bf16_pair = pltpu.VMEM((2, seq, LANES), BF16)
    return _host_call(
        body, rider, name="dil_bwd", grid=(nb, npair),
        in_specs=[blk(off), blk(off + npair), blk(off + 2 * npair), vec, vec, tab, tab, tab,
                  blk(0), blk(0), blk(0)],
        out_specs=[blk(0), blk(0), blk(0), pl.BlockSpec((8, LANES), lambda b, p: (0, 0))],
        out_shape=[jax.ShapeDtypeStruct((t, W_GROUP), BF16), jax.ShapeDtypeStruct((t, W_GROUP), BF16),
                   jax.ShapeDtypeStruct((t, W_GROUP), BF16), jax.ShapeDtypeStruct((8, LANES), F32)],
        scratch_shapes=[f32_buf] * 7 + [bf16_pair, bf16_buf, bf16_buf, bf16_pair, window_buf, window_buf]
        + [f32_buf] * 5,
        inputs=(proj, proj, proj, gq, gk, cos, up, dn, do, o, lse), semantics=("arbitrary", "arbitrary"))


def _adamw(w, g, m, v, name, rider=None):
    row_major = w.ndim == 3 and w.shape[1] == 1
    rows, cols = (w.shape[0], w.shape[2]) if row_major else w.shape[-2:]
    if row_major:
        tr = max(t for t in range(1, 65) if rows % t == 0)
    else:
        tr = _row_tile(rows) if rows >= 8 else rows
    c1 = 1.0 - ADAM_B1 ** ADAM_STEP
    c2 = 1.0 - ADAM_B2 ** ADAM_STEP

    def body(w_ref, g_ref, m_ref, v_ref, d_ref, nm_ref, nv_ref):
        g_ = g_ref[...]
        nm = ADAM_B1 * m_ref[...] + (1.0 - ADAM_B1) * g_
        nv = ADAM_B2 * v_ref[...] + (1.0 - ADAM_B2) * (g_ * g_)
        nm_ref[...] = nm
        nv_ref[...] = nv
        d_ref[...] = -ADAM_LR * ((nm / c1) / (jnp.sqrt(nv / c2) + ADAM_EPS) + ADAM_WD * w_ref[...])

    if row_major:
        spec = pl.BlockSpec((tr, 1, cols), lambda i: (i, 0, 0))
    elif w.ndim == 3:
        spec = pl.BlockSpec((1, tr, cols), lambda i: (0, i, 0))
    else:
        spec = pl.BlockSpec((tr, cols), lambda i: (i, 0))
    shape = jax.ShapeDtypeStruct(w.shape, F32)
    return _host_call(
        body, rider, name=name, grid=(rows // tr,), in_specs=[spec] * 4, out_specs=[spec] * 3,
        out_shape=[shape] * 3, scratch_shapes=[], inputs=(w, g, m, v), semantics=("arbitrary",))


def _place():
    x, y, c = lax.axis_index("x"), lax.axis_index("y"), lax.axis_index("c")
    chips = [(1 - x, y), (x, 1 - y), (1 - x, 1 - y)]
    return x, y, c, chips


def _gather_weight(w, name):
    _, rows, cols = w.shape
    half_rows = rows // 2

    def body(w_ref, out_ref, send_sems, recv_sems):
        x, y, c, chips = _place()
        sibling = (x, y, 1 - c)
        mine = 2 * x + y
        lo = pl.multiple_of(c * half_rows, 16)
        lo_sib = pl.multiple_of((1 - c) * half_rows, 16)
        out_ref[mine] = w_ref[0].astype(BF16)

        def copy(k, shard, first_row, to):
            ref = out_ref.at[shard, pl.ds(first_row, half_rows), :]
            return pltpu.make_async_remote_copy(src_ref=ref, dst_ref=ref, send_sem=send_sems.at[k],
                                                recv_sem=recv_sems.at[k], device_id=to, device_id_type=MESH)

        sends = [copy(k, mine, lo, (cx, cy, c)) for k, (cx, cy) in enumerate(chips)]
        for cp in sends:
            cp.start()
        passed = []
        for k, (cx, cy) in enumerate(chips):
            theirs = 2 * cx + cy
            copy(k, theirs, lo, (cx, cy, c)).wait_recv()
            fw = copy(3 + k, theirs, lo, sibling)
            fw.start()
            passed.append(fw)
        for k, (cx, cy) in enumerate(chips):
            copy(3 + k, 2 * cx + cy, lo_sib, sibling).wait_recv()
        for cp in sends + passed:
            cp.wait_send()

    return pl.pallas_call(
        body, name=name,
        in_specs=[pl.BlockSpec(memory_space=pltpu.VMEM)],
        out_specs=pl.BlockSpec(memory_space=pltpu.VMEM),
        out_shape=jax.ShapeDtypeStruct((4, rows, cols), BF16),
        scratch_shapes=[pltpu.SemaphoreType.DMA((6,)), pltpu.SemaphoreType.DMA((6,))],
        compiler_params=pltpu.CompilerParams(vmem_limit_bytes=VMEM_LIMIT),
    )(w)


def _remote(src, dst, sems, k, to):
    send_sems, recv_sems = sems
    return pltpu.make_async_remote_copy(src_ref=src, dst_ref=dst, send_sem=send_sems.at[k], recv_sem=recv_sems.at[k],
                                        device_id=to, device_id_type=MESH)


def _cast_bf16(parts, name):
    def body(*refs):
        for src, dst in zip(refs[:len(parts)], refs[len(parts):]):
            dst[...] = src[0].astype(BF16)

    return pl.pallas_call(
        body, name=name, in_specs=[pl.BlockSpec(memory_space=pltpu.VMEM)] * len(parts),
        out_specs=[pl.BlockSpec(memory_space=pltpu.VMEM)] * len(parts),
        out_shape=[jax.ShapeDtypeStruct(p.shape[1:], BF16) for p in parts],
        compiler_params=pltpu.CompilerParams(vmem_limit_bytes=VMEM_LIMIT),
    )(*parts)


def _gather_rider(shards):
    def copies(ins, outs, sems, which):
        x, y, c, chips = _place()
        sibling = (x, y, 1 - c)
        mine = 2 * x + y
        made = {name: [] for name in which}
        for i, (p_ref, g_ref) in enumerate(zip(ins, outs)):
            half = p_ref.shape[0] // 2
            lo = pl.multiple_of(c * half, 16)
            lo_sib = pl.multiple_of((1 - c) * half, 16)
            spot = lambda shard, first, g_ref=g_ref, half=half: g_ref.at[shard, pl.ds(first, half), :]
            groups = {
                "own": lambda: [pltpu.make_async_copy(p_ref, g_ref.at[mine], sems[0].at[7 * i + 6])],
                "sends": lambda: [_remote(p_ref.at[pl.ds(lo, half), :], spot(mine, lo), sems, 7 * i + k, (cx, cy, c))
                                  for k, (cx, cy) in enumerate(chips)],
                "arrivals": lambda: [_remote(spot(2 * cx + cy, lo), spot(2 * cx + cy, lo), sems, 7 * i + k, (cx, cy, c))
                                     for k, (cx, cy) in enumerate(chips)],
                "passes": lambda: [_remote(spot(2 * cx + cy, lo), spot(2 * cx + cy, lo), sems, 7 * i + 3 + k, sibling)
                                   for k, (cx, cy) in enumerate(chips)],
                "from_sibling": lambda: [_remote(spot(2 * cx + cy, lo_sib), spot(2 * cx + cy, lo_sib), sems,
                                                 7 * i + 3 + k, sibling) for k, (cx, cy) in enumerate(chips)],
            }
            for name in which:
                made[name] += groups[name]()
        return [made[name] for name in which]

    def start(ins, outs, send_sems, recv_sems):
        own, sends = copies(ins, outs, (send_sems, recv_sems), ("own", "sends"))
        for cp in own + sends:
            cp.start()

    def middle(ins, outs, send_sems, recv_sems):
        arrivals, passes = copies(ins, outs, (send_sems, recv_sems), ("arrivals", "passes"))
        for landed, onward in zip(arrivals, passes):
            landed.wait_recv()
            onward.start()

    def finish(ins, outs, send_sems, recv_sems):
        own, sends, passes, from_sibling = copies(ins, outs, (send_sems, recv_sems),
                                                  ("own", "sends", "passes", "from_sibling"))
        for cp in from_sibling:
            cp.wait_recv()
        for cp in sends + passes:
            cp.wait_send()
        for cp in own:
            cp.wait()

    shapes = [jax.ShapeDtypeStruct((4,) + s.shape, BF16) for s in shards]
    return _Rider(shards, shapes, 7 * len(shards), start, finish, middle=middle)


def _exchange_rider(inputs, out_shapes, n_sems, copies, aliases=None):
    def start(ins, outs, send_sems, recv_sems):
        for cp in copies(ins, outs, (send_sems, recv_sems)):
            cp.start()

    def finish(ins, outs, send_sems, recv_sems):
        for cp in copies(ins, outs, (send_sems, recv_sems)):
            cp.wait()

    return _Rider(inputs, out_shapes, n_sems, start, finish, aliases)


def _swap_rider(grads4):
    halves = [g.shape[1] // 2 for g in grads4]

    def copies(ins, outs, sems):
        x, y, c, _ = _place()
        return [_remote(g.at[:, pl.ds(pl.multiple_of((1 - c) * h, 8), h), :], a, sems, i, (x, y, 1 - c))
                for i, (g, a, h) in enumerate(zip(ins, outs, halves))]

    shapes = [jax.ShapeDtypeStruct((4, h, g.shape[2]), F32) for g, h in zip(grads4, halves)]
    return _exchange_rider(grads4, shapes, len(grads4), copies)


def _chip_sum(g4, from_sibling, name):
    _, rows, cols = g4.shape
    half = rows // 2

    def body(g_ref, s_ref, stage_ref, own_ref):
        x, y, c, chips = _place()
        lo = pl.multiple_of(c * half, 8)
        for k, (cx, cy) in enumerate(chips):
            theirs = 2 * cx + cy
            stage_ref[k] = (g_ref[theirs, pl.ds(lo, half), :] + s_ref[theirs]).astype(BF16)
        mine = 2 * x + y
        own_ref[...] = g_ref[mine, pl.ds(lo, half), :] + s_ref[mine]

    return pl.pallas_call(
        body, name=name, in_specs=[pl.BlockSpec(memory_space=pltpu.VMEM)] * 2,
        out_specs=[pl.BlockSpec(memory_space=pltpu.VMEM)] * 2,
        out_shape=[jax.ShapeDtypeStruct((3, half, cols), BF16), jax.ShapeDtypeStruct((half, cols), F32)],
        compiler_params=pltpu.CompilerParams(vmem_limit_bytes=VMEM_LIMIT),
    )(g4, from_sibling)


def _spread_rider(stages):
    def copies(ins, outs, sems):
        _, _, c, chips = _place()
        return [_remote(st.at[k], ld.at[k], sems, 3 * i + k, (cx, cy, c))
                for i, (st, ld) in enumerate(zip(ins, outs)) for k, (cx, cy) in enumerate(chips)]

    shapes = [jax.ShapeDtypeStruct(s.shape, s.dtype) for s in stages]
    return _exchange_rider(stages, shapes, 3 * len(stages), copies)


def _finish_half(own, landed, name):
    half, cols = own.shape

    def body(own_ref, landed_ref, out_ref):
        c = lax.axis_index("c")
        acc = own_ref[...]
        for k in range(3):
            acc = acc + landed_ref[k].astype(F32)
        out_ref[pl.ds(pl.multiple_of(c * half, 8), half), :] = acc

    return pl.pallas_call(
        body, name=name, in_specs=[pl.BlockSpec(memory_space=pltpu.VMEM)] * 2,
        out_specs=pl.BlockSpec(memory_space=pltpu.VMEM),
        out_shape=jax.ShapeDtypeStruct((2 * half, cols), F32),
        compiler_params=pltpu.CompilerParams(vmem_limit_bytes=VMEM_LIMIT),
    )(own, landed)


def _share_rider(fulls):
    def copies(ins, outs, sems):
        x, y, c, _ = _place()
        out = []
        for i, full in enumerate(outs):
            half = full.shape[0] // 2
            rows = full.at[pl.ds(pl.multiple_of(c * half, 8), half), :]
            out.append(_remote(rows, rows, sems, i, (x, y, 1 - c)))
        return out

    def finish_copies(ins, outs, sems):
        x, y, c, _ = _place()
        out = []
        for i, full in enumerate(outs):
            half = full.shape[0] // 2
            mine = full.at[pl.ds(pl.multiple_of(c * half, 8), half), :]
            theirs = full.at[pl.ds(pl.multiple_of((1 - c) * half, 8), half), :]
            out.append((_remote(mine, mine, sems, i, (x, y, 1 - c)), _remote(theirs, theirs, sems, i, (x, y, 1 - c))))
        return out

    def start(ins, outs, send_sems, recv_sems):
        for cp in copies(ins, outs, (send_sems, recv_sems)):
            cp.start()

    def finish(ins, outs, send_sems, recv_sems):
        for sent, landed in finish_copies(ins, outs, (send_sems, recv_sems)):
            sent.wait_send()
            landed.wait_recv()

    shapes = [jax.ShapeDtypeStruct(f.shape, f.dtype) for f in fulls]
    return _Rider(fulls, shapes, len(fulls), start, finish, aliases={i: i for i in range(len(fulls))})


def _all_sum_small(v):
    shape = v.shape

    def body(v_ref, out_ref, buf, send_sems, recv_sems):
        x, y, c, _ = _place()
        me = 4 * x + 2 * y + c
        buf[me] = v_ref[...]
        flips = [(dx, dy, dc) for dx in (0, 1) for dy in (0, 1) for dc in (0, 1)][1:]

        def copy(k, slot, flip):
            dx, dy, dc = flip
            to = (1 - x if dx else x, 1 - y if dy else y, 1 - c if dc else c)
            return pltpu.make_async_remote_copy(src_ref=buf.at[slot], dst_ref=buf.at[slot], send_sem=send_sems.at[k],
                                                recv_sem=recv_sems.at[k], device_id=to, device_id_type=MESH)

        sends = [copy(k, me, flip) for k, flip in enumerate(flips)]
        for cp in sends:
            cp.start()
        for k, (dx, dy, dc) in enumerate(flips):
            sender = 4 * (1 - x if dx else x) + 2 * (1 - y if dy else y) + (1 - c if dc else c)
            copy(k, sender, (dx, dy, dc)).wait_recv()
        for cp in sends:
            cp.wait_send()
        total = buf[0]
        for i in range(1, 8):
            total = total + buf[i]
        out_ref[...] = total

    return pl.pallas_call(
        body, name="all_sum_small",
        in_specs=[pl.BlockSpec(memory_space=pltpu.VMEM)],
        out_specs=pl.BlockSpec(memory_space=pltpu.VMEM),
        out_shape=jax.ShapeDtypeStruct(shape, F32),
        scratch_shapes=[pltpu.VMEM((8,) + shape, F32), pltpu.SemaphoreType.DMA((7,)), pltpu.SemaphoreType.DMA((7,))],
    )(v)


SMALL = (("g_mix", 1024), ("g_ffn", 1024), ("g_out_fox", 512), ("g_out_dil", 512), ("g_q_fox", 64),
         ("g_k_fox", 64), ("g_q_dil", 64), ("g_k_dil", 64), ("b_forget", 8))
SMALL_PACKED = (32, LANES)


def _local_grads(x, target, gains, w1, wft, dense, packed, nb, seq):
    tile2 = lambda g: jnp.tile(g, (1, 2))
    gq_f, gk_f, gq_d, gk_d = (tile2(gains[n]) for n in ("g_q_fox", "g_k_fox", "g_q_dil", "g_k_dil"))
    b_col = gains["b_forget"].reshape(N_FOX_HEADS, 1)
    cos, up, dn = _rope_tables(seq)
    npair = N_FOX_HEADS // 2

    proj, fa_row, h1, h1_t = _in_proj(x, gains["g_mix"], w1, wft)
    c_row = _gate_fwd(fa_row, b_col, seq)
    c3 = c_row.reshape(npair, 2, nb * seq)
    (o_fox, lse_fox), gathered = _fox_fwd(proj, c3, gq_f, gk_f, nb, seq,
                                          rider=None if packed is None else _gather_rider(packed))
    if packed is not None:
        dense = [g.reshape(-1, g.shape[2]) for g in gathered]
    w_out, w_gate, w_up, w_down = dense
    o_dil, lse_dil = _dil_fwd(proj, gq_d, gk_d, cos, up, dn, nb, seq)
    x1, o_n_t = _attn_out(o_fox, o_dil, x, gains["g_out_fox"], gains["g_out_dil"], w_out)
    a, u, dy, loss_parts = _ffn_fwd(x1, target, gains["g_ffn"], w_gate, w_up, w_down)
    loss = jnp.sum(loss_parts[:, 0, 0])

    dx1, s, da, du, h2, dg_ffn = _ffn_bwd(dy, a, u, x1, gains["g_ffn"], w_gate, w_up, w_down)
    d_w_down = _token_matmul(s, dy, "dw_down", 512, False)
    d_w_gate = _token_matmul(da, h2, "dw_gate", 512, False)
    d_w_up = _token_matmul(du, h2, "dw_up", 512, False)
    d_w_out = _token_matmul(o_n_t, dx1, "dw_out", 1024)
    names = ("w_out", "w_gate", "w_up", "w_down")
    grads4 = [g.reshape(4, -1, g.shape[1]) for g in (d_w_out, d_w_gate, d_w_up, d_w_down)]
    exchange = packed is not None
    (do_fox, do_dil, dg_of, dg_od), from_sibling = _attn_out_bwd(
        dx1, o_fox, o_dil, gains["g_out_fox"], gains["g_out_dil"], w_out,
        rider=_swap_rider(grads4) if exchange else None)
    if exchange:
        sums = [_chip_sum(g, s, "chip_sum_" + n) for g, s, n in zip(grads4, from_sibling, names)]
    (dq_f, dk_f, dv_f, dc3, dg_fox), landed = _fox_bwd(
        proj, c3, gq_f, gk_f, do_fox, o_fox, lse_fox, nb, seq,
        rider=_spread_rider([st for st, _ in sums]) if exchange else None)
    if exchange:
        halves = [_finish_half(own, ld, "finish_half_" + n) for (_, own), ld, n in zip(sums, landed, names)]
    (dq_d, dk_d, dv_d, dg_dil), reduced = _dil_bwd(
        proj, gq_d, gk_d, cos, up, dn, do_dil, o_dil, lse_dil, nb, seq,
        rider=_share_rider(halves) if exchange else None)
    if exchange:
        d_w_out, d_w_gate, d_w_up, d_w_down = reduced
    dfa_row, db = _gate_bwd(dc3.reshape(N_FOX_HEADS, nb * seq), fa_row, b_col, seq)
    dparts = [dq_f, dk_f, dv_f, dq_d, dk_d, dv_d]
    d_w1 = _token_matmul_parts(h1_t, dparts, "dw_in")
    d_wf = _row_matmul(dfa_row, h1, "dw_forget")
    fox_w = 3 * W_GROUP
    d_w_in = jnp.concatenate([d_w1[:, :fox_w], d_wf.T, d_w1[:, fox_w:]], axis=1)
    if exchange:
        shards = [_shards_of_columns(d_w_in)]
        _, from_sibling = _idle_host(_swap_rider(shards), "swap_w_in")
        stage, own = _chip_sum(shards[0], from_sibling[0], "chip_sum_w_in")
    (grad_x, dg_mix), landed = _in_proj_bwd(dparts, dfa_row, w1, wft, x, gains["g_mix"], dx1,
                                            rider=_spread_rider([stage]) if exchange else None)
    if exchange:
        d_w_in = _finish_half(own, landed[0], "finish_half_w_in")

    fold = lambda g2: (g2[:, :HEAD_DIM] + g2[:, HEAD_DIM:])
    small = {
        "g_mix": dg_mix[0:1], "g_ffn": dg_ffn[0:1], "g_out_fox": dg_of[0:1], "g_out_dil": dg_od[0:1],
        "g_q_fox": fold(dg_fox[0:1]), "g_k_fox": fold(dg_fox[1:2]),
        "g_q_dil": fold(dg_dil[0:1]), "g_k_dil": fold(dg_dil[1:2]),
        "b_forget": db[:, 0].reshape(1, N_FOX_HEADS),
    }
    big = {"w_in": d_w_in, "w_out": d_w_out, "w_gate": d_w_gate, "w_up": d_w_up, "w_down": d_w_down}
    return loss, grad_x, big, small


def _shards_of_columns(full, n=4):
    r, nc = full.shape
    return full.reshape(r, n, nc // n).transpose(1, 0, 2)


def _columns_of_shards(slabs):
    n, r, c = slabs.shape
    return slabs.transpose(1, 0, 2).reshape(r, n * c)


def kernel(x, g_mix, w_in, b_forget, g_q_fox, g_k_fox, g_q_dil, g_k_dil, g_out_fox, g_out_dil, w_out, g_ffn, w_gate, w_up, w_down, loss_target, m_g_mix, m_w_in, m_b_forget, m_g_q_fox, m_g_k_fox, m_g_q_dil, m_g_k_dil, m_g_out_fox, m_g_out_dil, m_w_out, m_g_ffn, m_w_gate, m_w_up, m_w_down, v_g_mix, v_w_in, v_b_forget, v_g_q_fox, v_g_k_fox, v_g_q_dil, v_g_k_dil, v_g_out_fox, v_g_out_dil, v_w_out, v_g_ffn, v_w_gate, v_w_up, v_w_down):
    nb, seq, d = x.shape
    weights = dict(g_mix=g_mix, w_in=w_in, b_forget=b_forget, g_q_fox=g_q_fox, g_k_fox=g_k_fox, g_q_dil=g_q_dil,
                   g_k_dil=g_k_dil, g_out_fox=g_out_fox, g_out_dil=g_out_dil, w_out=w_out, g_ffn=g_ffn,
                   w_gate=w_gate, w_up=w_up, w_down=w_down)
    m_in = dict(g_mix=m_g_mix, w_in=m_w_in, b_forget=m_b_forget, g_q_fox=m_g_q_fox, g_k_fox=m_g_k_fox,
                g_q_dil=m_g_q_dil, g_k_dil=m_g_k_dil, g_out_fox=m_g_out_fox, g_out_dil=m_g_out_dil, w_out=m_w_out,
                g_ffn=m_g_ffn, w_gate=m_w_gate, w_up=m_w_up, w_down=m_w_down)
    v_in = dict(g_mix=v_g_mix, w_in=v_w_in, b_forget=v_b_forget, g_q_fox=v_g_q_fox, g_k_fox=v_g_k_fox,
                g_q_dil=v_g_q_dil, g_k_dil=v_g_k_dil, g_out_fox=v_g_out_fox, g_out_dil=v_g_out_dil, w_out=v_w_out,
                g_ffn=v_g_ffn, w_gate=v_w_gate, w_up=v_w_up, w_down=v_w_down)
    order = ["g_mix", "w_in", "b_forget", "g_q_fox", "g_k_fox", "g_q_dil", "g_k_dil", "g_out_fox", "g_out_dil",
             "w_out", "g_ffn", "w_gate", "w_up", "w_down"]

    w_in_full = _columns_of_shards(_gather_weight(w_in, "gather_w_in"))
    fox_w = 3 * W_GROUP
    w1 = jnp.concatenate([w_in_full[:, :fox_w], w_in_full[:, fox_w + N_FOX_HEADS:]], axis=1)
    wft = w_in_full[:, fox_w:fox_w + N_FOX_HEADS].T
    swap = lambda a: jnp.transpose(a, (0, 2, 1))
    for n in ("w_gate", "w_up"):
        weights[n], m_in[n], v_in[n] = swap(weights[n]), swap(m_in[n]), swap(v_in[n])
    shards = _cast_bf16([weights[n] for n in ("w_out", "w_gate", "w_up", "w_down")], "cast_shards")

    gains = {n: weights[n] for n, _ in SMALL}
    loss, grad_x, big, small = _local_grads(
        x.reshape(nb * seq, d), loss_target.reshape(nb * seq, d), gains, w1, wft, None, shards, nb, seq)

    grads = {n: big[n][None] for n in ("w_out", "w_gate", "w_up", "w_down")}
    packed = jnp.concatenate([small[n].reshape(-1) for n, _ in SMALL] + [loss.reshape(1)])
    packed = jnp.pad(packed, (0, SMALL_PACKED[0] * SMALL_PACKED[1] - packed.shape[0])).reshape(SMALL_PACKED)
    summed = _all_sum_small(packed).reshape(-1)
    pos = 0
    for n, size in SMALL:
        grads[n] = summed[pos:pos + size].reshape(1, size)
        pos += size
    loss = summed[pos]

    to_entry = lambda a: jnp.transpose(a, (2, 0, 1))
    deltas, new_m, new_v, grad_out = {}, {}, {}, {}
    for n in ["w_down"] + [n for n in order if n != "w_down"]:
        rider = _share_rider([big["w_in"]]) if n == "w_down" else None
        (deltas[n], new_m[n], new_v[n]), shared = _adamw(weights[n], grads[n], m_in[n], v_in[n], "adamw_" + n, rider)
        if rider is not None:
            grads["w_in"] = to_entry(shared[0][None])
            weights["w_in"], m_in["w_in"], v_in["w_in"] = (to_entry(a) for a in (w_in, m_w_in, v_w_in))
        grad_out[n] = grads[n]
    for n in ("w_gate", "w_up"):
        grad_out[n], deltas[n], new_m[n], new_v[n] = (swap(a) for a in (grad_out[n], deltas[n], new_m[n], new_v[n]))
    from_entry = lambda a: jnp.transpose(a, (1, 2, 0))
    grad_out["w_in"], deltas["w_in"], new_m["w_in"], new_v["w_in"] = (
        from_entry(a) for a in (grad_out["w_in"], deltas["w_in"], new_m["w_in"], new_v["w_in"]))

    return (loss, grad_x.reshape(nb, seq, d), *[grad_out[n] for n in order], *[deltas[n] for n in order],
            *[new_m[n] for n in order], *[new_v[n] for n in order])
```

```python
import functools
import math

import numpy as np
import jax
import jax.numpy as jnp
from jax import lax
from jax.experimental import pallas as pl
from jax.experimental.pallas import tpu as pltpu

F32, BF16 = jnp.float32, jnp.bfloat16
MESH = pl.DeviceIdType.MESH

EPS = 1e-6
NEG = -1e30
HEAD_DIM = 64
SCALE = HEAD_DIM ** -0.5
LOG2E = math.log2(math.e)
LN2 = math.log(2.0)
ROPE_THETA = 500000.0
ROPE_DIM = HEAD_DIM // 4
LANES = 128
W_GROUP = 512
N_FOX_HEADS = 8
VMEM_LIMIT = 56 * 1024 * 1024
DILATIONS = (1, 4, 16)
BAND = 128

ADAM_LR, ADAM_B1, ADAM_B2, ADAM_EPS, ADAM_WD, ADAM_STEP = 0.001, 0.9, 0.999, 1e-08, 0.01, 10

NT = (((1,), (1,)), ((), ()))
TN = (((0,), (0,)), ((), ()))
BATCH_NT = (((2,), (2,)), ((0,), (0,)))
BATCH_NN = (((2,), (1,)), ((0,), (0,)))
BATCH_TN = (((1,), (1,)), ((0,), (0,)))


def _params(sem=None):
    return pltpu.CompilerParams(dimension_semantics=sem, vmem_limit_bytes=VMEM_LIMIT)


def _dot(a, b, dims=None):
    if dims is None:
        return jnp.dot(a, b, preferred_element_type=F32)
    return lax.dot_general(a, b, dims, preferred_element_type=F32)


def _group_ones():
    i = lax.broadcasted_iota(jnp.int32, (LANES, LANES), 0) >> 6
    j = lax.broadcasted_iota(jnp.int32, (LANES, LANES), 1) >> 6
    return (i == j).astype(BF16)


def _split3(x):
    a = x.astype(BF16)
    r = x - a.astype(F32)
    b = r.astype(BF16)
    c = (r - b.astype(F32)).astype(BF16)
    return a, b, c


def _groupsum(x, ones, pieces=2):
    total = None
    for _ in range(pieces):
        piece = x.astype(BF16)
        part = _dot(piece, ones)
        total = part if total is None else total + part
        x = x - piece.astype(F32)
    return total


def _head_masks():
    lane = lax.broadcasted_iota(jnp.int32, (1, LANES), 1)
    return [(lane < HEAD_DIM).astype(F32), (lane >= HEAD_DIM).astype(F32)]


def _head_norm(raw, gain, ones):
    r = lax.rsqrt(_groupsum(raw * raw, ones, 1) * (1.0 / HEAD_DIM) + EPS)
    return raw * r, r


def _head_norm_bwd(dy, xhat, r, gain, ones):
    u = dy * gain
    dgain = jnp.sum(dy * xhat, axis=0, keepdims=True)
    draw = r * (u - xhat * (_groupsum(u * xhat, ones) * (1.0 / HEAD_DIM)))
    return draw, dgain


def _rope(x, cos, s_up, s_dn):
    return x * cos + pltpu.roll(x, LANES - 8, 1) * s_up + pltpu.roll(x, 8, 1) * s_dn


def _rope_bwd(dy, cos, s_up, s_dn):
    return dy * cos + pltpu.roll(dy * s_up, 8, 1) + pltpu.roll(dy * s_dn, LANES - 8, 1)


def _rope_tables(seq):
    half = ROPE_DIM // 2
    inv_freq = jnp.power(jnp.float32(ROPE_THETA), -jnp.arange(half, dtype=F32) * 2.0 / ROPE_DIM)
    ang = jnp.arange(seq).astype(F32)[:, None] * inv_freq[None, :]
    cos, sin = jnp.cos(ang), jnp.sin(ang)
    one = jnp.ones((seq, HEAD_DIM - ROPE_DIM), F32)
    zero_h = jnp.zeros((seq, half), F32)
    zero_r = jnp.zeros((seq, HEAD_DIM - ROPE_DIM), F32)
    c = jnp.concatenate([cos, cos, one], axis=1)
    up = jnp.concatenate([-sin, zero_h, zero_r], axis=1)
    dn = jnp.concatenate([zero_h, sin, zero_r], axis=1)
    return jnp.tile(c, (1, 2)), jnp.tile(up, (1, 2)), jnp.tile(dn, (1, 2))


def _row_tile(rows, cap=256):
    best = rows
    for t in range(8, min(rows, cap) + 1, 8):
        if rows % t == 0:
            best = t
    return best


class _Rider:
    def __init__(self, inputs, out_shapes, n_sems, start, finish, aliases=None, middle=None):
        self.inputs, self.out_shapes, self.n_sems = list(inputs), list(out_shapes), n_sems
        self.start, self.finish, self.middle, self.aliases = start, finish, middle, dict(aliases or {})


def _host_call(body, rider, *, name, grid, in_specs, out_specs, out_shape, scratch_shapes, inputs, semantics):
    if rider is None:
        return pl.pallas_call(body, name=name, grid=grid, in_specs=in_specs, out_specs=out_specs,
                              out_shape=out_shape, scratch_shapes=scratch_shapes,
                              compiler_params=_params(semantics))(*inputs), []
    n_in, n_out, n_scr = len(in_specs), len(out_specs), len(scratch_shapes)
    r_in, r_out = len(rider.inputs), len(rider.out_shapes)

    def wrapped(*refs):
        ins, refs = refs[:n_in], refs[n_in:]
        r_ins, refs = refs[:r_in], refs[r_in:]
        outs, refs = refs[:n_out], refs[n_out:]
        r_outs, refs = refs[:r_out], refs[r_out:]
        scratch, (send_sems, recv_sems) = refs[:n_scr], refs[n_scr:]
        ids = [pl.program_id(a) for a in range(len(grid))]
        first = functools.reduce(lambda p, q: p & q, [i == 0 for i in ids])
        last = functools.reduce(lambda p, q: p & q, [i == g - 1 for i, g in zip(ids, grid)])

        @pl.when(first)
        def _():
            rider.start(r_ins, r_outs, send_sems, recv_sems)

        body(*ins, *outs, *scratch)

        if rider.middle is not None:
            step, steps = ids[0], grid[0]
            for i, g in zip(ids[1:], grid[1:]):
                step, steps = step * g + i, steps * g

            @pl.when(step == (3 * steps) // 4)
            def _():
                rider.middle(r_ins, r_outs, send_sems, recv_sems)

        @pl.when(last)
        def _():
            rider.finish(r_ins, r_outs, send_sems, recv_sems)

    hbm = pl.BlockSpec(memory_space=pl.ANY)
    res = pl.pallas_call(
        wrapped, name=name, grid=grid,
        in_specs=list(in_specs) + [hbm] * r_in, out_specs=list(out_specs) + [hbm] * r_out,
        out_shape=list(out_shape) + rider.out_shapes,
        scratch_shapes=list(scratch_shapes) + [pltpu.SemaphoreType.DMA((rider.n_sems,))] * 2,
        input_output_aliases={n_in + i: n_out + o for i, o in rider.aliases.items()},
        compiler_params=_params(semantics),
    )(*inputs, *rider.inputs)
    return res[:n_out], res[n_out:]


def _idle_host(rider, name):
    def body(o_ref):
        o_ref[...] = jnp.zeros_like(o_ref)

    return _host_call(body, rider, name=name, grid=(1,), in_specs=[],
                      out_specs=[pl.BlockSpec((8, LANES), lambda i: (0, 0))],
                      out_shape=[jax.ShapeDtypeStruct((8, LANES), F32)], scratch_shapes=[], inputs=(),
                      semantics=("arbitrary",))


def _in_proj(x, g_mix, w1, wft):
    t, d = x.shape
    n = w1.shape[1]
    tt = 512

    def body(x_ref, g_ref, w_ref, wf_ref, p_ref, fa_ref, h_ref, ht_ref):
        xx = x_ref[...]
        r = lax.rsqrt(jnp.mean(xx * xx, axis=-1, keepdims=True) + EPS)
        h = (xx * r * g_ref[...]).astype(BF16)
        h_ref[...] = h
        ht_ref[...] = h.T
        for j in range(n // W_GROUP):
            cols = slice(j * W_GROUP, (j + 1) * W_GROUP)
            p_ref[:, cols] = _dot(h, w_ref[:, cols]).astype(BF16)
        fa_ref[...] = _dot(wf_ref[...], h, NT)

    return pl.pallas_call(
        body, name="in_proj", grid=(t // tt,),
        in_specs=[pl.BlockSpec((tt, d), lambda i: (i, 0)), pl.BlockSpec((1, d), lambda i: (0, 0)),
                  pl.BlockSpec(memory_space=pltpu.VMEM), pl.BlockSpec(memory_space=pltpu.VMEM)],
        out_specs=[pl.BlockSpec((tt, n), lambda i: (i, 0)), pl.BlockSpec((8, tt), lambda i: (0, i)),
                   pl.BlockSpec((tt, d), lambda i: (i, 0)), pl.BlockSpec((d, tt), lambda i: (0, i))],
        out_shape=[jax.ShapeDtypeStruct((t, n), BF16), jax.ShapeDtypeStruct((8, t), F32),
                   jax.ShapeDtypeStruct((t, d), BF16), jax.ShapeDtypeStruct((d, t), BF16)],
        compiler_params=_params(("arbitrary",)),
    )(x, g_mix, w1, wft)


def _tri(n, upper):
    i = lax.broadcasted_iota(jnp.int32, (n, n), 0)
    j = lax.broadcasted_iota(jnp.int32, (n, n), 1)
    return ((i <= j) if upper else (i >= j)).astype(BF16)


def _gate_fwd(fa_row, b_col, seq):
    t = fa_row.shape[1]
    cb = 256

    def body(fa_ref, b_ref, c_ref):
        tri = _tri(cb, True)
        carry = jnp.zeros((8, 1), F32)
        for k in range(seq // cb):
            z = fa_ref[:, k * cb:(k + 1) * cb] + b_ref[...]
            lf = jnp.minimum(z, 0.0) - jnp.log(1.0 + jnp.exp(-jnp.abs(z)))
            a, b, c = _split3(lf)
            blk = _dot(a, tri) + _dot(b, tri) + _dot(c, tri) + carry
            c_ref[:, k * cb:(k + 1) * cb] = blk
            carry = blk[:, cb - 1:cb]

    return pl.pallas_call(
        body, name="gate_fwd", grid=(t // seq,),
        in_specs=[pl.BlockSpec((8, seq), lambda i: (0, i)), pl.BlockSpec((8, 1), lambda i: (0, 0))],
        out_specs=pl.BlockSpec((8, seq), lambda i: (0, i)),
        out_shape=jax.ShapeDtypeStruct((8, t), F32),
        compiler_params=_params(("arbitrary",)),
    )(fa_row, b_col)


def _gate_bwd(dc_row, fa_row, b_col, seq):
    t = fa_row.shape[1]
    cb = 256

    def body(dc_ref, fa_ref, b_ref, dfa_ref, db_ref):
        @pl.when(pl.program_id(0) == 0)
        def _():
            db_ref[...] = jnp.zeros_like(db_ref)

        tri = _tri(cb, False)
        carry = jnp.zeros((8, 1), F32)
        dbs = jnp.zeros((8, 1), F32)
        for k in reversed(range(seq // cb)):
            a, b, c = _split3(dc_ref[:, k * cb:(k + 1) * cb])
            dlf = _dot(a, tri) + _dot(b, tri) + _dot(c, tri) + carry
            carry = dlf[:, 0:1]
            z = fa_ref[:, k * cb:(k + 1) * cb] + b_ref[...]
            dfa = dlf / (1.0 + jnp.exp(z))
            dfa_ref[:, k * cb:(k + 1) * cb] = dfa
            dbs = dbs + jnp.sum(dfa, axis=1, keepdims=True)
        db_ref[...] += jnp.broadcast_to(dbs, (8, LANES))

    return pl.pallas_call(
        body, name="gate_bwd", grid=(t // seq,),
        in_specs=[pl.BlockSpec((8, seq), lambda i: (0, i)), pl.BlockSpec((8, seq), lambda i: (0, i)),
                  pl.BlockSpec((8, 1), lambda i: (0, 0))],
        out_specs=[pl.BlockSpec((8, seq), lambda i: (0, i)), pl.BlockSpec((8, LANES), lambda i: (0, 0))],
        out_shape=[jax.ShapeDtypeStruct((8, t), F32), jax.ShapeDtypeStruct((8, LANES), F32)],
        compiler_params=_params(("arbitrary",)),
    )(dc_row, fa_row, b_col)


def _attn_out(o_fox, o_dil, x, g_fox, g_dil, w_out):
    t, d = x.shape
    w = o_fox.shape[1]
    tt = 512

    def body(of_ref, od_ref, x_ref, gf_ref, gd_ref, w_ref, x1_ref, ont_ref):
        acc = x_ref[...]
        for k, (o_ref, g_ref) in enumerate(((of_ref, gf_ref), (od_ref, gd_ref))):
            o = o_ref[...]
            r = lax.rsqrt(jnp.mean(o * o, axis=-1, keepdims=True) + EPS)
            on = (o * r * g_ref[...]).astype(BF16)
            ont_ref[k * w:(k + 1) * w, :] = on.T
            acc = acc + _dot(on, w_ref[k * w:(k + 1) * w, :])
        x1_ref[...] = acc

    return pl.pallas_call(
        body, name="attn_out", grid=(t // tt,),
        in_specs=[pl.BlockSpec((tt, w), lambda i: (i, 0)), pl.BlockSpec((tt, w), lambda i: (i, 0)),
                  pl.BlockSpec((tt, d), lambda i: (i, 0)), pl.BlockSpec((1, w), lambda i: (0, 0)),
                  pl.BlockSpec((1, w), lambda i: (0, 0)), pl.BlockSpec(memory_space=pltpu.VMEM)],
        out_specs=[pl.BlockSpec((tt, d), lambda i: (i, 0)), pl.BlockSpec((2 * w, tt), lambda i: (0, i))],
        out_shape=[jax.ShapeDtypeStruct((t, d), F32), jax.ShapeDtypeStruct((2 * w, t), BF16)],
        compiler_params=_params(("arbitrary",)),
    )(o_fox, o_dil, x, g_fox, g_dil, w_out)


def _attn_out_bwd(dx1, o_fox, o_dil, g_fox, g_dil, w_out, rider=None):
    t, d = dx1.shape
    w = o_fox.shape[1]
    tt = 512

    def body(dx_ref, of_ref, od_ref, gf_ref, gd_ref, w_ref, dof_ref, dod_ref, dgf_ref, dgd_ref):
        @pl.when(pl.program_id(0) == 0)
        def _():
            dgf_ref[...] = jnp.zeros_like(dgf_ref)
            dgd_ref[...] = jnp.zeros_like(dgd_ref)

        dxb = dx_ref[...].astype(BF16)
        for k, (o_ref, g_ref, do_ref, dg_ref) in enumerate(
                ((of_ref, gf_ref, dof_ref, dgf_ref), (od_ref, gd_ref, dod_ref, dgd_ref))):
            don = _dot(dxb, w_ref[k * w:(k + 1) * w, :], NT)
            o = o_ref[...]
            r = lax.rsqrt(jnp.mean(o * o, axis=-1, keepdims=True) + EPS)
            xhat = o * r
            u = don * g_ref[...]
            do_ref[...] = r * (u - xhat * jnp.mean(u * xhat, axis=-1, keepdims=True))
            dg_ref[0:1, :] += jnp.sum(don * xhat, axis=0, keepdims=True)

    return _host_call(
        body, rider, name="attn_out_bwd", grid=(t // tt,),
        in_specs=[pl.BlockSpec((tt, d), lambda i: (i, 0)), pl.BlockSpec((tt, w), lambda i: (i, 0)),
                  pl.BlockSpec((tt, w), lambda i: (i, 0)), pl.BlockSpec((1, w), lambda i: (0, 0)),
                  pl.BlockSpec((1, w), lambda i: (0, 0)), pl.BlockSpec(memory_space=pltpu.VMEM)],
        out_specs=[pl.BlockSpec((tt, w), lambda i: (i, 0)), pl.BlockSpec((tt, w), lambda i: (i, 0)),
                   pl.BlockSpec((8, w), lambda i: (0, 0)), pl.BlockSpec((8, w), lambda i: (0, 0))],
        out_shape=[jax.ShapeDtypeStruct((t, w), F32), jax.ShapeDtypeStruct((t, w), F32),
                   jax.ShapeDtypeStruct((8, w), F32), jax.ShapeDtypeStruct((8, w), F32)],
        scratch_shapes=[], inputs=(dx1, o_fox, o_dil, g_fox, g_dil, w_out), semantics=("arbitrary",))


def _ffn_fwd(x1, target, g_ffn, w_gate, w_up, w_down):
    t, d = x1.shape
    f = w_gate.shape[0]
    tt = 256

    def body(x_ref, t_ref, g_ref, wg_ref, wu_ref, wd_ref, a_ref, u_ref, dy_ref, loss_ref):
        xx = x_ref[...]
        r = lax.rsqrt(jnp.mean(xx * xx, axis=-1, keepdims=True) + EPS)
        h = (xx * r * g_ref[...]).astype(BF16)
        a = _dot(h, wg_ref[...], NT)
        u = _dot(h, wu_ref[...], NT)
        a_ref[...] = a.astype(BF16)
        u_ref[...] = u.astype(BF16)
        s = (a / (1.0 + jnp.exp(-a)) * u).astype(BF16)
        y = xx + _dot(s, wd_ref[...])
        e = y - t_ref[...]
        dy_ref[...] = e * (1.0 / d)
        loss_ref[...] = jnp.broadcast_to(0.5 * jnp.sum(e * e) * (1.0 / d), (1, 8, LANES))

    return pl.pallas_call(
        body, name="ffn_fwd", grid=(t // tt,),
        in_specs=[pl.BlockSpec((tt, d), lambda i: (i, 0)), pl.BlockSpec((tt, d), lambda i: (i, 0)),
                  pl.BlockSpec((1, d), lambda i: (0, 0)), pl.BlockSpec(memory_space=pltpu.VMEM),
                  pl.BlockSpec(memory_space=pltpu.VMEM), pl.BlockSpec(memory_space=pltpu.VMEM)],
        out_specs=[pl.BlockSpec((tt, f), lambda i: (i, 0)), pl.BlockSpec((tt, f), lambda i: (i, 0)),
                   pl.BlockSpec((tt, d), lambda i: (i, 0)), pl.BlockSpec((1, 8, LANES), lambda i: (i, 0, 0))],
        out_shape=[jax.ShapeDtypeStruct((t, f), BF16), jax.ShapeDtypeStruct((t, f), BF16),
                   jax.ShapeDtypeStruct((t, d), F32), jax.ShapeDtypeStruct((t // tt, 8, LANES), F32)],
        compiler_params=_params(("arbitrary",)),
    )(x1, target, g_ffn, w_gate, w_up, w_down)


def _ffn_bwd(dy, a, u, x1, g_ffn, w_gate, w_up, w_down):
    t, d = x1.shape
    f = w_gate.shape[0]
    tt = 256

    def body(dy_ref, a_ref, u_ref, x_ref, g_ref, wg_ref, wu_ref, wd_ref,
             dx_ref, s_ref, da_ref, du_ref, h_ref, dg_ref):
        @pl.when(pl.program_id(0) == 0)
        def _():
            dg_ref[...] = jnp.zeros_like(dg_ref)

        dy_ = dy_ref[...]
        ds = _dot(dy_.astype(BF16), wd_ref[...], NT)
        a_ = a_ref[...].astype(F32)
        u_ = u_ref[...].astype(F32)
        sig = 1.0 / (1.0 + jnp.exp(-a_))
        silu = a_ * sig
        s_ref[...] = (silu * u_).astype(BF16)
        da = (ds * u_ * (sig * (1.0 + a_ * (1.0 - sig)))).astype(BF16)
        du = (ds * silu).astype(BF16)
        da_ref[...] = da
        du_ref[...] = du
        dh = _dot(da, wg_ref[...]) + _dot(du, wu_ref[...])
        xx = x_ref[...]
        r = lax.rsqrt(jnp.mean(xx * xx, axis=-1, keepdims=True) + EPS)
        xhat = xx * r
        g = g_ref[...]
        h_ref[...] = (xhat * g).astype(BF16)
        uu = dh * g
        dx_ref[...] = dy_ + r * (uu - xhat * jnp.mean(uu * xhat, axis=-1, keepdims=True))
        dg_ref[0:1, :] += jnp.sum(dh * xhat, axis=0, keepdims=True)

    return pl.pallas_call(
        body, name="ffn_bwd", grid=(t // tt,),
        in_specs=[pl.BlockSpec((tt, d), lambda i: (i, 0)), pl.BlockSpec((tt, f), lambda i: (i, 0)),
                  pl.BlockSpec((tt, f), lambda i: (i, 0)), pl.BlockSpec((tt, d), lambda i: (i, 0)),
                  pl.BlockSpec((1, d), lambda i: (0, 0)), pl.BlockSpec(memory_space=pltpu.VMEM),
                  pl.BlockSpec(memory_space=pltpu.VMEM), pl.BlockSpec(memory_space=pltpu.VMEM)],
        out_specs=[pl.BlockSpec((tt, d), lambda i: (i, 0)), pl.BlockSpec((tt, f), lambda i: (i, 0)),
                   pl.BlockSpec((tt, f), lambda i: (i, 0)), pl.BlockSpec((tt, f), lambda i: (i, 0)),
                   pl.BlockSpec((tt, d), lambda i: (i, 0)), pl.BlockSpec((8, d), lambda i: (0, 0))],
        out_shape=[jax.ShapeDtypeStruct((t, d), F32), jax.ShapeDtypeStruct((t, f), BF16),
                   jax.ShapeDtypeStruct((t, f), BF16), jax.ShapeDtypeStruct((t, f), BF16),
                   jax.ShapeDtypeStruct((t, d), BF16), jax.ShapeDtypeStruct((8, d), F32)],
        compiler_params=_params(("arbitrary",)),
    )(dy, a, u, x1, g_ffn, w_gate, w_up, w_down)


def _in_proj_bwd(dparts, dfa_row, w1, wft, x, g_mix, dx1, rider=None):
    t, d = x.shape
    tt = 512
    npart = len(dparts)

    def body(*refs):
        dp_refs = refs[:npart]
        dfa_ref, w_ref, wf_ref, x_ref, g_ref, dx1_ref, dx_ref, dg_ref = refs[npart:]

        @pl.when(pl.program_id(0) == 0)
        def _():
            dg_ref[...] = jnp.zeros_like(dg_ref)

        dh = _dot(dfa_ref[...].astype(BF16), wf_ref[...], TN)
        for j in range(npart):
            dh = dh + _dot(dp_refs[j][...], w_ref[:, j * W_GROUP:(j + 1) * W_GROUP], NT)
        xx = x_ref[...]
        r = lax.rsqrt(jnp.mean(xx * xx, axis=-1, keepdims=True) + EPS)
        xhat = xx * r
        uu = dh * g_ref[...]
        dx_ref[...] = dx1_ref[...] + r * (uu - xhat * jnp.mean(uu * xhat, axis=-1, keepdims=True))
        dg_ref[0:1, :] += jnp.sum(dh * xhat, axis=0, keepdims=True)

    return _host_call(
        body, rider, name="in_proj_bwd", grid=(t // tt,),
        in_specs=[pl.BlockSpec((tt, W_GROUP), lambda i: (i, 0)) for _ in range(npart)]
        + [pl.BlockSpec((8, tt), lambda i: (0, i)), pl.BlockSpec(memory_space=pltpu.VMEM),
           pl.BlockSpec(memory_space=pltpu.VMEM), pl.BlockSpec((tt, d), lambda i: (i, 0)),
           pl.BlockSpec((1, d), lambda i: (0, 0)), pl.BlockSpec((tt, d), lambda i: (i, 0))],
        out_specs=[pl.BlockSpec((tt, d), lambda i: (i, 0)), pl.BlockSpec((8, d), lambda i: (0, 0))],
        out_shape=[jax.ShapeDtypeStruct((t, d), F32), jax.ShapeDtypeStruct((8, d), F32)],
        scratch_shapes=[], inputs=(*dparts, dfa_row, w1, wft, x, g_mix, dx1), semantics=("arbitrary",))


def _token_matmul(a, b, name, tn, a_is_transposed=True):
    m, t = a.shape if a_is_transposed else a.shape[::-1]
    n = b.shape[1]
    tk = 1024

    def body(a_ref, b_ref, o_ref):
        @pl.when(pl.program_id(1) == 0)
        def _():
            o_ref[...] = jnp.zeros_like(o_ref)

        o_ref[...] += _dot(a_ref[...], b_ref[...].astype(BF16), None if a_is_transposed else TN)

    a_spec = pl.BlockSpec((m, tk), lambda j, k: (0, k)) if a_is_transposed else pl.BlockSpec((tk, m), lambda j, k: (k, 0))
    return pl.pallas_call(
        body, name=name, grid=(n // tn, t // tk),
        in_specs=[a_spec, pl.BlockSpec((tk, tn), lambda j, k: (k, j))],
        out_specs=pl.BlockSpec((m, tn), lambda j, k: (0, j)),
        out_shape=jax.ShapeDtypeStruct((m, n), F32),
        compiler_params=_params(("arbitrary", "arbitrary")),
    )(a, b)


def _token_matmul_parts(at, parts, name):
    m, t = at.shape
    widths = [p.shape[1] for p in parts]
    tk = 1024

    def body(a_ref, *refs):
        o_ref = refs[-1]

        @pl.when(pl.program_id(0) == 0)
        def _():
            o_ref[...] = jnp.zeros_like(o_ref)

        a, first = a_ref[...], 0
        for b_ref, w in zip(refs[:-1], widths):
            o_ref[:, first:first + w] += _dot(a, b_ref[...])
            first += w

    return pl.pallas_call(
        body, name=name, grid=(t // tk,),
        in_specs=[pl.BlockSpec((m, tk), lambda k: (0, k))] + [pl.BlockSpec((tk, w), lambda k: (k, 0)) for w in widths],
        out_specs=pl.BlockSpec((m, sum(widths)), lambda k: (0, 0)),
        out_shape=jax.ShapeDtypeStruct((m, sum(widths)), F32),
        compiler_params=_params(("arbitrary",)),
    )(at, *parts)


def _row_matmul(a_row, b, name):
    t, n = b.shape
    tk = 1024
    nk = t // tk

    def body(a_ref, b_ref, o_ref):
        @pl.when(pl.program_id(0) == 0)
        def _():
            o_ref[...] = jnp.zeros_like(o_ref)

        o_ref[...] += _dot(a_ref[...].astype(BF16), b_ref[...])

    return pl.pallas_call(
        body, name=name, grid=(nk,),
        in_specs=[pl.BlockSpec((8, tk), lambda k: (0, k)), pl.BlockSpec((tk, n), lambda k: (k, 0))],
        out_specs=pl.BlockSpec((8, n), lambda k: (0, 0)),
        out_shape=jax.ShapeDtypeStruct((8, n), F32),
        compiler_params=_params(("arbitrary",)),
    )(a_row, b)


FOX_TQ = 256
SUM_LANE = (HEAD_DIM, 0)


def _fox_fwd(proj, c3, gq, gk, nb, seq, rider=None):
    t = nb * seq
    tq = FOX_TQ
    nq = seq // tq
    npair = N_FOX_HEADS // 2

    def body(q_ref, k_ref, v_ref, c_ref, gq_ref, gk_ref, o_ref, lse_ref, qs, ks, vs):
        ones = _group_ones()
        masks = _head_masks()
        qhat, _ = _head_norm(q_ref[...].astype(F32), None, ones)
        khat, _ = _head_norm(k_ref[...].astype(F32), None, ones)
        qs[...] = (qhat * gq_ref[...] * (SCALE * LOG2E)).astype(BF16)
        kn = khat * gk_ref[...]
        vv = v_ref[...].astype(F32)
        lane = lax.broadcasted_iota(jnp.int32, (1, LANES), 1)
        for hd in range(2):
            ks[hd] = (kn * masks[hd]).astype(BF16)
            vs[hd] = (vv * masks[hd] + (lane == SUM_LANE[hd]).astype(F32)).astype(BF16)
        row = lax.broadcasted_iota(jnp.int32, (tq, tq), 0)
        col = lax.broadcasted_iota(jnp.int32, (tq, tq), 1)
        causal = col <= row

        for qi in range(nq):
            q0 = qi * tq
            q_blk = qs[q0:q0 + tq, :]
            o_tot = jnp.zeros((tq, LANES), F32)
            lse_tot = jnp.zeros((tq, LANES), F32)
            for hd in range(2):
                crow = c_ref[0, hd:hd + 1, 0:q0 + tq] * LOG2E
                c0 = crow[:, q0:q0 + 1]
                s_d = _dot(q_blk, ks[hd, q0:q0 + tq, :], NT) + (c0 - crow[:, q0:q0 + tq])
                s_d = jnp.where(causal, s_d, NEG)
                m = jnp.max(s_d, axis=-1, keepdims=True)
                if qi > 0:
                    s_o = _dot(q_blk, ks[hd, 0:q0, :], NT) + (c0 - crow[:, 0:q0])
                    m = jnp.maximum(m, jnp.max(s_o, axis=-1, keepdims=True))
                acc = _dot(jnp.exp2(s_d - m).astype(BF16), vs[hd, q0:q0 + tq, :])
                if qi > 0:
                    acc = acc + _dot(jnp.exp2(s_o - m).astype(BF16), vs[hd, 0:q0, :])
                l = acc[:, SUM_LANE[hd]:SUM_LANE[hd] + 1]
                o_tot = o_tot + (acc / l) * masks[hd]
                lse_tot = lse_tot + (m + jnp.log2(l) - c0) * masks[hd]
            o_ref[q0:q0 + tq, :] = o_tot
            lse_ref[q0:q0 + tq, :] = lse_tot

    blk = lambda off: pl.BlockSpec((seq, LANES), lambda b, p: (b, off + p))
    return _host_call(
        body, rider, name="fox_fwd", grid=(nb, npair),
        in_specs=[blk(0), blk(npair), blk(2 * npair), pl.BlockSpec((1, 2, seq), lambda b, p: (p, 0, b)),
                  pl.BlockSpec((1, LANES), lambda b, p: (0, 0)), pl.BlockSpec((1, LANES), lambda b, p: (0, 0))],
        out_specs=[blk(0), blk(0)],
        out_shape=[jax.ShapeDtypeStruct((t, W_GROUP), F32), jax.ShapeDtypeStruct((t, W_GROUP), F32)],
        scratch_shapes=[pltpu.VMEM((seq, LANES), BF16), pltpu.VMEM((2, seq, LANES), BF16),
                        pltpu.VMEM((2, seq, LANES), BF16)],
        inputs=(proj, proj, proj, c3, gq, gk), semantics=("arbitrary", "arbitrary"))


def _fox_bwd(proj, c3, gq, gk, do, o, lse, nb, seq, rider=None):
    t = nb * seq
    tq = FOX_TQ
    nq = seq // tq
    npair = N_FOX_HEADS // 2

    def body(q_ref, k_ref, v_ref, c_ref, gq_ref, gk_ref, do_ref, o_ref, lse_ref,
             dq_ref, dk_ref, dv_ref, dc_ref, dg_ref, qs, ks, vs, kts, dos, lse_t, delta_t, dqt_acc, dk_acc, dv_acc,
             row_sum):
        @pl.when((pl.program_id(0) == 0) & (pl.program_id(1) == 0))
        def _():
            dg_ref[...] = jnp.zeros_like(dg_ref)

        ones = _group_ones()
        masks = _head_masks()
        qhat, rq = _head_norm(q_ref[...].astype(F32), None, ones)
        khat, rk = _head_norm(k_ref[...].astype(F32), None, ones)
        qs[...] = (qhat * gq_ref[...] * (SCALE * LOG2E)).astype(BF16)
        kn = khat * gk_ref[...]
        vv = v_ref[...].astype(F32)
        for hd in range(2):
            ks[hd] = (kn * masks[hd]).astype(BF16)
            vs[hd] = (vv * masks[hd]).astype(BF16)
            kts[hd] = ks[hd].T
        dof = do_ref[...]
        dos[...] = dof.astype(BF16)
        lse_t[...] = lse_ref[...].T
        delta_t[...] = _groupsum(dof * o_ref[...], ones).T
        dqt_acc[...] = jnp.zeros_like(dqt_acc)
        dk_acc[...] = jnp.zeros_like(dk_acc)
        dv_acc[...] = jnp.zeros_like(dv_acc)
        row_sum[...] = jnp.zeros_like(row_sum)
        key = lax.broadcasted_iota(jnp.int32, (tq, tq), 0)
        qry = lax.broadcasted_iota(jnp.int32, (tq, tq), 1)
        causal = key <= qry

        for hd in range(2):
            lane0 = hd * HEAD_DIM
            for kj in range(nq):
                k0 = kj * tq
                k_blk = ks[hd, k0:k0 + tq, :]
                v_blk = vs[hd, k0:k0 + tq, :]
                kt_blk = kts[hd, :, k0:k0 + tq]
                crow = c_ref[0, hd:hd + 1, k0:k0 + tq] * LOG2E
                ck0 = crow[:, 0:1]
                bias = jnp.broadcast_to(ck0 - crow, (LANES, tq)).T[:, 0:1]

                def queries_step(r0, r1, diag, hd=hd, lane0=lane0, k_blk=k_blk, v_blk=v_blk, kt_blk=kt_blk,
                                 bias=bias, ck0=ck0):
                    q_r = qs[r0:r1, :]
                    do_r = dos[r0:r1, :]
                    z = _dot(k_blk, q_r, NT) + bias
                    p = jnp.exp2(z - (lse_t[lane0:lane0 + 1, r0:r1] + ck0))
                    if diag:
                        p = jnp.where(causal, p, 0.0)
                    dp = _dot(v_blk, do_r, NT)
                    ds = p * (dp - delta_t[lane0:lane0 + 1, r0:r1])
                    dsb = ds.astype(BF16)
                    dqt_acc[:, r0:r1] += _dot(kt_blk, dsb)
                    row_sum[hd:hd + 1, r0:r1] += jnp.sum(ds, axis=0, keepdims=True)
                    return _dot(dsb, q_r), _dot(p.astype(BF16), do_r), -jnp.sum(ds, axis=1, keepdims=True)

                dk_j, dv_j, dc_j = queries_step(k0, k0 + tq, True)
                if k0 + tq < seq:
                    dk_o, dv_o, dc_o = queries_step(k0 + tq, seq, False)
                    dk_j, dv_j, dc_j = dk_j + dk_o, dv_j + dv_o, dc_j + dc_o
                dk_acc[k0:k0 + tq, :] += dk_j * masks[hd]
                dv_acc[k0:k0 + tq, :] += dv_j * masks[hd]
                dc_ref[0, hd:hd + 1, k0:k0 + tq] = jnp.broadcast_to(dc_j, (tq, LANES)).T[0:1, :]

        dc_ref[0] += row_sum[0:2, :]

        dq_raw, dgq = _head_norm_bwd(dqt_acc[...].T * SCALE, qhat, rq, gq_ref[...], ones)
        dk_raw, dgk = _head_norm_bwd(dk_acc[...] * LN2, khat, rk, gk_ref[...], ones)
        dq_ref[...] = dq_raw.astype(BF16)
        dk_ref[...] = dk_raw.astype(BF16)
        dv_ref[...] = dv_acc[...].astype(BF16)
        dg_ref[0:1, :] += dgq
        dg_ref[1:2, :] += dgk

    blk = lambda off: pl.BlockSpec((seq, LANES), lambda b, p: (b, off + p))
    vec = pl.BlockSpec((1, LANES), lambda b, p: (0, 0))
    c_spec = pl.BlockSpec((1, 2, seq), lambda b, p: (p, 0, b))
    return _host_call(
        body, rider, name="fox_bwd", grid=(nb, npair),
        in_specs=[blk(0), blk(npair), blk(2 * npair), c_spec, vec, vec, blk(0), blk(0), blk(0)],
        out_specs=[blk(0), blk(0), blk(0), c_spec, pl.BlockSpec((8, LANES), lambda b, p: (0, 0))],
        out_shape=[jax.ShapeDtypeStruct((t, W_GROUP), BF16), jax.ShapeDtypeStruct((t, W_GROUP), BF16),
                   jax.ShapeDtypeStruct((t, W_GROUP), BF16), jax.ShapeDtypeStruct((npair, 2, t), F32),
                   jax.ShapeDtypeStruct((8, LANES), F32)],
        scratch_shapes=[pltpu.VMEM((seq, LANES), BF16), pltpu.VMEM((2, seq, LANES), BF16),
                        pltpu.VMEM((2, seq, LANES), BF16), pltpu.VMEM((2, LANES, seq), BF16),
                        pltpu.VMEM((seq, LANES), BF16), pltpu.VMEM((LANES, seq), F32),
                        pltpu.VMEM((LANES, seq), F32), pltpu.VMEM((LANES, seq), F32),
                        pltpu.VMEM((seq, LANES), F32), pltpu.VMEM((seq, LANES), F32),
                        pltpu.VMEM((8, seq), F32)],
        inputs=(proj, proj, proj, c3, gq, gk, do, o, lse), semantics=("arbitrary", "arbitrary"))


def _dil_prep(q_ref, k_ref, gq_ref, gk_ref, cos_ref, up_ref, dn_ref, ones):
    qhat, rq = _head_norm(q_ref[...].astype(F32), None, ones)
    khat, rk = _head_norm(k_ref[...].astype(F32), None, ones)
    cos, up, dn = cos_ref[...], up_ref[...], dn_ref[...]
    qn = _rope(qhat * gq_ref[...], cos, up, dn) * (SCALE * LOG2E)
    kn = _rope(khat * gk_ref[...], cos, up, dn)
    return qhat, rq, khat, rk, qn, kn


def _dil_keys(d, seq, kp, vp, kw, vw):
    nblk = seq // BAND
    per_res = seq // (d * BAND)
    as_blocks = lambda ref, rows: ref[rows, :].reshape(-1, BAND, LANES)
    if per_res == 1:
        a = lax.broadcasted_iota(jnp.int32, (1, BAND, BAND), 1)
        j = lax.broadcasted_iota(jnp.int32, (1, BAND, BAND), 2)
        causal = jnp.where(j <= a, 0.0, NEG)
        return as_blocks(kp, slice(0, seq)), as_blocks(vp, slice(0, seq)), [causal]
    for src, dst in ((kp, kw), (vp, vw)):
        dst[:, BAND:, :] = as_blocks(src, slice(0, seq))
        dst[1:, :BAND, :] = as_blocks(src, slice(0, seq - BAND))
        dst[0:1, :BAND, :] = jnp.zeros((1, BAND, LANES), BF16)
    a = lax.broadcasted_iota(jnp.int32, (1, BAND, 2 * BAND), 1)
    j = lax.broadcasted_iota(jnp.int32, (1, BAND, 2 * BAND), 2)
    band = jnp.where(((j < BAND) & (j >= a)) | ((j >= BAND) & (j - BAND <= a)), 0.0, NEG)
    e = lax.broadcasted_iota(jnp.int32, (nblk, 1, 2 * BAND), 0)
    j = lax.broadcasted_iota(jnp.int32, (nblk, 1, 2 * BAND), 2)
    no_prev = jnp.where(((e & (per_res - 1)) == 0) & (j < BAND), NEG, 0.0)
    return kw[...], vw[...], [band + no_prev]


def _residues(d, seq):
    n = seq // d
    if d == 1:
        return [(slice(0, seq), slice(0, seq))]
    return [(pl.ds(r, n, stride=d), slice(r * n, (r + 1) * n)) for r in range(d)]


def _dil_fwd(proj, gq, gk, cos, up, dn, nb, seq):
    t = nb * seq
    npair = W_GROUP // LANES
    off = 3 * npair

    def body(q_ref, k_ref, v_ref, gq_ref, gk_ref, cos_ref, up_ref, dn_ref, o_ref, lse_ref,
             qs, ks, vs, qp, kp, vp, kw, vw, m_b, l_b, o_b, m_s, l_s, o_s):
        ones = _group_ones()
        masks = _head_masks()
        _, _, _, _, qn, kn = _dil_prep(q_ref, k_ref, gq_ref, gk_ref, cos_ref, up_ref, dn_ref, ones)
        qs[...] = qn
        ks[...] = kn
        vs[...] = v_ref[...].astype(F32)
        nblk = seq // BAND

        for d in DILATIONS:
            for tok, res in _residues(d, seq):
                qv = qs[tok, :]
                for hd in range(2):
                    qp[hd, res, :] = (qv * masks[hd]).astype(BF16)
                kp[res, :] = ks[tok, :].astype(BF16)
                vp[res, :] = vs[tok, :].astype(BF16)
            keys_k, keys_v, bias = _dil_keys(d, seq, kp, vp, kw, vw)
            m_t = jnp.zeros((nblk, BAND, LANES), F32)
            l_t = jnp.zeros((nblk, BAND, LANES), F32)
            o_t = jnp.zeros((nblk, BAND, LANES), F32)
            for hd in range(2):
                s = _dot(qp[hd].reshape(nblk, BAND, LANES), keys_k, BATCH_NT)
                for b_ in bias:
                    s = s + b_
                m = jnp.max(s, axis=-1, keepdims=True)
                p = jnp.exp2(s - m)
                m_t = m_t + m * masks[hd]
                l_t = l_t + jnp.sum(p, axis=-1, keepdims=True) * masks[hd]
                o_t = o_t + _dot(p.astype(BF16), keys_v, BATCH_NN) * masks[hd]
            m_b[...] = m_t.reshape(seq, LANES)
            l_b[...] = l_t.reshape(seq, LANES)
            o_b[...] = o_t.reshape(seq, LANES)
            for tok, res in _residues(d, seq):
                if d == DILATIONS[0]:
                    m_s[tok, :] = m_b[res, :]
                    l_s[tok, :] = l_b[res, :]
                    o_s[tok, :] = o_b[res, :]
                else:
                    m_old = m_s[tok, :]
                    m_new = jnp.maximum(m_old, m_b[res, :])
                    w_old = jnp.exp2(m_old - m_new)
                    w_new = jnp.exp2(m_b[res, :] - m_new)
                    l_s[tok, :] = l_s[tok, :] * w_old + l_b[res, :] * w_new
                    o_s[tok, :] = o_s[tok, :] * w_old + o_b[res, :] * w_new
                    m_s[tok, :] = m_new

        l = l_s[...]
        o_ref[...] = o_s[...] / l
        lse_ref[...] = m_s[...] + jnp.log2(l)

    blk = lambda o_: pl.BlockSpec((seq, LANES), lambda b, p: (b, o_ + p))
    vec = pl.BlockSpec((1, LANES), lambda b, p: (0, 0))
    tab = pl.BlockSpec((seq, LANES), lambda b, p: (0, 0))
    f32_buf = pltpu.VMEM((seq, LANES), F32)
    bf16_buf = pltpu.VMEM((seq, LANES), BF16)
    window_buf = pltpu.VMEM((seq // BAND, 2 * BAND, LANES), BF16)
    return pl.pallas_call(
        body, name="dil_fwd", grid=(nb, npair),
        in_specs=[blk(off), blk(off + npair), blk(off + 2 * npair), vec, vec, tab, tab, tab],
        out_specs=[blk(0), blk(0)],
        out_shape=[jax.ShapeDtypeStruct((t, W_GROUP), F32), jax.ShapeDtypeStruct((t, W_GROUP), F32)],
        scratch_shapes=[f32_buf, f32_buf, f32_buf, pltpu.VMEM((2, seq, LANES), BF16), bf16_buf, bf16_buf,
                        window_buf, window_buf, f32_buf, f32_buf, f32_buf, f32_buf, f32_buf, f32_buf],
        compiler_params=_params(("arbitrary", "arbitrary")),
    )(proj, proj, proj, gq, gk, cos, up, dn)


def _dil_bwd(proj, gq, gk, cos, up, dn, do, o, lse, nb, seq, rider=None):
    t = nb * seq
    npair = W_GROUP // LANES
    off = 3 * npair

    def body(q_ref, k_ref, v_ref, gq_ref, gk_ref, cos_ref, up_ref, dn_ref, do_ref, o_ref, lse_ref,
             dq_ref, dk_ref, dv_ref, dg_ref, qs, ks, vs, delta, dq_s, dk_s, dv_s,
             qp, kp, vp, dop, kw, vw, lse_p, delta_p, dq_p, dk_p, dv_p):
        @pl.when((pl.program_id(0) == 0) & (pl.program_id(1) == 0))
        def _():
            dg_ref[...] = jnp.zeros_like(dg_ref)

        ones = _group_ones()
        masks = _head_masks()
        qhat, rq, khat, rk, qn, kn = _dil_prep(q_ref, k_ref, gq_ref, gk_ref, cos_ref, up_ref, dn_ref, ones)
        qs[...] = qn
        ks[...] = kn
        vs[...] = v_ref[...].astype(F32)
        delta[...] = _groupsum(do_ref[...] * o_ref[...], ones)
        nblk = seq // BAND

        for d in DILATIONS:
            for tok, res in _residues(d, seq):
                qv = qs[tok, :]
                dov = do_ref[tok, :]
                for hd in range(2):
                    qp[hd, res, :] = (qv * masks[hd]).astype(BF16)
                    dop[hd, res, :] = (dov * masks[hd]).astype(BF16)
                kp[res, :] = ks[tok, :].astype(BF16)
                vp[res, :] = vs[tok, :].astype(BF16)
                lse_p[res, :] = lse_ref[tok, :]
                delta_p[res, :] = delta[tok, :]
            keys_k, keys_v, bias = _dil_keys(d, seq, kp, vp, kw, vw)
            nk = keys_k.shape[1]
            dq_b = jnp.zeros((nblk, BAND, LANES), F32)
            dk_b = jnp.zeros((nblk, nk, LANES), F32)
            dv_b = jnp.zeros((nblk, nk, LANES), F32)
            for hd in range(2):
                lane0 = hd * HEAD_DIM
                q3 = qp[hd].reshape(nblk, BAND, LANES)
                do3 = dop[hd].reshape(nblk, BAND, LANES)
                z = _dot(q3, keys_k, BATCH_NT)
                for b_ in bias:
                    z = z + b_
                p = jnp.exp2(z - lse_p[...].reshape(nblk, BAND, LANES)[:, :, lane0:lane0 + 1])
                dp = _dot(do3, keys_v, BATCH_NT)
                ds = (p * (dp - delta_p[...].reshape(nblk, BAND, LANES)[:, :, lane0:lane0 + 1])).astype(BF16)
                dq_b = dq_b + _dot(ds, keys_k, BATCH_NN) * masks[hd]
                dk_b = dk_b + _dot(ds, q3, BATCH_TN)
                dv_b = dv_b + _dot(p.astype(BF16), do3, BATCH_TN)
            dq_p[...] = dq_b.reshape(seq, LANES)
            for acc, out in ((dk_b, dk_p), (dv_b, dv_p)):
                out[...] = acc[:, nk - BAND:, :].reshape(seq, LANES)
                if nk > BAND:
                    out[0:seq - BAND, :] += acc[1:, :BAND, :].reshape(seq - BAND, LANES)
            for tok, res in _residues(d, seq):
                if d == DILATIONS[0]:
                    dq_s[tok, :] = dq_p[res, :]
                    dk_s[tok, :] = dk_p[res, :]
                    dv_s[tok, :] = dv_p[res, :]
                else:
                    dq_s[tok, :] += dq_p[res, :]
                    dk_s[tok, :] += dk_p[res, :]
                    dv_s[tok, :] += dv_p[res, :]

        cos, up, dn = cos_ref[...], up_ref[...], dn_ref[...]
        dq_raw, dgq = _head_norm_bwd(_rope_bwd(dq_s[...] * SCALE, cos, up, dn), qhat, rq, gq_ref[...], ones)
        dk_raw, dgk = _head_norm_bwd(_rope_bwd(dk_s[...] * LN2, cos, up, dn), khat, rk, gk_ref[...], ones)
        dq_ref[...] = dq_raw.astype(BF16)
        dk_ref[...] = dk_raw.astype(BF16)
        dv_ref[...] = dv_s[...].astype(BF16)
        dg_ref[0:1, :] += dgq
        dg_ref[1:2, :] += dgk

    blk = lambda o_: pl.BlockSpec((seq, LANES), lambda b, p: (b, o_ + p))
    vec = pl.BlockSpec((1, LANES), lambda b, p: (0, 0))
    tab = pl.BlockSpec((seq, LANES), lambda b, p: (0, 0))
    f32_buf = pltpu.VMEM((seq, LANES), F32)
    bf16_buf = pltpu.VMEM((seq, LANES), BF16)
    window_buf = pltpu.VMEM((seq // BAND, 2 * BAND, LANES), BF16)
    bf16_pair = pltpu.VMEM((2, seq, LANES), BF16)
    return _host_call(
        body, rider, name="dil_bwd", grid=(nb, npair),
        in_specs=[blk(off), blk(off + npair), blk(off + 2 * npair), vec, vec, tab, tab, tab,
                  blk(0), blk(0), blk(0)],
        out_specs=[blk(0), blk(0), blk(0), pl.BlockSpec((8, LANES), lambda b, p: (0, 0))],
        out_shape=[jax.ShapeDtypeStruct((t, W_GROUP), BF16), jax.ShapeDtypeStruct((t, W_GROUP), BF16),
                   jax.ShapeDtypeStruct((t, W_GROUP), BF16), jax.ShapeDtypeStruct((8, LANES), F32)],
        scratch_shapes=[f32_buf] * 7 + [bf16_pair, bf16_buf, bf16_buf, bf16_pair, window_buf, window_buf]
        + [f32_buf] * 5,
        inputs=(proj, proj, proj, gq, gk, cos, up, dn, do, o, lse), semantics=("arbitrary", "arbitrary"))


def _adamw(w, g, m, v, name, rider=None):
    row_major = w.ndim == 3 and w.shape[1] == 1
    rows, cols = (w.shape[0], w.shape[2]) if row_major else w.shape[-2:]
    if row_major:
        tr = max(t for t in range(1, 65) if rows % t == 0)
    else:
        tr = _row_tile(rows) if rows >= 8 else rows
    c1 = 1.0 - ADAM_B1 ** ADAM_STEP
    c2 = 1.0 - ADAM_B2 ** ADAM_STEP

    def body(w_ref, g_ref, m_ref, v_ref, d_ref, nm_ref, nv_ref):
        g_ = g_ref[...]
        nm = ADAM_B1 * m_ref[...] + (1.0 - ADAM_B1) * g_
        nv = ADAM_B2 * v_ref[...] + (1.0 - ADAM_B2) * (g_ * g_)
        nm_ref[...] = nm
        nv_ref[...] = nv
        d_ref[...] = -ADAM_LR * ((nm / c1) / (jnp.sqrt(nv / c2) + ADAM_EPS) + ADAM_WD * w_ref[...])

    if row_major:
        spec = pl.BlockSpec((tr, 1, cols), lambda i: (i, 0, 0))
    elif w.ndim == 3:
        spec = pl.BlockSpec((1, tr, cols), lambda i: (0, i, 0))
    else:
        spec = pl.BlockSpec((tr, cols), lambda i: (i, 0))
    shape = jax.ShapeDtypeStruct(w.shape, F32)
    return _host_call(
        body, rider, name=name, grid=(rows // tr,), in_specs=[spec] * 4, out_specs=[spec] * 3,
        out_shape=[shape] * 3, scratch_shapes=[], inputs=(w, g, m, v), semantics=("arbitrary",))


def _place():
    x, y, c = lax.axis_index("x"), lax.axis_index("y"), lax.axis_index("c")
    chips = [(1 - x, y), (x, 1 - y), (1 - x, 1 - y)]
    return x, y, c, chips


def _gather_weight(w, name):
    _, rows, cols = w.shape
    half_rows = rows // 2

    def body(w_ref, out_ref, send_sems, recv_sems):
        x, y, c, chips = _place()
        sibling = (x, y, 1 - c)
        mine = 2 * x + y
        lo = pl.multiple_of(c * half_rows, 16)
        lo_sib = pl.multiple_of((1 - c) * half_rows, 16)
        out_ref[mine] = w_ref[0].astype(BF16)

        def copy(k, shard, first_row, to):
            ref = out_ref.at[shard, pl.ds(first_row, half_rows), :]
            return pltpu.make_async_remote_copy(src_ref=ref, dst_ref=ref, send_sem=send_sems.at[k],
                                                recv_sem=recv_sems.at[k], device_id=to, device_id_type=MESH)

        sends = [copy(k, mine, lo, (cx, cy, c)) for k, (cx, cy) in enumerate(chips)]
        for cp in sends:
            cp.start()
        passed = []
        for k, (cx, cy) in enumerate(chips):
            theirs = 2 * cx + cy
            copy(k, theirs, lo, (cx, cy, c)).wait_recv()
            fw = copy(3 + k, theirs, lo, sibling)
            fw.start()
            passed.append(fw)
        for k, (cx, cy) in enumerate(chips):
            copy(3 + k, 2 * cx + cy, lo_sib, sibling).wait_recv()
        for cp in sends + passed:
            cp.wait_send()

    return pl.pallas_call(
        body, name=name,
        in_specs=[pl.BlockSpec(memory_space=pltpu.VMEM)],
        out_specs=pl.BlockSpec(memory_space=pltpu.VMEM),
        out_shape=jax.ShapeDtypeStruct((4, rows, cols), BF16),
        scratch_shapes=[pltpu.SemaphoreType.DMA((6,)), pltpu.SemaphoreType.DMA((6,))],
        compiler_params=pltpu.CompilerParams(vmem_limit_bytes=VMEM_LIMIT),
    )(w)


def _remote(src, dst, sems, k, to):
    send_sems, recv_sems = sems
    return pltpu.make_async_remote_copy(src_ref=src, dst_ref=dst, send_sem=send_sems.at[k], recv_sem=recv_sems.at[k],
                                        device_id=to, device_id_type=MESH)


def _cast_bf16(parts, name):
    def body(*refs):
        for src, dst in zip(refs[:len(parts)], refs[len(parts):]):
            dst[...] = src[0].astype(BF16)

    return pl.pallas_call(
        body, name=name, in_specs=[pl.BlockSpec(memory_space=pltpu.VMEM)] * len(parts),
        out_specs=[pl.BlockSpec(memory_space=pltpu.VMEM)] * len(parts),
        out_shape=[jax.ShapeDtypeStruct(p.shape[1:], BF16) for p in parts],
        compiler_params=pltpu.CompilerParams(vmem_limit_bytes=VMEM_LIMIT),
    )(*parts)


def _gather_rider(shards):
    def copies(ins, outs, sems, which):
        x, y, c, chips = _place()
        sibling = (x, y, 1 - c)
        mine = 2 * x + y
        made = {name: [] for name in which}
        for i, (p_ref, g_ref) in enumerate(zip(ins, outs)):
            half = p_ref.shape[0] // 2
            lo = pl.multiple_of(c * half, 16)
            lo_sib = pl.multiple_of((1 - c) * half, 16)
            spot = lambda shard, first, g_ref=g_ref, half=half: g_ref.at[shard, pl.ds(first, half), :]
            groups = {
                "own": lambda: [pltpu.make_async_copy(p_ref, g_ref.at[mine], sems[0].at[7 * i + 6])],
                "sends": lambda: [_remote(p_ref.at[pl.ds(lo, half), :], spot(mine, lo), sems, 7 * i + k, (cx, cy, c))
                                  for k, (cx, cy) in enumerate(chips)],
                "arrivals": lambda: [_remote(spot(2 * cx + cy, lo), spot(2 * cx + cy, lo), sems, 7 * i + k, (cx, cy, c))
                                     for k, (cx, cy) in enumerate(chips)],
                "passes": lambda: [_remote(spot(2 * cx + cy, lo), spot(2 * cx + cy, lo), sems, 7 * i + 3 + k, sibling)
                                   for k, (cx, cy) in enumerate(chips)],
                "from_sibling": lambda: [_remote(spot(2 * cx + cy, lo_sib), spot(2 * cx + cy, lo_sib), sems,
                                                 7 * i + 3 + k, sibling) for k, (cx, cy) in enumerate(chips)],
            }
            for name in which:
                made[name] += groups[name]()
        return [made[name] for name in which]

    def start(ins, outs, send_sems, recv_sems):
        own, sends = copies(ins, outs, (send_sems, recv_sems), ("own", "sends"))
        for cp in own + sends:
            cp.start()

    def middle(ins, outs, send_sems, recv_sems):
        arrivals, passes = copies(ins, outs, (send_sems, recv_sems), ("arrivals", "passes"))
        for landed, onward in zip(arrivals, passes):
            landed.wait_recv()
            onward.start()

    def finish(ins, outs, send_sems, recv_sems):
        own, sends, passes, from_sibling = copies(ins, outs, (send_sems, recv_sems),
                                                  ("own", "sends", "passes", "from_sibling"))
        for cp in from_sibling:
            cp.wait_recv()
        for cp in sends + passes:
            cp.wait_send()
        for cp in own:
            cp.wait()

    shapes = [jax.ShapeDtypeStruct((4,) + s.shape, BF16) for s in shards]
    return _Rider(shards, shapes, 7 * len(shards), start, finish, middle=middle)


def _exchange_rider(inputs, out_shapes, n_sems, copies, aliases=None):
    def start(ins, outs, send_sems, recv_sems):
        for cp in copies(ins, outs, (send_sems, recv_sems)):
            cp.start()

    def finish(ins, outs, send_sems, recv_sems):
        for cp in copies(ins, outs, (send_sems, recv_sems)):
            cp.wait()

    return _Rider(inputs, out_shapes, n_sems, start, finish, aliases)


def _swap_rider(grads4):
    halves = [g.shape[1] // 2 for g in grads4]

    def copies(ins, outs, sems):
        x, y, c, _ = _place()
        return [_remote(g.at[:, pl.ds(pl.multiple_of((1 - c) * h, 8), h), :], a, sems, i, (x, y, 1 - c))
                for i, (g, a, h) in enumerate(zip(ins, outs, halves))]

    shapes = [jax.ShapeDtypeStruct((4, h, g.shape[2]), F32) for g, h in zip(grads4, halves)]
    return _exchange_rider(grads4, shapes, len(grads4), copies)


def _chip_sum(g4, from_sibling, name):
    _, rows, cols = g4.shape
    half = rows // 2

    def body(g_ref, s_ref, stage_ref, own_ref):
        x, y, c, chips = _place()
        lo = pl.multiple_of(c * half, 8)
        for k, (cx, cy) in enumerate(chips):
            theirs = 2 * cx + cy
            stage_ref[k] = (g_ref[theirs, pl.ds(lo, half), :] + s_ref[theirs]).astype(BF16)
        mine = 2 * x + y
        own_ref[...] = g_ref[mine, pl.ds(lo, half), :] + s_ref[mine]

    return pl.pallas_call(
        body, name=name, in_specs=[pl.BlockSpec(memory_space=pltpu.VMEM)] * 2,
        out_specs=[pl.BlockSpec(memory_space=pltpu.VMEM)] * 2,
        out_shape=[jax.ShapeDtypeStruct((3, half, cols), BF16), jax.ShapeDtypeStruct((half, cols), F32)],
        compiler_params=pltpu.CompilerParams(vmem_limit_bytes=VMEM_LIMIT),
    )(g4, from_sibling)


def _spread_rider(stages):
    def copies(ins, outs, sems):
        _, _, c, chips = _place()
        return [_remote(st.at[k], ld.at[k], sems, 3 * i + k, (cx, cy, c))
                for i, (st, ld) in enumerate(zip(ins, outs)) for k, (cx, cy) in enumerate(chips)]

    shapes = [jax.ShapeDtypeStruct(s.shape, s.dtype) for s in stages]
    return _exchange_rider(stages, shapes, 3 * len(stages), copies)


def _finish_half(own, landed, name):
    half, cols = own.shape

    def body(own_ref, landed_ref, out_ref):
        c = lax.axis_index("c")
        acc = own_ref[...]
        for k in range(3):
            acc = acc + landed_ref[k].astype(F32)
        out_ref[pl.ds(pl.multiple_of(c * half, 8), half), :] = acc

    return pl.pallas_call(
        body, name=name, in_specs=[pl.BlockSpec(memory_space=pltpu.VMEM)] * 2,
        out_specs=pl.BlockSpec(memory_space=pltpu.VMEM),
        out_shape=jax.ShapeDtypeStruct((2 * half, cols), F32),
        compiler_params=pltpu.CompilerParams(vmem_limit_bytes=VMEM_LIMIT),
    )(own, landed)


def _share_rider(fulls):
    def copies(ins, outs, sems):
        x, y, c, _ = _place()
        out = []
        for i, full in enumerate(outs):
            half = full.shape[0] // 2
            rows = full.at[pl.ds(pl.multiple_of(c * half, 8), half), :]
            out.append(_remote(rows, rows, sems, i, (x, y, 1 - c)))
        return out

    def finish_copies(ins, outs, sems):
        x, y, c, _ = _place()
        out = []
        for i, full in enumerate(outs):
            half = full.shape[0] // 2
            mine = full.at[pl.ds(pl.multiple_of(c * half, 8), half), :]
            theirs = full.at[pl.ds(pl.multiple_of((1 - c) * half, 8), half), :]
            out.append((_remote(mine, mine, sems, i, (x, y, 1 - c)), _remote(theirs, theirs, sems, i, (x, y, 1 - c))))
        return out

    def start(ins, outs, send_sems, recv_sems):
        for cp in copies(ins, outs, (send_sems, recv_sems)):
            cp.start()

    def finish(ins, outs, send_sems, recv_sems):
        for sent, landed in finish_copies(ins, outs, (send_sems, recv_sems)):
            sent.wait_send()
            landed.wait_recv()

    shapes = [jax.ShapeDtypeStruct(f.shape, f.dtype) for f in fulls]
    return _Rider(fulls, shapes, len(fulls), start, finish, aliases={i: i for i in range(len(fulls))})


def _all_sum_small(v):
    shape = v.shape

    def body(v_ref, out_ref, buf, send_sems, recv_sems):
        x, y, c, _ = _place()
        me = 4 * x + 2 * y + c
        buf[me] = v_ref[...]
        flips = [(dx, dy, dc) for dx in (0, 1) for dy in (0, 1) for dc in (0, 1)][1:]

        def copy(k, slot, flip):
            dx, dy, dc = flip
            to = (1 - x if dx else x, 1 - y if dy else y, 1 - c if dc else c)
            return pltpu.make_async_remote_copy(src_ref=buf.at[slot], dst_ref=buf.at[slot], send_sem=send_sems.at[k],
                                                recv_sem=recv_sems.at[k], device_id=to, device_id_type=MESH)

        sends = [copy(k, me, flip) for k, flip in enumerate(flips)]
        for cp in sends:
            cp.start()
        for k, (dx, dy, dc) in enumerate(flips):
            sender = 4 * (1 - x if dx else x) + 2 * (1 - y if dy else y) + (1 - c if dc else c)
            copy(k, sender, (dx, dy, dc)).wait_recv()
        for cp in sends:
            cp.wait_send()
        total = buf[0]
        for i in range(1, 8):
            total = total + buf[i]
        out_ref[...] = total

    return pl.pallas_call(
        body, name="all_sum_small",
        in_specs=[pl.BlockSpec(memory_space=pltpu.VMEM)],
        out_specs=pl.BlockSpec(memory_space=pltpu.VMEM),
        out_shape=jax.ShapeDtypeStruct(shape, F32),
        scratch_shapes=[pltpu.VMEM((8,) + shape, F32), pltpu.SemaphoreType.DMA((7,)), pltpu.SemaphoreType.DMA((7,))],
    )(v)


SMALL = (("g_mix", 1024), ("g_ffn", 1024), ("g_out_fox", 512), ("g_out_dil", 512), ("g_q_fox", 64),
         ("g_k_fox", 64), ("g_q_dil", 64), ("g_k_dil", 64), ("b_forget", 8))
SMALL_PACKED = (32, LANES)


def _local_grads(x, target, gains, w1, wft, dense, packed, nb, seq):
    tile2 = lambda g: jnp.tile(g, (1, 2))
    gq_f, gk_f, gq_d, gk_d = (tile2(gains[n]) for n in ("g_q_fox", "g_k_fox", "g_q_dil", "g_k_dil"))
    b_col = gains["b_forget"].reshape(N_FOX_HEADS, 1)
    cos, up, dn = _rope_tables(seq)
    npair = N_FOX_HEADS // 2

    proj, fa_row, h1, h1_t = _in_proj(x, gains["g_mix"], w1, wft)
    c_row = _gate_fwd(fa_row, b_col, seq)
    c3 = c_row.reshape(npair, 2, nb * seq)
    (o_fox, lse_fox), gathered = _fox_fwd(proj, c3, gq_f, gk_f, nb, seq,
                                          rider=None if packed is None else _gather_rider(packed))
    if packed is not None:
        dense = [g.reshape(-1, g.shape[2]) for g in gathered]
    w_out, w_gate, w_up, w_down = dense
    o_dil, lse_dil = _dil_fwd(proj, gq_d, gk_d, cos, up, dn, nb, seq)
    x1, o_n_t = _attn_out(o_fox, o_dil, x, gains["g_out_fox"], gains["g_out_dil"], w_out)
    a, u, dy, loss_parts = _ffn_fwd(x1, target, gains["g_ffn"], w_gate, w_up, w_down)
    loss = jnp.sum(loss_parts[:, 0, 0])

    dx1, s, da, du, h2, dg_ffn = _ffn_bwd(dy, a, u, x1, gains["g_ffn"], w_gate, w_up, w_down)
    d_w_down = _token_matmul(s, dy, "dw_down", 512, False)
    d_w_gate = _token_matmul(da, h2, "dw_gate", 512, False)
    d_w_up = _token_matmul(du, h2, "dw_up", 512, False)
    d_w_out = _token_matmul(o_n_t, dx1, "dw_out", 1024)
    names = ("w_out", "w_gate", "w_up", "w_down")
    grads4 = [g.reshape(4, -1, g.shape[1]) for g in (d_w_out, d_w_gate, d_w_up, d_w_down)]
    exchange = packed is not None
    (do_fox, do_dil, dg_of, dg_od), from_sibling = _attn_out_bwd(
        dx1, o_fox, o_dil, gains["g_out_fox"], gains["g_out_dil"], w_out,
        rider=_swap_rider(grads4) if exchange else None)
    if exchange:
        sums = [_chip_sum(g, s, "chip_sum_" + n) for g, s, n in zip(grads4, from_sibling, names)]
    (dq_f, dk_f, dv_f, dc3, dg_fox), landed = _fox_bwd(
        proj, c3, gq_f, gk_f, do_fox, o_fox, lse_fox, nb, seq,
        rider=_spread_rider([st for st, _ in sums]) if exchange else None)
    if exchange:
        halves = [_finish_half(own, ld, "finish_half_" + n) for (_, own), ld, n in zip(sums, landed, names)]
    (dq_d, dk_d, dv_d, dg_dil), reduced = _dil_bwd(
        proj, gq_d, gk_d, cos, up, dn, do_dil, o_dil, lse_dil, nb, seq,
        rider=_share_rider(halves) if exchange else None)
    if exchange:
        d_w_out, d_w_gate, d_w_up, d_w_down = reduced
    dfa_row, db = _gate_bwd(dc3.reshape(N_FOX_HEADS, nb * seq), fa_row, b_col, seq)
    dparts = [dq_f, dk_f, dv_f, dq_d, dk_d, dv_d]
    d_w1 = _token_matmul_parts(h1_t, dparts, "dw_in")
    d_wf = _row_matmul(dfa_row, h1, "dw_forget")
    fox_w = 3 * W_GROUP
    d_w_in = jnp.concatenate([d_w1[:, :fox_w], d_wf.T, d_w1[:, fox_w:]], axis=1)
    if exchange:
        shards = [_shards_of_columns(d_w_in)]
        _, from_sibling = _idle_host(_swap_rider(shards), "swap_w_in")
        stage, own = _chip_sum(shards[0], from_sibling[0], "chip_sum_w_in")
    (grad_x, dg_mix), landed = _in_proj_bwd(dparts, dfa_row, w1, wft, x, gains["g_mix"], dx1,
                                            rider=_spread_rider([stage]) if exchange else None)
    if exchange:
        d_w_in = _finish_half(own, landed[0], "finish_half_w_in")

    fold = lambda g2: (g2[:, :HEAD_DIM] + g2[:, HEAD_DIM:])
    small = {
        "g_mix": dg_mix[0:1], "g_ffn": dg_ffn[0:1], "g_out_fox": dg_of[0:1], "g_out_dil": dg_od[0:1],
        "g_q_fox": fold(dg_fox[0:1]), "g_k_fox": fold(dg_fox[1:2]),
        "g_q_dil": fold(dg_dil[0:1]), "g_k_dil": fold(dg_dil[1:2]),
        "b_forget": db[:, 0].reshape(1, N_FOX_HEADS),
    }
    big = {"w_in": d_w_in, "w_out": d_w_out, "w_gate": d_w_gate, "w_up": d_w_up, "w_down": d_w_down}
    return loss, grad_x, big, small


def _shards_of_columns(full, n=4):
    r, nc = full.shape
    return full.reshape(r, n, nc // n).transpose(1, 0, 2)


def _columns_of_shards(slabs):
    n, r, c = slabs.shape
    return slabs.transpose(1, 0, 2).reshape(r, n * c)


def kernel(x, g_mix, w_in, b_forget, g_q_fox, g_k_fox, g_q_dil, g_k_dil, g_out_fox, g_out_dil, w_out, g_ffn, w_gate, w_up, w_down, loss_target, m_g_mix, m_w_in, m_b_forget, m_g_q_fox, m_g_k_fox, m_g_q_dil, m_g_k_dil, m_g_out_fox, m_g_out_dil, m_w_out, m_g_ffn, m_w_gate, m_w_up, m_w_down, v_g_mix, v_w_in, v_b_forget, v_g_q_fox, v_g_k_fox, v_g_q_dil, v_g_k_dil, v_g_out_fox, v_g_out_dil, v_w_out, v_g_ffn, v_w_gate, v_w_up, v_w_down):
    nb, seq, d = x.shape
    weights = dict(g_mix=g_mix, w_in=w_in, b_forget=b_forget, g_q_fox=g_q_fox, g_k_fox=g_k_fox, g_q_dil=g_q_dil,
                   g_k_dil=g_k_dil, g_out_fox=g_out_fox, g_out_dil=g_out_dil, w_out=w_out, g_ffn=g_ffn,
                   w_gate=w_gate, w_up=w_up, w_down=w_down)
    m_in = dict(g_mix=m_g_mix, w_in=m_w_in, b_forget=m_b_forget, g_q_fox=m_g_q_fox, g_k_fox=m_g_k_fox,
                g_q_dil=m_g_q_dil, g_k_dil=m_g_k_dil, g_out_fox=m_g_out_fox, g_out_dil=m_g_out_dil, w_out=m_w_out,
                g_ffn=m_g_ffn, w_gate=m_w_gate, w_up=m_w_up, w_down=m_w_down)
    v_in = dict(g_mix=v_g_mix, w_in=v_w_in, b_forget=v_b_forget, g_q_fox=v_g_q_fox, g_k_fox=v_g_k_fox,
                g_q_dil=v_g_q_dil, g_k_dil=v_g_k_dil, g_out_fox=v_g_out_fox, g_out_dil=v_g_out_dil, w_out=v_w_out,
                g_ffn=v_g_ffn, w_gate=v_w_gate, w_up=v_w_up, w_down=v_w_down)
    order = ["g_mix", "w_in", "b_forget", "g_q_fox", "g_k_fox", "g_q_dil", "g_k_dil", "g_out_fox", "g_out_dil",
             "w_out", "g_ffn", "w_gate", "w_up", "w_down"]

    w_in_full = _columns_of_shards(_gather_weight(w_in, "gather_w_in"))
    fox_w = 3 * W_GROUP
    w1 = jnp.concatenate([w_in_full[:, :fox_w], w_in_full[:, fox_w + N_FOX_HEADS:]], axis=1)
    wft = w_in_full[:, fox_w:fox_w + N_FOX_HEADS].T
    swap = lambda a: jnp.transpose(a, (0, 2, 1))
    for n in ("w_gate", "w_up"):
        weights[n], m_in[n], v_in[n] = swap(weights[n]), swap(m_in[n]), swap(v_in[n])
    shards = _cast_bf16([weights[n] for n in ("w_out", "w_gate", "w_up", "w_down")], "cast_shards")

    gains = {n: weights[n] for n, _ in SMALL}
    loss, grad_x, big, small = _local_grads(
        x.reshape(nb * seq, d), loss_target.reshape(nb * seq, d), gains, w1, wft, None, shards, nb, seq)

    grads = {n: big[n][None] for n in ("w_out", "w_gate", "w_up", "w_down")}
    packed = jnp.concatenate([small[n].reshape(-1) for n, _ in SMALL] + [loss.reshape(1)])
    packed = jnp.pad(packed, (0, SMALL_PACKED[0] * SMALL_PACKED[1] - packed.shape[0])).reshape(SMALL_PACKED)
    summed = _all_sum_small(packed).reshape(-1)
    pos = 0
    for n, size in SMALL:
        grads[n] = summed[pos:pos + size].reshape(1, size)
        pos += size
    loss = summed[pos]

    to_entry = lambda a: jnp.transpose(a, (2, 0, 1))
    deltas, new_m, new_v, grad_out = {}, {}, {}, {}
    for n in ["w_down"] + [n for n in order if n != "w_down"]:
        rider = _share_rider([big["w_in"]]) if n == "w_down" else None
        (deltas[n], new_m[n], new_v[n]), shared = _adamw(weights[n], grads[n], m_in[n], v_in[n], "adamw_" + n, rider)
        if rider is not None:
            grads["w_in"] = to_entry(shared[0][None])
            weights["w_in"], m_in["w_in"], v_in["w_in"] = (to_entry(a) for a in (w_in, m_w_in, v_w_in))
        grad_out[n] = grads[n]
    for n in ("w_gate", "w_up"):
        grad_out[n], deltas[n], new_m[n], new_v[n] = (swap(a) for a in (grad_out[n], deltas[n], new_m[n], new_v[n]))
    from_entry = lambda a: jnp.transpose(a, (1, 2, 0))
    grad_out["w_in"], deltas["w_in"], new_m["w_in"], new_v["w_in"] = (
        from_entry(a) for a in (grad_out["w_in"], deltas["w_in"], new_m["w_in"], new_v["w_in"]))

    return (loss, grad_x.reshape(nb, seq, d), *[grad_out[n] for n in order], *[deltas[n] for n in order],
            *[new_m[n] for n in order], *[new_v[n] for n in order])
```

```python
import functools
import math

import numpy as np
import jax
import jax.numpy as jnp
from jax import lax
from jax.experimental import pallas as pl
from jax.experimental.pallas import tpu as pltpu

F32, BF16 = jnp.float32, jnp.bfloat16
MESH = pl.DeviceIdType.MESH

EPS = 1e-6
NEG = -1e30
HEAD_DIM = 64
SCALE = HEAD_DIM ** -0.5
LOG2E = math.log2(math.e)
LN2 = math.log(2.0)
ROPE_THETA = 500000.0
ROPE_DIM = HEAD_DIM // 4
LANES = 128
W_GROUP = 512
N_FOX_HEADS = 8
VMEM_LIMIT = 56 * 1024 * 1024
DILATIONS = (1, 4, 16)
BAND = 128

ADAM_LR, ADAM_B1, ADAM_B2, ADAM_EPS, ADAM_WD, ADAM_STEP = 0.001, 0.9, 0.999, 1e-08, 0.01, 10

NT = (((1,), (1,)), ((), ()))
TN = (((0,), (0,)), ((), ()))
BATCH_NT = (((2,), (2,)), ((0,), (0,)))
BATCH_NN = (((2,), (1,)), ((0,), (0,)))
BATCH_TN = (((1,), (1,)), ((0,), (0,)))


def _params(sem=None):
    return pltpu.CompilerParams(dimension_semantics=sem, vmem_limit_bytes=VMEM_LIMIT)


def _dot(a, b, dims=None):
    if dims is None:
        return jnp.dot(a, b, preferred_element_type=F32)
    return lax.dot_general(a, b, dims, preferred_element_type=F32)


def _group_ones():
    i = lax.broadcasted_iota(jnp.int32, (LANES, LANES), 0) >> 6
    j = lax.broadcasted_iota(jnp.int32, (LANES, LANES), 1) >> 6
    return (i == j).astype(BF16)


def _split3(x):
    a = x.astype(BF16)
    r = x - a.astype(F32)
    b = r.astype(BF16)
    c = (r - b.astype(F32)).astype(BF16)
    return a, b, c


def _groupsum(x, ones, pieces=2):
    total = None
    for _ in range(pieces):
        piece = x.astype(BF16)
        part = _dot(piece, ones)
        total = part if total is None else total + part
        x = x - piece.astype(F32)
    return total


def _head_masks():
    lane = lax.broadcasted_iota(jnp.int32, (1, LANES), 1)
    return [(lane < HEAD_DIM).astype(F32), (lane >= HEAD_DIM).astype(F32)]


def _head_norm(raw, gain, ones):
    r = lax.rsqrt(_groupsum(raw * raw, ones, 1) * (1.0 / HEAD_DIM) + EPS)
    return raw * r, r


def _head_norm_bwd(dy, xhat, r, gain, ones):
    u = dy * gain
    dgain = jnp.sum(dy * xhat, axis=0, keepdims=True)
    draw = r * (u - xhat * (_groupsum(u * xhat, ones) * (1.0 / HEAD_DIM)))
    return draw, dgain


def _rope(x, cos, s_up, s_dn):
    return x * cos + pltpu.roll(x, LANES - 8, 1) * s_up + pltpu.roll(x, 8, 1) * s_dn


def _rope_bwd(dy, cos, s_up, s_dn):
    return dy * cos + pltpu.roll(dy * s_up, 8, 1) + pltpu.roll(dy * s_dn, LANES - 8, 1)


def _rope_tables(seq):
    half = ROPE_DIM // 2
    inv_freq = jnp.power(jnp.float32(ROPE_THETA), -jnp.arange(half, dtype=F32) * 2.0 / ROPE_DIM)
    ang = jnp.arange(seq).astype(F32)[:, None] * inv_freq[None, :]
    cos, sin = jnp.cos(ang), jnp.sin(ang)
    one = jnp.ones((seq, HEAD_DIM - ROPE_DIM), F32)
    zero_h = jnp.zeros((seq, half), F32)
    zero_r = jnp.zeros((seq, HEAD_DIM - ROPE_DIM), F32)
    c = jnp.concatenate([cos, cos, one], axis=1)
    up = jnp.concatenate([-sin, zero_h, zero_r], axis=1)
    dn = jnp.concatenate([zero_h, sin, zero_r], axis=1)
    return jnp.tile(c, (1, 2)), jnp.tile(up, (1, 2)), jnp.tile(dn, (1, 2))


def _row_tile(rows, cap=256):
    best = rows
    for t in range(8, min(rows, cap) + 1, 8):
        if rows % t == 0:
            best = t
    return best


class _Rider:
    def __init__(self, inputs, out_shapes, n_sems, start, finish, aliases=None, middle=None):
        self.inputs, self.out_shapes, self.n_sems = list(inputs), list(out_shapes), n_sems
        self.start, self.finish, self.middle, self.aliases = start, finish, middle, dict(aliases or {})


def _host_call(body, rider, *, name, grid, in_specs, out_specs, out_shape, scratch_shapes, inputs, semantics):
    if rider is None:
        return pl.pallas_call(body, name=name, grid=grid, in_specs=in_specs, out_specs=out_specs,
                              out_shape=out_shape, scratch_shapes=scratch_shapes,
                              compiler_params=_params(semantics))(*inputs), []
    n_in, n_out, n_scr = len(in_specs), len(out_specs), len(scratch_shapes)
    r_in, r_out = len(rider.inputs), len(rider.out_shapes)

    def wrapped(*refs):
        ins, refs = refs[:n_in], refs[n_in:]
        r_ins, refs = refs[:r_in], refs[r_in:]
        outs, refs = refs[:n_out], refs[n_out:]
        r_outs, refs = refs[:r_out], refs[r_out:]
        scratch, (send_sems, recv_sems) = refs[:n_scr], refs[n_scr:]
        ids = [pl.program_id(a) for a in range(len(grid))]
        first = functools.reduce(lambda p, q: p & q, [i == 0 for i in ids])
        last = functools.reduce(lambda p, q: p & q, [i == g - 1 for i, g in zip(ids, grid)])

        @pl.when(first)
        def _():
            rider.start(r_ins, r_outs, send_sems, recv_sems)

        body(*ins, *outs, *scratch)

        if rider.middle is not None:
            step, steps = ids[0], grid[0]
            for i, g in zip(ids[1:], grid[1:]):
                step, steps = step * g + i, steps * g

            @pl.when(step == (3 * steps) // 4)
            def _():
                rider.middle(r_ins, r_outs, send_sems, recv_sems)

        @pl.when(last)
        def _():
            rider.finish(r_ins, r_outs, send_sems, recv_sems)

    hbm = pl.BlockSpec(memory_space=pl.ANY)
    res = pl.pallas_call(
        wrapped, name=name, grid=grid,
        in_specs=list(in_specs) + [hbm] * r_in, out_specs=list(out_specs) + [hbm] * r_out,
        out_shape=list(out_shape) + rider.out_shapes,
        scratch_shapes=list(scratch_shapes) + [pltpu.SemaphoreType.DMA((rider.n_sems,))] * 2,
        input_output_aliases={n_in + i: n_out + o for i, o in rider.aliases.items()},
        compiler_params=_params(semantics),
    )(*inputs, *rider.inputs)
    return res[:n_out], res[n_out:]


def _idle_host(rider, name):
    def body(o_ref):
        o_ref[...] = jnp.zeros_like(o_ref)

    return _host_call(body, rider, name=name, grid=(1,), in_specs=[],
                      out_specs=[pl.BlockSpec((8, LANES), lambda i: (0, 0))],
                      out_shape=[jax.ShapeDtypeStruct((8, LANES), F32)], scratch_shapes=[], inputs=(),
                      semantics=("arbitrary",))


def _in_proj(x, g_mix, w1, wft):
    t, d = x.shape
    n = w1.shape[1]
    tt = 512

    def body(x_ref, g_ref, w_ref, wf_ref, p_ref, fa_ref, h_ref, ht_ref):
        xx = x_ref[...]
        r = lax.rsqrt(jnp.mean(xx * xx, axis=-1, keepdims=True) + EPS)
        h = (xx * r * g_ref[...]).astype(BF16)
        h_ref[...] = h
        ht_ref[...] = h.T
        for j in range(n // W_GROUP):
            cols = slice(j * W_GROUP, (j + 1) * W_GROUP)
            p_ref[:, cols] = _dot(h, w_ref[:, cols]).astype(BF16)
        fa_ref[...] = _dot(wf_ref[...], h, NT)

    return pl.pallas_call(
        body, name="in_proj", grid=(t // tt,),
        in_specs=[pl.BlockSpec((tt, d), lambda i: (i, 0)), pl.BlockSpec((1, d), lambda i: (0, 0)),
                  pl.BlockSpec(memory_space=pltpu.VMEM), pl.BlockSpec(memory_space=pltpu.VMEM)],
        out_specs=[pl.BlockSpec((tt, n), lambda i: (i, 0)), pl.BlockSpec((8, tt), lambda i: (0, i)),
                   pl.BlockSpec((tt, d), lambda i: (i, 0)), pl.BlockSpec((d, tt), lambda i: (0, i))],
        out_shape=[jax.ShapeDtypeStruct((t, n), BF16), jax.ShapeDtypeStruct((8, t), F32),
                   jax.ShapeDtypeStruct((t, d), BF16), jax.ShapeDtypeStruct((d, t), BF16)],
        compiler_params=_params(("arbitrary",)),
    )(x, g_mix, w1, wft)


def _tri(n, upper):
    i = lax.broadcasted_iota(jnp.int32, (n, n), 0)
    j = lax.broadcasted_iota(jnp.int32, (n, n), 1)
    return ((i <= j) if upper else (i >= j)).astype(BF16)


def _gate_fwd(fa_row, b_col, seq):
    t = fa_row.shape[1]
    cb = 256

    def body(fa_ref, b_ref, c_ref):
        tri = _tri(cb, True)
        carry = jnp.zeros((8, 1), F32)
        for k in range(seq // cb):
            z = fa_ref[:, k * cb:(k + 1) * cb] + b_ref[...]
            lf = jnp.minimum(z, 0.0) - jnp.log(1.0 + jnp.exp(-jnp.abs(z)))
            a, b, c = _split3(lf)
            blk = _dot(a, tri) + _dot(b, tri) + _dot(c, tri) + carry
            c_ref[:, k * cb:(k + 1) * cb] = blk
            carry = blk[:, cb - 1:cb]

    return pl.pallas_call(
        body, name="gate_fwd", grid=(t // seq,),
        in_specs=[pl.BlockSpec((8, seq), lambda i: (0, i)), pl.BlockSpec((8, 1), lambda i: (0, 0))],
        out_specs=pl.BlockSpec((8, seq), lambda i: (0, i)),
        out_shape=jax.ShapeDtypeStruct((8, t), F32),
        compiler_params=_params(("arbitrary",)),
    )(fa_row, b_col)


def _gate_bwd(dc_row, fa_row, b_col, seq):
    t = fa_row.shape[1]
    cb = 256

    def body(dc_ref, fa_ref, b_ref, dfa_ref, db_ref):
        @pl.when(pl.program_id(0) == 0)
        def _():
            db_ref[...] = jnp.zeros_like(db_ref)

        tri = _tri(cb, False)
        carry = jnp.zeros((8, 1), F32)
        dbs = jnp.zeros((8, 1), F32)
        for k in reversed(range(seq // cb)):
            a, b, c = _split3(dc_ref[:, k * cb:(k + 1) * cb])
            dlf = _dot(a, tri) + _dot(b, tri) + _dot(c, tri) + carry
            carry = dlf[:, 0:1]
            z = fa_ref[:, k * cb:(k + 1) * cb] + b_ref[...]
            dfa = dlf / (1.0 + jnp.exp(z))
            dfa_ref[:, k * cb:(k + 1) * cb] = dfa
            dbs = dbs + jnp.sum(dfa, axis=1, keepdims=True)
        db_ref[...] += jnp.broadcast_to(dbs, (8, LANES))

    return pl.pallas_call(
        body, name="gate_bwd", grid=(t // seq,),
        in_specs=[pl.BlockSpec((8, seq), lambda i: (0, i)), pl.BlockSpec((8, seq), lambda i: (0, i)),
                  pl.BlockSpec((8, 1), lambda i: (0, 0))],
        out_specs=[pl.BlockSpec((8, seq), lambda i: (0, i)), pl.BlockSpec((8, LANES), lambda i: (0, 0))],
        out_shape=[jax.ShapeDtypeStruct((8, t), F32), jax.ShapeDtypeStruct((8, LANES), F32)],
        compiler_params=_params(("arbitrary",)),
    )(dc_row, fa_row, b_col)


def _attn_out(o_fox, o_dil, x, g_fox, g_dil, w_out):
    t, d = x.shape
    w = o_fox.shape[1]
    tt = 512

    def body(of_ref, od_ref, x_ref, gf_ref, gd_ref, w_ref, x1_ref, ont_ref):
        acc = x_ref[...]
        for k, (o_ref, g_ref) in enumerate(((of_ref, gf_ref), (od_ref, gd_ref))):
            o = o_ref[...]
            r = lax.rsqrt(jnp.mean(o * o, axis=-1, keepdims=True) + EPS)
            on = (o * r * g_ref[...]).astype(BF16)
            ont_ref[k * w:(k + 1) * w, :] = on.T
            acc = acc + _dot(on, w_ref[k * w:(k + 1) * w, :])
        x1_ref[...] = acc

    return pl.pallas_call(
        body, name="attn_out", grid=(t // tt,),
        in_specs=[pl.BlockSpec((tt, w), lambda i: (i, 0)), pl.BlockSpec((tt, w), lambda i: (i, 0)),
                  pl.BlockSpec((tt, d), lambda i: (i, 0)), pl.BlockSpec((1, w), lambda i: (0, 0)),
                  pl.BlockSpec((1, w), lambda i: (0, 0)), pl.BlockSpec(memory_space=pltpu.VMEM)],
        out_specs=[pl.BlockSpec((tt, d), lambda i: (i, 0)), pl.BlockSpec((2 * w, tt), lambda i: (0, i))],
        out_shape=[jax.ShapeDtypeStruct((t, d), F32), jax.ShapeDtypeStruct((2 * w, t), BF16)],
        compiler_params=_params(("arbitrary",)),
    )(o_fox, o_dil, x, g_fox, g_dil, w_out)


def _attn_out_bwd(dx1, o_fox, o_dil, g_fox, g_dil, w_out, rider=None):
    t, d = dx1.shape
    w = o_fox.shape[1]
    tt = 512

    def body(dx_ref, of_ref, od_ref, gf_ref, gd_ref, w_ref, dof_ref, dod_ref, dgf_ref, dgd_ref):
        @pl.when(pl.program_id(0) == 0)
        def _():
            dgf_ref[...] = jnp.zeros_like(dgf_ref)
            dgd_ref[...] = jnp.zeros_like(dgd_ref)

        dxb = dx_ref[...].astype(BF16)
        for k, (o_ref, g_ref, do_ref, dg_ref) in enumerate(
                ((of_ref, gf_ref, dof_ref, dgf_ref), (od_ref, gd_ref, dod_ref, dgd_ref))):
            don = _dot(dxb, w_ref[k * w:(k + 1) * w, :], NT)
            o = o_ref[...]
            r = lax.rsqrt(jnp.mean(o * o, axis=-1, keepdims=True) + EPS)
            xhat = o * r
            u = don * g_ref[...]
            do_ref[...] = r * (u - xhat * jnp.mean(u * xhat, axis=-1, keepdims=True))
            dg_ref[0:1, :] += jnp.sum(don * xhat, axis=0, keepdims=True)

    return _host_call(
        body, rider, name="attn_out_bwd", grid=(t // tt,),
        in_specs=[pl.BlockSpec((tt, d), lambda i: (i, 0)), pl.BlockSpec((tt, w), lambda i: (i, 0)),
                  pl.BlockSpec((tt, w), lambda i: (i, 0)), pl.BlockSpec((1, w), lambda i: (0, 0)),
                  pl.BlockSpec((1, w), lambda i: (0, 0)), pl.BlockSpec(memory_space=pltpu.VMEM)],
        out_specs=[pl.BlockSpec((tt, w), lambda i: (i, 0)), pl.BlockSpec((tt, w), lambda i: (i, 0)),
                   pl.BlockSpec((8, w), lambda i: (0, 0)), pl.BlockSpec((8, w), lambda i: (0, 0))],
        out_shape=[jax.ShapeDtypeStruct((t, w), F32), jax.ShapeDtypeStruct((t, w), F32),
                   jax.ShapeDtypeStruct((8, w), F32), jax.ShapeDtypeStruct((8, w), F32)],
        scratch_shapes=[], inputs=(dx1, o_fox, o_dil, g_fox, g_dil, w_out), semantics=("arbitrary",))


def _ffn_fwd(x1, target, g_ffn, w_gate, w_up, w_down):
    t, d = x1.shape
    f = w_gate.shape[0]
    tt = 256

    def body(x_ref, t_ref, g_ref, wg_ref, wu_ref, wd_ref, a_ref, u_ref, dy_ref, loss_ref):
        xx = x_ref[...]
        r = lax.rsqrt(jnp.mean(xx * xx, axis=-1, keepdims=True) + EPS)
        h = (xx * r * g_ref[...]).astype(BF16)
        a = _dot(h, wg_ref[...], NT)
        u = _dot(h, wu_ref[...], NT)
        a_ref[...] = a.astype(BF16)
        u_ref[...] = u.astype(BF16)
        s = (a / (1.0 + jnp.exp(-a)) * u).astype(BF16)
        y = xx + _dot(s, wd_ref[...])
        e = y - t_ref[...]
        dy_ref[...] = e * (1.0 / d)
        loss_ref[...] = jnp.broadcast_to(0.5 * jnp.sum(e * e) * (1.0 / d), (1, 8, LANES))

    return pl.pallas_call(
        body, name="ffn_fwd", grid=(t // tt,),
        in_specs=[pl.BlockSpec((tt, d), lambda i: (i, 0)), pl.BlockSpec((tt, d), lambda i: (i, 0)),
                  pl.BlockSpec((1, d), lambda i: (0, 0)), pl.BlockSpec(memory_space=pltpu.VMEM),
                  pl.BlockSpec(memory_space=pltpu.VMEM), pl.BlockSpec(memory_space=pltpu.VMEM)],
        out_specs=[pl.BlockSpec((tt, f), lambda i: (i, 0)), pl.BlockSpec((tt, f), lambda i: (i, 0)),
                   pl.BlockSpec((tt, d), lambda i: (i, 0)), pl.BlockSpec((1, 8, LANES), lambda i: (i, 0, 0))],
        out_shape=[jax.ShapeDtypeStruct((t, f), BF16), jax.ShapeDtypeStruct((t, f), BF16),
                   jax.ShapeDtypeStruct((t, d), F32), jax.ShapeDtypeStruct((t // tt, 8, LANES), F32)],
        compiler_params=_params(("arbitrary",)),
    )(x1, target, g_ffn, w_gate, w_up, w_down)


def _ffn_bwd(dy, a, u, x1, g_ffn, w_gate, w_up, w_down):
    t, d = x1.shape
    f = w_gate.shape[0]
    tt = 256

    def body(dy_ref, a_ref, u_ref, x_ref, g_ref, wg_ref, wu_ref, wd_ref,
             dx_ref, s_ref, da_ref, du_ref, h_ref, dg_ref):
        @pl.when(pl.program_id(0) == 0)
        def _():
            dg_ref[...] = jnp.zeros_like(dg_ref)

        dy_ = dy_ref[...]
        ds = _dot(dy_.astype(BF16), wd_ref[...], NT)
        a_ = a_ref[...].astype(F32)
        u_ = u_ref[...].astype(F32)
        sig = 1.0 / (1.0 + jnp.exp(-a_))
        silu = a_ * sig
        s_ref[...] = (silu * u_).astype(BF16)
        da = (ds * u_ * (sig * (1.0 + a_ * (1.0 - sig)))).astype(BF16)
        du = (ds * silu).astype(BF16)
        da_ref[...] = da
        du_ref[...] = du
        dh = _dot(da, wg_ref[...]) + _dot(du, wu_ref[...])
        xx = x_ref[...]
        r = lax.rsqrt(jnp.mean(xx * xx, axis=-1, keepdims=True) + EPS)
        xhat = xx * r
        g = g_ref[...]
        h_ref[...] = (xhat * g).astype(BF16)
        uu = dh * g
        dx_ref[...] = dy_ + r * (uu - xhat * jnp.mean(uu * xhat, axis=-1, keepdims=True))
        dg_ref[0:1, :] += jnp.sum(dh * xhat, axis=0, keepdims=True)

    return pl.pallas_call(
        body, name="ffn_bwd", grid=(t // tt,),
        in_specs=[pl.BlockSpec((tt, d), lambda i: (i, 0)), pl.BlockSpec((tt, f), lambda i: (i, 0)),
                  pl.BlockSpec((tt, f), lambda i: (i, 0)), pl.BlockSpec((tt, d), lambda i: (i, 0)),
                  pl.BlockSpec((1, d), lambda i: (0, 0)), pl.BlockSpec(memory_space=pltpu.VMEM),
                  pl.BlockSpec(memory_space=pltpu.VMEM), pl.BlockSpec(memory_space=pltpu.VMEM)],
        out_specs=[pl.BlockSpec((tt, d), lambda i: (i, 0)), pl.BlockSpec((tt, f), lambda i: (i, 0)),
                   pl.BlockSpec((tt, f), lambda i: (i, 0)), pl.BlockSpec((tt, f), lambda i: (i, 0)),
                   pl.BlockSpec((tt, d), lambda i: (i, 0)), pl.BlockSpec((8, d), lambda i: (0, 0))],
        out_shape=[jax.ShapeDtypeStruct((t, d), F32), jax.ShapeDtypeStruct((t, f), BF16),
                   jax.ShapeDtypeStruct((t, f), BF16), jax.ShapeDtypeStruct((t, f), BF16),
                   jax.ShapeDtypeStruct((t, d), BF16), jax.ShapeDtypeStruct((8, d), F32)],
        compiler_params=_params(("arbitrary",)),
    )(dy, a, u, x1, g_ffn, w_gate, w_up, w_down)


def _in_proj_bwd(dparts, dfa_row, w1, wft, x, g_mix, dx1, rider=None):
    t, d = x.shape
    tt = 512
    npart = len(dparts)

    def body(*refs):
        dp_refs = refs[:npart]
        dfa_ref, w_ref, wf_ref, x_ref, g_ref, dx1_ref, dx_ref, dg_ref = refs[npart:]

        @pl.when(pl.program_id(0) == 0)
        def _():
            dg_ref[...] = jnp.zeros_like(dg_ref)

        dh = _dot(dfa_ref[...].astype(BF16), wf_ref[...], TN)
        for j in range(npart):
            dh = dh + _dot(dp_refs[j][...], w_ref[:, j * W_GROUP:(j + 1) * W_GROUP], NT)
        xx = x_ref[...]
        r = lax.rsqrt(jnp.mean(xx * xx, axis=-1, keepdims=True) + EPS)
        xhat = xx * r
        uu = dh * g_ref[...]
        dx_ref[...] = dx1_ref[...] + r * (uu - xhat * jnp.mean(uu * xhat, axis=-1, keepdims=True))
        dg_ref[0:1, :] += jnp.sum(dh * xhat, axis=0, keepdims=True)

    return _host_call(
        body, rider, name="in_proj_bwd", grid=(t // tt,),
        in_specs=[pl.BlockSpec((tt, W_GROUP), lambda i: (i, 0)) for _ in range(npart)]
        + [pl.BlockSpec((8, tt), lambda i: (0, i)), pl.BlockSpec(memory_space=pltpu.VMEM),
           pl.BlockSpec(memory_space=pltpu.VMEM), pl.BlockSpec((tt, d), lambda i: (i, 0)),
           pl.BlockSpec((1, d), lambda i: (0, 0)), pl.BlockSpec((tt, d), lambda i: (i, 0))],
        out_specs=[pl.BlockSpec((tt, d), lambda i: (i, 0)), pl.BlockSpec((8, d), lambda i: (0, 0))],
        out_shape=[jax.ShapeDtypeStruct((t, d), F32), jax.ShapeDtypeStruct((8, d), F32)],
        scratch_shapes=[], inputs=(*dparts, dfa_row, w1, wft, x, g_mix, dx1), semantics=("arbitrary",))


def _token_matmul(a, b, name, tn, a_is_transposed=True):
    m, t = a.shape if a_is_transposed else a.shape[::-1]
    n = b.shape[1]
    tk = 1024

    def body(a_ref, b_ref, o_ref):
        @pl.when(pl.program_id(1) == 0)
        def _():
            o_ref[...] = jnp.zeros_like(o_ref)

        o_ref[...] += _dot(a_ref[...], b_ref[...].astype(BF16), None if a_is_transposed else TN)

    a_spec = pl.BlockSpec((m, tk), lambda j, k: (0, k)) if a_is_transposed else pl.BlockSpec((tk, m), lambda j, k: (k, 0))
    return pl.pallas_call(
        body, name=name, grid=(n // tn, t // tk),
        in_specs=[a_spec, pl.BlockSpec((tk, tn), lambda j, k: (k, j))],
        out_specs=pl.BlockSpec((m, tn), lambda j, k: (0, j)),
        out_shape=jax.ShapeDtypeStruct((m, n), F32),
        compiler_params=_params(("arbitrary", "arbitrary")),
    )(a, b)


def _token_matmul_parts(at, parts, name):
    m, t = at.shape
    widths = [p.shape[1] for p in parts]
    tk = 1024

    def body(a_ref, *refs):
        o_ref = refs[-1]

        @pl.when(pl.program_id(0) == 0)
        def _():
            o_ref[...] = jnp.zeros_like(o_ref)

        a, first = a_ref[...], 0
        for b_ref, w in zip(refs[:-1], widths):
            o_ref[:, first:first + w] += _dot(a, b_ref[...])
            first += w

    return pl.pallas_call(
        body, name=name, grid=(t // tk,),
        in_specs=[pl.BlockSpec((m, tk), lambda k: (0, k))] + [pl.BlockSpec((tk, w), lambda k: (k, 0)) for w in widths],
        out_specs=pl.BlockSpec((m, sum(widths)), lambda k: (0, 0)),
        out_shape=jax.ShapeDtypeStruct((m, sum(widths)), F32),
        compiler_params=_params(("arbitrary",)),
    )(at, *parts)


def _row_matmul(a_row, b, name):
    t, n = b.shape
    tk = 1024
    nk = t // tk

    def body(a_ref, b_ref, o_ref):
        @pl.when(pl.program_id(0) == 0)
        def _():
            o_ref[...] = jnp.zeros_like(o_ref)

        o_ref[...] += _dot(a_ref[...].astype(BF16), b_ref[...])

    return pl.pallas_call(
        body, name=name, grid=(nk,),
        in_specs=[pl.BlockSpec((8, tk), lambda k: (0, k)), pl.BlockSpec((tk, n), lambda k: (k, 0))],
        out_specs=pl.BlockSpec((8, n), lambda k: (0, 0)),
        out_shape=jax.ShapeDtypeStruct((8, n), F32),
        compiler_params=_params(("arbitrary",)),
    )(a_row, b)


FOX_TQ = 256
SUM_LANE = (HEAD_DIM, 0)


def _fox_fwd(proj, c3, gq, gk, nb, seq, rider=None):
    t = nb * seq
    tq = FOX_TQ
    nq = seq // tq
    npair = N_FOX_HEADS // 2

    def body(q_ref, k_ref, v_ref, c_ref, gq_ref, gk_ref, o_ref, lse_ref, qs, ks, vs):
        ones = _group_ones()
        masks = _head_masks()
        qhat, _ = _head_norm(q_ref[...].astype(F32), None, ones)
        khat, _ = _head_norm(k_ref[...].astype(F32), None, ones)
        qs[...] = (qhat * gq_ref[...] * (SCALE * LOG2E)).astype(BF16)
        kn = khat * gk_ref[...]
        vv = v_ref[...].astype(F32)
        lane = lax.broadcasted_iota(jnp.int32, (1, LANES), 1)
        for hd in range(2):
            ks[hd] = (kn * masks[hd]).astype(BF16)
            vs[hd] = (vv * masks[hd] + (lane == SUM_LANE[hd]).astype(F32)).astype(BF16)
        row = lax.broadcasted_iota(jnp.int32, (tq, tq), 0)
        col = lax.broadcasted_iota(jnp.int32, (tq, tq), 1)
        causal = col <= row

        for qi in range(nq):
            q0 = qi * tq
            q_blk = qs[q0:q0 + tq, :]
            o_tot = jnp.zeros((tq, LANES), F32)
            lse_tot = jnp.zeros((tq, LANES), F32)
            for hd in range(2):
                crow = c_ref[0, hd:hd + 1, 0:q0 + tq] * LOG2E
                c0 = crow[:, q0:q0 + 1]
                s_d = _dot(q_blk, ks[hd, q0:q0 + tq, :], NT) + (c0 - crow[:, q0:q0 + tq])
                s_d = jnp.where(causal, s_d, NEG)
                m = jnp.max(s_d, axis=-1, keepdims=True)
                if qi > 0:
                    s_o = _dot(q_blk, ks[hd, 0:q0, :], NT) + (c0 - crow[:, 0:q0])
                    m = jnp.maximum(m, jnp.max(s_o, axis=-1, keepdims=True))
                acc = _dot(jnp.exp2(s_d - m).astype(BF16), vs[hd, q0:q0 + tq, :])
                if qi > 0:
                    acc = acc + _dot(jnp.exp2(s_o - m).astype(BF16), vs[hd, 0:q0, :])
                l = acc[:, SUM_LANE[hd]:SUM_LANE[hd] + 1]
                o_tot = o_tot + (acc / l) * masks[hd]
                lse_tot = lse_tot + (m + jnp.log2(l) - c0) * masks[hd]
            o_ref[q0:q0 + tq, :] = o_tot
            lse_ref[q0:q0 + tq, :] = lse_tot

    blk = lambda off: pl.BlockSpec((seq, LANES), lambda b, p: (b, off + p))
    return _host_call(
        body, rider, name="fox_fwd", grid=(nb, npair),
        in_specs=[blk(0), blk(npair), blk(2 * npair), pl.BlockSpec((1, 2, seq), lambda b, p: (p, 0, b)),
                  pl.BlockSpec((1, LANES), lambda b, p: (0, 0)), pl.BlockSpec((1, LANES), lambda b, p: (0, 0))],
        out_specs=[blk(0), blk(0)],
        out_shape=[jax.ShapeDtypeStruct((t, W_GROUP), F32), jax.ShapeDtypeStruct((t, W_GROUP), F32)],
        scratch_shapes=[pltpu.VMEM((seq, LANES), BF16), pltpu.VMEM((2, seq, LANES), BF16),
                        pltpu.VMEM((2, seq, LANES), BF16)],
        inputs=(proj, proj, proj, c3, gq, gk), semantics=("arbitrary", "arbitrary"))


def _fox_bwd(proj, c3, gq, gk, do, o, lse, nb, seq, rider=None):
    t = nb * seq
    tq = FOX_TQ
    nq = seq // tq
    npair = N_FOX_HEADS // 2

    def body(q_ref, k_ref, v_ref, c_ref, gq_ref, gk_ref, do_ref, o_ref, lse_ref,
             dq_ref, dk_ref, dv_ref, dc_ref, dg_ref, qs, ks, vs, kts, dos, lse_t, delta_t, dqt_acc, dk_acc, dv_acc,
             row_sum):
        @pl.when((pl.program_id(0) == 0) & (pl.program_id(1) == 0))
        def _():
            dg_ref[...] = jnp.zeros_like(dg_ref)

        ones = _group_ones()
        masks = _head_masks()
        qhat, rq = _head_norm(q_ref[...].astype(F32), None, ones)
        khat, rk = _head_norm(k_ref[...].astype(F32), None, ones)
        qs[...] = (qhat * gq_ref[...] * (SCALE * LOG2E)).astype(BF16)
        kn = khat * gk_ref[...]
        vv = v_ref[...].astype(F32)
        for hd in range(2):
            ks[hd] = (kn * masks[hd]).astype(BF16)
            vs[hd] = (vv * masks[hd]).astype(BF16)
            kts[hd] = ks[hd].T
        dof = do_ref[...]
        dos[...] = dof.astype(BF16)
        lse_t[...] = lse_ref[...].T
        delta_t[...] = _groupsum(dof * o_ref[...], ones).T
        dqt_acc[...] = jnp.zeros_like(dqt_acc)
        dk_acc[...] = jnp.zeros_like(dk_acc)
        dv_acc[...] = jnp.zeros_like(dv_acc)
        row_sum[...] = jnp.zeros_like(row_sum)
        key = lax.broadcasted_iota(jnp.int32, (tq, tq), 0)
        qry = lax.broadcasted_iota(jnp.int32, (tq, tq), 1)
        causal = key <= qry

        for hd in range(2):
            lane0 = hd * HEAD_DIM
            for kj in range(nq):
                k0 = kj * tq
                k_blk = ks[hd, k0:k0 + tq, :]
                v_blk = vs[hd, k0:k0 + tq, :]
                kt_blk = kts[hd, :, k0:k0 + tq]
                crow = c_ref[0, hd:hd + 1, k0:k0 + tq] * LOG2E
                ck0 = crow[:, 0:1]
                bias = jnp.broadcast_to(ck0 - crow, (LANES, tq)).T[:, 0:1]

                def queries_step(r0, r1, diag, hd=hd, lane0=lane0, k_blk=k_blk, v_blk=v_blk, kt_blk=kt_blk,
                                 bias=bias, ck0=ck0):
                    q_r = qs[r0:r1, :]
                    do_r = dos[r0:r1, :]
                    z = _dot(k_blk, q_r, NT) + bias
                    p = jnp.exp2(z - (lse_t[lane0:lane0 + 1, r0:r1] + ck0))
                    if diag:
                        p = jnp.where(causal, p, 0.0)
                    dp = _dot(v_blk, do_r, NT)
                    ds = p * (dp - delta_t[lane0:lane0 + 1, r0:r1])
                    dsb = ds.astype(BF16)
                    dqt_acc[:, r0:r1] += _dot(kt_blk, dsb)
                    row_sum[hd:hd + 1, r0:r1] += jnp.sum(ds, axis=0, keepdims=True)
                    return _dot(dsb, q_r), _dot(p.astype(BF16), do_r), -jnp.sum(ds, axis=1, keepdims=True)

                dk_j, dv_j, dc_j = queries_step(k0, k0 + tq, True)
                if k0 + tq < seq:
                    dk_o, dv_o, dc_o = queries_step(k0 + tq, seq, False)
                    dk_j, dv_j, dc_j = dk_j + dk_o, dv_j + dv_o, dc_j + dc_o
                dk_acc[k0:k0 + tq, :] += dk_j * masks[hd]
                dv_acc[k0:k0 + tq, :] += dv_j * masks[hd]
                dc_ref[0, hd:hd + 1, k0:k0 + tq] = jnp.broadcast_to(dc_j, (tq, LANES)).T[0:1, :]

        dc_ref[0] += row_sum[0:2, :]

        dq_raw, dgq = _head_norm_bwd(dqt_acc[...].T * SCALE, qhat, rq, gq_ref[...], ones)
        dk_raw, dgk = _head_norm_bwd(dk_acc[...] * LN2, khat, rk, gk_ref[...], ones)
        dq_ref[...] = dq_raw.astype(BF16)
        dk_ref[...] = dk_raw.astype(BF16)
        dv_ref[...] = dv_acc[...].astype(BF16)
        dg_ref[0:1, :] += dgq
        dg_ref[1:2, :] += dgk

    blk = lambda off: pl.BlockSpec((seq, LANES), lambda b, p: (b, off + p))
    vec = pl.BlockSpec((1, LANES), lambda b, p: (0, 0))
    c_spec = pl.BlockSpec((1, 2, seq), lambda b, p: (p, 0, b))
    return _host_call(
        body, rider, name="fox_bwd", grid=(nb, npair),
        in_specs=[blk(0), blk(npair), blk(2 * npair), c_spec, vec, vec, blk(0), blk(0), blk(0)],
        out_specs=[blk(0), blk(0), blk(0), c_spec, pl.BlockSpec((8, LANES), lambda b, p: (0, 0))],
        out_shape=[jax.ShapeDtypeStruct((t, W_GROUP), BF16), jax.ShapeDtypeStruct((t, W_GROUP), BF16),
                   jax.ShapeDtypeStruct((t, W_GROUP), BF16), jax.ShapeDtypeStruct((npair, 2, t), F32),
                   jax.ShapeDtypeStruct((8, LANES), F32)],
        scratch_shapes=[pltpu.VMEM((seq, LANES), BF16), pltpu.VMEM((2, seq, LANES), BF16),
                        pltpu.VMEM((2, seq, LANES), BF16), pltpu.VMEM((2, LANES, seq), BF16),
                        pltpu.VMEM((seq, LANES), BF16), pltpu.VMEM((LANES, seq), F32),
                        pltpu.VMEM((LANES, seq), F32), pltpu.VMEM((LANES, seq), F32),
                        pltpu.VMEM((seq, LANES), F32), pltpu.VMEM((seq, LANES), F32),
                        pltpu.VMEM((8, seq), F32)],
        inputs=(proj, proj, proj, c3, gq, gk, do, o, lse), semantics=("arbitrary", "arbitrary"))


def _dil_prep(q_ref, k_ref, gq_ref, gk_ref, cos_ref, up_ref, dn_ref, ones):
    qhat, rq = _head_norm(q_ref[...].astype(F32), None, ones)
    khat, rk = _head_norm(k_ref[...].astype(F32), None, ones)
    cos, up, dn = cos_ref[...], up_ref[...], dn_ref[...]
    qn = _rope(qhat * gq_ref[...], cos, up, dn) * (SCALE * LOG2E)
    kn = _rope(khat * gk_ref[...], cos, up, dn)
    return qhat, rq, khat, rk, qn, kn


def _dil_keys(d, seq, pairs):
    nblk = seq // BAND
    per_res = seq // (d * BAND)
    as_blocks = lambda ref, rows: ref[rows, :].reshape(-1, BAND, LANES)
    if per_res == 1:
        a = lax.broadcasted_iota(jnp.int32, (1, BAND, BAND), 1)
        j = lax.broadcasted_iota(jnp.int32, (1, BAND, BAND), 2)
        causal = jnp.where(j <= a, 0.0, NEG)
        return [as_blocks(src, slice(0, seq)) for src, _ in pairs], [causal]
    for src, dst in pairs:
        dst[:, BAND:, :] = as_blocks(src, slice(0, seq))
        dst[1:, :BAND, :] = as_blocks(src, slice(0, seq - BAND))
        dst[0:1, :BAND, :] = jnp.zeros((1, BAND, LANES), BF16)
    a = lax.broadcasted_iota(jnp.int32, (1, BAND, 2 * BAND), 1)
    j = lax.broadcasted_iota(jnp.int32, (1, BAND, 2 * BAND), 2)
    band = jnp.where(((j < BAND) & (j >= a)) | ((j >= BAND) & (j - BAND <= a)), 0.0, NEG)
    e = lax.broadcasted_iota(jnp.int32, (nblk, 1, 2 * BAND), 0)
    j = lax.broadcasted_iota(jnp.int32, (nblk, 1, 2 * BAND), 2)
    no_prev = jnp.where(((e & (per_res - 1)) == 0) & (j < BAND), NEG, 0.0)
    return [dst[...] for _, dst in pairs], [band + no_prev]


def _residues(d, seq):
    n = seq // d
    if d == 1:
        return [(slice(0, seq), slice(0, seq))]
    return [(pl.ds(r, n, stride=d), slice(r * n, (r + 1) * n)) for r in range(d)]


def _dil_fwd(proj, gq, gk, cos, up, dn, nb, seq):
    t = nb * seq
    npair = W_GROUP // LANES
    off = 3 * npair

    def body(q_ref, k_ref, v_ref, gq_ref, gk_ref, cos_ref, up_ref, dn_ref, o_ref, lse_ref,
             qs, ks, vs, qp, kp, vp, kw, vw, m_b, l_b, o_b, m_s, l_s, o_s):
        ones = _group_ones()
        masks = _head_masks()
        _, _, _, _, qn, kn = _dil_prep(q_ref, k_ref, gq_ref, gk_ref, cos_ref, up_ref, dn_ref, ones)
        qs[...] = qn
        ks[...] = kn
        vs[...] = v_ref[...].astype(F32)
        nblk = seq // BAND
        lane = lax.broadcasted_iota(jnp.int32, (1, LANES), 1)

        for d in DILATIONS:
            for tok, res in _residues(d, seq):
                qv = qs[tok, :]
                vv = vs[tok, :]
                for hd in range(2):
                    qp[hd, res, :] = (qv * masks[hd]).astype(BF16)
                    vp[hd, res, :] = (vv * masks[hd] + (lane == SUM_LANE[hd]).astype(F32)).astype(BF16)
                kp[res, :] = ks[tok, :].astype(BF16)
            (keys_k, *keys_v), bias = _dil_keys(d, seq, [(kp, kw), (vp.at[0], vw.at[0]), (vp.at[1], vw.at[1])])
            m_t = jnp.zeros((nblk, BAND, LANES), F32)
            l_t = jnp.zeros((nblk, BAND, LANES), F32)
            o_t = jnp.zeros((nblk, BAND, LANES), F32)
            for hd in range(2):
                s = _dot(qp[hd].reshape(nblk, BAND, LANES), keys_k, BATCH_NT)
                for b_ in bias:
                    s = s + b_
                m = jnp.max(s, axis=-1, keepdims=True)
                acc = _dot(jnp.exp2(s - m).astype(BF16), keys_v[hd], BATCH_NN)
                m_t = m_t + m * masks[hd]
                l_t = l_t + acc[:, :, SUM_LANE[hd]:SUM_LANE[hd] + 1] * masks[hd]
                o_t = o_t + acc * masks[hd]
            m_b[...] = m_t.reshape(seq, LANES)
            l_b[...] = l_t.reshape(seq, LANES)
            o_b[...] = o_t.reshape(seq, LANES)
            for tok, res in _residues(d, seq):
                if d == DILATIONS[0]:
                    m_s[tok, :] = m_b[res, :]
                    l_s[tok, :] = l_b[res, :]
                    o_s[tok, :] = o_b[res, :]
                else:
                    m_old = m_s[tok, :]
                    m_new = jnp.maximum(m_old, m_b[res, :])
                    w_old = jnp.exp2(m_old - m_new)
                    w_new = jnp.exp2(m_b[res, :] - m_new)
                    l_s[tok, :] = l_s[tok, :] * w_old + l_b[res, :] * w_new
                    o_s[tok, :] = o_s[tok, :] * w_old + o_b[res, :] * w_new
                    m_s[tok, :] = m_new

        l = l_s[...]
        o_ref[...] = o_s[...] / l
        lse_ref[...] = m_s[...] + jnp.log2(l)

    blk = lambda o_: pl.BlockSpec((seq, LANES), lambda b, p: (b, o_ + p))
    vec = pl.BlockSpec((1, LANES), lambda b, p: (0, 0))
    tab = pl.BlockSpec((seq, LANES), lambda b, p: (0, 0))
    f32_buf = pltpu.VMEM((seq, LANES), F32)
    bf16_buf = pltpu.VMEM((seq, LANES), BF16)
    window_buf = pltpu.VMEM((seq // BAND, 2 * BAND, LANES), BF16)
    return pl.pallas_call(
        body, name="dil_fwd", grid=(nb, npair),
        in_specs=[blk(off), blk(off + npair), blk(off + 2 * npair), vec, vec, tab, tab, tab],
        out_specs=[blk(0), blk(0)],
        out_shape=[jax.ShapeDtypeStruct((t, W_GROUP), F32), jax.ShapeDtypeStruct((t, W_GROUP), F32)],
        scratch_shapes=[f32_buf, f32_buf, f32_buf, pltpu.VMEM((2, seq, LANES), BF16), bf16_buf,
                        pltpu.VMEM((2, seq, LANES), BF16), window_buf,
                        pltpu.VMEM((2, seq // BAND, 2 * BAND, LANES), BF16),
                        f32_buf, f32_buf, f32_buf, f32_buf, f32_buf, f32_buf],
        compiler_params=_params(("arbitrary", "arbitrary")),
    )(proj, proj, proj, gq, gk, cos, up, dn)


def _dil_bwd(proj, gq, gk, cos, up, dn, do, o, lse, nb, seq, rider=None):
    t = nb * seq
    npair = W_GROUP // LANES
    off = 3 * npair

    def body(q_ref, k_ref, v_ref, gq_ref, gk_ref, cos_ref, up_ref, dn_ref, do_ref, o_ref, lse_ref,
             dq_ref, dk_ref, dv_ref, dg_ref, qs, ks, vs, delta, dq_s, dk_s, dv_s,
             qp, kp, vp, dop, kw, vw, lse_p, delta_p, dq_p, dk_p, dv_p):
        @pl.when((pl.program_id(0) == 0) & (pl.program_id(1) == 0))
        def _():
            dg_ref[...] = jnp.zeros_like(dg_ref)

        ones = _group_ones()
        masks = _head_masks()
        qhat, rq, khat, rk, qn, kn = _dil_prep(q_ref, k_ref, gq_ref, gk_ref, cos_ref, up_ref, dn_ref, ones)
        qs[...] = qn
        ks[...] = kn
        vs[...] = v_ref[...].astype(F32)
        delta[...] = _groupsum(do_ref[...] * o_ref[...], ones)
        nblk = seq // BAND

        for d in DILATIONS:
            for tok, res in _residues(d, seq):
                qv = qs[tok, :]
                dov = do_ref[tok, :]
                for hd in range(2):
                    qp[hd, res, :] = (qv * masks[hd]).astype(BF16)
                    dop[hd, res, :] = (dov * masks[hd]).astype(BF16)
                kp[res, :] = ks[tok, :].astype(BF16)
                vp[res, :] = vs[tok, :].astype(BF16)
                lse_p[res, :] = lse_ref[tok, :]
                delta_p[res, :] = delta[tok, :]
            (keys_k, keys_v), bias = _dil_keys(d, seq, [(kp, kw), (vp, vw)])
            nk = keys_k.shape[1]
            dq_b = jnp.zeros((nblk, BAND, LANES), F32)
            dk_b = jnp.zeros((nblk, nk, LANES), F32)
            dv_b = jnp.zeros((nblk, nk, LANES), F32)
            for hd in range(2):
                lane0 = hd * HEAD_DIM
                q3 = qp[hd].reshape(nblk, BAND, LANES)
                do3 = dop[hd].reshape(nblk, BAND, LANES)
                z = _dot(q3, keys_k, BATCH_NT)
                for b_ in bias:
                    z = z + b_
                p = jnp.exp2(z - lse_p[...].reshape(nblk, BAND, LANES)[:, :, lane0:lane0 + 1])
                dp = _dot(do3, keys_v, BATCH_NT)
                ds = (p * (dp - delta_p[...].reshape(nblk, BAND, LANES)[:, :, lane0:lane0 + 1])).astype(BF16)
                dq_b = dq_b + _dot(ds, keys_k, BATCH_NN) * masks[hd]
                dk_b = dk_b + _dot(ds, q3, BATCH_TN)
                dv_b = dv_b + _dot(p.astype(BF16), do3, BATCH_TN)
            dq_p[...] = dq_b.reshape(seq, LANES)
            for acc, out in ((dk_b, dk_p), (dv_b, dv_p)):
                out[...] = acc[:, nk - BAND:, :].reshape(seq, LANES)
                if nk > BAND:
                    out[0:seq - BAND, :] += acc[1:, :BAND, :].reshape(seq - BAND, LANES)
            for tok, res in _residues(d, seq):
                if d == DILATIONS[0]:
                    dq_s[tok, :] = dq_p[res, :]
                    dk_s[tok, :] = dk_p[res, :]
                    dv_s[tok, :] = dv_p[res, :]
                else:
                    dq_s[tok, :] += dq_p[res, :]
                    dk_s[tok, :] += dk_p[res, :]
                    dv_s[tok, :] += dv_p[res, :]

        cos, up, dn = cos_ref[...], up_ref[...], dn_ref[...]
        dq_raw, dgq = _head_norm_bwd(_rope_bwd(dq_s[...] * SCALE, cos, up, dn), qhat, rq, gq_ref[...], ones)
        dk_raw, dgk = _head_norm_bwd(_rope_bwd(dk_s[...] * LN2, cos, up, dn), khat, rk, gk_ref[...], ones)
        dq_ref[...] = dq_raw.astype(BF16)
        dk_ref[...] = dk_raw.astype(BF16)
        dv_ref[...] = dv_s[...].astype(BF16)
        dg_ref[0:1, :] += dgq
        dg_ref[1:2, :] += dgk

    blk = lambda o_: pl.BlockSpec((seq, LANES), lambda b, p: (b, o_ + p))
    vec = pl.BlockSpec((1, LANES), lambda b, p: (0, 0))
    tab = pl.BlockSpec((seq, LANES), lambda b, p: (0, 0))
    f32_buf = pltpu.VMEM((seq, LANES), F32)
    bf16_buf = pltpu.VMEM((seq, LANES), BF16)
    window_buf = pltpu.VMEM((seq // BAND, 2 * BAND, LANES), BF16)
    bf16_pair = pltpu.VMEM((2, seq, LANES), BF16)
    return _host_call(
        body, rider, name="dil_bwd", grid=(nb, npair),
        in_specs=[blk(off), blk(off + npair), blk(off + 2 * npair), vec, vec, tab, tab, tab,
                  blk(0), blk(0), blk(0)],
        out_specs=[blk(0), blk(0), blk(0), pl.BlockSpec((8, LANES), lambda b, p: (0, 0))],
        out_shape=[jax.ShapeDtypeStruct((t, W_GROUP), BF16), jax.ShapeDtypeStruct((t, W_GROUP), BF16),
                   jax.ShapeDtypeStruct((t, W_GROUP), BF16), jax.ShapeDtypeStruct((8, LANES), F32)],
        scratch_shapes=[f32_buf] * 7 + [bf16_pair, bf16_buf, bf16_buf, bf16_pair, window_buf, window_buf]
        + [f32_buf] * 5,
        inputs=(proj, proj, proj, gq, gk, cos, up, dn, do, o, lse), semantics=("arbitrary", "arbitrary"))


def _adamw(w, g, m, v, name, rider=None):
    row_major = w.ndim == 3 and w.shape[1] == 1
    rows, cols = (w.shape[0], w.shape[2]) if row_major else w.shape[-2:]
    if row_major:
        tr = max(t for t in range(1, 65) if rows % t == 0)
    else:
        tr = _row_tile(rows) if rows >= 8 else rows
    c1 = 1.0 - ADAM_B1 ** ADAM_STEP
    c2 = 1.0 - ADAM_B2 ** ADAM_STEP

    def body(w_ref, g_ref, m_ref, v_ref, d_ref, nm_ref, nv_ref):
        g_ = g_ref[...]
        nm = ADAM_B1 * m_ref[...] + (1.0 - ADAM_B1) * g_
        nv = ADAM_B2 * v_ref[...] + (1.0 - ADAM_B2) * (g_ * g_)
        nm_ref[...] = nm
        nv_ref[...] = nv
        d_ref[...] = -ADAM_LR * ((nm / c1) / (jnp.sqrt(nv / c2) + ADAM_EPS) + ADAM_WD * w_ref[...])

    if row_major:
        spec = pl.BlockSpec((tr, 1, cols), lambda i: (i, 0, 0))
    elif w.ndim == 3:
        spec = pl.BlockSpec((1, tr, cols), lambda i: (0, i, 0))
    else:
        spec = pl.BlockSpec((tr, cols), lambda i: (i, 0))
    shape = jax.ShapeDtypeStruct(w.shape, F32)
    return _host_call(
        body, rider, name=name, grid=(rows // tr,), in_specs=[spec] * 4, out_specs=[spec] * 3,
        out_shape=[shape] * 3, scratch_shapes=[], inputs=(w, g, m, v), semantics=("arbitrary",))


def _place():
    x, y, c = lax.axis_index("x"), lax.axis_index("y"), lax.axis_index("c")
    chips = [(1 - x, y), (x, 1 - y), (1 - x, 1 - y)]
    return x, y, c, chips


def _gather_weight(w, name):
    _, rows, cols = w.shape
    half_rows = rows // 2

    def body(w_ref, out_ref, send_sems, recv_sems):
        x, y, c, chips = _place()
        sibling = (x, y, 1 - c)
        mine = 2 * x + y
        lo = pl.multiple_of(c * half_rows, 16)
        lo_sib = pl.multiple_of((1 - c) * half_rows, 16)
        out_ref[mine] = w_ref[0].astype(BF16)

        def copy(k, shard, first_row, to):
            ref = out_ref.at[shard, pl.ds(first_row, half_rows), :]
            return pltpu.make_async_remote_copy(src_ref=ref, dst_ref=ref, send_sem=send_sems.at[k],
                                                recv_sem=recv_sems.at[k], device_id=to, device_id_type=MESH)

        sends = [copy(k, mine, lo, (cx, cy, c)) for k, (cx, cy) in enumerate(chips)]
        for cp in sends:
            cp.start()
        passed = []
        for k, (cx, cy) in enumerate(chips):
            theirs = 2 * cx + cy
            copy(k, theirs, lo, (cx, cy, c)).wait_recv()
            fw = copy(3 + k, theirs, lo, sibling)
            fw.start()
            passed.append(fw)
        for k, (cx, cy) in enumerate(chips):
            copy(3 + k, 2 * cx + cy, lo_sib, sibling).wait_recv()
        for cp in sends + passed:
            cp.wait_send()

    return pl.pallas_call(
        body, name=name,
        in_specs=[pl.BlockSpec(memory_space=pltpu.VMEM)],
        out_specs=pl.BlockSpec(memory_space=pltpu.VMEM),
        out_shape=jax.ShapeDtypeStruct((4, rows, cols), BF16),
        scratch_shapes=[pltpu.SemaphoreType.DMA((6,)), pltpu.SemaphoreType.DMA((6,))],
        compiler_params=pltpu.CompilerParams(vmem_limit_bytes=VMEM_LIMIT),
    )(w)


def _remote(src, dst, sems, k, to):
    send_sems, recv_sems = sems
    return pltpu.make_async_remote_copy(src_ref=src, dst_ref=dst, send_sem=send_sems.at[k], recv_sem=recv_sems.at[k],
                                        device_id=to, device_id_type=MESH)


def _cast_bf16(parts, name):
    def body(*refs):
        for src, dst in zip(refs[:len(parts)], refs[len(parts):]):
            dst[...] = src[0].astype(BF16)

    return pl.pallas_call(
        body, name=name, in_specs=[pl.BlockSpec(memory_space=pltpu.VMEM)] * len(parts),
        out_specs=[pl.BlockSpec(memory_space=pltpu.VMEM)] * len(parts),
        out_shape=[jax.ShapeDtypeStruct(p.shape[1:], BF16) for p in parts],
        compiler_params=pltpu.CompilerParams(vmem_limit_bytes=VMEM_LIMIT),
    )(*parts)


def _gather_rider(shards):
    def copies(ins, outs, sems, which):
        x, y, c, chips = _place()
        sibling = (x, y, 1 - c)
        mine = 2 * x + y
        made = {name: [] for name in which}
        for i, (p_ref, g_ref) in enumerate(zip(ins, outs)):
            half = p_ref.shape[0] // 2
            lo = pl.multiple_of(c * half, 16)
            lo_sib = pl.multiple_of((1 - c) * half, 16)
            spot = lambda shard, first, g_ref=g_ref, half=half: g_ref.at[shard, pl.ds(first, half), :]
            groups = {
                "own": lambda: [pltpu.make_async_copy(p_ref, g_ref.at[mine], sems[0].at[7 * i + 6])],
                "sends": lambda: [_remote(p_ref.at[pl.ds(lo, half), :], spot(mine, lo), sems, 7 * i + k, (cx, cy, c))
                                  for k, (cx, cy) in enumerate(chips)],
                "arrivals": lambda: [_remote(spot(2 * cx + cy, lo), spot(2 * cx + cy, lo), sems, 7 * i + k, (cx, cy, c))
                                     for k, (cx, cy) in enumerate(chips)],
                "passes": lambda: [_remote(spot(2 * cx + cy, lo), spot(2 * cx + cy, lo), sems, 7 * i + 3 + k, sibling)
                                   for k, (cx, cy) in enumerate(chips)],
                "from_sibling": lambda: [_remote(spot(2 * cx + cy, lo_sib), spot(2 * cx + cy, lo_sib), sems,
                                                 7 * i + 3 + k, sibling) for k, (cx, cy) in enumerate(chips)],
            }
            for name in which:
                made[name] += groups[name]()
        return [made[name] for name in which]

    def start(ins, outs, send_sems, recv_sems):
        own, sends = copies(ins, outs, (send_sems, recv_sems), ("own", "sends"))
        for cp in own + sends:
            cp.start()

    def middle(ins, outs, send_sems, recv_sems):
        arrivals, passes = copies(ins, outs, (send_sems, recv_sems), ("arrivals", "passes"))
        for landed, onward in zip(arrivals, passes):
            landed.wait_recv()
            onward.start()

    def finish(ins, outs, send_sems, recv_sems):
        own, sends, passes, from_sibling = copies(ins, outs, (send_sems, recv_sems),
                                                  ("own", "sends", "passes", "from_sibling"))
        for cp in from_sibling:
            cp.wait_recv()
        for cp in sends + passes:
            cp.wait_send()
        for cp in own:
            cp.wait()

    shapes = [jax.ShapeDtypeStruct((4,) + s.shape, BF16) for s in shards]
    return _Rider(shards, shapes, 7 * len(shards), start, finish, middle=middle)


def _exchange_rider(inputs, out_shapes, n_sems, copies, aliases=None):
    def start(ins, outs, send_sems, recv_sems):
        for cp in copies(ins, outs, (send_sems, recv_sems)):
            cp.start()

    def finish(ins, outs, send_sems, recv_sems):
        for cp in copies(ins, outs, (send_sems, recv_sems)):
            cp.wait()

    return _Rider(inputs, out_shapes, n_sems, start, finish, aliases)


def _swap_rider(grads4):
    halves = [g.shape[1] // 2 for g in grads4]

    def copies(ins, outs, sems):
        x, y, c, _ = _place()
        return [_remote(g.at[:, pl.ds(pl.multiple_of((1 - c) * h, 8), h), :], a, sems, i, (x, y, 1 - c))
                for i, (g, a, h) in enumerate(zip(ins, outs, halves))]

    shapes = [jax.ShapeDtypeStruct((4, h, g.shape[2]), F32) for g, h in zip(grads4, halves)]
    return _exchange_rider(grads4, shapes, len(grads4), copies)


def _chip_sum(g4, from_sibling, name):
    _, rows, cols = g4.shape
    half = rows // 2

    def body(g_ref, s_ref, stage_ref, own_ref):
        x, y, c, chips = _place()
        lo = pl.multiple_of(c * half, 8)
        for k, (cx, cy) in enumerate(chips):
            theirs = 2 * cx + cy
            stage_ref[k] = (g_ref[theirs, pl.ds(lo, half), :] + s_ref[theirs]).astype(BF16)
        mine = 2 * x + y
        own_ref[...] = g_ref[mine, pl.ds(lo, half), :] + s_ref[mine]

    return pl.pallas_call(
        body, name=name, in_specs=[pl.BlockSpec(memory_space=pltpu.VMEM)] * 2,
        out_specs=[pl.BlockSpec(memory_space=pltpu.VMEM)] * 2,
        out_shape=[jax.ShapeDtypeStruct((3, half, cols), BF16), jax.ShapeDtypeStruct((half, cols), F32)],
        compiler_params=pltpu.CompilerParams(vmem_limit_bytes=VMEM_LIMIT),
    )(g4, from_sibling)


def _spread_rider(stages):
    def copies(ins, outs, sems):
        _, _, c, chips = _place()
        return [_remote(st.at[k], ld.at[k], sems, 3 * i + k, (cx, cy, c))
                for i, (st, ld) in enumerate(zip(ins, outs)) for k, (cx, cy) in enumerate(chips)]

    shapes = [jax.ShapeDtypeStruct(s.shape, s.dtype) for s in stages]
    return _exchange_rider(stages, shapes, 3 * len(stages), copies)


def _finish_half(own, landed, name):
    half, cols = own.shape

    def body(own_ref, landed_ref, out_ref):
        c = lax.axis_index("c")
        acc = own_ref[...]
        for k in range(3):
            acc = acc + landed_ref[k].astype(F32)
        out_ref[pl.ds(pl.multiple_of(c * half, 8), half), :] = acc

    return pl.pallas_call(
        body, name=name, in_specs=[pl.BlockSpec(memory_space=pltpu.VMEM)] * 2,
        out_specs=pl.BlockSpec(memory_space=pltpu.VMEM),
        out_shape=jax.ShapeDtypeStruct((2 * half, cols), F32),
        compiler_params=pltpu.CompilerParams(vmem_limit_bytes=VMEM_LIMIT),
    )(own, landed)


def _share_rider(fulls):
    def copies(ins, outs, sems):
        x, y, c, _ = _place()
        out = []
        for i, full in enumerate(outs):
            half = full.shape[0] // 2
            rows = full.at[pl.ds(pl.multiple_of(c * half, 8), half), :]
            out.append(_remote(rows, rows, sems, i, (x, y, 1 - c)))
        return out

    def finish_copies(ins, outs, sems):
        x, y, c, _ = _place()
        out = []
        for i, full in enumerate(outs):
            half = full.shape[0] // 2
            mine = full.at[pl.ds(pl.multiple_of(c * half, 8), half), :]
            theirs = full.at[pl.ds(pl.multiple_of((1 - c) * half, 8), half), :]
            out.append((_remote(mine, mine, sems, i, (x, y, 1 - c)), _remote(theirs, theirs, sems, i, (x, y, 1 - c))))
        return out

    def start(ins, outs, send_sems, recv_sems):
        for cp in copies(ins, outs, (send_sems, recv_sems)):
            cp.start()

    def finish(ins, outs, send_sems, recv_sems):
        for sent, landed in finish_copies(ins, outs, (send_sems, recv_sems)):
            sent.wait_send()
            landed.wait_recv()

    shapes = [jax.ShapeDtypeStruct(f.shape, f.dtype) for f in fulls]
    return _Rider(fulls, shapes, len(fulls), start, finish, aliases={i: i for i in range(len(fulls))})


def _all_sum_small(v):
    shape = v.shape

    def body(v_ref, out_ref, buf, send_sems, recv_sems):
        x, y, c, _ = _place()
        me = 4 * x + 2 * y + c
        buf[me] = v_ref[...]
        flips = [(dx, dy, dc) for dx in (0, 1) for dy in (0, 1) for dc in (0, 1)][1:]

        def copy(k, slot, flip):
            dx, dy, dc = flip
            to = (1 - x if dx else x, 1 - y if dy else y, 1 - c if dc else c)
            return pltpu.make_async_remote_copy(src_ref=buf.at[slot], dst_ref=buf.at[slot], send_sem=send_sems.at[k],
                                                recv_sem=recv_sems.at[k], device_id=to, device_id_type=MESH)

        sends = [copy(k, me, flip) for k, flip in enumerate(flips)]
        for cp in sends:
            cp.start()
        for k, (dx, dy, dc) in enumerate(flips):
            sender = 4 * (1 - x if dx else x) + 2 * (1 - y if dy else y) + (1 - c if dc else c)
            copy(k, sender, (dx, dy, dc)).wait_recv()
        for cp in sends:
            cp.wait_send()
        total = buf[0]
        for i in range(1, 8):
            total = total + buf[i]
        out_ref[...] = total

    return pl.pallas_call(
        body, name="all_sum_small",
        in_specs=[pl.BlockSpec(memory_space=pltpu.VMEM)],
        out_specs=pl.BlockSpec(memory_space=pltpu.VMEM),
        out_shape=jax.ShapeDtypeStruct(shape, F32),
        scratch_shapes=[pltpu.VMEM((8,) + shape, F32), pltpu.SemaphoreType.DMA((7,)), pltpu.SemaphoreType.DMA((7,))],
    )(v)


SMALL = (("g_mix", 1024), ("g_ffn", 1024), ("g_out_fox", 512), ("g_out_dil", 512), ("g_q_fox", 64),
         ("g_k_fox", 64), ("g_q_dil", 64), ("g_k_dil", 64), ("b_forget", 8))
SMALL_PACKED = (32, LANES)


def _local_grads(x, target, gains, w1, wft, dense, packed, nb, seq):
    tile2 = lambda g: jnp.tile(g, (1, 2))
    gq_f, gk_f, gq_d, gk_d = (tile2(gains[n]) for n in ("g_q_fox", "g_k_fox", "g_q_dil", "g_k_dil"))
    b_col = gains["b_forget"].reshape(N_FOX_HEADS, 1)
    cos, up, dn = _rope_tables(seq)
    npair = N_FOX_HEADS // 2

    proj, fa_row, h1, h1_t = _in_proj(x, gains["g_mix"], w1, wft)
    c_row = _gate_fwd(fa_row, b_col, seq)
    c3 = c_row.reshape(npair, 2, nb * seq)
    (o_fox, lse_fox), gathered = _fox_fwd(proj, c3, gq_f, gk_f, nb, seq,
                                          rider=None if packed is None else _gather_rider(packed))
    if packed is not None:
        dense = [g.reshape(-1, g.shape[2]) for g in gathered]
    w_out, w_gate, w_up, w_down = dense
    o_dil, lse_dil = _dil_fwd(proj, gq_d, gk_d, cos, up, dn, nb, seq)
    x1, o_n_t = _attn_out(o_fox, o_dil, x, gains["g_out_fox"], gains["g_out_dil"], w_out)
    a, u, dy, loss_parts = _ffn_fwd(x1, target, gains["g_ffn"], w_gate, w_up, w_down)
    loss = jnp.sum(loss_parts[:, 0, 0])

    dx1, s, da, du, h2, dg_ffn = _ffn_bwd(dy, a, u, x1, gains["g_ffn"], w_gate, w_up, w_down)
    d_w_down = _token_matmul(s, dy, "dw_down", 512, False)
    d_w_gate = _token_matmul(da, h2, "dw_gate", 512, False)
    d_w_up = _token_matmul(du, h2, "dw_up", 512, False)
    d_w_out = _token_matmul(o_n_t, dx1, "dw_out", 1024)
    names = ("w_out", "w_gate", "w_up", "w_down")
    grads4 = [g.reshape(4, -1, g.shape[1]) for g in (d_w_out, d_w_gate, d_w_up, d_w_down)]
    exchange = packed is not None
    (do_fox, do_dil, dg_of, dg_od), from_sibling = _attn_out_bwd(
        dx1, o_fox, o_dil, gains["g_out_fox"], gains["g_out_dil"], w_out,
        rider=_swap_rider(grads4) if exchange else None)
    if exchange:
        sums = [_chip_sum(g, s, "chip_sum_" + n) for g, s, n in zip(grads4, from_sibling, names)]
    (dq_f, dk_f, dv_f, dc3, dg_fox), landed = _fox_bwd(
        proj, c3, gq_f, gk_f, do_fox, o_fox, lse_fox, nb, seq,
        rider=_spread_rider([st for st, _ in sums]) if exchange else None)
    if exchange:
        halves = [_finish_half(own, ld, "finish_half_" + n) for (_, own), ld, n in zip(sums, landed, names)]
    (dq_d, dk_d, dv_d, dg_dil), reduced = _dil_bwd(
        proj, gq_d, gk_d, cos, up, dn, do_dil, o_dil, lse_dil, nb, seq,
        rider=_share_rider(halves) if exchange else None)
    if exchange:
        d_w_out, d_w_gate, d_w_up, d_w_down = reduced
    dfa_row, db = _gate_bwd(dc3.reshape(N_FOX_HEADS, nb * seq), fa_row, b_col, seq)
    dparts = [dq_f, dk_f, dv_f, dq_d, dk_d, dv_d]
    d_w1 = _token_matmul_parts(h1_t, dparts, "dw_in")
    d_wf = _row_matmul(dfa_row, h1, "dw_forget")
    fox_w = 3 * W_GROUP
    d_w_in = jnp.concatenate([d_w1[:, :fox_w], d_wf.T, d_w1[:, fox_w:]], axis=1)
    if exchange:
        shards = [_shards_of_columns(d_w_in)]
        _, from_sibling = _idle_host(_swap_rider(shards), "swap_w_in")
        stage, own = _chip_sum(shards[0], from_sibling[0], "chip_sum_w_in")
    (grad_x, dg_mix), landed = _in_proj_bwd(dparts, dfa_row, w1, wft, x, gains["g_mix"], dx1,
                                            rider=_spread_rider([stage]) if exchange else None)
    if exchange:
        d_w_in = _finish_half(own, landed[0], "finish_half_w_in")

    fold = lambda g2: (g2[:, :HEAD_DIM] + g2[:, HEAD_DIM:])
    small = {
        "g_mix": dg_mix[0:1], "g_ffn": dg_ffn[0:1], "g_out_fox": dg_of[0:1], "g_out_dil": dg_od[0:1],
        "g_q_fox": fold(dg_fox[0:1]), "g_k_fox": fold(dg_fox[1:2]),
        "g_q_dil": fold(dg_dil[0:1]), "g_k_dil": fold(dg_dil[1:2]),
        "b_forget": db[:, 0].reshape(1, N_FOX_HEADS),
    }
    big = {"w_in": d_w_in, "w_out": d_w_out, "w_gate": d_w_gate, "w_up": d_w_up, "w_down": d_w_down}
    return loss, grad_x, big, small


def _shards_of_columns(full, n=4):
    r, nc = full.shape
    return full.reshape(r, n, nc // n).transpose(1, 0, 2)


def _columns_of_shards(slabs):
    n, r, c = slabs.shape
    return slabs.transpose(1, 0, 2).reshape(r, n * c)


def kernel(x, g_mix, w_in, b_forget, g_q_fox, g_k_fox, g_q_dil, g_k_dil, g_out_fox, g_out_dil, w_out, g_ffn, w_gate, w_up, w_down, loss_target, m_g_mix, m_w_in, m_b_forget, m_g_q_fox, m_g_k_fox, m_g_q_dil, m_g_k_dil, m_g_out_fox, m_g_out_dil, m_w_out, m_g_ffn, m_w_gate, m_w_up, m_w_down, v_g_mix, v_w_in, v_b_forget, v_g_q_fox, v_g_k_fox, v_g_q_dil, v_g_k_dil, v_g_out_fox, v_g_out_dil, v_w_out, v_g_ffn, v_w_gate, v_w_up, v_w_down):
    nb, seq, d = x.shape
    weights = dict(g_mix=g_mix, w_in=w_in, b_forget=b_forget, g_q_fox=g_q_fox, g_k_fox=g_k_fox, g_q_dil=g_q_dil,
                   g_k_dil=g_k_dil, g_out_fox=g_out_fox, g_out_dil=g_out_dil, w_out=w_out, g_ffn=g_ffn,
                   w_gate=w_gate, w_up=w_up, w_down=w_down)
    m_in = dict(g_mix=m_g_mix, w_in=m_w_in, b_forget=m_b_forget, g_q_fox=m_g_q_fox, g_k_fox=m_g_k_fox,
                g_q_dil=m_g_q_dil, g_k_dil=m_g_k_dil, g_out_fox=m_g_out_fox, g_out_dil=m_g_out_dil, w_out=m_w_out,
                g_ffn=m_g_ffn, w_gate=m_w_gate, w_up=m_w_up, w_down=m_w_down)
    v_in = dict(g_mix=v_g_mix, w_in=v_w_in, b_forget=v_b_forget, g_q_fox=v_g_q_fox, g_k_fox=v_g_k_fox,
                g_q_dil=v_g_q_dil, g_k_dil=v_g_k_dil, g_out_fox=v_g_out_fox, g_out_dil=v_g_out_dil, w_out=v_w_out,
                g_ffn=v_g_ffn, w_gate=v_w_gate, w_up=v_w_up, w_down=v_w_down)
    order = ["g_mix", "w_in", "b_forget", "g_q_fox", "g_k_fox", "g_q_dil", "g_k_dil", "g_out_fox", "g_out_dil",
             "w_out", "g_ffn", "w_gate", "w_up", "w_down"]

    w_in_full = _columns_of_shards(_gather_weight(w_in, "gather_w_in"))
    fox_w = 3 * W_GROUP
    w1 = jnp.concatenate([w_in_full[:, :fox_w], w_in_full[:, fox_w + N_FOX_HEADS:]], axis=1)
    wft = w_in_full[:, fox_w:fox_w + N_FOX_HEADS].T
    swap = lambda a: jnp.transpose(a, (0, 2, 1))
    for n in ("w_gate", "w_up"):
        weights[n], m_in[n], v_in[n] = swap(weights[n]), swap(m_in[n]), swap(v_in[n])
    shards = _cast_bf16([weights[n] for n in ("w_out", "w_gate", "w_up", "w_down")], "cast_shards")

    gains = {n: weights[n] for n, _ in SMALL}
    loss, grad_x, big, small = _local_grads(
        x.reshape(nb * seq, d), loss_target.reshape(nb * seq, d), gains, w1, wft, None, shards, nb, seq)

    grads = {n: big[n][None] for n in ("w_out", "w_gate", "w_up", "w_down")}
    packed = jnp.concatenate([small[n].reshape(-1) for n, _ in SMALL] + [loss.reshape(1)])
    packed = jnp.pad(packed, (0, SMALL_PACKED[0] * SMALL_PACKED[1] - packed.shape[0])).reshape(SMALL_PACKED)
    summed = _all_sum_small(packed).reshape(-1)
    pos = 0
    for n, size in SMALL:
        grads[n] = summed[pos:pos + size].reshape(1, size)
        pos += size
    loss = summed[pos]

    to_entry = lambda a: jnp.transpose(a, (2, 0, 1))
    deltas, new_m, new_v, grad_out = {}, {}, {}, {}
    for n in ["w_down"] + [n for n in order if n != "w_down"]:
        rider = _share_rider([big["w_in"]]) if n == "w_down" else None
        (deltas[n], new_m[n], new_v[n]), shared = _adamw(weights[n], grads[n], m_in[n], v_in[n], "adamw_" + n, rider)
        if rider is not None:
            grads["w_in"] = to_entry(shared[0][None])
            weights["w_in"], m_in["w_in"], v_in["w_in"] = (to_entry(a) for a in (w_in, m_w_in, v_w_in))
        grad_out[n] = grads[n]
    for n in ("w_gate", "w_up"):
        grad_out[n], deltas[n], new_m[n], new_v[n] = (swap(a) for a in (grad_out[n], deltas[n], new_m[n], new_v[n]))
    from_entry = lambda a: jnp.transpose(a, (1, 2, 0))
    grad_out["w_in"], deltas["w_in"], new_m["w_in"], new_v["w_in"] = (
        from_entry(a) for a in (grad_out["w_in"], deltas["w_in"], new_m["w_in"], new_v["w_in"]))

    return (loss, grad_x.reshape(nb, seq, d), *[grad_out[n] for n in order], *[deltas[n] for n in order],
            *[new_m[n] for n in order], *[new_v[n] for n in order])
```

```python
import functools
import math

import numpy as np
import jax
import jax.numpy as jnp
from jax import lax
from jax.experimental import pallas as pl
from jax.experimental.pallas import tpu as pltpu

F32, BF16 = jnp.float32, jnp.bfloat16
MESH = pl.DeviceIdType.MESH

EPS = 1e-6
NEG = -1e30
HEAD_DIM = 64
SCALE = HEAD_DIM ** -0.5
LOG2E = math.log2(math.e)
LN2 = math.log(2.0)
ROPE_THETA = 500000.0
ROPE_DIM = HEAD_DIM // 4
LANES = 128
W_GROUP = 512
N_FOX_HEADS = 8
VMEM_LIMIT = 56 * 1024 * 1024
DILATIONS = (1, 4, 16)
BAND = 128

ADAM_LR, ADAM_B1, ADAM_B2, ADAM_EPS, ADAM_WD, ADAM_STEP = 0.001, 0.9, 0.999, 1e-08, 0.01, 10

NT = (((1,), (1,)), ((), ()))
TN = (((0,), (0,)), ((), ()))
BATCH_NT = (((2,), (2,)), ((0,), (0,)))
BATCH_NN = (((2,), (1,)), ((0,), (0,)))
BATCH_TN = (((1,), (1,)), ((0,), (0,)))


def _params(sem=None):
    return pltpu.CompilerParams(dimension_semantics=sem, vmem_limit_bytes=VMEM_LIMIT)


def _dot(a, b, dims=None):
    if dims is None:
        return jnp.dot(a, b, preferred_element_type=F32)
    return lax.dot_general(a, b, dims, preferred_element_type=F32)


def _group_ones():
    i = lax.broadcasted_iota(jnp.int32, (LANES, LANES), 0) >> 6
    j = lax.broadcasted_iota(jnp.int32, (LANES, LANES), 1) >> 6
    return (i == j).astype(BF16)


def _split3(x):
    a = x.astype(BF16)
    r = x - a.astype(F32)
    b = r.astype(BF16)
    c = (r - b.astype(F32)).astype(BF16)
    return a, b, c


def _groupsum(x, ones, pieces=2):
    total = None
    for _ in range(pieces):
        piece = x.astype(BF16)
        part = _dot(piece, ones)
        total = part if total is None else total + part
        x = x - piece.astype(F32)
    return total


def _head_masks():
    lane = lax.broadcasted_iota(jnp.int32, (1, LANES), 1)
    return [(lane < HEAD_DIM).astype(F32), (lane >= HEAD_DIM).astype(F32)]


def _head_norm(raw, gain, ones):
    r = lax.rsqrt(_groupsum(raw * raw, ones, 1) * (1.0 / HEAD_DIM) + EPS)
    return raw * r, r


def _head_norm_bwd(dy, xhat, r, gain, ones):
    u = dy * gain
    dgain = jnp.sum(dy * xhat, axis=0, keepdims=True)
    draw = r * (u - xhat * (_groupsum(u * xhat, ones) * (1.0 / HEAD_DIM)))
    return draw, dgain


def _rope(x, cos, s_up, s_dn):
    return x * cos + pltpu.roll(x, LANES - 8, 1) * s_up + pltpu.roll(x, 8, 1) * s_dn


def _rope_bwd(dy, cos, s_up, s_dn):
    return dy * cos + pltpu.roll(dy * s_up, 8, 1) + pltpu.roll(dy * s_dn, LANES - 8, 1)


def _rope_tables(seq):
    half = ROPE_DIM // 2
    inv_freq = jnp.power(jnp.float32(ROPE_THETA), -jnp.arange(half, dtype=F32) * 2.0 / ROPE_DIM)
    ang = jnp.arange(seq).astype(F32)[:, None] * inv_freq[None, :]
    cos, sin = jnp.cos(ang), jnp.sin(ang)
    one = jnp.ones((seq, HEAD_DIM - ROPE_DIM), F32)
    zero_h = jnp.zeros((seq, half), F32)
    zero_r = jnp.zeros((seq, HEAD_DIM - ROPE_DIM), F32)
    c = jnp.concatenate([cos, cos, one], axis=1)
    up = jnp.concatenate([-sin, zero_h, zero_r], axis=1)
    dn = jnp.concatenate([zero_h, sin, zero_r], axis=1)
    return jnp.tile(c, (1, 2)), jnp.tile(up, (1, 2)), jnp.tile(dn, (1, 2))


def _row_tile(rows, cap=256):
    best = rows
    for t in range(8, min(rows, cap) + 1, 8):
        if rows % t == 0:
            best = t
    return best


class _Rider:
    def __init__(self, inputs, out_shapes, n_sems, start, finish, aliases=None, middle=None):
        self.inputs, self.out_shapes, self.n_sems = list(inputs), list(out_shapes), n_sems
        self.start, self.finish, self.middle, self.aliases = start, finish, middle, dict(aliases or {})


def _host_call(body, rider, *, name, grid, in_specs, out_specs, out_shape, scratch_shapes, inputs, semantics):
    if rider is None:
        return pl.pallas_call(body, name=name, grid=grid, in_specs=in_specs, out_specs=out_specs,
                              out_shape=out_shape, scratch_shapes=scratch_shapes,
                              compiler_params=_params(semantics))(*inputs), []
    n_in, n_out, n_scr = len(in_specs), len(out_specs), len(scratch_shapes)
    r_in, r_out = len(rider.inputs), len(rider.out_shapes)

    def wrapped(*refs):
        ins, refs = refs[:n_in], refs[n_in:]
        r_ins, refs = refs[:r_in], refs[r_in:]
        outs, refs = refs[:n_out], refs[n_out:]
        r_outs, refs = refs[:r_out], refs[r_out:]
        scratch, (send_sems, recv_sems) = refs[:n_scr], refs[n_scr:]
        ids = [pl.program_id(a) for a in range(len(grid))]
        first = functools.reduce(lambda p, q: p & q, [i == 0 for i in ids])
        last = functools.reduce(lambda p, q: p & q, [i == g - 1 for i, g in zip(ids, grid)])

        @pl.when(first)
        def _():
            rider.start(r_ins, r_outs, send_sems, recv_sems)

        body(*ins, *outs, *scratch)

        if rider.middle is not None:
            step, steps = ids[0], grid[0]
            for i, g in zip(ids[1:], grid[1:]):
                step, steps = step * g + i, steps * g

            @pl.when(step == (3 * steps) // 4)
            def _():
                rider.middle(r_ins, r_outs, send_sems, recv_sems)

        @pl.when(last)
        def _():
            rider.finish(r_ins, r_outs, send_sems, recv_sems)

    hbm = pl.BlockSpec(memory_space=pl.ANY)
    res = pl.pallas_call(
        wrapped, name=name, grid=grid,
        in_specs=list(in_specs) + [hbm] * r_in, out_specs=list(out_specs) + [hbm] * r_out,
        out_shape=list(out_shape) + rider.out_shapes,
        scratch_shapes=list(scratch_shapes) + [pltpu.SemaphoreType.DMA((rider.n_sems,))] * 2,
        input_output_aliases={n_in + i: n_out + o for i, o in rider.aliases.items()},
        compiler_params=_params(semantics),
    )(*inputs, *rider.inputs)
    return res[:n_out], res[n_out:]


def _idle_host(rider, name):
    def body(o_ref):
        o_ref[...] = jnp.zeros_like(o_ref)

    return _host_call(body, rider, name=name, grid=(1,), in_specs=[],
                      out_specs=[pl.BlockSpec((8, LANES), lambda i: (0, 0))],
                      out_shape=[jax.ShapeDtypeStruct((8, LANES), F32)], scratch_shapes=[], inputs=(),
                      semantics=("arbitrary",))


def _in_proj(x, g_mix, w1, wft):
    t, d = x.shape
    n = w1.shape[1]
    tt = 512

    def body(x_ref, g_ref, w_ref, wf_ref, p_ref, fa_ref, h_ref, ht_ref):
        xx = x_ref[...]
        r = lax.rsqrt(jnp.mean(xx * xx, axis=-1, keepdims=True) + EPS)
        h = (xx * r * g_ref[...]).astype(BF16)
        h_ref[...] = h
        ht_ref[...] = h.T
        for j in range(n // W_GROUP):
            cols = slice(j * W_GROUP, (j + 1) * W_GROUP)
            p_ref[:, cols] = _dot(h, w_ref[:, cols]).astype(BF16)
        fa_ref[...] = _dot(wf_ref[...], h, NT)

    return pl.pallas_call(
        body, name="in_proj", grid=(t // tt,),
        in_specs=[pl.BlockSpec((tt, d), lambda i: (i, 0)), pl.BlockSpec((1, d), lambda i: (0, 0)),
                  pl.BlockSpec(memory_space=pltpu.VMEM), pl.BlockSpec(memory_space=pltpu.VMEM)],
        out_specs=[pl.BlockSpec((tt, n), lambda i: (i, 0)), pl.BlockSpec((8, tt), lambda i: (0, i)),
                   pl.BlockSpec((tt, d), lambda i: (i, 0)), pl.BlockSpec((d, tt), lambda i: (0, i))],
        out_shape=[jax.ShapeDtypeStruct((t, n), BF16), jax.ShapeDtypeStruct((8, t), F32),
                   jax.ShapeDtypeStruct((t, d), BF16), jax.ShapeDtypeStruct((d, t), BF16)],
        compiler_params=_params(("arbitrary",)),
    )(x, g_mix, w1, wft)


def _tri(n, upper):
    i = lax.broadcasted_iota(jnp.int32, (n, n), 0)
    j = lax.broadcasted_iota(jnp.int32, (n, n), 1)
    return ((i <= j) if upper else (i >= j)).astype(BF16)


def _gate_fwd(fa_row, b_col, seq):
    t = fa_row.shape[1]
    cb = 256

    def body(fa_ref, b_ref, c_ref):
        tri = _tri(cb, True)
        carry = jnp.zeros((8, 1), F32)
        for k in range(seq // cb):
            z = fa_ref[:, k * cb:(k + 1) * cb] + b_ref[...]
            lf = jnp.minimum(z, 0.0) - jnp.log(1.0 + jnp.exp(-jnp.abs(z)))
            a, b, c = _split3(lf)
            blk = _dot(a, tri) + _dot(b, tri) + _dot(c, tri) + carry
            c_ref[:, k * cb:(k + 1) * cb] = blk
            carry = blk[:, cb - 1:cb]

    return pl.pallas_call(
        body, name="gate_fwd", grid=(t // seq,),
        in_specs=[pl.BlockSpec((8, seq), lambda i: (0, i)), pl.BlockSpec((8, 1), lambda i: (0, 0))],
        out_specs=pl.BlockSpec((8, seq), lambda i: (0, i)),
        out_shape=jax.ShapeDtypeStruct((8, t), F32),
        compiler_params=_params(("arbitrary",)),
    )(fa_row, b_col)


def _gate_bwd(dc_row, fa_row, b_col, seq):
    t = fa_row.shape[1]
    cb = 256

    def body(dc_ref, fa_ref, b_ref, dfa_ref, db_ref):
        @pl.when(pl.program_id(0) == 0)
        def _():
            db_ref[...] = jnp.zeros_like(db_ref)

        tri = _tri(cb, False)
        carry = jnp.zeros((8, 1), F32)
        dbs = jnp.zeros((8, 1), F32)
        for k in reversed(range(seq // cb)):
            a, b, c = _split3(dc_ref[:, k * cb:(k + 1) * cb])
            dlf = _dot(a, tri) + _dot(b, tri) + _dot(c, tri) + carry
            carry = dlf[:, 0:1]
            z = fa_ref[:, k * cb:(k + 1) * cb] + b_ref[...]
            dfa = dlf / (1.0 + jnp.exp(z))
            dfa_ref[:, k * cb:(k + 1) * cb] = dfa
            dbs = dbs + jnp.sum(dfa, axis=1, keepdims=True)
        db_ref[...] += jnp.broadcast_to(dbs, (8, LANES))

    return pl.pallas_call(
        body, name="gate_bwd", grid=(t // seq,),
        in_specs=[pl.BlockSpec((8, seq), lambda i: (0, i)), pl.BlockSpec((8, seq), lambda i: (0, i)),
                  pl.BlockSpec((8, 1), lambda i: (0, 0))],
        out_specs=[pl.BlockSpec((8, seq), lambda i: (0, i)), pl.BlockSpec((8, LANES), lambda i: (0, 0))],
        out_shape=[jax.ShapeDtypeStruct((8, t), F32), jax.ShapeDtypeStruct((8, LANES), F32)],
        compiler_params=_params(("arbitrary",)),
    )(dc_row, fa_row, b_col)


def _attn_out(o_fox, o_dil, x, g_fox, g_dil, w_out):
    t, d = x.shape
    w = o_fox.shape[1]
    tt = 512

    def body(of_ref, od_ref, x_ref, gf_ref, gd_ref, w_ref, x1_ref, ont_ref):
        acc = x_ref[...]
        for k, (o_ref, g_ref) in enumerate(((of_ref, gf_ref), (od_ref, gd_ref))):
            o = o_ref[...]
            r = lax.rsqrt(jnp.mean(o * o, axis=-1, keepdims=True) + EPS)
            on = (o * r * g_ref[...]).astype(BF16)
            ont_ref[k * w:(k + 1) * w, :] = on.T
            acc = acc + _dot(on, w_ref[k * w:(k + 1) * w, :])
        x1_ref[...] = acc

    return pl.pallas_call(
        body, name="attn_out", grid=(t // tt,),
        in_specs=[pl.BlockSpec((tt, w), lambda i: (i, 0)), pl.BlockSpec((tt, w), lambda i: (i, 0)),
                  pl.BlockSpec((tt, d), lambda i: (i, 0)), pl.BlockSpec((1, w), lambda i: (0, 0)),
                  pl.BlockSpec((1, w), lambda i: (0, 0)), pl.BlockSpec(memory_space=pltpu.VMEM)],
        out_specs=[pl.BlockSpec((tt, d), lambda i: (i, 0)), pl.BlockSpec((2 * w, tt), lambda i: (0, i))],
        out_shape=[jax.ShapeDtypeStruct((t, d), F32), jax.ShapeDtypeStruct((2 * w, t), BF16)],
        compiler_params=_params(("arbitrary",)),
    )(o_fox, o_dil, x, g_fox, g_dil, w_out)


def _attn_out_bwd(dx1, o_fox, o_dil, g_fox, g_dil, w_out, rider=None):
    t, d = dx1.shape
    w = o_fox.shape[1]
    tt = 512

    def body(dx_ref, of_ref, od_ref, gf_ref, gd_ref, w_ref, dof_ref, dod_ref, dgf_ref, dgd_ref):
        @pl.when(pl.program_id(0) == 0)
        def _():
            dgf_ref[...] = jnp.zeros_like(dgf_ref)
            dgd_ref[...] = jnp.zeros_like(dgd_ref)

        dxb = dx_ref[...].astype(BF16)
        for k, (o_ref, g_ref, do_ref, dg_ref) in enumerate(
                ((of_ref, gf_ref, dof_ref, dgf_ref), (od_ref, gd_ref, dod_ref, dgd_ref))):
            don = _dot(dxb, w_ref[k * w:(k + 1) * w, :], NT)
            o = o_ref[...]
            r = lax.rsqrt(jnp.mean(o * o, axis=-1, keepdims=True) + EPS)
            xhat = o * r
            u = don * g_ref[...]
            do_ref[...] = r * (u - xhat * jnp.mean(u * xhat, axis=-1, keepdims=True))
            dg_ref[0:1, :] += jnp.sum(don * xhat, axis=0, keepdims=True)

    return _host_call(
        body, rider, name="attn_out_bwd", grid=(t // tt,),
        in_specs=[pl.BlockSpec((tt, d), lambda i: (i, 0)), pl.BlockSpec((tt, w), lambda i: (i, 0)),
                  pl.BlockSpec((tt, w), lambda i: (i, 0)), pl.BlockSpec((1, w), lambda i: (0, 0)),
                  pl.BlockSpec((1, w), lambda i: (0, 0)), pl.BlockSpec(memory_space=pltpu.VMEM)],
        out_specs=[pl.BlockSpec((tt, w), lambda i: (i, 0)), pl.BlockSpec((tt, w), lambda i: (i, 0)),
                   pl.BlockSpec((8, w), lambda i: (0, 0)), pl.BlockSpec((8, w), lambda i: (0, 0))],
        out_shape=[jax.ShapeDtypeStruct((t, w), F32), jax.ShapeDtypeStruct((t, w), F32),
                   jax.ShapeDtypeStruct((8, w), F32), jax.ShapeDtypeStruct((8, w), F32)],
        scratch_shapes=[], inputs=(dx1, o_fox, o_dil, g_fox, g_dil, w_out), semantics=("arbitrary",))


def _ffn_fwd(x1, target, g_ffn, w_gate, w_up, w_down):
    t, d = x1.shape
    f = w_gate.shape[0]
    tt = 256

    def body(x_ref, t_ref, g_ref, wg_ref, wu_ref, wd_ref, a_ref, u_ref, dy_ref, loss_ref):
        xx = x_ref[...]
        r = lax.rsqrt(jnp.mean(xx * xx, axis=-1, keepdims=True) + EPS)
        h = (xx * r * g_ref[...]).astype(BF16)
        a = _dot(h, wg_ref[...], NT)
        u = _dot(h, wu_ref[...], NT)
        a_ref[...] = a.astype(BF16)
        u_ref[...] = u.astype(BF16)
        s = (a / (1.0 + jnp.exp(-a)) * u).astype(BF16)
        y = xx + _dot(s, wd_ref[...])
        e = y - t_ref[...]
        dy_ref[...] = e * (1.0 / d)
        loss_ref[...] = jnp.broadcast_to(0.5 * jnp.sum(e * e) * (1.0 / d), (1, 8, LANES))

    return pl.pallas_call(
        body, name="ffn_fwd", grid=(t // tt,),
        in_specs=[pl.BlockSpec((tt, d), lambda i: (i, 0)), pl.BlockSpec((tt, d), lambda i: (i, 0)),
                  pl.BlockSpec((1, d), lambda i: (0, 0)), pl.BlockSpec(memory_space=pltpu.VMEM),
                  pl.BlockSpec(memory_space=pltpu.VMEM), pl.BlockSpec(memory_space=pltpu.VMEM)],
        out_specs=[pl.BlockSpec((tt, f), lambda i: (i, 0)), pl.BlockSpec((tt, f), lambda i: (i, 0)),
                   pl.BlockSpec((tt, d), lambda i: (i, 0)), pl.BlockSpec((1, 8, LANES), lambda i: (i, 0, 0))],
        out_shape=[jax.ShapeDtypeStruct((t, f), BF16), jax.ShapeDtypeStruct((t, f), BF16),
                   jax.ShapeDtypeStruct((t, d), F32), jax.ShapeDtypeStruct((t // tt, 8, LANES), F32)],
        compiler_params=_params(("arbitrary",)),
    )(x1, target, g_ffn, w_gate, w_up, w_down)


def _ffn_bwd(dy, a, u, x1, g_ffn, w_gate, w_up, w_down):
    t, d = x1.shape
    f = w_gate.shape[0]
    tt = 256

    def body(dy_ref, a_ref, u_ref, x_ref, g_ref, wg_ref, wu_ref, wd_ref,
             dx_ref, s_ref, da_ref, du_ref, h_ref, dg_ref):
        @pl.when(pl.program_id(0) == 0)
        def _():
            dg_ref[...] = jnp.zeros_like(dg_ref)

        dy_ = dy_ref[...]
        ds = _dot(dy_.astype(BF16), wd_ref[...], NT)
        a_ = a_ref[...].astype(F32)
        u_ = u_ref[...].astype(F32)
        sig = 1.0 / (1.0 + jnp.exp(-a_))
        silu = a_ * sig
        s_ref[...] = (silu * u_).astype(BF16)
        da = (ds * u_ * (sig * (1.0 + a_ * (1.0 - sig)))).astype(BF16)
        du = (ds * silu).astype(BF16)
        da_ref[...] = da
        du_ref[...] = du
        dh = _dot(da, wg_ref[...]) + _dot(du, wu_ref[...])
        xx = x_ref[...]
        r = lax.rsqrt(jnp.mean(xx * xx, axis=-1, keepdims=True) + EPS)
        xhat = xx * r
        g = g_ref[...]
        h_ref[...] = (xhat * g).astype(BF16)
        uu = dh * g
        dx_ref[...] = dy_ + r * (uu - xhat * jnp.mean(uu * xhat, axis=-1, keepdims=True))
        dg_ref[0:1, :] += jnp.sum(dh * xhat, axis=0, keepdims=True)

    return pl.pallas_call(
        body, name="ffn_bwd", grid=(t // tt,),
        in_specs=[pl.BlockSpec((tt, d), lambda i: (i, 0)), pl.BlockSpec((tt, f), lambda i: (i, 0)),
                  pl.BlockSpec((tt, f), lambda i: (i, 0)), pl.BlockSpec((tt, d), lambda i: (i, 0)),
                  pl.BlockSpec((1, d), lambda i: (0, 0)), pl.BlockSpec(memory_space=pltpu.VMEM),
                  pl.BlockSpec(memory_space=pltpu.VMEM), pl.BlockSpec(memory_space=pltpu.VMEM)],
        out_specs=[pl.BlockSpec((tt, d), lambda i: (i, 0)), pl.BlockSpec((tt, f), lambda i: (i, 0)),
                   pl.BlockSpec((tt, f), lambda i: (i, 0)), pl.BlockSpec((tt, f), lambda i: (i, 0)),
                   pl.BlockSpec((tt, d), lambda i: (i, 0)), pl.BlockSpec((8, d), lambda i: (0, 0))],
        out_shape=[jax.ShapeDtypeStruct((t, d), F32), jax.ShapeDtypeStruct((t, f), BF16),
                   jax.ShapeDtypeStruct((t, f), BF16), jax.ShapeDtypeStruct((t, f), BF16),
                   jax.ShapeDtypeStruct((t, d), BF16), jax.ShapeDtypeStruct((8, d), F32)],
        compiler_params=_params(("arbitrary",)),
    )(dy, a, u, x1, g_ffn, w_gate, w_up, w_down)


def _in_proj_bwd(dparts, dfa_row, w1, wft, x, g_mix, dx1, rider=None):
    t, d = x.shape
    tt = 512
    npart = len(dparts)

    def body(*refs):
        dp_refs = refs[:npart]
        dfa_ref, w_ref, wf_ref, x_ref, g_ref, dx1_ref, dx_ref, dg_ref = refs[npart:]

        @pl.when(pl.program_id(0) == 0)
        def _():
            dg_ref[...] = jnp.zeros_like(dg_ref)

        dh = _dot(dfa_ref[...].astype(BF16), wf_ref[...], TN)
        for j in range(npart):
            dh = dh + _dot(dp_refs[j][...], w_ref[:, j * W_GROUP:(j + 1) * W_GROUP], NT)
        xx = x_ref[...]
        r = lax.rsqrt(jnp.mean(xx * xx, axis=-1, keepdims=True) + EPS)
        xhat = xx * r
        uu = dh * g_ref[...]
        dx_ref[...] = dx1_ref[...] + r * (uu - xhat * jnp.mean(uu * xhat, axis=-1, keepdims=True))
        dg_ref[0:1, :] += jnp.sum(dh * xhat, axis=0, keepdims=True)

    return _host_call(
        body, rider, name="in_proj_bwd", grid=(t // tt,),
        in_specs=[pl.BlockSpec((tt, W_GROUP), lambda i: (i, 0)) for _ in range(npart)]
        + [pl.BlockSpec((8, tt), lambda i: (0, i)), pl.BlockSpec(memory_space=pltpu.VMEM),
           pl.BlockSpec(memory_space=pltpu.VMEM), pl.BlockSpec((tt, d), lambda i: (i, 0)),
           pl.BlockSpec((1, d), lambda i: (0, 0)), pl.BlockSpec((tt, d), lambda i: (i, 0))],
        out_specs=[pl.BlockSpec((tt, d), lambda i: (i, 0)), pl.BlockSpec((8, d), lambda i: (0, 0))],
        out_shape=[jax.ShapeDtypeStruct((t, d), F32), jax.ShapeDtypeStruct((8, d), F32)],
        scratch_shapes=[], inputs=(*dparts, dfa_row, w1, wft, x, g_mix, dx1), semantics=("arbitrary",))


def _token_matmul(a, b, name, tn, a_is_transposed=True):
    m, t = a.shape if a_is_transposed else a.shape[::-1]
    n = b.shape[1]
    tk = 1024

    def body(a_ref, b_ref, o_ref):
        @pl.when(pl.program_id(1) == 0)
        def _():
            o_ref[...] = jnp.zeros_like(o_ref)

        o_ref[...] += _dot(a_ref[...], b_ref[...].astype(BF16), None if a_is_transposed else TN)

    a_spec = pl.BlockSpec((m, tk), lambda j, k: (0, k)) if a_is_transposed else pl.BlockSpec((tk, m), lambda j, k: (k, 0))
    return pl.pallas_call(
        body, name=name, grid=(n // tn, t // tk),
        in_specs=[a_spec, pl.BlockSpec((tk, tn), lambda j, k: (k, j))],
        out_specs=pl.BlockSpec((m, tn), lambda j, k: (0, j)),
        out_shape=jax.ShapeDtypeStruct((m, n), F32),
        compiler_params=_params(("arbitrary", "arbitrary")),
    )(a, b)


def _token_matmul_parts(at, parts, name):
    m, t = at.shape
    widths = [p.shape[1] for p in parts]
    tk = 1024

    def body(a_ref, *refs):
        o_ref = refs[-1]

        @pl.when(pl.program_id(0) == 0)
        def _():
            o_ref[...] = jnp.zeros_like(o_ref)

        a, first = a_ref[...], 0
        for b_ref, w in zip(refs[:-1], widths):
            o_ref[:, first:first + w] += _dot(a, b_ref[...])
            first += w

    return pl.pallas_call(
        body, name=name, grid=(t // tk,),
        in_specs=[pl.BlockSpec((m, tk), lambda k: (0, k))] + [pl.BlockSpec((tk, w), lambda k: (k, 0)) for w in widths],
        out_specs=pl.BlockSpec((m, sum(widths)), lambda k: (0, 0)),
        out_shape=jax.ShapeDtypeStruct((m, sum(widths)), F32),
        compiler_params=_params(("arbitrary",)),
    )(at, *parts)


def _row_matmul(a_row, b, name):
    t, n = b.shape
    tk = 1024
    nk = t // tk

    def body(a_ref, b_ref, o_ref):
        @pl.when(pl.program_id(0) == 0)
        def _():
            o_ref[...] = jnp.zeros_like(o_ref)

        o_ref[...] += _dot(a_ref[...].astype(BF16), b_ref[...])

    return pl.pallas_call(
        body, name=name, grid=(nk,),
        in_specs=[pl.BlockSpec((8, tk), lambda k: (0, k)), pl.BlockSpec((tk, n), lambda k: (k, 0))],
        out_specs=pl.BlockSpec((8, n), lambda k: (0, 0)),
        out_shape=jax.ShapeDtypeStruct((8, n), F32),
        compiler_params=_params(("arbitrary",)),
    )(a_row, b)


FOX_TQ = 256
SUM_LANE = (HEAD_DIM, 0)


def _fox_fwd(proj, c3, gq, gk, nb, seq, rider=None):
    t = nb * seq
    tq = FOX_TQ
    nq = seq // tq
    npair = N_FOX_HEADS // 2

    def body(q_ref, k_ref, v_ref, c_ref, gq_ref, gk_ref, o_ref, lse_ref, qs, ks, vs):
        ones = _group_ones()
        masks = _head_masks()
        qhat, _ = _head_norm(q_ref[...].astype(F32), None, ones)
        khat, _ = _head_norm(k_ref[...].astype(F32), None, ones)
        qs[...] = (qhat * gq_ref[...] * (SCALE * LOG2E)).astype(BF16)
        kn = khat * gk_ref[...]
        vv = v_ref[...].astype(F32)
        lane = lax.broadcasted_iota(jnp.int32, (1, LANES), 1)
        for hd in range(2):
            ks[hd] = (kn * masks[hd]).astype(BF16)
            vs[hd] = (vv * masks[hd] + (lane == SUM_LANE[hd]).astype(F32)).astype(BF16)
        row = lax.broadcasted_iota(jnp.int32, (tq, tq), 0)
        col = lax.broadcasted_iota(jnp.int32, (tq, tq), 1)
        causal = col <= row

        for qi in range(nq):
            q0 = qi * tq
            q_blk = qs[q0:q0 + tq, :]
            o_tot = jnp.zeros((tq, LANES), F32)
            lse_tot = jnp.zeros((tq, LANES), F32)
            for hd in range(2):
                crow = c_ref[0, hd:hd + 1, 0:q0 + tq] * LOG2E
                c0 = crow[:, q0:q0 + 1]
                s_d = _dot(q_blk, ks[hd, q0:q0 + tq, :], NT) + (c0 - crow[:, q0:q0 + tq])
                s_d = jnp.where(causal, s_d, NEG)
                m = jnp.max(s_d, axis=-1, keepdims=True)
                if qi > 0:
                    s_o = _dot(q_blk, ks[hd, 0:q0, :], NT) + (c0 - crow[:, 0:q0])
                    m = jnp.maximum(m, jnp.max(s_o, axis=-1, keepdims=True))
                acc = _dot(jnp.exp2(s_d - m).astype(BF16), vs[hd, q0:q0 + tq, :])
                if qi > 0:
                    acc = acc + _dot(jnp.exp2(s_o - m).astype(BF16), vs[hd, 0:q0, :])
                l = acc[:, SUM_LANE[hd]:SUM_LANE[hd] + 1]
                o_tot = o_tot + (acc / l) * masks[hd]
                lse_tot = lse_tot + (m + jnp.log2(l) - c0) * masks[hd]
            o_ref[q0:q0 + tq, :] = o_tot
            lse_ref[q0:q0 + tq, :] = lse_tot

    blk = lambda off: pl.BlockSpec((seq, LANES), lambda b, p: (b, off + p))
    return _host_call(
        body, rider, name="fox_fwd", grid=(nb, npair),
        in_specs=[blk(0), blk(npair), blk(2 * npair), pl.BlockSpec((1, 2, seq), lambda b, p: (p, 0, b)),
                  pl.BlockSpec((1, LANES), lambda b, p: (0, 0)), pl.BlockSpec((1, LANES), lambda b, p: (0, 0))],
        out_specs=[blk(0), blk(0)],
        out_shape=[jax.ShapeDtypeStruct((t, W_GROUP), F32), jax.ShapeDtypeStruct((t, W_GROUP), F32)],
        scratch_shapes=[pltpu.VMEM((seq, LANES), BF16), pltpu.VMEM((2, seq, LANES), BF16),
                        pltpu.VMEM((2, seq, LANES), BF16)],
        inputs=(proj, proj, proj, c3, gq, gk), semantics=("arbitrary", "arbitrary"))


def _fox_bwd(proj, c3, gq, gk, do, o, lse, nb, seq, rider=None):
    t = nb * seq
    tq = FOX_TQ
    nq = seq // tq
    npair = N_FOX_HEADS // 2

    def body(q_ref, k_ref, v_ref, c_ref, gq_ref, gk_ref, do_ref, o_ref, lse_ref,
             dq_ref, dk_ref, dv_ref, dc_ref, dg_ref, qs, ks, vs, kts, dos, lse_t, delta_t, dqt_acc, dk_acc, dv_acc,
             row_sum):
        @pl.when((pl.program_id(0) == 0) & (pl.program_id(1) == 0))
        def _():
            dg_ref[...] = jnp.zeros_like(dg_ref)

        ones = _group_ones()
        masks = _head_masks()
        qhat, rq = _head_norm(q_ref[...].astype(F32), None, ones)
        khat, rk = _head_norm(k_ref[...].astype(F32), None, ones)
        qs[...] = (qhat * gq_ref[...] * (SCALE * LOG2E)).astype(BF16)
        kn = khat * gk_ref[...]
        vv = v_ref[...].astype(F32)
        for hd in range(2):
            ks[hd] = (kn * masks[hd]).astype(BF16)
            vs[hd] = (vv * masks[hd]).astype(BF16)
            kts[hd] = ks[hd].T
        dof = do_ref[...]
        dos[...] = dof.astype(BF16)
        lse_t[...] = lse_ref[...].T
        delta_t[...] = _groupsum(dof * o_ref[...], ones).T
        dqt_acc[...] = jnp.zeros_like(dqt_acc)
        dk_acc[...] = jnp.zeros_like(dk_acc)
        dv_acc[...] = jnp.zeros_like(dv_acc)
        row_sum[...] = jnp.zeros_like(row_sum)
        key = lax.broadcasted_iota(jnp.int32, (tq, tq), 0)
        qry = lax.broadcasted_iota(jnp.int32, (tq, tq), 1)
        causal = key <= qry

        for hd in range(2):
            lane0 = hd * HEAD_DIM
            for kj in range(nq):
                k0 = kj * tq
                k_blk = ks[hd, k0:k0 + tq, :]
                v_blk = vs[hd, k0:k0 + tq, :]
                kt_blk = kts[hd, :, k0:k0 + tq]
                crow = c_ref[0, hd:hd + 1, k0:k0 + tq] * LOG2E
                ck0 = crow[:, 0:1]
                bias = jnp.broadcast_to(ck0 - crow, (LANES, tq)).T[:, 0:1]

                def queries_step(r0, r1, diag, hd=hd, lane0=lane0, k_blk=k_blk, v_blk=v_blk, kt_blk=kt_blk,
                                 bias=bias, ck0=ck0):
                    q_r = qs[r0:r1, :]
                    do_r = dos[r0:r1, :]
                    z = _dot(k_blk, q_r, NT) + bias
                    p = jnp.exp2(z - (lse_t[lane0:lane0 + 1, r0:r1] + ck0))
                    if diag:
                        p = jnp.where(causal, p, 0.0)
                    dp = _dot(v_blk, do_r, NT)
                    ds = p * (dp - delta_t[lane0:lane0 + 1, r0:r1])
                    dsb = ds.astype(BF16)
                    dqt_acc[:, r0:r1] += _dot(kt_blk, dsb)
                    row_sum[hd:hd + 1, r0:r1] += jnp.sum(ds, axis=0, keepdims=True)
                    return _dot(dsb, q_r), _dot(p.astype(BF16), do_r), -jnp.sum(ds, axis=1, keepdims=True)

                dk_j, dv_j, dc_j = queries_step(k0, k0 + tq, True)
                if k0 + tq < seq:
                    dk_o, dv_o, dc_o = queries_step(k0 + tq, seq, False)
                    dk_j, dv_j, dc_j = dk_j + dk_o, dv_j + dv_o, dc_j + dc_o
                dk_acc[k0:k0 + tq, :] += dk_j * masks[hd]
                dv_acc[k0:k0 + tq, :] += dv_j * masks[hd]
                dc_ref[0, hd:hd + 1, k0:k0 + tq] = jnp.broadcast_to(dc_j, (tq, LANES)).T[0:1, :]

        dc_ref[0] += row_sum[0:2, :]

        dq_raw, dgq = _head_norm_bwd(dqt_acc[...].T * SCALE, qhat, rq, gq_ref[...], ones)
        dk_raw, dgk = _head_norm_bwd(dk_acc[...] * LN2, khat, rk, gk_ref[...], ones)
        dq_ref[...] = dq_raw.astype(BF16)
        dk_ref[...] = dk_raw.astype(BF16)
        dv_ref[...] = dv_acc[...].astype(BF16)
        dg_ref[0:1, :] += dgq
        dg_ref[1:2, :] += dgk

    blk = lambda off: pl.BlockSpec((seq, LANES), lambda b, p: (b, off + p))
    vec = pl.BlockSpec((1, LANES), lambda b, p: (0, 0))
    c_spec = pl.BlockSpec((1, 2, seq), lambda b, p: (p, 0, b))
    return _host_call(
        body, rider, name="fox_bwd", grid=(nb, npair),
        in_specs=[blk(0), blk(npair), blk(2 * npair), c_spec, vec, vec, blk(0), blk(0), blk(0)],
        out_specs=[blk(0), blk(0), blk(0), c_spec, pl.BlockSpec((8, LANES), lambda b, p: (0, 0))],
        out_shape=[jax.ShapeDtypeStruct((t, W_GROUP), BF16), jax.ShapeDtypeStruct((t, W_GROUP), BF16),
                   jax.ShapeDtypeStruct((t, W_GROUP), BF16), jax.ShapeDtypeStruct((npair, 2, t), F32),
                   jax.ShapeDtypeStruct((8, LANES), F32)],
        scratch_shapes=[pltpu.VMEM((seq, LANES), BF16), pltpu.VMEM((2, seq, LANES), BF16),
                        pltpu.VMEM((2, seq, LANES), BF16), pltpu.VMEM((2, LANES, seq), BF16),
                        pltpu.VMEM((seq, LANES), BF16), pltpu.VMEM((LANES, seq), F32),
                        pltpu.VMEM((LANES, seq), F32), pltpu.VMEM((LANES, seq), F32),
                        pltpu.VMEM((seq, LANES), F32), pltpu.VMEM((seq, LANES), F32),
                        pltpu.VMEM((8, seq), F32)],
        inputs=(proj, proj, proj, c3, gq, gk, do, o, lse), semantics=("arbitrary", "arbitrary"))


def _dil_prep(q_ref, k_ref, gq_ref, gk_ref, cos_ref, up_ref, dn_ref, ones):
    qhat, rq = _head_norm(q_ref[...].astype(F32), None, ones)
    khat, rk = _head_norm(k_ref[...].astype(F32), None, ones)
    cos, up, dn = cos_ref[...], up_ref[...], dn_ref[...]
    qn = _rope(qhat * gq_ref[...], cos, up, dn) * (SCALE * LOG2E)
    kn = _rope(khat * gk_ref[...], cos, up, dn)
    return qhat, rq, khat, rk, qn, kn


def _dil_keys(d, seq, pairs):
    nblk = seq // BAND
    per_res = seq // (d * BAND)
    as_blocks = lambda ref, rows: ref[rows, :].reshape(-1, BAND, LANES)
    if per_res == 1:
        a = lax.broadcasted_iota(jnp.int32, (1, BAND, BAND), 1)
        j = lax.broadcasted_iota(jnp.int32, (1, BAND, BAND), 2)
        causal = jnp.where(j <= a, 0.0, NEG)
        return [as_blocks(src, slice(0, seq)) for src, _ in pairs], [causal]
    for src, dst in pairs:
        dst[:, BAND:, :] = as_blocks(src, slice(0, seq))
        dst[1:, :BAND, :] = as_blocks(src, slice(0, seq - BAND))
        dst[0:1, :BAND, :] = jnp.zeros((1, BAND, LANES), BF16)
    a = lax.broadcasted_iota(jnp.int32, (1, BAND, 2 * BAND), 1)
    j = lax.broadcasted_iota(jnp.int32, (1, BAND, 2 * BAND), 2)
    band = jnp.where(((j < BAND) & (j >= a)) | ((j >= BAND) & (j - BAND <= a)), 0.0, NEG)
    e = lax.broadcasted_iota(jnp.int32, (nblk, 1, 2 * BAND), 0)
    j = lax.broadcasted_iota(jnp.int32, (nblk, 1, 2 * BAND), 2)
    no_prev = jnp.where(((e & (per_res - 1)) == 0) & (j < BAND), NEG, 0.0)
    return [dst[...] for _, dst in pairs], [band + no_prev]


def _regroup(d, seq):
    if d == 1:
        return [(slice(0, seq), slice(0, seq))]
    before, n = d // 4, seq // d
    return [(pl.ds(r1 * (seq // before) + r2, n, stride=4), slice((before * r2 + r1) * n, (before * r2 + r1 + 1) * n))
            for r1 in range(before) for r2 in range(4)]


def _dil_fwd(proj, gq, gk, cos, up, dn, nb, seq):
    t = nb * seq
    npair = W_GROUP // LANES
    off = 3 * npair

    def body(q_ref, k_ref, v_ref, gq_ref, gk_ref, cos_ref, up_ref, dn_ref, o_ref, lse_ref,
             src_a, src_b, qp, kp, vp, kw, vw, m_b, l_b, o_b, state_a, state_b):
        ones = _group_ones()
        masks = _head_masks()
        _, _, _, _, qn, kn = _dil_prep(q_ref, k_ref, gq_ref, gk_ref, cos_ref, up_ref, dn_ref, ones)
        src_a[0] = qn
        src_a[1] = kn
        src_a[2] = v_ref[...].astype(F32)
        nblk = seq // BAND
        src, state = (src_a, src_b), (state_a, state_b)

        for d in DILATIONS:
            last = d == DILATIONS[-1]
            for before, after in _regroup(d, seq):
                qv, kv, vv = src[0].at[0][before, :], src[0].at[1][before, :], src[0].at[2][before, :]
                for hd in range(2):
                    qp[hd, after, :] = (qv * masks[hd]).astype(BF16)
                kp[after, :] = kv.astype(BF16)
                vp[after, :] = vv.astype(BF16)
                if d > 1 and not last:
                    src[1][0, after, :], src[1][1, after, :], src[1][2, after, :] = qv, kv, vv
            if d > 1:
                src = src[::-1]
            (keys_k, keys_v), bias = _dil_keys(d, seq, [(kp, kw), (vp, vw)])
            m_t = jnp.zeros((nblk, BAND, LANES), F32)
            l_t = jnp.zeros((nblk, BAND, LANES), F32)
            o_t = jnp.zeros((nblk, BAND, LANES), F32)
            for hd in range(2):
                s = _dot(qp[hd].reshape(nblk, BAND, LANES), keys_k, BATCH_NT)
                for b_ in bias:
                    s = s + b_
                m = jnp.max(s, axis=-1, keepdims=True)
                p = jnp.exp2(s - m)
                m_t = m_t + m * masks[hd]
                l_t = l_t + jnp.sum(p, axis=-1, keepdims=True) * masks[hd]
                o_t = o_t + _dot(p.astype(BF16), keys_v, BATCH_NN) * masks[hd]
            if d == 1:
                state[0][0] = m_t.reshape(seq, LANES)
                state[0][1] = l_t.reshape(seq, LANES)
                state[0][2] = o_t.reshape(seq, LANES)
                continue
            m_b[...] = m_t.reshape(seq, LANES)
            l_b[...] = l_t.reshape(seq, LANES)
            o_b[...] = o_t.reshape(seq, LANES)
            for before, after in _regroup(d, seq):
                m_old = state[0].at[0][before, :]
                m_new = jnp.maximum(m_old, m_b[after, :])
                w_old = jnp.exp2(m_old - m_new)
                w_new = jnp.exp2(m_b[after, :] - m_new)
                state[1][0, after, :] = m_new
                state[1][1, after, :] = state[0].at[1][before, :] * w_old + l_b[after, :] * w_new
                state[1][2, after, :] = state[0].at[2][before, :] * w_old + o_b[after, :] * w_new
            state = state[::-1]

        l = state[0][1]
        o_b[...] = state[0][2] / l
        l_b[...] = state[0][0] + jnp.log2(l)
        held, spare = [o_b, l_b], [m_b, state[1].at[0]]
        for d in DILATIONS[:0:-1]:
            dests = [o_ref, lse_ref] if d == DILATIONS[1] else spare
            for h, dst in zip(held, dests):
                for before, after in _regroup(d, seq):
                    dst[before, :] = h[after, :]
            held, spare = dests, held

    blk = lambda o_: pl.BlockSpec((seq, LANES), lambda b, p: (b, o_ + p))
    vec = pl.BlockSpec((1, LANES), lambda b, p: (0, 0))
    tab = pl.BlockSpec(memory_space=pltpu.VMEM)
    f32_buf = pltpu.VMEM((seq, LANES), F32)
    f32_x3 = pltpu.VMEM((3, seq, LANES), F32)
    bf16_buf = pltpu.VMEM((seq, LANES), BF16)
    window_buf = pltpu.VMEM((seq // BAND, 2 * BAND, LANES), BF16)
    return pl.pallas_call(
        body, name="dil_fwd", grid=(nb, npair),
        in_specs=[blk(off), blk(off + npair), blk(off + 2 * npair), vec, vec, tab, tab, tab],
        out_specs=[blk(0), blk(0)],
        out_shape=[jax.ShapeDtypeStruct((t, W_GROUP), F32), jax.ShapeDtypeStruct((t, W_GROUP), F32)],
        scratch_shapes=[f32_x3, f32_x3, pltpu.VMEM((2, seq, LANES), BF16), bf16_buf, bf16_buf,
                        window_buf, window_buf, f32_buf, f32_buf, f32_buf, f32_x3, f32_x3],
        compiler_params=_params(("arbitrary", "arbitrary")),
    )(proj, proj, proj, gq, gk, cos, up, dn)


def _dil_bwd(proj, gq, gk, cos, up, dn, do, o, lse, nb, seq, rider=None):
    t = nb * seq
    npair = W_GROUP // LANES
    off = 3 * npair

    def body(q_ref, k_ref, v_ref, gq_ref, gk_ref, cos_ref, up_ref, dn_ref, do_ref, o_ref, lse_ref,
             dq_ref, dk_ref, dv_ref, dg_ref, src_a, src_b, sums_a, sums_b,
             qp, kp, vp, dop, kw, vw, lse_p, delta_p, dq_p, dk_p, dv_p):
        @pl.when((pl.program_id(0) == 0) & (pl.program_id(1) == 0))
        def _():
            dg_ref[...] = jnp.zeros_like(dg_ref)

        ones = _group_ones()
        masks = _head_masks()
        qhat, rq, khat, rk, qn, kn = _dil_prep(q_ref, k_ref, gq_ref, gk_ref, cos_ref, up_ref, dn_ref, ones)
        src_a[0] = qn
        src_a[1] = kn
        src_a[2] = v_ref[...].astype(F32)
        src_a[3] = do_ref[...]
        src_a[4] = lse_ref[...]
        src_a[5] = _groupsum(do_ref[...] * o_ref[...], ones)
        nblk = seq // BAND
        src, sums = (src_a, src_b), (sums_a, sums_b)

        for d in DILATIONS:
            last = d == DILATIONS[-1]
            for before, after in _regroup(d, seq):
                planes = [src[0].at[i][before, :] for i in range(6)]
                for hd in range(2):
                    qp[hd, after, :] = (planes[0] * masks[hd]).astype(BF16)
                    dop[hd, after, :] = (planes[3] * masks[hd]).astype(BF16)
                kp[after, :] = planes[1].astype(BF16)
                vp[after, :] = planes[2].astype(BF16)
                lse_p[after, :] = planes[4]
                delta_p[after, :] = planes[5]
                if d > 1 and not last:
                    for i in range(6):
                        src[1][i, after, :] = planes[i]
            if d > 1:
                src = src[::-1]
            (keys_k, keys_v), bias = _dil_keys(d, seq, [(kp, kw), (vp, vw)])
            nk = keys_k.shape[1]
            dq_b = jnp.zeros((nblk, BAND, LANES), F32)
            dk_b = jnp.zeros((nblk, nk, LANES), F32)
            dv_b = jnp.zeros((nblk, nk, LANES), F32)
            for hd in range(2):
                lane0 = hd * HEAD_DIM
                q3 = qp[hd].reshape(nblk, BAND, LANES)
                do3 = dop[hd].reshape(nblk, BAND, LANES)
                z = _dot(q3, keys_k, BATCH_NT)
                for b_ in bias:
                    z = z + b_
                p = jnp.exp2(z - lse_p[...].reshape(nblk, BAND, LANES)[:, :, lane0:lane0 + 1])
                dp = _dot(do3, keys_v, BATCH_NT)
                ds = (p * (dp - delta_p[...].reshape(nblk, BAND, LANES)[:, :, lane0:lane0 + 1])).astype(BF16)
                dq_b = dq_b + _dot(ds, keys_k, BATCH_NN) * masks[hd]
                dk_b = dk_b + _dot(ds, q3, BATCH_TN)
                dv_b = dv_b + _dot(p.astype(BF16), do3, BATCH_TN)
            dq_p[...] = dq_b.reshape(seq, LANES)
            for acc, out in ((dk_b, dk_p), (dv_b, dv_p)):
                out[...] = acc[:, nk - BAND:, :].reshape(seq, LANES)
                if nk > BAND:
                    out[0:seq - BAND, :] += acc[1:, :BAND, :].reshape(seq - BAND, LANES)
            if d == 1:
                sums[0][0], sums[0][1], sums[0][2] = dq_p[...], dk_p[...], dv_p[...]
                continue
            for before, after in _regroup(d, seq):
                for i, part in enumerate((dq_p, dk_p, dv_p)):
                    sums[1][i, after, :] = sums[0].at[i][before, :] + part[after, :]
            sums = sums[::-1]

        for d in DILATIONS[:0:-1]:
            for i in range(3):
                for before, after in _regroup(d, seq):
                    sums[1].at[i][before, :] = sums[0][i, after, :]
            sums = sums[::-1]

        cos, up, dn = cos_ref[...], up_ref[...], dn_ref[...]
        dq_raw, dgq = _head_norm_bwd(_rope_bwd(sums[0][0] * SCALE, cos, up, dn), qhat, rq, gq_ref[...], ones)
        dk_raw, dgk = _head_norm_bwd(_rope_bwd(sums[0][1] * LN2, cos, up, dn), khat, rk, gk_ref[...], ones)
        dq_ref[...] = dq_raw.astype(BF16)
        dk_ref[...] = dk_raw.astype(BF16)
        dv_ref[...] = sums[0][2].astype(BF16)
        dg_ref[0:1, :] += dgq
        dg_ref[1:2, :] += dgk

    blk = lambda o_: pl.BlockSpec((seq, LANES), lambda b, p: (b, o_ + p))
    vec = pl.BlockSpec((1, LANES), lambda b, p: (0, 0))
    tab = pl.BlockSpec(memory_space=pltpu.VMEM)
    f32_buf = pltpu.VMEM((seq, LANES), F32)
    bf16_buf = pltpu.VMEM((seq, LANES), BF16)
    window_buf = pltpu.VMEM((seq // BAND, 2 * BAND, LANES), BF16)
    bf16_pair = pltpu.VMEM((2, seq, LANES), BF16)
    return _host_call(
        body, rider, name="dil_bwd", grid=(nb, npair),
        in_specs=[blk(off), blk(off + npair), blk(off + 2 * npair), vec, vec, tab, tab, tab,
                  blk(0), blk(0), blk(0)],
        out_specs=[blk(0), blk(0), blk(0), pl.BlockSpec((8, LANES), lambda b, p: (0, 0))],
        out_shape=[jax.ShapeDtypeStruct((t, W_GROUP), BF16), jax.ShapeDtypeStruct((t, W_GROUP), BF16),
                   jax.ShapeDtypeStruct((t, W_GROUP), BF16), jax.ShapeDtypeStruct((8, LANES), F32)],
        scratch_shapes=[pltpu.VMEM((6, seq, LANES), F32)] * 2 + [pltpu.VMEM((3, seq, LANES), F32)] * 2
        + [bf16_pair, bf16_buf, bf16_buf, bf16_pair, window_buf, window_buf] + [f32_buf] * 5,
        inputs=(proj, proj, proj, gq, gk, cos, up, dn, do, o, lse), semantics=("arbitrary", "arbitrary"))


def _adamw(w, g, m, v, name, rider=None):
    row_major = w.ndim == 3 and w.shape[1] == 1
    rows, cols = (w.shape[0], w.shape[2]) if row_major else w.shape[-2:]
    if row_major:
        tr = max(t for t in range(1, 65) if rows % t == 0)
    else:
        tr = _row_tile(rows) if rows >= 8 else rows
    c1 = 1.0 - ADAM_B1 ** ADAM_STEP
    c2 = 1.0 - ADAM_B2 ** ADAM_STEP

    def body(w_ref, g_ref, m_ref, v_ref, d_ref, nm_ref, nv_ref):
        g_ = g_ref[...]
        nm = ADAM_B1 * m_ref[...] + (1.0 - ADAM_B1) * g_
        nv = ADAM_B2 * v_ref[...] + (1.0 - ADAM_B2) * (g_ * g_)
        nm_ref[...] = nm
        nv_ref[...] = nv
        d_ref[...] = -ADAM_LR * ((nm / c1) / (jnp.sqrt(nv / c2) + ADAM_EPS) + ADAM_WD * w_ref[...])

    if row_major:
        spec = pl.BlockSpec((tr, 1, cols), lambda i: (i, 0, 0))
    elif w.ndim == 3:
        spec = pl.BlockSpec((1, tr, cols), lambda i: (0, i, 0))
    else:
        spec = pl.BlockSpec((tr, cols), lambda i: (i, 0))
    shape = jax.ShapeDtypeStruct(w.shape, F32)
    return _host_call(
        body, rider, name=name, grid=(rows // tr,), in_specs=[spec] * 4, out_specs=[spec] * 3,
        out_shape=[shape] * 3, scratch_shapes=[], inputs=(w, g, m, v), semantics=("arbitrary",))


def _place():
    x, y, c = lax.axis_index("x"), lax.axis_index("y"), lax.axis_index("c")
    chips = [(1 - x, y), (x, 1 - y), (1 - x, 1 - y)]
    return x, y, c, chips


def _gather_weight(w, name):
    _, rows, cols = w.shape
    half_rows = rows // 2

    def body(w_ref, out_ref, send_sems, recv_sems):
        x, y, c, chips = _place()
        sibling = (x, y, 1 - c)
        mine = 2 * x + y
        lo = pl.multiple_of(c * half_rows, 16)
        lo_sib = pl.multiple_of((1 - c) * half_rows, 16)
        out_ref[mine] = w_ref[0].astype(BF16)

        def copy(k, shard, first_row, to):
            ref = out_ref.at[shard, pl.ds(first_row, half_rows), :]
            return pltpu.make_async_remote_copy(src_ref=ref, dst_ref=ref, send_sem=send_sems.at[k],
                                                recv_sem=recv_sems.at[k], device_id=to, device_id_type=MESH)

        sends = [copy(k, mine, lo, (cx, cy, c)) for k, (cx, cy) in enumerate(chips)]
        for cp in sends:
            cp.start()
        passed = []
        for k, (cx, cy) in enumerate(chips):
            theirs = 2 * cx + cy
            copy(k, theirs, lo, (cx, cy, c)).wait_recv()
            fw = copy(3 + k, theirs, lo, sibling)
            fw.start()
            passed.append(fw)
        for k, (cx, cy) in enumerate(chips):
            copy(3 + k, 2 * cx + cy, lo_sib, sibling).wait_recv()
        for cp in sends + passed:
            cp.wait_send()

    return pl.pallas_call(
        body, name=name,
        in_specs=[pl.BlockSpec(memory_space=pltpu.VMEM)],
        out_specs=pl.BlockSpec(memory_space=pltpu.VMEM),
        out_shape=jax.ShapeDtypeStruct((4, rows, cols), BF16),
        scratch_shapes=[pltpu.SemaphoreType.DMA((6,)), pltpu.SemaphoreType.DMA((6,))],
        compiler_params=pltpu.CompilerParams(vmem_limit_bytes=VMEM_LIMIT),
    )(w)


def _remote(src, dst, sems, k, to):
    send_sems, recv_sems = sems
    return pltpu.make_async_remote_copy(src_ref=src, dst_ref=dst, send_sem=send_sems.at[k], recv_sem=recv_sems.at[k],
                                        device_id=to, device_id_type=MESH)


def _cast_bf16(parts, name):
    def body(*refs):
        for src, dst in zip(refs[:len(parts)], refs[len(parts):]):
            dst[...] = src[0].astype(BF16)

    return pl.pallas_call(
        body, name=name, in_specs=[pl.BlockSpec(memory_space=pltpu.VMEM)] * len(parts),
        out_specs=[pl.BlockSpec(memory_space=pltpu.VMEM)] * len(parts),
        out_shape=[jax.ShapeDtypeStruct(p.shape[1:], BF16) for p in parts],
        compiler_params=pltpu.CompilerParams(vmem_limit_bytes=VMEM_LIMIT),
    )(*parts)


def _gather_rider(shards):
    def copies(ins, outs, sems, which):
        x, y, c, chips = _place()
        sibling = (x, y, 1 - c)
        mine = 2 * x + y
        made = {name: [] for name in which}
        for i, (p_ref, g_ref) in enumerate(zip(ins, outs)):
            half = p_ref.shape[0] // 2
            lo = pl.multiple_of(c * half, 16)
            lo_sib = pl.multiple_of((1 - c) * half, 16)
            spot = lambda shard, first, g_ref=g_ref, half=half: g_ref.at[shard, pl.ds(first, half), :]
            groups = {
                "own": lambda: [pltpu.make_async_copy(p_ref, g_ref.at[mine], sems[0].at[7 * i + 6])],
                "sends": lambda: [_remote(p_ref.at[pl.ds(lo, half), :], spot(mine, lo), sems, 7 * i + k, (cx, cy, c))
                                  for k, (cx, cy) in enumerate(chips)],
                "arrivals": lambda: [_remote(spot(2 * cx + cy, lo), spot(2 * cx + cy, lo), sems, 7 * i + k, (cx, cy, c))
                                     for k, (cx, cy) in enumerate(chips)],
                "passes": lambda: [_remote(spot(2 * cx + cy, lo), spot(2 * cx + cy, lo), sems, 7 * i + 3 + k, sibling)
                                   for k, (cx, cy) in enumerate(chips)],
                "from_sibling": lambda: [_remote(spot(2 * cx + cy, lo_sib), spot(2 * cx + cy, lo_sib), sems,
                                                 7 * i + 3 + k, sibling) for k, (cx, cy) in enumerate(chips)],
            }
            for name in which:
                made[name] += groups[name]()
        return [made[name] for name in which]

    def start(ins, outs, send_sems, recv_sems):
        own, sends = copies(ins, outs, (send_sems, recv_sems), ("own", "sends"))
        for cp in own + sends:
            cp.start()

    def middle(ins, outs, send_sems, recv_sems):
        arrivals, passes = copies(ins, outs, (send_sems, recv_sems), ("arrivals", "passes"))
        for landed, onward in zip(arrivals, passes):
            landed.wait_recv()
            onward.start()

    def finish(ins, outs, send_sems, recv_sems):
        own, sends, passes, from_sibling = copies(ins, outs, (send_sems, recv_sems),
                                                  ("own", "sends", "passes", "from_sibling"))
        for cp in from_sibling:
            cp.wait_recv()
        for cp in sends + passes:
            cp.wait_send()
        for cp in own:
            cp.wait()

    shapes = [jax.ShapeDtypeStruct((4,) + s.shape, BF16) for s in shards]
    return _Rider(shards, shapes, 7 * len(shards), start, finish, middle=middle)


def _exchange_rider(inputs, out_shapes, n_sems, copies, aliases=None):
    def start(ins, outs, send_sems, recv_sems):
        for cp in copies(ins, outs, (send_sems, recv_sems)):
            cp.start()

    def finish(ins, outs, send_sems, recv_sems):
        for cp in copies(ins, outs, (send_sems, recv_sems)):
            cp.wait()

    return _Rider(inputs, out_shapes, n_sems, start, finish, aliases)


def _swap_rider(grads4):
    halves = [g.shape[1] // 2 for g in grads4]

    def copies(ins, outs, sems):
        x, y, c, _ = _place()
        return [_remote(g.at[:, pl.ds(pl.multiple_of((1 - c) * h, 8), h), :], a, sems, i, (x, y, 1 - c))
                for i, (g, a, h) in enumerate(zip(ins, outs, halves))]

    shapes = [jax.ShapeDtypeStruct((4, h, g.shape[2]), F32) for g, h in zip(grads4, halves)]
    return _exchange_rider(grads4, shapes, len(grads4), copies)


def _chip_sum(g4, from_sibling, name):
    _, rows, cols = g4.shape
    half = rows // 2

    def body(g_ref, s_ref, stage_ref, own_ref):
        x, y, c, chips = _place()
        lo = pl.multiple_of(c * half, 8)
        for k, (cx, cy) in enumerate(chips):
            theirs = 2 * cx + cy
            stage_ref[k] = (g_ref[theirs, pl.ds(lo, half), :] + s_ref[theirs]).astype(BF16)
        mine = 2 * x + y
        own_ref[...] = g_ref[mine, pl.ds(lo, half), :] + s_ref[mine]

    return pl.pallas_call(
        body, name=name, in_specs=[pl.BlockSpec(memory_space=pltpu.VMEM)] * 2,
        out_specs=[pl.BlockSpec(memory_space=pltpu.VMEM)] * 2,
        out_shape=[jax.ShapeDtypeStruct((3, half, cols), BF16), jax.ShapeDtypeStruct((half, cols), F32)],
        compiler_params=pltpu.CompilerParams(vmem_limit_bytes=VMEM_LIMIT),
    )(g4, from_sibling)


def _spread_rider(stages):
    def copies(ins, outs, sems):
        _, _, c, chips = _place()
        return [_remote(st.at[k], ld.at[k], sems, 3 * i + k, (cx, cy, c))
                for i, (st, ld) in enumerate(zip(ins, outs)) for k, (cx, cy) in enumerate(chips)]

    shapes = [jax.ShapeDtypeStruct(s.shape, s.dtype) for s in stages]
    return _exchange_rider(stages, shapes, 3 * len(stages), copies)


def _finish_half(own, landed, name):
    half, cols = own.shape

    def body(own_ref, landed_ref, out_ref):
        c = lax.axis_index("c")
        acc = own_ref[...]
        for k in range(3):
            acc = acc + landed_ref[k].astype(F32)
        out_ref[pl.ds(pl.multiple_of(c * half, 8), half), :] = acc

    return pl.pallas_call(
        body, name=name, in_specs=[pl.BlockSpec(memory_space=pltpu.VMEM)] * 2,
        out_specs=pl.BlockSpec(memory_space=pltpu.VMEM),
        out_shape=jax.ShapeDtypeStruct((2 * half, cols), F32),
        compiler_params=pltpu.CompilerParams(vmem_limit_bytes=VMEM_LIMIT),
    )(own, landed)


def _share_rider(fulls):
    def copies(ins, outs, sems):
        x, y, c, _ = _place()
        out = []
        for i, full in enumerate(outs):
            half = full.shape[0] // 2
            rows = full.at[pl.ds(pl.multiple_of(c * half, 8), half), :]
            out.append(_remote(rows, rows, sems, i, (x, y, 1 - c)))
        return out

    def finish_copies(ins, outs, sems):
        x, y, c, _ = _place()
        out = []
        for i, full in enumerate(outs):
            half = full.shape[0] // 2
            mine = full.at[pl.ds(pl.multiple_of(c * half, 8), half), :]
            theirs = full.at[pl.ds(pl.multiple_of((1 - c) * half, 8), half), :]
            out.append((_remote(mine, mine, sems, i, (x, y, 1 - c)), _remote(theirs, theirs, sems, i, (x, y, 1 - c))))
        return out

    def start(ins, outs, send_sems, recv_sems):
        for cp in copies(ins, outs, (send_sems, recv_sems)):
            cp.start()

    def finish(ins, outs, send_sems, recv_sems):
        for sent, landed in finish_copies(ins, outs, (send_sems, recv_sems)):
            sent.wait_send()
            landed.wait_recv()

    shapes = [jax.ShapeDtypeStruct(f.shape, f.dtype) for f in fulls]
    return _Rider(fulls, shapes, len(fulls), start, finish, aliases={i: i for i in range(len(fulls))})


def _all_sum_small(v):
    shape = v.shape

    def body(v_ref, out_ref, buf, send_sems, recv_sems):
        x, y, c, _ = _place()
        me = 4 * x + 2 * y + c
        buf[me] = v_ref[...]
        flips = [(dx, dy, dc) for dx in (0, 1) for dy in (0, 1) for dc in (0, 1)][1:]

        def copy(k, slot, flip):
            dx, dy, dc = flip
            to = (1 - x if dx else x, 1 - y if dy else y, 1 - c if dc else c)
            return pltpu.make_async_remote_copy(src_ref=buf.at[slot], dst_ref=buf.at[slot], send_sem=send_sems.at[k],
                                                recv_sem=recv_sems.at[k], device_id=to, device_id_type=MESH)

        sends = [copy(k, me, flip) for k, flip in enumerate(flips)]
        for cp in sends:
            cp.start()
        for k, (dx, dy, dc) in enumerate(flips):
            sender = 4 * (1 - x if dx else x) + 2 * (1 - y if dy else y) + (1 - c if dc else c)
            copy(k, sender, (dx, dy, dc)).wait_recv()
        for cp in sends:
            cp.wait_send()
        total = buf[0]
        for i in range(1, 8):
            total = total + buf[i]
        out_ref[...] = total

    return pl.pallas_call(
        body, name="all_sum_small",
        in_specs=[pl.BlockSpec(memory_space=pltpu.VMEM)],
        out_specs=pl.BlockSpec(memory_space=pltpu.VMEM),
        out_shape=jax.ShapeDtypeStruct(shape, F32),
        scratch_shapes=[pltpu.VMEM((8,) + shape, F32), pltpu.SemaphoreType.DMA((7,)), pltpu.SemaphoreType.DMA((7,))],
    )(v)


SMALL = (("g_mix", 1024), ("g_ffn", 1024), ("g_out_fox", 512), ("g_out_dil", 512), ("g_q_fox", 64),
         ("g_k_fox", 64), ("g_q_dil", 64), ("g_k_dil", 64), ("b_forget", 8))
SMALL_PACKED = (32, LANES)


def _local_grads(x, target, gains, w1, wft, dense, packed, nb, seq):
    tile2 = lambda g: jnp.tile(g, (1, 2))
    gq_f, gk_f, gq_d, gk_d = (tile2(gains[n]) for n in ("g_q_fox", "g_k_fox", "g_q_dil", "g_k_dil"))
    b_col = gains["b_forget"].reshape(N_FOX_HEADS, 1)
    cos, up, dn = _rope_tables(seq)
    npair = N_FOX_HEADS // 2

    proj, fa_row, h1, h1_t = _in_proj(x, gains["g_mix"], w1, wft)
    c_row = _gate_fwd(fa_row, b_col, seq)
    c3 = c_row.reshape(npair, 2, nb * seq)
    (o_fox, lse_fox), gathered = _fox_fwd(proj, c3, gq_f, gk_f, nb, seq,
                                          rider=None if packed is None else _gather_rider(packed))
    if packed is not None:
        dense = [g.reshape(-1, g.shape[2]) for g in gathered]
    w_out, w_gate, w_up, w_down = dense
    o_dil, lse_dil = _dil_fwd(proj, gq_d, gk_d, cos, up, dn, nb, seq)
    x1, o_n_t = _attn_out(o_fox, o_dil, x, gains["g_out_fox"], gains["g_out_dil"], w_out)
    a, u, dy, loss_parts = _ffn_fwd(x1, target, gains["g_ffn"], w_gate, w_up, w_down)
    loss = jnp.sum(loss_parts[:, 0, 0])

    dx1, s, da, du, h2, dg_ffn = _ffn_bwd(dy, a, u, x1, gains["g_ffn"], w_gate, w_up, w_down)
    d_w_down = _token_matmul(s, dy, "dw_down", 512, False)
    d_w_gate = _token_matmul(da, h2, "dw_gate", 512, False)
    d_w_up = _token_matmul(du, h2, "dw_up", 512, False)
    d_w_out = _token_matmul(o_n_t, dx1, "dw_out", 1024)
    names = ("w_out", "w_gate", "w_up", "w_down")
    grads4 = [g.reshape(4, -1, g.shape[1]) for g in (d_w_out, d_w_gate, d_w_up, d_w_down)]
    exchange = packed is not None
    (do_fox, do_dil, dg_of, dg_od), from_sibling = _attn_out_bwd(
        dx1, o_fox, o_dil, gains["g_out_fox"], gains["g_out_dil"], w_out,
        rider=_swap_rider(grads4) if exchange else None)
    if exchange:
        sums = [_chip_sum(g, s, "chip_sum_" + n) for g, s, n in zip(grads4, from_sibling, names)]
    (dq_f, dk_f, dv_f, dc3, dg_fox), landed = _fox_bwd(
        proj, c3, gq_f, gk_f, do_fox, o_fox, lse_fox, nb, seq,
        rider=_spread_rider([st for st, _ in sums]) if exchange else None)
    if exchange:
        halves = [_finish_half(own, ld, "finish_half_" + n) for (_, own), ld, n in zip(sums, landed, names)]
    (dq_d, dk_d, dv_d, dg_dil), reduced = _dil_bwd(
        proj, gq_d, gk_d, cos, up, dn, do_dil, o_dil, lse_dil, nb, seq,
        rider=_share_rider(halves) if exchange else None)
    if exchange:
        d_w_out, d_w_gate, d_w_up, d_w_down = reduced
    dfa_row, db = _gate_bwd(dc3.reshape(N_FOX_HEADS, nb * seq), fa_row, b_col, seq)
    dparts = [dq_f, dk_f, dv_f, dq_d, dk_d, dv_d]
    d_w1 = _token_matmul_parts(h1_t, dparts, "dw_in")
    d_wf = _row_matmul(dfa_row, h1, "dw_forget")
    fox_w = 3 * W_GROUP
    d_w_in = jnp.concatenate([d_w1[:, :fox_w], d_wf.T, d_w1[:, fox_w:]], axis=1)
    if exchange:
        shards = [_shards_of_columns(d_w_in)]
        _, from_sibling = _idle_host(_swap_rider(shards), "swap_w_in")
        stage, own = _chip_sum(shards[0], from_sibling[0], "chip_sum_w_in")
    (grad_x, dg_mix), landed = _in_proj_bwd(dparts, dfa_row, w1, wft, x, gains["g_mix"], dx1,
                                            rider=_spread_rider([stage]) if exchange else None)
    if exchange:
        d_w_in = _finish_half(own, landed[0], "finish_half_w_in")

    fold = lambda g2: (g2[:, :HEAD_DIM] + g2[:, HEAD_DIM:])
    small = {
        "g_mix": dg_mix[0:1], "g_ffn": dg_ffn[0:1], "g_out_fox": dg_of[0:1], "g_out_dil": dg_od[0:1],
        "g_q_fox": fold(dg_fox[0:1]), "g_k_fox": fold(dg_fox[1:2]),
        "g_q_dil": fold(dg_dil[0:1]), "g_k_dil": fold(dg_dil[1:2]),
        "b_forget": db[:, 0].reshape(1, N_FOX_HEADS),
    }
    big = {"w_in": d_w_in, "w_out": d_w_out, "w_gate": d_w_gate, "w_up": d_w_up, "w_down": d_w_down}
    return loss, grad_x, big, small


def _shards_of_columns(full, n=4):
    r, nc = full.shape
    return full.reshape(r, n, nc // n).transpose(1, 0, 2)


def _columns_of_shards(slabs):
    n, r, c = slabs.shape
    return slabs.transpose(1, 0, 2).reshape(r, n * c)


def kernel(x, g_mix, w_in, b_forget, g_q_fox, g_k_fox, g_q_dil, g_k_dil, g_out_fox, g_out_dil, w_out, g_ffn, w_gate, w_up, w_down, loss_target, m_g_mix, m_w_in, m_b_forget, m_g_q_fox, m_g_k_fox, m_g_q_dil, m_g_k_dil, m_g_out_fox, m_g_out_dil, m_w_out, m_g_ffn, m_w_gate, m_w_up, m_w_down, v_g_mix, v_w_in, v_b_forget, v_g_q_fox, v_g_k_fox, v_g_q_dil, v_g_k_dil, v_g_out_fox, v_g_out_dil, v_w_out, v_g_ffn, v_w_gate, v_w_up, v_w_down):
    nb, seq, d = x.shape
    weights = dict(g_mix=g_mix, w_in=w_in, b_forget=b_forget, g_q_fox=g_q_fox, g_k_fox=g_k_fox, g_q_dil=g_q_dil,
                   g_k_dil=g_k_dil, g_out_fox=g_out_fox, g_out_dil=g_out_dil, w_out=w_out, g_ffn=g_ffn,
                   w_gate=w_gate, w_up=w_up, w_down=w_down)
    m_in = dict(g_mix=m_g_mix, w_in=m_w_in, b_forget=m_b_forget, g_q_fox=m_g_q_fox, g_k_fox=m_g_k_fox,
                g_q_dil=m_g_q_dil, g_k_dil=m_g_k_dil, g_out_fox=m_g_out_fox, g_out_dil=m_g_out_dil, w_out=m_w_out,
                g_ffn=m_g_ffn, w_gate=m_w_gate, w_up=m_w_up, w_down=m_w_down)
    v_in = dict(g_mix=v_g_mix, w_in=v_w_in, b_forget=v_b_forget, g_q_fox=v_g_q_fox, g_k_fox=v_g_k_fox,
                g_q_dil=v_g_q_dil, g_k_dil=v_g_k_dil, g_out_fox=v_g_out_fox, g_out_dil=v_g_out_dil, w_out=v_w_out,
                g_ffn=v_g_ffn, w_gate=v_w_gate, w_up=v_w_up, w_down=v_w_down)
    order = ["g_mix", "w_in", "b_forget", "g_q_fox", "g_k_fox", "g_q_dil", "g_k_dil", "g_out_fox", "g_out_dil",
             "w_out", "g_ffn", "w_gate", "w_up", "w_down"]

    w_in_full = _columns_of_shards(_gather_weight(w_in, "gather_w_in"))
    fox_w = 3 * W_GROUP
    w1 = jnp.concatenate([w_in_full[:, :fox_w], w_in_full[:, fox_w + N_FOX_HEADS:]], axis=1)
    wft = w_in_full[:, fox_w:fox_w + N_FOX_HEADS].T
    swap = lambda a: jnp.transpose(a, (0, 2, 1))
    for n in ("w_gate", "w_up"):
        weights[n], m_in[n], v_in[n] = swap(weights[n]), swap(m_in[n]), swap(v_in[n])
    shards = _cast_bf16([weights[n] for n in ("w_out", "w_gate", "w_up", "w_down")], "cast_shards")

    gains = {n: weights[n] for n, _ in SMALL}
    loss, grad_x, big, small = _local_grads(
        x.reshape(nb * seq, d), loss_target.reshape(nb * seq, d), gains, w1, wft, None, shards, nb, seq)

    grads = {n: big[n][None] for n in ("w_out", "w_gate", "w_up", "w_down")}
    packed = jnp.concatenate([small[n].reshape(-1) for n, _ in SMALL] + [loss.reshape(1)])
    packed = jnp.pad(packed, (0, SMALL_PACKED[0] * SMALL_PACKED[1] - packed.shape[0])).reshape(SMALL_PACKED)
    summed = _all_sum_small(packed).reshape(-1)
    pos = 0
    for n, size in SMALL:
        grads[n] = summed[pos:pos + size].reshape(1, size)
        pos += size
    loss = summed[pos]

    to_entry = lambda a: jnp.transpose(a, (2, 0, 1))
    deltas, new_m, new_v, grad_out = {}, {}, {}, {}
    for n in ["w_down"] + [n for n in order if n != "w_down"]:
        rider = _share_rider([big["w_in"]]) if n == "w_down" else None
        (deltas[n], new_m[n], new_v[n]), shared = _adamw(weights[n], grads[n], m_in[n], v_in[n], "adamw_" + n, rider)
        if rider is not None:
            grads["w_in"] = to_entry(shared[0][None])
            weights["w_in"], m_in["w_in"], v_in["w_in"] = (to_entry(a) for a in (w_in, m_w_in, v_w_in))
        grad_out[n] = grads[n]
    for n in ("w_gate", "w_up"):
        grad_out[n], deltas[n], new_m[n], new_v[n] = (swap(a) for a in (grad_out[n], deltas[n], new_m[n], new_v[n]))
    from_entry = lambda a: jnp.transpose(a, (1, 2, 0))
    grad_out["w_in"], deltas["w_in"], new_m["w_in"], new_v["w_in"] = (
        from_entry(a) for a in (grad_out["w_in"], deltas["w_in"], new_m["w_in"], new_v["w_in"]))

    return (loss, grad_x.reshape(nb, seq, d), *[grad_out[n] for n in order], *[deltas[n] for n in order],
            *[new_m[n] for n in order], *[new_v[n] for n in order])
```

```python
import functools
import math

import numpy as np
import jax
import jax.numpy as jnp
from jax import lax
from jax.experimental import pallas as pl
from jax.experimental.pallas import tpu as pltpu

F32, BF16 = jnp.float32, jnp.bfloat16
MESH = pl.DeviceIdType.MESH

EPS = 1e-6
NEG = -1e30
HEAD_DIM = 64
SCALE = HEAD_DIM ** -0.5
LOG2E = math.log2(math.e)
LN2 = math.log(2.0)
ROPE_THETA = 500000.0
ROPE_DIM = HEAD_DIM // 4
LANES = 128
W_GROUP = 512
N_FOX_HEADS = 8
VMEM_LIMIT = 56 * 1024 * 1024
DILATIONS = (1, 4, 16)
BAND = 128

ADAM_LR, ADAM_B1, ADAM_B2, ADAM_EPS, ADAM_WD, ADAM_STEP = 0.001, 0.9, 0.999, 1e-08, 0.01, 10

NT = (((1,), (1,)), ((), ()))
TN = (((0,), (0,)), ((), ()))
BATCH_NT = (((2,), (2,)), ((0,), (0,)))
BATCH_NN = (((2,), (1,)), ((0,), (0,)))
BATCH_TN = (((1,), (1,)), ((0,), (0,)))


def _params(sem=None):
    return pltpu.CompilerParams(dimension_semantics=sem, vmem_limit_bytes=VMEM_LIMIT)


def _dot(a, b, dims=None):
    if dims is None:
        return jnp.dot(a, b, preferred_element_type=F32)
    return lax.dot_general(a, b, dims, preferred_element_type=F32)


def _group_ones():
    i = lax.broadcasted_iota(jnp.int32, (LANES, LANES), 0) >> 6
    j = lax.broadcasted_iota(jnp.int32, (LANES, LANES), 1) >> 6
    return (i == j).astype(BF16)


def _split3(x):
    a = x.astype(BF16)
    r = x - a.astype(F32)
    b = r.astype(BF16)
    c = (r - b.astype(F32)).astype(BF16)
    return a, b, c


def _groupsum(x, ones, pieces=2):
    total = None
    for _ in range(pieces):
        piece = x.astype(BF16)
        part = _dot(piece, ones)
        total = part if total is None else total + part
        x = x - piece.astype(F32)
    return total


def _head_masks():
    lane = lax.broadcasted_iota(jnp.int32, (1, LANES), 1)
    return [(lane < HEAD_DIM).astype(F32), (lane >= HEAD_DIM).astype(F32)]


def _head_norm(raw, gain, ones):
    r = lax.rsqrt(_groupsum(raw * raw, ones, 1) * (1.0 / HEAD_DIM) + EPS)
    return raw * r, r


def _head_norm_bwd(dy, xhat, r, gain, ones):
    u = dy * gain
    dgain = jnp.sum(dy * xhat, axis=0, keepdims=True)
    draw = r * (u - xhat * (_groupsum(u * xhat, ones) * (1.0 / HEAD_DIM)))
    return draw, dgain


def _rope(x, cos, s_up, s_dn):
    return x * cos + pltpu.roll(x, LANES - 8, 1) * s_up + pltpu.roll(x, 8, 1) * s_dn


def _rope_bwd(dy, cos, s_up, s_dn):
    return dy * cos + pltpu.roll(dy * s_up, 8, 1) + pltpu.roll(dy * s_dn, LANES - 8, 1)


def _rope_tables(seq):
    half = ROPE_DIM // 2
    inv_freq = jnp.power(jnp.float32(ROPE_THETA), -jnp.arange(half, dtype=F32) * 2.0 / ROPE_DIM)
    ang = jnp.arange(seq).astype(F32)[:, None] * inv_freq[None, :]
    cos, sin = jnp.cos(ang), jnp.sin(ang)
    one = jnp.ones((seq, HEAD_DIM - ROPE_DIM), F32)
    zero_h = jnp.zeros((seq, half), F32)
    zero_r = jnp.zeros((seq, HEAD_DIM - ROPE_DIM), F32)
    c = jnp.concatenate([cos, cos, one], axis=1)
    up = jnp.concatenate([-sin, zero_h, zero_r], axis=1)
    dn = jnp.concatenate([zero_h, sin, zero_r], axis=1)
    return jnp.tile(c, (1, 2)), jnp.tile(up, (1, 2)), jnp.tile(dn, (1, 2))


def _row_tile(rows, cap=256):
    best = rows
    for t in range(8, min(rows, cap) + 1, 8):
        if rows % t == 0:
            best = t
    return best


class _Rider:
    def __init__(self, inputs, out_shapes, n_sems, start, finish, aliases=None, middle=None):
        self.inputs, self.out_shapes, self.n_sems = list(inputs), list(out_shapes), n_sems
        self.start, self.finish, self.middle, self.aliases = start, finish, middle, dict(aliases or {})


def _host_call(body, rider, *, name, grid, in_specs, out_specs, out_shape, scratch_shapes, inputs, semantics):
    if rider is None:
        return pl.pallas_call(body, name=name, grid=grid, in_specs=in_specs, out_specs=out_specs,
                              out_shape=out_shape, scratch_shapes=scratch_shapes,
                              compiler_params=_params(semantics))(*inputs), []
    n_in, n_out, n_scr = len(in_specs), len(out_specs), len(scratch_shapes)
    r_in, r_out = len(rider.inputs), len(rider.out_shapes)

    def wrapped(*refs):
        ins, refs = refs[:n_in], refs[n_in:]
        r_ins, refs = refs[:r_in], refs[r_in:]
        outs, refs = refs[:n_out], refs[n_out:]
        r_outs, refs = refs[:r_out], refs[r_out:]
        scratch, (send_sems, recv_sems) = refs[:n_scr], refs[n_scr:]
        ids = [pl.program_id(a) for a in range(len(grid))]
        first = functools.reduce(lambda p, q: p & q, [i == 0 for i in ids])
        last = functools.reduce(lambda p, q: p & q, [i == g - 1 for i, g in zip(ids, grid)])

        @pl.when(first)
        def _():
            rider.start(r_ins, r_outs, send_sems, recv_sems)

        body(*ins, *outs, *scratch)

        if rider.middle is not None:
            step, steps = ids[0], grid[0]
            for i, g in zip(ids[1:], grid[1:]):
                step, steps = step * g + i, steps * g

            @pl.when(step == (3 * steps) // 4)
            def _():
                rider.middle(r_ins, r_outs, send_sems, recv_sems)

        @pl.when(last)
        def _():
            rider.finish(r_ins, r_outs, send_sems, recv_sems)

    hbm = pl.BlockSpec(memory_space=pl.ANY)
    res = pl.pallas_call(
        wrapped, name=name, grid=grid,
        in_specs=list(in_specs) + [hbm] * r_in, out_specs=list(out_specs) + [hbm] * r_out,
        out_shape=list(out_shape) + rider.out_shapes,
        scratch_shapes=list(scratch_shapes) + [pltpu.SemaphoreType.DMA((rider.n_sems,))] * 2,
        input_output_aliases={n_in + i: n_out + o for i, o in rider.aliases.items()},
        compiler_params=_params(semantics),
    )(*inputs, *rider.inputs)
    return res[:n_out], res[n_out:]


def _idle_host(rider, name):
    def body(o_ref):
        o_ref[...] = jnp.zeros_like(o_ref)

    return _host_call(body, rider, name=name, grid=(1,), in_specs=[],
                      out_specs=[pl.BlockSpec((8, LANES), lambda i: (0, 0))],
                      out_shape=[jax.ShapeDtypeStruct((8, LANES), F32)], scratch_shapes=[], inputs=(),
                      semantics=("arbitrary",))


def _in_proj(x, g_mix, w1, wft):
    t, d = x.shape
    n = w1.shape[1]
    tt = 512

    def body(x_ref, g_ref, w_ref, wf_ref, p_ref, fa_ref, h_ref, ht_ref):
        xx = x_ref[...]
        r = lax.rsqrt(jnp.mean(xx * xx, axis=-1, keepdims=True) + EPS)
        h = (xx * r * g_ref[...]).astype(BF16)
        h_ref[...] = h
        ht_ref[...] = h.T
        for j in range(n // W_GROUP):
            cols = slice(j * W_GROUP, (j + 1) * W_GROUP)
            p_ref[:, cols] = _dot(h, w_ref[:, cols]).astype(BF16)
        fa_ref[...] = _dot(wf_ref[...], h, NT)

    return pl.pallas_call(
        body, name="in_proj", grid=(t // tt,),
        in_specs=[pl.BlockSpec((tt, d), lambda i: (i, 0)), pl.BlockSpec((1, d), lambda i: (0, 0)),
                  pl.BlockSpec(memory_space=pltpu.VMEM), pl.BlockSpec(memory_space=pltpu.VMEM)],
        out_specs=[pl.BlockSpec((tt, n), lambda i: (i, 0)), pl.BlockSpec((8, tt), lambda i: (0, i)),
                   pl.BlockSpec((tt, d), lambda i: (i, 0)), pl.BlockSpec((d, tt), lambda i: (0, i))],
        out_shape=[jax.ShapeDtypeStruct((t, n), BF16), jax.ShapeDtypeStruct((8, t), F32),
                   jax.ShapeDtypeStruct((t, d), BF16), jax.ShapeDtypeStruct((d, t), BF16)],
        compiler_params=_params(("arbitrary",)),
    )(x, g_mix, w1, wft)


def _tri(n, upper):
    i = lax.broadcasted_iota(jnp.int32, (n, n), 0)
    j = lax.broadcasted_iota(jnp.int32, (n, n), 1)
    return ((i <= j) if upper else (i >= j)).astype(BF16)


def _gate_fwd(fa_row, b_col, seq):
    t = fa_row.shape[1]
    cb = 256

    def body(fa_ref, b_ref, c_ref):
        tri = _tri(cb, True)
        carry = jnp.zeros((8, 1), F32)
        for k in range(seq // cb):
            z = fa_ref[:, k * cb:(k + 1) * cb] + b_ref[...]
            lf = jnp.minimum(z, 0.0) - jnp.log(1.0 + jnp.exp(-jnp.abs(z)))
            a, b, c = _split3(lf)
            blk = _dot(a, tri) + _dot(b, tri) + _dot(c, tri) + carry
            c_ref[:, k * cb:(k + 1) * cb] = blk
            carry = blk[:, cb - 1:cb]

    return pl.pallas_call(
        body, name="gate_fwd", grid=(t // seq,),
        in_specs=[pl.BlockSpec((8, seq), lambda i: (0, i)), pl.BlockSpec((8, 1), lambda i: (0, 0))],
        out_specs=pl.BlockSpec((8, seq), lambda i: (0, i)),
        out_shape=jax.ShapeDtypeStruct((8, t), F32),
        compiler_params=_params(("arbitrary",)),
    )(fa_row, b_col)


def _gate_bwd(dc_row, fa_row, b_col, seq):
    t = fa_row.shape[1]
    cb = 256

    def body(dc_ref, fa_ref, b_ref, dfa_ref, db_ref):
        @pl.when(pl.program_id(0) == 0)
        def _():
            db_ref[...] = jnp.zeros_like(db_ref)

        tri = _tri(cb, False)
        carry = jnp.zeros((8, 1), F32)
        dbs = jnp.zeros((8, 1), F32)
        for k in reversed(range(seq // cb)):
            a, b, c = _split3(dc_ref[:, k * cb:(k + 1) * cb])
            dlf = _dot(a, tri) + _dot(b, tri) + _dot(c, tri) + carry
            carry = dlf[:, 0:1]
            z = fa_ref[:, k * cb:(k + 1) * cb] + b_ref[...]
            dfa = dlf / (1.0 + jnp.exp(z))
            dfa_ref[:, k * cb:(k + 1) * cb] = dfa
            dbs = dbs + jnp.sum(dfa, axis=1, keepdims=True)
        db_ref[...] += jnp.broadcast_to(dbs, (8, LANES))

    return pl.pallas_call(
        body, name="gate_bwd", grid=(t // seq,),
        in_specs=[pl.BlockSpec((8, seq), lambda i: (0, i)), pl.BlockSpec((8, seq), lambda i: (0, i)),
                  pl.BlockSpec((8, 1), lambda i: (0, 0))],
        out_specs=[pl.BlockSpec((8, seq), lambda i: (0, i)), pl.BlockSpec((8, LANES), lambda i: (0, 0))],
        out_shape=[jax.ShapeDtypeStruct((8, t), F32), jax.ShapeDtypeStruct((8, LANES), F32)],
        compiler_params=_params(("arbitrary",)),
    )(dc_row, fa_row, b_col)


def _attn_out(o_fox, o_dil, x, g_fox, g_dil, w_out):
    t, d = x.shape
    w = o_fox.shape[1]
    tt = 512

    def body(of_ref, od_ref, x_ref, gf_ref, gd_ref, w_ref, x1_ref, ont_ref):
        acc = x_ref[...]
        for k, (o_ref, g_ref) in enumerate(((of_ref, gf_ref), (od_ref, gd_ref))):
            o = o_ref[...]
            r = lax.rsqrt(jnp.mean(o * o, axis=-1, keepdims=True) + EPS)
            on = (o * r * g_ref[...]).astype(BF16)
            ont_ref[k * w:(k + 1) * w, :] = on.T
            acc = acc + _dot(on, w_ref[k * w:(k + 1) * w, :])
        x1_ref[...] = acc

    return pl.pallas_call(
        body, name="attn_out", grid=(t // tt,),
        in_specs=[pl.BlockSpec((tt, w), lambda i: (i, 0)), pl.BlockSpec((tt, w), lambda i: (i, 0)),
                  pl.BlockSpec((tt, d), lambda i: (i, 0)), pl.BlockSpec((1, w), lambda i: (0, 0)),
                  pl.BlockSpec((1, w), lambda i: (0, 0)), pl.BlockSpec(memory_space=pltpu.VMEM)],
        out_specs=[pl.BlockSpec((tt, d), lambda i: (i, 0)), pl.BlockSpec((2 * w, tt), lambda i: (0, i))],
        out_shape=[jax.ShapeDtypeStruct((t, d), F32), jax.ShapeDtypeStruct((2 * w, t), BF16)],
        compiler_params=_params(("arbitrary",)),
    )(o_fox, o_dil, x, g_fox, g_dil, w_out)


def _attn_out_bwd(dx1, o_fox, o_dil, g_fox, g_dil, w_out, rider=None):
    t, d = dx1.shape
    w = o_fox.shape[1]
    tt = 512

    def body(dx_ref, of_ref, od_ref, gf_ref, gd_ref, w_ref, dof_ref, dod_ref, dgf_ref, dgd_ref):
        @pl.when(pl.program_id(0) == 0)
        def _():
            dgf_ref[...] = jnp.zeros_like(dgf_ref)
            dgd_ref[...] = jnp.zeros_like(dgd_ref)

        dxb = dx_ref[...].astype(BF16)
        for k, (o_ref, g_ref, do_ref, dg_ref) in enumerate(
                ((of_ref, gf_ref, dof_ref, dgf_ref), (od_ref, gd_ref, dod_ref, dgd_ref))):
            don = _dot(dxb, w_ref[k * w:(k + 1) * w, :], NT)
            o = o_ref[...]
            r = lax.rsqrt(jnp.mean(o * o, axis=-1, keepdims=True) + EPS)
            xhat = o * r
            u = don * g_ref[...]
            do_ref[...] = r * (u - xhat * jnp.mean(u * xhat, axis=-1, keepdims=True))
            dg_ref[0:1, :] += jnp.sum(don * xhat, axis=0, keepdims=True)

    return _host_call(
        body, rider, name="attn_out_bwd", grid=(t // tt,),
        in_specs=[pl.BlockSpec((tt, d), lambda i: (i, 0)), pl.BlockSpec((tt, w), lambda i: (i, 0)),
                  pl.BlockSpec((tt, w), lambda i: (i, 0)), pl.BlockSpec((1, w), lambda i: (0, 0)),
                  pl.BlockSpec((1, w), lambda i: (0, 0)), pl.BlockSpec(memory_space=pltpu.VMEM)],
        out_specs=[pl.BlockSpec((tt, w), lambda i: (i, 0)), pl.BlockSpec((tt, w), lambda i: (i, 0)),
                   pl.BlockSpec((8, w), lambda i: (0, 0)), pl.BlockSpec((8, w), lambda i: (0, 0))],
        out_shape=[jax.ShapeDtypeStruct((t, w), F32), jax.ShapeDtypeStruct((t, w), F32),
                   jax.ShapeDtypeStruct((8, w), F32), jax.ShapeDtypeStruct((8, w), F32)],
        scratch_shapes=[], inputs=(dx1, o_fox, o_dil, g_fox, g_dil, w_out), semantics=("arbitrary",))


def _ffn_fwd(x1, target, g_ffn, w_gate, w_up, w_down):
    t, d = x1.shape
    f = w_gate.shape[0]
    tt = 256

    def body(x_ref, t_ref, g_ref, wg_ref, wu_ref, wd_ref, a_ref, u_ref, dy_ref, loss_ref):
        xx = x_ref[...]
        r = lax.rsqrt(jnp.mean(xx * xx, axis=-1, keepdims=True) + EPS)
        h = (xx * r * g_ref[...]).astype(BF16)
        a = _dot(h, wg_ref[...], NT)
        u = _dot(h, wu_ref[...], NT)
        a_ref[...] = a.astype(BF16)
        u_ref[...] = u.astype(BF16)
        s = (a / (1.0 + jnp.exp(-a)) * u).astype(BF16)
        y = xx + _dot(s, wd_ref[...])
        e = y - t_ref[...]
        dy_ref[...] = e * (1.0 / d)
        loss_ref[...] = jnp.broadcast_to(0.5 * jnp.sum(e * e) * (1.0 / d), (1, 8, LANES))

    return pl.pallas_call(
        body, name="ffn_fwd", grid=(t // tt,),
        in_specs=[pl.BlockSpec((tt, d), lambda i: (i, 0)), pl.BlockSpec((tt, d), lambda i: (i, 0)),
                  pl.BlockSpec((1, d), lambda i: (0, 0)), pl.BlockSpec(memory_space=pltpu.VMEM),
                  pl.BlockSpec(memory_space=pltpu.VMEM), pl.BlockSpec(memory_space=pltpu.VMEM)],
        out_specs=[pl.BlockSpec((tt, f), lambda i: (i, 0)), pl.BlockSpec((tt, f), lambda i: (i, 0)),
                   pl.BlockSpec((tt, d), lambda i: (i, 0)), pl.BlockSpec((1, 8, LANES), lambda i: (i, 0, 0))],
        out_shape=[jax.ShapeDtypeStruct((t, f), BF16), jax.ShapeDtypeStruct((t, f), BF16),
                   jax.ShapeDtypeStruct((t, d), F32), jax.ShapeDtypeStruct((t // tt, 8, LANES), F32)],
        compiler_params=_params(("arbitrary",)),
    )(x1, target, g_ffn, w_gate, w_up, w_down)


def _ffn_bwd(dy, a, u, x1, g_ffn, w_gate, w_up, w_down):
    t, d = x1.shape
    f = w_gate.shape[0]
    tt = 256

    def body(dy_ref, a_ref, u_ref, x_ref, g_ref, wg_ref, wu_ref, wd_ref,
             dx_ref, s_ref, da_ref, du_ref, h_ref, dg_ref):
        @pl.when(pl.program_id(0) == 0)
        def _():
            dg_ref[...] = jnp.zeros_like(dg_ref)

        dy_ = dy_ref[...]
        ds = _dot(dy_.astype(BF16), wd_ref[...], NT)
        a_ = a_ref[...].astype(F32)
        u_ = u_ref[...].astype(F32)
        sig = 1.0 / (1.0 + jnp.exp(-a_))
        silu = a_ * sig
        s_ref[...] = (silu * u_).astype(BF16)
        da = (ds * u_ * (sig * (1.0 + a_ * (1.0 - sig)))).astype(BF16)
        du = (ds * silu).astype(BF16)
        da_ref[...] = da
        du_ref[...] = du
        dh = _dot(da, wg_ref[...]) + _dot(du, wu_ref[...])
        xx = x_ref[...]
        r = lax.rsqrt(jnp.mean(xx * xx, axis=-1, keepdims=True) + EPS)
        xhat = xx * r
        g = g_ref[...]
        h_ref[...] = (xhat * g).astype(BF16)
        uu = dh * g
        dx_ref[...] = dy_ + r * (uu - xhat * jnp.mean(uu * xhat, axis=-1, keepdims=True))
        dg_ref[0:1, :] += jnp.sum(dh * xhat, axis=0, keepdims=True)

    return pl.pallas_call(
        body, name="ffn_bwd", grid=(t // tt,),
        in_specs=[pl.BlockSpec((tt, d), lambda i: (i, 0)), pl.BlockSpec((tt, f), lambda i: (i, 0)),
                  pl.BlockSpec((tt, f), lambda i: (i, 0)), pl.BlockSpec((tt, d), lambda i: (i, 0)),
                  pl.BlockSpec((1, d), lambda i: (0, 0)), pl.BlockSpec(memory_space=pltpu.VMEM),
                  pl.BlockSpec(memory_space=pltpu.VMEM), pl.BlockSpec(memory_space=pltpu.VMEM)],
        out_specs=[pl.BlockSpec((tt, d), lambda i: (i, 0)), pl.BlockSpec((tt, f), lambda i: (i, 0)),
                   pl.BlockSpec((tt, f), lambda i: (i, 0)), pl.BlockSpec((tt, f), lambda i: (i, 0)),
                   pl.BlockSpec((tt, d), lambda i: (i, 0)), pl.BlockSpec((8, d), lambda i: (0, 0))],
        out_shape=[jax.ShapeDtypeStruct((t, d), F32), jax.ShapeDtypeStruct((t, f), BF16),
                   jax.ShapeDtypeStruct((t, f), BF16), jax.ShapeDtypeStruct((t, f), BF16),
                   jax.ShapeDtypeStruct((t, d), BF16), jax.ShapeDtypeStruct((8, d), F32)],
        compiler_params=_params(("arbitrary",)),
    )(dy, a, u, x1, g_ffn, w_gate, w_up, w_down)


def _in_proj_bwd(dparts, dfa_row, w1, wft, x, g_mix, dx1, rider=None):
    t, d = x.shape
    tt = 512
    npart = len(dparts)

    def body(*refs):
        dp_refs = refs[:npart]
        dfa_ref, w_ref, wf_ref, x_ref, g_ref, dx1_ref, dx_ref, dg_ref = refs[npart:]

        @pl.when(pl.program_id(0) == 0)
        def _():
            dg_ref[...] = jnp.zeros_like(dg_ref)

        dh = _dot(dfa_ref[...].astype(BF16), wf_ref[...], TN)
        for j in range(npart):
            dh = dh + _dot(dp_refs[j][...], w_ref[:, j * W_GROUP:(j + 1) * W_GROUP], NT)
        xx = x_ref[...]
        r = lax.rsqrt(jnp.mean(xx * xx, axis=-1, keepdims=True) + EPS)
        xhat = xx * r
        uu = dh * g_ref[...]
        dx_ref[...] = dx1_ref[...] + r * (uu - xhat * jnp.mean(uu * xhat, axis=-1, keepdims=True))
        dg_ref[0:1, :] += jnp.sum(dh * xhat, axis=0, keepdims=True)

    return _host_call(
        body, rider, name="in_proj_bwd", grid=(t // tt,),
        in_specs=[pl.BlockSpec((tt, W_GROUP), lambda i: (i, 0)) for _ in range(npart)]
        + [pl.BlockSpec((8, tt), lambda i: (0, i)), pl.BlockSpec(memory_space=pltpu.VMEM),
           pl.BlockSpec(memory_space=pltpu.VMEM), pl.BlockSpec((tt, d), lambda i: (i, 0)),
           pl.BlockSpec((1, d), lambda i: (0, 0)), pl.BlockSpec((tt, d), lambda i: (i, 0))],
        out_specs=[pl.BlockSpec((tt, d), lambda i: (i, 0)), pl.BlockSpec((8, d), lambda i: (0, 0))],
        out_shape=[jax.ShapeDtypeStruct((t, d), F32), jax.ShapeDtypeStruct((8, d), F32)],
        scratch_shapes=[], inputs=(*dparts, dfa_row, w1, wft, x, g_mix, dx1), semantics=("arbitrary",))


def _token_matmul(a, b, name, tn, a_is_transposed=True):
    m, t = a.shape if a_is_transposed else a.shape[::-1]
    n = b.shape[1]
    tk = 1024

    def body(a_ref, b_ref, o_ref):
        @pl.when(pl.program_id(1) == 0)
        def _():
            o_ref[...] = jnp.zeros_like(o_ref)

        o_ref[...] += _dot(a_ref[...], b_ref[...].astype(BF16), None if a_is_transposed else TN)

    a_spec = pl.BlockSpec((m, tk), lambda j, k: (0, k)) if a_is_transposed else pl.BlockSpec((tk, m), lambda j, k: (k, 0))
    return pl.pallas_call(
        body, name=name, grid=(n // tn, t // tk),
        in_specs=[a_spec, pl.BlockSpec((tk, tn), lambda j, k: (k, j))],
        out_specs=pl.BlockSpec((m, tn), lambda j, k: (0, j)),
        out_shape=jax.ShapeDtypeStruct((m, n), F32),
        compiler_params=_params(("arbitrary", "arbitrary")),
    )(a, b)


def _token_matmul_parts(at, parts, name):
    m, t = at.shape
    widths = [p.shape[1] for p in parts]
    tk = 1024

    def body(a_ref, *refs):
        o_ref = refs[-1]

        @pl.when(pl.program_id(0) == 0)
        def _():
            o_ref[...] = jnp.zeros_like(o_ref)

        a, first = a_ref[...], 0
        for b_ref, w in zip(refs[:-1], widths):
            o_ref[:, first:first + w] += _dot(a, b_ref[...])
            first += w

    return pl.pallas_call(
        body, name=name, grid=(t // tk,),
        in_specs=[pl.BlockSpec((m, tk), lambda k: (0, k))] + [pl.BlockSpec((tk, w), lambda k: (k, 0)) for w in widths],
        out_specs=pl.BlockSpec((m, sum(widths)), lambda k: (0, 0)),
        out_shape=jax.ShapeDtypeStruct((m, sum(widths)), F32),
        compiler_params=_params(("arbitrary",)),
    )(at, *parts)


def _row_matmul(a_row, b, name):
    t, n = b.shape
    tk = 1024
    nk = t // tk

    def body(a_ref, b_ref, o_ref):
        @pl.when(pl.program_id(0) == 0)
        def _():
            o_ref[...] = jnp.zeros_like(o_ref)

        o_ref[...] += _dot(a_ref[...].astype(BF16), b_ref[...])

    return pl.pallas_call(
        body, name=name, grid=(nk,),
        in_specs=[pl.BlockSpec((8, tk), lambda k: (0, k)), pl.BlockSpec((tk, n), lambda k: (k, 0))],
        out_specs=pl.BlockSpec((8, n), lambda k: (0, 0)),
        out_shape=jax.ShapeDtypeStruct((8, n), F32),
        compiler_params=_params(("arbitrary",)),
    )(a_row, b)


FOX_TQ = 256
SUM_LANE = (HEAD_DIM, 0)


def _fox_fwd(proj, c3, gq, gk, nb, seq, rider=None):
    t = nb * seq
    tq = FOX_TQ
    nq = seq // tq
    npair = N_FOX_HEADS // 2

    def body(q_ref, k_ref, v_ref, c_ref, gq_ref, gk_ref, o_ref, lse_ref, qs, ks, vs):
        ones = _group_ones()
        masks = _head_masks()
        qhat, _ = _head_norm(q_ref[...].astype(F32), None, ones)
        khat, _ = _head_norm(k_ref[...].astype(F32), None, ones)
        qs[...] = (qhat * gq_ref[...] * (SCALE * LOG2E)).astype(BF16)
        kn = khat * gk_ref[...]
        vv = v_ref[...].astype(F32)
        lane = lax.broadcasted_iota(jnp.int32, (1, LANES), 1)
        for hd in range(2):
            ks[hd] = (kn * masks[hd]).astype(BF16)
            vs[hd] = (vv * masks[hd] + (lane == SUM_LANE[hd]).astype(F32)).astype(BF16)
        row = lax.broadcasted_iota(jnp.int32, (tq, tq), 0)
        col = lax.broadcasted_iota(jnp.int32, (tq, tq), 1)
        causal = col <= row

        for qi in range(nq):
            q0 = qi * tq
            q_blk = qs[q0:q0 + tq, :]
            o_tot = jnp.zeros((tq, LANES), F32)
            lse_tot = jnp.zeros((tq, LANES), F32)
            for hd in range(2):
                crow = c_ref[0, hd:hd + 1, 0:q0 + tq] * LOG2E
                c0 = crow[:, q0:q0 + 1]
                s_d = _dot(q_blk, ks[hd, q0:q0 + tq, :], NT) + (c0 - crow[:, q0:q0 + tq])
                s_d = jnp.where(causal, s_d, NEG)
                m = jnp.max(s_d, axis=-1, keepdims=True)
                if qi > 0:
                    s_o = _dot(q_blk, ks[hd, 0:q0, :], NT) + (c0 - crow[:, 0:q0])
                    m = jnp.maximum(m, jnp.max(s_o, axis=-1, keepdims=True))
                acc = _dot(jnp.exp2(s_d - m).astype(BF16), vs[hd, q0:q0 + tq, :])
                if qi > 0:
                    acc = acc + _dot(jnp.exp2(s_o - m).astype(BF16), vs[hd, 0:q0, :])
                l = acc[:, SUM_LANE[hd]:SUM_LANE[hd] + 1]
                o_tot = o_tot + (acc / l) * masks[hd]
                lse_tot = lse_tot + (m + jnp.log2(l) - c0) * masks[hd]
            o_ref[q0:q0 + tq, :] = o_tot
            lse_ref[q0:q0 + tq, :] = lse_tot

    blk = lambda off: pl.BlockSpec((seq, LANES), lambda b, p: (b, off + p))
    return _host_call(
        body, rider, name="fox_fwd", grid=(nb, npair),
        in_specs=[blk(0), blk(npair), blk(2 * npair), pl.BlockSpec((1, 2, seq), lambda b, p: (p, 0, b)),
                  pl.BlockSpec((1, LANES), lambda b, p: (0, 0)), pl.BlockSpec((1, LANES), lambda b, p: (0, 0))],
        out_specs=[blk(0), blk(0)],
        out_shape=[jax.ShapeDtypeStruct((t, W_GROUP), F32), jax.ShapeDtypeStruct((t, W_GROUP), F32)],
        scratch_shapes=[pltpu.VMEM((seq, LANES), BF16), pltpu.VMEM((2, seq, LANES), BF16),
                        pltpu.VMEM((2, seq, LANES), BF16)],
        inputs=(proj, proj, proj, c3, gq, gk), semantics=("arbitrary", "arbitrary"))


def _fox_bwd(proj, c3, gq, gk, do, o, lse, nb, seq, rider=None):
    t = nb * seq
    tq = FOX_TQ
    nq = seq // tq
    npair = N_FOX_HEADS // 2

    def body(q_ref, k_ref, v_ref, c_ref, gq_ref, gk_ref, do_ref, o_ref, lse_ref,
             dq_ref, dk_ref, dv_ref, dc_ref, dg_ref, qs, ks, vs, kts, dos, lse_t, delta_t, dqt_acc, dk_acc, dv_acc,
             row_sum):
        @pl.when((pl.program_id(0) == 0) & (pl.program_id(1) == 0))
        def _():
            dg_ref[...] = jnp.zeros_like(dg_ref)

        ones = _group_ones()
        masks = _head_masks()
        qhat, rq = _head_norm(q_ref[...].astype(F32), None, ones)
        khat, rk = _head_norm(k_ref[...].astype(F32), None, ones)
        qs[...] = (qhat * gq_ref[...] * (SCALE * LOG2E)).astype(BF16)
        kn = khat * gk_ref[...]
        vv = v_ref[...].astype(F32)
        for hd in range(2):
            ks[hd] = (kn * masks[hd]).astype(BF16)
            vs[hd] = (vv * masks[hd]).astype(BF16)
            kts[hd] = ks[hd].T
        dof = do_ref[...]
        dos[...] = dof.astype(BF16)
        lse_t[...] = lse_ref[...].T
        delta_t[...] = _groupsum(dof * o_ref[...], ones).T
        dqt_acc[...] = jnp.zeros_like(dqt_acc)
        dk_acc[...] = jnp.zeros_like(dk_acc)
        dv_acc[...] = jnp.zeros_like(dv_acc)
        row_sum[...] = jnp.zeros_like(row_sum)
        key = lax.broadcasted_iota(jnp.int32, (tq, tq), 0)
        qry = lax.broadcasted_iota(jnp.int32, (tq, tq), 1)
        causal = key <= qry

        for hd in range(2):
            lane0 = hd * HEAD_DIM
            for kj in range(nq):
                k0 = kj * tq
                k_blk = ks[hd, k0:k0 + tq, :]
                v_blk = vs[hd, k0:k0 + tq, :]
                kt_blk = kts[hd, :, k0:k0 + tq]
                crow = c_ref[0, hd:hd + 1, k0:k0 + tq] * LOG2E
                ck0 = crow[:, 0:1]
                bias = jnp.broadcast_to(ck0 - crow, (LANES, tq)).T[:, 0:1]

                def queries_step(r0, r1, diag, hd=hd, lane0=lane0, k_blk=k_blk, v_blk=v_blk, kt_blk=kt_blk,
                                 bias=bias, ck0=ck0):
                    q_r = qs[r0:r1, :]
                    do_r = dos[r0:r1, :]
                    z = _dot(k_blk, q_r, NT) + bias
                    p = jnp.exp2(z - (lse_t[lane0:lane0 + 1, r0:r1] + ck0))
                    if diag:
                        p = jnp.where(causal, p, 0.0)
                    dp = _dot(v_blk, do_r, NT)
                    ds = p * (dp - delta_t[lane0:lane0 + 1, r0:r1])
                    dsb = ds.astype(BF16)
                    dqt_acc[:, r0:r1] += _dot(kt_blk, dsb)
                    row_sum[hd:hd + 1, r0:r1] += jnp.sum(ds, axis=0, keepdims=True)
                    return _dot(dsb, q_r), _dot(p.astype(BF16), do_r), -jnp.sum(ds, axis=1, keepdims=True)

                dk_j, dv_j, dc_j = queries_step(k0, k0 + tq, True)
                if k0 + tq < seq:
                    dk_o, dv_o, dc_o = queries_step(k0 + tq, seq, False)
                    dk_j, dv_j, dc_j = dk_j + dk_o, dv_j + dv_o, dc_j + dc_o
                dk_acc[k0:k0 + tq, :] += dk_j * masks[hd]
                dv_acc[k0:k0 + tq, :] += dv_j * masks[hd]
                dc_ref[0, hd:hd + 1, k0:k0 + tq] = jnp.broadcast_to(dc_j, (tq, LANES)).T[0:1, :]

        dc_ref[0] += row_sum[0:2, :]

        dq_raw, dgq = _head_norm_bwd(dqt_acc[...].T * SCALE, qhat, rq, gq_ref[...], ones)
        dk_raw, dgk = _head_norm_bwd(dk_acc[...] * LN2, khat, rk, gk_ref[...], ones)
        dq_ref[...] = dq_raw.astype(BF16)
        dk_ref[...] = dk_raw.astype(BF16)
        dv_ref[...] = dv_acc[...].astype(BF16)
        dg_ref[0:1, :] += dgq
        dg_ref[1:2, :] += dgk

    blk = lambda off: pl.BlockSpec((seq, LANES), lambda b, p: (b, off + p))
    vec = pl.BlockSpec((1, LANES), lambda b, p: (0, 0))
    c_spec = pl.BlockSpec((1, 2, seq), lambda b, p: (p, 0, b))
    return _host_call(
        body, rider, name="fox_bwd", grid=(nb, npair),
        in_specs=[blk(0), blk(npair), blk(2 * npair), c_spec, vec, vec, blk(0), blk(0), blk(0)],
        out_specs=[blk(0), blk(0), blk(0), c_spec, pl.BlockSpec((8, LANES), lambda b, p: (0, 0))],
        out_shape=[jax.ShapeDtypeStruct((t, W_GROUP), BF16), jax.ShapeDtypeStruct((t, W_GROUP), BF16),
                   jax.ShapeDtypeStruct((t, W_GROUP), BF16), jax.ShapeDtypeStruct((npair, 2, t), F32),
                   jax.ShapeDtypeStruct((8, LANES), F32)],
        scratch_shapes=[pltpu.VMEM((seq, LANES), BF16), pltpu.VMEM((2, seq, LANES), BF16),
                        pltpu.VMEM((2, seq, LANES), BF16), pltpu.VMEM((2, LANES, seq), BF16),
                        pltpu.VMEM((seq, LANES), BF16), pltpu.VMEM((LANES, seq), F32),
                        pltpu.VMEM((LANES, seq), F32), pltpu.VMEM((LANES, seq), F32),
                        pltpu.VMEM((seq, LANES), F32), pltpu.VMEM((seq, LANES), F32),
                        pltpu.VMEM((8, seq), F32)],
        inputs=(proj, proj, proj, c3, gq, gk, do, o, lse), semantics=("arbitrary", "arbitrary"))


def _dil_prep(q_ref, k_ref, gq_ref, gk_ref, cos_ref, up_ref, dn_ref, ones):
    qhat, rq = _head_norm(q_ref[...].astype(F32), None, ones)
    khat, rk = _head_norm(k_ref[...].astype(F32), None, ones)
    cos, up, dn = cos_ref[...], up_ref[...], dn_ref[...]
    qn = _rope(qhat * gq_ref[...], cos, up, dn) * (SCALE * LOG2E)
    kn = _rope(khat * gk_ref[...], cos, up, dn)
    return qhat, rq, khat, rk, qn, kn


def _dil_keys(d, seq, pairs):
    nblk = seq // BAND
    per_res = seq // (d * BAND)
    as_blocks = lambda ref, rows: ref[rows, :].reshape(-1, BAND, LANES)
    if per_res == 1:
        a = lax.broadcasted_iota(jnp.int32, (1, BAND, BAND), 1)
        j = lax.broadcasted_iota(jnp.int32, (1, BAND, BAND), 2)
        causal = jnp.where(j <= a, 0.0, NEG)
        return [as_blocks(src, slice(0, seq)) for src, _ in pairs], [causal]
    for src, dst in pairs:
        dst[:, BAND:, :] = as_blocks(src, slice(0, seq))
        dst[1:, :BAND, :] = as_blocks(src, slice(0, seq - BAND))
        dst[0:1, :BAND, :] = jnp.zeros((1, BAND, LANES), BF16)
    a = lax.broadcasted_iota(jnp.int32, (1, BAND, 2 * BAND), 1)
    j = lax.broadcasted_iota(jnp.int32, (1, BAND, 2 * BAND), 2)
    band = jnp.where(((j < BAND) & (j >= a)) | ((j >= BAND) & (j - BAND <= a)), 0.0, NEG)
    e = lax.broadcasted_iota(jnp.int32, (nblk, 1, 2 * BAND), 0)
    j = lax.broadcasted_iota(jnp.int32, (nblk, 1, 2 * BAND), 2)
    no_prev = jnp.where(((e & (per_res - 1)) == 0) & (j < BAND), NEG, 0.0)
    return [dst[...] for _, dst in pairs], [band + no_prev]


def _regroup(d, seq):
    if d == 1:
        return [(slice(0, seq), slice(0, seq))]
    before, n = d // 4, seq // d
    return [(pl.ds(r1 * (seq // before) + r2, n, stride=4), slice((before * r2 + r1) * n, (before * r2 + r1 + 1) * n))
            for r1 in range(before) for r2 in range(4)]


def _dil_fwd(proj, gq, gk, cos, up, dn, nb, seq):
    t = nb * seq
    npair = W_GROUP // LANES
    off = 3 * npair

    def body(q_ref, k_ref, v_ref, gq_ref, gk_ref, cos_ref, up_ref, dn_ref, o_ref, lse_ref,
             src_a, src_b, qp, kp, vp, kw, vw, m_b, l_b, o_b, state_a, state_b):
        ones = _group_ones()
        masks = _head_masks()
        _, _, _, _, qn, kn = _dil_prep(q_ref, k_ref, gq_ref, gk_ref, cos_ref, up_ref, dn_ref, ones)
        src_a[0] = qn
        src_a[1] = kn
        src_a[2] = v_ref[...].astype(F32)
        nblk = seq // BAND
        src, state = (src_a, src_b), (state_a, state_b)

        for d in DILATIONS:
            last = d == DILATIONS[-1]
            for before, after in _regroup(d, seq):
                qv, kv, vv = src[0].at[0][before, :], src[0].at[1][before, :], src[0].at[2][before, :]
                for hd in range(2):
                    qp[hd, after, :] = (qv * masks[hd]).astype(BF16)
                kp[after, :] = kv.astype(BF16)
                vp[after, :] = vv.astype(BF16)
                if d > 1 and not last:
                    src[1][0, after, :], src[1][1, after, :], src[1][2, after, :] = qv, kv, vv
            if d > 1:
                src = src[::-1]
            (keys_k, keys_v), bias = _dil_keys(d, seq, [(kp, kw), (vp, vw)])
            m_t = jnp.zeros((nblk, BAND, LANES), F32)
            l_t = jnp.zeros((nblk, BAND, LANES), F32)
            o_t = jnp.zeros((nblk, BAND, LANES), F32)
            for hd in range(2):
                s = _dot(qp[hd].reshape(nblk, BAND, LANES), keys_k, BATCH_NT)
                for b_ in bias:
                    s = s + b_
                m = jnp.max(s, axis=-1, keepdims=True)
                p = jnp.exp2(s - m).astype(BF16)
                m_t = m_t + m * masks[hd]
                l_t = l_t + _dot(p, jnp.ones(keys_v.shape, BF16), BATCH_NN) * masks[hd]
                o_t = o_t + _dot(p, keys_v, BATCH_NN) * masks[hd]
            if d == 1:
                state[0][0] = m_t.reshape(seq, LANES)
                state[0][1] = l_t.reshape(seq, LANES)
                state[0][2] = o_t.reshape(seq, LANES)
                continue
            m_b[...] = m_t.reshape(seq, LANES)
            l_b[...] = l_t.reshape(seq, LANES)
            o_b[...] = o_t.reshape(seq, LANES)
            for before, after in _regroup(d, seq):
                m_old = state[0].at[0][before, :]
                m_new = jnp.maximum(m_old, m_b[after, :])
                w_old = jnp.exp2(m_old - m_new)
                w_new = jnp.exp2(m_b[after, :] - m_new)
                state[1][0, after, :] = m_new
                state[1][1, after, :] = state[0].at[1][before, :] * w_old + l_b[after, :] * w_new
                state[1][2, after, :] = state[0].at[2][before, :] * w_old + o_b[after, :] * w_new
            state = state[::-1]

        l = state[0][1]
        o_b[...] = state[0][2] / l
        l_b[...] = state[0][0] + jnp.log2(l)
        held, spare = [o_b, l_b], [m_b, state[1].at[0]]
        for d in DILATIONS[:0:-1]:
            dests = [o_ref, lse_ref] if d == DILATIONS[1] else spare
            for h, dst in zip(held, dests):
                for before, after in _regroup(d, seq):
                    dst[before, :] = h[after, :]
            held, spare = dests, held

    blk = lambda o_: pl.BlockSpec((seq, LANES), lambda b, p: (b, o_ + p))
    vec = pl.BlockSpec((1, LANES), lambda b, p: (0, 0))
    tab = pl.BlockSpec(memory_space=pltpu.VMEM)
    f32_buf = pltpu.VMEM((seq, LANES), F32)
    f32_x3 = pltpu.VMEM((3, seq, LANES), F32)
    bf16_buf = pltpu.VMEM((seq, LANES), BF16)
    window_buf = pltpu.VMEM((seq // BAND, 2 * BAND, LANES), BF16)
    return pl.pallas_call(
        body, name="dil_fwd", grid=(nb, npair),
        in_specs=[blk(off), blk(off + npair), blk(off + 2 * npair), vec, vec, tab, tab, tab],
        out_specs=[blk(0), blk(0)],
        out_shape=[jax.ShapeDtypeStruct((t, W_GROUP), F32), jax.ShapeDtypeStruct((t, W_GROUP), F32)],
        scratch_shapes=[f32_x3, f32_x3, pltpu.VMEM((2, seq, LANES), BF16), bf16_buf, bf16_buf,
                        window_buf, window_buf, f32_buf, f32_buf, f32_buf, f32_x3, f32_x3],
        compiler_params=_params(("arbitrary", "arbitrary")),
    )(proj, proj, proj, gq, gk, cos, up, dn)


def _dil_bwd(proj, gq, gk, cos, up, dn, do, o, lse, nb, seq, rider=None):
    t = nb * seq
    npair = W_GROUP // LANES
    off = 3 * npair

    def body(q_ref, k_ref, v_ref, gq_ref, gk_ref, cos_ref, up_ref, dn_ref, do_ref, o_ref, lse_ref,
             dq_ref, dk_ref, dv_ref, dg_ref, src_a, src_b, sums_a, sums_b,
             qp, kp, vp, dop, kw, vw, lse_p, delta_p, dq_p, dk_p, dv_p):
        @pl.when((pl.program_id(0) == 0) & (pl.program_id(1) == 0))
        def _():
            dg_ref[...] = jnp.zeros_like(dg_ref)

        ones = _group_ones()
        masks = _head_masks()
        qhat, rq, khat, rk, qn, kn = _dil_prep(q_ref, k_ref, gq_ref, gk_ref, cos_ref, up_ref, dn_ref, ones)
        src_a[0] = qn
        src_a[1] = kn
        src_a[2] = v_ref[...].astype(F32)
        src_a[3] = do_ref[...]
        src_a[4] = lse_ref[...]
        src_a[5] = _groupsum(do_ref[...] * o_ref[...], ones)
        nblk = seq // BAND
        src, sums = (src_a, src_b), (sums_a, sums_b)

        for d in DILATIONS:
            last = d == DILATIONS[-1]
            for before, after in _regroup(d, seq):
                planes = [src[0].at[i][before, :] for i in range(6)]
                for hd in range(2):
                    qp[hd, after, :] = (planes[0] * masks[hd]).astype(BF16)
                    dop[hd, after, :] = (planes[3] * masks[hd]).astype(BF16)
                kp[after, :] = planes[1].astype(BF16)
                vp[after, :] = planes[2].astype(BF16)
                lse_p[after, :] = planes[4]
                delta_p[after, :] = planes[5]
                if d > 1 and not last:
                    for i in range(6):
                        src[1][i, after, :] = planes[i]
            if d > 1:
                src = src[::-1]
            (keys_k, keys_v), bias = _dil_keys(d, seq, [(kp, kw), (vp, vw)])
            nk = keys_k.shape[1]
            dq_b = jnp.zeros((nblk, BAND, LANES), F32)
            dk_b = jnp.zeros((nblk, nk, LANES), F32)
            dv_b = jnp.zeros((nblk, nk, LANES), F32)
            for hd in range(2):
                lane0 = hd * HEAD_DIM
                q3 = qp[hd].reshape(nblk, BAND, LANES)
                do3 = dop[hd].reshape(nblk, BAND, LANES)
                z = _dot(q3, keys_k, BATCH_NT)
                for b_ in bias:
                    z = z + b_
                p = jnp.exp2(z - lse_p[...].reshape(nblk, BAND, LANES)[:, :, lane0:lane0 + 1])
                dp = _dot(do3, keys_v, BATCH_NT)
                ds = (p * (dp - delta_p[...].reshape(nblk, BAND, LANES)[:, :, lane0:lane0 + 1])).astype(BF16)
                dq_b = dq_b + _dot(ds, keys_k, BATCH_NN) * masks[hd]
                dk_b = dk_b + _dot(ds, q3, BATCH_TN)
                dv_b = dv_b + _dot(p.astype(BF16), do3, BATCH_TN)
            dq_p[...] = dq_b.reshape(seq, LANES)
            for acc, out in ((dk_b, dk_p), (dv_b, dv_p)):
                out[...] = acc[:, nk - BAND:, :].reshape(seq, LANES)
                if nk > BAND:
                    out[0:seq - BAND, :] += acc[1:, :BAND, :].reshape(seq - BAND, LANES)
            if d == 1:
                sums[0][0], sums[0][1], sums[0][2] = dq_p[...], dk_p[...], dv_p[...]
                continue
            for before, after in _regroup(d, seq):
                for i, part in enumerate((dq_p, dk_p, dv_p)):
                    sums[1][i, after, :] = sums[0].at[i][before, :] + part[after, :]
            sums = sums[::-1]

        for d in DILATIONS[:0:-1]:
            for i in range(3):
                for before, after in _regroup(d, seq):
                    sums[1].at[i][before, :] = sums[0][i, after, :]
            sums = sums[::-1]

        cos, up, dn = cos_ref[...], up_ref[...], dn_ref[...]
        dq_raw, dgq = _head_norm_bwd(_rope_bwd(sums[0][0] * SCALE, cos, up, dn), qhat, rq, gq_ref[...], ones)
        dk_raw, dgk = _head_norm_bwd(_rope_bwd(sums[0][1] * LN2, cos, up, dn), khat, rk, gk_ref[...], ones)
        dq_ref[...] = dq_raw.astype(BF16)
        dk_ref[...] = dk_raw.astype(BF16)
        dv_ref[...] = sums[0][2].astype(BF16)
        dg_ref[0:1, :] += dgq
        dg_ref[1:2, :] += dgk

    blk = lambda o_: pl.BlockSpec((seq, LANES), lambda b, p: (b, o_ + p))
    vec = pl.BlockSpec((1, LANES), lambda b, p: (0, 0))
    tab = pl.BlockSpec(memory_space=pltpu.VMEM)
    f32_buf = pltpu.VMEM((seq, LANES), F32)
    bf16_buf = pltpu.VMEM((seq, LANES), BF16)
    window_buf = pltpu.VMEM((seq // BAND, 2 * BAND, LANES), BF16)
    bf16_pair = pltpu.VMEM((2, seq, LANES), BF16)
    return _host_call(
        body, rider, name="dil_bwd", grid=(nb, npair),
        in_specs=[blk(off), blk(off + npair), blk(off + 2 * npair), vec, vec, tab, tab, tab,
                  blk(0), blk(0), blk(0)],
        out_specs=[blk(0), blk(0), blk(0), pl.BlockSpec((8, LANES), lambda b, p: (0, 0))],
        out_shape=[jax.ShapeDtypeStruct((t, W_GROUP), BF16), jax.ShapeDtypeStruct((t, W_GROUP), BF16),
                   jax.ShapeDtypeStruct((t, W_GROUP), BF16), jax.ShapeDtypeStruct((8, LANES), F32)],
        scratch_shapes=[pltpu.VMEM((6, seq, LANES), F32)] * 2 + [pltpu.VMEM((3, seq, LANES), F32)] * 2
        + [bf16_pair, bf16_buf, bf16_buf, bf16_pair, window_buf, window_buf] + [f32_buf] * 5,
        inputs=(proj, proj, proj, gq, gk, cos, up, dn, do, o, lse), semantics=("arbitrary", "arbitrary"))


def _adamw(w, g, m, v, name, rider=None):
    row_major = w.ndim == 3 and w.shape[1] == 1
    rows, cols = (w.shape[0], w.shape[2]) if row_major else w.shape[-2:]
    if row_major:
        tr = max(t for t in range(1, 65) if rows % t == 0)
    else:
        tr = _row_tile(rows) if rows >= 8 else rows
    c1 = 1.0 - ADAM_B1 ** ADAM_STEP
    c2 = 1.0 - ADAM_B2 ** ADAM_STEP

    def body(w_ref, g_ref, m_ref, v_ref, d_ref, nm_ref, nv_ref):
        g_ = g_ref[...]
        nm = ADAM_B1 * m_ref[...] + (1.0 - ADAM_B1) * g_
        nv = ADAM_B2 * v_ref[...] + (1.0 - ADAM_B2) * (g_ * g_)
        nm_ref[...] = nm
        nv_ref[...] = nv
        d_ref[...] = -ADAM_LR * ((nm / c1) / (jnp.sqrt(nv / c2) + ADAM_EPS) + ADAM_WD * w_ref[...])

    if row_major:
        spec = pl.BlockSpec((tr, 1, cols), lambda i: (i, 0, 0))
    elif w.ndim == 3:
        spec = pl.BlockSpec((1, tr, cols), lambda i: (0, i, 0))
    else:
        spec = pl.BlockSpec((tr, cols), lambda i: (i, 0))
    shape = jax.ShapeDtypeStruct(w.shape, F32)
    return _host_call(
        body, rider, name=name, grid=(rows // tr,), in_specs=[spec] * 4, out_specs=[spec] * 3,
        out_shape=[shape] * 3, scratch_shapes=[], inputs=(w, g, m, v), semantics=("arbitrary",))


def _place():
    x, y, c = lax.axis_index("x"), lax.axis_index("y"), lax.axis_index("c")
    chips = [(1 - x, y), (x, 1 - y), (1 - x, 1 - y)]
    return x, y, c, chips


def _gather_weight(w, name):
    _, rows, cols = w.shape
    half_rows = rows // 2

    def body(w_ref, out_ref, send_sems, recv_sems):
        x, y, c, chips = _place()
        sibling = (x, y, 1 - c)
        mine = 2 * x + y
        lo = pl.multiple_of(c * half_rows, 16)
        lo_sib = pl.multiple_of((1 - c) * half_rows, 16)
        out_ref[mine] = w_ref[0].astype(BF16)

        def copy(k, shard, first_row, to):
            ref = out_ref.at[shard, pl.ds(first_row, half_rows), :]
            return pltpu.make_async_remote_copy(src_ref=ref, dst_ref=ref, send_sem=send_sems.at[k],
                                                recv_sem=recv_sems.at[k], device_id=to, device_id_type=MESH)

        sends = [copy(k, mine, lo, (cx, cy, c)) for k, (cx, cy) in enumerate(chips)]
        for cp in sends:
            cp.start()
        passed = []
        for k, (cx, cy) in enumerate(chips):
            theirs = 2 * cx + cy
            copy(k, theirs, lo, (cx, cy, c)).wait_recv()
            fw = copy(3 + k, theirs, lo, sibling)
            fw.start()
            passed.append(fw)
        for k, (cx, cy) in enumerate(chips):
            copy(3 + k, 2 * cx + cy, lo_sib, sibling).wait_recv()
        for cp in sends + passed:
            cp.wait_send()

    return pl.pallas_call(
        body, name=name,
        in_specs=[pl.BlockSpec(memory_space=pltpu.VMEM)],
        out_specs=pl.BlockSpec(memory_space=pltpu.VMEM),
        out_shape=jax.ShapeDtypeStruct((4, rows, cols), BF16),
        scratch_shapes=[pltpu.SemaphoreType.DMA((6,)), pltpu.SemaphoreType.DMA((6,))],
        compiler_params=pltpu.CompilerParams(vmem_limit_bytes=VMEM_LIMIT),
    )(w)


def _remote(src, dst, sems, k, to):
    send_sems, recv_sems = sems
    return pltpu.make_async_remote_copy(src_ref=src, dst_ref=dst, send_sem=send_sems.at[k], recv_sem=recv_sems.at[k],
                                        device_id=to, device_id_type=MESH)


def _cast_bf16(parts, name):
    def body(*refs):
        for src, dst in zip(refs[:len(parts)], refs[len(parts):]):
            dst[...] = src[0].astype(BF16)

    return pl.pallas_call(
        body, name=name, in_specs=[pl.BlockSpec(memory_space=pltpu.VMEM)] * len(parts),
        out_specs=[pl.BlockSpec(memory_space=pltpu.VMEM)] * len(parts),
        out_shape=[jax.ShapeDtypeStruct(p.shape[1:], BF16) for p in parts],
        compiler_params=pltpu.CompilerParams(vmem_limit_bytes=VMEM_LIMIT),
    )(*parts)


def _gather_rider(shards):
    def copies(ins, outs, sems, which):
        x, y, c, chips = _place()
        sibling = (x, y, 1 - c)
        mine = 2 * x + y
        made = {name: [] for name in which}
        for i, (p_ref, g_ref) in enumerate(zip(ins, outs)):
            half = p_ref.shape[0] // 2
            lo = pl.multiple_of(c * half, 16)
            lo_sib = pl.multiple_of((1 - c) * half, 16)
            spot = lambda shard, first, g_ref=g_ref, half=half: g_ref.at[shard, pl.ds(first, half), :]
            groups = {
                "own": lambda: [pltpu.make_async_copy(p_ref, g_ref.at[mine], sems[0].at[7 * i + 6])],
                "sends": lambda: [_remote(p_ref.at[pl.ds(lo, half), :], spot(mine, lo), sems, 7 * i + k, (cx, cy, c))
                                  for k, (cx, cy) in enumerate(chips)],
                "arrivals": lambda: [_remote(spot(2 * cx + cy, lo), spot(2 * cx + cy, lo), sems, 7 * i + k, (cx, cy, c))
                                     for k, (cx, cy) in enumerate(chips)],
                "passes": lambda: [_remote(spot(2 * cx + cy, lo), spot(2 * cx + cy, lo), sems, 7 * i + 3 + k, sibling)
                                   for k, (cx, cy) in enumerate(chips)],
                "from_sibling": lambda: [_remote(spot(2 * cx + cy, lo_sib), spot(2 * cx + cy, lo_sib), sems,
                                                 7 * i + 3 + k, sibling) for k, (cx, cy) in enumerate(chips)],
            }
            for name in which:
                made[name] += groups[name]()
        return [made[name] for name in which]

    def start(ins, outs, send_sems, recv_sems):
        own, sends = copies(ins, outs, (send_sems, recv_sems), ("own", "sends"))
        for cp in own + sends:
            cp.start()

    def middle(ins, outs, send_sems, recv_sems):
        arrivals, passes = copies(ins, outs, (send_sems, recv_sems), ("arrivals", "passes"))
        for landed, onward in zip(arrivals, passes):
            landed.wait_recv()
            onward.start()

    def finish(ins, outs, send_sems, recv_sems):
        own, sends, passes, from_sibling = copies(ins, outs, (send_sems, recv_sems),
                                                  ("own", "sends", "passes", "from_sibling"))
        for cp in from_sibling:
            cp.wait_recv()
        for cp in sends + passes:
            cp.wait_send()
        for cp in own:
            cp.wait()

    shapes = [jax.ShapeDtypeStruct((4,) + s.shape, BF16) for s in shards]
    return _Rider(shards, shapes, 7 * len(shards), start, finish, middle=middle)


def _exchange_rider(inputs, out_shapes, n_sems, copies, aliases=None):
    def start(ins, outs, send_sems, recv_sems):
        for cp in copies(ins, outs, (send_sems, recv_sems)):
            cp.start()

    def finish(ins, outs, send_sems, recv_sems):
        for cp in copies(ins, outs, (send_sems, recv_sems)):
            cp.wait()

    return _Rider(inputs, out_shapes, n_sems, start, finish, aliases)


def _swap_rider(grads4):
    halves = [g.shape[1] // 2 for g in grads4]

    def copies(ins, outs, sems):
        x, y, c, _ = _place()
        return [_remote(g.at[:, pl.ds(pl.multiple_of((1 - c) * h, 8), h), :], a, sems, i, (x, y, 1 - c))
                for i, (g, a, h) in enumerate(zip(ins, outs, halves))]

    shapes = [jax.ShapeDtypeStruct((4, h, g.shape[2]), F32) for g, h in zip(grads4, halves)]
    return _exchange_rider(grads4, shapes, len(grads4), copies)


def _chip_sum(g4, from_sibling, name):
    _, rows, cols = g4.shape
    half = rows // 2

    def body(g_ref, s_ref, stage_ref, own_ref):
        x, y, c, chips = _place()
        lo = pl.multiple_of(c * half, 8)
        for k, (cx, cy) in enumerate(chips):
            theirs = 2 * cx + cy
            stage_ref[k] = (g_ref[theirs, pl.ds(lo, half), :] + s_ref[theirs]).astype(BF16)
        mine = 2 * x + y
        own_ref[...] = g_ref[mine, pl.ds(lo, half), :] + s_ref[mine]

    return pl.pallas_call(
        body, name=name, in_specs=[pl.BlockSpec(memory_space=pltpu.VMEM)] * 2,
        out_specs=[pl.BlockSpec(memory_space=pltpu.VMEM)] * 2,
        out_shape=[jax.ShapeDtypeStruct((3, half, cols), BF16), jax.ShapeDtypeStruct((half, cols), F32)],
        compiler_params=pltpu.CompilerParams(vmem_limit_bytes=VMEM_LIMIT),
    )(g4, from_sibling)


def _spread_rider(stages):
    def copies(ins, outs, sems):
        _, _, c, chips = _place()
        return [_remote(st.at[k], ld.at[k], sems, 3 * i + k, (cx, cy, c))
                for i, (st, ld) in enumerate(zip(ins, outs)) for k, (cx, cy) in enumerate(chips)]

    shapes = [jax.ShapeDtypeStruct(s.shape, s.dtype) for s in stages]
    return _exchange_rider(stages, shapes, 3 * len(stages), copies)


def _finish_half(own, landed, name):
    half, cols = own.shape

    def body(own_ref, landed_ref, out_ref):
        c = lax.axis_index("c")
        acc = own_ref[...]
        for k in range(3):
            acc = acc + landed_ref[k].astype(F32)
        out_ref[pl.ds(pl.multiple_of(c * half, 8), half), :] = acc

    return pl.pallas_call(
        body, name=name, in_specs=[pl.BlockSpec(memory_space=pltpu.VMEM)] * 2,
        out_specs=pl.BlockSpec(memory_space=pltpu.VMEM),
        out_shape=jax.ShapeDtypeStruct((2 * half, cols), F32),
        compiler_params=pltpu.CompilerParams(vmem_limit_bytes=VMEM_LIMIT),
    )(own, landed)


def _share_rider(fulls):
    def copies(ins, outs, sems):
        x, y, c, _ = _place()
        out = []
        for i, full in enumerate(outs):
            half = full.shape[0] // 2
            rows = full.at[pl.ds(pl.multiple_of(c * half, 8), half), :]
            out.append(_remote(rows, rows, sems, i, (x, y, 1 - c)))
        return out

    def finish_copies(ins, outs, sems):
        x, y, c, _ = _place()
        out = []
        for i, full in enumerate(outs):
            half = full.shape[0] // 2
            mine = full.at[pl.ds(pl.multiple_of(c * half, 8), half), :]
            theirs = full.at[pl.ds(pl.multiple_of((1 - c) * half, 8), half), :]
            out.append((_remote(mine, mine, sems, i, (x, y, 1 - c)), _remote(theirs, theirs, sems, i, (x, y, 1 - c))))
        return out

    def start(ins, outs, send_sems, recv_sems):
        for cp in copies(ins, outs, (send_sems, recv_sems)):
            cp.start()

    def finish(ins, outs, send_sems, recv_sems):
        for sent, landed in finish_copies(ins, outs, (send_sems, recv_sems)):
            sent.wait_send()
            landed.wait_recv()

    shapes = [jax.ShapeDtypeStruct(f.shape, f.dtype) for f in fulls]
    return _Rider(fulls, shapes, len(fulls), start, finish, aliases={i: i for i in range(len(fulls))})


def _all_sum_small(v):
    shape = v.shape

    def body(v_ref, out_ref, buf, send_sems, recv_sems):
        x, y, c, _ = _place()
        me = 4 * x + 2 * y + c
        buf[me] = v_ref[...]
        flips = [(dx, dy, dc) for dx in (0, 1) for dy in (0, 1) for dc in (0, 1)][1:]

        def copy(k, slot, flip):
            dx, dy, dc = flip
            to = (1 - x if dx else x, 1 - y if dy else y, 1 - c if dc else c)
            return pltpu.make_async_remote_copy(src_ref=buf.at[slot], dst_ref=buf.at[slot], send_sem=send_sems.at[k],
                                                recv_sem=recv_sems.at[k], device_id=to, device_id_type=MESH)

        sends = [copy(k, me, flip) for k, flip in enumerate(flips)]
        for cp in sends:
            cp.start()
        for k, (dx, dy, dc) in enumerate(flips):
            sender = 4 * (1 - x if dx else x) + 2 * (1 - y if dy else y) + (1 - c if dc else c)
            copy(k, sender, (dx, dy, dc)).wait_recv()
        for cp in sends:
            cp.wait_send()
        total = buf[0]
        for i in range(1, 8):
            total = total + buf[i]
        out_ref[...] = total

    return pl.pallas_call(
        body, name="all_sum_small",
        in_specs=[pl.BlockSpec(memory_space=pltpu.VMEM)],
        out_specs=pl.BlockSpec(memory_space=pltpu.VMEM),
        out_shape=jax.ShapeDtypeStruct(shape, F32),
        scratch_shapes=[pltpu.VMEM((8,) + shape, F32), pltpu.SemaphoreType.DMA((7,)), pltpu.SemaphoreType.DMA((7,))],
    )(v)


SMALL = (("g_mix", 1024), ("g_ffn", 1024), ("g_out_fox", 512), ("g_out_dil", 512), ("g_q_fox", 64),
         ("g_k_fox", 64), ("g_q_dil", 64), ("g_k_dil", 64), ("b_forget", 8))
SMALL_PACKED = (32, LANES)


def _local_grads(x, target, gains, w1, wft, dense, packed, nb, seq):
    tile2 = lambda g: jnp.tile(g, (1, 2))
    gq_f, gk_f, gq_d, gk_d = (tile2(gains[n]) for n in ("g_q_fox", "g_k_fox", "g_q_dil", "g_k_dil"))
    b_col = gains["b_forget"].reshape(N_FOX_HEADS, 1)
    cos, up, dn = _rope_tables(seq)
    npair = N_FOX_HEADS // 2

    proj, fa_row, h1, h1_t = _in_proj(x, gains["g_mix"], w1, wft)
    c_row = _gate_fwd(fa_row, b_col, seq)
    c3 = c_row.reshape(npair, 2, nb * seq)
    (o_fox, lse_fox), gathered = _fox_fwd(proj, c3, gq_f, gk_f, nb, seq,
                                          rider=None if packed is None else _gather_rider(packed))
    if packed is not None:
        dense = [g.reshape(-1, g.shape[2]) for g in gathered]
    w_out, w_gate, w_up, w_down = dense
    o_dil, lse_dil = _dil_fwd(proj, gq_d, gk_d, cos, up, dn, nb, seq)
    x1, o_n_t = _attn_out(o_fox, o_dil, x, gains["g_out_fox"], gains["g_out_dil"], w_out)
    a, u, dy, loss_parts = _ffn_fwd(x1, target, gains["g_ffn"], w_gate, w_up, w_down)
    loss = jnp.sum(loss_parts[:, 0, 0])

    dx1, s, da, du, h2, dg_ffn = _ffn_bwd(dy, a, u, x1, gains["g_ffn"], w_gate, w_up, w_down)
    d_w_down = _token_matmul(s, dy, "dw_down", 512, False)
    d_w_gate = _token_matmul(da, h2, "dw_gate", 512, False)
    d_w_up = _token_matmul(du, h2, "dw_up", 512, False)
    d_w_out = _token_matmul(o_n_t, dx1, "dw_out", 1024)
    names = ("w_out", "w_gate", "w_up", "w_down")
    grads4 = [g.reshape(4, -1, g.shape[1]) for g in (d_w_out, d_w_gate, d_w_up, d_w_down)]
    exchange = packed is not None
    (do_fox, do_dil, dg_of, dg_od), from_sibling = _attn_out_bwd(
        dx1, o_fox, o_dil, gains["g_out_fox"], gains["g_out_dil"], w_out,
        rider=_swap_rider(grads4) if exchange else None)
    if exchange:
        sums = [_chip_sum(g, s, "chip_sum_" + n) for g, s, n in zip(grads4, from_sibling, names)]
    (dq_f, dk_f, dv_f, dc3, dg_fox), landed = _fox_bwd(
        proj, c3, gq_f, gk_f, do_fox, o_fox, lse_fox, nb, seq,
        rider=_spread_rider([st for st, _ in sums]) if exchange else None)
    if exchange:
        halves = [_finish_half(own, ld, "finish_half_" + n) for (_, own), ld, n in zip(sums, landed, names)]
    (dq_d, dk_d, dv_d, dg_dil), reduced = _dil_bwd(
        proj, gq_d, gk_d, cos, up, dn, do_dil, o_dil, lse_dil, nb, seq,
        rider=_share_rider(halves) if exchange else None)
    if exchange:
        d_w_out, d_w_gate, d_w_up, d_w_down = reduced
    dfa_row, db = _gate_bwd(dc3.reshape(N_FOX_HEADS, nb * seq), fa_row, b_col, seq)
    dparts = [dq_f, dk_f, dv_f, dq_d, dk_d, dv_d]
    d_w1 = _token_matmul_parts(h1_t, dparts, "dw_in")
    d_wf = _row_matmul(dfa_row, h1, "dw_forget")
    fox_w = 3 * W_GROUP
    d_w_in = jnp.concatenate([d_w1[:, :fox_w], d_wf.T, d_w1[:, fox_w:]], axis=1)
    if exchange:
        shards = [_shards_of_columns(d_w_in)]
        _, from_sibling = _idle_host(_swap_rider(shards), "swap_w_in")
        stage, own = _chip_sum(shards[0], from_sibling[0], "chip_sum_w_in")
    (grad_x, dg_mix), landed = _in_proj_bwd(dparts, dfa_row, w1, wft, x, gains["g_mix"], dx1,
                                            rider=_spread_rider([stage]) if exchange else None)
    if exchange:
        d_w_in = _finish_half(own, landed[0], "finish_half_w_in")

    fold = lambda g2: (g2[:, :HEAD_DIM] + g2[:, HEAD_DIM:])
    small = {
        "g_mix": dg_mix[0:1], "g_ffn": dg_ffn[0:1], "g_out_fox": dg_of[0:1], "g_out_dil": dg_od[0:1],
        "g_q_fox": fold(dg_fox[0:1]), "g_k_fox": fold(dg_fox[1:2]),
        "g_q_dil": fold(dg_dil[0:1]), "g_k_dil": fold(dg_dil[1:2]),
        "b_forget": db[:, 0].reshape(1, N_FOX_HEADS),
    }
    big = {"w_in": d_w_in, "w_out": d_w_out, "w_gate": d_w_gate, "w_up": d_w_up, "w_down": d_w_down}
    return loss, grad_x, big, small


def _shards_of_columns(full, n=4):
    r, nc = full.shape
    return full.reshape(r, n, nc // n).transpose(1, 0, 2)


def _columns_of_shards(slabs):
    n, r, c = slabs.shape
    return slabs.transpose(1, 0, 2).reshape(r, n * c)


def kernel(x, g_mix, w_in, b_forget, g_q_fox, g_k_fox, g_q_dil, g_k_dil, g_out_fox, g_out_dil, w_out, g_ffn, w_gate, w_up, w_down, loss_target, m_g_mix, m_w_in, m_b_forget, m_g_q_fox, m_g_k_fox, m_g_q_dil, m_g_k_dil, m_g_out_fox, m_g_out_dil, m_w_out, m_g_ffn, m_w_gate, m_w_up, m_w_down, v_g_mix, v_w_in, v_b_forget, v_g_q_fox, v_g_k_fox, v_g_q_dil, v_g_k_dil, v_g_out_fox, v_g_out_dil, v_w_out, v_g_ffn, v_w_gate, v_w_up, v_w_down):
    nb, seq, d = x.shape
    weights = dict(g_mix=g_mix, w_in=w_in, b_forget=b_forget, g_q_fox=g_q_fox, g_k_fox=g_k_fox, g_q_dil=g_q_dil,
                   g_k_dil=g_k_dil, g_out_fox=g_out_fox, g_out_dil=g_out_dil, w_out=w_out, g_ffn=g_ffn,
                   w_gate=w_gate, w_up=w_up, w_down=w_down)
    m_in = dict(g_mix=m_g_mix, w_in=m_w_in, b_forget=m_b_forget, g_q_fox=m_g_q_fox, g_k_fox=m_g_k_fox,
                g_q_dil=m_g_q_dil, g_k_dil=m_g_k_dil, g_out_fox=m_g_out_fox, g_out_dil=m_g_out_dil, w_out=m_w_out,
                g_ffn=m_g_ffn, w_gate=m_w_gate, w_up=m_w_up, w_down=m_w_down)
    v_in = dict(g_mix=v_g_mix, w_in=v_w_in, b_forget=v_b_forget, g_q_fox=v_g_q_fox, g_k_fox=v_g_k_fox,
                g_q_dil=v_g_q_dil, g_k_dil=v_g_k_dil, g_out_fox=v_g_out_fox, g_out_dil=v_g_out_dil, w_out=v_w_out,
                g_ffn=v_g_ffn, w_gate=v_w_gate, w_up=v_w_up, w_down=v_w_down)
    order = ["g_mix", "w_in", "b_forget", "g_q_fox", "g_k_fox", "g_q_dil", "g_k_dil", "g_out_fox", "g_out_dil",
             "w_out", "g_ffn", "w_gate", "w_up", "w_down"]

    w_in_full = _columns_of_shards(_gather_weight(w_in, "gather_w_in"))
    fox_w = 3 * W_GROUP
    w1 = jnp.concatenate([w_in_full[:, :fox_w], w_in_full[:, fox_w + N_FOX_HEADS:]], axis=1)
    wft = w_in_full[:, fox_w:fox_w + N_FOX_HEADS].T
    swap = lambda a: jnp.transpose(a, (0, 2, 1))
    for n in ("w_gate", "w_up"):
        weights[n], m_in[n], v_in[n] = swap(weights[n]), swap(m_in[n]), swap(v_in[n])
    shards = _cast_bf16([weights[n] for n in ("w_out", "w_gate", "w_up", "w_down")], "cast_shards")

    gains = {n: weights[n] for n, _ in SMALL}
    loss, grad_x, big, small = _local_grads(
        x.reshape(nb * seq, d), loss_target.reshape(nb * seq, d), gains, w1, wft, None, shards, nb, seq)

    grads = {n: big[n][None] for n in ("w_out", "w_gate", "w_up", "w_down")}
    packed = jnp.concatenate([small[n].reshape(-1) for n, _ in SMALL] + [loss.reshape(1)])
    packed = jnp.pad(packed, (0, SMALL_PACKED[0] * SMALL_PACKED[1] - packed.shape[0])).reshape(SMALL_PACKED)
    summed = _all_sum_small(packed).reshape(-1)
    pos = 0
    for n, size in SMALL:
        grads[n] = summed[pos:pos + size].reshape(1, size)
        pos += size
    loss = summed[pos]

    to_entry = lambda a: jnp.transpose(a, (2, 0, 1))
    deltas, new_m, new_v, grad_out = {}, {}, {}, {}
    for n in ["w_down"] + [n for n in order if n != "w_down"]:
        rider = _share_rider([big["w_in"]]) if n == "w_down" else None
        (deltas[n], new_m[n], new_v[n]), shared = _adamw(weights[n], grads[n], m_in[n], v_in[n], "adamw_" + n, rider)
        if rider is not None:
            grads["w_in"] = to_entry(shared[0][None])
            weights["w_in"], m_in["w_in"], v_in["w_in"] = (to_entry(a) for a in (w_in, m_w_in, v_w_in))
        grad_out[n] = grads[n]
    for n in ("w_gate", "w_up"):
        grad_out[n], deltas[n], new_m[n], new_v[n] = (swap(a) for a in (grad_out[n], deltas[n], new_m[n], new_v[n]))
    from_entry = lambda a: jnp.transpose(a, (1, 2, 0))
    grad_out["w_in"], deltas["w_in"], new_m["w_in"], new_v["w_in"] = (
        from_entry(a) for a in (grad_out["w_in"], deltas["w_in"], new_m["w_in"], new_v["w_in"]))

    return (loss, grad_x.reshape(nb, seq, d), *[grad_out[n] for n in order], *[deltas[n] for n in order],
            *[new_m[n] for n in order], *[new_v[n] for n in order])
```

```python
import functools
import math

import numpy as np
import jax
import jax.numpy as jnp
from jax import lax
from jax.experimental import pallas as pl
from jax.experimental.pallas import tpu as pltpu

F32, BF16 = jnp.float32, jnp.bfloat16
MESH = pl.DeviceIdType.MESH

EPS = 1e-6
NEG = -1e30
HEAD_DIM = 64
SCALE = HEAD_DIM ** -0.5
LOG2E = math.log2(math.e)
LN2 = math.log(2.0)
ROPE_THETA = 500000.0
ROPE_DIM = HEAD_DIM // 4
LANES = 128
W_GROUP = 512
N_FOX_HEADS = 8
VMEM_LIMIT = 56 * 1024 * 1024
DILATIONS = (1, 4, 16)
BAND = 128

ADAM_LR, ADAM_B1, ADAM_B2, ADAM_EPS, ADAM_WD, ADAM_STEP = 0.001, 0.9, 0.999, 1e-08, 0.01, 10

NT = (((1,), (1,)), ((), ()))
TN = (((0,), (0,)), ((), ()))
BATCH_NT = (((2,), (2,)), ((0,), (0,)))
BATCH_NN = (((2,), (1,)), ((0,), (0,)))
BATCH_TN = (((1,), (1,)), ((0,), (0,)))


def _params(sem=None):
    return pltpu.CompilerParams(dimension_semantics=sem, vmem_limit_bytes=VMEM_LIMIT)


def _dot(a, b, dims=None):
    if dims is None:
        return jnp.dot(a, b, preferred_element_type=F32)
    return lax.dot_general(a, b, dims, preferred_element_type=F32)


def _group_ones():
    i = lax.broadcasted_iota(jnp.int32, (LANES, LANES), 0) >> 6
    j = lax.broadcasted_iota(jnp.int32, (LANES, LANES), 1) >> 6
    return (i == j).astype(BF16)


def _split3(x):
    a = x.astype(BF16)
    r = x - a.astype(F32)
    b = r.astype(BF16)
    c = (r - b.astype(F32)).astype(BF16)
    return a, b, c


def _groupsum(x, ones, pieces=2):
    total = None
    for _ in range(pieces):
        piece = x.astype(BF16)
        part = _dot(piece, ones)
        total = part if total is None else total + part
        x = x - piece.astype(F32)
    return total


def _head_masks():
    lane = lax.broadcasted_iota(jnp.int32, (1, LANES), 1)
    return [(lane < HEAD_DIM).astype(F32), (lane >= HEAD_DIM).astype(F32)]


def _head_norm(raw, gain, ones):
    r = lax.rsqrt(_groupsum(raw * raw, ones, 1) * (1.0 / HEAD_DIM) + EPS)
    return raw * r, r


def _head_norm_bwd(dy, xhat, r, gain, ones):
    u = dy * gain
    dgain = jnp.sum(dy * xhat, axis=0, keepdims=True)
    draw = r * (u - xhat * (_groupsum(u * xhat, ones) * (1.0 / HEAD_DIM)))
    return draw, dgain


def _rope(x, cos, s_up, s_dn):
    return x * cos + pltpu.roll(x, LANES - 8, 1) * s_up + pltpu.roll(x, 8, 1) * s_dn


def _rope_bwd(dy, cos, s_up, s_dn):
    return dy * cos + pltpu.roll(dy * s_up, 8, 1) + pltpu.roll(dy * s_dn, LANES - 8, 1)


def _rope_tables(seq):
    half = ROPE_DIM // 2
    inv_freq = jnp.power(jnp.float32(ROPE_THETA), -jnp.arange(half, dtype=F32) * 2.0 / ROPE_DIM)
    ang = jnp.arange(seq).astype(F32)[:, None] * inv_freq[None, :]
    cos, sin = jnp.cos(ang), jnp.sin(ang)
    one = jnp.ones((seq, HEAD_DIM - ROPE_DIM), F32)
    zero_h = jnp.zeros((seq, half), F32)
    zero_r = jnp.zeros((seq, HEAD_DIM - ROPE_DIM), F32)
    c = jnp.concatenate([cos, cos, one], axis=1)
    up = jnp.concatenate([-sin, zero_h, zero_r], axis=1)
    dn = jnp.concatenate([zero_h, sin, zero_r], axis=1)
    return jnp.tile(c, (1, 2)), jnp.tile(up, (1, 2)), jnp.tile(dn, (1, 2))


def _row_tile(rows, cap=256):
    best = rows
    for t in range(8, min(rows, cap) + 1, 8):
        if rows % t == 0:
            best = t
    return best


class _Rider:
    def __init__(self, inputs, out_shapes, n_sems, start, finish, aliases=None, middle=None):
        self.inputs, self.out_shapes, self.n_sems = list(inputs), list(out_shapes), n_sems
        self.start, self.finish, self.middle, self.aliases = start, finish, middle, dict(aliases or {})


def _host_call(body, rider, *, name, grid, in_specs, out_specs, out_shape, scratch_shapes, inputs, semantics):
    if rider is None:
        return pl.pallas_call(body, name=name, grid=grid, in_specs=in_specs, out_specs=out_specs,
                              out_shape=out_shape, scratch_shapes=scratch_shapes,
                              compiler_params=_params(semantics))(*inputs), []
    n_in, n_out, n_scr = len(in_specs), len(out_specs), len(scratch_shapes)
    r_in, r_out = len(rider.inputs), len(rider.out_shapes)

    def wrapped(*refs):
        ins, refs = refs[:n_in], refs[n_in:]
        r_ins, refs = refs[:r_in], refs[r_in:]
        outs, refs = refs[:n_out], refs[n_out:]
        r_outs, refs = refs[:r_out], refs[r_out:]
        scratch, (send_sems, recv_sems) = refs[:n_scr], refs[n_scr:]
        ids = [pl.program_id(a) for a in range(len(grid))]
        first = functools.reduce(lambda p, q: p & q, [i == 0 for i in ids])
        last = functools.reduce(lambda p, q: p & q, [i == g - 1 for i, g in zip(ids, grid)])

        @pl.when(first)
        def _():
            rider.start(r_ins, r_outs, send_sems, recv_sems)

        body(*ins, *outs, *scratch)

        if rider.middle is not None:
            step, steps = ids[0], grid[0]
            for i, g in zip(ids[1:], grid[1:]):
                step, steps = step * g + i, steps * g

            @pl.when(step == (3 * steps) // 4)
            def _():
                rider.middle(r_ins, r_outs, send_sems, recv_sems)

        @pl.when(last)
        def _():
            rider.finish(r_ins, r_outs, send_sems, recv_sems)

    hbm = pl.BlockSpec(memory_space=pl.ANY)
    res = pl.pallas_call(
        wrapped, name=name, grid=grid,
        in_specs=list(in_specs) + [hbm] * r_in, out_specs=list(out_specs) + [hbm] * r_out,
        out_shape=list(out_shape) + rider.out_shapes,
        scratch_shapes=list(scratch_shapes) + [pltpu.SemaphoreType.DMA((rider.n_sems,))] * 2,
        input_output_aliases={n_in + i: n_out + o for i, o in rider.aliases.items()},
        compiler_params=_params(semantics),
    )(*inputs, *rider.inputs)
    return res[:n_out], res[n_out:]


def _idle_host(rider, name):
    def body(o_ref):
        o_ref[...] = jnp.zeros_like(o_ref)

    return _host_call(body, rider, name=name, grid=(1,), in_specs=[],
                      out_specs=[pl.BlockSpec((8, LANES), lambda i: (0, 0))],
                      out_shape=[jax.ShapeDtypeStruct((8, LANES), F32)], scratch_shapes=[], inputs=(),
                      semantics=("arbitrary",))


def _in_proj(x, g_mix, w1, wft):
    t, d = x.shape
    n = w1.shape[1]
    tt = 512

    def body(x_ref, g_ref, w_ref, wf_ref, p_ref, fa_ref, h_ref, ht_ref):
        xx = x_ref[...]
        r = lax.rsqrt(jnp.mean(xx * xx, axis=-1, keepdims=True) + EPS)
        h = (xx * r * g_ref[...]).astype(BF16)
        h_ref[...] = h
        ht_ref[...] = h.T
        for j in range(n // W_GROUP):
            cols = slice(j * W_GROUP, (j + 1) * W_GROUP)
            p_ref[:, cols] = _dot(h, w_ref[:, cols]).astype(BF16)
        fa_ref[...] = _dot(wf_ref[...], h, NT)

    return pl.pallas_call(
        body, name="in_proj", grid=(t // tt,),
        in_specs=[pl.BlockSpec((tt, d), lambda i: (i, 0)), pl.BlockSpec((1, d), lambda i: (0, 0)),
                  pl.BlockSpec(memory_space=pltpu.VMEM), pl.BlockSpec(memory_space=pltpu.VMEM)],
        out_specs=[pl.BlockSpec((tt, n), lambda i: (i, 0)), pl.BlockSpec((8, tt), lambda i: (0, i)),
                   pl.BlockSpec((tt, d), lambda i: (i, 0)), pl.BlockSpec((d, tt), lambda i: (0, i))],
        out_shape=[jax.ShapeDtypeStruct((t, n), BF16), jax.ShapeDtypeStruct((8, t), F32),
                   jax.ShapeDtypeStruct((t, d), BF16), jax.ShapeDtypeStruct((d, t), BF16)],
        compiler_params=_params(("arbitrary",)),
    )(x, g_mix, w1, wft)


def _tri(n, upper):
    i = lax.broadcasted_iota(jnp.int32, (n, n), 0)
    j = lax.broadcasted_iota(jnp.int32, (n, n), 1)
    return ((i <= j) if upper else (i >= j)).astype(BF16)


def _gate_fwd(fa_row, b_col, seq):
    t = fa_row.shape[1]
    cb = 256

    def body(fa_ref, b_ref, c_ref):
        tri = _tri(cb, True)
        carry = jnp.zeros((8, 1), F32)
        for k in range(seq // cb):
            z = fa_ref[:, k * cb:(k + 1) * cb] + b_ref[...]
            lf = jnp.minimum(z, 0.0) - jnp.log(1.0 + jnp.exp(-jnp.abs(z)))
            a, b, c = _split3(lf)
            blk = _dot(a, tri) + _dot(b, tri) + _dot(c, tri) + carry
            c_ref[:, k * cb:(k + 1) * cb] = blk
            carry = blk[:, cb - 1:cb]

    return pl.pallas_call(
        body, name="gate_fwd", grid=(t // seq,),
        in_specs=[pl.BlockSpec((8, seq), lambda i: (0, i)), pl.BlockSpec((8, 1), lambda i: (0, 0))],
        out_specs=pl.BlockSpec((8, seq), lambda i: (0, i)),
        out_shape=jax.ShapeDtypeStruct((8, t), F32),
        compiler_params=_params(("arbitrary",)),
    )(fa_row, b_col)


def _gate_bwd(dc_row, fa_row, b_col, seq):
    t = fa_row.shape[1]
    cb = 256

    def body(dc_ref, fa_ref, b_ref, dfa_ref, db_ref):
        @pl.when(pl.program_id(0) == 0)
        def _():
            db_ref[...] = jnp.zeros_like(db_ref)

        tri = _tri(cb, False)
        carry = jnp.zeros((8, 1), F32)
        dbs = jnp.zeros((8, 1), F32)
        for k in reversed(range(seq // cb)):
            a, b, c = _split3(dc_ref[:, k * cb:(k + 1) * cb])
            dlf = _dot(a, tri) + _dot(b, tri) + _dot(c, tri) + carry
            carry = dlf[:, 0:1]
            z = fa_ref[:, k * cb:(k + 1) * cb] + b_ref[...]
            dfa = dlf / (1.0 + jnp.exp(z))
            dfa_ref[:, k * cb:(k + 1) * cb] = dfa
            dbs = dbs + jnp.sum(dfa, axis=1, keepdims=True)
        db_ref[...] += jnp.broadcast_to(dbs, (8, LANES))

    return pl.pallas_call(
        body, name="gate_bwd", grid=(t // seq,),
        in_specs=[pl.BlockSpec((8, seq), lambda i: (0, i)), pl.BlockSpec((8, seq), lambda i: (0, i)),
                  pl.BlockSpec((8, 1), lambda i: (0, 0))],
        out_specs=[pl.BlockSpec((8, seq), lambda i: (0, i)), pl.BlockSpec((8, LANES), lambda i: (0, 0))],
        out_shape=[jax.ShapeDtypeStruct((8, t), F32), jax.ShapeDtypeStruct((8, LANES), F32)],
        compiler_params=_params(("arbitrary",)),
    )(dc_row, fa_row, b_col)


def _attn_out(o_fox, o_dil, x, g_fox, g_dil, w_out):
    t, d = x.shape
    w = o_fox.shape[1]
    tt = 512

    def body(of_ref, od_ref, x_ref, gf_ref, gd_ref, w_ref, x1_ref, ont_ref):
        acc = x_ref[...]
        for k, (o_ref, g_ref) in enumerate(((of_ref, gf_ref), (od_ref, gd_ref))):
            o = o_ref[...]
            r = lax.rsqrt(jnp.mean(o * o, axis=-1, keepdims=True) + EPS)
            on = (o * r * g_ref[...]).astype(BF16)
            ont_ref[k * w:(k + 1) * w, :] = on.T
            acc = acc + _dot(on, w_ref[k * w:(k + 1) * w, :])
        x1_ref[...] = acc

    return pl.pallas_call(
        body, name="attn_out", grid=(t // tt,),
        in_specs=[pl.BlockSpec((tt, w), lambda i: (i, 0)), pl.BlockSpec((tt, w), lambda i: (i, 0)),
                  pl.BlockSpec((tt, d), lambda i: (i, 0)), pl.BlockSpec((1, w), lambda i: (0, 0)),
                  pl.BlockSpec((1, w), lambda i: (0, 0)), pl.BlockSpec(memory_space=pltpu.VMEM)],
        out_specs=[pl.BlockSpec((tt, d), lambda i: (i, 0)), pl.BlockSpec((2 * w, tt), lambda i: (0, i))],
        out_shape=[jax.ShapeDtypeStruct((t, d), F32), jax.ShapeDtypeStruct((2 * w, t), BF16)],
        compiler_params=_params(("arbitrary",)),
    )(o_fox, o_dil, x, g_fox, g_dil, w_out)


def _attn_out_bwd(dx1, o_fox, o_dil, g_fox, g_dil, w_out, rider=None):
    t, d = dx1.shape
    w = o_fox.shape[1]
    tt = 512

    def body(dx_ref, of_ref, od_ref, gf_ref, gd_ref, w_ref, dof_ref, dod_ref, dgf_ref, dgd_ref):
        @pl.when(pl.program_id(0) == 0)
        def _():
            dgf_ref[...] = jnp.zeros_like(dgf_ref)
            dgd_ref[...] = jnp.zeros_like(dgd_ref)

        dxb = dx_ref[...].astype(BF16)
        for k, (o_ref, g_ref, do_ref, dg_ref) in enumerate(
                ((of_ref, gf_ref, dof_ref, dgf_ref), (od_ref, gd_ref, dod_ref, dgd_ref))):
            don = _dot(dxb, w_ref[k * w:(k + 1) * w, :], NT)
            o = o_ref[...]
            r = lax.rsqrt(jnp.mean(o * o, axis=-1, keepdims=True) + EPS)
            xhat = o * r
            u = don * g_ref[...]
            do_ref[...] = r * (u - xhat * jnp.mean(u * xhat, axis=-1, keepdims=True))
            dg_ref[0:1, :] += jnp.sum(don * xhat, axis=0, keepdims=True)

    return _host_call(
        body, rider, name="attn_out_bwd", grid=(t // tt,),
        in_specs=[pl.BlockSpec((tt, d), lambda i: (i, 0)), pl.BlockSpec((tt, w), lambda i: (i, 0)),
                  pl.BlockSpec((tt, w), lambda i: (i, 0)), pl.BlockSpec((1, w), lambda i: (0, 0)),
                  pl.BlockSpec((1, w), lambda i: (0, 0)), pl.BlockSpec(memory_space=pltpu.VMEM)],
        out_specs=[pl.BlockSpec((tt, w), lambda i: (i, 0)), pl.BlockSpec((tt, w), lambda i: (i, 0)),
                   pl.BlockSpec((8, w), lambda i: (0, 0)), pl.BlockSpec((8, w), lambda i: (0, 0))],
        out_shape=[jax.ShapeDtypeStruct((t, w), F32), jax.ShapeDtypeStruct((t, w), F32),
                   jax.ShapeDtypeStruct((8, w), F32), jax.ShapeDtypeStruct((8, w), F32)],
        scratch_shapes=[], inputs=(dx1, o_fox, o_dil, g_fox, g_dil, w_out), semantics=("arbitrary",))


def _ffn_fwd(x1, target, g_ffn, w_gate, w_up, w_down):
    t, d = x1.shape
    f = w_gate.shape[0]
    tt = 256

    def body(x_ref, t_ref, g_ref, wg_ref, wu_ref, wd_ref, a_ref, u_ref, dy_ref, loss_ref):
        xx = x_ref[...]
        r = lax.rsqrt(jnp.mean(xx * xx, axis=-1, keepdims=True) + EPS)
        h = (xx * r * g_ref[...]).astype(BF16)
        a = _dot(h, wg_ref[...], NT)
        u = _dot(h, wu_ref[...], NT)
        a_ref[...] = a.astype(BF16)
        u_ref[...] = u.astype(BF16)
        s = (a / (1.0 + jnp.exp(-a)) * u).astype(BF16)
        y = xx + _dot(s, wd_ref[...])
        e = y - t_ref[...]
        dy_ref[...] = e * (1.0 / d)
        loss_ref[...] = jnp.broadcast_to(0.5 * jnp.sum(e * e) * (1.0 / d), (1, 8, LANES))

    return pl.pallas_call(
        body, name="ffn_fwd", grid=(t // tt,),
        in_specs=[pl.BlockSpec((tt, d), lambda i: (i, 0)), pl.BlockSpec((tt, d), lambda i: (i, 0)),
                  pl.BlockSpec((1, d), lambda i: (0, 0)), pl.BlockSpec(memory_space=pltpu.VMEM),
                  pl.BlockSpec(memory_space=pltpu.VMEM), pl.BlockSpec(memory_space=pltpu.VMEM)],
        out_specs=[pl.BlockSpec((tt, f), lambda i: (i, 0)), pl.BlockSpec((tt, f), lambda i: (i, 0)),
                   pl.BlockSpec((tt, d), lambda i: (i, 0)), pl.BlockSpec((1, 8, LANES), lambda i: (i, 0, 0))],
        out_shape=[jax.ShapeDtypeStruct((t, f), BF16), jax.ShapeDtypeStruct((t, f), BF16),
                   jax.ShapeDtypeStruct((t, d), F32), jax.ShapeDtypeStruct((t // tt, 8, LANES), F32)],
        compiler_params=_params(("arbitrary",)),
    )(x1, target, g_ffn, w_gate, w_up, w_down)


def _ffn_bwd(dy, a, u, x1, g_ffn, w_gate, w_up, w_down):
    t, d = x1.shape
    f = w_gate.shape[0]
    tt = 256

    def body(dy_ref, a_ref, u_ref, x_ref, g_ref, wg_ref, wu_ref, wd_ref,
             dx_ref, s_ref, da_ref, du_ref, h_ref, dg_ref):
        @pl.when(pl.program_id(0) == 0)
        def _():
            dg_ref[...] = jnp.zeros_like(dg_ref)

        dy_ = dy_ref[...]
        ds = _dot(dy_.astype(BF16), wd_ref[...], NT)
        a_ = a_ref[...].astype(F32)
        u_ = u_ref[...].astype(F32)
        sig = 1.0 / (1.0 + jnp.exp(-a_))
        silu = a_ * sig
        s_ref[...] = (silu * u_).astype(BF16)
        da = (ds * u_ * (sig * (1.0 + a_ * (1.0 - sig)))).astype(BF16)
        du = (ds * silu).astype(BF16)
        da_ref[...] = da
        du_ref[...] = du
        dh = _dot(da, wg_ref[...]) + _dot(du, wu_ref[...])
        xx = x_ref[...]
        r = lax.rsqrt(jnp.mean(xx * xx, axis=-1, keepdims=True) + EPS)
        xhat = xx * r
        g = g_ref[...]
        h_ref[...] = (xhat * g).astype(BF16)
        uu = dh * g
        dx_ref[...] = dy_ + r * (uu - xhat * jnp.mean(uu * xhat, axis=-1, keepdims=True))
        dg_ref[0:1, :] += jnp.sum(dh * xhat, axis=0, keepdims=True)

    return pl.pallas_call(
        body, name="ffn_bwd", grid=(t // tt,),
        in_specs=[pl.BlockSpec((tt, d), lambda i: (i, 0)), pl.BlockSpec((tt, f), lambda i: (i, 0)),
                  pl.BlockSpec((tt, f), lambda i: (i, 0)), pl.BlockSpec((tt, d), lambda i: (i, 0)),
                  pl.BlockSpec((1, d), lambda i: (0, 0)), pl.BlockSpec(memory_space=pltpu.VMEM),
                  pl.BlockSpec(memory_space=pltpu.VMEM), pl.BlockSpec(memory_space=pltpu.VMEM)],
        out_specs=[pl.BlockSpec((tt, d), lambda i: (i, 0)), pl.BlockSpec((tt, f), lambda i: (i, 0)),
                   pl.BlockSpec((tt, f), lambda i: (i, 0)), pl.BlockSpec((tt, f), lambda i: (i, 0)),
                   pl.BlockSpec((tt, d), lambda i: (i, 0)), pl.BlockSpec((8, d), lambda i: (0, 0))],
        out_shape=[jax.ShapeDtypeStruct((t, d), F32), jax.ShapeDtypeStruct((t, f), BF16),
                   jax.ShapeDtypeStruct((t, f), BF16), jax.ShapeDtypeStruct((t, f), BF16),
                   jax.ShapeDtypeStruct((t, d), BF16), jax.ShapeDtypeStruct((8, d), F32)],
        compiler_params=_params(("arbitrary",)),
    )(dy, a, u, x1, g_ffn, w_gate, w_up, w_down)


def _in_proj_bwd(dparts, dfa_row, w1, wft, x, g_mix, dx1, rider=None):
    t, d = x.shape
    tt = 512
    npart = len(dparts)

    def body(*refs):
        dp_refs = refs[:npart]
        dfa_ref, w_ref, wf_ref, x_ref, g_ref, dx1_ref, dx_ref, dg_ref = refs[npart:]

        @pl.when(pl.program_id(0) == 0)
        def _():
            dg_ref[...] = jnp.zeros_like(dg_ref)

        dh = _dot(dfa_ref[...].astype(BF16), wf_ref[...], TN)
        for j in range(npart):
            dh = dh + _dot(dp_refs[j][...], w_ref[:, j * W_GROUP:(j + 1) * W_GROUP], NT)
        xx = x_ref[...]
        r = lax.rsqrt(jnp.mean(xx * xx, axis=-1, keepdims=True) + EPS)
        xhat = xx * r
        uu = dh * g_ref[...]
        dx_ref[...] = dx1_ref[...] + r * (uu - xhat * jnp.mean(uu * xhat, axis=-1, keepdims=True))
        dg_ref[0:1, :] += jnp.sum(dh * xhat, axis=0, keepdims=True)

    return _host_call(
        body, rider, name="in_proj_bwd", grid=(t // tt,),
        in_specs=[pl.BlockSpec((tt, W_GROUP), lambda i: (i, 0)) for _ in range(npart)]
        + [pl.BlockSpec((8, tt), lambda i: (0, i)), pl.BlockSpec(memory_space=pltpu.VMEM),
           pl.BlockSpec(memory_space=pltpu.VMEM), pl.BlockSpec((tt, d), lambda i: (i, 0)),
           pl.BlockSpec((1, d), lambda i: (0, 0)), pl.BlockSpec((tt, d), lambda i: (i, 0))],
        out_specs=[pl.BlockSpec((tt, d), lambda i: (i, 0)), pl.BlockSpec((8, d), lambda i: (0, 0))],
        out_shape=[jax.ShapeDtypeStruct((t, d), F32), jax.ShapeDtypeStruct((8, d), F32)],
        scratch_shapes=[], inputs=(*dparts, dfa_row, w1, wft, x, g_mix, dx1), semantics=("arbitrary",))


def _token_matmul(a, b, name, tn, a_is_transposed=True):
    m, t = a.shape if a_is_transposed else a.shape[::-1]
    n = b.shape[1]
    tk = 1024

    def body(a_ref, b_ref, o_ref):
        @pl.when(pl.program_id(1) == 0)
        def _():
            o_ref[...] = jnp.zeros_like(o_ref)

        o_ref[...] += _dot(a_ref[...], b_ref[...].astype(BF16), None if a_is_transposed else TN)

    a_spec = pl.BlockSpec((m, tk), lambda j, k: (0, k)) if a_is_transposed else pl.BlockSpec((tk, m), lambda j, k: (k, 0))
    return pl.pallas_call(
        body, name=name, grid=(n // tn, t // tk),
        in_specs=[a_spec, pl.BlockSpec((tk, tn), lambda j, k: (k, j))],
        out_specs=pl.BlockSpec((m, tn), lambda j, k: (0, j)),
        out_shape=jax.ShapeDtypeStruct((m, n), F32),
        compiler_params=_params(("arbitrary", "arbitrary")),
    )(a, b)


def _token_matmul_parts(at, parts, name):
    m, t = at.shape
    widths = [p.shape[1] for p in parts]
    tk = 1024

    def body(a_ref, *refs):
        o_ref = refs[-1]

        @pl.when(pl.program_id(0) == 0)
        def _():
            o_ref[...] = jnp.zeros_like(o_ref)

        a, first = a_ref[...], 0
        for b_ref, w in zip(refs[:-1], widths):
            o_ref[:, first:first + w] += _dot(a, b_ref[...])
            first += w

    return pl.pallas_call(
        body, name=name, grid=(t // tk,),
        in_specs=[pl.BlockSpec((m, tk), lambda k: (0, k))] + [pl.BlockSpec((tk, w), lambda k: (k, 0)) for w in widths],
        out_specs=pl.BlockSpec((m, sum(widths)), lambda k: (0, 0)),
        out_shape=jax.ShapeDtypeStruct((m, sum(widths)), F32),
        compiler_params=_params(("arbitrary",)),
    )(at, *parts)


def _row_matmul(a_row, b, name):
    t, n = b.shape
    tk = 1024
    nk = t // tk

    def body(a_ref, b_ref, o_ref):
        @pl.when(pl.program_id(0) == 0)
        def _():
            o_ref[...] = jnp.zeros_like(o_ref)

        o_ref[...] += _dot(a_ref[...].astype(BF16), b_ref[...])

    return pl.pallas_call(
        body, name=name, grid=(nk,),
        in_specs=[pl.BlockSpec((8, tk), lambda k: (0, k)), pl.BlockSpec((tk, n), lambda k: (k, 0))],
        out_specs=pl.BlockSpec((8, n), lambda k: (0, 0)),
        out_shape=jax.ShapeDtypeStruct((8, n), F32),
        compiler_params=_params(("arbitrary",)),
    )(a_row, b)


FOX_TQ = 512
SUM_LANE = (HEAD_DIM, 0)


def _fox_fwd(proj, c3, gq, gk, nb, seq, rider=None):
    t = nb * seq
    tq = FOX_TQ
    nq = seq // tq
    npair = N_FOX_HEADS // 2

    def body(q_ref, k_ref, v_ref, c_ref, gq_ref, gk_ref, o_ref, lse_ref, qs, ks, vs):
        ones = _group_ones()
        masks = _head_masks()
        qhat, _ = _head_norm(q_ref[...].astype(F32), None, ones)
        khat, _ = _head_norm(k_ref[...].astype(F32), None, ones)
        qs[...] = (qhat * gq_ref[...] * (SCALE * LOG2E)).astype(BF16)
        kn = khat * gk_ref[...]
        vv = v_ref[...].astype(F32)
        lane = lax.broadcasted_iota(jnp.int32, (1, LANES), 1)
        for hd in range(2):
            ks[hd] = (kn * masks[hd]).astype(BF16)
            vs[hd] = (vv * masks[hd] + (lane == SUM_LANE[hd]).astype(F32)).astype(BF16)
        row = lax.broadcasted_iota(jnp.int32, (tq, tq), 0)
        col = lax.broadcasted_iota(jnp.int32, (tq, tq), 1)
        causal = col <= row

        for qi in range(nq):
            q0 = qi * tq
            q_blk = qs[q0:q0 + tq, :]
            o_tot = jnp.zeros((tq, LANES), F32)
            lse_tot = jnp.zeros((tq, LANES), F32)
            for hd in range(2):
                crow = c_ref[0, hd:hd + 1, 0:q0 + tq] * LOG2E
                c0 = crow[:, q0:q0 + 1]
                s_d = _dot(q_blk, ks[hd, q0:q0 + tq, :], NT) + (c0 - crow[:, q0:q0 + tq])
                s_d = jnp.where(causal, s_d, NEG)
                m = jnp.max(s_d, axis=-1, keepdims=True)
                if qi > 0:
                    s_o = _dot(q_blk, ks[hd, 0:q0, :], NT) + (c0 - crow[:, 0:q0])
                    m = jnp.maximum(m, jnp.max(s_o, axis=-1, keepdims=True))
                acc = _dot(jnp.exp2(s_d - m).astype(BF16), vs[hd, q0:q0 + tq, :])
                if qi > 0:
                    acc = acc + _dot(jnp.exp2(s_o - m).astype(BF16), vs[hd, 0:q0, :])
                l = acc[:, SUM_LANE[hd]:SUM_LANE[hd] + 1]
                o_tot = o_tot + (acc / l) * masks[hd]
                lse_tot = lse_tot + (m + jnp.log2(l) - c0) * masks[hd]
            o_ref[q0:q0 + tq, :] = o_tot
            lse_ref[q0:q0 + tq, :] = lse_tot

    blk = lambda off: pl.BlockSpec((seq, LANES), lambda b, p: (b, off + p))
    return _host_call(
        body, rider, name="fox_fwd", grid=(nb, npair),
        in_specs=[blk(0), blk(npair), blk(2 * npair), pl.BlockSpec((1, 2, seq), lambda b, p: (p, 0, b)),
                  pl.BlockSpec((1, LANES), lambda b, p: (0, 0)), pl.BlockSpec((1, LANES), lambda b, p: (0, 0))],
        out_specs=[blk(0), blk(0)],
        out_shape=[jax.ShapeDtypeStruct((t, W_GROUP), F32), jax.ShapeDtypeStruct((t, W_GROUP), F32)],
        scratch_shapes=[pltpu.VMEM((seq, LANES), BF16), pltpu.VMEM((2, seq, LANES), BF16),
                        pltpu.VMEM((2, seq, LANES), BF16)],
        inputs=(proj, proj, proj, c3, gq, gk), semantics=("arbitrary", "arbitrary"))


def _fox_bwd(proj, c3, gq, gk, do, o, lse, nb, seq, rider=None):
    t = nb * seq
    tq = FOX_TQ
    nq = seq // tq
    npair = N_FOX_HEADS // 2

    def body(q_ref, k_ref, v_ref, c_ref, gq_ref, gk_ref, do_ref, o_ref, lse_ref,
             dq_ref, dk_ref, dv_ref, dc_ref, dg_ref, qs, ks, vs, kts, dos, lse_t, delta_t, dqt_acc, dk_acc, dv_acc,
             row_sum):
        @pl.when((pl.program_id(0) == 0) & (pl.program_id(1) == 0))
        def _():
            dg_ref[...] = jnp.zeros_like(dg_ref)

        ones = _group_ones()
        masks = _head_masks()
        qhat, rq = _head_norm(q_ref[...].astype(F32), None, ones)
        khat, rk = _head_norm(k_ref[...].astype(F32), None, ones)
        qs[...] = (qhat * gq_ref[...] * (SCALE * LOG2E)).astype(BF16)
        kn = khat * gk_ref[...]
        vv = v_ref[...].astype(F32)
        for hd in range(2):
            ks[hd] = (kn * masks[hd]).astype(BF16)
            vs[hd] = (vv * masks[hd]).astype(BF16)
            kts[hd] = ks[hd].T
        dof = do_ref[...]
        dos[...] = dof.astype(BF16)
        lse_t[...] = lse_ref[...].T
        delta_t[...] = _groupsum(dof * o_ref[...], ones).T
        dqt_acc[...] = jnp.zeros_like(dqt_acc)
        dk_acc[...] = jnp.zeros_like(dk_acc)
        dv_acc[...] = jnp.zeros_like(dv_acc)
        row_sum[...] = jnp.zeros_like(row_sum)
        key = lax.broadcasted_iota(jnp.int32, (tq, tq), 0)
        qry = lax.broadcasted_iota(jnp.int32, (tq, tq), 1)
        causal = key <= qry

        for hd in range(2):
            lane0 = hd * HEAD_DIM
            for kj in range(nq):
                k0 = kj * tq
                k_blk = ks[hd, k0:k0 + tq, :]
                v_blk = vs[hd, k0:k0 + tq, :]
                kt_blk = kts[hd, :, k0:k0 + tq]
                crow = c_ref[0, hd:hd + 1, k0:k0 + tq] * LOG2E
                ck0 = crow[:, 0:1]
                bias = jnp.broadcast_to(ck0 - crow, (LANES, tq)).T[:, 0:1]

                def queries_step(r0, r1, diag, hd=hd, lane0=lane0, k_blk=k_blk, v_blk=v_blk, kt_blk=kt_blk,
                                 bias=bias, ck0=ck0):
                    q_r = qs[r0:r1, :]
                    do_r = dos[r0:r1, :]
                    z = _dot(k_blk, q_r, NT) + bias
                    p = jnp.exp2(z - (lse_t[lane0:lane0 + 1, r0:r1] + ck0))
                    if diag:
                        p = jnp.where(causal, p, 0.0)
                    dp = _dot(v_blk, do_r, NT)
                    ds = p * (dp - delta_t[lane0:lane0 + 1, r0:r1])
                    dsb = ds.astype(BF16)
                    dqt_acc[:, r0:r1] += _dot(kt_blk, dsb)
                    row_sum[hd:hd + 1, r0:r1] += jnp.sum(ds, axis=0, keepdims=True)
                    return _dot(dsb, q_r), _dot(p.astype(BF16), do_r), -jnp.sum(ds, axis=1, keepdims=True)

                dk_j, dv_j, dc_j = queries_step(k0, k0 + tq, True)
                if k0 + tq < seq:
                    dk_o, dv_o, dc_o = queries_step(k0 + tq, seq, False)
                    dk_j, dv_j, dc_j = dk_j + dk_o, dv_j + dv_o, dc_j + dc_o
                dk_acc[k0:k0 + tq, :] += dk_j * masks[hd]
                dv_acc[k0:k0 + tq, :] += dv_j * masks[hd]
                dc_ref[0, hd:hd + 1, k0:k0 + tq] = jnp.broadcast_to(dc_j, (tq, LANES)).T[0:1, :]

        dc_ref[0] += row_sum[0:2, :]

        dq_raw, dgq = _head_norm_bwd(dqt_acc[...].T * SCALE, qhat, rq, gq_ref[...], ones)
        dk_raw, dgk = _head_norm_bwd(dk_acc[...] * LN2, khat, rk, gk_ref[...], ones)
        dq_ref[...] = dq_raw.astype(BF16)
        dk_ref[...] = dk_raw.astype(BF16)
        dv_ref[...] = dv_acc[...].astype(BF16)
        dg_ref[0:1, :] += dgq
        dg_ref[1:2, :] += dgk

    blk = lambda off: pl.BlockSpec((seq, LANES), lambda b, p: (b, off + p))
    vec = pl.BlockSpec((1, LANES), lambda b, p: (0, 0))
    c_spec = pl.BlockSpec((1, 2, seq), lambda b, p: (p, 0, b))
    return _host_call(
        body, rider, name="fox_bwd", grid=(nb, npair),
        in_specs=[blk(0), blk(npair), blk(2 * npair), c_spec, vec, vec, blk(0), blk(0), blk(0)],
        out_specs=[blk(0), blk(0), blk(0), c_spec, pl.BlockSpec((8, LANES), lambda b, p: (0, 0))],
        out_shape=[jax.ShapeDtypeStruct((t, W_GROUP), BF16), jax.ShapeDtypeStruct((t, W_GROUP), BF16),
                   jax.ShapeDtypeStruct((t, W_GROUP), BF16), jax.ShapeDtypeStruct((npair, 2, t), F32),
                   jax.ShapeDtypeStruct((8, LANES), F32)],
        scratch_shapes=[pltpu.VMEM((seq, LANES), BF16), pltpu.VMEM((2, seq, LANES), BF16),
                        pltpu.VMEM((2, seq, LANES), BF16), pltpu.VMEM((2, LANES, seq), BF16),
                        pltpu.VMEM((seq, LANES), BF16), pltpu.VMEM((LANES, seq), F32),
                        pltpu.VMEM((LANES, seq), F32), pltpu.VMEM((LANES, seq), F32),
                        pltpu.VMEM((seq, LANES), F32), pltpu.VMEM((seq, LANES), F32),
                        pltpu.VMEM((8, seq), F32)],
        inputs=(proj, proj, proj, c3, gq, gk, do, o, lse), semantics=("arbitrary", "arbitrary"))


def _dil_prep(q_ref, k_ref, gq_ref, gk_ref, cos_ref, up_ref, dn_ref, ones):
    qhat, rq = _head_norm(q_ref[...].astype(F32), None, ones)
    khat, rk = _head_norm(k_ref[...].astype(F32), None, ones)
    cos, up, dn = cos_ref[...], up_ref[...], dn_ref[...]
    qn = _rope(qhat * gq_ref[...], cos, up, dn) * (SCALE * LOG2E)
    kn = _rope(khat * gk_ref[...], cos, up, dn)
    return qhat, rq, khat, rk, qn, kn


def _dil_keys(d, seq, pairs):
    nblk = seq // BAND
    per_res = seq // (d * BAND)
    as_blocks = lambda ref, rows: ref[rows, :].reshape(-1, BAND, LANES)
    if per_res == 1:
        a = lax.broadcasted_iota(jnp.int32, (1, BAND, BAND), 1)
        j = lax.broadcasted_iota(jnp.int32, (1, BAND, BAND), 2)
        causal = jnp.where(j <= a, 0.0, NEG)
        return [as_blocks(src, slice(0, seq)) for src, _ in pairs], [causal]
    for src, dst in pairs:
        dst[:, BAND:, :] = as_blocks(src, slice(0, seq))
        dst[1:, :BAND, :] = as_blocks(src, slice(0, seq - BAND))
        dst[0:1, :BAND, :] = jnp.zeros((1, BAND, LANES), BF16)
    a = lax.broadcasted_iota(jnp.int32, (1, BAND, 2 * BAND), 1)
    j = lax.broadcasted_iota(jnp.int32, (1, BAND, 2 * BAND), 2)
    band = jnp.where(((j < BAND) & (j >= a)) | ((j >= BAND) & (j - BAND <= a)), 0.0, NEG)
    e = lax.broadcasted_iota(jnp.int32, (nblk, 1, 2 * BAND), 0)
    j = lax.broadcasted_iota(jnp.int32, (nblk, 1, 2 * BAND), 2)
    no_prev = jnp.where(((e & (per_res - 1)) == 0) & (j < BAND), NEG, 0.0)
    return [dst[...] for _, dst in pairs], [band + no_prev]


def _regroup(d, seq):
    if d == 1:
        return [(slice(0, seq), slice(0, seq))]
    before, n = d // 4, seq // d
    return [(pl.ds(r1 * (seq // before) + r2, n, stride=4), slice((before * r2 + r1) * n, (before * r2 + r1 + 1) * n))
            for r1 in range(before) for r2 in range(4)]


def _dil_fwd(proj, gq, gk, cos, up, dn, nb, seq):
    t = nb * seq
    npair = W_GROUP // LANES
    off = 3 * npair

    def body(q_ref, k_ref, v_ref, gq_ref, gk_ref, cos_ref, up_ref, dn_ref, o_ref, lse_ref,
             src_a, src_b, qp, kp, vp, kw, vw, m_b, l_b, o_b, state_a, state_b):
        ones = _group_ones()
        masks = _head_masks()
        _, _, _, _, qn, kn = _dil_prep(q_ref, k_ref, gq_ref, gk_ref, cos_ref, up_ref, dn_ref, ones)
        src_a[0] = qn
        src_a[1] = kn
        src_a[2] = v_ref[...].astype(F32)
        nblk = seq // BAND
        src, state = (src_a, src_b), (state_a, state_b)

        for d in DILATIONS:
            last = d == DILATIONS[-1]
            for before, after in _regroup(d, seq):
                qv, kv, vv = src[0].at[0][before, :], src[0].at[1][before, :], src[0].at[2][before, :]
                for hd in range(2):
                    qp[hd, after, :] = (qv * masks[hd]).astype(BF16)
                kp[after, :] = kv.astype(BF16)
                vp[after, :] = vv.astype(BF16)
                if d > 1 and not last:
                    src[1][0, after, :], src[1][1, after, :], src[1][2, after, :] = qv, kv, vv
            if d > 1:
                src = src[::-1]
            (keys_k, keys_v), bias = _dil_keys(d, seq, [(kp, kw), (vp, vw)])
            m_t = jnp.zeros((nblk, BAND, LANES), F32)
            l_t = jnp.zeros((nblk, BAND, LANES), F32)
            o_t = jnp.zeros((nblk, BAND, LANES), F32)
            for hd in range(2):
                s = _dot(qp[hd].reshape(nblk, BAND, LANES), keys_k, BATCH_NT)
                for b_ in bias:
                    s = s + b_
                m = jnp.max(s, axis=-1, keepdims=True)
                p = jnp.exp2(s - m)
                m_t = m_t + m * masks[hd]
                l_t = l_t + jnp.sum(p, axis=-1, keepdims=True) * masks[hd]
                o_t = o_t + _dot(p.astype(BF16), keys_v, BATCH_NN) * masks[hd]
            if d == 1:
                state[0][0] = m_t.reshape(seq, LANES)
                state[0][1] = l_t.reshape(seq, LANES)
                state[0][2] = o_t.reshape(seq, LANES)
                continue
            m_b[...] = m_t.reshape(seq, LANES)
            l_b[...] = l_t.reshape(seq, LANES)
            o_b[...] = o_t.reshape(seq, LANES)
            for before, after in _regroup(d, seq):
                m_old = state[0].at[0][before, :]
                m_new = jnp.maximum(m_old, m_b[after, :])
                w_old = jnp.exp2(m_old - m_new)
                w_new = jnp.exp2(m_b[after, :] - m_new)
                state[1][0, after, :] = m_new
                state[1][1, after, :] = state[0].at[1][before, :] * w_old + l_b[after, :] * w_new
                state[1][2, after, :] = state[0].at[2][before, :] * w_old + o_b[after, :] * w_new
            state = state[::-1]

        l = state[0][1]
        o_b[...] = state[0][2] / l
        l_b[...] = state[0][0] + jnp.log2(l)
        held, spare = [o_b, l_b], [m_b, state[1].at[0]]
        for d in DILATIONS[:0:-1]:
            dests = [o_ref, lse_ref] if d == DILATIONS[1] else spare
            for h, dst in zip(held, dests):
                for before, after in _regroup(d, seq):
                    dst[before, :] = h[after, :]
            held, spare = dests, held

    blk = lambda o_: pl.BlockSpec((seq, LANES), lambda b, p: (b, o_ + p))
    vec = pl.BlockSpec((1, LANES), lambda b, p: (0, 0))
    tab = pl.BlockSpec(memory_space=pltpu.VMEM)
    f32_buf = pltpu.VMEM((seq, LANES), F32)
    f32_x3 = pltpu.VMEM((3, seq, LANES), F32)
    bf16_buf = pltpu.VMEM((seq, LANES), BF16)
    window_buf = pltpu.VMEM((seq // BAND, 2 * BAND, LANES), BF16)
    return pl.pallas_call(
        body, name="dil_fwd", grid=(nb, npair),
        in_specs=[blk(off), blk(off + npair), blk(off + 2 * npair), vec, vec, tab, tab, tab],
        out_specs=[blk(0), blk(0)],
        out_shape=[jax.ShapeDtypeStruct((t, W_GROUP), F32), jax.ShapeDtypeStruct((t, W_GROUP), F32)],
        scratch_shapes=[f32_x3, f32_x3, pltpu.VMEM((2, seq, LANES), BF16), bf16_buf, bf16_buf,
                        window_buf, window_buf, f32_buf, f32_buf, f32_buf, f32_x3, f32_x3],
        compiler_params=_params(("arbitrary", "arbitrary")),
    )(proj, proj, proj, gq, gk, cos, up, dn)


def _dil_bwd(proj, gq, gk, cos, up, dn, do, o, lse, nb, seq, rider=None):
    t = nb * seq
    npair = W_GROUP // LANES
    off = 3 * npair

    def body(q_ref, k_ref, v_ref, gq_ref, gk_ref, cos_ref, up_ref, dn_ref, do_ref, o_ref, lse_ref,
             dq_ref, dk_ref, dv_ref, dg_ref, src_a, src_b, sums_a, sums_b,
             qp, kp, vp, dop, kw, vw, lse_p, delta_p, dq_p, dk_p, dv_p):
        @pl.when((pl.program_id(0) == 0) & (pl.program_id(1) == 0))
        def _():
            dg_ref[...] = jnp.zeros_like(dg_ref)

        ones = _group_ones()
        masks = _head_masks()
        qhat, rq, khat, rk, qn, kn = _dil_prep(q_ref, k_ref, gq_ref, gk_ref, cos_ref, up_ref, dn_ref, ones)
        src_a[0] = qn
        src_a[1] = kn
        src_a[2] = v_ref[...].astype(F32)
        src_a[3] = do_ref[...]
        src_a[4] = lse_ref[...]
        src_a[5] = _groupsum(do_ref[...] * o_ref[...], ones)
        nblk = seq // BAND
        src, sums = (src_a, src_b), (sums_a, sums_b)

        for d in DILATIONS:
            last = d == DILATIONS[-1]
            for before, after in _regroup(d, seq):
                planes = [src[0].at[i][before, :] for i in range(6)]
                for hd in range(2):
                    qp[hd, after, :] = (planes[0] * masks[hd]).astype(BF16)
                    dop[hd, after, :] = (planes[3] * masks[hd]).astype(BF16)
                kp[after, :] = planes[1].astype(BF16)
                vp[after, :] = planes[2].astype(BF16)
                lse_p[after, :] = planes[4]
                delta_p[after, :] = planes[5]
                if d > 1 and not last:
                    for i in range(6):
                        src[1][i, after, :] = planes[i]
            if d > 1:
                src = src[::-1]
            (keys_k, keys_v), bias = _dil_keys(d, seq, [(kp, kw), (vp, vw)])
            nk = keys_k.shape[1]
            dq_b = jnp.zeros((nblk, BAND, LANES), F32)
            dk_b = jnp.zeros((nblk, nk, LANES), F32)
            dv_b = jnp.zeros((nblk, nk, LANES), F32)
            for hd in range(2):
                lane0 = hd * HEAD_DIM
                q3 = qp[hd].reshape(nblk, BAND, LANES)
                do3 = dop[hd].reshape(nblk, BAND, LANES)
                z = _dot(q3, keys_k, BATCH_NT)
                for b_ in bias:
                    z = z + b_
                p = jnp.exp2(z - lse_p[...].reshape(nblk, BAND, LANES)[:, :, lane0:lane0 + 1])
                dp = _dot(do3, keys_v, BATCH_NT)
                ds = (p * (dp - delta_p[...].reshape(nblk, BAND, LANES)[:, :, lane0:lane0 + 1])).astype(BF16)
                dq_b = dq_b + _dot(ds, keys_k, BATCH_NN) * masks[hd]
                dk_b = dk_b + _dot(ds, q3, BATCH_TN)
                dv_b = dv_b + _dot(p.astype(BF16), do3, BATCH_TN)
            dq_p[...] = dq_b.reshape(seq, LANES)
            for acc, out in ((dk_b, dk_p), (dv_b, dv_p)):
                out[...] = acc[:, nk - BAND:, :].reshape(seq, LANES)
                if nk > BAND:
                    out[0:seq - BAND, :] += acc[1:, :BAND, :].reshape(seq - BAND, LANES)
            if d == 1:
                sums[0][0], sums[0][1], sums[0][2] = dq_p[...], dk_p[...], dv_p[...]
                continue
            for before, after in _regroup(d, seq):
                for i, part in enumerate((dq_p, dk_p, dv_p)):
                    sums[1][i, after, :] = sums[0].at[i][before, :] + part[after, :]
            sums = sums[::-1]

        for d in DILATIONS[:0:-1]:
            for i in range(3):
                for before, after in _regroup(d, seq):
                    sums[1].at[i][before, :] = sums[0][i, after, :]
            sums = sums[::-1]

        cos, up, dn = cos_ref[...], up_ref[...], dn_ref[...]
        dq_raw, dgq = _head_norm_bwd(_rope_bwd(sums[0][0] * SCALE, cos, up, dn), qhat, rq, gq_ref[...], ones)
        dk_raw, dgk = _head_norm_bwd(_rope_bwd(sums[0][1] * LN2, cos, up, dn), khat, rk, gk_ref[...], ones)
        dq_ref[...] = dq_raw.astype(BF16)
        dk_ref[...] = dk_raw.astype(BF16)
        dv_ref[...] = sums[0][2].astype(BF16)
        dg_ref[0:1, :] += dgq
        dg_ref[1:2, :] += dgk

    blk = lambda o_: pl.BlockSpec((seq, LANES), lambda b, p: (b, o_ + p))
    vec = pl.BlockSpec((1, LANES), lambda b, p: (0, 0))
    tab = pl.BlockSpec(memory_space=pltpu.VMEM)
    f32_buf = pltpu.VMEM((seq, LANES), F32)
    bf16_buf = pltpu.VMEM((seq, LANES), BF16)
    window_buf = pltpu.VMEM((seq // BAND, 2 * BAND, LANES), BF16)
    bf16_pair = pltpu.VMEM((2, seq, LANES), BF16)
    return _host_call(
        body, rider, name="dil_bwd", grid=(nb, npair),
        in_specs=[blk(off), blk(off + npair), blk(off + 2 * npair), vec, vec, tab, tab, tab,
                  blk(0), blk(0), blk(0)],
        out_specs=[blk(0), blk(0), blk(0), pl.BlockSpec((8, LANES), lambda b, p: (0, 0))],
        out_shape=[jax.ShapeDtypeStruct((t, W_GROUP), BF16), jax.ShapeDtypeStruct((t, W_GROUP), BF16),
                   jax.ShapeDtypeStruct((t, W_GROUP), BF16), jax.ShapeDtypeStruct((8, LANES), F32)],
        scratch_shapes=[pltpu.VMEM((6, seq, LANES), F32)] * 2 + [pltpu.VMEM((3, seq, LANES), F32)] * 2
        + [bf16_pair, bf16_buf, bf16_buf, bf16_pair, window_buf, window_buf] + [f32_buf] * 5,
        inputs=(proj, proj, proj, gq, gk, cos, up, dn, do, o, lse), semantics=("arbitrary", "arbitrary"))


def _adamw(w, g, m, v, name, rider=None):
    row_major = w.ndim == 3 and w.shape[1] == 1
    rows, cols = (w.shape[0], w.shape[2]) if row_major else w.shape[-2:]
    if row_major:
        tr = max(t for t in range(1, 65) if rows % t == 0)
    else:
        tr = _row_tile(rows) if rows >= 8 else rows
    c1 = 1.0 - ADAM_B1 ** ADAM_STEP
    c2 = 1.0 - ADAM_B2 ** ADAM_STEP

    def body(w_ref, g_ref, m_ref, v_ref, d_ref, nm_ref, nv_ref):
        g_ = g_ref[...]
        nm = ADAM_B1 * m_ref[...] + (1.0 - ADAM_B1) * g_
        nv = ADAM_B2 * v_ref[...] + (1.0 - ADAM_B2) * (g_ * g_)
        nm_ref[...] = nm
        nv_ref[...] = nv
        d_ref[...] = -ADAM_LR * ((nm / c1) / (jnp.sqrt(nv / c2) + ADAM_EPS) + ADAM_WD * w_ref[...])

    if row_major:
        spec = pl.BlockSpec((tr, 1, cols), lambda i: (i, 0, 0))
    elif w.ndim == 3:
        spec = pl.BlockSpec((1, tr, cols), lambda i: (0, i, 0))
    else:
        spec = pl.BlockSpec((tr, cols), lambda i: (i, 0))
    shape = jax.ShapeDtypeStruct(w.shape, F32)
    return _host_call(
        body, rider, name=name, grid=(rows // tr,), in_specs=[spec] * 4, out_specs=[spec] * 3,
        out_shape=[shape] * 3, scratch_shapes=[], inputs=(w, g, m, v), semantics=("arbitrary",))


def _place():
    x, y, c = lax.axis_index("x"), lax.axis_index("y"), lax.axis_index("c")
    chips = [(1 - x, y), (x, 1 - y), (1 - x, 1 - y)]
    return x, y, c, chips


def _gather_weight(w, name):
    _, rows, cols = w.shape
    half_rows = rows // 2

    def body(w_ref, out_ref, send_sems, recv_sems):
        x, y, c, chips = _place()
        sibling = (x, y, 1 - c)
        mine = 2 * x + y
        lo = pl.multiple_of(c * half_rows, 16)
        lo_sib = pl.multiple_of((1 - c) * half_rows, 16)
        out_ref[mine] = w_ref[0].astype(BF16)

        def copy(k, shard, first_row, to):
            ref = out_ref.at[shard, pl.ds(first_row, half_rows), :]
            return pltpu.make_async_remote_copy(src_ref=ref, dst_ref=ref, send_sem=send_sems.at[k],
                                                recv_sem=recv_sems.at[k], device_id=to, device_id_type=MESH)

        sends = [copy(k, mine, lo, (cx, cy, c)) for k, (cx, cy) in enumerate(chips)]
        for cp in sends:
            cp.start()
        passed = []
        for k, (cx, cy) in enumerate(chips):
            theirs = 2 * cx + cy
            copy(k, theirs, lo, (cx, cy, c)).wait_recv()
            fw = copy(3 + k, theirs, lo, sibling)
            fw.start()
            passed.append(fw)
        for k, (cx, cy) in enumerate(chips):
            copy(3 + k, 2 * cx + cy, lo_sib, sibling).wait_recv()
        for cp in sends + passed:
            cp.wait_send()

    return pl.pallas_call(
        body, name=name,
        in_specs=[pl.BlockSpec(memory_space=pltpu.VMEM)],
        out_specs=pl.BlockSpec(memory_space=pltpu.VMEM),
        out_shape=jax.ShapeDtypeStruct((4, rows, cols), BF16),
        scratch_shapes=[pltpu.SemaphoreType.DMA((6,)), pltpu.SemaphoreType.DMA((6,))],
        compiler_params=pltpu.CompilerParams(vmem_limit_bytes=VMEM_LIMIT),
    )(w)


def _remote(src, dst, sems, k, to):
    send_sems, recv_sems = sems
    return pltpu.make_async_remote_copy(src_ref=src, dst_ref=dst, send_sem=send_sems.at[k], recv_sem=recv_sems.at[k],
                                        device_id=to, device_id_type=MESH)


def _cast_bf16(parts, name):
    def body(*refs):
        for src, dst in zip(refs[:len(parts)], refs[len(parts):]):
            dst[...] = src[0].astype(BF16)

    return pl.pallas_call(
        body, name=name, in_specs=[pl.BlockSpec(memory_space=pltpu.VMEM)] * len(parts),
        out_specs=[pl.BlockSpec(memory_space=pltpu.VMEM)] * len(parts),
        out_shape=[jax.ShapeDtypeStruct(p.shape[1:], BF16) for p in parts],
        compiler_params=pltpu.CompilerParams(vmem_limit_bytes=VMEM_LIMIT),
    )(*parts)


def _gather_rider(shards):
    def copies(ins, outs, sems, which):
        x, y, c, chips = _place()
        sibling = (x, y, 1 - c)
        mine = 2 * x + y
        made = {name: [] for name in which}
        for i, (p_ref, g_ref) in enumerate(zip(ins, outs)):
            half = p_ref.shape[0] // 2
            lo = pl.multiple_of(c * half, 16)
            lo_sib = pl.multiple_of((1 - c) * half, 16)
            spot = lambda shard, first, g_ref=g_ref, half=half: g_ref.at[shard, pl.ds(first, half), :]
            groups = {
                "own": lambda: [pltpu.make_async_copy(p_ref, g_ref.at[mine], sems[0].at[7 * i + 6])],
                "sends": lambda: [_remote(p_ref.at[pl.ds(lo, half), :], spot(mine, lo), sems, 7 * i + k, (cx, cy, c))
                                  for k, (cx, cy) in enumerate(chips)],
                "arrivals": lambda: [_remote(spot(2 * cx + cy, lo), spot(2 * cx + cy, lo), sems, 7 * i + k, (cx, cy, c))
                                     for k, (cx, cy) in enumerate(chips)],
                "passes": lambda: [_remote(spot(2 * cx + cy, lo), spot(2 * cx + cy, lo), sems, 7 * i + 3 + k, sibling)
                                   for k, (cx, cy) in enumerate(chips)],
                "from_sibling": lambda: [_remote(spot(2 * cx + cy, lo_sib), spot(2 * cx + cy, lo_sib), sems,
                                                 7 * i + 3 + k, sibling) for k, (cx, cy) in enumerate(chips)],
            }
            for name in which:
                made[name] += groups[name]()
        return [made[name] for name in which]

    def start(ins, outs, send_sems, recv_sems):
        own, sends = copies(ins, outs, (send_sems, recv_sems), ("own", "sends"))
        for cp in own + sends:
            cp.start()

    def middle(ins, outs, send_sems, recv_sems):
        arrivals, passes = copies(ins, outs, (send_sems, recv_sems), ("arrivals", "passes"))
        for landed, onward in zip(arrivals, passes):
            landed.wait_recv()
            onward.start()

    def finish(ins, outs, send_sems, recv_sems):
        own, sends, passes, from_sibling = copies(ins, outs, (send_sems, recv_sems),
                                                  ("own", "sends", "passes", "from_sibling"))
        for cp in from_sibling:
            cp.wait_recv()
        for cp in sends + passes:
            cp.wait_send()
        for cp in own:
            cp.wait()

    shapes = [jax.ShapeDtypeStruct((4,) + s.shape, BF16) for s in shards]
    return _Rider(shards, shapes, 7 * len(shards), start, finish, middle=middle)


def _exchange_rider(inputs, out_shapes, n_sems, copies, aliases=None):
    def start(ins, outs, send_sems, recv_sems):
        for cp in copies(ins, outs, (send_sems, recv_sems)):
            cp.start()

    def finish(ins, outs, send_sems, recv_sems):
        for cp in copies(ins, outs, (send_sems, recv_sems)):
            cp.wait()

    return _Rider(inputs, out_shapes, n_sems, start, finish, aliases)


def _swap_rider(grads4):
    halves = [g.shape[1] // 2 for g in grads4]

    def copies(ins, outs, sems):
        x, y, c, _ = _place()
        return [_remote(g.at[:, pl.ds(pl.multiple_of((1 - c) * h, 8), h), :], a, sems, i, (x, y, 1 - c))
                for i, (g, a, h) in enumerate(zip(ins, outs, halves))]

    shapes = [jax.ShapeDtypeStruct((4, h, g.shape[2]), F32) for g, h in zip(grads4, halves)]
    return _exchange_rider(grads4, shapes, len(grads4), copies)


def _chip_sum(g4, from_sibling, name):
    _, rows, cols = g4.shape
    half = rows // 2

    def body(g_ref, s_ref, stage_ref, own_ref):
        x, y, c, chips = _place()
        lo = pl.multiple_of(c * half, 8)
        for k, (cx, cy) in enumerate(chips):
            theirs = 2 * cx + cy
            stage_ref[k] = (g_ref[theirs, pl.ds(lo, half), :] + s_ref[theirs]).astype(BF16)
        mine = 2 * x + y
        own_ref[...] = g_ref[mine, pl.ds(lo, half), :] + s_ref[mine]

    return pl.pallas_call(
        body, name=name, in_specs=[pl.BlockSpec(memory_space=pltpu.VMEM)] * 2,
        out_specs=[pl.BlockSpec(memory_space=pltpu.VMEM)] * 2,
        out_shape=[jax.ShapeDtypeStruct((3, half, cols), BF16), jax.ShapeDtypeStruct((half, cols), F32)],
        compiler_params=pltpu.CompilerParams(vmem_limit_bytes=VMEM_LIMIT),
    )(g4, from_sibling)


def _spread_rider(stages):
    def copies(ins, outs, sems):
        _, _, c, chips = _place()
        return [_remote(st.at[k], ld.at[k], sems, 3 * i + k, (cx, cy, c))
                for i, (st, ld) in enumerate(zip(ins, outs)) for k, (cx, cy) in enumerate(chips)]

    shapes = [jax.ShapeDtypeStruct(s.shape, s.dtype) for s in stages]
    return _exchange_rider(stages, shapes, 3 * len(stages), copies)


def _finish_half(own, landed, name):
    half, cols = own.shape

    def body(own_ref, landed_ref, out_ref):
        c = lax.axis_index("c")
        acc = own_ref[...]
        for k in range(3):
            acc = acc + landed_ref[k].astype(F32)
        out_ref[pl.ds(pl.multiple_of(c * half, 8), half), :] = acc

    return pl.pallas_call(
        body, name=name, in_specs=[pl.BlockSpec(memory_space=pltpu.VMEM)] * 2,
        out_specs=pl.BlockSpec(memory_space=pltpu.VMEM),
        out_shape=jax.ShapeDtypeStruct((2 * half, cols), F32),
        compiler_params=pltpu.CompilerParams(vmem_limit_bytes=VMEM_LIMIT),
    )(own, landed)


def _share_rider(fulls):
    def copies(ins, outs, sems):
        x, y, c, _ = _place()
        out = []
        for i, full in enumerate(outs):
            half = full.shape[0] // 2
            rows = full.at[pl.ds(pl.multiple_of(c * half, 8), half), :]
            out.append(_remote(rows, rows, sems, i, (x, y, 1 - c)))
        return out

    def finish_copies(ins, outs, sems):
        x, y, c, _ = _place()
        out = []
        for i, full in enumerate(outs):
            half = full.shape[0] // 2
            mine = full.at[pl.ds(pl.multiple_of(c * half, 8), half), :]
            theirs = full.at[pl.ds(pl.multiple_of((1 - c) * half, 8), half), :]
            out.append((_remote(mine, mine, sems, i, (x, y, 1 - c)), _remote(theirs, theirs, sems, i, (x, y, 1 - c))))
        return out

    def start(ins, outs, send_sems, recv_sems):
        for cp in copies(ins, outs, (send_sems, recv_sems)):
            cp.start()

    def finish(ins, outs, send_sems, recv_sems):
        for sent, landed in finish_copies(ins, outs, (send_sems, recv_sems)):
            sent.wait_send()
            landed.wait_recv()

    shapes = [jax.ShapeDtypeStruct(f.shape, f.dtype) for f in fulls]
    return _Rider(fulls, shapes, len(fulls), start, finish, aliases={i: i for i in range(len(fulls))})


def _all_sum_small(v):
    shape = v.shape

    def body(v_ref, out_ref, buf, send_sems, recv_sems):
        x, y, c, _ = _place()
        me = 4 * x + 2 * y + c
        buf[me] = v_ref[...]
        flips = [(dx, dy, dc) for dx in (0, 1) for dy in (0, 1) for dc in (0, 1)][1:]

        def copy(k, slot, flip):
            dx, dy, dc = flip
            to = (1 - x if dx else x, 1 - y if dy else y, 1 - c if dc else c)
            return pltpu.make_async_remote_copy(src_ref=buf.at[slot], dst_ref=buf.at[slot], send_sem=send_sems.at[k],
                                                recv_sem=recv_sems.at[k], device_id=to, device_id_type=MESH)

        sends = [copy(k, me, flip) for k, flip in enumerate(flips)]
        for cp in sends:
            cp.start()
        for k, (dx, dy, dc) in enumerate(flips):
            sender = 4 * (1 - x if dx else x) + 2 * (1 - y if dy else y) + (1 - c if dc else c)
            copy(k, sender, (dx, dy, dc)).wait_recv()
        for cp in sends:
            cp.wait_send()
        total = buf[0]
        for i in range(1, 8):
            total = total + buf[i]
        out_ref[...] = total

    return pl.pallas_call(
        body, name="all_sum_small",
        in_specs=[pl.BlockSpec(memory_space=pltpu.VMEM)],
        out_specs=pl.BlockSpec(memory_space=pltpu.VMEM),
        out_shape=jax.ShapeDtypeStruct(shape, F32),
        scratch_shapes=[pltpu.VMEM((8,) + shape, F32), pltpu.SemaphoreType.DMA((7,)), pltpu.SemaphoreType.DMA((7,))],
    )(v)


SMALL = (("g_mix", 1024), ("g_ffn", 1024), ("g_out_fox", 512), ("g_out_dil", 512), ("g_q_fox", 64),
         ("g_k_fox", 64), ("g_q_dil", 64), ("g_k_dil", 64), ("b_forget", 8))
SMALL_PACKED = (32, LANES)


def _local_grads(x, target, gains, w1, wft, dense, packed, nb, seq):
    tile2 = lambda g: jnp.tile(g, (1, 2))
    gq_f, gk_f, gq_d, gk_d = (tile2(gains[n]) for n in ("g_q_fox", "g_k_fox", "g_q_dil", "g_k_dil"))
    b_col = gains["b_forget"].reshape(N_FOX_HEADS, 1)
    cos, up, dn = _rope_tables(seq)
    npair = N_FOX_HEADS // 2

    proj, fa_row, h1, h1_t = _in_proj(x, gains["g_mix"], w1, wft)
    c_row = _gate_fwd(fa_row, b_col, seq)
    c3 = c_row.reshape(npair, 2, nb * seq)
    (o_fox, lse_fox), gathered = _fox_fwd(proj, c3, gq_f, gk_f, nb, seq,
                                          rider=None if packed is None else _gather_rider(packed))
    if packed is not None:
        dense = [g.reshape(-1, g.shape[2]) for g in gathered]
    w_out, w_gate, w_up, w_down = dense
    o_dil, lse_dil = _dil_fwd(proj, gq_d, gk_d, cos, up, dn, nb, seq)
    x1, o_n_t = _attn_out(o_fox, o_dil, x, gains["g_out_fox"], gains["g_out_dil"], w_out)
    a, u, dy, loss_parts = _ffn_fwd(x1, target, gains["g_ffn"], w_gate, w_up, w_down)
    loss = jnp.sum(loss_parts[:, 0, 0])

    dx1, s, da, du, h2, dg_ffn = _ffn_bwd(dy, a, u, x1, gains["g_ffn"], w_gate, w_up, w_down)
    d_w_down = _token_matmul(s, dy, "dw_down", 512, False)
    d_w_gate = _token_matmul(da, h2, "dw_gate", 512, False)
    d_w_up = _token_matmul(du, h2, "dw_up", 512, False)
    d_w_out = _token_matmul(o_n_t, dx1, "dw_out", 1024)
    names = ("w_out", "w_gate", "w_up", "w_down")
    grads4 = [g.reshape(4, -1, g.shape[1]) for g in (d_w_out, d_w_gate, d_w_up, d_w_down)]
    exchange = packed is not None
    (do_fox, do_dil, dg_of, dg_od), from_sibling = _attn_out_bwd(
        dx1, o_fox, o_dil, gains["g_out_fox"], gains["g_out_dil"], w_out,
        rider=_swap_rider(grads4) if exchange else None)
    if exchange:
        sums = [_chip_sum(g, s, "chip_sum_" + n) for g, s, n in zip(grads4, from_sibling, names)]
    (dq_f, dk_f, dv_f, dc3, dg_fox), landed = _fox_bwd(
        proj, c3, gq_f, gk_f, do_fox, o_fox, lse_fox, nb, seq,
        rider=_spread_rider([st for st, _ in sums]) if exchange else None)
    if exchange:
        halves = [_finish_half(own, ld, "finish_half_" + n) for (_, own), ld, n in zip(sums, landed, names)]
    (dq_d, dk_d, dv_d, dg_dil), reduced = _dil_bwd(
        proj, gq_d, gk_d, cos, up, dn, do_dil, o_dil, lse_dil, nb, seq,
        rider=_share_rider(halves) if exchange else None)
    if exchange:
        d_w_out, d_w_gate, d_w_up, d_w_down = reduced
    dfa_row, db = _gate_bwd(dc3.reshape(N_FOX_HEADS, nb * seq), fa_row, b_col, seq)
    dparts = [dq_f, dk_f, dv_f, dq_d, dk_d, dv_d]
    d_w1 = _token_matmul_parts(h1_t, dparts, "dw_in")
    d_wf = _row_matmul(dfa_row, h1, "dw_forget")
    fox_w = 3 * W_GROUP
    d_w_in = jnp.concatenate([d_w1[:, :fox_w], d_wf.T, d_w1[:, fox_w:]], axis=1)
    if exchange:
        shards = [_shards_of_columns(d_w_in)]
        _, from_sibling = _idle_host(_swap_rider(shards), "swap_w_in")
        stage, own = _chip_sum(shards[0], from_sibling[0], "chip_sum_w_in")
    (grad_x, dg_mix), landed = _in_proj_bwd(dparts, dfa_row, w1, wft, x, gains["g_mix"], dx1,
                                            rider=_spread_rider([stage]) if exchange else None)
    if exchange:
        d_w_in = _finish_half(own, landed[0], "finish_half_w_in")

    fold = lambda g2: (g2[:, :HEAD_DIM] + g2[:, HEAD_DIM:])
    small = {
        "g_mix": dg_mix[0:1], "g_ffn": dg_ffn[0:1], "g_out_fox": dg_of[0:1], "g_out_dil": dg_od[0:1],
        "g_q_fox": fold(dg_fox[0:1]), "g_k_fox": fold(dg_fox[1:2]),
        "g_q_dil": fold(dg_dil[0:1]), "g_k_dil": fold(dg_dil[1:2]),
        "b_forget": db[:, 0].reshape(1, N_FOX_HEADS),
    }
    big = {"w_in": d_w_in, "w_out": d_w_out, "w_gate": d_w_gate, "w_up": d_w_up, "w_down": d_w_down}
    return loss, grad_x, big, small


def _shards_of_columns(full, n=4):
    r, nc = full.shape
    return full.reshape(r, n, nc // n).transpose(1, 0, 2)


def _columns_of_shards(slabs):
    n, r, c = slabs.shape
    return slabs.transpose(1, 0, 2).reshape(r, n * c)


def kernel(x, g_mix, w_in, b_forget, g_q_fox, g_k_fox, g_q_dil, g_k_dil, g_out_fox, g_out_dil, w_out, g_ffn, w_gate, w_up, w_down, loss_target, m_g_mix, m_w_in, m_b_forget, m_g_q_fox, m_g_k_fox, m_g_q_dil, m_g_k_dil, m_g_out_fox, m_g_out_dil, m_w_out, m_g_ffn, m_w_gate, m_w_up, m_w_down, v_g_mix, v_w_in, v_b_forget, v_g_q_fox, v_g_k_fox, v_g_q_dil, v_g_k_dil, v_g_out_fox, v_g_out_dil, v_w_out, v_g_ffn, v_w_gate, v_w_up, v_w_down):
    nb, seq, d = x.shape
    weights = dict(g_mix=g_mix, w_in=w_in, b_forget=b_forget, g_q_fox=g_q_fox, g_k_fox=g_k_fox, g_q_dil=g_q_dil,
                   g_k_dil=g_k_dil, g_out_fox=g_out_fox, g_out_dil=g_out_dil, w_out=w_out, g_ffn=g_ffn,
                   w_gate=w_gate, w_up=w_up, w_down=w_down)
    m_in = dict(g_mix=m_g_mix, w_in=m_w_in, b_forget=m_b_forget, g_q_fox=m_g_q_fox, g_k_fox=m_g_k_fox,
                g_q_dil=m_g_q_dil, g_k_dil=m_g_k_dil, g_out_fox=m_g_out_fox, g_out_dil=m_g_out_dil, w_out=m_w_out,
                g_ffn=m_g_ffn, w_gate=m_w_gate, w_up=m_w_up, w_down=m_w_down)
    v_in = dict(g_mix=v_g_mix, w_in=v_w_in, b_forget=v_b_forget, g_q_fox=v_g_q_fox, g_k_fox=v_g_k_fox,
                g_q_dil=v_g_q_dil, g_k_dil=v_g_k_dil, g_out_fox=v_g_out_fox, g_out_dil=v_g_out_dil, w_out=v_w_out,
                g_ffn=v_g_ffn, w_gate=v_w_gate, w_up=v_w_up, w_down=v_w_down)
    order = ["g_mix", "w_in", "b_forget", "g_q_fox", "g_k_fox", "g_q_dil", "g_k_dil", "g_out_fox", "g_out_dil",
             "w_out", "g_ffn", "w_gate", "w_up", "w_down"]

    w_in_full = _columns_of_shards(_gather_weight(w_in, "gather_w_in"))
    fox_w = 3 * W_GROUP
    w1 = jnp.concatenate([w_in_full[:, :fox_w], w_in_full[:, fox_w + N_FOX_HEADS:]], axis=1)
    wft = w_in_full[:, fox_w:fox_w + N_FOX_HEADS].T
    swap = lambda a: jnp.transpose(a, (0, 2, 1))
    for n in ("w_gate", "w_up"):
        weights[n], m_in[n], v_in[n] = swap(weights[n]), swap(m_in[n]), swap(v_in[n])
    shards = _cast_bf16([weights[n] for n in ("w_out", "w_gate", "w_up", "w_down")], "cast_shards")

    gains = {n: weights[n] for n, _ in SMALL}
    loss, grad_x, big, small = _local_grads(
        x.reshape(nb * seq, d), loss_target.reshape(nb * seq, d), gains, w1, wft, None, shards, nb, seq)

    grads = {n: big[n][None] for n in ("w_out", "w_gate", "w_up", "w_down")}
    packed = jnp.concatenate([small[n].reshape(-1) for n, _ in SMALL] + [loss.reshape(1)])
    packed = jnp.pad(packed, (0, SMALL_PACKED[0] * SMALL_PACKED[1] - packed.shape[0])).reshape(SMALL_PACKED)
    summed = _all_sum_small(packed).reshape(-1)
    pos = 0
    for n, size in SMALL:
        grads[n] = summed[pos:pos + size].reshape(1, size)
        pos += size
    loss = summed[pos]

    to_entry = lambda a: jnp.transpose(a, (2, 0, 1))
    deltas, new_m, new_v, grad_out = {}, {}, {}, {}
    for n in ["w_down"] + [n for n in order if n != "w_down"]:
        rider = _share_rider([big["w_in"]]) if n == "w_down" else None
        (deltas[n], new_m[n], new_v[n]), shared = _adamw(weights[n], grads[n], m_in[n], v_in[n], "adamw_" + n, rider)
        if rider is not None:
            grads["w_in"] = to_entry(shared[0][None])
            weights["w_in"], m_in["w_in"], v_in["w_in"] = (to_entry(a) for a in (w_in, m_w_in, v_w_in))
        grad_out[n] = grads[n]
    for n in ("w_gate", "w_up"):
        grad_out[n], deltas[n], new_m[n], new_v[n] = (swap(a) for a in (grad_out[n], deltas[n], new_m[n], new_v[n]))
    from_entry = lambda a: jnp.transpose(a, (1, 2, 0))
    grad_out["w_in"], deltas["w_in"], new_m["w_in"], new_v["w_in"] = (
        from_entry(a) for a in (grad_out["w_in"], deltas["w_in"], new_m["w_in"], new_v["w_in"]))

    return (loss, grad_x.reshape(nb, seq, d), *[grad_out[n] for n in order], *[deltas[n] for n in order],
            *[new_m[n] for n in order], *[new_v[n] for n in order])
```

```python
import functools
import math

import jax
import jax.numpy as jnp
from jax import lax
from jax.experimental import pallas as pl
from jax.experimental.pallas import tpu as pltpu

F32, BF16 = jnp.float32, jnp.bfloat16
MESH = pl.DeviceIdType.MESH

EPS = 1e-6
NEG = -1e30
HEAD_DIM = 64
SCALE = HEAD_DIM ** -0.5
LOG2E = math.log2(math.e)
LN2 = math.log(2.0)
ROPE_THETA = 500000.0
ROPE_DIM = HEAD_DIM // 4
LANES = 128
W_GROUP = 512
N_FOX_HEADS = 8
VMEM_LIMIT = 56 * 1024 * 1024
DILATIONS = (1, 4, 16)
BAND = 128

ADAM_LR, ADAM_B1, ADAM_B2, ADAM_EPS, ADAM_WD, ADAM_STEP = 0.001, 0.9, 0.999, 1e-08, 0.01, 10

NT = (((1,), (1,)), ((), ()))
TN = (((0,), (0,)), ((), ()))
BATCH_NT = (((2,), (2,)), ((0,), (0,)))
BATCH_NN = (((2,), (1,)), ((0,), (0,)))
BATCH_TN = (((1,), (1,)), ((0,), (0,)))


def _params(sem=None):
    return pltpu.CompilerParams(dimension_semantics=sem, vmem_limit_bytes=VMEM_LIMIT)


def _dot(a, b, dims=None):
    if dims is None:
        return jnp.dot(a, b, preferred_element_type=F32)
    return lax.dot_general(a, b, dims, preferred_element_type=F32)


def _group_ones():
    i = lax.broadcasted_iota(jnp.int32, (LANES, LANES), 0) >> 6
    j = lax.broadcasted_iota(jnp.int32, (LANES, LANES), 1) >> 6
    return (i == j).astype(BF16)


def _split3(x):
    a = x.astype(BF16)
    r = x - a.astype(F32)
    b = r.astype(BF16)
    c = (r - b.astype(F32)).astype(BF16)
    return a, b, c


def _groupsum(x, ones, pieces=2):
    total = None
    for _ in range(pieces):
        piece = x.astype(BF16)
        part = _dot(piece, ones)
        total = part if total is None else total + part
        x = x - piece.astype(F32)
    return total


def _head_masks():
    lane = lax.broadcasted_iota(jnp.int32, (1, LANES), 1)
    return [(lane < HEAD_DIM).astype(F32), (lane >= HEAD_DIM).astype(F32)]


def _head_norm(raw, ones):
    r = lax.rsqrt(_groupsum(raw * raw, ones, 1) * (1.0 / HEAD_DIM) + EPS)
    return raw * r, r


def _head_norm_bwd(dy, xhat, r, gain, ones):
    u = dy * gain
    dgain = jnp.sum(dy * xhat, axis=0, keepdims=True)
    draw = r * (u - xhat * (_groupsum(u * xhat, ones) * (1.0 / HEAD_DIM)))
    return draw, dgain


def _rope(x, cos, s_up, s_dn):
    return x * cos + pltpu.roll(x, LANES - 8, 1) * s_up + pltpu.roll(x, 8, 1) * s_dn


def _rope_bwd(dy, cos, s_up, s_dn):
    return dy * cos + pltpu.roll(dy * s_up, 8, 1) + pltpu.roll(dy * s_dn, LANES - 8, 1)


def _rope_tables(seq):
    half = ROPE_DIM // 2
    inv_freq = jnp.power(jnp.float32(ROPE_THETA), -jnp.arange(half, dtype=F32) * 2.0 / ROPE_DIM)
    ang = jnp.arange(seq).astype(F32)[:, None] * inv_freq[None, :]
    cos, sin = jnp.cos(ang), jnp.sin(ang)
    one = jnp.ones((seq, HEAD_DIM - ROPE_DIM), F32)
    zero_h = jnp.zeros((seq, half), F32)
    zero_r = jnp.zeros((seq, HEAD_DIM - ROPE_DIM), F32)
    c = jnp.concatenate([cos, cos, one], axis=1)
    up = jnp.concatenate([-sin, zero_h, zero_r], axis=1)
    dn = jnp.concatenate([zero_h, sin, zero_r], axis=1)
    return jnp.tile(c, (1, 2)), jnp.tile(up, (1, 2)), jnp.tile(dn, (1, 2))


def _row_tile(rows, cap=256):
    best = rows
    for t in range(8, min(rows, cap) + 1, 8):
        if rows % t == 0:
            best = t
    return best


class _Rider:
    def __init__(self, inputs, out_shapes, n_sems, start, finish, aliases=None, middle=None):
        self.inputs, self.out_shapes, self.n_sems = list(inputs), list(out_shapes), n_sems
        self.start, self.finish, self.middle, self.aliases = start, finish, middle, dict(aliases or {})


def _host_call(body, rider, *, name, grid, in_specs, out_specs, out_shape, scratch_shapes, inputs, semantics):
    if rider is None:
        return pl.pallas_call(body, name=name, grid=grid, in_specs=in_specs, out_specs=out_specs,
                              out_shape=out_shape, scratch_shapes=scratch_shapes,
                              compiler_params=_params(semantics))(*inputs), []
    n_in, n_out, n_scr = len(in_specs), len(out_specs), len(scratch_shapes)
    r_in, r_out = len(rider.inputs), len(rider.out_shapes)

    def wrapped(*refs):
        ins, refs = refs[:n_in], refs[n_in:]
        r_ins, refs = refs[:r_in], refs[r_in:]
        outs, refs = refs[:n_out], refs[n_out:]
        r_outs, refs = refs[:r_out], refs[r_out:]
        scratch, (send_sems, recv_sems) = refs[:n_scr], refs[n_scr:]
        ids = [pl.program_id(a) for a in range(len(grid))]
        first = functools.reduce(lambda p, q: p & q, [i == 0 for i in ids])
        last = functools.reduce(lambda p, q: p & q, [i == g - 1 for i, g in zip(ids, grid)])

        @pl.when(first)
        def _():
            rider.start(r_ins, r_outs, send_sems, recv_sems)

        body(*ins, *outs, *scratch)

        if rider.middle is not None:
            step, steps = ids[0], grid[0]
            for i, g in zip(ids[1:], grid[1:]):
                step, steps = step * g + i, steps * g

            @pl.when(step == (3 * steps) // 4)
            def _():
                rider.middle(r_ins, r_outs, send_sems, recv_sems)

        @pl.when(last)
        def _():
            rider.finish(r_ins, r_outs, send_sems, recv_sems)

    hbm = pl.BlockSpec(memory_space=pl.ANY)
    res = pl.pallas_call(
        wrapped, name=name, grid=grid,
        in_specs=list(in_specs) + [hbm] * r_in, out_specs=list(out_specs) + [hbm] * r_out,
        out_shape=list(out_shape) + rider.out_shapes,
        scratch_shapes=list(scratch_shapes) + [pltpu.SemaphoreType.DMA((rider.n_sems,))] * 2,
        input_output_aliases={n_in + i: n_out + o for i, o in rider.aliases.items()},
        compiler_params=_params(semantics),
    )(*inputs, *rider.inputs)
    return res[:n_out], res[n_out:]


def _idle_host(rider, name):
    def body(o_ref):
        o_ref[...] = jnp.zeros_like(o_ref)

    return _host_call(body, rider, name=name, grid=(1,), in_specs=[],
                      out_specs=[pl.BlockSpec((8, LANES), lambda i: (0, 0))],
                      out_shape=[jax.ShapeDtypeStruct((8, LANES), F32)], scratch_shapes=[], inputs=(),
                      semantics=("arbitrary",))


def _in_proj(x, g_mix, w1, wft):
    t, d = x.shape
    n = w1.shape[1]
    tt = 512

    def body(x_ref, g_ref, w_ref, wf_ref, p_ref, fa_ref, h_ref, ht_ref):
        xx = x_ref[...]
        r = lax.rsqrt(jnp.mean(xx * xx, axis=-1, keepdims=True) + EPS)
        h = (xx * r * g_ref[...]).astype(BF16)
        h_ref[...] = h
        ht_ref[...] = h.T
        for j in range(n // W_GROUP):
            cols = slice(j * W_GROUP, (j + 1) * W_GROUP)
            p_ref[:, cols] = _dot(h, w_ref[:, cols]).astype(BF16)
        fa_ref[...] = _dot(wf_ref[...], h, NT)

    return pl.pallas_call(
        body, name="in_proj", grid=(t // tt,),
        in_specs=[pl.BlockSpec((tt, d), lambda i: (i, 0)), pl.BlockSpec((1, d), lambda i: (0, 0)),
                  pl.BlockSpec(memory_space=pltpu.VMEM), pl.BlockSpec(memory_space=pltpu.VMEM)],
        out_specs=[pl.BlockSpec((tt, n), lambda i: (i, 0)), pl.BlockSpec((8, tt), lambda i: (0, i)),
                   pl.BlockSpec((tt, d), lambda i: (i, 0)), pl.BlockSpec((d, tt), lambda i: (0, i))],
        out_shape=[jax.ShapeDtypeStruct((t, n), BF16), jax.ShapeDtypeStruct((8, t), F32),
                   jax.ShapeDtypeStruct((t, d), BF16), jax.ShapeDtypeStruct((d, t), BF16)],
        compiler_params=_params(("arbitrary",)),
    )(x, g_mix, w1, wft)


def _tri(n, upper):
    i = lax.broadcasted_iota(jnp.int32, (n, n), 0)
    j = lax.broadcasted_iota(jnp.int32, (n, n), 1)
    return ((i <= j) if upper else (i >= j)).astype(BF16)


def _gate_fwd(fa_row, b_col, seq):
    t = fa_row.shape[1]
    cb = 256

    def body(fa_ref, b_ref, c_ref):
        tri = _tri(cb, True)
        carry = jnp.zeros((8, 1), F32)
        for k in range(seq // cb):
            z = fa_ref[:, k * cb:(k + 1) * cb] + b_ref[...]
            lf = jnp.minimum(z, 0.0) - jnp.log(1.0 + jnp.exp(-jnp.abs(z)))
            a, b, c = _split3(lf)
            blk = _dot(a, tri) + _dot(b, tri) + _dot(c, tri) + carry
            c_ref[:, k * cb:(k + 1) * cb] = blk
            carry = blk[:, cb - 1:cb]

    return pl.pallas_call(
        body, name="gate_fwd", grid=(t // seq,),
        in_specs=[pl.BlockSpec((8, seq), lambda i: (0, i)), pl.BlockSpec((8, 1), lambda i: (0, 0))],
        out_specs=pl.BlockSpec((8, seq), lambda i: (0, i)),
        out_shape=jax.ShapeDtypeStruct((8, t), F32),
        compiler_params=_params(("arbitrary",)),
    )(fa_row, b_col)


def _gate_bwd(dc_row, fa_row, b_col, seq):
    t = fa_row.shape[1]
    cb = 256

    def body(dc_ref, fa_ref, b_ref, dfa_ref, db_ref):
        @pl.when(pl.program_id(0) == 0)
        def _():
            db_ref[...] = jnp.zeros_like(db_ref)

        tri = _tri(cb, False)
        carry = jnp.zeros((8, 1), F32)
        dbs = jnp.zeros((8, 1), F32)
        for k in reversed(range(seq // cb)):
            a, b, c = _split3(dc_ref[:, k * cb:(k + 1) * cb])
            dlf = _dot(a, tri) + _dot(b, tri) + _dot(c, tri) + carry
            carry = dlf[:, 0:1]
            z = fa_ref[:, k * cb:(k + 1) * cb] + b_ref[...]
            dfa = dlf / (1.0 + jnp.exp(z))
            dfa_ref[:, k * cb:(k + 1) * cb] = dfa
            dbs = dbs + jnp.sum(dfa, axis=1, keepdims=True)
        db_ref[...] += jnp.broadcast_to(dbs, (8, LANES))

    return pl.pallas_call(
        body, name="gate_bwd", grid=(t // seq,),
        in_specs=[pl.BlockSpec((8, seq), lambda i: (0, i)), pl.BlockSpec((8, seq), lambda i: (0, i)),
                  pl.BlockSpec((8, 1), lambda i: (0, 0))],
        out_specs=[pl.BlockSpec((8, seq), lambda i: (0, i)), pl.BlockSpec((8, LANES), lambda i: (0, 0))],
        out_shape=[jax.ShapeDtypeStruct((8, t), F32), jax.ShapeDtypeStruct((8, LANES), F32)],
        compiler_params=_params(("arbitrary",)),
    )(dc_row, fa_row, b_col)


def _attn_out(o_fox, o_dil, x, g_fox, g_dil, w_out):
    t, d = x.shape
    w = o_fox.shape[1]
    tt = 512

    def body(of_ref, od_ref, x_ref, gf_ref, gd_ref, w_ref, x1_ref, ont_ref):
        acc = x_ref[...]
        for k, (o_ref, g_ref) in enumerate(((of_ref, gf_ref), (od_ref, gd_ref))):
            o = o_ref[...]
            r = lax.rsqrt(jnp.mean(o * o, axis=-1, keepdims=True) + EPS)
            on = (o * r * g_ref[...]).astype(BF16)
            ont_ref[k * w:(k + 1) * w, :] = on.T
            acc = acc + _dot(on, w_ref[k * w:(k + 1) * w, :])
        x1_ref[...] = acc

    return pl.pallas_call(
        body, name="attn_out", grid=(t // tt,),
        in_specs=[pl.BlockSpec((tt, w), lambda i: (i, 0)), pl.BlockSpec((tt, w), lambda i: (i, 0)),
                  pl.BlockSpec((tt, d), lambda i: (i, 0)), pl.BlockSpec((1, w), lambda i: (0, 0)),
                  pl.BlockSpec((1, w), lambda i: (0, 0)), pl.BlockSpec(memory_space=pltpu.VMEM)],
        out_specs=[pl.BlockSpec((tt, d), lambda i: (i, 0)), pl.BlockSpec((2 * w, tt), lambda i: (0, i))],
        out_shape=[jax.ShapeDtypeStruct((t, d), F32), jax.ShapeDtypeStruct((2 * w, t), BF16)],
        compiler_params=_params(("arbitrary",)),
    )(o_fox, o_dil, x, g_fox, g_dil, w_out)


def _attn_out_bwd(dx1, o_fox, o_dil, g_fox, g_dil, w_out, rider=None):
    t, d = dx1.shape
    w = o_fox.shape[1]
    tt = 512

    def body(dx_ref, of_ref, od_ref, gf_ref, gd_ref, w_ref, dof_ref, dod_ref, dgf_ref, dgd_ref):
        @pl.when(pl.program_id(0) == 0)
        def _():
            dgf_ref[...] = jnp.zeros_like(dgf_ref)
            dgd_ref[...] = jnp.zeros_like(dgd_ref)

        dxb = dx_ref[...].astype(BF16)
        for k, (o_ref, g_ref, do_ref, dg_ref) in enumerate(
                ((of_ref, gf_ref, dof_ref, dgf_ref), (od_ref, gd_ref, dod_ref, dgd_ref))):
            don = _dot(dxb, w_ref[k * w:(k + 1) * w, :], NT)
            o = o_ref[...]
            r = lax.rsqrt(jnp.mean(o * o, axis=-1, keepdims=True) + EPS)
            xhat = o * r
            u = don * g_ref[...]
            do_ref[...] = r * (u - xhat * jnp.mean(u * xhat, axis=-1, keepdims=True))
            dg_ref[0:1, :] += jnp.sum(don * xhat, axis=0, keepdims=True)

    return _host_call(
        body, rider, name="attn_out_bwd", grid=(t // tt,),
        in_specs=[pl.BlockSpec((tt, d), lambda i: (i, 0)), pl.BlockSpec((tt, w), lambda i: (i, 0)),
                  pl.BlockSpec((tt, w), lambda i: (i, 0)), pl.BlockSpec((1, w), lambda i: (0, 0)),
                  pl.BlockSpec((1, w), lambda i: (0, 0)), pl.BlockSpec(memory_space=pltpu.VMEM)],
        out_specs=[pl.BlockSpec((tt, w), lambda i: (i, 0)), pl.BlockSpec((tt, w), lambda i: (i, 0)),
                   pl.BlockSpec((8, w), lambda i: (0, 0)), pl.BlockSpec((8, w), lambda i: (0, 0))],
        out_shape=[jax.ShapeDtypeStruct((t, w), F32), jax.ShapeDtypeStruct((t, w), F32),
                   jax.ShapeDtypeStruct((8, w), F32), jax.ShapeDtypeStruct((8, w), F32)],
        scratch_shapes=[], inputs=(dx1, o_fox, o_dil, g_fox, g_dil, w_out), semantics=("arbitrary",))


def _ffn_fwd(x1, target, g_ffn, w_gate, w_up, w_down):
    t, d = x1.shape
    f = w_gate.shape[0]
    tt = 256

    def body(x_ref, t_ref, g_ref, wg_ref, wu_ref, wd_ref, a_ref, u_ref, dy_ref, loss_ref):
        xx = x_ref[...]
        r = lax.rsqrt(jnp.mean(xx * xx, axis=-1, keepdims=True) + EPS)
        h = (xx * r * g_ref[...]).astype(BF16)
        a = _dot(h, wg_ref[...], NT)
        u = _dot(h, wu_ref[...], NT)
        a_ref[...] = a.astype(BF16)
        u_ref[...] = u.astype(BF16)
        s = (a / (1.0 + jnp.exp(-a)) * u).astype(BF16)
        y = xx + _dot(s, wd_ref[...])
        e = y - t_ref[...]
        dy_ref[...] = e * (1.0 / d)
        loss_ref[...] = jnp.broadcast_to(0.5 * jnp.sum(e * e) * (1.0 / d), (1, 8, LANES))

    return pl.pallas_call(
        body, name="ffn_fwd", grid=(t // tt,),
        in_specs=[pl.BlockSpec((tt, d), lambda i: (i, 0)), pl.BlockSpec((tt, d), lambda i: (i, 0)),
                  pl.BlockSpec((1, d), lambda i: (0, 0)), pl.BlockSpec(memory_space=pltpu.VMEM),
                  pl.BlockSpec(memory_space=pltpu.VMEM), pl.BlockSpec(memory_space=pltpu.VMEM)],
        out_specs=[pl.BlockSpec((tt, f), lambda i: (i, 0)), pl.BlockSpec((tt, f), lambda i: (i, 0)),
                   pl.BlockSpec((tt, d), lambda i: (i, 0)), pl.BlockSpec((1, 8, LANES), lambda i: (i, 0, 0))],
        out_shape=[jax.ShapeDtypeStruct((t, f), BF16), jax.ShapeDtypeStruct((t, f), BF16),
                   jax.ShapeDtypeStruct((t, d), F32), jax.ShapeDtypeStruct((t // tt, 8, LANES), F32)],
        compiler_params=_params(("arbitrary",)),
    )(x1, target, g_ffn, w_gate, w_up, w_down)


def _ffn_bwd(dy, a, u, x1, g_ffn, w_gate, w_up, w_down):
    t, d = x1.shape
    f = w_gate.shape[0]
    tt = 256

    def body(dy_ref, a_ref, u_ref, x_ref, g_ref, wg_ref, wu_ref, wd_ref,
             dx_ref, s_ref, da_ref, du_ref, h_ref, dg_ref):
        @pl.when(pl.program_id(0) == 0)
        def _():
            dg_ref[...] = jnp.zeros_like(dg_ref)

        dy_ = dy_ref[...]
        ds = _dot(dy_.astype(BF16), wd_ref[...], NT)
        a_ = a_ref[...].astype(F32)
        u_ = u_ref[...].astype(F32)
        sig = 1.0 / (1.0 + jnp.exp(-a_))
        silu = a_ * sig
        s_ref[...] = (silu * u_).astype(BF16)
        da = (ds * u_ * (sig * (1.0 + a_ * (1.0 - sig)))).astype(BF16)
        du = (ds * silu).astype(BF16)
        da_ref[...] = da
        du_ref[...] = du
        dh = _dot(da, wg_ref[...]) + _dot(du, wu_ref[...])
        xx = x_ref[...]
        r = lax.rsqrt(jnp.mean(xx * xx, axis=-1, keepdims=True) + EPS)
        xhat = xx * r
        g = g_ref[...]
        h_ref[...] = (xhat * g).astype(BF16)
        uu = dh * g
        dx_ref[...] = dy_ + r * (uu - xhat * jnp.mean(uu * xhat, axis=-1, keepdims=True))
        dg_ref[0:1, :] += jnp.sum(dh * xhat, axis=0, keepdims=True)

    return pl.pallas_call(
        body, name="ffn_bwd", grid=(t // tt,),
        in_specs=[pl.BlockSpec((tt, d), lambda i: (i, 0)), pl.BlockSpec((tt, f), lambda i: (i, 0)),
                  pl.BlockSpec((tt, f), lambda i: (i, 0)), pl.BlockSpec((tt, d), lambda i: (i, 0)),
                  pl.BlockSpec((1, d), lambda i: (0, 0)), pl.BlockSpec(memory_space=pltpu.VMEM),
                  pl.BlockSpec(memory_space=pltpu.VMEM), pl.BlockSpec(memory_space=pltpu.VMEM)],
        out_specs=[pl.BlockSpec((tt, d), lambda i: (i, 0)), pl.BlockSpec((tt, f), lambda i: (i, 0)),
                   pl.BlockSpec((tt, f), lambda i: (i, 0)), pl.BlockSpec((tt, f), lambda i: (i, 0)),
                   pl.BlockSpec((tt, d), lambda i: (i, 0)), pl.BlockSpec((8, d), lambda i: (0, 0))],
        out_shape=[jax.ShapeDtypeStruct((t, d), F32), jax.ShapeDtypeStruct((t, f), BF16),
                   jax.ShapeDtypeStruct((t, f), BF16), jax.ShapeDtypeStruct((t, f), BF16),
                   jax.ShapeDtypeStruct((t, d), BF16), jax.ShapeDtypeStruct((8, d), F32)],
        compiler_params=_params(("arbitrary",)),
    )(dy, a, u, x1, g_ffn, w_gate, w_up, w_down)


def _in_proj_bwd(dparts, dfa_row, w1, wft, x, g_mix, dx1, rider=None):
    t, d = x.shape
    tt = 512
    npart = len(dparts)

    def body(*refs):
        dp_refs = refs[:npart]
        dfa_ref, w_ref, wf_ref, x_ref, g_ref, dx1_ref, dx_ref, dg_ref = refs[npart:]

        @pl.when(pl.program_id(0) == 0)
        def _():
            dg_ref[...] = jnp.zeros_like(dg_ref)

        dh = _dot(dfa_ref[...].astype(BF16), wf_ref[...], TN)
        for j in range(npart):
            dh = dh + _dot(dp_refs[j][...], w_ref[:, j * W_GROUP:(j + 1) * W_GROUP], NT)
        xx = x_ref[...]
        r = lax.rsqrt(jnp.mean(xx * xx, axis=-1, keepdims=True) + EPS)
        xhat = xx * r
        uu = dh * g_ref[...]
        dx_ref[...] = dx1_ref[...] + r * (uu - xhat * jnp.mean(uu * xhat, axis=-1, keepdims=True))
        dg_ref[0:1, :] += jnp.sum(dh * xhat, axis=0, keepdims=True)

    return _host_call(
        body, rider, name="in_proj_bwd", grid=(t // tt,),
        in_specs=[pl.BlockSpec((tt, W_GROUP), lambda i: (i, 0)) for _ in range(npart)]
        + [pl.BlockSpec((8, tt), lambda i: (0, i)), pl.BlockSpec(memory_space=pltpu.VMEM),
           pl.BlockSpec(memory_space=pltpu.VMEM), pl.BlockSpec((tt, d), lambda i: (i, 0)),
           pl.BlockSpec((1, d), lambda i: (0, 0)), pl.BlockSpec((tt, d), lambda i: (i, 0))],
        out_specs=[pl.BlockSpec((tt, d), lambda i: (i, 0)), pl.BlockSpec((8, d), lambda i: (0, 0))],
        out_shape=[jax.ShapeDtypeStruct((t, d), F32), jax.ShapeDtypeStruct((8, d), F32)],
        scratch_shapes=[], inputs=(*dparts, dfa_row, w1, wft, x, g_mix, dx1), semantics=("arbitrary",))


def _token_matmul(a, b, name, tn, a_is_transposed=True):
    m, t = a.shape if a_is_transposed else a.shape[::-1]
    n = b.shape[1]
    tk = 1024

    def body(a_ref, b_ref, o_ref):
        @pl.when(pl.program_id(1) == 0)
        def _():
            o_ref[...] = jnp.zeros_like(o_ref)

        o_ref[...] += _dot(a_ref[...], b_ref[...].astype(BF16), None if a_is_transposed else TN)

    a_spec = pl.BlockSpec((m, tk), lambda j, k: (0, k)) if a_is_transposed else pl.BlockSpec((tk, m), lambda j, k: (k, 0))
    return pl.pallas_call(
        body, name=name, grid=(n // tn, t // tk),
        in_specs=[a_spec, pl.BlockSpec((tk, tn), lambda j, k: (k, j))],
        out_specs=pl.BlockSpec((m, tn), lambda j, k: (0, j)),
        out_shape=jax.ShapeDtypeStruct((m, n), F32),
        compiler_params=_params(("arbitrary", "arbitrary")),
    )(a, b)


def _token_matmul_parts(at, parts, name):
    m, t = at.shape
    widths = [p.shape[1] for p in parts]
    tk = 1024

    def body(a_ref, *refs):
        o_ref = refs[-1]

        @pl.when(pl.program_id(0) == 0)
        def _():
            o_ref[...] = jnp.zeros_like(o_ref)

        a, first = a_ref[...], 0
        for b_ref, w in zip(refs[:-1], widths):
            o_ref[:, first:first + w] += _dot(a, b_ref[...])
            first += w

    return pl.pallas_call(
        body, name=name, grid=(t // tk,),
        in_specs=[pl.BlockSpec((m, tk), lambda k: (0, k))] + [pl.BlockSpec((tk, w), lambda k: (k, 0)) for w in widths],
        out_specs=pl.BlockSpec((m, sum(widths)), lambda k: (0, 0)),
        out_shape=jax.ShapeDtypeStruct((m, sum(widths)), F32),
        compiler_params=_params(("arbitrary",)),
    )(at, *parts)


def _row_matmul(a_row, b, name):
    t, n = b.shape
    tk = 1024
    nk = t // tk

    def body(a_ref, b_ref, o_ref):
        @pl.when(pl.program_id(0) == 0)
        def _():
            o_ref[...] = jnp.zeros_like(o_ref)

        o_ref[...] += _dot(a_ref[...].astype(BF16), b_ref[...])

    return pl.pallas_call(
        body, name=name, grid=(nk,),
        in_specs=[pl.BlockSpec((8, tk), lambda k: (0, k)), pl.BlockSpec((tk, n), lambda k: (k, 0))],
        out_specs=pl.BlockSpec((8, n), lambda k: (0, 0)),
        out_shape=jax.ShapeDtypeStruct((8, n), F32),
        compiler_params=_params(("arbitrary",)),
    )(a_row, b)


FOX_TQ = 512
SUM_LANE = (HEAD_DIM, 0)


def _fox_fwd(proj, c3, gq, gk, nb, seq, rider=None):
    t = nb * seq
    tq = FOX_TQ
    nq = seq // tq
    npair = N_FOX_HEADS // 2

    def body(q_ref, k_ref, v_ref, c_ref, gq_ref, gk_ref, o_ref, lse_ref, qs, ks, vs):
        ones = _group_ones()
        masks = _head_masks()
        qhat, _ = _head_norm(q_ref[...].astype(F32), ones)
        khat, _ = _head_norm(k_ref[...].astype(F32), ones)
        qs[...] = (qhat * gq_ref[...] * (SCALE * LOG2E)).astype(BF16)
        kn = khat * gk_ref[...]
        vv = v_ref[...].astype(F32)
        lane = lax.broadcasted_iota(jnp.int32, (1, LANES), 1)
        for hd in range(2):
            ks[hd] = (kn * masks[hd]).astype(BF16)
            vs[hd] = (vv * masks[hd] + (lane == SUM_LANE[hd]).astype(F32)).astype(BF16)
        row = lax.broadcasted_iota(jnp.int32, (tq, tq), 0)
        col = lax.broadcasted_iota(jnp.int32, (tq, tq), 1)
        causal = col <= row

        for qi in range(nq):
            q0 = qi * tq
            q_blk = qs[q0:q0 + tq, :]
            o_tot = jnp.zeros((tq, LANES), F32)
            lse_tot = jnp.zeros((tq, LANES), F32)
            for hd in range(2):
                crow = c_ref[0, hd:hd + 1, 0:q0 + tq] * LOG2E
                c0 = crow[:, q0:q0 + 1]
                s_d = _dot(q_blk, ks[hd, q0:q0 + tq, :], NT) + (c0 - crow[:, q0:q0 + tq])
                s_d = jnp.where(causal, s_d, NEG)
                m = jnp.max(s_d, axis=-1, keepdims=True)
                if qi > 0:
                    s_o = _dot(q_blk, ks[hd, 0:q0, :], NT) + (c0 - crow[:, 0:q0])
                    m = jnp.maximum(m, jnp.max(s_o, axis=-1, keepdims=True))
                acc = _dot(jnp.exp2(s_d - m).astype(BF16), vs[hd, q0:q0 + tq, :])
                if qi > 0:
                    acc = acc + _dot(jnp.exp2(s_o - m).astype(BF16), vs[hd, 0:q0, :])
                l = acc[:, SUM_LANE[hd]:SUM_LANE[hd] + 1]
                o_tot = o_tot + (acc / l) * masks[hd]
                lse_tot = lse_tot + (m + jnp.log2(l) - c0) * masks[hd]
            o_ref[q0:q0 + tq, :] = o_tot
            lse_ref[q0:q0 + tq, :] = lse_tot

    blk = lambda off: pl.BlockSpec((seq, LANES), lambda b, p: (b, off + p))
    return _host_call(
        body, rider, name="fox_fwd", grid=(nb, npair),
        in_specs=[blk(0), blk(npair), blk(2 * npair), pl.BlockSpec((1, 2, seq), lambda b, p: (p, 0, b)),
                  pl.BlockSpec((1, LANES), lambda b, p: (0, 0)), pl.BlockSpec((1, LANES), lambda b, p: (0, 0))],
        out_specs=[blk(0), blk(0)],
        out_shape=[jax.ShapeDtypeStruct((t, W_GROUP), F32), jax.ShapeDtypeStruct((t, W_GROUP), F32)],
        scratch_shapes=[pltpu.VMEM((seq, LANES), BF16), pltpu.VMEM((2, seq, LANES), BF16),
                        pltpu.VMEM((2, seq, LANES), BF16)],
        inputs=(proj, proj, proj, c3, gq, gk), semantics=("arbitrary", "arbitrary"))


def _fox_bwd(proj, c3, gq, gk, do, o, lse, nb, seq, rider=None):
    t = nb * seq
    tq = FOX_TQ
    nq = seq // tq
    npair = N_FOX_HEADS // 2

    def body(q_ref, k_ref, v_ref, c_ref, gq_ref, gk_ref, do_ref, o_ref, lse_ref,
             dq_ref, dk_ref, dv_ref, dc_ref, dg_ref, qs, ks, vs, kts, dos, lse_t, delta_t, dqt_acc, dk_acc, dv_acc,
             row_sum):
        @pl.when((pl.program_id(0) == 0) & (pl.program_id(1) == 0))
        def _():
            dg_ref[...] = jnp.zeros_like(dg_ref)

        ones = _group_ones()
        masks = _head_masks()
        qhat, rq = _head_norm(q_ref[...].astype(F32), ones)
        khat, rk = _head_norm(k_ref[...].astype(F32), ones)
        qs[...] = (qhat * gq_ref[...] * (SCALE * LOG2E)).astype(BF16)
        kn = khat * gk_ref[...]
        vv = v_ref[...].astype(F32)
        for hd in range(2):
            ks[hd] = (kn * masks[hd]).astype(BF16)
            vs[hd] = (vv * masks[hd]).astype(BF16)
            kts[hd] = ks[hd].T
        dof = do_ref[...]
        dos[...] = dof.astype(BF16)
        lse_t[...] = lse_ref[...].T
        delta_t[...] = _groupsum(dof * o_ref[...], ones).T
        dqt_acc[...] = jnp.zeros_like(dqt_acc)
        dk_acc[...] = jnp.zeros_like(dk_acc)
        dv_acc[...] = jnp.zeros_like(dv_acc)
        row_sum[...] = jnp.zeros_like(row_sum)
        key = lax.broadcasted_iota(jnp.int32, (tq, tq), 0)
        qry = lax.broadcasted_iota(jnp.int32, (tq, tq), 1)
        causal = key <= qry

        for hd in range(2):
            lane0 = hd * HEAD_DIM
            for kj in range(nq):
                k0 = kj * tq
                k_blk = ks[hd, k0:k0 + tq, :]
                v_blk = vs[hd, k0:k0 + tq, :]
                kt_blk = kts[hd, :, k0:k0 + tq]
                crow = c_ref[0, hd:hd + 1, k0:k0 + tq] * LOG2E
                ck0 = crow[:, 0:1]
                bias = jnp.broadcast_to(ck0 - crow, (LANES, tq)).T[:, 0:1]

                def queries_step(r0, r1, diag, hd=hd, lane0=lane0, k_blk=k_blk, v_blk=v_blk, kt_blk=kt_blk,
                                 bias=bias, ck0=ck0):
                    q_r = qs[r0:r1, :]
                    do_r = dos[r0:r1, :]
                    z = _dot(k_blk, q_r, NT) + bias
                    p = jnp.exp2(z - (lse_t[lane0:lane0 + 1, r0:r1] + ck0))
                    if diag:
                        p = jnp.where(causal, p, 0.0)
                    dp = _dot(v_blk, do_r, NT)
                    ds = p * (dp - delta_t[lane0:lane0 + 1, r0:r1])
                    dsb = ds.astype(BF16)
                    dqt_acc[:, r0:r1] += _dot(kt_blk, dsb)
                    row_sum[hd:hd + 1, r0:r1] += jnp.sum(ds, axis=0, keepdims=True)
                    return _dot(dsb, q_r), _dot(p.astype(BF16), do_r), -jnp.sum(ds, axis=1, keepdims=True)

                dk_j, dv_j, dc_j = queries_step(k0, k0 + tq, True)
                if k0 + tq < seq:
                    dk_o, dv_o, dc_o = queries_step(k0 + tq, seq, False)
                    dk_j, dv_j, dc_j = dk_j + dk_o, dv_j + dv_o, dc_j + dc_o
                dk_acc[k0:k0 + tq, :] += dk_j * masks[hd]
                dv_acc[k0:k0 + tq, :] += dv_j * masks[hd]
                dc_ref[0, hd:hd + 1, k0:k0 + tq] = jnp.broadcast_to(dc_j, (tq, LANES)).T[0:1, :]

        dc_ref[0] += row_sum[0:2, :]

        dq_raw, dgq = _head_norm_bwd(dqt_acc[...].T * SCALE, qhat, rq, gq_ref[...], ones)
        dk_raw, dgk = _head_norm_bwd(dk_acc[...] * LN2, khat, rk, gk_ref[...], ones)
        dq_ref[...] = dq_raw.astype(BF16)
        dk_ref[...] = dk_raw.astype(BF16)
        dv_ref[...] = dv_acc[...].astype(BF16)
        dg_ref[0:1, :] += dgq
        dg_ref[1:2, :] += dgk

    blk = lambda off: pl.BlockSpec((seq, LANES), lambda b, p: (b, off + p))
    vec = pl.BlockSpec((1, LANES), lambda b, p: (0, 0))
    c_spec = pl.BlockSpec((1, 2, seq), lambda b, p: (p, 0, b))
    return _host_call(
        body, rider, name="fox_bwd", grid=(nb, npair),
        in_specs=[blk(0), blk(npair), blk(2 * npair), c_spec, vec, vec, blk(0), blk(0), blk(0)],
        out_specs=[blk(0), blk(0), blk(0), c_spec, pl.BlockSpec((8, LANES), lambda b, p: (0, 0))],
        out_shape=[jax.ShapeDtypeStruct((t, W_GROUP), BF16), jax.ShapeDtypeStruct((t, W_GROUP), BF16),
                   jax.ShapeDtypeStruct((t, W_GROUP), BF16), jax.ShapeDtypeStruct((npair, 2, t), F32),
                   jax.ShapeDtypeStruct((8, LANES), F32)],
        scratch_shapes=[pltpu.VMEM((seq, LANES), BF16), pltpu.VMEM((2, seq, LANES), BF16),
                        pltpu.VMEM((2, seq, LANES), BF16), pltpu.VMEM((2, LANES, seq), BF16),
                        pltpu.VMEM((seq, LANES), BF16), pltpu.VMEM((LANES, seq), F32),
                        pltpu.VMEM((LANES, seq), F32), pltpu.VMEM((LANES, seq), F32),
                        pltpu.VMEM((seq, LANES), F32), pltpu.VMEM((seq, LANES), F32),
                        pltpu.VMEM((8, seq), F32)],
        inputs=(proj, proj, proj, c3, gq, gk, do, o, lse), semantics=("arbitrary", "arbitrary"))


def _dil_prep(q_ref, k_ref, gq_ref, gk_ref, cos_ref, up_ref, dn_ref, ones):
    qhat, rq = _head_norm(q_ref[...].astype(F32), ones)
    khat, rk = _head_norm(k_ref[...].astype(F32), ones)
    cos, up, dn = cos_ref[...], up_ref[...], dn_ref[...]
    qn = _rope(qhat * gq_ref[...], cos, up, dn) * (SCALE * LOG2E)
    kn = _rope(khat * gk_ref[...], cos, up, dn)
    return qhat, rq, khat, rk, qn, kn


def _dil_keys(d, seq, pairs):
    nblk = seq // BAND
    per_res = seq // (d * BAND)
    as_blocks = lambda ref, rows: ref[rows, :].reshape(-1, BAND, LANES)
    if per_res == 1:
        a = lax.broadcasted_iota(jnp.int32, (1, BAND, BAND), 1)
        j = lax.broadcasted_iota(jnp.int32, (1, BAND, BAND), 2)
        causal = jnp.where(j <= a, 0.0, NEG)
        return [as_blocks(src, slice(0, seq)) for src, _ in pairs], [causal]
    for src, dst in pairs:
        dst[:, BAND:, :] = as_blocks(src, slice(0, seq))
        dst[1:, :BAND, :] = as_blocks(src, slice(0, seq - BAND))
        dst[0:1, :BAND, :] = jnp.zeros((1, BAND, LANES), BF16)
    a = lax.broadcasted_iota(jnp.int32, (1, BAND, 2 * BAND), 1)
    j = lax.broadcasted_iota(jnp.int32, (1, BAND, 2 * BAND), 2)
    band = jnp.where(((j < BAND) & (j >= a)) | ((j >= BAND) & (j - BAND <= a)), 0.0, NEG)
    e = lax.broadcasted_iota(jnp.int32, (nblk, 1, 2 * BAND), 0)
    j = lax.broadcasted_iota(jnp.int32, (nblk, 1, 2 * BAND), 2)
    no_prev = jnp.where(((e & (per_res - 1)) == 0) & (j < BAND), NEG, 0.0)
    return [dst[...] for _, dst in pairs], [band + no_prev]


def _regroup(d, seq):
    if d == 1:
        return [(slice(0, seq), slice(0, seq))]
    before, n = d // 4, seq // d
    return [(pl.ds(r1 * (seq // before) + r2, n, stride=4), slice((before * r2 + r1) * n, (before * r2 + r1 + 1) * n))
            for r1 in range(before) for r2 in range(4)]


def _dil_fwd(proj, gq, gk, cos, up, dn, nb, seq):
    t = nb * seq
    npair = W_GROUP // LANES
    off = 3 * npair

    def body(q_ref, k_ref, v_ref, gq_ref, gk_ref, cos_ref, up_ref, dn_ref, o_ref, lse_ref,
             src_a, src_b, qp, kp, vp, kw, vw, m_b, l_b, o_b, state_a, state_b):
        ones = _group_ones()
        masks = _head_masks()
        _, _, _, _, qn, kn = _dil_prep(q_ref, k_ref, gq_ref, gk_ref, cos_ref, up_ref, dn_ref, ones)
        src_a[0] = qn
        src_a[1] = kn
        src_a[2] = v_ref[...].astype(F32)
        nblk = seq // BAND
        src, state = (src_a, src_b), (state_a, state_b)

        for d in DILATIONS:
            last = d == DILATIONS[-1]
            for before, after in _regroup(d, seq):
                qv, kv, vv = src[0].at[0][before, :], src[0].at[1][before, :], src[0].at[2][before, :]
                for hd in range(2):
                    qp[hd, after, :] = (qv * masks[hd]).astype(BF16)
                kp[after, :] = kv.astype(BF16)
                vp[after, :] = vv.astype(BF16)
                if d > 1 and not last:
                    src[1][0, after, :], src[1][1, after, :], src[1][2, after, :] = qv, kv, vv
            if d > 1:
                src = src[::-1]
            (keys_k, keys_v), bias = _dil_keys(d, seq, [(kp, kw), (vp, vw)])
            m_t = jnp.zeros((nblk, BAND, LANES), F32)
            l_t = jnp.zeros((nblk, BAND, LANES), F32)
            o_t = jnp.zeros((nblk, BAND, LANES), F32)
            for hd in range(2):
                s = _dot(qp[hd].reshape(nblk, BAND, LANES), keys_k, BATCH_NT)
                for b_ in bias:
                    s = s + b_
                m = jnp.max(s, axis=-1, keepdims=True)
                p = jnp.exp2(s - m)
                m_t = m_t + m * masks[hd]
                l_t = l_t + jnp.sum(p, axis=-1, keepdims=True) * masks[hd]
                o_t = o_t + _dot(p.astype(BF16), keys_v, BATCH_NN) * masks[hd]
            if d == 1:
                state[0][0] = m_t.reshape(seq, LANES)
                state[0][1] = l_t.reshape(seq, LANES)
                state[0][2] = o_t.reshape(seq, LANES)
                continue
            m_b[...] = m_t.reshape(seq, LANES)
            l_b[...] = l_t.reshape(seq, LANES)
            o_b[...] = o_t.reshape(seq, LANES)
            for before, after in _regroup(d, seq):
                m_old = state[0].at[0][before, :]
                m_new = jnp.maximum(m_old, m_b[after, :])
                w_old = jnp.exp2(m_old - m_new)
                w_new = jnp.exp2(m_b[after, :] - m_new)
                state[1][0, after, :] = m_new
                state[1][1, after, :] = state[0].at[1][before, :] * w_old + l_b[after, :] * w_new
                state[1][2, after, :] = state[0].at[2][before, :] * w_old + o_b[after, :] * w_new
            state = state[::-1]

        l = state[0][1]
        o_b[...] = state[0][2] / l
        l_b[...] = state[0][0] + jnp.log2(l)
        held, spare = [o_b, l_b], [m_b, state[1].at[0]]
        for d in DILATIONS[:0:-1]:
            dests = [o_ref, lse_ref] if d == DILATIONS[1] else spare
            for h, dst in zip(held, dests):
                for before, after in _regroup(d, seq):
                    dst[before, :] = h[after, :]
            held, spare = dests, held

    blk = lambda o_: pl.BlockSpec((seq, LANES), lambda b, p: (b, o_ + p))
    vec = pl.BlockSpec((1, LANES), lambda b, p: (0, 0))
    tab = pl.BlockSpec(memory_space=pltpu.VMEM)
    f32_buf = pltpu.VMEM((seq, LANES), F32)
    f32_x3 = pltpu.VMEM((3, seq, LANES), F32)
    bf16_buf = pltpu.VMEM((seq, LANES), BF16)
    window_buf = pltpu.VMEM((seq // BAND, 2 * BAND, LANES), BF16)
    return pl.pallas_call(
        body, name="dil_fwd", grid=(nb, npair),
        in_specs=[blk(off), blk(off + npair), blk(off + 2 * npair), vec, vec, tab, tab, tab],
        out_specs=[blk(0), blk(0)],
        out_shape=[jax.ShapeDtypeStruct((t, W_GROUP), F32), jax.ShapeDtypeStruct((t, W_GROUP), F32)],
        scratch_shapes=[f32_x3, f32_x3, pltpu.VMEM((2, seq, LANES), BF16), bf16_buf, bf16_buf,
                        window_buf, window_buf, f32_buf, f32_buf, f32_buf, f32_x3, f32_x3],
        compiler_params=_params(("arbitrary", "arbitrary")),
    )(proj, proj, proj, gq, gk, cos, up, dn)


def _dil_bwd(proj, gq, gk, cos, up, dn, do, o, lse, nb, seq, rider=None):
    t = nb * seq
    npair = W_GROUP // LANES
    off = 3 * npair

    def body(q_ref, k_ref, v_ref, gq_ref, gk_ref, cos_ref, up_ref, dn_ref, do_ref, o_ref, lse_ref,
             dq_ref, dk_ref, dv_ref, dg_ref, src_a, src_b, sums_a, sums_b,
             qp, kp, vp, dop, kw, vw, lse_p, delta_p, dq_p, dk_p, dv_p):
        @pl.when((pl.program_id(0) == 0) & (pl.program_id(1) == 0))
        def _():
            dg_ref[...] = jnp.zeros_like(dg_ref)

        ones = _group_ones()
        masks = _head_masks()
        qhat, rq, khat, rk, qn, kn = _dil_prep(q_ref, k_ref, gq_ref, gk_ref, cos_ref, up_ref, dn_ref, ones)
        src_a[0] = qn
        src_a[1] = kn
        src_a[2] = v_ref[...].astype(F32)
        src_a[3] = do_ref[...]
        src_a[4] = lse_ref[...]
        src_a[5] = _groupsum(do_ref[...] * o_ref[...], ones)
        nblk = seq // BAND
        src, sums = (src_a, src_b), (sums_a, sums_b)

        for d in DILATIONS:
            last = d == DILATIONS[-1]
            for before, after in _regroup(d, seq):
                planes = [src[0].at[i][before, :] for i in range(6)]
                for hd in range(2):
                    qp[hd, after, :] = (planes[0] * masks[hd]).astype(BF16)
                    dop[hd, after, :] = (planes[3] * masks[hd]).astype(BF16)
                kp[after, :] = planes[1].astype(BF16)
                vp[after, :] = planes[2].astype(BF16)
                lse_p[after, :] = planes[4]
                delta_p[after, :] = planes[5]
                if d > 1 and not last:
                    for i in range(6):
                        src[1][i, after, :] = planes[i]
            if d > 1:
                src = src[::-1]
            (keys_k, keys_v), bias = _dil_keys(d, seq, [(kp, kw), (vp, vw)])
            nk = keys_k.shape[1]
            dq_b = jnp.zeros((nblk, BAND, LANES), F32)
            dk_b = jnp.zeros((nblk, nk, LANES), F32)
            dv_b = jnp.zeros((nblk, nk, LANES), F32)
            for hd in range(2):
                lane0 = hd * HEAD_DIM
                q3 = qp[hd].reshape(nblk, BAND, LANES)
                do3 = dop[hd].reshape(nblk, BAND, LANES)
                z = _dot(q3, keys_k, BATCH_NT)
                for b_ in bias:
                    z = z + b_
                p = jnp.exp2(z - lse_p[...].reshape(nblk, BAND, LANES)[:, :, lane0:lane0 + 1])
                dp = _dot(do3, keys_v, BATCH_NT)
                ds = (p * (dp - delta_p[...].reshape(nblk, BAND, LANES)[:, :, lane0:lane0 + 1])).astype(BF16)
                dq_b = dq_b + _dot(ds, keys_k, BATCH_NN) * masks[hd]
                dk_b = dk_b + _dot(ds, q3, BATCH_TN)
                dv_b = dv_b + _dot(p.astype(BF16), do3, BATCH_TN)
            dq_p[...] = dq_b.reshape(seq, LANES)
            for acc, out in ((dk_b, dk_p), (dv_b, dv_p)):
                out[...] = acc[:, nk - BAND:, :].reshape(seq, LANES)
                if nk > BAND:
                    out[0:seq - BAND, :] += acc[1:, :BAND, :].reshape(seq - BAND, LANES)
            if d == 1:
                sums[0][0], sums[0][1], sums[0][2] = dq_p[...], dk_p[...], dv_p[...]
                continue
            for before, after in _regroup(d, seq):
                for i, part in enumerate((dq_p, dk_p, dv_p)):
                    sums[1][i, after, :] = sums[0].at[i][before, :] + part[after, :]
            sums = sums[::-1]

        for d in DILATIONS[:0:-1]:
            for i in range(3):
                for before, after in _regroup(d, seq):
                    sums[1].at[i][before, :] = sums[0][i, after, :]
            sums = sums[::-1]

        cos, up, dn = cos_ref[...], up_ref[...], dn_ref[...]
        dq_raw, dgq = _head_norm_bwd(_rope_bwd(sums[0][0] * SCALE, cos, up, dn), qhat, rq, gq_ref[...], ones)
        dk_raw, dgk = _head_norm_bwd(_rope_bwd(sums[0][1] * LN2, cos, up, dn), khat, rk, gk_ref[...], ones)
        dq_ref[...] = dq_raw.astype(BF16)
        dk_ref[...] = dk_raw.astype(BF16)
        dv_ref[...] = sums[0][2].astype(BF16)
        dg_ref[0:1, :] += dgq
        dg_ref[1:2, :] += dgk

    blk = lambda o_: pl.BlockSpec((seq, LANES), lambda b, p: (b, o_ + p))
    vec = pl.BlockSpec((1, LANES), lambda b, p: (0, 0))
    tab = pl.BlockSpec(memory_space=pltpu.VMEM)
    f32_buf = pltpu.VMEM((seq, LANES), F32)
    bf16_buf = pltpu.VMEM((seq, LANES), BF16)
    window_buf = pltpu.VMEM((seq // BAND, 2 * BAND, LANES), BF16)
    bf16_pair = pltpu.VMEM((2, seq, LANES), BF16)
    return _host_call(
        body, rider, name="dil_bwd", grid=(nb, npair),
        in_specs=[blk(off), blk(off + npair), blk(off + 2 * npair), vec, vec, tab, tab, tab,
                  blk(0), blk(0), blk(0)],
        out_specs=[blk(0), blk(0), blk(0), pl.BlockSpec((8, LANES), lambda b, p: (0, 0))],
        out_shape=[jax.ShapeDtypeStruct((t, W_GROUP), BF16), jax.ShapeDtypeStruct((t, W_GROUP), BF16),
                   jax.ShapeDtypeStruct((t, W_GROUP), BF16), jax.ShapeDtypeStruct((8, LANES), F32)],
        scratch_shapes=[pltpu.VMEM((6, seq, LANES), F32)] * 2 + [pltpu.VMEM((3, seq, LANES), F32)] * 2
        + [bf16_pair, bf16_buf, bf16_buf, bf16_pair, window_buf, window_buf] + [f32_buf] * 5,
        inputs=(proj, proj, proj, gq, gk, cos, up, dn, do, o, lse), semantics=("arbitrary", "arbitrary"))


def _adamw(w, g, m, v, name, rider=None):
    row_major = w.ndim == 3 and w.shape[1] == 1
    rows, cols = (w.shape[0], w.shape[2]) if row_major else w.shape[-2:]
    if row_major:
        tr = max(t for t in range(1, 65) if rows % t == 0)
    else:
        tr = _row_tile(rows) if rows >= 8 else rows
    c1 = 1.0 - ADAM_B1 ** ADAM_STEP
    c2 = 1.0 - ADAM_B2 ** ADAM_STEP

    def body(w_ref, g_ref, m_ref, v_ref, d_ref, nm_ref, nv_ref):
        g_ = g_ref[...]
        nm = ADAM_B1 * m_ref[...] + (1.0 - ADAM_B1) * g_
        nv = ADAM_B2 * v_ref[...] + (1.0 - ADAM_B2) * (g_ * g_)
        nm_ref[...] = nm
        nv_ref[...] = nv
        d_ref[...] = -ADAM_LR * ((nm / c1) / (jnp.sqrt(nv / c2) + ADAM_EPS) + ADAM_WD * w_ref[...])

    if row_major:
        spec = pl.BlockSpec((tr, 1, cols), lambda i: (i, 0, 0))
    elif w.ndim == 3:
        spec = pl.BlockSpec((1, tr, cols), lambda i: (0, i, 0))
    else:
        spec = pl.BlockSpec((tr, cols), lambda i: (i, 0))
    shape = jax.ShapeDtypeStruct(w.shape, F32)
    return _host_call(
        body, rider, name=name, grid=(rows // tr,), in_specs=[spec] * 4, out_specs=[spec] * 3,
        out_shape=[shape] * 3, scratch_shapes=[], inputs=(w, g, m, v), semantics=("arbitrary",))


def _place():
    x, y, c = lax.axis_index("x"), lax.axis_index("y"), lax.axis_index("c")
    chips = [(1 - x, y), (x, 1 - y), (1 - x, 1 - y)]
    return x, y, c, chips


def _gather_weight(w, name):
    _, rows, cols = w.shape
    half_rows = rows // 2

    def body(w_ref, out_ref, send_sems, recv_sems):
        x, y, c, chips = _place()
        sibling = (x, y, 1 - c)
        mine = 2 * x + y
        lo = pl.multiple_of(c * half_rows, 16)
        lo_sib = pl.multiple_of((1 - c) * half_rows, 16)
        out_ref[mine] = w_ref[0].astype(BF16)

        def copy(k, shard, first_row, to):
            ref = out_ref.at[shard, pl.ds(first_row, half_rows), :]
            return pltpu.make_async_remote_copy(src_ref=ref, dst_ref=ref, send_sem=send_sems.at[k],
                                                recv_sem=recv_sems.at[k], device_id=to, device_id_type=MESH)

        sends = [copy(k, mine, lo, (cx, cy, c)) for k, (cx, cy) in enumerate(chips)]
        for cp in sends:
            cp.start()
        passed = []
        for k, (cx, cy) in enumerate(chips):
            theirs = 2 * cx + cy
            copy(k, theirs, lo, (cx, cy, c)).wait_recv()
            fw = copy(3 + k, theirs, lo, sibling)
            fw.start()
            passed.append(fw)
        for k, (cx, cy) in enumerate(chips):
            copy(3 + k, 2 * cx + cy, lo_sib, sibling).wait_recv()
        for cp in sends + passed:
            cp.wait_send()

    return pl.pallas_call(
        body, name=name,
        in_specs=[pl.BlockSpec(memory_space=pltpu.VMEM)],
        out_specs=pl.BlockSpec(memory_space=pltpu.VMEM),
        out_shape=jax.ShapeDtypeStruct((4, rows, cols), BF16),
        scratch_shapes=[pltpu.SemaphoreType.DMA((6,)), pltpu.SemaphoreType.DMA((6,))],
        compiler_params=pltpu.CompilerParams(vmem_limit_bytes=VMEM_LIMIT),
    )(w)


def _remote(src, dst, sems, k, to):
    send_sems, recv_sems = sems
    return pltpu.make_async_remote_copy(src_ref=src, dst_ref=dst, send_sem=send_sems.at[k], recv_sem=recv_sems.at[k],
                                        device_id=to, device_id_type=MESH)


def _cast_bf16(parts, name):
    def body(*refs):
        for src, dst in zip(refs[:len(parts)], refs[len(parts):]):
            dst[...] = src[0].astype(BF16)

    return pl.pallas_call(
        body, name=name, in_specs=[pl.BlockSpec(memory_space=pltpu.VMEM)] * len(parts),
        out_specs=[pl.BlockSpec(memory_space=pltpu.VMEM)] * len(parts),
        out_shape=[jax.ShapeDtypeStruct(p.shape[1:], BF16) for p in parts],
        compiler_params=pltpu.CompilerParams(vmem_limit_bytes=VMEM_LIMIT),
    )(*parts)


def _gather_rider(shards):
    def copies(ins, outs, sems, which):
        x, y, c, chips = _place()
        sibling = (x, y, 1 - c)
        mine = 2 * x + y
        made = {name: [] for name in which}
        for i, (p_ref, g_ref) in enumerate(zip(ins, outs)):
            half = p_ref.shape[0] // 2
            lo = pl.multiple_of(c * half, 16)
            lo_sib = pl.multiple_of((1 - c) * half, 16)
            spot = lambda shard, first, g_ref=g_ref, half=half: g_ref.at[shard, pl.ds(first, half), :]
            groups = {
                "own": lambda: [pltpu.make_async_copy(p_ref, g_ref.at[mine], sems[0].at[7 * i + 6])],
                "sends": lambda: [_remote(p_ref.at[pl.ds(lo, half), :], spot(mine, lo), sems, 7 * i + k, (cx, cy, c))
                                  for k, (cx, cy) in enumerate(chips)],
                "arrivals": lambda: [_remote(spot(2 * cx + cy, lo), spot(2 * cx + cy, lo), sems, 7 * i + k, (cx, cy, c))
                                     for k, (cx, cy) in enumerate(chips)],
                "passes": lambda: [_remote(spot(2 * cx + cy, lo), spot(2 * cx + cy, lo), sems, 7 * i + 3 + k, sibling)
                                   for k, (cx, cy) in enumerate(chips)],
                "from_sibling": lambda: [_remote(spot(2 * cx + cy, lo_sib), spot(2 * cx + cy, lo_sib), sems,
                                                 7 * i + 3 + k, sibling) for k, (cx, cy) in enumerate(chips)],
            }
            for name in which:
                made[name] += groups[name]()
        return [made[name] for name in which]

    def start(ins, outs, send_sems, recv_sems):
        own, sends = copies(ins, outs, (send_sems, recv_sems), ("own", "sends"))
        for cp in own + sends:
            cp.start()

    def middle(ins, outs, send_sems, recv_sems):
        arrivals, passes = copies(ins, outs, (send_sems, recv_sems), ("arrivals", "passes"))
        for landed, onward in zip(arrivals, passes):
            landed.wait_recv()
            onward.start()

    def finish(ins, outs, send_sems, recv_sems):
        own, sends, passes, from_sibling = copies(ins, outs, (send_sems, recv_sems),
                                                  ("own", "sends", "passes", "from_sibling"))
        for cp in from_sibling:
            cp.wait_recv()
        for cp in sends + passes:
            cp.wait_send()
        for cp in own:
            cp.wait()

    shapes = [jax.ShapeDtypeStruct((4,) + s.shape, BF16) for s in shards]
    return _Rider(shards, shapes, 7 * len(shards), start, finish, middle=middle)


def _exchange_rider(inputs, out_shapes, n_sems, copies, aliases=None):
    def start(ins, outs, send_sems, recv_sems):
        for cp in copies(ins, outs, (send_sems, recv_sems)):
            cp.start()

    def finish(ins, outs, send_sems, recv_sems):
        for cp in copies(ins, outs, (send_sems, recv_sems)):
            cp.wait()

    return _Rider(inputs, out_shapes, n_sems, start, finish, aliases)


def _swap_rider(grads4):
    halves = [g.shape[1] // 2 for g in grads4]

    def copies(ins, outs, sems):
        x, y, c, _ = _place()
        return [_remote(g.at[:, pl.ds(pl.multiple_of((1 - c) * h, 8), h), :], a, sems, i, (x, y, 1 - c))
                for i, (g, a, h) in enumerate(zip(ins, outs, halves))]

    shapes = [jax.ShapeDtypeStruct((4, h, g.shape[2]), F32) for g, h in zip(grads4, halves)]
    return _exchange_rider(grads4, shapes, len(grads4), copies)


def _chip_sum(g4, from_sibling, name):
    _, rows, cols = g4.shape
    half = rows // 2

    def body(g_ref, s_ref, stage_ref, own_ref):
        x, y, c, chips = _place()
        lo = pl.multiple_of(c * half, 8)
        for k, (cx, cy) in enumerate(chips):
            theirs = 2 * cx + cy
            stage_ref[k] = (g_ref[theirs, pl.ds(lo, half), :] + s_ref[theirs]).astype(BF16)
        mine = 2 * x + y
        own_ref[...] = g_ref[mine, pl.ds(lo, half), :] + s_ref[mine]

    return pl.pallas_call(
        body, name=name, in_specs=[pl.BlockSpec(memory_space=pltpu.VMEM)] * 2,
        out_specs=[pl.BlockSpec(memory_space=pltpu.VMEM)] * 2,
        out_shape=[jax.ShapeDtypeStruct((3, half, cols), BF16), jax.ShapeDtypeStruct((half, cols), F32)],
        compiler_params=pltpu.CompilerParams(vmem_limit_bytes=VMEM_LIMIT),
    )(g4, from_sibling)


def _spread_rider(stages):
    def copies(ins, outs, sems):
        _, _, c, chips = _place()
        return [_remote(st.at[k], ld.at[k], sems, 3 * i + k, (cx, cy, c))
                for i, (st, ld) in enumerate(zip(ins, outs)) for k, (cx, cy) in enumerate(chips)]

    shapes = [jax.ShapeDtypeStruct(s.shape, s.dtype) for s in stages]
    return _exchange_rider(stages, shapes, 3 * len(stages), copies)


def _finish_half(own, landed, name):
    half, cols = own.shape

    def body(own_ref, landed_ref, out_ref):
        c = lax.axis_index("c")
        acc = own_ref[...]
        for k in range(3):
            acc = acc + landed_ref[k].astype(F32)
        out_ref[pl.ds(pl.multiple_of(c * half, 8), half), :] = acc

    return pl.pallas_call(
        body, name=name, in_specs=[pl.BlockSpec(memory_space=pltpu.VMEM)] * 2,
        out_specs=pl.BlockSpec(memory_space=pltpu.VMEM),
        out_shape=jax.ShapeDtypeStruct((2 * half, cols), F32),
        compiler_params=pltpu.CompilerParams(vmem_limit_bytes=VMEM_LIMIT),
    )(own, landed)


def _share_rider(fulls):
    def copies(ins, outs, sems):
        x, y, c, _ = _place()
        out = []
        for i, full in enumerate(outs):
            half = full.shape[0] // 2
            rows = full.at[pl.ds(pl.multiple_of(c * half, 8), half), :]
            out.append(_remote(rows, rows, sems, i, (x, y, 1 - c)))
        return out

    def finish_copies(ins, outs, sems):
        x, y, c, _ = _place()
        out = []
        for i, full in enumerate(outs):
            half = full.shape[0] // 2
            mine = full.at[pl.ds(pl.multiple_of(c * half, 8), half), :]
            theirs = full.at[pl.ds(pl.multiple_of((1 - c) * half, 8), half), :]
            out.append((_remote(mine, mine, sems, i, (x, y, 1 - c)), _remote(theirs, theirs, sems, i, (x, y, 1 - c))))
        return out

    def start(ins, outs, send_sems, recv_sems):
        for cp in copies(ins, outs, (send_sems, recv_sems)):
            cp.start()

    def finish(ins, outs, send_sems, recv_sems):
        for sent, landed in finish_copies(ins, outs, (send_sems, recv_sems)):
            sent.wait_send()
            landed.wait_recv()

    shapes = [jax.ShapeDtypeStruct(f.shape, f.dtype) for f in fulls]
    return _Rider(fulls, shapes, len(fulls), start, finish, aliases={i: i for i in range(len(fulls))})


def _all_sum_small(v):
    shape = v.shape

    def body(v_ref, out_ref, buf, send_sems, recv_sems):
        x, y, c, _ = _place()
        me = 4 * x + 2 * y + c
        buf[me] = v_ref[...]
        flips = [(dx, dy, dc) for dx in (0, 1) for dy in (0, 1) for dc in (0, 1)][1:]

        def copy(k, slot, flip):
            dx, dy, dc = flip
            to = (1 - x if dx else x, 1 - y if dy else y, 1 - c if dc else c)
            return pltpu.make_async_remote_copy(src_ref=buf.at[slot], dst_ref=buf.at[slot], send_sem=send_sems.at[k],
                                                recv_sem=recv_sems.at[k], device_id=to, device_id_type=MESH)

        sends = [copy(k, me, flip) for k, flip in enumerate(flips)]
        for cp in sends:
            cp.start()
        for k, (dx, dy, dc) in enumerate(flips):
            sender = 4 * (1 - x if dx else x) + 2 * (1 - y if dy else y) + (1 - c if dc else c)
            copy(k, sender, (dx, dy, dc)).wait_recv()
        for cp in sends:
            cp.wait_send()
        total = buf[0]
        for i in range(1, 8):
            total = total + buf[i]
        out_ref[...] = total

    return pl.pallas_call(
        body, name="all_sum_small",
        in_specs=[pl.BlockSpec(memory_space=pltpu.VMEM)],
        out_specs=pl.BlockSpec(memory_space=pltpu.VMEM),
        out_shape=jax.ShapeDtypeStruct(shape, F32),
        scratch_shapes=[pltpu.VMEM((8,) + shape, F32), pltpu.SemaphoreType.DMA((7,)), pltpu.SemaphoreType.DMA((7,))],
    )(v)


SMALL = (("g_mix", 1024), ("g_ffn", 1024), ("g_out_fox", 512), ("g_out_dil", 512), ("g_q_fox", 64),
         ("g_k_fox", 64), ("g_q_dil", 64), ("g_k_dil", 64), ("b_forget", 8))
SMALL_PACKED = (32, LANES)


def _local_grads(x, target, gains, w1, wft, dense, packed, nb, seq):
    tile2 = lambda g: jnp.tile(g, (1, 2))
    gq_f, gk_f, gq_d, gk_d = (tile2(gains[n]) for n in ("g_q_fox", "g_k_fox", "g_q_dil", "g_k_dil"))
    b_col = gains["b_forget"].reshape(N_FOX_HEADS, 1)
    cos, up, dn = _rope_tables(seq)
    npair = N_FOX_HEADS // 2

    proj, fa_row, h1, h1_t = _in_proj(x, gains["g_mix"], w1, wft)
    c_row = _gate_fwd(fa_row, b_col, seq)
    c3 = c_row.reshape(npair, 2, nb * seq)
    (o_fox, lse_fox), gathered = _fox_fwd(proj, c3, gq_f, gk_f, nb, seq,
                                          rider=None if packed is None else _gather_rider(packed))
    if packed is not None:
        dense = [g.reshape(-1, g.shape[2]) for g in gathered]
    w_out, w_gate, w_up, w_down = dense
    o_dil, lse_dil = _dil_fwd(proj, gq_d, gk_d, cos, up, dn, nb, seq)
    x1, o_n_t = _attn_out(o_fox, o_dil, x, gains["g_out_fox"], gains["g_out_dil"], w_out)
    a, u, dy, loss_parts = _ffn_fwd(x1, target, gains["g_ffn"], w_gate, w_up, w_down)
    loss = jnp.sum(loss_parts[:, 0, 0])

    dx1, s, da, du, h2, dg_ffn = _ffn_bwd(dy, a, u, x1, gains["g_ffn"], w_gate, w_up, w_down)
    d_w_down = _token_matmul(s, dy, "dw_down", 512, False)
    d_w_gate = _token_matmul(da, h2, "dw_gate", 512, False)
    d_w_up = _token_matmul(du, h2, "dw_up", 512, False)
    d_w_out = _token_matmul(o_n_t, dx1, "dw_out", 1024)
    names = ("w_out", "w_gate", "w_up", "w_down")
    grads4 = [g.reshape(4, -1, g.shape[1]) for g in (d_w_out, d_w_gate, d_w_up, d_w_down)]
    exchange = packed is not None
    (do_fox, do_dil, dg_of, dg_od), from_sibling = _attn_out_bwd(
        dx1, o_fox, o_dil, gains["g_out_fox"], gains["g_out_dil"], w_out,
        rider=_swap_rider(grads4) if exchange else None)
    if exchange:
        sums = [_chip_sum(g, s, "chip_sum_" + n) for g, s, n in zip(grads4, from_sibling, names)]
    (dq_f, dk_f, dv_f, dc3, dg_fox), landed = _fox_bwd(
        proj, c3, gq_f, gk_f, do_fox, o_fox, lse_fox, nb, seq,
        rider=_spread_rider([st for st, _ in sums]) if exchange else None)
    if exchange:
        halves = [_finish_half(own, ld, "finish_half_" + n) for (_, own), ld, n in zip(sums, landed, names)]
    (dq_d, dk_d, dv_d, dg_dil), reduced = _dil_bwd(
        proj, gq_d, gk_d, cos, up, dn, do_dil, o_dil, lse_dil, nb, seq,
        rider=_share_rider(halves) if exchange else None)
    if exchange:
        d_w_out, d_w_gate, d_w_up, d_w_down = reduced
    dfa_row, db = _gate_bwd(dc3.reshape(N_FOX_HEADS, nb * seq), fa_row, b_col, seq)
    dparts = [dq_f, dk_f, dv_f, dq_d, dk_d, dv_d]
    d_w1 = _token_matmul_parts(h1_t, dparts, "dw_in")
    d_wf = _row_matmul(dfa_row, h1, "dw_forget")
    fox_w = 3 * W_GROUP
    in_order = [(d_w1[:, :fox_w], fox_w), (d_wf.T, N_FOX_HEADS), (d_w1[:, fox_w:], d_w1.shape[1] - fox_w)]
    n_cols = d_w1.shape[1] + N_FOX_HEADS
    if exchange:
        shards = [jnp.stack([_pick_columns(in_order, s * n_cols // 4, (s + 1) * n_cols // 4) for s in range(4)])]
        _, from_sibling = _idle_host(_swap_rider(shards), "swap_w_in")
        stage, own = _chip_sum(shards[0], from_sibling[0], "chip_sum_w_in")
    (grad_x, dg_mix), landed = _in_proj_bwd(dparts, dfa_row, w1, wft, x, gains["g_mix"], dx1,
                                            rider=_spread_rider([stage]) if exchange else None)
    if exchange:
        d_w_in = _finish_half(own, landed[0], "finish_half_w_in")
    else:
        d_w_in = _pick_columns(in_order, 0, n_cols)

    fold = lambda g2: (g2[:, :HEAD_DIM] + g2[:, HEAD_DIM:])
    small = {
        "g_mix": dg_mix[0:1], "g_ffn": dg_ffn[0:1], "g_out_fox": dg_of[0:1], "g_out_dil": dg_od[0:1],
        "g_q_fox": fold(dg_fox[0:1]), "g_k_fox": fold(dg_fox[1:2]),
        "g_q_dil": fold(dg_dil[0:1]), "g_k_dil": fold(dg_dil[1:2]),
        "b_forget": db[:, 0].reshape(1, N_FOX_HEADS),
    }
    big = {"w_in": d_w_in, "w_out": d_w_out, "w_gate": d_w_gate, "w_up": d_w_up, "w_down": d_w_down}
    return loss, grad_x, big, small


def _pick_columns(pieces, lo, hi):
    out, first = [], 0
    for a, w in pieces:
        a_lo, a_hi = max(lo, first), min(hi, first + w)
        if a_lo < a_hi:
            out.append(a[:, a_lo - first:a_hi - first])
        first += w
    return out[0] if len(out) == 1 else jnp.concatenate(out, axis=1)


def kernel(x, g_mix, w_in, b_forget, g_q_fox, g_k_fox, g_q_dil, g_k_dil, g_out_fox, g_out_dil, w_out, g_ffn, w_gate, w_up, w_down, loss_target, m_g_mix, m_w_in, m_b_forget, m_g_q_fox, m_g_k_fox, m_g_q_dil, m_g_k_dil, m_g_out_fox, m_g_out_dil, m_w_out, m_g_ffn, m_w_gate, m_w_up, m_w_down, v_g_mix, v_w_in, v_b_forget, v_g_q_fox, v_g_k_fox, v_g_q_dil, v_g_k_dil, v_g_out_fox, v_g_out_dil, v_w_out, v_g_ffn, v_w_gate, v_w_up, v_w_down):
    nb, seq, d = x.shape
    weights = dict(g_mix=g_mix, w_in=w_in, b_forget=b_forget, g_q_fox=g_q_fox, g_k_fox=g_k_fox, g_q_dil=g_q_dil,
                   g_k_dil=g_k_dil, g_out_fox=g_out_fox, g_out_dil=g_out_dil, w_out=w_out, g_ffn=g_ffn,
                   w_gate=w_gate, w_up=w_up, w_down=w_down)
    m_in = dict(g_mix=m_g_mix, w_in=m_w_in, b_forget=m_b_forget, g_q_fox=m_g_q_fox, g_k_fox=m_g_k_fox,
                g_q_dil=m_g_q_dil, g_k_dil=m_g_k_dil, g_out_fox=m_g_out_fox, g_out_dil=m_g_out_dil, w_out=m_w_out,
                g_ffn=m_g_ffn, w_gate=m_w_gate, w_up=m_w_up, w_down=m_w_down)
    v_in = dict(g_mix=v_g_mix, w_in=v_w_in, b_forget=v_b_forget, g_q_fox=v_g_q_fox, g_k_fox=v_g_k_fox,
                g_q_dil=v_g_q_dil, g_k_dil=v_g_k_dil, g_out_fox=v_g_out_fox, g_out_dil=v_g_out_dil, w_out=v_w_out,
                g_ffn=v_g_ffn, w_gate=v_w_gate, w_up=v_w_up, w_down=v_w_down)
    order = ["g_mix", "w_in", "b_forget", "g_q_fox", "g_k_fox", "g_q_dil", "g_k_dil", "g_out_fox", "g_out_dil",
             "w_out", "g_ffn", "w_gate", "w_up", "w_down"]

    w_in_all = _gather_weight(w_in, "gather_w_in")
    in_shards = [(w_in_all[s], w_in_all.shape[2]) for s in range(4)]
    fox_w = 3 * W_GROUP
    n_cols = 4 * w_in_all.shape[2]
    w1 = jnp.concatenate([_pick_columns(in_shards, 0, fox_w), _pick_columns(in_shards, fox_w + N_FOX_HEADS, n_cols)],
                         axis=1)
    wft = _pick_columns(in_shards, fox_w, fox_w + N_FOX_HEADS).T
    swap = lambda a: jnp.transpose(a, (0, 2, 1))
    for n in ("w_gate", "w_up"):
        weights[n], m_in[n], v_in[n] = swap(weights[n]), swap(m_in[n]), swap(v_in[n])
    shards = _cast_bf16([weights[n] for n in ("w_out", "w_gate", "w_up", "w_down")], "cast_shards")

    gains = {n: weights[n] for n, _ in SMALL}
    loss, grad_x, big, small = _local_grads(
        x.reshape(nb * seq, d), loss_target.reshape(nb * seq, d), gains, w1, wft, None, shards, nb, seq)

    grads = {n: big[n][None] for n in ("w_out", "w_gate", "w_up", "w_down")}
    packed = jnp.concatenate([small[n].reshape(-1) for n, _ in SMALL] + [loss.reshape(1)])
    packed = jnp.pad(packed, (0, SMALL_PACKED[0] * SMALL_PACKED[1] - packed.shape[0])).reshape(SMALL_PACKED)
    summed = _all_sum_small(packed).reshape(-1)
    pos = 0
    for n, size in SMALL:
        grads[n] = summed[pos:pos + size].reshape(1, size)
        pos += size
    loss = summed[pos]

    to_entry = lambda a: jnp.transpose(a, (2, 0, 1))
    deltas, new_m, new_v, grad_out = {}, {}, {}, {}
    for n in ["w_down"] + [n for n in order if n != "w_down"]:
        rider = _share_rider([big["w_in"]]) if n == "w_down" else None
        (deltas[n], new_m[n], new_v[n]), shared = _adamw(weights[n], grads[n], m_in[n], v_in[n], "adamw_" + n, rider)
        if rider is not None:
            grads["w_in"] = to_entry(shared[0][None])
            weights["w_in"], m_in["w_in"], v_in["w_in"] = (to_entry(a) for a in (w_in, m_w_in, v_w_in))
        grad_out[n] = grads[n]
    for n in ("w_gate", "w_up"):
        grad_out[n], deltas[n], new_m[n], new_v[n] = (swap(a) for a in (grad_out[n], deltas[n], new_m[n], new_v[n]))
    from_entry = lambda a: jnp.transpose(a, (1, 2, 0))
    grad_out["w_in"], deltas["w_in"], new_m["w_in"], new_v["w_in"] = (
        from_entry(a) for a in (grad_out["w_in"], deltas["w_in"], new_m["w_in"], new_v["w_in"]))

    return (loss, grad_x.reshape(nb, seq, d), *[grad_out[n] for n in order], *[deltas[n] for n in order],
            *[new_m[n] for n in order], *[new_v[n] for n in order])
```

```python
import functools
import math

import jax
import jax.numpy as jnp
from jax import lax
from jax.experimental import pallas as pl
from jax.experimental.pallas import tpu as pltpu

F32, BF16 = jnp.float32, jnp.bfloat16
MESH = pl.DeviceIdType.MESH

EPS = 1e-6
NEG = -1e30
HEAD_DIM = 64
SCALE = HEAD_DIM ** -0.5
LOG2E = math.log2(math.e)
LN2 = math.log(2.0)
ROPE_THETA = 500000.0
ROPE_DIM = HEAD_DIM // 4
LANES = 128
W_GROUP = 512
N_FOX_HEADS = 8
VMEM_LIMIT = 56 * 1024 * 1024
DILATIONS = (1, 4, 16)
BAND = 128

ADAM_LR, ADAM_B1, ADAM_B2, ADAM_EPS, ADAM_WD, ADAM_STEP = 0.001, 0.9, 0.999, 1e-08, 0.01, 10

NT = (((1,), (1,)), ((), ()))
TN = (((0,), (0,)), ((), ()))
BATCH_NT = (((2,), (2,)), ((0,), (0,)))
BATCH_NN = (((2,), (1,)), ((0,), (0,)))
BATCH_TN = (((1,), (1,)), ((0,), (0,)))


def _params(sem=None):
    return pltpu.CompilerParams(dimension_semantics=sem, vmem_limit_bytes=VMEM_LIMIT)


def _dot(a, b, dims=None):
    if dims is None:
        return jnp.dot(a, b, preferred_element_type=F32)
    return lax.dot_general(a, b, dims, preferred_element_type=F32)


def _group_ones():
    i = lax.broadcasted_iota(jnp.int32, (LANES, LANES), 0) >> 6
    j = lax.broadcasted_iota(jnp.int32, (LANES, LANES), 1) >> 6
    return (i == j).astype(BF16)


def _split3(x):
    a = x.astype(BF16)
    r = x - a.astype(F32)
    b = r.astype(BF16)
    c = (r - b.astype(F32)).astype(BF16)
    return a, b, c


def _groupsum(x, ones, pieces=2):
    total = None
    for _ in range(pieces):
        piece = x.astype(BF16)
        part = _dot(piece, ones)
        total = part if total is None else total + part
        x = x - piece.astype(F32)
    return total


def _head_masks():
    lane = lax.broadcasted_iota(jnp.int32, (1, LANES), 1)
    return [(lane < HEAD_DIM).astype(F32), (lane >= HEAD_DIM).astype(F32)]


def _head_norm(raw, ones):
    r = lax.rsqrt(_groupsum(raw * raw, ones, 1) * (1.0 / HEAD_DIM) + EPS)
    return raw * r, r


def _head_norm_bwd(dy, xhat, r, gain, ones):
    u = dy * gain
    dgain = jnp.sum(dy * xhat, axis=0, keepdims=True)
    draw = r * (u - xhat * (_groupsum(u * xhat, ones) * (1.0 / HEAD_DIM)))
    return draw, dgain


def _rope(x, cos, s_up, s_dn):
    return x * cos + pltpu.roll(x, LANES - 8, 1) * s_up + pltpu.roll(x, 8, 1) * s_dn


def _rope_bwd(dy, cos, s_up, s_dn):
    return dy * cos + pltpu.roll(dy * s_up, 8, 1) + pltpu.roll(dy * s_dn, LANES - 8, 1)


def _rope_tables(seq):
    half = ROPE_DIM // 2
    inv_freq = jnp.power(jnp.float32(ROPE_THETA), -jnp.arange(half, dtype=F32) * 2.0 / ROPE_DIM)
    ang = jnp.arange(seq).astype(F32)[:, None] * inv_freq[None, :]
    cos, sin = jnp.cos(ang), jnp.sin(ang)
    one = jnp.ones((seq, HEAD_DIM - ROPE_DIM), F32)
    zero_h = jnp.zeros((seq, half), F32)
    zero_r = jnp.zeros((seq, HEAD_DIM - ROPE_DIM), F32)
    c = jnp.concatenate([cos, cos, one], axis=1)
    up = jnp.concatenate([-sin, zero_h, zero_r], axis=1)
    dn = jnp.concatenate([zero_h, sin, zero_r], axis=1)
    return jnp.tile(c, (1, 2)), jnp.tile(up, (1, 2)), jnp.tile(dn, (1, 2))


def _row_tile(rows, cap=256):
    best = rows
    for t in range(8, min(rows, cap) + 1, 8):
        if rows % t == 0:
            best = t
    return best


class _Rider:
    def __init__(self, inputs, out_shapes, n_sems, start, finish, aliases=None, middle=None):
        self.inputs, self.out_shapes, self.n_sems = list(inputs), list(out_shapes), n_sems
        self.start, self.finish, self.middle, self.aliases = start, finish, middle, dict(aliases or {})


def _host_call(body, rider, *, name, grid, in_specs, out_specs, out_shape, scratch_shapes, inputs, semantics):
    if rider is None:
        return pl.pallas_call(body, name=name, grid=grid, in_specs=in_specs, out_specs=out_specs,
                              out_shape=out_shape, scratch_shapes=scratch_shapes,
                              compiler_params=_params(semantics))(*inputs), []
    n_in, n_out, n_scr = len(in_specs), len(out_specs), len(scratch_shapes)
    r_in, r_out = len(rider.inputs), len(rider.out_shapes)

    def wrapped(*refs):
        ins, refs = refs[:n_in], refs[n_in:]
        r_ins, refs = refs[:r_in], refs[r_in:]
        outs, refs = refs[:n_out], refs[n_out:]
        r_outs, refs = refs[:r_out], refs[r_out:]
        scratch, (send_sems, recv_sems) = refs[:n_scr], refs[n_scr:]
        ids = [pl.program_id(a) for a in range(len(grid))]
        first = functools.reduce(lambda p, q: p & q, [i == 0 for i in ids])
        last = functools.reduce(lambda p, q: p & q, [i == g - 1 for i, g in zip(ids, grid)])

        @pl.when(first)
        def _():
            rider.start(r_ins, r_outs, send_sems, recv_sems)

        body(*ins, *outs, *scratch)

        if rider.middle is not None:
            step, steps = ids[0], grid[0]
            for i, g in zip(ids[1:], grid[1:]):
                step, steps = step * g + i, steps * g

            @pl.when(step == (3 * steps) // 4)
            def _():
                rider.middle(r_ins, r_outs, send_sems, recv_sems)

        @pl.when(last)
        def _():
            rider.finish(r_ins, r_outs, send_sems, recv_sems)

    hbm = pl.BlockSpec(memory_space=pl.ANY)
    res = pl.pallas_call(
        wrapped, name=name, grid=grid,
        in_specs=list(in_specs) + [hbm] * r_in, out_specs=list(out_specs) + [hbm] * r_out,
        out_shape=list(out_shape) + rider.out_shapes,
        scratch_shapes=list(scratch_shapes) + [pltpu.SemaphoreType.DMA((rider.n_sems,))] * 2,
        input_output_aliases={n_in + i: n_out + o for i, o in rider.aliases.items()},
        compiler_params=_params(semantics),
    )(*inputs, *rider.inputs)
    return res[:n_out], res[n_out:]


def _hand_over(arrays, name):
    def body(*refs):
        del refs

    hbm = pl.BlockSpec(memory_space=pl.ANY)
    return pl.pallas_call(
        body, name=name, in_specs=[hbm] * len(arrays), out_specs=[hbm] * len(arrays),
        out_shape=[jax.ShapeDtypeStruct(a.shape, a.dtype) for a in arrays],
        input_output_aliases={i: i for i in range(len(arrays))},
    )(*arrays)


def _idle_host(rider, name):
    def body(o_ref):
        o_ref[...] = jnp.zeros_like(o_ref)

    return _host_call(body, rider, name=name, grid=(1,), in_specs=[],
                      out_specs=[pl.BlockSpec((8, LANES), lambda i: (0, 0))],
                      out_shape=[jax.ShapeDtypeStruct((8, LANES), F32)], scratch_shapes=[], inputs=(),
                      semantics=("arbitrary",))


def _in_proj(x, g_mix, w1, wft):
    t, d = x.shape
    n = w1.shape[1]
    tt = 512

    def body(x_ref, g_ref, w_ref, wf_ref, p_ref, fa_ref, h_ref, ht_ref):
        xx = x_ref[...]
        r = lax.rsqrt(jnp.mean(xx * xx, axis=-1, keepdims=True) + EPS)
        h = (xx * r * g_ref[...]).astype(BF16)
        h_ref[...] = h
        ht_ref[...] = h.T
        for j in range(n // W_GROUP):
            cols = slice(j * W_GROUP, (j + 1) * W_GROUP)
            p_ref[:, cols] = _dot(h, w_ref[:, cols]).astype(BF16)
        fa_ref[...] = _dot(wf_ref[...], h, NT)

    return pl.pallas_call(
        body, name="in_proj", grid=(t // tt,),
        in_specs=[pl.BlockSpec((tt, d), lambda i: (i, 0)), pl.BlockSpec((1, d), lambda i: (0, 0)),
                  pl.BlockSpec(memory_space=pltpu.VMEM), pl.BlockSpec(memory_space=pltpu.VMEM)],
        out_specs=[pl.BlockSpec((tt, n), lambda i: (i, 0)), pl.BlockSpec((8, tt), lambda i: (0, i)),
                   pl.BlockSpec((tt, d), lambda i: (i, 0)), pl.BlockSpec((d, tt), lambda i: (0, i))],
        out_shape=[jax.ShapeDtypeStruct((t, n), BF16), jax.ShapeDtypeStruct((8, t), F32),
                   jax.ShapeDtypeStruct((t, d), BF16), jax.ShapeDtypeStruct((d, t), BF16)],
        compiler_params=_params(("arbitrary",)),
    )(x, g_mix, w1, wft)


def _tri(n, upper):
    i = lax.broadcasted_iota(jnp.int32, (n, n), 0)
    j = lax.broadcasted_iota(jnp.int32, (n, n), 1)
    return ((i <= j) if upper else (i >= j)).astype(BF16)


def _gate_fwd(fa_row, b_col, seq):
    t = fa_row.shape[1]
    cb = 256

    def body(fa_ref, b_ref, c_ref):
        tri = _tri(cb, True)
        carry = jnp.zeros((8, 1), F32)
        for k in range(seq // cb):
            z = fa_ref[:, k * cb:(k + 1) * cb] + b_ref[...]
            lf = jnp.minimum(z, 0.0) - jnp.log(1.0 + jnp.exp(-jnp.abs(z)))
            a, b, c = _split3(lf)
            blk = _dot(a, tri) + _dot(b, tri) + _dot(c, tri) + carry
            c_ref[:, k * cb:(k + 1) * cb] = blk
            carry = blk[:, cb - 1:cb]

    return pl.pallas_call(
        body, name="gate_fwd", grid=(t // seq,),
        in_specs=[pl.BlockSpec((8, seq), lambda i: (0, i)), pl.BlockSpec((8, 1), lambda i: (0, 0))],
        out_specs=pl.BlockSpec((8, seq), lambda i: (0, i)),
        out_shape=jax.ShapeDtypeStruct((8, t), F32),
        compiler_params=_params(("arbitrary",)),
    )(fa_row, b_col)


def _gate_bwd(dc_row, fa_row, b_col, seq):
    t = fa_row.shape[1]
    cb = 256

    def body(dc_ref, fa_ref, b_ref, dfa_ref, db_ref):
        @pl.when(pl.program_id(0) == 0)
        def _():
            db_ref[...] = jnp.zeros_like(db_ref)

        tri = _tri(cb, False)
        carry = jnp.zeros((8, 1), F32)
        dbs = jnp.zeros((8, 1), F32)
        for k in reversed(range(seq // cb)):
            a, b, c = _split3(dc_ref[:, k * cb:(k + 1) * cb])
            dlf = _dot(a, tri) + _dot(b, tri) + _dot(c, tri) + carry
            carry = dlf[:, 0:1]
            z = fa_ref[:, k * cb:(k + 1) * cb] + b_ref[...]
            dfa = dlf / (1.0 + jnp.exp(z))
            dfa_ref[:, k * cb:(k + 1) * cb] = dfa
            dbs = dbs + jnp.sum(dfa, axis=1, keepdims=True)
        db_ref[...] += jnp.broadcast_to(dbs, (8, LANES))

    return pl.pallas_call(
        body, name="gate_bwd", grid=(t // seq,),
        in_specs=[pl.BlockSpec((8, seq), lambda i: (0, i)), pl.BlockSpec((8, seq), lambda i: (0, i)),
                  pl.BlockSpec((8, 1), lambda i: (0, 0))],
        out_specs=[pl.BlockSpec((8, seq), lambda i: (0, i)), pl.BlockSpec((8, LANES), lambda i: (0, 0))],
        out_shape=[jax.ShapeDtypeStruct((8, t), F32), jax.ShapeDtypeStruct((8, LANES), F32)],
        compiler_params=_params(("arbitrary",)),
    )(dc_row, fa_row, b_col)


def _attn_out(o_fox, o_dil, x, g_fox, g_dil, w_out):
    t, d = x.shape
    w = o_fox.shape[1]
    tt = 512

    def body(of_ref, od_ref, x_ref, gf_ref, gd_ref, w_ref, x1_ref, ont_ref):
        acc = x_ref[...]
        for k, (o_ref, g_ref) in enumerate(((of_ref, gf_ref), (od_ref, gd_ref))):
            o = o_ref[...]
            r = lax.rsqrt(jnp.mean(o * o, axis=-1, keepdims=True) + EPS)
            on = (o * r * g_ref[...]).astype(BF16)
            ont_ref[k * w:(k + 1) * w, :] = on.T
            acc = acc + _dot(on, w_ref[k * w:(k + 1) * w, :])
        x1_ref[...] = acc

    return pl.pallas_call(
        body, name="attn_out", grid=(t // tt,),
        in_specs=[pl.BlockSpec((tt, w), lambda i: (i, 0)), pl.BlockSpec((tt, w), lambda i: (i, 0)),
                  pl.BlockSpec((tt, d), lambda i: (i, 0)), pl.BlockSpec((1, w), lambda i: (0, 0)),
                  pl.BlockSpec((1, w), lambda i: (0, 0)), pl.BlockSpec(memory_space=pltpu.VMEM)],
        out_specs=[pl.BlockSpec((tt, d), lambda i: (i, 0)), pl.BlockSpec((2 * w, tt), lambda i: (0, i))],
        out_shape=[jax.ShapeDtypeStruct((t, d), F32), jax.ShapeDtypeStruct((2 * w, t), BF16)],
        compiler_params=_params(("arbitrary",)),
    )(o_fox, o_dil, x, g_fox, g_dil, w_out)


def _attn_out_bwd(dx1, o_fox, o_dil, g_fox, g_dil, w_out, rider=None):
    t, d = dx1.shape
    w = o_fox.shape[1]
    tt = 512

    def body(dx_ref, of_ref, od_ref, gf_ref, gd_ref, w_ref, dof_ref, dod_ref, dgf_ref, dgd_ref):
        @pl.when(pl.program_id(0) == 0)
        def _():
            dgf_ref[...] = jnp.zeros_like(dgf_ref)
            dgd_ref[...] = jnp.zeros_like(dgd_ref)

        dxb = dx_ref[...].astype(BF16)
        for k, (o_ref, g_ref, do_ref, dg_ref) in enumerate(
                ((of_ref, gf_ref, dof_ref, dgf_ref), (od_ref, gd_ref, dod_ref, dgd_ref))):
            don = _dot(dxb, w_ref[k * w:(k + 1) * w, :], NT)
            o = o_ref[...]
            r = lax.rsqrt(jnp.mean(o * o, axis=-1, keepdims=True) + EPS)
            xhat = o * r
            u = don * g_ref[...]
            do_ref[...] = r * (u - xhat * jnp.mean(u * xhat, axis=-1, keepdims=True))
            dg_ref[0:1, :] += jnp.sum(don * xhat, axis=0, keepdims=True)

    return _host_call(
        body, rider, name="attn_out_bwd", grid=(t // tt,),
        in_specs=[pl.BlockSpec((tt, d), lambda i: (i, 0)), pl.BlockSpec((tt, w), lambda i: (i, 0)),
                  pl.BlockSpec((tt, w), lambda i: (i, 0)), pl.BlockSpec((1, w), lambda i: (0, 0)),
                  pl.BlockSpec((1, w), lambda i: (0, 0)), pl.BlockSpec(memory_space=pltpu.VMEM)],
        out_specs=[pl.BlockSpec((tt, w), lambda i: (i, 0)), pl.BlockSpec((tt, w), lambda i: (i, 0)),
                   pl.BlockSpec((8, w), lambda i: (0, 0)), pl.BlockSpec((8, w), lambda i: (0, 0))],
        out_shape=[jax.ShapeDtypeStruct((t, w), F32), jax.ShapeDtypeStruct((t, w), F32),
                   jax.ShapeDtypeStruct((8, w), F32), jax.ShapeDtypeStruct((8, w), F32)],
        scratch_shapes=[], inputs=(dx1, o_fox, o_dil, g_fox, g_dil, w_out), semantics=("arbitrary",))


def _ffn_fwd(x1, target, g_ffn, w_gate, w_up, w_down):
    t, d = x1.shape
    f = w_gate.shape[0]
    tt = 256

    def body(x_ref, t_ref, g_ref, wg_ref, wu_ref, wd_ref, a_ref, u_ref, dy_ref, loss_ref):
        xx = x_ref[...]
        r = lax.rsqrt(jnp.mean(xx * xx, axis=-1, keepdims=True) + EPS)
        h = (xx * r * g_ref[...]).astype(BF16)
        a = _dot(h, wg_ref[...], NT)
        u = _dot(h, wu_ref[...], NT)
        a_ref[...] = a.astype(BF16)
        u_ref[...] = u.astype(BF16)
        s = (a / (1.0 + jnp.exp(-a)) * u).astype(BF16)
        y = xx + _dot(s, wd_ref[...])
        e = y - t_ref[...]
        dy_ref[...] = e * (1.0 / d)
        loss_ref[...] = jnp.broadcast_to(0.5 * jnp.sum(e * e) * (1.0 / d), (1, 8, LANES))

    return pl.pallas_call(
        body, name="ffn_fwd", grid=(t // tt,),
        in_specs=[pl.BlockSpec((tt, d), lambda i: (i, 0)), pl.BlockSpec((tt, d), lambda i: (i, 0)),
                  pl.BlockSpec((1, d), lambda i: (0, 0)), pl.BlockSpec(memory_space=pltpu.VMEM),
                  pl.BlockSpec(memory_space=pltpu.VMEM), pl.BlockSpec(memory_space=pltpu.VMEM)],
        out_specs=[pl.BlockSpec((tt, f), lambda i: (i, 0)), pl.BlockSpec((tt, f), lambda i: (i, 0)),
                   pl.BlockSpec((tt, d), lambda i: (i, 0)), pl.BlockSpec((1, 8, LANES), lambda i: (i, 0, 0))],
        out_shape=[jax.ShapeDtypeStruct((t, f), BF16), jax.ShapeDtypeStruct((t, f), BF16),
                   jax.ShapeDtypeStruct((t, d), F32), jax.ShapeDtypeStruct((t // tt, 8, LANES), F32)],
        compiler_params=_params(("arbitrary",)),
    )(x1, target, g_ffn, w_gate, w_up, w_down)


def _ffn_bwd(dy, a, u, x1, g_ffn, w_gate, w_up, w_down):
    t, d = x1.shape
    f = w_gate.shape[0]
    tt = 256

    def body(dy_ref, a_ref, u_ref, x_ref, g_ref, wg_ref, wu_ref, wd_ref,
             dx_ref, s_ref, da_ref, du_ref, h_ref, dg_ref):
        @pl.when(pl.program_id(0) == 0)
        def _():
            dg_ref[...] = jnp.zeros_like(dg_ref)

        dy_ = dy_ref[...]
        ds = _dot(dy_.astype(BF16), wd_ref[...], NT)
        a_ = a_ref[...].astype(F32)
        u_ = u_ref[...].astype(F32)
        sig = 1.0 / (1.0 + jnp.exp(-a_))
        silu = a_ * sig
        s_ref[...] = (silu * u_).astype(BF16)
        da = (ds * u_ * (sig * (1.0 + a_ * (1.0 - sig)))).astype(BF16)
        du = (ds * silu).astype(BF16)
        da_ref[...] = da
        du_ref[...] = du
        dh = _dot(da, wg_ref[...]) + _dot(du, wu_ref[...])
        xx = x_ref[...]
        r = lax.rsqrt(jnp.mean(xx * xx, axis=-1, keepdims=True) + EPS)
        xhat = xx * r
        g = g_ref[...]
        h_ref[...] = (xhat * g).astype(BF16)
        uu = dh * g
        dx_ref[...] = dy_ + r * (uu - xhat * jnp.mean(uu * xhat, axis=-1, keepdims=True))
        dg_ref[0:1, :] += jnp.sum(dh * xhat, axis=0, keepdims=True)

    return pl.pallas_call(
        body, name="ffn_bwd", grid=(t // tt,),
        in_specs=[pl.BlockSpec((tt, d), lambda i: (i, 0)), pl.BlockSpec((tt, f), lambda i: (i, 0)),
                  pl.BlockSpec((tt, f), lambda i: (i, 0)), pl.BlockSpec((tt, d), lambda i: (i, 0)),
                  pl.BlockSpec((1, d), lambda i: (0, 0)), pl.BlockSpec(memory_space=pltpu.VMEM),
                  pl.BlockSpec(memory_space=pltpu.VMEM), pl.BlockSpec(memory_space=pltpu.VMEM)],
        out_specs=[pl.BlockSpec((tt, d), lambda i: (i, 0)), pl.BlockSpec((tt, f), lambda i: (i, 0)),
                   pl.BlockSpec((tt, f), lambda i: (i, 0)), pl.BlockSpec((tt, f), lambda i: (i, 0)),
                   pl.BlockSpec((tt, d), lambda i: (i, 0)), pl.BlockSpec((8, d), lambda i: (0, 0))],
        out_shape=[jax.ShapeDtypeStruct((t, d), F32), jax.ShapeDtypeStruct((t, f), BF16),
                   jax.ShapeDtypeStruct((t, f), BF16), jax.ShapeDtypeStruct((t, f), BF16),
                   jax.ShapeDtypeStruct((t, d), BF16), jax.ShapeDtypeStruct((8, d), F32)],
        compiler_params=_params(("arbitrary",)),
    )(dy, a, u, x1, g_ffn, w_gate, w_up, w_down)


def _in_proj_bwd(dparts, dfa_row, w1, wft, x, g_mix, dx1, rider=None):
    t, d = x.shape
    tt = 512
    npart = len(dparts)

    def body(*refs):
        dp_refs = refs[:npart]
        dfa_ref, w_ref, wf_ref, x_ref, g_ref, dx1_ref, dx_ref, dg_ref = refs[npart:]

        @pl.when(pl.program_id(0) == 0)
        def _():
            dg_ref[...] = jnp.zeros_like(dg_ref)

        dh = _dot(dfa_ref[...].astype(BF16), wf_ref[...], TN)
        for j in range(npart):
            dh = dh + _dot(dp_refs[j][...], w_ref[:, j * W_GROUP:(j + 1) * W_GROUP], NT)
        xx = x_ref[...]
        r = lax.rsqrt(jnp.mean(xx * xx, axis=-1, keepdims=True) + EPS)
        xhat = xx * r
        uu = dh * g_ref[...]
        dx_ref[...] = dx1_ref[...] + r * (uu - xhat * jnp.mean(uu * xhat, axis=-1, keepdims=True))
        dg_ref[0:1, :] += jnp.sum(dh * xhat, axis=0, keepdims=True)

    return _host_call(
        body, rider, name="in_proj_bwd", grid=(t // tt,),
        in_specs=[pl.BlockSpec((tt, W_GROUP), lambda i: (i, 0)) for _ in range(npart)]
        + [pl.BlockSpec((8, tt), lambda i: (0, i)), pl.BlockSpec(memory_space=pltpu.VMEM),
           pl.BlockSpec(memory_space=pltpu.VMEM), pl.BlockSpec((tt, d), lambda i: (i, 0)),
           pl.BlockSpec((1, d), lambda i: (0, 0)), pl.BlockSpec((tt, d), lambda i: (i, 0))],
        out_specs=[pl.BlockSpec((tt, d), lambda i: (i, 0)), pl.BlockSpec((8, d), lambda i: (0, 0))],
        out_shape=[jax.ShapeDtypeStruct((t, d), F32), jax.ShapeDtypeStruct((8, d), F32)],
        scratch_shapes=[], inputs=(*dparts, dfa_row, w1, wft, x, g_mix, dx1), semantics=("arbitrary",))


def _token_matmul(a, b, name, tn, a_is_transposed=True):
    m, t = a.shape if a_is_transposed else a.shape[::-1]
    n = b.shape[1]
    tk = 1024

    def body(a_ref, b_ref, o_ref):
        @pl.when(pl.program_id(1) == 0)
        def _():
            o_ref[...] = jnp.zeros_like(o_ref)

        o_ref[...] += _dot(a_ref[...], b_ref[...].astype(BF16), None if a_is_transposed else TN)

    a_spec = pl.BlockSpec((m, tk), lambda j, k: (0, k)) if a_is_transposed else pl.BlockSpec((tk, m), lambda j, k: (k, 0))
    return pl.pallas_call(
        body, name=name, grid=(n // tn, t // tk),
        in_specs=[a_spec, pl.BlockSpec((tk, tn), lambda j, k: (k, j))],
        out_specs=pl.BlockSpec((m, tn), lambda j, k: (0, j)),
        out_shape=jax.ShapeDtypeStruct((m, n), F32),
        compiler_params=_params(("arbitrary", "arbitrary")),
    )(a, b)


def _token_matmul_parts(at, parts, name):
    m, t = at.shape
    widths = [p.shape[1] for p in parts]
    tk = 1024

    def body(a_ref, *refs):
        o_ref = refs[-1]

        @pl.when(pl.program_id(0) == 0)
        def _():
            o_ref[...] = jnp.zeros_like(o_ref)

        a, first = a_ref[...], 0
        for b_ref, w in zip(refs[:-1], widths):
            o_ref[:, first:first + w] += _dot(a, b_ref[...])
            first += w

    return pl.pallas_call(
        body, name=name, grid=(t // tk,),
        in_specs=[pl.BlockSpec((m, tk), lambda k: (0, k))] + [pl.BlockSpec((tk, w), lambda k: (k, 0)) for w in widths],
        out_specs=pl.BlockSpec((m, sum(widths)), lambda k: (0, 0)),
        out_shape=jax.ShapeDtypeStruct((m, sum(widths)), F32),
        compiler_params=_params(("arbitrary",)),
    )(at, *parts)


def _row_matmul(a_row, b, name):
    t, n = b.shape
    tk = 1024
    nk = t // tk

    def body(a_ref, b_ref, o_ref):
        @pl.when(pl.program_id(0) == 0)
        def _():
            o_ref[...] = jnp.zeros_like(o_ref)

        o_ref[...] += _dot(a_ref[...].astype(BF16), b_ref[...])

    return pl.pallas_call(
        body, name=name, grid=(nk,),
        in_specs=[pl.BlockSpec((8, tk), lambda k: (0, k)), pl.BlockSpec((tk, n), lambda k: (k, 0))],
        out_specs=pl.BlockSpec((8, n), lambda k: (0, 0)),
        out_shape=jax.ShapeDtypeStruct((8, n), F32),
        compiler_params=_params(("arbitrary",)),
    )(a_row, b)


FOX_TQ = 512
SUM_LANE = (HEAD_DIM, 0)


def _fox_fwd(proj, c3, gq, gk, nb, seq, rider=None):
    t = nb * seq
    tq = FOX_TQ
    nq = seq // tq
    npair = N_FOX_HEADS // 2

    def body(q_ref, k_ref, v_ref, c_ref, gq_ref, gk_ref, o_ref, lse_ref, qs, ks, vs):
        ones = _group_ones()
        masks = _head_masks()
        qhat, _ = _head_norm(q_ref[...].astype(F32), ones)
        khat, _ = _head_norm(k_ref[...].astype(F32), ones)
        qs[...] = (qhat * gq_ref[...] * (SCALE * LOG2E)).astype(BF16)
        kn = khat * gk_ref[...]
        vv = v_ref[...].astype(F32)
        lane = lax.broadcasted_iota(jnp.int32, (1, LANES), 1)
        for hd in range(2):
            ks[hd] = (kn * masks[hd]).astype(BF16)
            vs[hd] = (vv * masks[hd] + (lane == SUM_LANE[hd]).astype(F32)).astype(BF16)
        row = lax.broadcasted_iota(jnp.int32, (tq, tq), 0)
        col = lax.broadcasted_iota(jnp.int32, (tq, tq), 1)
        causal = col <= row

        for qi in range(nq):
            q0 = qi * tq
            q_blk = qs[q0:q0 + tq, :]
            o_tot = jnp.zeros((tq, LANES), F32)
            lse_tot = jnp.zeros((tq, LANES), F32)
            for hd in range(2):
                crow = c_ref[0, hd:hd + 1, 0:q0 + tq] * LOG2E
                c0 = crow[:, q0:q0 + 1]
                s_d = _dot(q_blk, ks[hd, q0:q0 + tq, :], NT) + (c0 - crow[:, q0:q0 + tq])
                s_d = jnp.where(causal, s_d, NEG)
                m = jnp.max(s_d, axis=-1, keepdims=True)
                if qi > 0:
                    s_o = _dot(q_blk, ks[hd, 0:q0, :], NT) + (c0 - crow[:, 0:q0])
                    m = jnp.maximum(m, jnp.max(s_o, axis=-1, keepdims=True))
                acc = _dot(jnp.exp2(s_d - m).astype(BF16), vs[hd, q0:q0 + tq, :])
                if qi > 0:
                    acc = acc + _dot(jnp.exp2(s_o - m).astype(BF16), vs[hd, 0:q0, :])
                l = acc[:, SUM_LANE[hd]:SUM_LANE[hd] + 1]
                o_tot = o_tot + (acc / l) * masks[hd]
                lse_tot = lse_tot + (m + jnp.log2(l) - c0) * masks[hd]
            o_ref[q0:q0 + tq, :] = o_tot
            lse_ref[q0:q0 + tq, :] = lse_tot

    blk = lambda off: pl.BlockSpec((seq, LANES), lambda b, p: (b, off + p))
    return _host_call(
        body, rider, name="fox_fwd", grid=(nb, npair),
        in_specs=[blk(0), blk(npair), blk(2 * npair), pl.BlockSpec((1, 2, seq), lambda b, p: (p, 0, b)),
                  pl.BlockSpec((1, LANES), lambda b, p: (0, 0)), pl.BlockSpec((1, LANES), lambda b, p: (0, 0))],
        out_specs=[blk(0), blk(0)],
        out_shape=[jax.ShapeDtypeStruct((t, W_GROUP), F32), jax.ShapeDtypeStruct((t, W_GROUP), F32)],
        scratch_shapes=[pltpu.VMEM((seq, LANES), BF16), pltpu.VMEM((2, seq, LANES), BF16),
                        pltpu.VMEM((2, seq, LANES), BF16)],
        inputs=(proj, proj, proj, c3, gq, gk), semantics=("arbitrary", "arbitrary"))


def _fox_bwd(proj, c3, gq, gk, do, o, lse, nb, seq, rider=None):
    t = nb * seq
    tq = FOX_TQ
    nq = seq // tq
    npair = N_FOX_HEADS // 2

    def body(q_ref, k_ref, v_ref, c_ref, gq_ref, gk_ref, do_ref, o_ref, lse_ref,
             dq_ref, dk_ref, dv_ref, dc_ref, dg_ref, qs, ks, vs, kts, dos, lse_t, delta_t, dqt_acc, dk_acc, dv_acc,
             row_sum):
        @pl.when((pl.program_id(0) == 0) & (pl.program_id(1) == 0))
        def _():
            dg_ref[...] = jnp.zeros_like(dg_ref)

        ones = _group_ones()
        masks = _head_masks()
        qhat, rq = _head_norm(q_ref[...].astype(F32), ones)
        khat, rk = _head_norm(k_ref[...].astype(F32), ones)
        qs[...] = (qhat * gq_ref[...] * (SCALE * LOG2E)).astype(BF16)
        kn = khat * gk_ref[...]
        vv = v_ref[...].astype(F32)
        for hd in range(2):
            ks[hd] = (kn * masks[hd]).astype(BF16)
            vs[hd] = (vv * masks[hd]).astype(BF16)
            kts[hd] = ks[hd].T
        dof = do_ref[...]
        dos[...] = dof.astype(BF16)
        lse_t[...] = lse_ref[...].T
        delta_t[...] = _groupsum(dof * o_ref[...], ones).T
        dqt_acc[...] = jnp.zeros_like(dqt_acc)
        dk_acc[...] = jnp.zeros_like(dk_acc)
        dv_acc[...] = jnp.zeros_like(dv_acc)
        row_sum[...] = jnp.zeros_like(row_sum)
        key = lax.broadcasted_iota(jnp.int32, (tq, tq), 0)
        qry = lax.broadcasted_iota(jnp.int32, (tq, tq), 1)
        causal = key <= qry

        for hd in range(2):
            lane0 = hd * HEAD_DIM
            for kj in range(nq):
                k0 = kj * tq
                k_blk = ks[hd, k0:k0 + tq, :]
                v_blk = vs[hd, k0:k0 + tq, :]
                kt_blk = kts[hd, :, k0:k0 + tq]
                crow = c_ref[0, hd:hd + 1, k0:k0 + tq] * LOG2E
                ck0 = crow[:, 0:1]
                bias = jnp.broadcast_to(ck0 - crow, (LANES, tq)).T[:, 0:1]

                def queries_step(r0, r1, diag, hd=hd, lane0=lane0, k_blk=k_blk, v_blk=v_blk, kt_blk=kt_blk,
                                 bias=bias, ck0=ck0):
                    q_r = qs[r0:r1, :]
                    do_r = dos[r0:r1, :]
                    z = _dot(k_blk, q_r, NT) + bias
                    p = jnp.exp2(z - (lse_t[lane0:lane0 + 1, r0:r1] + ck0))
                    if diag:
                        p = jnp.where(causal, p, 0.0)
                    dp = _dot(v_blk, do_r, NT)
                    ds = p * (dp - delta_t[lane0:lane0 + 1, r0:r1])
                    dsb = ds.astype(BF16)
                    dqt_acc[:, r0:r1] += _dot(kt_blk, dsb)
                    row_sum[hd:hd + 1, r0:r1] += jnp.sum(ds, axis=0, keepdims=True)
                    return _dot(dsb, q_r), _dot(p.astype(BF16), do_r), -jnp.sum(ds, axis=1, keepdims=True)

                dk_j, dv_j, dc_j = queries_step(k0, k0 + tq, True)
                if k0 + tq < seq:
                    dk_o, dv_o, dc_o = queries_step(k0 + tq, seq, False)
                    dk_j, dv_j, dc_j = dk_j + dk_o, dv_j + dv_o, dc_j + dc_o
                dk_acc[k0:k0 + tq, :] += dk_j * masks[hd]
                dv_acc[k0:k0 + tq, :] += dv_j * masks[hd]
                dc_ref[0, hd:hd + 1, k0:k0 + tq] = jnp.broadcast_to(dc_j, (tq, LANES)).T[0:1, :]

        dc_ref[0] += row_sum[0:2, :]

        dq_raw, dgq = _head_norm_bwd(dqt_acc[...].T * SCALE, qhat, rq, gq_ref[...], ones)
        dk_raw, dgk = _head_norm_bwd(dk_acc[...] * LN2, khat, rk, gk_ref[...], ones)
        dq_ref[...] = dq_raw.astype(BF16)
        dk_ref[...] = dk_raw.astype(BF16)
        dv_ref[...] = dv_acc[...].astype(BF16)
        dg_ref[0:1, :] += dgq
        dg_ref[1:2, :] += dgk

    blk = lambda off: pl.BlockSpec((seq, LANES), lambda b, p: (b, off + p))
    vec = pl.BlockSpec((1, LANES), lambda b, p: (0, 0))
    c_spec = pl.BlockSpec((1, 2, seq), lambda b, p: (p, 0, b))
    return _host_call(
        body, rider, name="fox_bwd", grid=(nb, npair),
        in_specs=[blk(0), blk(npair), blk(2 * npair), c_spec, vec, vec, blk(0), blk(0), blk(0)],
        out_specs=[blk(0), blk(0), blk(0), c_spec, pl.BlockSpec((8, LANES), lambda b, p: (0, 0))],
        out_shape=[jax.ShapeDtypeStruct((t, W_GROUP), BF16), jax.ShapeDtypeStruct((t, W_GROUP), BF16),
                   jax.ShapeDtypeStruct((t, W_GROUP), BF16), jax.ShapeDtypeStruct((npair, 2, t), F32),
                   jax.ShapeDtypeStruct((8, LANES), F32)],
        scratch_shapes=[pltpu.VMEM((seq, LANES), BF16), pltpu.VMEM((2, seq, LANES), BF16),
                        pltpu.VMEM((2, seq, LANES), BF16), pltpu.VMEM((2, LANES, seq), BF16),
                        pltpu.VMEM((seq, LANES), BF16), pltpu.VMEM((LANES, seq), F32),
                        pltpu.VMEM((LANES, seq), F32), pltpu.VMEM((LANES, seq), F32),
                        pltpu.VMEM((seq, LANES), F32), pltpu.VMEM((seq, LANES), F32),
                        pltpu.VMEM((8, seq), F32)],
        inputs=(proj, proj, proj, c3, gq, gk, do, o, lse), semantics=("arbitrary", "arbitrary"))


def _dil_prep(q_ref, k_ref, gq_ref, gk_ref, cos_ref, up_ref, dn_ref, ones):
    qhat, rq = _head_norm(q_ref[...].astype(F32), ones)
    khat, rk = _head_norm(k_ref[...].astype(F32), ones)
    cos, up, dn = cos_ref[...], up_ref[...], dn_ref[...]
    qn = _rope(qhat * gq_ref[...], cos, up, dn) * (SCALE * LOG2E)
    kn = _rope(khat * gk_ref[...], cos, up, dn)
    return qhat, rq, khat, rk, qn, kn


def _dil_keys(d, seq, pairs):
    nblk = seq // BAND
    per_res = seq // (d * BAND)
    as_blocks = lambda ref, rows: ref[rows, :].reshape(-1, BAND, LANES)
    if per_res == 1:
        a = lax.broadcasted_iota(jnp.int32, (1, BAND, BAND), 1)
        j = lax.broadcasted_iota(jnp.int32, (1, BAND, BAND), 2)
        causal = jnp.where(j <= a, 0.0, NEG)
        return [as_blocks(src, slice(0, seq)) for src, _ in pairs], [causal]
    for src, dst in pairs:
        dst[:, BAND:, :] = as_blocks(src, slice(0, seq))
        dst[1:, :BAND, :] = as_blocks(src, slice(0, seq - BAND))
        dst[0:1, :BAND, :] = jnp.zeros((1, BAND, LANES), BF16)
    a = lax.broadcasted_iota(jnp.int32, (1, BAND, 2 * BAND), 1)
    j = lax.broadcasted_iota(jnp.int32, (1, BAND, 2 * BAND), 2)
    band = jnp.where(((j < BAND) & (j >= a)) | ((j >= BAND) & (j - BAND <= a)), 0.0, NEG)
    e = lax.broadcasted_iota(jnp.int32, (nblk, 1, 2 * BAND), 0)
    j = lax.broadcasted_iota(jnp.int32, (nblk, 1, 2 * BAND), 2)
    no_prev = jnp.where(((e & (per_res - 1)) == 0) & (j < BAND), NEG, 0.0)
    return [dst[...] for _, dst in pairs], [band + no_prev]


def _regroup(d, seq):
    if d == 1:
        return [(slice(0, seq), slice(0, seq))]
    before, n = d // 4, seq // d
    return [(pl.ds(r1 * (seq // before) + r2, n, stride=4), slice((before * r2 + r1) * n, (before * r2 + r1 + 1) * n))
            for r1 in range(before) for r2 in range(4)]


def _dil_fwd(proj, gq, gk, cos, up, dn, nb, seq):
    t = nb * seq
    npair = W_GROUP // LANES
    off = 3 * npair

    def body(q_ref, k_ref, v_ref, gq_ref, gk_ref, cos_ref, up_ref, dn_ref, o_ref, lse_ref,
             src_a, src_b, qp, kp, vp, kw, vw, m_b, l_b, o_b, state_a, state_b):
        ones = _group_ones()
        masks = _head_masks()
        _, _, _, _, qn, kn = _dil_prep(q_ref, k_ref, gq_ref, gk_ref, cos_ref, up_ref, dn_ref, ones)
        src_a[0] = qn
        src_a[1] = kn
        src_a[2] = v_ref[...].astype(F32)
        nblk = seq // BAND
        src, state = (src_a, src_b), (state_a, state_b)

        for d in DILATIONS:
            last = d == DILATIONS[-1]
            for before, after in _regroup(d, seq):
                qv, kv, vv = src[0].at[0][before, :], src[0].at[1][before, :], src[0].at[2][before, :]
                for hd in range(2):
                    qp[hd, after, :] = (qv * masks[hd]).astype(BF16)
                kp[after, :] = kv.astype(BF16)
                vp[after, :] = vv.astype(BF16)
                if d > 1 and not last:
                    src[1][0, after, :], src[1][1, after, :], src[1][2, after, :] = qv, kv, vv
            if d > 1:
                src = src[::-1]
            (keys_k, keys_v), bias = _dil_keys(d, seq, [(kp, kw), (vp, vw)])
            m_t = jnp.zeros((nblk, BAND, LANES), F32)
            l_t = jnp.zeros((nblk, BAND, LANES), F32)
            o_t = jnp.zeros((nblk, BAND, LANES), F32)
            for hd in range(2):
                s = _dot(qp[hd].reshape(nblk, BAND, LANES), keys_k, BATCH_NT)
                for b_ in bias:
                    s = s + b_
                m = jnp.max(s, axis=-1, keepdims=True)
                p = jnp.exp2(s - m)
                m_t = m_t + m * masks[hd]
                l_t = l_t + jnp.sum(p, axis=-1, keepdims=True) * masks[hd]
                o_t = o_t + _dot(p.astype(BF16), keys_v, BATCH_NN) * masks[hd]
            if d == 1:
                state[0][0] = m_t.reshape(seq, LANES)
                state[0][1] = l_t.reshape(seq, LANES)
                state[0][2] = o_t.reshape(seq, LANES)
                continue
            m_b[...] = m_t.reshape(seq, LANES)
            l_b[...] = l_t.reshape(seq, LANES)
            o_b[...] = o_t.reshape(seq, LANES)
            for before, after in _regroup(d, seq):
                m_old = state[0].at[0][before, :]
                m_new = jnp.maximum(m_old, m_b[after, :])
                w_old = jnp.exp2(m_old - m_new)
                w_new = jnp.exp2(m_b[after, :] - m_new)
                state[1][0, after, :] = m_new
                state[1][1, after, :] = state[0].at[1][before, :] * w_old + l_b[after, :] * w_new
                state[1][2, after, :] = state[0].at[2][before, :] * w_old + o_b[after, :] * w_new
            state = state[::-1]

        l = state[0][1]
        o_b[...] = state[0][2] / l
        l_b[...] = state[0][0] + jnp.log2(l)
        held, spare = [o_b, l_b], [m_b, state[1].at[0]]
        for d in DILATIONS[:0:-1]:
            dests = [o_ref, lse_ref] if d == DILATIONS[1] else spare
            for h, dst in zip(held, dests):
                for before, after in _regroup(d, seq):
                    dst[before, :] = h[after, :]
            held, spare = dests, held

    blk = lambda o_: pl.BlockSpec((seq, LANES), lambda b, p: (b, o_ + p))
    vec = pl.BlockSpec((1, LANES), lambda b, p: (0, 0))
    tab = pl.BlockSpec(memory_space=pltpu.VMEM)
    f32_buf = pltpu.VMEM((seq, LANES), F32)
    f32_x3 = pltpu.VMEM((3, seq, LANES), F32)
    bf16_buf = pltpu.VMEM((seq, LANES), BF16)
    window_buf = pltpu.VMEM((seq // BAND, 2 * BAND, LANES), BF16)
    return pl.pallas_call(
        body, name="dil_fwd", grid=(nb, npair),
        in_specs=[blk(off), blk(off + npair), blk(off + 2 * npair), vec, vec, tab, tab, tab],
        out_specs=[blk(0), blk(0)],
        out_shape=[jax.ShapeDtypeStruct((t, W_GROUP), F32), jax.ShapeDtypeStruct((t, W_GROUP), F32)],
        scratch_shapes=[f32_x3, f32_x3, pltpu.VMEM((2, seq, LANES), BF16), bf16_buf, bf16_buf,
                        window_buf, window_buf, f32_buf, f32_buf, f32_buf, f32_x3, f32_x3],
        compiler_params=_params(("arbitrary", "arbitrary")),
    )(proj, proj, proj, gq, gk, cos, up, dn)


def _dil_bwd(proj, gq, gk, cos, up, dn, do, o, lse, nb, seq, rider=None):
    t = nb * seq
    npair = W_GROUP // LANES
    off = 3 * npair

    def body(q_ref, k_ref, v_ref, gq_ref, gk_ref, cos_ref, up_ref, dn_ref, do_ref, o_ref, lse_ref,
             dq_ref, dk_ref, dv_ref, dg_ref, src_a, src_b, sums_a, sums_b,
             qp, kp, vp, dop, kw, vw, lse_p, delta_p, dq_p, dk_p, dv_p):
        @pl.when((pl.program_id(0) == 0) & (pl.program_id(1) == 0))
        def _():
            dg_ref[...] = jnp.zeros_like(dg_ref)

        ones = _group_ones()
        masks = _head_masks()
        qhat, rq, khat, rk, qn, kn = _dil_prep(q_ref, k_ref, gq_ref, gk_ref, cos_ref, up_ref, dn_ref, ones)
        src_a[0] = qn
        src_a[1] = kn
        src_a[2] = v_ref[...].astype(F32)
        src_a[3] = do_ref[...]
        src_a[4] = lse_ref[...]
        src_a[5] = _groupsum(do_ref[...] * o_ref[...], ones)
        nblk = seq // BAND
        src, sums = (src_a, src_b), (sums_a, sums_b)

        for d in DILATIONS:
            last = d == DILATIONS[-1]
            for before, after in _regroup(d, seq):
                planes = [src[0].at[i][before, :] for i in range(6)]
                for hd in range(2):
                    qp[hd, after, :] = (planes[0] * masks[hd]).astype(BF16)
                    dop[hd, after, :] = (planes[3] * masks[hd]).astype(BF16)
                kp[after, :] = planes[1].astype(BF16)
                vp[after, :] = planes[2].astype(BF16)
                lse_p[after, :] = planes[4]
                delta_p[after, :] = planes[5]
                if d > 1 and not last:
                    for i in range(6):
                        src[1][i, after, :] = planes[i]
            if d > 1:
                src = src[::-1]
            (keys_k, keys_v), bias = _dil_keys(d, seq, [(kp, kw), (vp, vw)])
            nk = keys_k.shape[1]
            dq_b = jnp.zeros((nblk, BAND, LANES), F32)
            dk_b = jnp.zeros((nblk, nk, LANES), F32)
            dv_b = jnp.zeros((nblk, nk, LANES), F32)
            for hd in range(2):
                lane0 = hd * HEAD_DIM
                q3 = qp[hd].reshape(nblk, BAND, LANES)
                do3 = dop[hd].reshape(nblk, BAND, LANES)
                z = _dot(q3, keys_k, BATCH_NT)
                for b_ in bias:
                    z = z + b_
                p = jnp.exp2(z - lse_p[...].reshape(nblk, BAND, LANES)[:, :, lane0:lane0 + 1])
                dp = _dot(do3, keys_v, BATCH_NT)
                ds = (p * (dp - delta_p[...].reshape(nblk, BAND, LANES)[:, :, lane0:lane0 + 1])).astype(BF16)
                dq_b = dq_b + _dot(ds, keys_k, BATCH_NN) * masks[hd]
                dk_b = dk_b + _dot(ds, q3, BATCH_TN)
                dv_b = dv_b + _dot(p.astype(BF16), do3, BATCH_TN)
            dq_p[...] = dq_b.reshape(seq, LANES)
            for acc, out in ((dk_b, dk_p), (dv_b, dv_p)):
                out[...] = acc[:, nk - BAND:, :].reshape(seq, LANES)
                if nk > BAND:
                    out[0:seq - BAND, :] += acc[1:, :BAND, :].reshape(seq - BAND, LANES)
            if d == 1:
                sums[0][0], sums[0][1], sums[0][2] = dq_p[...], dk_p[...], dv_p[...]
                continue
            for before, after in _regroup(d, seq):
                for i, part in enumerate((dq_p, dk_p, dv_p)):
                    sums[1][i, after, :] = sums[0].at[i][before, :] + part[after, :]
            sums = sums[::-1]

        for d in DILATIONS[:0:-1]:
            for i in range(3):
                for before, after in _regroup(d, seq):
                    sums[1].at[i][before, :] = sums[0][i, after, :]
            sums = sums[::-1]

        cos, up, dn = cos_ref[...], up_ref[...], dn_ref[...]
        dq_raw, dgq = _head_norm_bwd(_rope_bwd(sums[0][0] * SCALE, cos, up, dn), qhat, rq, gq_ref[...], ones)
        dk_raw, dgk = _head_norm_bwd(_rope_bwd(sums[0][1] * LN2, cos, up, dn), khat, rk, gk_ref[...], ones)
        dq_ref[...] = dq_raw.astype(BF16)
        dk_ref[...] = dk_raw.astype(BF16)
        dv_ref[...] = sums[0][2].astype(BF16)
        dg_ref[0:1, :] += dgq
        dg_ref[1:2, :] += dgk

    blk = lambda o_: pl.BlockSpec((seq, LANES), lambda b, p: (b, o_ + p))
    vec = pl.BlockSpec((1, LANES), lambda b, p: (0, 0))
    tab = pl.BlockSpec(memory_space=pltpu.VMEM)
    f32_buf = pltpu.VMEM((seq, LANES), F32)
    bf16_buf = pltpu.VMEM((seq, LANES), BF16)
    window_buf = pltpu.VMEM((seq // BAND, 2 * BAND, LANES), BF16)
    bf16_pair = pltpu.VMEM((2, seq, LANES), BF16)
    return _host_call(
        body, rider, name="dil_bwd", grid=(nb, npair),
        in_specs=[blk(off), blk(off + npair), blk(off + 2 * npair), vec, vec, tab, tab, tab,
                  blk(0), blk(0), blk(0)],
        out_specs=[blk(0), blk(0), blk(0), pl.BlockSpec((8, LANES), lambda b, p: (0, 0))],
        out_shape=[jax.ShapeDtypeStruct((t, W_GROUP), BF16), jax.ShapeDtypeStruct((t, W_GROUP), BF16),
                   jax.ShapeDtypeStruct((t, W_GROUP), BF16), jax.ShapeDtypeStruct((8, LANES), F32)],
        scratch_shapes=[pltpu.VMEM((6, seq, LANES), F32)] * 2 + [pltpu.VMEM((3, seq, LANES), F32)] * 2
        + [bf16_pair, bf16_buf, bf16_buf, bf16_pair, window_buf, window_buf] + [f32_buf] * 5,
        inputs=(proj, proj, proj, gq, gk, cos, up, dn, do, o, lse), semantics=("arbitrary", "arbitrary"))


def _adamw(w, g, m, v, name, rider=None):
    row_major = w.ndim == 3 and w.shape[1] == 1
    rows, cols = (w.shape[0], w.shape[2]) if row_major else w.shape[-2:]
    if row_major:
        tr = max(t for t in range(1, 65) if rows % t == 0)
    else:
        tr = _row_tile(rows) if rows >= 8 else rows
    c1 = 1.0 - ADAM_B1 ** ADAM_STEP
    c2 = 1.0 - ADAM_B2 ** ADAM_STEP

    def body(w_ref, g_ref, m_ref, v_ref, d_ref, nm_ref, nv_ref):
        g_ = g_ref[...]
        nm = ADAM_B1 * m_ref[...] + (1.0 - ADAM_B1) * g_
        nv = ADAM_B2 * v_ref[...] + (1.0 - ADAM_B2) * (g_ * g_)
        nm_ref[...] = nm
        nv_ref[...] = nv
        d_ref[...] = -ADAM_LR * ((nm / c1) / (jnp.sqrt(nv / c2) + ADAM_EPS) + ADAM_WD * w_ref[...])

    if row_major:
        spec = pl.BlockSpec((tr, 1, cols), lambda i: (i, 0, 0))
    elif w.ndim == 3:
        spec = pl.BlockSpec((1, tr, cols), lambda i: (0, i, 0))
    else:
        spec = pl.BlockSpec((tr, cols), lambda i: (i, 0))
    shape = jax.ShapeDtypeStruct(w.shape, F32)
    return _host_call(
        body, rider, name=name, grid=(rows // tr,), in_specs=[spec] * 4, out_specs=[spec] * 3,
        out_shape=[shape] * 3, scratch_shapes=[], inputs=(w, g, m, v), semantics=("arbitrary",))


def _place():
    x, y, c = lax.axis_index("x"), lax.axis_index("y"), lax.axis_index("c")
    chips = [(1 - x, y), (x, 1 - y), (1 - x, 1 - y)]
    return x, y, c, chips


def _gather_weight(w, name):
    _, rows, cols = w.shape
    half_rows = rows // 2

    def body(w_ref, out_ref, send_sems, recv_sems):
        x, y, c, chips = _place()
        sibling = (x, y, 1 - c)
        mine = 2 * x + y
        lo = pl.multiple_of(c * half_rows, 16)
        lo_sib = pl.multiple_of((1 - c) * half_rows, 16)
        out_ref[mine] = w_ref[0].astype(BF16)

        def copy(k, shard, first_row, to):
            ref = out_ref.at[shard, pl.ds(first_row, half_rows), :]
            return pltpu.make_async_remote_copy(src_ref=ref, dst_ref=ref, send_sem=send_sems.at[k],
                                                recv_sem=recv_sems.at[k], device_id=to, device_id_type=MESH)

        sends = [copy(k, mine, lo, (cx, cy, c)) for k, (cx, cy) in enumerate(chips)]
        for cp in sends:
            cp.start()
        passed = []
        for k, (cx, cy) in enumerate(chips):
            theirs = 2 * cx + cy
            copy(k, theirs, lo, (cx, cy, c)).wait_recv()
            fw = copy(3 + k, theirs, lo, sibling)
            fw.start()
            passed.append(fw)
        for k, (cx, cy) in enumerate(chips):
            copy(3 + k, 2 * cx + cy, lo_sib, sibling).wait_recv()
        for cp in sends + passed:
            cp.wait_send()

    return pl.pallas_call(
        body, name=name,
        in_specs=[pl.BlockSpec(memory_space=pltpu.VMEM)],
        out_specs=pl.BlockSpec(memory_space=pltpu.VMEM),
        out_shape=jax.ShapeDtypeStruct((4, rows, cols), BF16),
        scratch_shapes=[pltpu.SemaphoreType.DMA((6,)), pltpu.SemaphoreType.DMA((6,))],
        compiler_params=pltpu.CompilerParams(vmem_limit_bytes=VMEM_LIMIT),
    )(w)


def _remote(src, dst, sems, k, to):
    send_sems, recv_sems = sems
    return pltpu.make_async_remote_copy(src_ref=src, dst_ref=dst, send_sem=send_sems.at[k], recv_sem=recv_sems.at[k],
                                        device_id=to, device_id_type=MESH)


def _cast_bf16(parts, name):
    def body(*refs):
        for src, dst in zip(refs[:len(parts)], refs[len(parts):]):
            dst[...] = src[0].astype(BF16)

    return pl.pallas_call(
        body, name=name, in_specs=[pl.BlockSpec(memory_space=pltpu.VMEM)] * len(parts),
        out_specs=[pl.BlockSpec(memory_space=pltpu.VMEM)] * len(parts),
        out_shape=[jax.ShapeDtypeStruct(p.shape[1:], BF16) for p in parts],
        compiler_params=pltpu.CompilerParams(vmem_limit_bytes=VMEM_LIMIT),
    )(*parts)


def _gather_rider(shards):
    def copies(ins, outs, sems, which):
        x, y, c, chips = _place()
        sibling = (x, y, 1 - c)
        mine = 2 * x + y
        made = {name: [] for name in which}
        for i, (p_ref, g_ref) in enumerate(zip(ins, outs)):
            half = p_ref.shape[0] // 2
            lo = pl.multiple_of(c * half, 16)
            lo_sib = pl.multiple_of((1 - c) * half, 16)
            spot = lambda shard, first, g_ref=g_ref, half=half: g_ref.at[shard, pl.ds(first, half), :]
            groups = {
                "own": lambda: [pltpu.make_async_copy(p_ref, g_ref.at[mine], sems[0].at[7 * i + 6])],
                "sends": lambda: [_remote(p_ref.at[pl.ds(lo, half), :], spot(mine, lo), sems, 7 * i + k, (cx, cy, c))
                                  for k, (cx, cy) in enumerate(chips)],
                "arrivals": lambda: [_remote(spot(2 * cx + cy, lo), spot(2 * cx + cy, lo), sems, 7 * i + k, (cx, cy, c))
                                     for k, (cx, cy) in enumerate(chips)],
                "passes": lambda: [_remote(spot(2 * cx + cy, lo), spot(2 * cx + cy, lo), sems, 7 * i + 3 + k, sibling)
                                   for k, (cx, cy) in enumerate(chips)],
                "from_sibling": lambda: [_remote(spot(2 * cx + cy, lo_sib), spot(2 * cx + cy, lo_sib), sems,
                                                 7 * i + 3 + k, sibling) for k, (cx, cy) in enumerate(chips)],
            }
            for name in which:
                made[name] += groups[name]()
        return [made[name] for name in which]

    def start(ins, outs, send_sems, recv_sems):
        own, sends = copies(ins, outs, (send_sems, recv_sems), ("own", "sends"))
        for cp in own + sends:
            cp.start()

    def middle(ins, outs, send_sems, recv_sems):
        arrivals, passes = copies(ins, outs, (send_sems, recv_sems), ("arrivals", "passes"))
        for landed, onward in zip(arrivals, passes):
            landed.wait_recv()
            onward.start()

    def finish(ins, outs, send_sems, recv_sems):
        own, sends, passes, from_sibling = copies(ins, outs, (send_sems, recv_sems),
                                                  ("own", "sends", "passes", "from_sibling"))
        for cp in from_sibling:
            cp.wait_recv()
        for cp in sends + passes:
            cp.wait_send()
        for cp in own:
            cp.wait()

    shapes = [jax.ShapeDtypeStruct((4,) + s.shape, BF16) for s in shards]
    return _Rider(shards, shapes, 7 * len(shards), start, finish, middle=middle)


def _exchange_rider(inputs, out_shapes, n_sems, copies, aliases=None):
    def start(ins, outs, send_sems, recv_sems):
        for cp in copies(ins, outs, (send_sems, recv_sems)):
            cp.start()

    def finish(ins, outs, send_sems, recv_sems):
        for cp in copies(ins, outs, (send_sems, recv_sems)):
            cp.wait()

    return _Rider(inputs, out_shapes, n_sems, start, finish, aliases)


def _swap_rider(grads4):
    halves = [g.shape[1] // 2 for g in grads4]

    def copies(ins, outs, sems):
        x, y, c, _ = _place()
        return [_remote(g.at[:, pl.ds(pl.multiple_of((1 - c) * h, 8), h), :], a, sems, i, (x, y, 1 - c))
                for i, (g, a, h) in enumerate(zip(ins, outs, halves))]

    shapes = [jax.ShapeDtypeStruct((4, h, g.shape[2]), F32) for g, h in zip(grads4, halves)]
    return _exchange_rider(grads4, shapes, len(grads4), copies)


def _chip_sum(g4, from_sibling, name):
    _, rows, cols = g4.shape
    half = rows // 2

    def body(g_ref, s_ref, stage_ref, own_ref):
        x, y, c, chips = _place()
        lo = pl.multiple_of(c * half, 8)
        for k, (cx, cy) in enumerate(chips):
            theirs = 2 * cx + cy
            stage_ref[k] = (g_ref[theirs, pl.ds(lo, half), :] + s_ref[theirs]).astype(BF16)
        mine = 2 * x + y
        own_ref[...] = g_ref[mine, pl.ds(lo, half), :] + s_ref[mine]

    return pl.pallas_call(
        body, name=name, in_specs=[pl.BlockSpec(memory_space=pltpu.VMEM)] * 2,
        out_specs=[pl.BlockSpec(memory_space=pltpu.VMEM)] * 2,
        out_shape=[jax.ShapeDtypeStruct((3, half, cols), BF16), jax.ShapeDtypeStruct((half, cols), F32)],
        compiler_params=pltpu.CompilerParams(vmem_limit_bytes=VMEM_LIMIT),
    )(g4, from_sibling)


def _spread_rider(stages):
    def copies(ins, outs, sems):
        _, _, c, chips = _place()
        return [_remote(st.at[k], ld.at[k], sems, 3 * i + k, (cx, cy, c))
                for i, (st, ld) in enumerate(zip(ins, outs)) for k, (cx, cy) in enumerate(chips)]

    shapes = [jax.ShapeDtypeStruct(s.shape, s.dtype) for s in stages]
    return _exchange_rider(stages, shapes, 3 * len(stages), copies)


def _finish_half(own, landed, name):
    half, cols = own.shape

    def body(own_ref, landed_ref, out_ref):
        c = lax.axis_index("c")
        acc = own_ref[...]
        for k in range(3):
            acc = acc + landed_ref[k].astype(F32)
        out_ref[pl.ds(pl.multiple_of(c * half, 8), half), :] = acc

    return pl.pallas_call(
        body, name=name, in_specs=[pl.BlockSpec(memory_space=pltpu.VMEM)] * 2,
        out_specs=pl.BlockSpec(memory_space=pltpu.VMEM),
        out_shape=jax.ShapeDtypeStruct((2 * half, cols), F32),
        compiler_params=pltpu.CompilerParams(vmem_limit_bytes=VMEM_LIMIT),
    )(own, landed)


def _share_rider(fulls):
    def copies(ins, outs, sems):
        x, y, c, _ = _place()
        out = []
        for i, full in enumerate(outs):
            half = full.shape[0] // 2
            rows = full.at[pl.ds(pl.multiple_of(c * half, 8), half), :]
            out.append(_remote(rows, rows, sems, i, (x, y, 1 - c)))
        return out

    def finish_copies(ins, outs, sems):
        x, y, c, _ = _place()
        out = []
        for i, full in enumerate(outs):
            half = full.shape[0] // 2
            mine = full.at[pl.ds(pl.multiple_of(c * half, 8), half), :]
            theirs = full.at[pl.ds(pl.multiple_of((1 - c) * half, 8), half), :]
            out.append((_remote(mine, mine, sems, i, (x, y, 1 - c)), _remote(theirs, theirs, sems, i, (x, y, 1 - c))))
        return out

    def start(ins, outs, send_sems, recv_sems):
        for cp in copies(ins, outs, (send_sems, recv_sems)):
            cp.start()

    def finish(ins, outs, send_sems, recv_sems):
        for sent, landed in finish_copies(ins, outs, (send_sems, recv_sems)):
            sent.wait_send()
            landed.wait_recv()

    shapes = [jax.ShapeDtypeStruct(f.shape, f.dtype) for f in fulls]
    return _Rider(fulls, shapes, len(fulls), start, finish, aliases={i: i for i in range(len(fulls))})


def _all_sum_small(v):
    shape = v.shape

    def body(v_ref, out_ref, buf, send_sems, recv_sems):
        x, y, c, _ = _place()
        me = 4 * x + 2 * y + c
        buf[me] = v_ref[...]
        flips = [(dx, dy, dc) for dx in (0, 1) for dy in (0, 1) for dc in (0, 1)][1:]

        def copy(k, slot, flip):
            dx, dy, dc = flip
            to = (1 - x if dx else x, 1 - y if dy else y, 1 - c if dc else c)
            return pltpu.make_async_remote_copy(src_ref=buf.at[slot], dst_ref=buf.at[slot], send_sem=send_sems.at[k],
                                                recv_sem=recv_sems.at[k], device_id=to, device_id_type=MESH)

        sends = [copy(k, me, flip) for k, flip in enumerate(flips)]
        for cp in sends:
            cp.start()
        for k, (dx, dy, dc) in enumerate(flips):
            sender = 4 * (1 - x if dx else x) + 2 * (1 - y if dy else y) + (1 - c if dc else c)
            copy(k, sender, (dx, dy, dc)).wait_recv()
        for cp in sends:
            cp.wait_send()
        total = buf[0]
        for i in range(1, 8):
            total = total + buf[i]
        out_ref[...] = total

    return pl.pallas_call(
        body, name="all_sum_small",
        in_specs=[pl.BlockSpec(memory_space=pltpu.VMEM)],
        out_specs=pl.BlockSpec(memory_space=pltpu.VMEM),
        out_shape=jax.ShapeDtypeStruct(shape, F32),
        scratch_shapes=[pltpu.VMEM((8,) + shape, F32), pltpu.SemaphoreType.DMA((7,)), pltpu.SemaphoreType.DMA((7,))],
    )(v)


SMALL = (("g_mix", 1024), ("g_ffn", 1024), ("g_out_fox", 512), ("g_out_dil", 512), ("g_q_fox", 64),
         ("g_k_fox", 64), ("g_q_dil", 64), ("g_k_dil", 64), ("b_forget", 8))
SMALL_PACKED = (32, LANES)


def _local_grads(x, target, gains, w1, wft, dense, packed, nb, seq):
    tile2 = lambda g: jnp.tile(g, (1, 2))
    gq_f, gk_f, gq_d, gk_d = (tile2(gains[n]) for n in ("g_q_fox", "g_k_fox", "g_q_dil", "g_k_dil"))
    b_col = gains["b_forget"].reshape(N_FOX_HEADS, 1)
    cos, up, dn = _rope_tables(seq)
    npair = N_FOX_HEADS // 2

    proj, fa_row, h1, h1_t = _in_proj(x, gains["g_mix"], w1, wft)
    c_row = _gate_fwd(fa_row, b_col, seq)
    c3 = c_row.reshape(npair, 2, nb * seq)
    (o_fox, lse_fox), gathered = _fox_fwd(proj, c3, gq_f, gk_f, nb, seq,
                                          rider=None if packed is None else _gather_rider(packed))
    if packed is not None:
        dense = [g.reshape(-1, g.shape[2]) for g in gathered]
    w_out, w_gate, w_up, w_down = dense
    o_dil, lse_dil = _dil_fwd(proj, gq_d, gk_d, cos, up, dn, nb, seq)
    x1, o_n_t = _attn_out(o_fox, o_dil, x, gains["g_out_fox"], gains["g_out_dil"], w_out)
    a, u, dy, loss_parts = _ffn_fwd(x1, target, gains["g_ffn"], w_gate, w_up, w_down)
    loss = jnp.sum(loss_parts[:, 0, 0])

    dx1, s, da, du, h2, dg_ffn = _ffn_bwd(dy, a, u, x1, gains["g_ffn"], w_gate, w_up, w_down)
    d_w_down = _token_matmul(s, dy, "dw_down", 512, False)
    d_w_gate = _token_matmul(da, h2, "dw_gate", 512, False)
    d_w_up = _token_matmul(du, h2, "dw_up", 512, False)
    d_w_out = _token_matmul(o_n_t, dx1, "dw_out", 1024)
    names = ("w_out", "w_gate", "w_up", "w_down")
    grads4 = [g.reshape(4, -1, g.shape[1]) for g in (d_w_out, d_w_gate, d_w_up, d_w_down)]
    exchange = packed is not None
    (do_fox, do_dil, dg_of, dg_od), from_sibling = _attn_out_bwd(
        dx1, o_fox, o_dil, gains["g_out_fox"], gains["g_out_dil"], w_out,
        rider=_swap_rider(grads4) if exchange else None)
    if exchange:
        sums = [_chip_sum(g, s, "chip_sum_" + n) for g, s, n in zip(grads4, from_sibling, names)]
    (dq_f, dk_f, dv_f, dc3, dg_fox), landed = _fox_bwd(
        proj, c3, gq_f, gk_f, do_fox, o_fox, lse_fox, nb, seq,
        rider=_spread_rider([st for st, _ in sums]) if exchange else None)
    if exchange:
        halves = [_finish_half(own, ld, "finish_half_" + n) for (_, own), ld, n in zip(sums, landed, names)]
    (dq_d, dk_d, dv_d, dg_dil), reduced = _dil_bwd(
        proj, gq_d, gk_d, cos, up, dn, do_dil, o_dil, lse_dil, nb, seq,
        rider=_share_rider(halves) if exchange else None)
    if exchange:
        d_w_out, d_w_gate, d_w_up, d_w_down = reduced
    dfa_row, db = _gate_bwd(dc3.reshape(N_FOX_HEADS, nb * seq), fa_row, b_col, seq)
    dparts = [dq_f, dk_f, dv_f, dq_d, dk_d, dv_d]
    d_w1 = _token_matmul_parts(h1_t, dparts, "dw_in")
    d_wf = _row_matmul(dfa_row, h1, "dw_forget")
    fox_w = 3 * W_GROUP
    in_order = [(d_w1[:, :fox_w], fox_w), (d_wf.T, N_FOX_HEADS), (d_w1[:, fox_w:], d_w1.shape[1] - fox_w)]
    n_cols = d_w1.shape[1] + N_FOX_HEADS
    if exchange:
        shards = [jnp.stack([_pick_columns(in_order, s * n_cols // 4, (s + 1) * n_cols // 4) for s in range(4)])]
        _, from_sibling = _idle_host(_swap_rider(shards), "swap_w_in")
        stage, own = _chip_sum(shards[0], from_sibling[0], "chip_sum_w_in")
    (grad_x, dg_mix), landed = _in_proj_bwd(dparts, dfa_row, w1, wft, x, gains["g_mix"], dx1,
                                            rider=_spread_rider([stage]) if exchange else None)
    if exchange:
        d_w_in = _finish_half(own, landed[0], "finish_half_w_in")
        grad_x, = _hand_over([grad_x], "hand_over_grad_x")
    else:
        d_w_in = _pick_columns(in_order, 0, n_cols)

    fold = lambda g2: (g2[:, :HEAD_DIM] + g2[:, HEAD_DIM:])
    small = {
        "g_mix": dg_mix[0:1], "g_ffn": dg_ffn[0:1], "g_out_fox": dg_of[0:1], "g_out_dil": dg_od[0:1],
        "g_q_fox": fold(dg_fox[0:1]), "g_k_fox": fold(dg_fox[1:2]),
        "g_q_dil": fold(dg_dil[0:1]), "g_k_dil": fold(dg_dil[1:2]),
        "b_forget": db[:, 0].reshape(1, N_FOX_HEADS),
    }
    big = {"w_in": d_w_in, "w_out": d_w_out, "w_gate": d_w_gate, "w_up": d_w_up, "w_down": d_w_down}
    return loss, grad_x, big, small


def _pick_columns(pieces, lo, hi):
    out, first = [], 0
    for a, w in pieces:
        a_lo, a_hi = max(lo, first), min(hi, first + w)
        if a_lo < a_hi:
            out.append(a[:, a_lo - first:a_hi - first])
        first += w
    return out[0] if len(out) == 1 else jnp.concatenate(out, axis=1)


def kernel(x, g_mix, w_in, b_forget, g_q_fox, g_k_fox, g_q_dil, g_k_dil, g_out_fox, g_out_dil, w_out, g_ffn, w_gate, w_up, w_down, loss_target, m_g_mix, m_w_in, m_b_forget, m_g_q_fox, m_g_k_fox, m_g_q_dil, m_g_k_dil, m_g_out_fox, m_g_out_dil, m_w_out, m_g_ffn, m_w_gate, m_w_up, m_w_down, v_g_mix, v_w_in, v_b_forget, v_g_q_fox, v_g_k_fox, v_g_q_dil, v_g_k_dil, v_g_out_fox, v_g_out_dil, v_w_out, v_g_ffn, v_w_gate, v_w_up, v_w_down):
    nb, seq, d = x.shape
    weights = dict(g_mix=g_mix, w_in=w_in, b_forget=b_forget, g_q_fox=g_q_fox, g_k_fox=g_k_fox, g_q_dil=g_q_dil,
                   g_k_dil=g_k_dil, g_out_fox=g_out_fox, g_out_dil=g_out_dil, w_out=w_out, g_ffn=g_ffn,
                   w_gate=w_gate, w_up=w_up, w_down=w_down)
    m_in = dict(g_mix=m_g_mix, w_in=m_w_in, b_forget=m_b_forget, g_q_fox=m_g_q_fox, g_k_fox=m_g_k_fox,
                g_q_dil=m_g_q_dil, g_k_dil=m_g_k_dil, g_out_fox=m_g_out_fox, g_out_dil=m_g_out_dil, w_out=m_w_out,
                g_ffn=m_g_ffn, w_gate=m_w_gate, w_up=m_w_up, w_down=m_w_down)
    v_in = dict(g_mix=v_g_mix, w_in=v_w_in, b_forget=v_b_forget, g_q_fox=v_g_q_fox, g_k_fox=v_g_k_fox,
                g_q_dil=v_g_q_dil, g_k_dil=v_g_k_dil, g_out_fox=v_g_out_fox, g_out_dil=v_g_out_dil, w_out=v_w_out,
                g_ffn=v_g_ffn, w_gate=v_w_gate, w_up=v_w_up, w_down=v_w_down)
    order = ["g_mix", "w_in", "b_forget", "g_q_fox", "g_k_fox", "g_q_dil", "g_k_dil", "g_out_fox", "g_out_dil",
             "w_out", "g_ffn", "w_gate", "w_up", "w_down"]

    w_in_all = _gather_weight(w_in, "gather_w_in")
    in_shards = [(w_in_all[s], w_in_all.shape[2]) for s in range(4)]
    fox_w = 3 * W_GROUP
    n_cols = 4 * w_in_all.shape[2]
    w1 = jnp.concatenate([_pick_columns(in_shards, 0, fox_w), _pick_columns(in_shards, fox_w + N_FOX_HEADS, n_cols)],
                         axis=1)
    wft = _pick_columns(in_shards, fox_w, fox_w + N_FOX_HEADS).T
    swap = lambda a: jnp.transpose(a, (0, 2, 1))
    for n in ("w_gate", "w_up"):
        weights[n], m_in[n], v_in[n] = swap(weights[n]), swap(m_in[n]), swap(v_in[n])
    shards = _cast_bf16([weights[n] for n in ("w_out", "w_gate", "w_up", "w_down")], "cast_shards")

    gains = {n: weights[n] for n, _ in SMALL}
    loss, grad_x, big, small = _local_grads(
        x.reshape(nb * seq, d), loss_target.reshape(nb * seq, d), gains, w1, wft, None, shards, nb, seq)

    grads = {n: big[n][None] for n in ("w_out", "w_gate", "w_up", "w_down")}
    packed = jnp.concatenate([small[n].reshape(-1) for n, _ in SMALL] + [loss.reshape(1)])
    packed = jnp.pad(packed, (0, SMALL_PACKED[0] * SMALL_PACKED[1] - packed.shape[0])).reshape(SMALL_PACKED)
    summed = _all_sum_small(packed).reshape(-1)
    pos = 0
    for n, size in SMALL:
        grads[n] = summed[pos:pos + size].reshape(1, size)
        pos += size
    loss = summed[pos]

    to_entry = lambda a: jnp.transpose(a, (2, 0, 1))
    deltas, new_m, new_v, grad_out = {}, {}, {}, {}
    for n in ["w_down"] + [n for n in order if n != "w_down"]:
        rider = _share_rider([big["w_in"]]) if n == "w_down" else None
        (deltas[n], new_m[n], new_v[n]), shared = _adamw(weights[n], grads[n], m_in[n], v_in[n], "adamw_" + n, rider)
        if rider is not None:
            deltas[n], new_m[n], new_v[n] = _hand_over([deltas[n], new_m[n], new_v[n]], "hand_over_" + n)
            grads["w_in"] = to_entry(shared[0][None])
            weights["w_in"], m_in["w_in"], v_in["w_in"] = (to_entry(a) for a in (w_in, m_w_in, v_w_in))
        grad_out[n] = grads[n]
    for n in ("w_gate", "w_up"):
        grad_out[n], deltas[n], new_m[n], new_v[n] = (swap(a) for a in (grad_out[n], deltas[n], new_m[n], new_v[n]))
    from_entry = lambda a: jnp.transpose(a, (1, 2, 0))
    grad_out["w_in"], deltas["w_in"], new_m["w_in"], new_v["w_in"] = (
        from_entry(a) for a in (grad_out["w_in"], deltas["w_in"], new_m["w_in"], new_v["w_in"]))

    return (loss, grad_x.reshape(nb, seq, d), *[grad_out[n] for n in order], *[deltas[n] for n in order],
            *[new_m[n] for n in order], *[new_v[n] for n in order])
```

```python
import functools
import math

import jax
import jax.numpy as jnp
from jax import lax
from jax.experimental import pallas as pl
from jax.experimental.pallas import tpu as pltpu

F32, BF16 = jnp.float32, jnp.bfloat16
MESH = pl.DeviceIdType.MESH

EPS = 1e-6
NEG = -1e30
HEAD_DIM = 64
SCALE = HEAD_DIM ** -0.5
LOG2E = math.log2(math.e)
LN2 = math.log(2.0)
ROPE_THETA = 500000.0
ROPE_DIM = HEAD_DIM // 4
LANES = 128
W_GROUP = 512
N_FOX_HEADS = 8
VMEM_LIMIT = 56 * 1024 * 1024
DILATIONS = (1, 4, 16)
BAND = 128

ADAM_LR, ADAM_B1, ADAM_B2, ADAM_EPS, ADAM_WD, ADAM_STEP = 0.001, 0.9, 0.999, 1e-08, 0.01, 10

NT = (((1,), (1,)), ((), ()))
TN = (((0,), (0,)), ((), ()))
BATCH_NT = (((2,), (2,)), ((0,), (0,)))
BATCH_NN = (((2,), (1,)), ((0,), (0,)))
BATCH_TN = (((1,), (1,)), ((0,), (0,)))


def _params(sem=None):
    return pltpu.CompilerParams(dimension_semantics=sem, vmem_limit_bytes=VMEM_LIMIT)


def _dot(a, b, dims=None):
    if dims is None:
        return jnp.dot(a, b, preferred_element_type=F32)
    return lax.dot_general(a, b, dims, preferred_element_type=F32)


def _group_ones():
    i = lax.broadcasted_iota(jnp.int32, (LANES, LANES), 0) >> 6
    j = lax.broadcasted_iota(jnp.int32, (LANES, LANES), 1) >> 6
    return (i == j).astype(BF16)


def _split3(x):
    a = x.astype(BF16)
    r = x - a.astype(F32)
    b = r.astype(BF16)
    c = (r - b.astype(F32)).astype(BF16)
    return a, b, c


def _groupsum(x, ones, pieces=2):
    total = None
    for _ in range(pieces):
        piece = x.astype(BF16)
        part = _dot(piece, ones)
        total = part if total is None else total + part
        x = x - piece.astype(F32)
    return total


def _head_masks():
    lane = lax.broadcasted_iota(jnp.int32, (1, LANES), 1)
    return [(lane < HEAD_DIM).astype(F32), (lane >= HEAD_DIM).astype(F32)]


def _head_norm(raw, ones):
    r = lax.rsqrt(_groupsum(raw * raw, ones, 1) * (1.0 / HEAD_DIM) + EPS)
    return raw * r, r


def _head_norm_bwd(dy, xhat, r, gain, ones):
    u = dy * gain
    dgain = jnp.sum(dy * xhat, axis=0, keepdims=True)
    draw = r * (u - xhat * (_groupsum(u * xhat, ones) * (1.0 / HEAD_DIM)))
    return draw, dgain


def _rope(x, cos, s_up, s_dn):
    return x * cos + pltpu.roll(x, LANES - 8, 1) * s_up + pltpu.roll(x, 8, 1) * s_dn


def _rope_bwd(dy, cos, s_up, s_dn):
    return dy * cos + pltpu.roll(dy * s_up, 8, 1) + pltpu.roll(dy * s_dn, LANES - 8, 1)


def _rope_tables(seq):
    half = ROPE_DIM // 2
    inv_freq = jnp.power(jnp.float32(ROPE_THETA), -jnp.arange(half, dtype=F32) * 2.0 / ROPE_DIM)
    ang = jnp.arange(seq).astype(F32)[:, None] * inv_freq[None, :]
    cos, sin = jnp.cos(ang), jnp.sin(ang)
    one = jnp.ones((seq, HEAD_DIM - ROPE_DIM), F32)
    zero_h = jnp.zeros((seq, half), F32)
    zero_r = jnp.zeros((seq, HEAD_DIM - ROPE_DIM), F32)
    c = jnp.concatenate([cos, cos, one], axis=1)
    up = jnp.concatenate([-sin, zero_h, zero_r], axis=1)
    dn = jnp.concatenate([zero_h, sin, zero_r], axis=1)
    return jnp.tile(c, (1, 2)), jnp.tile(up, (1, 2)), jnp.tile(dn, (1, 2))


def _row_tile(rows, cap=256):
    best = rows
    for t in range(8, min(rows, cap) + 1, 8):
        if rows % t == 0:
            best = t
    return best


class _Rider:
    def __init__(self, inputs, out_shapes, n_sems, start, finish, aliases=None, middle=None):
        self.inputs, self.out_shapes, self.n_sems = list(inputs), list(out_shapes), n_sems
        self.start, self.finish, self.middle, self.aliases = start, finish, middle, dict(aliases or {})


def _host_call(body, rider, *, name, grid, in_specs, out_specs, out_shape, scratch_shapes, inputs, semantics):
    if rider is None:
        return pl.pallas_call(body, name=name, grid=grid, in_specs=in_specs, out_specs=out_specs,
                              out_shape=out_shape, scratch_shapes=scratch_shapes,
                              compiler_params=_params(semantics))(*inputs), []
    n_in, n_out, n_scr = len(in_specs), len(out_specs), len(scratch_shapes)
    r_in, r_out = len(rider.inputs), len(rider.out_shapes)

    def wrapped(*refs):
        ins, refs = refs[:n_in], refs[n_in:]
        r_ins, refs = refs[:r_in], refs[r_in:]
        outs, refs = refs[:n_out], refs[n_out:]
        r_outs, refs = refs[:r_out], refs[r_out:]
        scratch, (send_sems, recv_sems) = refs[:n_scr], refs[n_scr:]
        ids = [pl.program_id(a) for a in range(len(grid))]
        first = functools.reduce(lambda p, q: p & q, [i == 0 for i in ids])
        last = functools.reduce(lambda p, q: p & q, [i == g - 1 for i, g in zip(ids, grid)])

        @pl.when(first)
        def _():
            rider.start(r_ins, r_outs, send_sems, recv_sems)

        body(*ins, *outs, *scratch)

        if rider.middle is not None:
            step, steps = ids[0], grid[0]
            for i, g in zip(ids[1:], grid[1:]):
                step, steps = step * g + i, steps * g

            @pl.when(step == (3 * steps) // 4)
            def _():
                rider.middle(r_ins, r_outs, send_sems, recv_sems)

        @pl.when(last)
        def _():
            rider.finish(r_ins, r_outs, send_sems, recv_sems)

    hbm = pl.BlockSpec(memory_space=pl.ANY)
    res = pl.pallas_call(
        wrapped, name=name, grid=grid,
        in_specs=list(in_specs) + [hbm] * r_in, out_specs=list(out_specs) + [hbm] * r_out,
        out_shape=list(out_shape) + rider.out_shapes,
        scratch_shapes=list(scratch_shapes) + [pltpu.SemaphoreType.DMA((rider.n_sems,))] * 2,
        input_output_aliases={n_in + i: n_out + o for i, o in rider.aliases.items()},
        compiler_params=_params(semantics),
    )(*inputs, *rider.inputs)
    return res[:n_out], res[n_out:]


def _idle_host(rider, name):
    def body(o_ref):
        o_ref[...] = jnp.zeros_like(o_ref)

    return _host_call(body, rider, name=name, grid=(1,), in_specs=[],
                      out_specs=[pl.BlockSpec((8, LANES), lambda i: (0, 0))],
                      out_shape=[jax.ShapeDtypeStruct((8, LANES), F32)], scratch_shapes=[], inputs=(),
                      semantics=("arbitrary",))


def _in_proj(x, g_mix, w1, wft):
    t, d = x.shape
    n = w1.shape[1]
    tt = 512

    def body(x_ref, g_ref, w_ref, wf_ref, p_ref, fa_ref, h_ref, ht_ref):
        xx = x_ref[...]
        r = lax.rsqrt(jnp.mean(xx * xx, axis=-1, keepdims=True) + EPS)
        h = (xx * r * g_ref[...]).astype(BF16)
        h_ref[...] = h
        ht_ref[...] = h.T
        for j in range(n // W_GROUP):
            cols = slice(j * W_GROUP, (j + 1) * W_GROUP)
            p_ref[:, cols] = _dot(h, w_ref[:, cols]).astype(BF16)
        fa_ref[...] = _dot(wf_ref[...], h, NT)

    return pl.pallas_call(
        body, name="in_proj", grid=(t // tt,),
        in_specs=[pl.BlockSpec((tt, d), lambda i: (i, 0)), pl.BlockSpec((1, d), lambda i: (0, 0)),
                  pl.BlockSpec(memory_space=pltpu.VMEM), pl.BlockSpec(memory_space=pltpu.VMEM)],
        out_specs=[pl.BlockSpec((tt, n), lambda i: (i, 0)), pl.BlockSpec((8, tt), lambda i: (0, i)),
                   pl.BlockSpec((tt, d), lambda i: (i, 0)), pl.BlockSpec((d, tt), lambda i: (0, i))],
        out_shape=[jax.ShapeDtypeStruct((t, n), BF16), jax.ShapeDtypeStruct((8, t), F32),
                   jax.ShapeDtypeStruct((t, d), BF16), jax.ShapeDtypeStruct((d, t), BF16)],
        compiler_params=_params(("arbitrary",)),
    )(x, g_mix, w1, wft)


def _tri(n, upper):
    i = lax.broadcasted_iota(jnp.int32, (n, n), 0)
    j = lax.broadcasted_iota(jnp.int32, (n, n), 1)
    return ((i <= j) if upper else (i >= j)).astype(BF16)


def _gate_fwd(fa_row, b_col, seq):
    t = fa_row.shape[1]
    cb = 256

    def body(fa_ref, b_ref, c_ref):
        tri = _tri(cb, True)
        carry = jnp.zeros((8, 1), F32)
        for k in range(seq // cb):
            z = fa_ref[:, k * cb:(k + 1) * cb] + b_ref[...]
            lf = jnp.minimum(z, 0.0) - jnp.log(1.0 + jnp.exp(-jnp.abs(z)))
            a, b, c = _split3(lf)
            blk = _dot(a, tri) + _dot(b, tri) + _dot(c, tri) + carry
            c_ref[:, k * cb:(k + 1) * cb] = blk
            carry = blk[:, cb - 1:cb]

    return pl.pallas_call(
        body, name="gate_fwd", grid=(t // seq,),
        in_specs=[pl.BlockSpec((8, seq), lambda i: (0, i)), pl.BlockSpec((8, 1), lambda i: (0, 0))],
        out_specs=pl.BlockSpec((8, seq), lambda i: (0, i)),
        out_shape=jax.ShapeDtypeStruct((8, t), F32),
        compiler_params=_params(("arbitrary",)),
    )(fa_row, b_col)


def _gate_bwd(dc_row, fa_row, b_col, seq):
    t = fa_row.shape[1]
    cb = 256

    def body(dc_ref, fa_ref, b_ref, dfa_ref, db_ref):
        @pl.when(pl.program_id(0) == 0)
        def _():
            db_ref[...] = jnp.zeros_like(db_ref)

        tri = _tri(cb, False)
        carry = jnp.zeros((8, 1), F32)
        dbs = jnp.zeros((8, 1), F32)
        for k in reversed(range(seq // cb)):
            a, b, c = _split3(dc_ref[:, k * cb:(k + 1) * cb])
            dlf = _dot(a, tri) + _dot(b, tri) + _dot(c, tri) + carry
            carry = dlf[:, 0:1]
            z = fa_ref[:, k * cb:(k + 1) * cb] + b_ref[...]
            dfa = dlf / (1.0 + jnp.exp(z))
            dfa_ref[:, k * cb:(k + 1) * cb] = dfa
            dbs = dbs + jnp.sum(dfa, axis=1, keepdims=True)
        db_ref[...] += jnp.broadcast_to(dbs, (8, LANES))

    return pl.pallas_call(
        body, name="gate_bwd", grid=(t // seq,),
        in_specs=[pl.BlockSpec((8, seq), lambda i: (0, i)), pl.BlockSpec((8, seq), lambda i: (0, i)),
                  pl.BlockSpec((8, 1), lambda i: (0, 0))],
        out_specs=[pl.BlockSpec((8, seq), lambda i: (0, i)), pl.BlockSpec((8, LANES), lambda i: (0, 0))],
        out_shape=[jax.ShapeDtypeStruct((8, t), F32), jax.ShapeDtypeStruct((8, LANES), F32)],
        compiler_params=_params(("arbitrary",)),
    )(dc_row, fa_row, b_col)


def _attn_out(o_fox, o_dil, x, g_fox, g_dil, w_out):
    t, d = x.shape
    w = o_fox.shape[1]
    tt = 512

    def body(of_ref, od_ref, x_ref, gf_ref, gd_ref, w_ref, x1_ref, ont_ref):
        acc = x_ref[...]
        for k, (o_ref, g_ref) in enumerate(((of_ref, gf_ref), (od_ref, gd_ref))):
            o = o_ref[...]
            r = lax.rsqrt(jnp.mean(o * o, axis=-1, keepdims=True) + EPS)
            on = (o * r * g_ref[...]).astype(BF16)
            ont_ref[k * w:(k + 1) * w, :] = on.T
            acc = acc + _dot(on, w_ref[k * w:(k + 1) * w, :])
        x1_ref[...] = acc

    return pl.pallas_call(
        body, name="attn_out", grid=(t // tt,),
        in_specs=[pl.BlockSpec((tt, w), lambda i: (i, 0)), pl.BlockSpec((tt, w), lambda i: (i, 0)),
                  pl.BlockSpec((tt, d), lambda i: (i, 0)), pl.BlockSpec((1, w), lambda i: (0, 0)),
                  pl.BlockSpec((1, w), lambda i: (0, 0)), pl.BlockSpec(memory_space=pltpu.VMEM)],
        out_specs=[pl.BlockSpec((tt, d), lambda i: (i, 0)), pl.BlockSpec((2 * w, tt), lambda i: (0, i))],
        out_shape=[jax.ShapeDtypeStruct((t, d), F32), jax.ShapeDtypeStruct((2 * w, t), BF16)],
        compiler_params=_params(("arbitrary",)),
    )(o_fox, o_dil, x, g_fox, g_dil, w_out)


def _attn_out_bwd(dx1, o_fox, o_dil, g_fox, g_dil, w_out, rider=None):
    t, d = dx1.shape
    w = o_fox.shape[1]
    tt = 512

    def body(dx_ref, of_ref, od_ref, gf_ref, gd_ref, w_ref, dof_ref, dod_ref, dgf_ref, dgd_ref):
        @pl.when(pl.program_id(0) == 0)
        def _():
            dgf_ref[...] = jnp.zeros_like(dgf_ref)
            dgd_ref[...] = jnp.zeros_like(dgd_ref)

        dxb = dx_ref[...].astype(BF16)
        for k, (o_ref, g_ref, do_ref, dg_ref) in enumerate(
                ((of_ref, gf_ref, dof_ref, dgf_ref), (od_ref, gd_ref, dod_ref, dgd_ref))):
            don = _dot(dxb, w_ref[k * w:(k + 1) * w, :], NT)
            o = o_ref[...]
            r = lax.rsqrt(jnp.mean(o * o, axis=-1, keepdims=True) + EPS)
            xhat = o * r
            u = don * g_ref[...]
            do_ref[...] = r * (u - xhat * jnp.mean(u * xhat, axis=-1, keepdims=True))
            dg_ref[0:1, :] += jnp.sum(don * xhat, axis=0, keepdims=True)

    return _host_call(
        body, rider, name="attn_out_bwd", grid=(t // tt,),
        in_specs=[pl.BlockSpec((tt, d), lambda i: (i, 0)), pl.BlockSpec((tt, w), lambda i: (i, 0)),
                  pl.BlockSpec((tt, w), lambda i: (i, 0)), pl.BlockSpec((1, w), lambda i: (0, 0)),
                  pl.BlockSpec((1, w), lambda i: (0, 0)), pl.BlockSpec(memory_space=pltpu.VMEM)],
        out_specs=[pl.BlockSpec((tt, w), lambda i: (i, 0)), pl.BlockSpec((tt, w), lambda i: (i, 0)),
                   pl.BlockSpec((8, w), lambda i: (0, 0)), pl.BlockSpec((8, w), lambda i: (0, 0))],
        out_shape=[jax.ShapeDtypeStruct((t, w), F32), jax.ShapeDtypeStruct((t, w), F32),
                   jax.ShapeDtypeStruct((8, w), F32), jax.ShapeDtypeStruct((8, w), F32)],
        scratch_shapes=[], inputs=(dx1, o_fox, o_dil, g_fox, g_dil, w_out), semantics=("arbitrary",))


def _ffn_fwd(x1, target, g_ffn, w_gate, w_up, w_down):
    t, d = x1.shape
    f = w_gate.shape[0]
    tt = 256

    def body(x_ref, t_ref, g_ref, wg_ref, wu_ref, wd_ref, a_ref, u_ref, dy_ref, loss_ref):
        xx = x_ref[...]
        r = lax.rsqrt(jnp.mean(xx * xx, axis=-1, keepdims=True) + EPS)
        h = (xx * r * g_ref[...]).astype(BF16)
        a = _dot(h, wg_ref[...], NT)
        u = _dot(h, wu_ref[...], NT)
        a_ref[...] = a.astype(BF16)
        u_ref[...] = u.astype(BF16)
        s = (a / (1.0 + jnp.exp(-a)) * u).astype(BF16)
        y = xx + _dot(s, wd_ref[...])
        e = y - t_ref[...]
        dy_ref[...] = e * (1.0 / d)
        loss_ref[...] = jnp.broadcast_to(0.5 * jnp.sum(e * e) * (1.0 / d), (1, 8, LANES))

    return pl.pallas_call(
        body, name="ffn_fwd", grid=(t // tt,),
        in_specs=[pl.BlockSpec((tt, d), lambda i: (i, 0)), pl.BlockSpec((tt, d), lambda i: (i, 0)),
                  pl.BlockSpec((1, d), lambda i: (0, 0)), pl.BlockSpec(memory_space=pltpu.VMEM),
                  pl.BlockSpec(memory_space=pltpu.VMEM), pl.BlockSpec(memory_space=pltpu.VMEM)],
        out_specs=[pl.BlockSpec((tt, f), lambda i: (i, 0)), pl.BlockSpec((tt, f), lambda i: (i, 0)),
                   pl.BlockSpec((tt, d), lambda i: (i, 0)), pl.BlockSpec((1, 8, LANES), lambda i: (i, 0, 0))],
        out_shape=[jax.ShapeDtypeStruct((t, f), BF16), jax.ShapeDtypeStruct((t, f), BF16),
                   jax.ShapeDtypeStruct((t, d), F32), jax.ShapeDtypeStruct((t // tt, 8, LANES), F32)],
        compiler_params=_params(("arbitrary",)),
    )(x1, target, g_ffn, w_gate, w_up, w_down)


def _ffn_bwd(dy, a, u, x1, g_ffn, w_gate, w_up, w_down):
    t, d = x1.shape
    f = w_gate.shape[0]
    tt = 256

    def body(dy_ref, a_ref, u_ref, x_ref, g_ref, wg_ref, wu_ref, wd_ref,
             dx_ref, s_ref, da_ref, du_ref, h_ref, dg_ref):
        @pl.when(pl.program_id(0) == 0)
        def _():
            dg_ref[...] = jnp.zeros_like(dg_ref)

        dy_ = dy_ref[...]
        ds = _dot(dy_.astype(BF16), wd_ref[...], NT)
        a_ = a_ref[...].astype(F32)
        u_ = u_ref[...].astype(F32)
        sig = 1.0 / (1.0 + jnp.exp(-a_))
        silu = a_ * sig
        s_ref[...] = (silu * u_).astype(BF16)
        da = (ds * u_ * (sig * (1.0 + a_ * (1.0 - sig)))).astype(BF16)
        du = (ds * silu).astype(BF16)
        da_ref[...] = da
        du_ref[...] = du
        dh = _dot(da, wg_ref[...]) + _dot(du, wu_ref[...])
        xx = x_ref[...]
        r = lax.rsqrt(jnp.mean(xx * xx, axis=-1, keepdims=True) + EPS)
        xhat = xx * r
        g = g_ref[...]
        h_ref[...] = (xhat * g).astype(BF16)
        uu = dh * g
        dx_ref[...] = dy_ + r * (uu - xhat * jnp.mean(uu * xhat, axis=-1, keepdims=True))
        dg_ref[0:1, :] += jnp.sum(dh * xhat, axis=0, keepdims=True)

    return pl.pallas_call(
        body, name="ffn_bwd", grid=(t // tt,),
        in_specs=[pl.BlockSpec((tt, d), lambda i: (i, 0)), pl.BlockSpec((tt, f), lambda i: (i, 0)),
                  pl.BlockSpec((tt, f), lambda i: (i, 0)), pl.BlockSpec((tt, d), lambda i: (i, 0)),
                  pl.BlockSpec((1, d), lambda i: (0, 0)), pl.BlockSpec(memory_space=pltpu.VMEM),
                  pl.BlockSpec(memory_space=pltpu.VMEM), pl.BlockSpec(memory_space=pltpu.VMEM)],
        out_specs=[pl.BlockSpec((tt, d), lambda i: (i, 0)), pl.BlockSpec((tt, f), lambda i: (i, 0)),
                   pl.BlockSpec((tt, f), lambda i: (i, 0)), pl.BlockSpec((tt, f), lambda i: (i, 0)),
                   pl.BlockSpec((tt, d), lambda i: (i, 0)), pl.BlockSpec((8, d), lambda i: (0, 0))],
        out_shape=[jax.ShapeDtypeStruct((t, d), F32), jax.ShapeDtypeStruct((t, f), BF16),
                   jax.ShapeDtypeStruct((t, f), BF16), jax.ShapeDtypeStruct((t, f), BF16),
                   jax.ShapeDtypeStruct((t, d), BF16), jax.ShapeDtypeStruct((8, d), F32)],
        compiler_params=_params(("arbitrary",)),
    )(dy, a, u, x1, g_ffn, w_gate, w_up, w_down)


def _in_proj_bwd(dparts, dfa_row, w1, wft, x, g_mix, dx1, rider=None):
    t, d = x.shape
    tt = 512
    npart = len(dparts)

    def body(*refs):
        dp_refs = refs[:npart]
        dfa_ref, w_ref, wf_ref, x_ref, g_ref, dx1_ref, dx_ref, dg_ref = refs[npart:]

        @pl.when(pl.program_id(0) == 0)
        def _():
            dg_ref[...] = jnp.zeros_like(dg_ref)

        dh = _dot(dfa_ref[...].astype(BF16), wf_ref[...], TN)
        for j in range(npart):
            dh = dh + _dot(dp_refs[j][...], w_ref[:, j * W_GROUP:(j + 1) * W_GROUP], NT)
        xx = x_ref[...]
        r = lax.rsqrt(jnp.mean(xx * xx, axis=-1, keepdims=True) + EPS)
        xhat = xx * r
        uu = dh * g_ref[...]
        dx_ref[...] = dx1_ref[...] + r * (uu - xhat * jnp.mean(uu * xhat, axis=-1, keepdims=True))
        dg_ref[0:1, :] += jnp.sum(dh * xhat, axis=0, keepdims=True)

    return _host_call(
        body, rider, name="in_proj_bwd", grid=(t // tt,),
        in_specs=[pl.BlockSpec((tt, W_GROUP), lambda i: (i, 0)) for _ in range(npart)]
        + [pl.BlockSpec((8, tt), lambda i: (0, i)), pl.BlockSpec(memory_space=pltpu.VMEM),
           pl.BlockSpec(memory_space=pltpu.VMEM), pl.BlockSpec((tt, d), lambda i: (i, 0)),
           pl.BlockSpec((1, d), lambda i: (0, 0)), pl.BlockSpec((tt, d), lambda i: (i, 0))],
        out_specs=[pl.BlockSpec((tt, d), lambda i: (i, 0)), pl.BlockSpec((8, d), lambda i: (0, 0))],
        out_shape=[jax.ShapeDtypeStruct((t, d), F32), jax.ShapeDtypeStruct((8, d), F32)],
        scratch_shapes=[], inputs=(*dparts, dfa_row, w1, wft, x, g_mix, dx1), semantics=("arbitrary",))


def _token_matmul(a, b, name, tn, a_is_transposed=True):
    m, t = a.shape if a_is_transposed else a.shape[::-1]
    n = b.shape[1]
    tk = 1024

    def body(a_ref, b_ref, o_ref):
        @pl.when(pl.program_id(1) == 0)
        def _():
            o_ref[...] = jnp.zeros_like(o_ref)

        o_ref[...] += _dot(a_ref[...], b_ref[...].astype(BF16), None if a_is_transposed else TN)

    a_spec = pl.BlockSpec((m, tk), lambda j, k: (0, k)) if a_is_transposed else pl.BlockSpec((tk, m), lambda j, k: (k, 0))
    return pl.pallas_call(
        body, name=name, grid=(n // tn, t // tk),
        in_specs=[a_spec, pl.BlockSpec((tk, tn), lambda j, k: (k, j))],
        out_specs=pl.BlockSpec((m, tn), lambda j, k: (0, j)),
        out_shape=jax.ShapeDtypeStruct((m, n), F32),
        compiler_params=_params(("arbitrary", "arbitrary")),
    )(a, b)


def _token_matmul_parts(at, parts, name):
    m, t = at.shape
    widths = [p.shape[1] for p in parts]
    tk = 1024

    def body(a_ref, *refs):
        o_ref = refs[-1]

        @pl.when(pl.program_id(0) == 0)
        def _():
            o_ref[...] = jnp.zeros_like(o_ref)

        a, first = a_ref[...], 0
        for b_ref, w in zip(refs[:-1], widths):
            o_ref[:, first:first + w] += _dot(a, b_ref[...])
            first += w

    return pl.pallas_call(
        body, name=name, grid=(t // tk,),
        in_specs=[pl.BlockSpec((m, tk), lambda k: (0, k))] + [pl.BlockSpec((tk, w), lambda k: (k, 0)) for w in widths],
        out_specs=pl.BlockSpec((m, sum(widths)), lambda k: (0, 0)),
        out_shape=jax.ShapeDtypeStruct((m, sum(widths)), F32),
        compiler_params=_params(("arbitrary",)),
    )(at, *parts)


def _row_matmul(a_row, b, name):
    t, n = b.shape
    tk = 1024
    nk = t // tk

    def body(a_ref, b_ref, o_ref):
        @pl.when(pl.program_id(0) == 0)
        def _():
            o_ref[...] = jnp.zeros_like(o_ref)

        o_ref[...] += _dot(a_ref[...].astype(BF16), b_ref[...])

    return pl.pallas_call(
        body, name=name, grid=(nk,),
        in_specs=[pl.BlockSpec((8, tk), lambda k: (0, k)), pl.BlockSpec((tk, n), lambda k: (k, 0))],
        out_specs=pl.BlockSpec((8, n), lambda k: (0, 0)),
        out_shape=jax.ShapeDtypeStruct((8, n), F32),
        compiler_params=_params(("arbitrary",)),
    )(a_row, b)


FOX_TQ = 512
SUM_LANE = (HEAD_DIM, 0)


def _fox_fwd(proj, c3, gq, gk, nb, seq, rider=None):
    t = nb * seq
    tq = FOX_TQ
    nq = seq // tq
    npair = N_FOX_HEADS // 2

    def body(q_ref, k_ref, v_ref, c_ref, gq_ref, gk_ref, o_ref, lse_ref, qs, ks, vs):
        ones = _group_ones()
        masks = _head_masks()
        qhat, _ = _head_norm(q_ref[...].astype(F32), ones)
        khat, _ = _head_norm(k_ref[...].astype(F32), ones)
        qs[...] = (qhat * gq_ref[...] * (SCALE * LOG2E)).astype(BF16)
        kn = khat * gk_ref[...]
        vv = v_ref[...].astype(F32)
        lane = lax.broadcasted_iota(jnp.int32, (1, LANES), 1)
        for hd in range(2):
            ks[hd] = (kn * masks[hd]).astype(BF16)
            vs[hd] = (vv * masks[hd] + (lane == SUM_LANE[hd]).astype(F32)).astype(BF16)
        row = lax.broadcasted_iota(jnp.int32, (tq, tq), 0)
        col = lax.broadcasted_iota(jnp.int32, (tq, tq), 1)
        causal = col <= row

        for qi in range(nq):
            q0 = qi * tq
            q_blk = qs[q0:q0 + tq, :]
            o_tot = jnp.zeros((tq, LANES), F32)
            lse_tot = jnp.zeros((tq, LANES), F32)
            for hd in range(2):
                crow = c_ref[0, hd:hd + 1, 0:q0 + tq] * LOG2E
                c0 = crow[:, q0:q0 + 1]
                s_d = _dot(q_blk, ks[hd, q0:q0 + tq, :], NT) + (c0 - crow[:, q0:q0 + tq])
                s_d = jnp.where(causal, s_d, NEG)
                m = jnp.max(s_d, axis=-1, keepdims=True)
                if qi > 0:
                    s_o = _dot(q_blk, ks[hd, 0:q0, :], NT) + (c0 - crow[:, 0:q0])
                    m = jnp.maximum(m, jnp.max(s_o, axis=-1, keepdims=True))
                acc = _dot(jnp.exp2(s_d - m).astype(BF16), vs[hd, q0:q0 + tq, :])
                if qi > 0:
                    acc = acc + _dot(jnp.exp2(s_o - m).astype(BF16), vs[hd, 0:q0, :])
                l = acc[:, SUM_LANE[hd]:SUM_LANE[hd] + 1]
                o_tot = o_tot + (acc / l) * masks[hd]
                lse_tot = lse_tot + (m + jnp.log2(l) - c0) * masks[hd]
            o_ref[q0:q0 + tq, :] = o_tot
            lse_ref[q0:q0 + tq, :] = lse_tot

    blk = lambda off: pl.BlockSpec((seq, LANES), lambda b, p: (b, off + p))
    return _host_call(
        body, rider, name="fox_fwd", grid=(nb, npair),
        in_specs=[blk(0), blk(npair), blk(2 * npair), pl.BlockSpec((1, 2, seq), lambda b, p: (p, 0, b)),
                  pl.BlockSpec((1, LANES), lambda b, p: (0, 0)), pl.BlockSpec((1, LANES), lambda b, p: (0, 0))],
        out_specs=[blk(0), blk(0)],
        out_shape=[jax.ShapeDtypeStruct((t, W_GROUP), F32), jax.ShapeDtypeStruct((t, W_GROUP), F32)],
        scratch_shapes=[pltpu.VMEM((seq, LANES), BF16), pltpu.VMEM((2, seq, LANES), BF16),
                        pltpu.VMEM((2, seq, LANES), BF16)],
        inputs=(proj, proj, proj, c3, gq, gk), semantics=("arbitrary", "arbitrary"))


def _fox_bwd(proj, c3, gq, gk, do, o, lse, nb, seq, rider=None):
    t = nb * seq
    tq = FOX_TQ
    nq = seq // tq
    npair = N_FOX_HEADS // 2

    def body(q_ref, k_ref, v_ref, c_ref, gq_ref, gk_ref, do_ref, o_ref, lse_ref,
             dq_ref, dk_ref, dv_ref, dc_ref, dg_ref, qs, ks, vs, kts, dos, lse_t, delta_t, dqt_acc, dk_acc, dv_acc,
             row_sum):
        @pl.when((pl.program_id(0) == 0) & (pl.program_id(1) == 0))
        def _():
            dg_ref[...] = jnp.zeros_like(dg_ref)

        ones = _group_ones()
        masks = _head_masks()
        qhat, rq = _head_norm(q_ref[...].astype(F32), ones)
        khat, rk = _head_norm(k_ref[...].astype(F32), ones)
        qs[...] = (qhat * gq_ref[...] * (SCALE * LOG2E)).astype(BF16)
        kn = khat * gk_ref[...]
        vv = v_ref[...].astype(F32)
        for hd in range(2):
            ks[hd] = (kn * masks[hd]).astype(BF16)
            vs[hd] = (vv * masks[hd]).astype(BF16)
            kts[hd] = ks[hd].T
        dof = do_ref[...]
        dos[...] = dof.astype(BF16)
        lse_t[...] = lse_ref[...].T
        delta_t[...] = _groupsum(dof * o_ref[...], ones).T
        dqt_acc[...] = jnp.zeros_like(dqt_acc)
        dk_acc[...] = jnp.zeros_like(dk_acc)
        dv_acc[...] = jnp.zeros_like(dv_acc)
        row_sum[...] = jnp.zeros_like(row_sum)
        key = lax.broadcasted_iota(jnp.int32, (tq, tq), 0)
        qry = lax.broadcasted_iota(jnp.int32, (tq, tq), 1)
        causal = key <= qry

        for hd in range(2):
            lane0 = hd * HEAD_DIM
            for kj in range(nq):
                k0 = kj * tq
                k_blk = ks[hd, k0:k0 + tq, :]
                v_blk = vs[hd, k0:k0 + tq, :]
                kt_blk = kts[hd, :, k0:k0 + tq]
                crow = c_ref[0, hd:hd + 1, k0:k0 + tq] * LOG2E
                ck0 = crow[:, 0:1]
                bias = jnp.broadcast_to(ck0 - crow, (LANES, tq)).T[:, 0:1]

                def queries_step(r0, r1, diag, hd=hd, lane0=lane0, k_blk=k_blk, v_blk=v_blk, kt_blk=kt_blk,
                                 bias=bias, ck0=ck0):
                    q_r = qs[r0:r1, :]
                    do_r = dos[r0:r1, :]
                    z = _dot(k_blk, q_r, NT) + bias
                    p = jnp.exp2(z - (lse_t[lane0:lane0 + 1, r0:r1] + ck0))
                    if diag:
                        p = jnp.where(causal, p, 0.0)
                    dp = _dot(v_blk, do_r, NT)
                    ds = p * (dp - delta_t[lane0:lane0 + 1, r0:r1])
                    dsb = ds.astype(BF16)
                    dqt_acc[:, r0:r1] += _dot(kt_blk, dsb)
                    row_sum[hd:hd + 1, r0:r1] += jnp.sum(ds, axis=0, keepdims=True)
                    return _dot(dsb, q_r), _dot(p.astype(BF16), do_r), -jnp.sum(ds, axis=1, keepdims=True)

                dk_j, dv_j, dc_j = queries_step(k0, k0 + tq, True)
                if k0 + tq < seq:
                    dk_o, dv_o, dc_o = queries_step(k0 + tq, seq, False)
                    dk_j, dv_j, dc_j = dk_j + dk_o, dv_j + dv_o, dc_j + dc_o
                dk_acc[k0:k0 + tq, :] += dk_j * masks[hd]
                dv_acc[k0:k0 + tq, :] += dv_j * masks[hd]
                dc_ref[0, hd:hd + 1, k0:k0 + tq] = jnp.broadcast_to(dc_j, (tq, LANES)).T[0:1, :]

        dc_ref[0] += row_sum[0:2, :]

        dq_raw, dgq = _head_norm_bwd(dqt_acc[...].T * SCALE, qhat, rq, gq_ref[...], ones)
        dk_raw, dgk = _head_norm_bwd(dk_acc[...] * LN2, khat, rk, gk_ref[...], ones)
        dq_ref[...] = dq_raw.astype(BF16)
        dk_ref[...] = dk_raw.astype(BF16)
        dv_ref[...] = dv_acc[...].astype(BF16)
        dg_ref[0:1, :] += dgq
        dg_ref[1:2, :] += dgk

    blk = lambda off: pl.BlockSpec((seq, LANES), lambda b, p: (b, off + p))
    vec = pl.BlockSpec((1, LANES), lambda b, p: (0, 0))
    c_spec = pl.BlockSpec((1, 2, seq), lambda b, p: (p, 0, b))
    return _host_call(
        body, rider, name="fox_bwd", grid=(nb, npair),
        in_specs=[blk(0), blk(npair), blk(2 * npair), c_spec, vec, vec, blk(0), blk(0), blk(0)],
        out_specs=[blk(0), blk(0), blk(0), c_spec, pl.BlockSpec((8, LANES), lambda b, p: (0, 0))],
        out_shape=[jax.ShapeDtypeStruct((t, W_GROUP), BF16), jax.ShapeDtypeStruct((t, W_GROUP), BF16),
                   jax.ShapeDtypeStruct((t, W_GROUP), BF16), jax.ShapeDtypeStruct((npair, 2, t), F32),
                   jax.ShapeDtypeStruct((8, LANES), F32)],
        scratch_shapes=[pltpu.VMEM((seq, LANES), BF16), pltpu.VMEM((2, seq, LANES), BF16),
                        pltpu.VMEM((2, seq, LANES), BF16), pltpu.VMEM((2, LANES, seq), BF16),
                        pltpu.VMEM((seq, LANES), BF16), pltpu.VMEM((LANES, seq), F32),
                        pltpu.VMEM((LANES, seq), F32), pltpu.VMEM((LANES, seq), F32),
                        pltpu.VMEM((seq, LANES), F32), pltpu.VMEM((seq, LANES), F32),
                        pltpu.VMEM((8, seq), F32)],
        inputs=(proj, proj, proj, c3, gq, gk, do, o, lse), semantics=("arbitrary", "arbitrary"))


def _dil_prep(q_ref, k_ref, gq_ref, gk_ref, cos_ref, up_ref, dn_ref, ones):
    qhat, rq = _head_norm(q_ref[...].astype(F32), ones)
    khat, rk = _head_norm(k_ref[...].astype(F32), ones)
    cos, up, dn = cos_ref[...], up_ref[...], dn_ref[...]
    qn = _rope(qhat * gq_ref[...], cos, up, dn) * (SCALE * LOG2E)
    kn = _rope(khat * gk_ref[...], cos, up, dn)
    return qhat, rq, khat, rk, qn, kn


def _dil_keys(d, seq, pairs):
    nblk = seq // BAND
    per_res = seq // (d * BAND)
    as_blocks = lambda ref, rows: ref[rows, :].reshape(-1, BAND, LANES)
    if per_res == 1:
        a = lax.broadcasted_iota(jnp.int32, (1, BAND, BAND), 1)
        j = lax.broadcasted_iota(jnp.int32, (1, BAND, BAND), 2)
        causal = jnp.where(j <= a, 0.0, NEG)
        return [as_blocks(src, slice(0, seq)) for src, _ in pairs], [causal]
    for src, dst in pairs:
        dst[:, BAND:, :] = as_blocks(src, slice(0, seq))
        dst[1:, :BAND, :] = as_blocks(src, slice(0, seq - BAND))
        dst[0:1, :BAND, :] = jnp.zeros((1, BAND, LANES), BF16)
    a = lax.broadcasted_iota(jnp.int32, (1, BAND, 2 * BAND), 1)
    j = lax.broadcasted_iota(jnp.int32, (1, BAND, 2 * BAND), 2)
    band = jnp.where(((j < BAND) & (j >= a)) | ((j >= BAND) & (j - BAND <= a)), 0.0, NEG)
    e = lax.broadcasted_iota(jnp.int32, (nblk, 1, 2 * BAND), 0)
    j = lax.broadcasted_iota(jnp.int32, (nblk, 1, 2 * BAND), 2)
    no_prev = jnp.where(((e & (per_res - 1)) == 0) & (j < BAND), NEG, 0.0)
    return [dst[...] for _, dst in pairs], [band + no_prev]


def _regroup(d, seq):
    if d == 1:
        return [(slice(0, seq), slice(0, seq))]
    before, n = d // 4, seq // d
    return [(pl.ds(r1 * (seq // before) + r2, n, stride=4), slice((before * r2 + r1) * n, (before * r2 + r1 + 1) * n))
            for r1 in range(before) for r2 in range(4)]


def _dil_fwd(proj, gq, gk, cos, up, dn, nb, seq):
    t = nb * seq
    npair = W_GROUP // LANES
    off = 3 * npair

    def body(q_ref, k_ref, v_ref, gq_ref, gk_ref, cos_ref, up_ref, dn_ref, o_ref, lse_ref,
             src_a, src_b, qp, kp, vp, kw, vw, m_b, l_b, o_b, state_a, state_b):
        ones = _group_ones()
        masks = _head_masks()
        _, _, _, _, qn, kn = _dil_prep(q_ref, k_ref, gq_ref, gk_ref, cos_ref, up_ref, dn_ref, ones)
        src_a[0] = qn
        src_a[1] = kn
        src_a[2] = v_ref[...].astype(F32)
        nblk = seq // BAND
        src, state = (src_a, src_b), (state_a, state_b)

        for d in DILATIONS:
            last = d == DILATIONS[-1]
            for before, after in _regroup(d, seq):
                qv, kv, vv = src[0].at[0][before, :], src[0].at[1][before, :], src[0].at[2][before, :]
                for hd in range(2):
                    qp[hd, after, :] = (qv * masks[hd]).astype(BF16)
                kp[after, :] = kv.astype(BF16)
                vp[after, :] = vv.astype(BF16)
                if d > 1 and not last:
                    src[1][0, after, :], src[1][1, after, :], src[1][2, after, :] = qv, kv, vv
            if d > 1:
                src = src[::-1]
            (keys_k, keys_v), bias = _dil_keys(d, seq, [(kp, kw), (vp, vw)])
            m_t = jnp.zeros((nblk, BAND, LANES), F32)
            l_t = jnp.zeros((nblk, BAND, LANES), F32)
            o_t = jnp.zeros((nblk, BAND, LANES), F32)
            for hd in range(2):
                s = _dot(qp[hd].reshape(nblk, BAND, LANES), keys_k, BATCH_NT)
                for b_ in bias:
                    s = s + b_
                m = jnp.max(s, axis=-1, keepdims=True)
                p = jnp.exp2(s - m)
                m_t = m_t + m * masks[hd]
                l_t = l_t + jnp.sum(p, axis=-1, keepdims=True) * masks[hd]
                o_t = o_t + _dot(p.astype(BF16), keys_v, BATCH_NN) * masks[hd]
            if d == 1:
                state[0][0] = m_t.reshape(seq, LANES)
                state[0][1] = l_t.reshape(seq, LANES)
                state[0][2] = o_t.reshape(seq, LANES)
                continue
            m_b[...] = m_t.reshape(seq, LANES)
            l_b[...] = l_t.reshape(seq, LANES)
            o_b[...] = o_t.reshape(seq, LANES)
            for before, after in _regroup(d, seq):
                m_old = state[0].at[0][before, :]
                m_new = jnp.maximum(m_old, m_b[after, :])
                w_old = jnp.exp2(m_old - m_new)
                w_new = jnp.exp2(m_b[after, :] - m_new)
                state[1][0, after, :] = m_new
                state[1][1, after, :] = state[0].at[1][before, :] * w_old + l_b[after, :] * w_new
                state[1][2, after, :] = state[0].at[2][before, :] * w_old + o_b[after, :] * w_new
            state = state[::-1]

        l = state[0][1]
        o_b[...] = state[0][2] / l
        l_b[...] = state[0][0] + jnp.log2(l)
        held, spare = [o_b, l_b], [m_b, state[1].at[0]]
        for d in DILATIONS[:0:-1]:
            dests = [o_ref, lse_ref] if d == DILATIONS[1] else spare
            for h, dst in zip(held, dests):
                for before, after in _regroup(d, seq):
                    dst[before, :] = h[after, :]
            held, spare = dests, held

    blk = lambda o_: pl.BlockSpec((seq, LANES), lambda b, p: (b, o_ + p))
    vec = pl.BlockSpec((1, LANES), lambda b, p: (0, 0))
    tab = pl.BlockSpec(memory_space=pltpu.VMEM)
    f32_buf = pltpu.VMEM((seq, LANES), F32)
    f32_x3 = pltpu.VMEM((3, seq, LANES), F32)
    bf16_buf = pltpu.VMEM((seq, LANES), BF16)
    window_buf = pltpu.VMEM((seq // BAND, 2 * BAND, LANES), BF16)
    return pl.pallas_call(
        body, name="dil_fwd", grid=(nb, npair),
        in_specs=[blk(off), blk(off + npair), blk(off + 2 * npair), vec, vec, tab, tab, tab],
        out_specs=[blk(0), blk(0)],
        out_shape=[jax.ShapeDtypeStruct((t, W_GROUP), F32), jax.ShapeDtypeStruct((t, W_GROUP), F32)],
        scratch_shapes=[f32_x3, f32_x3, pltpu.VMEM((2, seq, LANES), BF16), bf16_buf, bf16_buf,
                        window_buf, window_buf, f32_buf, f32_buf, f32_buf, f32_x3, f32_x3],
        compiler_params=_params(("arbitrary", "arbitrary")),
    )(proj, proj, proj, gq, gk, cos, up, dn)


def _dil_bwd(proj, gq, gk, cos, up, dn, do, o, lse, nb, seq, rider=None):
    t = nb * seq
    npair = W_GROUP // LANES
    off = 3 * npair

    def body(q_ref, k_ref, v_ref, gq_ref, gk_ref, cos_ref, up_ref, dn_ref, do_ref, o_ref, lse_ref,
             dq_ref, dk_ref, dv_ref, dg_ref, src_a, src_b, sums_a, sums_b,
             qp, kp, vp, dop, kw, vw, lse_p, delta_p, dq_p, dk_p, dv_p):
        @pl.when((pl.program_id(0) == 0) & (pl.program_id(1) == 0))
        def _():
            dg_ref[...] = jnp.zeros_like(dg_ref)

        ones = _group_ones()
        masks = _head_masks()
        qhat, rq, khat, rk, qn, kn = _dil_prep(q_ref, k_ref, gq_ref, gk_ref, cos_ref, up_ref, dn_ref, ones)
        src_a[0] = qn
        src_a[1] = kn
        src_a[2] = v_ref[...].astype(F32)
        src_a[3] = do_ref[...]
        src_a[4] = lse_ref[...]
        src_a[5] = _groupsum(do_ref[...] * o_ref[...], ones)
        nblk = seq // BAND
        src, sums = (src_a, src_b), (sums_a, sums_b)

        for d in DILATIONS:
            last = d == DILATIONS[-1]
            for before, after in _regroup(d, seq):
                planes = [src[0].at[i][before, :] for i in range(6)]
                for hd in range(2):
                    qp[hd, after, :] = (planes[0] * masks[hd]).astype(BF16)
                    dop[hd, after, :] = (planes[3] * masks[hd]).astype(BF16)
                kp[after, :] = planes[1].astype(BF16)
                vp[after, :] = planes[2].astype(BF16)
                lse_p[after, :] = planes[4]
                delta_p[after, :] = planes[5]
                if d > 1 and not last:
                    for i in range(6):
                        src[1][i, after, :] = planes[i]
            if d > 1:
                src = src[::-1]
            (keys_k, keys_v), bias = _dil_keys(d, seq, [(kp, kw), (vp, vw)])
            nk = keys_k.shape[1]
            dq_b = jnp.zeros((nblk, BAND, LANES), F32)
            dk_b = jnp.zeros((nblk, nk, LANES), F32)
            dv_b = jnp.zeros((nblk, nk, LANES), F32)
            for hd in range(2):
                lane0 = hd * HEAD_DIM
                q3 = qp[hd].reshape(nblk, BAND, LANES)
                do3 = dop[hd].reshape(nblk, BAND, LANES)
                z = _dot(q3, keys_k, BATCH_NT)
                for b_ in bias:
                    z = z + b_
                p = jnp.exp2(z - lse_p[...].reshape(nblk, BAND, LANES)[:, :, lane0:lane0 + 1])
                dp = _dot(do3, keys_v, BATCH_NT)
                ds = (p * (dp - delta_p[...].reshape(nblk, BAND, LANES)[:, :, lane0:lane0 + 1])).astype(BF16)
                dq_b = dq_b + _dot(ds, keys_k, BATCH_NN) * masks[hd]
                dk_b = dk_b + _dot(ds, q3, BATCH_TN)
                dv_b = dv_b + _dot(p.astype(BF16), do3, BATCH_TN)
            dq_p[...] = dq_b.reshape(seq, LANES)
            for acc, out in ((dk_b, dk_p), (dv_b, dv_p)):
                out[...] = acc[:, nk - BAND:, :].reshape(seq, LANES)
                if nk > BAND:
                    out[0:seq - BAND, :] += acc[1:, :BAND, :].reshape(seq - BAND, LANES)
            if d == 1:
                sums[0][0], sums[0][1], sums[0][2] = dq_p[...], dk_p[...], dv_p[...]
                continue
            for before, after in _regroup(d, seq):
                for i, part in enumerate((dq_p, dk_p, dv_p)):
                    sums[1][i, after, :] = sums[0].at[i][before, :] + part[after, :]
            sums = sums[::-1]

        for d in DILATIONS[:0:-1]:
            for i in range(3):
                for before, after in _regroup(d, seq):
                    sums[1].at[i][before, :] = sums[0][i, after, :]
            sums = sums[::-1]

        cos, up, dn = cos_ref[...], up_ref[...], dn_ref[...]
        dq_raw, dgq = _head_norm_bwd(_rope_bwd(sums[0][0] * SCALE, cos, up, dn), qhat, rq, gq_ref[...], ones)
        dk_raw, dgk = _head_norm_bwd(_rope_bwd(sums[0][1] * LN2, cos, up, dn), khat, rk, gk_ref[...], ones)
        dq_ref[...] = dq_raw.astype(BF16)
        dk_ref[...] = dk_raw.astype(BF16)
        dv_ref[...] = sums[0][2].astype(BF16)
        dg_ref[0:1, :] += dgq
        dg_ref[1:2, :] += dgk

    blk = lambda o_: pl.BlockSpec((seq, LANES), lambda b, p: (b, o_ + p))
    vec = pl.BlockSpec((1, LANES), lambda b, p: (0, 0))
    tab = pl.BlockSpec(memory_space=pltpu.VMEM)
    f32_buf = pltpu.VMEM((seq, LANES), F32)
    bf16_buf = pltpu.VMEM((seq, LANES), BF16)
    window_buf = pltpu.VMEM((seq // BAND, 2 * BAND, LANES), BF16)
    bf16_pair = pltpu.VMEM((2, seq, LANES), BF16)
    return _host_call(
        body, rider, name="dil_bwd", grid=(nb, npair),
        in_specs=[blk(off), blk(off + npair), blk(off + 2 * npair), vec, vec, tab, tab, tab,
                  blk(0), blk(0), blk(0)],
        out_specs=[blk(0), blk(0), blk(0), pl.BlockSpec((8, LANES), lambda b, p: (0, 0))],
        out_shape=[jax.ShapeDtypeStruct((t, W_GROUP), BF16), jax.ShapeDtypeStruct((t, W_GROUP), BF16),
                   jax.ShapeDtypeStruct((t, W_GROUP), BF16), jax.ShapeDtypeStruct((8, LANES), F32)],
        scratch_shapes=[pltpu.VMEM((6, seq, LANES), F32)] * 2 + [pltpu.VMEM((3, seq, LANES), F32)] * 2
        + [bf16_pair, bf16_buf, bf16_buf, bf16_pair, window_buf, window_buf] + [f32_buf] * 5,
        inputs=(proj, proj, proj, gq, gk, cos, up, dn, do, o, lse), semantics=("arbitrary", "arbitrary"))


def _adamw(w, g, m, v, name, rider=None):
    row_major = w.ndim == 3 and w.shape[1] == 1
    rows, cols = (w.shape[0], w.shape[2]) if row_major else w.shape[-2:]
    if row_major:
        tr = max(t for t in range(1, 65) if rows % t == 0)
    else:
        tr = _row_tile(rows, max(8, rows // 8)) if rows >= 8 else rows
    c1 = 1.0 - ADAM_B1 ** ADAM_STEP
    c2 = 1.0 - ADAM_B2 ** ADAM_STEP

    def body(w_ref, g_ref, m_ref, v_ref, d_ref, nm_ref, nv_ref):
        g_ = g_ref[...]
        nm = ADAM_B1 * m_ref[...] + (1.0 - ADAM_B1) * g_
        nv = ADAM_B2 * v_ref[...] + (1.0 - ADAM_B2) * (g_ * g_)
        nm_ref[...] = nm
        nv_ref[...] = nv
        d_ref[...] = -ADAM_LR * ((nm / c1) / (jnp.sqrt(nv / c2) + ADAM_EPS) + ADAM_WD * w_ref[...])

    if row_major:
        spec = pl.BlockSpec((tr, 1, cols), lambda i: (i, 0, 0))
    elif w.ndim == 3:
        spec = pl.BlockSpec((1, tr, cols), lambda i: (0, i, 0))
    else:
        spec = pl.BlockSpec((tr, cols), lambda i: (i, 0))
    shape = jax.ShapeDtypeStruct(w.shape, F32)
    return _host_call(
        body, rider, name=name, grid=(rows // tr,), in_specs=[spec] * 4, out_specs=[spec] * 3,
        out_shape=[shape] * 3, scratch_shapes=[], inputs=(w, g, m, v), semantics=("arbitrary",))


def _place():
    x, y, c = lax.axis_index("x"), lax.axis_index("y"), lax.axis_index("c")
    chips = [(1 - x, y), (x, 1 - y), (1 - x, 1 - y)]
    return x, y, c, chips


def _gather_weight(w, name):
    _, rows, cols = w.shape
    half_rows = rows // 2

    def body(w_ref, out_ref, send_sems, recv_sems):
        x, y, c, chips = _place()
        sibling = (x, y, 1 - c)
        mine = 2 * x + y
        lo = pl.multiple_of(c * half_rows, 16)
        lo_sib = pl.multiple_of((1 - c) * half_rows, 16)
        out_ref[mine] = w_ref[0].astype(BF16)

        def copy(k, shard, first_row, to):
            ref = out_ref.at[shard, pl.ds(first_row, half_rows), :]
            return pltpu.make_async_remote_copy(src_ref=ref, dst_ref=ref, send_sem=send_sems.at[k],
                                                recv_sem=recv_sems.at[k], device_id=to, device_id_type=MESH)

        sends = [copy(k, mine, lo, (cx, cy, c)) for k, (cx, cy) in enumerate(chips)]
        for cp in sends:
            cp.start()
        passed = []
        for k, (cx, cy) in enumerate(chips):
            theirs = 2 * cx + cy
            copy(k, theirs, lo, (cx, cy, c)).wait_recv()
            fw = copy(3 + k, theirs, lo, sibling)
            fw.start()
            passed.append(fw)
        for k, (cx, cy) in enumerate(chips):
            copy(3 + k, 2 * cx + cy, lo_sib, sibling).wait_recv()
        for cp in sends + passed:
            cp.wait_send()

    return pl.pallas_call(
        body, name=name,
        in_specs=[pl.BlockSpec(memory_space=pltpu.VMEM)],
        out_specs=pl.BlockSpec(memory_space=pltpu.VMEM),
        out_shape=jax.ShapeDtypeStruct((4, rows, cols), BF16),
        scratch_shapes=[pltpu.SemaphoreType.DMA((6,)), pltpu.SemaphoreType.DMA((6,))],
        compiler_params=pltpu.CompilerParams(vmem_limit_bytes=VMEM_LIMIT),
    )(w)


def _remote(src, dst, sems, k, to):
    send_sems, recv_sems = sems
    return pltpu.make_async_remote_copy(src_ref=src, dst_ref=dst, send_sem=send_sems.at[k], recv_sem=recv_sems.at[k],
                                        device_id=to, device_id_type=MESH)


def _cast_bf16(parts, name):
    def body(*refs):
        for src, dst in zip(refs[:len(parts)], refs[len(parts):]):
            dst[...] = src[0].astype(BF16)

    return pl.pallas_call(
        body, name=name, in_specs=[pl.BlockSpec(memory_space=pltpu.VMEM)] * len(parts),
        out_specs=[pl.BlockSpec(memory_space=pltpu.VMEM)] * len(parts),
        out_shape=[jax.ShapeDtypeStruct(p.shape[1:], BF16) for p in parts],
        compiler_params=pltpu.CompilerParams(vmem_limit_bytes=VMEM_LIMIT),
    )(*parts)


def _gather_rider(shards):
    def copies(ins, outs, sems, which):
        x, y, c, chips = _place()
        sibling = (x, y, 1 - c)
        mine = 2 * x + y
        made = {name: [] for name in which}
        for i, (p_ref, g_ref) in enumerate(zip(ins, outs)):
            half = p_ref.shape[0] // 2
            lo = pl.multiple_of(c * half, 16)
            lo_sib = pl.multiple_of((1 - c) * half, 16)
            spot = lambda shard, first, g_ref=g_ref, half=half: g_ref.at[shard, pl.ds(first, half), :]
            groups = {
                "own": lambda: [pltpu.make_async_copy(p_ref, g_ref.at[mine], sems[0].at[7 * i + 6])],
                "sends": lambda: [_remote(p_ref.at[pl.ds(lo, half), :], spot(mine, lo), sems, 7 * i + k, (cx, cy, c))
                                  for k, (cx, cy) in enumerate(chips)],
                "arrivals": lambda: [_remote(spot(2 * cx + cy, lo), spot(2 * cx + cy, lo), sems, 7 * i + k, (cx, cy, c))
                                     for k, (cx, cy) in enumerate(chips)],
                "passes": lambda: [_remote(spot(2 * cx + cy, lo), spot(2 * cx + cy, lo), sems, 7 * i + 3 + k, sibling)
                                   for k, (cx, cy) in enumerate(chips)],
                "from_sibling": lambda: [_remote(spot(2 * cx + cy, lo_sib), spot(2 * cx + cy, lo_sib), sems,
                                                 7 * i + 3 + k, sibling) for k, (cx, cy) in enumerate(chips)],
            }
            for name in which:
                made[name] += groups[name]()
        return [made[name] for name in which]

    def start(ins, outs, send_sems, recv_sems):
        own, sends = copies(ins, outs, (send_sems, recv_sems), ("own", "sends"))
        for cp in own + sends:
            cp.start()

    def middle(ins, outs, send_sems, recv_sems):
        arrivals, passes = copies(ins, outs, (send_sems, recv_sems), ("arrivals", "passes"))
        for landed, onward in zip(arrivals, passes):
            landed.wait_recv()
            onward.start()

    def finish(ins, outs, send_sems, recv_sems):
        own, sends, passes, from_sibling = copies(ins, outs, (send_sems, recv_sems),
                                                  ("own", "sends", "passes", "from_sibling"))
        for cp in from_sibling:
            cp.wait_recv()
        for cp in sends + passes:
            cp.wait_send()
        for cp in own:
            cp.wait()

    shapes = [jax.ShapeDtypeStruct((4,) + s.shape, BF16) for s in shards]
    return _Rider(shards, shapes, 7 * len(shards), start, finish, middle=middle)


def _exchange_rider(inputs, out_shapes, n_sems, copies, aliases=None):
    def start(ins, outs, send_sems, recv_sems):
        for cp in copies(ins, outs, (send_sems, recv_sems)):
            cp.start()

    def finish(ins, outs, send_sems, recv_sems):
        for cp in copies(ins, outs, (send_sems, recv_sems)):
            cp.wait()

    return _Rider(inputs, out_shapes, n_sems, start, finish, aliases)


def _swap_rider(grads4):
    halves = [g.shape[1] // 2 for g in grads4]

    def copies(ins, outs, sems):
        x, y, c, _ = _place()
        return [_remote(g.at[:, pl.ds(pl.multiple_of((1 - c) * h, 8), h), :], a, sems, i, (x, y, 1 - c))
                for i, (g, a, h) in enumerate(zip(ins, outs, halves))]

    shapes = [jax.ShapeDtypeStruct((4, h, g.shape[2]), F32) for g, h in zip(grads4, halves)]
    return _exchange_rider(grads4, shapes, len(grads4), copies)


def _chip_sum(g4, from_sibling, name):
    _, rows, cols = g4.shape
    half = rows // 2

    def body(g_ref, s_ref, stage_ref, own_ref):
        x, y, c, chips = _place()
        lo = pl.multiple_of(c * half, 8)
        for k, (cx, cy) in enumerate(chips):
            theirs = 2 * cx + cy
            stage_ref[k] = (g_ref[theirs, pl.ds(lo, half), :] + s_ref[theirs]).astype(BF16)
        mine = 2 * x + y
        own_ref[...] = g_ref[mine, pl.ds(lo, half), :] + s_ref[mine]

    return pl.pallas_call(
        body, name=name, in_specs=[pl.BlockSpec(memory_space=pltpu.VMEM)] * 2,
        out_specs=[pl.BlockSpec(memory_space=pltpu.VMEM)] * 2,
        out_shape=[jax.ShapeDtypeStruct((3, half, cols), BF16), jax.ShapeDtypeStruct((half, cols), F32)],
        compiler_params=pltpu.CompilerParams(vmem_limit_bytes=VMEM_LIMIT),
    )(g4, from_sibling)


def _spread_rider(stages):
    def copies(ins, outs, sems):
        _, _, c, chips = _place()
        return [_remote(st.at[k], ld.at[k], sems, 3 * i + k, (cx, cy, c))
                for i, (st, ld) in enumerate(zip(ins, outs)) for k, (cx, cy) in enumerate(chips)]

    shapes = [jax.ShapeDtypeStruct(s.shape, s.dtype) for s in stages]
    return _exchange_rider(stages, shapes, 3 * len(stages), copies)


def _finish_half(own, landed, name):
    half, cols = own.shape

    def body(own_ref, landed_ref, out_ref):
        c = lax.axis_index("c")
        acc = own_ref[...]
        for k in range(3):
            acc = acc + landed_ref[k].astype(F32)
        out_ref[pl.ds(pl.multiple_of(c * half, 8), half), :] = acc

    return pl.pallas_call(
        body, name=name, in_specs=[pl.BlockSpec(memory_space=pltpu.VMEM)] * 2,
        out_specs=pl.BlockSpec(memory_space=pltpu.VMEM),
        out_shape=jax.ShapeDtypeStruct((2 * half, cols), F32),
        compiler_params=pltpu.CompilerParams(vmem_limit_bytes=VMEM_LIMIT),
    )(own, landed)


def _share_rider(fulls):
    def copies(ins, outs, sems):
        x, y, c, _ = _place()
        out = []
        for i, full in enumerate(outs):
            half = full.shape[0] // 2
            rows = full.at[pl.ds(pl.multiple_of(c * half, 8), half), :]
            out.append(_remote(rows, rows, sems, i, (x, y, 1 - c)))
        return out

    def finish_copies(ins, outs, sems):
        x, y, c, _ = _place()
        out = []
        for i, full in enumerate(outs):
            half = full.shape[0] // 2
            mine = full.at[pl.ds(pl.multiple_of(c * half, 8), half), :]
            theirs = full.at[pl.ds(pl.multiple_of((1 - c) * half, 8), half), :]
            out.append((_remote(mine, mine, sems, i, (x, y, 1 - c)), _remote(theirs, theirs, sems, i, (x, y, 1 - c))))
        return out

    def start(ins, outs, send_sems, recv_sems):
        for cp in copies(ins, outs, (send_sems, recv_sems)):
            cp.start()

    def finish(ins, outs, send_sems, recv_sems):
        for sent, landed in finish_copies(ins, outs, (send_sems, recv_sems)):
            sent.wait_send()
            landed.wait_recv()

    shapes = [jax.ShapeDtypeStruct(f.shape, f.dtype) for f in fulls]
    return _Rider(fulls, shapes, len(fulls), start, finish, aliases={i: i for i in range(len(fulls))})


def _all_sum_small(v):
    shape = v.shape

    def body(v_ref, out_ref, buf, send_sems, recv_sems):
        x, y, c, _ = _place()
        me = 4 * x + 2 * y + c
        buf[me] = v_ref[...]
        flips = [(dx, dy, dc) for dx in (0, 1) for dy in (0, 1) for dc in (0, 1)][1:]

        def copy(k, slot, flip):
            dx, dy, dc = flip
            to = (1 - x if dx else x, 1 - y if dy else y, 1 - c if dc else c)
            return pltpu.make_async_remote_copy(src_ref=buf.at[slot], dst_ref=buf.at[slot], send_sem=send_sems.at[k],
                                                recv_sem=recv_sems.at[k], device_id=to, device_id_type=MESH)

        sends = [copy(k, me, flip) for k, flip in enumerate(flips)]
        for cp in sends:
            cp.start()
        for k, (dx, dy, dc) in enumerate(flips):
            sender = 4 * (1 - x if dx else x) + 2 * (1 - y if dy else y) + (1 - c if dc else c)
            copy(k, sender, (dx, dy, dc)).wait_recv()
        for cp in sends:
            cp.wait_send()
        total = buf[0]
        for i in range(1, 8):
            total = total + buf[i]
        out_ref[...] = total

    return pl.pallas_call(
        body, name="all_sum_small",
        in_specs=[pl.BlockSpec(memory_space=pltpu.VMEM)],
        out_specs=pl.BlockSpec(memory_space=pltpu.VMEM),
        out_shape=jax.ShapeDtypeStruct(shape, F32),
        scratch_shapes=[pltpu.VMEM((8,) + shape, F32), pltpu.SemaphoreType.DMA((7,)), pltpu.SemaphoreType.DMA((7,))],
    )(v)


SMALL = (("g_mix", 1024), ("g_ffn", 1024), ("g_out_fox", 512), ("g_out_dil", 512), ("g_q_fox", 64),
         ("g_k_fox", 64), ("g_q_dil", 64), ("g_k_dil", 64), ("b_forget", 8))
SMALL_PACKED = (32, LANES)


def _local_grads(x, target, gains, w1, wft, dense, packed, nb, seq):
    tile2 = lambda g: jnp.tile(g, (1, 2))
    gq_f, gk_f, gq_d, gk_d = (tile2(gains[n]) for n in ("g_q_fox", "g_k_fox", "g_q_dil", "g_k_dil"))
    b_col = gains["b_forget"].reshape(N_FOX_HEADS, 1)
    cos, up, dn = _rope_tables(seq)
    npair = N_FOX_HEADS // 2

    proj, fa_row, h1, h1_t = _in_proj(x, gains["g_mix"], w1, wft)
    c_row = _gate_fwd(fa_row, b_col, seq)
    c3 = c_row.reshape(npair, 2, nb * seq)
    (o_fox, lse_fox), gathered = _fox_fwd(proj, c3, gq_f, gk_f, nb, seq,
                                          rider=None if packed is None else _gather_rider(packed))
    if packed is not None:
        dense = [g.reshape(-1, g.shape[2]) for g in gathered]
    w_out, w_gate, w_up, w_down = dense
    o_dil, lse_dil = _dil_fwd(proj, gq_d, gk_d, cos, up, dn, nb, seq)
    x1, o_n_t = _attn_out(o_fox, o_dil, x, gains["g_out_fox"], gains["g_out_dil"], w_out)
    a, u, dy, loss_parts = _ffn_fwd(x1, target, gains["g_ffn"], w_gate, w_up, w_down)
    loss = jnp.sum(loss_parts[:, 0, 0])

    dx1, s, da, du, h2, dg_ffn = _ffn_bwd(dy, a, u, x1, gains["g_ffn"], w_gate, w_up, w_down)
    d_w_down = _token_matmul(s, dy, "dw_down", 512, False)
    d_w_gate = _token_matmul(da, h2, "dw_gate", 512, False)
    d_w_up = _token_matmul(du, h2, "dw_up", 512, False)
    d_w_out = _token_matmul(o_n_t, dx1, "dw_out", 1024)
    names = ("w_out", "w_gate", "w_up", "w_down")
    grads4 = [g.reshape(4, -1, g.shape[1]) for g in (d_w_out, d_w_gate, d_w_up, d_w_down)]
    exchange = packed is not None
    (do_fox, do_dil, dg_of, dg_od), from_sibling = _attn_out_bwd(
        dx1, o_fox, o_dil, gains["g_out_fox"], gains["g_out_dil"], w_out,
        rider=_swap_rider(grads4) if exchange else None)
    if exchange:
        sums = [_chip_sum(g, s, "chip_sum_" + n) for g, s, n in zip(grads4, from_sibling, names)]
    (dq_f, dk_f, dv_f, dc3, dg_fox), landed = _fox_bwd(
        proj, c3, gq_f, gk_f, do_fox, o_fox, lse_fox, nb, seq,
        rider=_spread_rider([st for st, _ in sums]) if exchange else None)
    if exchange:
        halves = [_finish_half(own, ld, "finish_half_" + n) for (_, own), ld, n in zip(sums, landed, names)]
    (dq_d, dk_d, dv_d, dg_dil), reduced = _dil_bwd(
        proj, gq_d, gk_d, cos, up, dn, do_dil, o_dil, lse_dil, nb, seq,
        rider=_share_rider(halves) if exchange else None)
    if exchange:
        d_w_out, d_w_gate, d_w_up, d_w_down = reduced
    dfa_row, db = _gate_bwd(dc3.reshape(N_FOX_HEADS, nb * seq), fa_row, b_col, seq)
    dparts = [dq_f, dk_f, dv_f, dq_d, dk_d, dv_d]
    d_w1 = _token_matmul_parts(h1_t, dparts, "dw_in")
    d_wf = _row_matmul(dfa_row, h1, "dw_forget")
    fox_w = 3 * W_GROUP
    in_order = [(d_w1[:, :fox_w], fox_w), (d_wf.T, N_FOX_HEADS), (d_w1[:, fox_w:], d_w1.shape[1] - fox_w)]
    n_cols = d_w1.shape[1] + N_FOX_HEADS
    if exchange:
        shards = [jnp.stack([_pick_columns(in_order, s * n_cols // 4, (s + 1) * n_cols // 4) for s in range(4)])]
        _, from_sibling = _idle_host(_swap_rider(shards), "swap_w_in")
        stage, own = _chip_sum(shards[0], from_sibling[0], "chip_sum_w_in")
    (grad_x, dg_mix), landed = _in_proj_bwd(dparts, dfa_row, w1, wft, x, gains["g_mix"], dx1,
                                            rider=_spread_rider([stage]) if exchange else None)
    if exchange:
        d_w_in = _finish_half(own, landed[0], "finish_half_w_in")
    else:
        d_w_in = _pick_columns(in_order, 0, n_cols)

    fold = lambda g2: (g2[:, :HEAD_DIM] + g2[:, HEAD_DIM:])
    small = {
        "g_mix": dg_mix[0:1], "g_ffn": dg_ffn[0:1], "g_out_fox": dg_of[0:1], "g_out_dil": dg_od[0:1],
        "g_q_fox": fold(dg_fox[0:1]), "g_k_fox": fold(dg_fox[1:2]),
        "g_q_dil": fold(dg_dil[0:1]), "g_k_dil": fold(dg_dil[1:2]),
        "b_forget": db[:, 0].reshape(1, N_FOX_HEADS),
    }
    big = {"w_in": d_w_in, "w_out": d_w_out, "w_gate": d_w_gate, "w_up": d_w_up, "w_down": d_w_down}
    return loss, grad_x, big, small


def _pick_columns(pieces, lo, hi):
    out, first = [], 0
    for a, w in pieces:
        a_lo, a_hi = max(lo, first), min(hi, first + w)
        if a_lo < a_hi:
            out.append(a[:, a_lo - first:a_hi - first])
        first += w
    return out[0] if len(out) == 1 else jnp.concatenate(out, axis=1)


def kernel(x, g_mix, w_in, b_forget, g_q_fox, g_k_fox, g_q_dil, g_k_dil, g_out_fox, g_out_dil, w_out, g_ffn, w_gate, w_up, w_down, loss_target, m_g_mix, m_w_in, m_b_forget, m_g_q_fox, m_g_k_fox, m_g_q_dil, m_g_k_dil, m_g_out_fox, m_g_out_dil, m_w_out, m_g_ffn, m_w_gate, m_w_up, m_w_down, v_g_mix, v_w_in, v_b_forget, v_g_q_fox, v_g_k_fox, v_g_q_dil, v_g_k_dil, v_g_out_fox, v_g_out_dil, v_w_out, v_g_ffn, v_w_gate, v_w_up, v_w_down):
    nb, seq, d = x.shape
    weights = dict(g_mix=g_mix, w_in=w_in, b_forget=b_forget, g_q_fox=g_q_fox, g_k_fox=g_k_fox, g_q_dil=g_q_dil,
                   g_k_dil=g_k_dil, g_out_fox=g_out_fox, g_out_dil=g_out_dil, w_out=w_out, g_ffn=g_ffn,
                   w_gate=w_gate, w_up=w_up, w_down=w_down)
    m_in = dict(g_mix=m_g_mix, w_in=m_w_in, b_forget=m_b_forget, g_q_fox=m_g_q_fox, g_k_fox=m_g_k_fox,
                g_q_dil=m_g_q_dil, g_k_dil=m_g_k_dil, g_out_fox=m_g_out_fox, g_out_dil=m_g_out_dil, w_out=m_w_out,
                g_ffn=m_g_ffn, w_gate=m_w_gate, w_up=m_w_up, w_down=m_w_down)
    v_in = dict(g_mix=v_g_mix, w_in=v_w_in, b_forget=v_b_forget, g_q_fox=v_g_q_fox, g_k_fox=v_g_k_fox,
                g_q_dil=v_g_q_dil, g_k_dil=v_g_k_dil, g_out_fox=v_g_out_fox, g_out_dil=v_g_out_dil, w_out=v_w_out,
                g_ffn=v_g_ffn, w_gate=v_w_gate, w_up=v_w_up, w_down=v_w_down)
    order = ["g_mix", "w_in", "b_forget", "g_q_fox", "g_k_fox", "g_q_dil", "g_k_dil", "g_out_fox", "g_out_dil",
             "w_out", "g_ffn", "w_gate", "w_up", "w_down"]

    w_in_all = _gather_weight(w_in, "gather_w_in")
    in_shards = [(w_in_all[s], w_in_all.shape[2]) for s in range(4)]
    fox_w = 3 * W_GROUP
    n_cols = 4 * w_in_all.shape[2]
    w1 = jnp.concatenate([_pick_columns(in_shards, 0, fox_w), _pick_columns(in_shards, fox_w + N_FOX_HEADS, n_cols)],
                         axis=1)
    wft = _pick_columns(in_shards, fox_w, fox_w + N_FOX_HEADS).T
    swap = lambda a: jnp.transpose(a, (0, 2, 1))
    for n in ("w_gate", "w_up"):
        weights[n], m_in[n], v_in[n] = swap(weights[n]), swap(m_in[n]), swap(v_in[n])
    shards = _cast_bf16([weights[n] for n in ("w_out", "w_gate", "w_up", "w_down")], "cast_shards")

    gains = {n: weights[n] for n, _ in SMALL}
    loss, grad_x, big, small = _local_grads(
        x.reshape(nb * seq, d), loss_target.reshape(nb * seq, d), gains, w1, wft, None, shards, nb, seq)

    grads = {n: big[n][None] for n in ("w_out", "w_gate", "w_up", "w_down")}
    packed = jnp.concatenate([small[n].reshape(-1) for n, _ in SMALL] + [loss.reshape(1)])
    packed = jnp.pad(packed, (0, SMALL_PACKED[0] * SMALL_PACKED[1] - packed.shape[0])).reshape(SMALL_PACKED)
    summed = _all_sum_small(packed).reshape(-1)
    pos = 0
    for n, size in SMALL:
        grads[n] = summed[pos:pos + size].reshape(1, size)
        pos += size
    loss = summed[pos]

    to_entry = lambda a: jnp.transpose(a, (2, 0, 1))
    deltas, new_m, new_v, grad_out = {}, {}, {}, {}
    for n in ["w_down"] + [n for n in order if n != "w_down"]:
        rider = _share_rider([big["w_in"]]) if n == "w_down" else None
        (deltas[n], new_m[n], new_v[n]), shared = _adamw(weights[n], grads[n], m_in[n], v_in[n], "adamw_" + n, rider)
        if rider is not None:
            grads["w_in"] = to_entry(shared[0][None])
            weights["w_in"], m_in["w_in"], v_in["w_in"] = (to_entry(a) for a in (w_in, m_w_in, v_w_in))
        grad_out[n] = grads[n]
    for n in ("w_gate", "w_up"):
        grad_out[n], deltas[n], new_m[n], new_v[n] = (swap(a) for a in (grad_out[n], deltas[n], new_m[n], new_v[n]))
    from_entry = lambda a: jnp.transpose(a, (1, 2, 0))
    grad_out["w_in"], deltas["w_in"], new_m["w_in"], new_v["w_in"] = (
        from_entry(a) for a in (grad_out["w_in"], deltas["w_in"], new_m["w_in"], new_v["w_in"]))

    return (loss, grad_x.reshape(nb, seq, d), *[grad_out[n] for n in order], *[deltas[n] for n in order],
            *[new_m[n] for n in order], *[new_v[n] for n in order])
```

```python
import functools
import math

import jax
import jax.numpy as jnp
from jax import lax
from jax.experimental import pallas as pl
from jax.experimental.pallas import tpu as pltpu

F32, BF16 = jnp.float32, jnp.bfloat16
MESH = pl.DeviceIdType.MESH

EPS = 1e-6
NEG = -1e30
HEAD_DIM = 64
SCALE = HEAD_DIM ** -0.5
LOG2E = math.log2(math.e)
LN2 = math.log(2.0)
ROPE_THETA = 500000.0
ROPE_DIM = HEAD_DIM // 4
LANES = 128
W_GROUP = 512
N_FOX_HEADS = 8
VMEM_LIMIT = 56 * 1024 * 1024
DILATIONS = (1, 4, 16)
BAND = 128

ADAM_LR, ADAM_B1, ADAM_B2, ADAM_EPS, ADAM_WD, ADAM_STEP = 0.001, 0.9, 0.999, 1e-08, 0.01, 10

NT = (((1,), (1,)), ((), ()))
TN = (((0,), (0,)), ((), ()))
BATCH_NT = (((2,), (2,)), ((0,), (0,)))
BATCH_NN = (((2,), (1,)), ((0,), (0,)))
BATCH_TN = (((1,), (1,)), ((0,), (0,)))


def _params(sem=None):
    return pltpu.CompilerParams(dimension_semantics=sem, vmem_limit_bytes=VMEM_LIMIT)


def _dot(a, b, dims=None):
    if dims is None:
        return jnp.dot(a, b, preferred_element_type=F32)
    return lax.dot_general(a, b, dims, preferred_element_type=F32)


def _group_ones():
    i = lax.broadcasted_iota(jnp.int32, (LANES, LANES), 0) >> 6
    j = lax.broadcasted_iota(jnp.int32, (LANES, LANES), 1) >> 6
    return (i == j).astype(BF16)


def _split3(x):
    a = x.astype(BF16)
    r = x - a.astype(F32)
    b = r.astype(BF16)
    c = (r - b.astype(F32)).astype(BF16)
    return a, b, c


def _groupsum(x, ones, pieces=2):
    total = None
    for _ in range(pieces):
        piece = x.astype(BF16)
        part = _dot(piece, ones)
        total = part if total is None else total + part
        x = x - piece.astype(F32)
    return total


def _head_masks():
    lane = lax.broadcasted_iota(jnp.int32, (1, LANES), 1)
    return [(lane < HEAD_DIM).astype(F32), (lane >= HEAD_DIM).astype(F32)]


def _head_norm(raw, ones):
    r = lax.rsqrt(_groupsum(raw * raw, ones, 1) * (1.0 / HEAD_DIM) + EPS)
    return raw * r, r


def _head_norm_bwd(dy, xhat, r, gain, ones):
    u = dy * gain
    dgain = jnp.sum(dy * xhat, axis=0, keepdims=True)
    draw = r * (u - xhat * (_groupsum(u * xhat, ones) * (1.0 / HEAD_DIM)))
    return draw, dgain


def _rope(x, cos, s_up, s_dn):
    return x * cos + pltpu.roll(x, LANES - 8, 1) * s_up + pltpu.roll(x, 8, 1) * s_dn


def _rope_bwd(dy, cos, s_up, s_dn):
    return dy * cos + pltpu.roll(dy * s_up, 8, 1) + pltpu.roll(dy * s_dn, LANES - 8, 1)


def _rope_tables(seq):
    half = ROPE_DIM // 2
    inv_freq = jnp.power(jnp.float32(ROPE_THETA), -jnp.arange(half, dtype=F32) * 2.0 / ROPE_DIM)
    ang = jnp.arange(seq).astype(F32)[:, None] * inv_freq[None, :]
    cos, sin = jnp.cos(ang), jnp.sin(ang)
    one = jnp.ones((seq, HEAD_DIM - ROPE_DIM), F32)
    zero_h = jnp.zeros((seq, half), F32)
    zero_r = jnp.zeros((seq, HEAD_DIM - ROPE_DIM), F32)
    c = jnp.concatenate([cos, cos, one], axis=1)
    up = jnp.concatenate([-sin, zero_h, zero_r], axis=1)
    dn = jnp.concatenate([zero_h, sin, zero_r], axis=1)
    return jnp.tile(c, (1, 2)), jnp.tile(up, (1, 2)), jnp.tile(dn, (1, 2))


def _row_tile(rows, cap=256):
    best = rows
    for t in range(8, min(rows, cap) + 1, 8):
        if rows % t == 0:
            best = t
    return best


class _Rider:
    def __init__(self, inputs, out_shapes, n_sems, start, finish, aliases=None, middle=None):
        self.inputs, self.out_shapes, self.n_sems = list(inputs), list(out_shapes), n_sems
        self.start, self.finish, self.middle, self.aliases = start, finish, middle, dict(aliases or {})


def _host_call(body, rider, *, name, grid, in_specs, out_specs, out_shape, scratch_shapes, inputs, semantics):
    if rider is None:
        return pl.pallas_call(body, name=name, grid=grid, in_specs=in_specs, out_specs=out_specs,
                              out_shape=out_shape, scratch_shapes=scratch_shapes,
                              compiler_params=_params(semantics))(*inputs), []
    n_in, n_out, n_scr = len(in_specs), len(out_specs), len(scratch_shapes)
    r_in, r_out = len(rider.inputs), len(rider.out_shapes)

    def wrapped(*refs):
        ins, refs = refs[:n_in], refs[n_in:]
        r_ins, refs = refs[:r_in], refs[r_in:]
        outs, refs = refs[:n_out], refs[n_out:]
        r_outs, refs = refs[:r_out], refs[r_out:]
        scratch, (send_sems, recv_sems) = refs[:n_scr], refs[n_scr:]
        ids = [pl.program_id(a) for a in range(len(grid))]
        first = functools.reduce(lambda p, q: p & q, [i == 0 for i in ids])
        last = functools.reduce(lambda p, q: p & q, [i == g - 1 for i, g in zip(ids, grid)])

        @pl.when(first)
        def _():
            rider.start(r_ins, r_outs, send_sems, recv_sems)

        body(*ins, *outs, *scratch)

        if rider.middle is not None:
            step, steps = ids[0], grid[0]
            for i, g in zip(ids[1:], grid[1:]):
                step, steps = step * g + i, steps * g

            @pl.when(step == (3 * steps) // 4)
            def _():
                rider.middle(r_ins, r_outs, send_sems, recv_sems)

        @pl.when(last)
        def _():
            rider.finish(r_ins, r_outs, send_sems, recv_sems)

    hbm = pl.BlockSpec(memory_space=pl.ANY)
    res = pl.pallas_call(
        wrapped, name=name, grid=grid,
        in_specs=list(in_specs) + [hbm] * r_in, out_specs=list(out_specs) + [hbm] * r_out,
        out_shape=list(out_shape) + rider.out_shapes,
        scratch_shapes=list(scratch_shapes) + [pltpu.SemaphoreType.DMA((rider.n_sems,))] * 2,
        input_output_aliases={n_in + i: n_out + o for i, o in rider.aliases.items()},
        compiler_params=_params(semantics),
    )(*inputs, *rider.inputs)
    return res[:n_out], res[n_out:]


def _idle_host(rider, name):
    def body(o_ref):
        o_ref[...] = jnp.zeros_like(o_ref)

    return _host_call(body, rider, name=name, grid=(1,), in_specs=[],
                      out_specs=[pl.BlockSpec((8, LANES), lambda i: (0, 0))],
                      out_shape=[jax.ShapeDtypeStruct((8, LANES), F32)], scratch_shapes=[], inputs=(),
                      semantics=("arbitrary",))


def _in_proj(x, g_mix, w1, wft):
    t, d = x.shape
    n = w1.shape[1]
    tt = 512

    def body(x_ref, g_ref, w_ref, wf_ref, p_ref, fa_ref, h_ref, ht_ref):
        xx = x_ref[...]
        r = lax.rsqrt(jnp.mean(xx * xx, axis=-1, keepdims=True) + EPS)
        h = (xx * r * g_ref[...]).astype(BF16)
        h_ref[...] = h
        ht_ref[...] = h.T
        for j in range(n // W_GROUP):
            cols = slice(j * W_GROUP, (j + 1) * W_GROUP)
            p_ref[:, cols] = _dot(h, w_ref[:, cols]).astype(BF16)
        fa_ref[...] = _dot(wf_ref[...], h, NT)

    return pl.pallas_call(
        body, name="in_proj", grid=(t // tt,),
        in_specs=[pl.BlockSpec((tt, d), lambda i: (i, 0)), pl.BlockSpec((1, d), lambda i: (0, 0)),
                  pl.BlockSpec(memory_space=pltpu.VMEM), pl.BlockSpec(memory_space=pltpu.VMEM)],
        out_specs=[pl.BlockSpec((tt, n), lambda i: (i, 0)), pl.BlockSpec((8, tt), lambda i: (0, i)),
                   pl.BlockSpec((tt, d), lambda i: (i, 0)), pl.BlockSpec((d, tt), lambda i: (0, i))],
        out_shape=[jax.ShapeDtypeStruct((t, n), BF16), jax.ShapeDtypeStruct((8, t), F32),
                   jax.ShapeDtypeStruct((t, d), BF16), jax.ShapeDtypeStruct((d, t), BF16)],
        compiler_params=_params(("arbitrary",)),
    )(x, g_mix, w1, wft)


def _tri(n, upper):
    i = lax.broadcasted_iota(jnp.int32, (n, n), 0)
    j = lax.broadcasted_iota(jnp.int32, (n, n), 1)
    return ((i <= j) if upper else (i >= j)).astype(BF16)


def _gate_fwd(fa_row, b_col, seq):
    t = fa_row.shape[1]
    cb = 256

    def body(fa_ref, b_ref, c_ref):
        tri = _tri(cb, True)
        carry = jnp.zeros((8, 1), F32)
        for k in range(seq // cb):
            z = fa_ref[:, k * cb:(k + 1) * cb] + b_ref[...]
            lf = jnp.minimum(z, 0.0) - jnp.log(1.0 + jnp.exp(-jnp.abs(z)))
            a, b, c = _split3(lf)
            blk = _dot(a, tri) + _dot(b, tri) + _dot(c, tri) + carry
            c_ref[:, k * cb:(k + 1) * cb] = blk
            carry = blk[:, cb - 1:cb]

    return pl.pallas_call(
        body, name="gate_fwd", grid=(t // seq,),
        in_specs=[pl.BlockSpec((8, seq), lambda i: (0, i)), pl.BlockSpec((8, 1), lambda i: (0, 0))],
        out_specs=pl.BlockSpec((8, seq), lambda i: (0, i)),
        out_shape=jax.ShapeDtypeStruct((8, t), F32),
        compiler_params=_params(("arbitrary",)),
    )(fa_row, b_col)


def _gate_bwd(dc_row, fa_row, b_col, seq):
    t = fa_row.shape[1]
    cb = 256

    def body(dc_ref, fa_ref, b_ref, dfa_ref, db_ref):
        @pl.when(pl.program_id(0) == 0)
        def _():
            db_ref[...] = jnp.zeros_like(db_ref)

        tri = _tri(cb, False)
        carry = jnp.zeros((8, 1), F32)
        dbs = jnp.zeros((8, 1), F32)
        for k in reversed(range(seq // cb)):
            a, b, c = _split3(dc_ref[:, k * cb:(k + 1) * cb])
            dlf = _dot(a, tri) + _dot(b, tri) + _dot(c, tri) + carry
            carry = dlf[:, 0:1]
            z = fa_ref[:, k * cb:(k + 1) * cb] + b_ref[...]
            dfa = dlf / (1.0 + jnp.exp(z))
            dfa_ref[:, k * cb:(k + 1) * cb] = dfa
            dbs = dbs + jnp.sum(dfa, axis=1, keepdims=True)
        db_ref[...] += jnp.broadcast_to(dbs, (8, LANES))

    return pl.pallas_call(
        body, name="gate_bwd", grid=(t // seq,),
        in_specs=[pl.BlockSpec((8, seq), lambda i: (0, i)), pl.BlockSpec((8, seq), lambda i: (0, i)),
                  pl.BlockSpec((8, 1), lambda i: (0, 0))],
        out_specs=[pl.BlockSpec((8, seq), lambda i: (0, i)), pl.BlockSpec((8, LANES), lambda i: (0, 0))],
        out_shape=[jax.ShapeDtypeStruct((8, t), F32), jax.ShapeDtypeStruct((8, LANES), F32)],
        compiler_params=_params(("arbitrary",)),
    )(dc_row, fa_row, b_col)


def _attn_out(o_fox, o_dil, x, g_fox, g_dil, w_out):
    t, d = x.shape
    w = o_fox.shape[1]
    tt = 512

    def body(of_ref, od_ref, x_ref, gf_ref, gd_ref, w_ref, x1_ref, ont_ref):
        acc = x_ref[...]
        for k, (o_ref, g_ref) in enumerate(((of_ref, gf_ref), (od_ref, gd_ref))):
            o = o_ref[...]
            r = lax.rsqrt(jnp.mean(o * o, axis=-1, keepdims=True) + EPS)
            on = (o * r * g_ref[...]).astype(BF16)
            ont_ref[k * w:(k + 1) * w, :] = on.T
            acc = acc + _dot(on, w_ref[k * w:(k + 1) * w, :])
        x1_ref[...] = acc

    return pl.pallas_call(
        body, name="attn_out", grid=(t // tt,),
        in_specs=[pl.BlockSpec((tt, w), lambda i: (i, 0)), pl.BlockSpec((tt, w), lambda i: (i, 0)),
                  pl.BlockSpec((tt, d), lambda i: (i, 0)), pl.BlockSpec((1, w), lambda i: (0, 0)),
                  pl.BlockSpec((1, w), lambda i: (0, 0)), pl.BlockSpec(memory_space=pltpu.VMEM)],
        out_specs=[pl.BlockSpec((tt, d), lambda i: (i, 0)), pl.BlockSpec((2 * w, tt), lambda i: (0, i))],
        out_shape=[jax.ShapeDtypeStruct((t, d), F32), jax.ShapeDtypeStruct((2 * w, t), BF16)],
        compiler_params=_params(("arbitrary",)),
    )(o_fox, o_dil, x, g_fox, g_dil, w_out)


def _attn_out_bwd(dx1, o_fox, o_dil, g_fox, g_dil, w_out, rider=None):
    t, d = dx1.shape
    w = o_fox.shape[1]
    tt = 512

    def body(dx_ref, of_ref, od_ref, gf_ref, gd_ref, w_ref, dof_ref, dod_ref, dgf_ref, dgd_ref):
        @pl.when(pl.program_id(0) == 0)
        def _():
            dgf_ref[...] = jnp.zeros_like(dgf_ref)
            dgd_ref[...] = jnp.zeros_like(dgd_ref)

        dxb = dx_ref[...].astype(BF16)
        for k, (o_ref, g_ref, do_ref, dg_ref) in enumerate(
                ((of_ref, gf_ref, dof_ref, dgf_ref), (od_ref, gd_ref, dod_ref, dgd_ref))):
            don = _dot(dxb, w_ref[k * w:(k + 1) * w, :], NT)
            o = o_ref[...]
            r = lax.rsqrt(jnp.mean(o * o, axis=-1, keepdims=True) + EPS)
            xhat = o * r
            u = don * g_ref[...]
            do_ref[...] = r * (u - xhat * jnp.mean(u * xhat, axis=-1, keepdims=True))
            dg_ref[0:1, :] += jnp.sum(don * xhat, axis=0, keepdims=True)

    return _host_call(
        body, rider, name="attn_out_bwd", grid=(t // tt,),
        in_specs=[pl.BlockSpec((tt, d), lambda i: (i, 0)), pl.BlockSpec((tt, w), lambda i: (i, 0)),
                  pl.BlockSpec((tt, w), lambda i: (i, 0)), pl.BlockSpec((1, w), lambda i: (0, 0)),
                  pl.BlockSpec((1, w), lambda i: (0, 0)), pl.BlockSpec(memory_space=pltpu.VMEM)],
        out_specs=[pl.BlockSpec((tt, w), lambda i: (i, 0)), pl.BlockSpec((tt, w), lambda i: (i, 0)),
                   pl.BlockSpec((8, w), lambda i: (0, 0)), pl.BlockSpec((8, w), lambda i: (0, 0))],
        out_shape=[jax.ShapeDtypeStruct((t, w), F32), jax.ShapeDtypeStruct((t, w), F32),
                   jax.ShapeDtypeStruct((8, w), F32), jax.ShapeDtypeStruct((8, w), F32)],
        scratch_shapes=[], inputs=(dx1, o_fox, o_dil, g_fox, g_dil, w_out), semantics=("arbitrary",))


def _ffn_fwd(x1, target, g_ffn, w_gate, w_up, w_down):
    t, d = x1.shape
    f = w_gate.shape[0]
    tt = 256

    def body(x_ref, t_ref, g_ref, wg_ref, wu_ref, wd_ref, a_ref, u_ref, dy_ref, loss_ref):
        xx = x_ref[...]
        r = lax.rsqrt(jnp.mean(xx * xx, axis=-1, keepdims=True) + EPS)
        h = (xx * r * g_ref[...]).astype(BF16)
        a = _dot(h, wg_ref[...], NT)
        u = _dot(h, wu_ref[...], NT)
        a_ref[...] = a.astype(BF16)
        u_ref[...] = u.astype(BF16)
        s = (a / (1.0 + jnp.exp(-a)) * u).astype(BF16)
        y = xx + _dot(s, wd_ref[...])
        e = y - t_ref[...]
        dy_ref[...] = e * (1.0 / d)
        loss_ref[...] = jnp.broadcast_to(0.5 * jnp.sum(e * e) * (1.0 / d), (1, 8, LANES))

    return pl.pallas_call(
        body, name="ffn_fwd", grid=(t // tt,),
        in_specs=[pl.BlockSpec((tt, d), lambda i: (i, 0)), pl.BlockSpec((tt, d), lambda i: (i, 0)),
                  pl.BlockSpec((1, d), lambda i: (0, 0)), pl.BlockSpec(memory_space=pltpu.VMEM),
                  pl.BlockSpec(memory_space=pltpu.VMEM), pl.BlockSpec(memory_space=pltpu.VMEM)],
        out_specs=[pl.BlockSpec((tt, f), lambda i: (i, 0)), pl.BlockSpec((tt, f), lambda i: (i, 0)),
                   pl.BlockSpec((tt, d), lambda i: (i, 0)), pl.BlockSpec((1, 8, LANES), lambda i: (i, 0, 0))],
        out_shape=[jax.ShapeDtypeStruct((t, f), BF16), jax.ShapeDtypeStruct((t, f), BF16),
                   jax.ShapeDtypeStruct((t, d), F32), jax.ShapeDtypeStruct((t // tt, 8, LANES), F32)],
        compiler_params=_params(("arbitrary",)),
    )(x1, target, g_ffn, w_gate, w_up, w_down)


def _ffn_bwd(dy, a, u, x1, g_ffn, w_gate, w_up, w_down):
    t, d = x1.shape
    f = w_gate.shape[0]
    tt = 256

    def body(dy_ref, a_ref, u_ref, x_ref, g_ref, wg_ref, wu_ref, wd_ref,
             dx_ref, s_ref, da_ref, du_ref, h_ref, dg_ref):
        @pl.when(pl.program_id(0) == 0)
        def _():
            dg_ref[...] = jnp.zeros_like(dg_ref)

        dy_ = dy_ref[...]
        ds = _dot(dy_.astype(BF16), wd_ref[...], NT)
        a_ = a_ref[...].astype(F32)
        u_ = u_ref[...].astype(F32)
        sig = 1.0 / (1.0 + jnp.exp(-a_))
        silu = a_ * sig
        s_ref[...] = (silu * u_).astype(BF16)
        da = (ds * u_ * (sig * (1.0 + a_ * (1.0 - sig)))).astype(BF16)
        du = (ds * silu).astype(BF16)
        da_ref[...] = da
        du_ref[...] = du
        dh = _dot(da, wg_ref[...]) + _dot(du, wu_ref[...])
        xx = x_ref[...]
        r = lax.rsqrt(jnp.mean(xx * xx, axis=-1, keepdims=True) + EPS)
        xhat = xx * r
        g = g_ref[...]
        h_ref[...] = (xhat * g).astype(BF16)
        uu = dh * g
        dx_ref[...] = dy_ + r * (uu - xhat * jnp.mean(uu * xhat, axis=-1, keepdims=True))
        dg_ref[0:1, :] += jnp.sum(dh * xhat, axis=0, keepdims=True)

    return pl.pallas_call(
        body, name="ffn_bwd", grid=(t // tt,),
        in_specs=[pl.BlockSpec((tt, d), lambda i: (i, 0)), pl.BlockSpec((tt, f), lambda i: (i, 0)),
                  pl.BlockSpec((tt, f), lambda i: (i, 0)), pl.BlockSpec((tt, d), lambda i: (i, 0)),
                  pl.BlockSpec((1, d), lambda i: (0, 0)), pl.BlockSpec(memory_space=pltpu.VMEM),
                  pl.BlockSpec(memory_space=pltpu.VMEM), pl.BlockSpec(memory_space=pltpu.VMEM)],
        out_specs=[pl.BlockSpec((tt, d), lambda i: (i, 0)), pl.BlockSpec((tt, f), lambda i: (i, 0)),
                   pl.BlockSpec((tt, f), lambda i: (i, 0)), pl.BlockSpec((tt, f), lambda i: (i, 0)),
                   pl.BlockSpec((tt, d), lambda i: (i, 0)), pl.BlockSpec((8, d), lambda i: (0, 0))],
        out_shape=[jax.ShapeDtypeStruct((t, d), F32), jax.ShapeDtypeStruct((t, f), BF16),
                   jax.ShapeDtypeStruct((t, f), BF16), jax.ShapeDtypeStruct((t, f), BF16),
                   jax.ShapeDtypeStruct((t, d), BF16), jax.ShapeDtypeStruct((8, d), F32)],
        compiler_params=_params(("arbitrary",)),
    )(dy, a, u, x1, g_ffn, w_gate, w_up, w_down)


def _in_proj_bwd(dparts, dfa_row, w1, wft, x, g_mix, dx1, rider=None):
    t, d = x.shape
    tt = 512
    npart = len(dparts)

    def body(*refs):
        dp_refs = refs[:npart]
        dfa_ref, w_ref, wf_ref, x_ref, g_ref, dx1_ref, dx_ref, dg_ref = refs[npart:]

        @pl.when(pl.program_id(0) == 0)
        def _():
            dg_ref[...] = jnp.zeros_like(dg_ref)

        dh = _dot(dfa_ref[...].astype(BF16), wf_ref[...], TN)
        for j in range(npart):
            dh = dh + _dot(dp_refs[j][...], w_ref[:, j * W_GROUP:(j + 1) * W_GROUP], NT)
        xx = x_ref[...]
        r = lax.rsqrt(jnp.mean(xx * xx, axis=-1, keepdims=True) + EPS)
        xhat = xx * r
        uu = dh * g_ref[...]
        dx_ref[...] = dx1_ref[...] + r * (uu - xhat * jnp.mean(uu * xhat, axis=-1, keepdims=True))
        dg_ref[0:1, :] += jnp.sum(dh * xhat, axis=0, keepdims=True)

    return _host_call(
        body, rider, name="in_proj_bwd", grid=(t // tt,),
        in_specs=[pl.BlockSpec((tt, W_GROUP), lambda i: (i, 0)) for _ in range(npart)]
        + [pl.BlockSpec((8, tt), lambda i: (0, i)), pl.BlockSpec(memory_space=pltpu.VMEM),
           pl.BlockSpec(memory_space=pltpu.VMEM), pl.BlockSpec((tt, d), lambda i: (i, 0)),
           pl.BlockSpec((1, d), lambda i: (0, 0)), pl.BlockSpec((tt, d), lambda i: (i, 0))],
        out_specs=[pl.BlockSpec((tt, d), lambda i: (i, 0)), pl.BlockSpec((8, d), lambda i: (0, 0))],
        out_shape=[jax.ShapeDtypeStruct((t, d), F32), jax.ShapeDtypeStruct((8, d), F32)],
        scratch_shapes=[], inputs=(*dparts, dfa_row, w1, wft, x, g_mix, dx1), semantics=("arbitrary",))


def _token_matmul(a, b, name, tn, a_is_transposed=True):
    m, t = a.shape if a_is_transposed else a.shape[::-1]
    n = b.shape[1]
    tk = 1024

    def body(a_ref, b_ref, o_ref):
        @pl.when(pl.program_id(1) == 0)
        def _():
            o_ref[...] = jnp.zeros_like(o_ref)

        o_ref[...] += _dot(a_ref[...], b_ref[...].astype(BF16), None if a_is_transposed else TN)

    a_spec = pl.BlockSpec((m, tk), lambda j, k: (0, k)) if a_is_transposed else pl.BlockSpec((tk, m), lambda j, k: (k, 0))
    return pl.pallas_call(
        body, name=name, grid=(n // tn, t // tk),
        in_specs=[a_spec, pl.BlockSpec((tk, tn), lambda j, k: (k, j))],
        out_specs=pl.BlockSpec((m, tn), lambda j, k: (0, j)),
        out_shape=jax.ShapeDtypeStruct((m, n), F32),
        compiler_params=_params(("arbitrary", "arbitrary")),
    )(a, b)


def _token_matmul_parts(at, parts, name):
    m, t = at.shape
    widths = [p.shape[1] for p in parts]
    tk = 1024

    def body(a_ref, *refs):
        o_ref = refs[-1]

        @pl.when(pl.program_id(0) == 0)
        def _():
            o_ref[...] = jnp.zeros_like(o_ref)

        a, first = a_ref[...], 0
        for b_ref, w in zip(refs[:-1], widths):
            o_ref[:, first:first + w] += _dot(a, b_ref[...])
            first += w

    return pl.pallas_call(
        body, name=name, grid=(t // tk,),
        in_specs=[pl.BlockSpec((m, tk), lambda k: (0, k))] + [pl.BlockSpec((tk, w), lambda k: (k, 0)) for w in widths],
        out_specs=pl.BlockSpec((m, sum(widths)), lambda k: (0, 0)),
        out_shape=jax.ShapeDtypeStruct((m, sum(widths)), F32),
        compiler_params=_params(("arbitrary",)),
    )(at, *parts)


def _row_matmul(a_row, b, name):
    t, n = b.shape
    tk = 1024
    nk = t // tk

    def body(a_ref, b_ref, o_ref):
        @pl.when(pl.program_id(0) == 0)
        def _():
            o_ref[...] = jnp.zeros_like(o_ref)

        o_ref[...] += _dot(a_ref[...].astype(BF16), b_ref[...])

    return pl.pallas_call(
        body, name=name, grid=(nk,),
        in_specs=[pl.BlockSpec((8, tk), lambda k: (0, k)), pl.BlockSpec((tk, n), lambda k: (k, 0))],
        out_specs=pl.BlockSpec((8, n), lambda k: (0, 0)),
        out_shape=jax.ShapeDtypeStruct((8, n), F32),
        compiler_params=_params(("arbitrary",)),
    )(a_row, b)


FOX_TQ = 512
SUM_LANE = (HEAD_DIM, 0)


def _fox_fwd(proj, c3, gq, gk, nb, seq, rider=None):
    t = nb * seq
    tq = FOX_TQ
    nq = seq // tq
    npair = N_FOX_HEADS // 2

    def body(q_ref, k_ref, v_ref, c_ref, gq_ref, gk_ref, o_ref, lse_ref, qs, ks, vs):
        ones = _group_ones()
        masks = _head_masks()
        qhat, _ = _head_norm(q_ref[...].astype(F32), ones)
        khat, _ = _head_norm(k_ref[...].astype(F32), ones)
        qs[...] = (qhat * gq_ref[...] * (SCALE * LOG2E)).astype(BF16)
        kn = khat * gk_ref[...]
        vv = v_ref[...].astype(F32)
        lane = lax.broadcasted_iota(jnp.int32, (1, LANES), 1)
        for hd in range(2):
            ks[hd] = (kn * masks[hd]).astype(BF16)
            vs[hd] = (vv * masks[hd] + (lane == SUM_LANE[hd]).astype(F32)).astype(BF16)
        row = lax.broadcasted_iota(jnp.int32, (tq, tq), 0)
        col = lax.broadcasted_iota(jnp.int32, (tq, tq), 1)
        causal = col <= row

        for qi in range(nq):
            q0 = qi * tq
            q_blk = qs[q0:q0 + tq, :]
            o_tot = jnp.zeros((tq, LANES), F32)
            lse_tot = jnp.zeros((tq, LANES), F32)
            for hd in range(2):
                crow = c_ref[0, hd:hd + 1, 0:q0 + tq] * LOG2E
                c0 = crow[:, q0:q0 + 1]
                s_d = _dot(q_blk, ks[hd, q0:q0 + tq, :], NT) + (c0 - crow[:, q0:q0 + tq])
                s_d = jnp.where(causal, s_d, NEG)
                m = jnp.max(s_d, axis=-1, keepdims=True)
                if qi > 0:
                    s_o = _dot(q_blk, ks[hd, 0:q0, :], NT) + (c0 - crow[:, 0:q0])
                    m = jnp.maximum(m, jnp.max(s_o, axis=-1, keepdims=True))
                acc = _dot(jnp.exp2(s_d - m).astype(BF16), vs[hd, q0:q0 + tq, :])
                if qi > 0:
                    acc = acc + _dot(jnp.exp2(s_o - m).astype(BF16), vs[hd, 0:q0, :])
                l = acc[:, SUM_LANE[hd]:SUM_LANE[hd] + 1]
                o_tot = o_tot + (acc / l) * masks[hd]
                lse_tot = lse_tot + (m + jnp.log2(l) - c0) * masks[hd]
            o_ref[q0:q0 + tq, :] = o_tot
            lse_ref[q0:q0 + tq, :] = lse_tot

    blk = lambda off: pl.BlockSpec((seq, LANES), lambda b, p: (b, off + p))
    return _host_call(
        body, rider, name="fox_fwd", grid=(nb, npair),
        in_specs=[blk(0), blk(npair), blk(2 * npair), pl.BlockSpec((1, 2, seq), lambda b, p: (p, 0, b)),
                  pl.BlockSpec((1, LANES), lambda b, p: (0, 0)), pl.BlockSpec((1, LANES), lambda b, p: (0, 0))],
        out_specs=[blk(0), blk(0)],
        out_shape=[jax.ShapeDtypeStruct((t, W_GROUP), F32), jax.ShapeDtypeStruct((t, W_GROUP), F32)],
        scratch_shapes=[pltpu.VMEM((seq, LANES), BF16), pltpu.VMEM((2, seq, LANES), BF16),
                        pltpu.VMEM((2, seq, LANES), BF16)],
        inputs=(proj, proj, proj, c3, gq, gk), semantics=("arbitrary", "arbitrary"))


def _fox_bwd(proj, c3, gq, gk, do, o, lse, nb, seq, rider=None):
    t = nb * seq
    tq = FOX_TQ
    nq = seq // tq
    npair = N_FOX_HEADS // 2

    def body(q_ref, k_ref, v_ref, c_ref, gq_ref, gk_ref, do_ref, o_ref, lse_ref,
             dq_ref, dk_ref, dv_ref, dc_ref, dg_ref, qs, ks, vs, kts, dos, lse_t, delta_t, dqt_acc, dk_acc, dv_acc,
             row_sum):
        @pl.when((pl.program_id(0) == 0) & (pl.program_id(1) == 0))
        def _():
            dg_ref[...] = jnp.zeros_like(dg_ref)

        ones = _group_ones()
        masks = _head_masks()
        qhat, rq = _head_norm(q_ref[...].astype(F32), ones)
        khat, rk = _head_norm(k_ref[...].astype(F32), ones)
        qs[...] = (qhat * gq_ref[...] * (SCALE * LOG2E)).astype(BF16)
        kn = khat * gk_ref[...]
        vv = v_ref[...].astype(F32)
        for hd in range(2):
            ks[hd] = (kn * masks[hd]).astype(BF16)
            vs[hd] = (vv * masks[hd]).astype(BF16)
            kts[hd] = ks[hd].T
        dof = do_ref[...]
        dos[...] = dof.astype(BF16)
        lse_t[...] = lse_ref[...].T
        delta_t[...] = _groupsum(dof * o_ref[...], ones).T
        dqt_acc[...] = jnp.zeros_like(dqt_acc)
        dk_acc[...] = jnp.zeros_like(dk_acc)
        dv_acc[...] = jnp.zeros_like(dv_acc)
        row_sum[...] = jnp.zeros_like(row_sum)
        key = lax.broadcasted_iota(jnp.int32, (tq, tq), 0)
        qry = lax.broadcasted_iota(jnp.int32, (tq, tq), 1)
        causal = key <= qry

        for hd in range(2):
            lane0 = hd * HEAD_DIM
            for kj in range(nq):
                k0 = kj * tq
                k_blk = ks[hd, k0:k0 + tq, :]
                v_blk = vs[hd, k0:k0 + tq, :]
                kt_blk = kts[hd, :, k0:k0 + tq]
                crow = c_ref[0, hd:hd + 1, k0:k0 + tq] * LOG2E
                ck0 = crow[:, 0:1]
                bias = jnp.broadcast_to(ck0 - crow, (LANES, tq)).T[:, 0:1]

                def queries_step(r0, r1, diag, hd=hd, lane0=lane0, k_blk=k_blk, v_blk=v_blk, kt_blk=kt_blk,
                                 bias=bias, ck0=ck0):
                    q_r = qs[r0:r1, :]
                    do_r = dos[r0:r1, :]
                    z = _dot(k_blk, q_r, NT) + bias
                    p = jnp.exp2(z - (lse_t[lane0:lane0 + 1, r0:r1] + ck0))
                    if diag:
                        p = jnp.where(causal, p, 0.0)
                    dp = _dot(v_blk, do_r, NT)
                    ds = p * (dp - delta_t[lane0:lane0 + 1, r0:r1])
                    dsb = ds.astype(BF16)
                    dqt_acc[:, r0:r1] += _dot(kt_blk, dsb)
                    row_sum[hd:hd + 1, r0:r1] += jnp.sum(ds, axis=0, keepdims=True)
                    return _dot(dsb, q_r), _dot(p.astype(BF16), do_r), -jnp.sum(ds, axis=1, keepdims=True)

                dk_j, dv_j, dc_j = queries_step(k0, k0 + tq, True)
                if k0 + tq < seq:
                    dk_o, dv_o, dc_o = queries_step(k0 + tq, seq, False)
                    dk_j, dv_j, dc_j = dk_j + dk_o, dv_j + dv_o, dc_j + dc_o
                dk_acc[k0:k0 + tq, :] += dk_j * masks[hd]
                dv_acc[k0:k0 + tq, :] += dv_j * masks[hd]
                dc_ref[0, hd:hd + 1, k0:k0 + tq] = jnp.broadcast_to(dc_j, (tq, LANES)).T[0:1, :]

        dc_ref[0] += row_sum[0:2, :]

        dq_raw, dgq = _head_norm_bwd(dqt_acc[...].T * SCALE, qhat, rq, gq_ref[...], ones)
        dk_raw, dgk = _head_norm_bwd(dk_acc[...] * LN2, khat, rk, gk_ref[...], ones)
        dq_ref[...] = dq_raw.astype(BF16)
        dk_ref[...] = dk_raw.astype(BF16)
        dv_ref[...] = dv_acc[...].astype(BF16)
        dg_ref[0:1, :] += dgq
        dg_ref[1:2, :] += dgk

    blk = lambda off: pl.BlockSpec((seq, LANES), lambda b, p: (b, off + p))
    vec = pl.BlockSpec((1, LANES), lambda b, p: (0, 0))
    c_spec = pl.BlockSpec((1, 2, seq), lambda b, p: (p, 0, b))
    return _host_call(
        body, rider, name="fox_bwd", grid=(nb, npair),
        in_specs=[blk(0), blk(npair), blk(2 * npair), c_spec, vec, vec, blk(0), blk(0), blk(0)],
        out_specs=[blk(0), blk(0), blk(0), c_spec, pl.BlockSpec((8, LANES), lambda b, p: (0, 0))],
        out_shape=[jax.ShapeDtypeStruct((t, W_GROUP), BF16), jax.ShapeDtypeStruct((t, W_GROUP), BF16),
                   jax.ShapeDtypeStruct((t, W_GROUP), BF16), jax.ShapeDtypeStruct((npair, 2, t), F32),
                   jax.ShapeDtypeStruct((8, LANES), F32)],
        scratch_shapes=[pltpu.VMEM((seq, LANES), BF16), pltpu.VMEM((2, seq, LANES), BF16),
                        pltpu.VMEM((2, seq, LANES), BF16), pltpu.VMEM((2, LANES, seq), BF16),
                        pltpu.VMEM((seq, LANES), BF16), pltpu.VMEM((LANES, seq), F32),
                        pltpu.VMEM((LANES, seq), F32), pltpu.VMEM((LANES, seq), F32),
                        pltpu.VMEM((seq, LANES), F32), pltpu.VMEM((seq, LANES), F32),
                        pltpu.VMEM((8, seq), F32)],
        inputs=(proj, proj, proj, c3, gq, gk, do, o, lse), semantics=("arbitrary", "arbitrary"))


def _dil_prep(q_ref, k_ref, gq_ref, gk_ref, cos_ref, up_ref, dn_ref, ones):
    qhat, rq = _head_norm(q_ref[...].astype(F32), ones)
    khat, rk = _head_norm(k_ref[...].astype(F32), ones)
    cos, up, dn = cos_ref[...], up_ref[...], dn_ref[...]
    qn = _rope(qhat * gq_ref[...], cos, up, dn) * (SCALE * LOG2E)
    kn = _rope(khat * gk_ref[...], cos, up, dn)
    return qhat, rq, khat, rk, qn, kn


def _dil_keys(d, seq, pairs):
    nblk = seq // BAND
    per_res = seq // (d * BAND)
    as_blocks = lambda ref, rows: ref[rows, :].reshape(-1, BAND, LANES)
    if per_res == 1:
        a = lax.broadcasted_iota(jnp.int32, (1, BAND, BAND), 1)
        j = lax.broadcasted_iota(jnp.int32, (1, BAND, BAND), 2)
        causal = jnp.where(j <= a, 0.0, NEG)
        return [as_blocks(src, slice(0, seq)) for src, _ in pairs], [causal]
    for src, dst in pairs:
        dst[:, BAND:, :] = as_blocks(src, slice(0, seq))
        dst[1:, :BAND, :] = as_blocks(src, slice(0, seq - BAND))
        dst[0:1, :BAND, :] = jnp.zeros((1, BAND, LANES), BF16)
    a = lax.broadcasted_iota(jnp.int32, (1, BAND, 2 * BAND), 1)
    j = lax.broadcasted_iota(jnp.int32, (1, BAND, 2 * BAND), 2)
    band = jnp.where(((j < BAND) & (j >= a)) | ((j >= BAND) & (j - BAND <= a)), 0.0, NEG)
    e = lax.broadcasted_iota(jnp.int32, (nblk, 1, 2 * BAND), 0)
    j = lax.broadcasted_iota(jnp.int32, (nblk, 1, 2 * BAND), 2)
    no_prev = jnp.where(((e & (per_res - 1)) == 0) & (j < BAND), NEG, 0.0)
    return [dst[...] for _, dst in pairs], [band + no_prev]


def _regroup(d, seq):
    if d == 1:
        return [(slice(0, seq), slice(0, seq))]
    before, n = d // 4, seq // d
    return [(pl.ds(r1 * (seq // before) + r2, n, stride=4), slice((before * r2 + r1) * n, (before * r2 + r1 + 1) * n))
            for r1 in range(before) for r2 in range(4)]


def _dil_fwd(proj, gq, gk, cos, up, dn, nb, seq):
    t = nb * seq
    npair = W_GROUP // LANES
    off = 3 * npair

    def body(q_ref, k_ref, v_ref, gq_ref, gk_ref, cos_ref, up_ref, dn_ref, o_ref, lse_ref,
             src_a, src_b, qp, kp, vp, kw, vw, m_b, l_b, o_b, state_a, state_b):
        ones = _group_ones()
        masks = _head_masks()
        _, _, _, _, qn, kn = _dil_prep(q_ref, k_ref, gq_ref, gk_ref, cos_ref, up_ref, dn_ref, ones)
        src_a[0] = qn
        src_a[1] = kn
        src_a[2] = v_ref[...].astype(F32)
        nblk = seq // BAND
        src, state = (src_a, src_b), (state_a, state_b)

        for d in DILATIONS:
            last = d == DILATIONS[-1]
            for before, after in _regroup(d, seq):
                qv, kv, vv = src[0].at[0][before, :], src[0].at[1][before, :], src[0].at[2][before, :]
                for hd in range(2):
                    qp[hd, after, :] = (qv * masks[hd]).astype(BF16)
                kp[after, :] = kv.astype(BF16)
                vp[after, :] = vv.astype(BF16)
                if d > 1 and not last:
                    src[1][0, after, :], src[1][1, after, :], src[1][2, after, :] = qv, kv, vv
            if d > 1:
                src = src[::-1]
            (keys_k, keys_v), bias = _dil_keys(d, seq, [(kp, kw), (vp, vw)])
            m_t = jnp.zeros((nblk, BAND, LANES), F32)
            l_t = jnp.zeros((nblk, BAND, LANES), F32)
            o_t = jnp.zeros((nblk, BAND, LANES), F32)
            for hd in range(2):
                s = _dot(qp[hd].reshape(nblk, BAND, LANES), keys_k, BATCH_NT)
                for b_ in bias:
                    s = s + b_
                m = jnp.max(s, axis=-1, keepdims=True)
                p = jnp.exp2(s - m)
                m_t = m_t + m * masks[hd]
                l_t = l_t + jnp.sum(p, axis=-1, keepdims=True) * masks[hd]
                o_t = o_t + _dot(p.astype(BF16), keys_v, BATCH_NN) * masks[hd]
            if d == 1:
                state[0][0] = m_t.reshape(seq, LANES)
                state[0][1] = l_t.reshape(seq, LANES)
                state[0][2] = o_t.reshape(seq, LANES)
                continue
            m_b[...] = m_t.reshape(seq, LANES)
            l_b[...] = l_t.reshape(seq, LANES)
            o_b[...] = o_t.reshape(seq, LANES)
            for before, after in _regroup(d, seq):
                m_old = state[0].at[0][before, :]
                m_new = jnp.maximum(m_old, m_b[after, :])
                w_old = jnp.exp2(m_old - m_new)
                w_new = jnp.exp2(m_b[after, :] - m_new)
                state[1][0, after, :] = m_new
                state[1][1, after, :] = state[0].at[1][before, :] * w_old + l_b[after, :] * w_new
                state[1][2, after, :] = state[0].at[2][before, :] * w_old + o_b[after, :] * w_new
            state = state[::-1]

        l = state[0][1]
        o_b[...] = state[0][2] / l
        l_b[...] = state[0][0] + jnp.log2(l)
        held, spare = [o_b, l_b], [m_b, state[1].at[0]]
        for d in DILATIONS[:0:-1]:
            dests = [o_ref, lse_ref] if d == DILATIONS[1] else spare
            for h, dst in zip(held, dests):
                for before, after in _regroup(d, seq):
                    dst[before, :] = h[after, :]
            held, spare = dests, held

    blk = lambda o_: pl.BlockSpec((seq, LANES), lambda b, p: (b, o_ + p))
    vec = pl.BlockSpec((1, LANES), lambda b, p: (0, 0))
    tab = pl.BlockSpec(memory_space=pltpu.VMEM)
    f32_buf = pltpu.VMEM((seq, LANES), F32)
    f32_x3 = pltpu.VMEM((3, seq, LANES), F32)
    bf16_buf = pltpu.VMEM((seq, LANES), BF16)
    window_buf = pltpu.VMEM((seq // BAND, 2 * BAND, LANES), BF16)
    return pl.pallas_call(
        body, name="dil_fwd", grid=(nb, npair),
        in_specs=[blk(off), blk(off + npair), blk(off + 2 * npair), vec, vec, tab, tab, tab],
        out_specs=[blk(0), blk(0)],
        out_shape=[jax.ShapeDtypeStruct((t, W_GROUP), F32), jax.ShapeDtypeStruct((t, W_GROUP), F32)],
        scratch_shapes=[f32_x3, f32_x3, pltpu.VMEM((2, seq, LANES), BF16), bf16_buf, bf16_buf,
                        window_buf, window_buf, f32_buf, f32_buf, f32_buf, f32_x3, f32_x3],
        compiler_params=_params(("arbitrary", "arbitrary")),
    )(proj, proj, proj, gq, gk, cos, up, dn)


def _dil_bwd(proj, gq, gk, cos, up, dn, do, o, lse, nb, seq, rider=None):
    t = nb * seq
    npair = W_GROUP // LANES
    off = 3 * npair

    def body(q_ref, k_ref, v_ref, gq_ref, gk_ref, cos_ref, up_ref, dn_ref, do_ref, o_ref, lse_ref,
             dq_ref, dk_ref, dv_ref, dg_ref, src_a, src_b, sums_a, sums_b,
             qp, kp, vp, dop, kw, vw, lse_p, delta_p, dq_p, dk_p, dv_p):
        @pl.when((pl.program_id(0) == 0) & (pl.program_id(1) == 0))
        def _():
            dg_ref[...] = jnp.zeros_like(dg_ref)

        ones = _group_ones()
        masks = _head_masks()
        qhat, rq, khat, rk, qn, kn = _dil_prep(q_ref, k_ref, gq_ref, gk_ref, cos_ref, up_ref, dn_ref, ones)
        src_a[0] = qn
        src_a[1] = kn
        src_a[2] = v_ref[...].astype(F32)
        src_a[3] = do_ref[...]
        src_a[4] = lse_ref[...]
        src_a[5] = _groupsum(do_ref[...] * o_ref[...], ones)
        nblk = seq // BAND
        src, sums = (src_a, src_b), (sums_a, sums_b)

        for d in DILATIONS:
            last = d == DILATIONS[-1]
            for before, after in _regroup(d, seq):
                planes = [src[0].at[i][before, :] for i in range(6)]
                for hd in range(2):
                    qp[hd, after, :] = (planes[0] * masks[hd]).astype(BF16)
                    dop[hd, after, :] = (planes[3] * masks[hd]).astype(BF16)
                kp[after, :] = planes[1].astype(BF16)
                vp[after, :] = planes[2].astype(BF16)
                lse_p[after, :] = planes[4]
                delta_p[after, :] = planes[5]
                if d > 1 and not last:
                    for i in range(6):
                        src[1][i, after, :] = planes[i]
            if d > 1:
                src = src[::-1]
            (keys_k, keys_v), bias = _dil_keys(d, seq, [(kp, kw), (vp, vw)])
            nk = keys_k.shape[1]
            dq_b = jnp.zeros((nblk, BAND, LANES), F32)
            dk_b = jnp.zeros((nblk, nk, LANES), F32)
            dv_b = jnp.zeros((nblk, nk, LANES), F32)
            for hd in range(2):
                lane0 = hd * HEAD_DIM
                q3 = qp[hd].reshape(nblk, BAND, LANES)
                do3 = dop[hd].reshape(nblk, BAND, LANES)
                z = _dot(q3, keys_k, BATCH_NT)
                for b_ in bias:
                    z = z + b_
                p = jnp.exp2(z - lse_p[...].reshape(nblk, BAND, LANES)[:, :, lane0:lane0 + 1])
                dp = _dot(do3, keys_v, BATCH_NT)
                ds = (p * (dp - delta_p[...].reshape(nblk, BAND, LANES)[:, :, lane0:lane0 + 1])).astype(BF16)
                dq_b = dq_b + _dot(ds, keys_k, BATCH_NN) * masks[hd]
                dk_b = dk_b + _dot(ds, q3, BATCH_TN)
                dv_b = dv_b + _dot(p.astype(BF16), do3, BATCH_TN)
            dq_p[...] = dq_b.reshape(seq, LANES)
            for acc, out in ((dk_b, dk_p), (dv_b, dv_p)):
                out[...] = acc[:, nk - BAND:, :].reshape(seq, LANES)
                if nk > BAND:
                    out[0:seq - BAND, :] += acc[1:, :BAND, :].reshape(seq - BAND, LANES)
            if d == 1:
                sums[0][0], sums[0][1], sums[0][2] = dq_p[...], dk_p[...], dv_p[...]
                continue
            for before, after in _regroup(d, seq):
                for i, part in enumerate((dq_p, dk_p, dv_p)):
                    sums[1][i, after, :] = sums[0].at[i][before, :] + part[after, :]
            sums = sums[::-1]

        for d in DILATIONS[:0:-1]:
            for i in range(3):
                for before, after in _regroup(d, seq):
                    sums[1].at[i][before, :] = sums[0][i, after, :]
            sums = sums[::-1]

        cos, up, dn = cos_ref[...], up_ref[...], dn_ref[...]
        dq_raw, dgq = _head_norm_bwd(_rope_bwd(sums[0][0] * SCALE, cos, up, dn), qhat, rq, gq_ref[...], ones)
        dk_raw, dgk = _head_norm_bwd(_rope_bwd(sums[0][1] * LN2, cos, up, dn), khat, rk, gk_ref[...], ones)
        dq_ref[...] = dq_raw.astype(BF16)
        dk_ref[...] = dk_raw.astype(BF16)
        dv_ref[...] = sums[0][2].astype(BF16)
        dg_ref[0:1, :] += dgq
        dg_ref[1:2, :] += dgk

    blk = lambda o_: pl.BlockSpec((seq, LANES), lambda b, p: (b, o_ + p))
    vec = pl.BlockSpec((1, LANES), lambda b, p: (0, 0))
    tab = pl.BlockSpec(memory_space=pltpu.VMEM)
    f32_buf = pltpu.VMEM((seq, LANES), F32)
    bf16_buf = pltpu.VMEM((seq, LANES), BF16)
    window_buf = pltpu.VMEM((seq // BAND, 2 * BAND, LANES), BF16)
    bf16_pair = pltpu.VMEM((2, seq, LANES), BF16)
    return _host_call(
        body, rider, name="dil_bwd", grid=(nb, npair),
        in_specs=[blk(off), blk(off + npair), blk(off + 2 * npair), vec, vec, tab, tab, tab,
                  blk(0), blk(0), blk(0)],
        out_specs=[blk(0), blk(0), blk(0), pl.BlockSpec((8, LANES), lambda b, p: (0, 0))],
        out_shape=[jax.ShapeDtypeStruct((t, W_GROUP), BF16), jax.ShapeDtypeStruct((t, W_GROUP), BF16),
                   jax.ShapeDtypeStruct((t, W_GROUP), BF16), jax.ShapeDtypeStruct((8, LANES), F32)],
        scratch_shapes=[pltpu.VMEM((6, seq, LANES), F32)] * 2 + [pltpu.VMEM((3, seq, LANES), F32)] * 2
        + [bf16_pair, bf16_buf, bf16_buf, bf16_pair, window_buf, window_buf] + [f32_buf] * 5,
        inputs=(proj, proj, proj, gq, gk, cos, up, dn, do, o, lse), semantics=("arbitrary", "arbitrary"))


def _adamw(w, g, m, v, name, rider=None):
    row_major = w.ndim == 3 and w.shape[1] == 1
    rows, cols = (w.shape[0], w.shape[2]) if row_major else w.shape[-2:]
    if row_major:
        tr = max(t for t in range(1, 65) if rows % t == 0)
    else:
        tr = _row_tile(rows) if rows >= 8 else rows
    c1 = 1.0 - ADAM_B1 ** ADAM_STEP
    c2 = 1.0 - ADAM_B2 ** ADAM_STEP

    def body(w_ref, g_ref, m_ref, v_ref, d_ref, nm_ref, nv_ref):
        g_ = g_ref[...]
        nm = ADAM_B1 * m_ref[...] + (1.0 - ADAM_B1) * g_
        nv = ADAM_B2 * v_ref[...] + (1.0 - ADAM_B2) * (g_ * g_)
        nm_ref[...] = nm
        nv_ref[...] = nv
        d_ref[...] = -ADAM_LR * ((nm / c1) / (jnp.sqrt(nv / c2) + ADAM_EPS) + ADAM_WD * w_ref[...])

    if row_major:
        spec = pl.BlockSpec((tr, 1, cols), lambda i: (i, 0, 0))
    elif w.ndim == 3:
        spec = pl.BlockSpec((1, tr, cols), lambda i: (0, i, 0))
    else:
        spec = pl.BlockSpec((tr, cols), lambda i: (i, 0))
    shape = jax.ShapeDtypeStruct(w.shape, F32)
    return _host_call(
        body, rider, name=name, grid=(rows // tr,), in_specs=[spec] * 4, out_specs=[spec] * 3,
        out_shape=[shape] * 3, scratch_shapes=[], inputs=(w, g, m, v), semantics=("arbitrary",))


def _place():
    x, y, c = lax.axis_index("x"), lax.axis_index("y"), lax.axis_index("c")
    chips = [(1 - x, y), (x, 1 - y), (1 - x, 1 - y)]
    return x, y, c, chips


def _gather_weight(w, name):
    _, rows, cols = w.shape
    half_rows = rows // 2

    def body(w_ref, out_ref, send_sems, recv_sems):
        x, y, c, chips = _place()
        sibling = (x, y, 1 - c)
        mine = 2 * x + y
        lo = pl.multiple_of(c * half_rows, 16)
        lo_sib = pl.multiple_of((1 - c) * half_rows, 16)
        out_ref[mine] = w_ref[0].astype(BF16)

        def copy(k, shard, first_row, to):
            ref = out_ref.at[shard, pl.ds(first_row, half_rows), :]
            return pltpu.make_async_remote_copy(src_ref=ref, dst_ref=ref, send_sem=send_sems.at[k],
                                                recv_sem=recv_sems.at[k], device_id=to, device_id_type=MESH)

        sends = [copy(k, mine, lo, (cx, cy, c)) for k, (cx, cy) in enumerate(chips)]
        for cp in sends:
            cp.start()
        passed = []
        for k, (cx, cy) in enumerate(chips):
            theirs = 2 * cx + cy
            copy(k, theirs, lo, (cx, cy, c)).wait_recv()
            fw = copy(3 + k, theirs, lo, sibling)
            fw.start()
            passed.append(fw)
        for k, (cx, cy) in enumerate(chips):
            copy(3 + k, 2 * cx + cy, lo_sib, sibling).wait_recv()
        for cp in sends + passed:
            cp.wait_send()

    return pl.pallas_call(
        body, name=name,
        in_specs=[pl.BlockSpec(memory_space=pltpu.VMEM)],
        out_specs=pl.BlockSpec(memory_space=pltpu.VMEM),
        out_shape=jax.ShapeDtypeStruct((4, rows, cols), BF16),
        scratch_shapes=[pltpu.SemaphoreType.DMA((6,)), pltpu.SemaphoreType.DMA((6,))],
        compiler_params=pltpu.CompilerParams(vmem_limit_bytes=VMEM_LIMIT),
    )(w)


def _remote(src, dst, sems, k, to):
    send_sems, recv_sems = sems
    return pltpu.make_async_remote_copy(src_ref=src, dst_ref=dst, send_sem=send_sems.at[k], recv_sem=recv_sems.at[k],
                                        device_id=to, device_id_type=MESH)


def _cast_bf16(parts, name):
    def body(*refs):
        for src, dst in zip(refs[:len(parts)], refs[len(parts):]):
            dst[...] = src[0].astype(BF16)

    return pl.pallas_call(
        body, name=name, in_specs=[pl.BlockSpec(memory_space=pltpu.VMEM)] * len(parts),
        out_specs=[pl.BlockSpec(memory_space=pltpu.VMEM)] * len(parts),
        out_shape=[jax.ShapeDtypeStruct(p.shape[1:], BF16) for p in parts],
        compiler_params=pltpu.CompilerParams(vmem_limit_bytes=VMEM_LIMIT),
    )(*parts)


def _gather_rider(shards):
    def copies(ins, outs, sems, which):
        x, y, c, chips = _place()
        sibling = (x, y, 1 - c)
        mine = 2 * x + y
        made = {name: [] for name in which}
        for i, (p_ref, g_ref) in enumerate(zip(ins, outs)):
            half = p_ref.shape[0] // 2
            lo = pl.multiple_of(c * half, 16)
            lo_sib = pl.multiple_of((1 - c) * half, 16)
            spot = lambda shard, first, g_ref=g_ref, half=half: g_ref.at[shard, pl.ds(first, half), :]
            groups = {
                "own": lambda: [pltpu.make_async_copy(p_ref, g_ref.at[mine], sems[0].at[7 * i + 6])],
                "sends": lambda: [_remote(p_ref.at[pl.ds(lo, half), :], spot(mine, lo), sems, 7 * i + k, (cx, cy, c))
                                  for k, (cx, cy) in enumerate(chips)],
                "arrivals": lambda: [_remote(spot(2 * cx + cy, lo), spot(2 * cx + cy, lo), sems, 7 * i + k, (cx, cy, c))
                                     for k, (cx, cy) in enumerate(chips)],
                "passes": lambda: [_remote(spot(2 * cx + cy, lo), spot(2 * cx + cy, lo), sems, 7 * i + 3 + k, sibling)
                                   for k, (cx, cy) in enumerate(chips)],
                "from_sibling": lambda: [_remote(spot(2 * cx + cy, lo_sib), spot(2 * cx + cy, lo_sib), sems,
                                                 7 * i + 3 + k, sibling) for k, (cx, cy) in enumerate(chips)],
            }
            for name in which:
                made[name] += groups[name]()
        return [made[name] for name in which]

    def start(ins, outs, send_sems, recv_sems):
        own, sends = copies(ins, outs, (send_sems, recv_sems), ("own", "sends"))
        for cp in own + sends:
            cp.start()

    def middle(ins, outs, send_sems, recv_sems):
        arrivals, passes = copies(ins, outs, (send_sems, recv_sems), ("arrivals", "passes"))
        for landed, onward in zip(arrivals, passes):
            landed.wait_recv()
            onward.start()

    def finish(ins, outs, send_sems, recv_sems):
        own, sends, passes, from_sibling = copies(ins, outs, (send_sems, recv_sems),
                                                  ("own", "sends", "passes", "from_sibling"))
        for cp in from_sibling:
            cp.wait_recv()
        for cp in sends + passes:
            cp.wait_send()
        for cp in own:
            cp.wait()

    shapes = [jax.ShapeDtypeStruct((4,) + s.shape, BF16) for s in shards]
    return _Rider(shards, shapes, 7 * len(shards), start, finish, middle=middle)


def _exchange_rider(inputs, out_shapes, n_sems, copies, aliases=None):
    def start(ins, outs, send_sems, recv_sems):
        for cp in copies(ins, outs, (send_sems, recv_sems)):
            cp.start()

    def finish(ins, outs, send_sems, recv_sems):
        for cp in copies(ins, outs, (send_sems, recv_sems)):
            cp.wait()

    return _Rider(inputs, out_shapes, n_sems, start, finish, aliases)


def _swap_rider(grads4):
    halves = [g.shape[1] // 2 for g in grads4]

    def copies(ins, outs, sems):
        x, y, c, _ = _place()
        return [_remote(g.at[:, pl.ds(pl.multiple_of((1 - c) * h, 8), h), :], a, sems, i, (x, y, 1 - c))
                for i, (g, a, h) in enumerate(zip(ins, outs, halves))]

    shapes = [jax.ShapeDtypeStruct((4, h, g.shape[2]), F32) for g, h in zip(grads4, halves)]
    return _exchange_rider(grads4, shapes, len(grads4), copies)


def _chip_sum(g4, from_sibling, name):
    _, rows, cols = g4.shape
    half = rows // 2

    def body(g_ref, s_ref, stage_ref, own_ref):
        x, y, c, chips = _place()
        lo = pl.multiple_of(c * half, 8)
        for k, (cx, cy) in enumerate(chips):
            theirs = 2 * cx + cy
            stage_ref[k] = (g_ref[theirs, pl.ds(lo, half), :] + s_ref[theirs]).astype(BF16)
        mine = 2 * x + y
        own_ref[...] = g_ref[mine, pl.ds(lo, half), :] + s_ref[mine]

    return pl.pallas_call(
        body, name=name, in_specs=[pl.BlockSpec(memory_space=pltpu.VMEM)] * 2,
        out_specs=[pl.BlockSpec(memory_space=pltpu.VMEM)] * 2,
        out_shape=[jax.ShapeDtypeStruct((3, half, cols), BF16), jax.ShapeDtypeStruct((half, cols), F32)],
        compiler_params=pltpu.CompilerParams(vmem_limit_bytes=VMEM_LIMIT),
    )(g4, from_sibling)


def _spread_rider(stages):
    def copies(ins, outs, sems):
        _, _, c, chips = _place()
        return [_remote(st.at[k], ld.at[k], sems, 3 * i + k, (cx, cy, c))
                for i, (st, ld) in enumerate(zip(ins, outs)) for k, (cx, cy) in enumerate(chips)]

    shapes = [jax.ShapeDtypeStruct(s.shape, s.dtype) for s in stages]
    return _exchange_rider(stages, shapes, 3 * len(stages), copies)


def _finish_half(own, landed, name):
    half, cols = own.shape

    def body(own_ref, landed_ref, out_ref):
        c = lax.axis_index("c")
        acc = own_ref[...]
        for k in range(3):
            acc = acc + landed_ref[k].astype(F32)
        out_ref[pl.ds(pl.multiple_of(c * half, 8), half), :] = acc

    return pl.pallas_call(
        body, name=name, in_specs=[pl.BlockSpec(memory_space=pltpu.VMEM)] * 2,
        out_specs=pl.BlockSpec(memory_space=pltpu.VMEM),
        out_shape=jax.ShapeDtypeStruct((2 * half, cols), F32),
        compiler_params=pltpu.CompilerParams(vmem_limit_bytes=VMEM_LIMIT),
    )(own, landed)


def _share_rider(fulls, first_sem=0):
    def copies(ins, outs, sems):
        x, y, c, _ = _place()
        out = []
        for i, full in enumerate(outs):
            half = full.shape[0] // 2
            rows = full.at[pl.ds(pl.multiple_of(c * half, 8), half), :]
            out.append(_remote(rows, rows, sems, first_sem + i, (x, y, 1 - c)))
        return out

    def finish_copies(ins, outs, sems):
        x, y, c, _ = _place()
        out = []
        for i, full in enumerate(outs):
            half = full.shape[0] // 2
            mine = full.at[pl.ds(pl.multiple_of(c * half, 8), half), :]
            theirs = full.at[pl.ds(pl.multiple_of((1 - c) * half, 8), half), :]
            k = first_sem + i
            out.append((_remote(mine, mine, sems, k, (x, y, 1 - c)), _remote(theirs, theirs, sems, k, (x, y, 1 - c))))
        return out

    def start(ins, outs, send_sems, recv_sems):
        for cp in copies(ins, outs, (send_sems, recv_sems)):
            cp.start()

    def finish(ins, outs, send_sems, recv_sems):
        for sent, landed in finish_copies(ins, outs, (send_sems, recv_sems)):
            sent.wait_send()
            landed.wait_recv()

    shapes = [jax.ShapeDtypeStruct(f.shape, f.dtype) for f in fulls]
    return _Rider(fulls, shapes, first_sem + len(fulls), start, finish, aliases={i: i for i in range(len(fulls))})


def _join_riders(a, b):
    n_in, n_out = len(a.inputs), len(a.out_shapes)

    def both(which):
        def run(ins, outs, send_sems, recv_sems):
            getattr(a, which)(ins[:n_in], outs[:n_out], send_sems, recv_sems)
            getattr(b, which)(ins[n_in:], outs[n_out:], send_sems, recv_sems)
        return run

    aliases = dict(a.aliases)
    aliases.update({n_in + i: n_out + o for i, o in b.aliases.items()})
    return _Rider(a.inputs + b.inputs, a.out_shapes + b.out_shapes, max(a.n_sems, b.n_sems), both("start"),
                  both("finish"), aliases)


def _all_sum_small(v):
    shape = v.shape

    def body(v_ref, out_ref, buf, send_sems, recv_sems):
        x, y, c, _ = _place()
        me = 4 * x + 2 * y + c
        buf[me] = v_ref[...]
        flips = [(dx, dy, dc) for dx in (0, 1) for dy in (0, 1) for dc in (0, 1)][1:]

        def copy(k, slot, flip):
            dx, dy, dc = flip
            to = (1 - x if dx else x, 1 - y if dy else y, 1 - c if dc else c)
            return pltpu.make_async_remote_copy(src_ref=buf.at[slot], dst_ref=buf.at[slot], send_sem=send_sems.at[k],
                                                recv_sem=recv_sems.at[k], device_id=to, device_id_type=MESH)

        sends = [copy(k, me, flip) for k, flip in enumerate(flips)]
        for cp in sends:
            cp.start()
        for k, (dx, dy, dc) in enumerate(flips):
            sender = 4 * (1 - x if dx else x) + 2 * (1 - y if dy else y) + (1 - c if dc else c)
            copy(k, sender, (dx, dy, dc)).wait_recv()
        for cp in sends:
            cp.wait_send()
        total = buf[0]
        for i in range(1, 8):
            total = total + buf[i]
        out_ref[...] = total

    return pl.pallas_call(
        body, name="all_sum_small",
        in_specs=[pl.BlockSpec(memory_space=pltpu.VMEM)],
        out_specs=pl.BlockSpec(memory_space=pltpu.VMEM),
        out_shape=jax.ShapeDtypeStruct(shape, F32),
        scratch_shapes=[pltpu.VMEM((8,) + shape, F32), pltpu.SemaphoreType.DMA((7,)), pltpu.SemaphoreType.DMA((7,))],
    )(v)


SMALL = (("g_mix", 1024), ("g_ffn", 1024), ("g_out_fox", 512), ("g_out_dil", 512), ("g_q_fox", 64),
         ("g_k_fox", 64), ("g_q_dil", 64), ("g_k_dil", 64), ("b_forget", 8))
SMALL_PACKED = (32, LANES)


def _local_grads(x, target, gains, w1, wft, dense, packed, nb, seq):
    tile2 = lambda g: jnp.tile(g, (1, 2))
    gq_f, gk_f, gq_d, gk_d = (tile2(gains[n]) for n in ("g_q_fox", "g_k_fox", "g_q_dil", "g_k_dil"))
    b_col = gains["b_forget"].reshape(N_FOX_HEADS, 1)
    cos, up, dn = _rope_tables(seq)
    npair = N_FOX_HEADS // 2

    proj, fa_row, h1, h1_t = _in_proj(x, gains["g_mix"], w1, wft)
    c_row = _gate_fwd(fa_row, b_col, seq)
    c3 = c_row.reshape(npair, 2, nb * seq)
    (o_fox, lse_fox), gathered = _fox_fwd(proj, c3, gq_f, gk_f, nb, seq,
                                          rider=None if packed is None else _gather_rider(packed))
    if packed is not None:
        dense = [g.reshape(-1, g.shape[2]) for g in gathered]
    w_out, w_gate, w_up, w_down = dense
    o_dil, lse_dil = _dil_fwd(proj, gq_d, gk_d, cos, up, dn, nb, seq)
    x1, o_n_t = _attn_out(o_fox, o_dil, x, gains["g_out_fox"], gains["g_out_dil"], w_out)
    a, u, dy, loss_parts = _ffn_fwd(x1, target, gains["g_ffn"], w_gate, w_up, w_down)
    loss = jnp.sum(loss_parts[:, 0, 0])

    dx1, s, da, du, h2, dg_ffn = _ffn_bwd(dy, a, u, x1, gains["g_ffn"], w_gate, w_up, w_down)
    d_w_down = _token_matmul(s, dy, "dw_down", 512, False)
    d_w_gate = _token_matmul(da, h2, "dw_gate", 512, False)
    d_w_up = _token_matmul(du, h2, "dw_up", 512, False)
    d_w_out = _token_matmul(o_n_t, dx1, "dw_out", 1024)
    names = ("w_out", "w_gate", "w_up", "w_down")
    grads4 = [g.reshape(4, -1, g.shape[1]) for g in (d_w_out, d_w_gate, d_w_up, d_w_down)]
    exchange = packed is not None
    (do_fox, do_dil, dg_of, dg_od), _ = _attn_out_bwd(
        dx1, o_fox, o_dil, gains["g_out_fox"], gains["g_out_dil"], w_out)
    (dq_f, dk_f, dv_f, dc3, dg_fox), from_sibling = _fox_bwd(
        proj, c3, gq_f, gk_f, do_fox, o_fox, lse_fox, nb, seq,
        rider=_swap_rider(grads4) if exchange else None)
    if exchange:
        sums = [_chip_sum(g, s, "chip_sum_" + n) for g, s, n in zip(grads4, from_sibling, names)]
    (dq_d, dk_d, dv_d, dg_dil), landed = _dil_bwd(
        proj, gq_d, gk_d, cos, up, dn, do_dil, o_dil, lse_dil, nb, seq,
        rider=_spread_rider([st for st, _ in sums]) if exchange else None)
    if exchange:
        halves = [_finish_half(own, ld, "finish_half_" + n) for (_, own), ld, n in zip(sums, landed, names)]
    dfa_row, db = _gate_bwd(dc3.reshape(N_FOX_HEADS, nb * seq), fa_row, b_col, seq)
    dparts = [dq_f, dk_f, dv_f, dq_d, dk_d, dv_d]
    d_w1 = _token_matmul_parts(h1_t, dparts, "dw_in")
    d_wf = _row_matmul(dfa_row, h1, "dw_forget")
    fox_w = 3 * W_GROUP
    in_order = [(d_w1[:, :fox_w], fox_w), (d_wf.T, N_FOX_HEADS), (d_w1[:, fox_w:], d_w1.shape[1] - fox_w)]
    n_cols = d_w1.shape[1] + N_FOX_HEADS
    if exchange:
        shards = [jnp.stack([_pick_columns(in_order, s * n_cols // 4, (s + 1) * n_cols // 4) for s in range(4)])]
        _, from_sibling = _idle_host(_swap_rider(shards), "swap_w_in")
        stage, own = _chip_sum(shards[0], from_sibling[0], "chip_sum_w_in")
        rider = _join_riders(_spread_rider([stage]), _share_rider(halves, first_sem=3))
    (grad_x, dg_mix), rode = _in_proj_bwd(dparts, dfa_row, w1, wft, x, gains["g_mix"], dx1,
                                          rider=rider if exchange else None)
    if exchange:
        d_w_in = _finish_half(own, rode[0], "finish_half_w_in")
        d_w_out, d_w_gate, d_w_up, d_w_down = rode[1:]
    else:
        d_w_in = _pick_columns(in_order, 0, n_cols)

    fold = lambda g2: (g2[:, :HEAD_DIM] + g2[:, HEAD_DIM:])
    small = {
        "g_mix": dg_mix[0:1], "g_ffn": dg_ffn[0:1], "g_out_fox": dg_of[0:1], "g_out_dil": dg_od[0:1],
        "g_q_fox": fold(dg_fox[0:1]), "g_k_fox": fold(dg_fox[1:2]),
        "g_q_dil": fold(dg_dil[0:1]), "g_k_dil": fold(dg_dil[1:2]),
        "b_forget": db[:, 0].reshape(1, N_FOX_HEADS),
    }
    big = {"w_in": d_w_in, "w_out": d_w_out, "w_gate": d_w_gate, "w_up": d_w_up, "w_down": d_w_down}
    return loss, grad_x, big, small


def _pick_columns(pieces, lo, hi):
    out, first = [], 0
    for a, w in pieces:
        a_lo, a_hi = max(lo, first), min(hi, first + w)
        if a_lo < a_hi:
            out.append(a[:, a_lo - first:a_hi - first])
        first += w
    return out[0] if len(out) == 1 else jnp.concatenate(out, axis=1)


def kernel(x, g_mix, w_in, b_forget, g_q_fox, g_k_fox, g_q_dil, g_k_dil, g_out_fox, g_out_dil, w_out, g_ffn, w_gate, w_up, w_down, loss_target, m_g_mix, m_w_in, m_b_forget, m_g_q_fox, m_g_k_fox, m_g_q_dil, m_g_k_dil, m_g_out_fox, m_g_out_dil, m_w_out, m_g_ffn, m_w_gate, m_w_up, m_w_down, v_g_mix, v_w_in, v_b_forget, v_g_q_fox, v_g_k_fox, v_g_q_dil, v_g_k_dil, v_g_out_fox, v_g_out_dil, v_w_out, v_g_ffn, v_w_gate, v_w_up, v_w_down):
    nb, seq, d = x.shape
    weights = dict(g_mix=g_mix, w_in=w_in, b_forget=b_forget, g_q_fox=g_q_fox, g_k_fox=g_k_fox, g_q_dil=g_q_dil,
                   g_k_dil=g_k_dil, g_out_fox=g_out_fox, g_out_dil=g_out_dil, w_out=w_out, g_ffn=g_ffn,
                   w_gate=w_gate, w_up=w_up, w_down=w_down)
    m_in = dict(g_mix=m_g_mix, w_in=m_w_in, b_forget=m_b_forget, g_q_fox=m_g_q_fox, g_k_fox=m_g_k_fox,
                g_q_dil=m_g_q_dil, g_k_dil=m_g_k_dil, g_out_fox=m_g_out_fox, g_out_dil=m_g_out_dil, w_out=m_w_out,
                g_ffn=m_g_ffn, w_gate=m_w_gate, w_up=m_w_up, w_down=m_w_down)
    v_in = dict(g_mix=v_g_mix, w_in=v_w_in, b_forget=v_b_forget, g_q_fox=v_g_q_fox, g_k_fox=v_g_k_fox,
                g_q_dil=v_g_q_dil, g_k_dil=v_g_k_dil, g_out_fox=v_g_out_fox, g_out_dil=v_g_out_dil, w_out=v_w_out,
                g_ffn=v_g_ffn, w_gate=v_w_gate, w_up=v_w_up, w_down=v_w_down)
    order = ["g_mix", "w_in", "b_forget", "g_q_fox", "g_k_fox", "g_q_dil", "g_k_dil", "g_out_fox", "g_out_dil",
             "w_out", "g_ffn", "w_gate", "w_up", "w_down"]

    w_in_all = _gather_weight(w_in, "gather_w_in")
    in_shards = [(w_in_all[s], w_in_all.shape[2]) for s in range(4)]
    fox_w = 3 * W_GROUP
    n_cols = 4 * w_in_all.shape[2]
    w1 = jnp.concatenate([_pick_columns(in_shards, 0, fox_w), _pick_columns(in_shards, fox_w + N_FOX_HEADS, n_cols)],
                         axis=1)
    wft = _pick_columns(in_shards, fox_w, fox_w + N_FOX_HEADS).T
    swap = lambda a: jnp.transpose(a, (0, 2, 1))
    for n in ("w_gate", "w_up"):
        weights[n], m_in[n], v_in[n] = swap(weights[n]), swap(m_in[n]), swap(v_in[n])
    shards = _cast_bf16([weights[n] for n in ("w_out", "w_gate", "w_up", "w_down")], "cast_shards")

    gains = {n: weights[n] for n, _ in SMALL}
    loss, grad_x, big, small = _local_grads(
        x.reshape(nb * seq, d), loss_target.reshape(nb * seq, d), gains, w1, wft, None, shards, nb, seq)

    grads = {n: big[n][None] for n in ("w_out", "w_gate", "w_up", "w_down")}
    packed = jnp.concatenate([small[n].reshape(-1) for n, _ in SMALL] + [loss.reshape(1)])
    packed = jnp.pad(packed, (0, SMALL_PACKED[0] * SMALL_PACKED[1] - packed.shape[0])).reshape(SMALL_PACKED)
    summed = _all_sum_small(packed).reshape(-1)
    pos = 0
    for n, size in SMALL:
        grads[n] = summed[pos:pos + size].reshape(1, size)
        pos += size
    loss = summed[pos]

    to_entry = lambda a: jnp.transpose(a, (2, 0, 1))
    deltas, new_m, new_v, grad_out = {}, {}, {}, {}
    for n in ["w_down"] + [n for n in order if n != "w_down"]:
        rider = _share_rider([big["w_in"]]) if n == "w_down" else None
        (deltas[n], new_m[n], new_v[n]), shared = _adamw(weights[n], grads[n], m_in[n], v_in[n], "adamw_" + n, rider)
        if rider is not None:
            grads["w_in"] = to_entry(shared[0][None])
            weights["w_in"], m_in["w_in"], v_in["w_in"] = (to_entry(a) for a in (w_in, m_w_in, v_w_in))
        grad_out[n] = grads[n]
    for n in ("w_gate", "w_up"):
        grad_out[n], deltas[n], new_m[n], new_v[n] = (swap(a) for a in (grad_out[n], deltas[n], new_m[n], new_v[n]))
    from_entry = lambda a: jnp.transpose(a, (1, 2, 0))
    grad_out["w_in"], deltas["w_in"], new_m["w_in"], new_v["w_in"] = (
        from_entry(a) for a in (grad_out["w_in"], deltas["w_in"], new_m["w_in"], new_v["w_in"]))

    return (loss, grad_x.reshape(nb, seq, d), *[grad_out[n] for n in order], *[deltas[n] for n in order],
            *[new_m[n] for n in order], *[new_v[n] for n in order])
```

```python
import functools
import math

import jax
import jax.numpy as jnp
from jax import lax
from jax.experimental import pallas as pl
from jax.experimental.pallas import tpu as pltpu

F32, BF16 = jnp.float32, jnp.bfloat16
MESH = pl.DeviceIdType.MESH

EPS = 1e-6
NEG = -1e30
HEAD_DIM = 64
SCALE = HEAD_DIM ** -0.5
LOG2E = math.log2(math.e)
LN2 = math.log(2.0)
ROPE_THETA = 500000.0
ROPE_DIM = HEAD_DIM // 4
LANES = 128
W_GROUP = 512
N_FOX_HEADS = 8
VMEM_LIMIT = 56 * 1024 * 1024
DILATIONS = (1, 4, 16)
BAND = 128

ADAM_LR, ADAM_B1, ADAM_B2, ADAM_EPS, ADAM_WD, ADAM_STEP = 0.001, 0.9, 0.999, 1e-08, 0.01, 10

NT = (((1,), (1,)), ((), ()))
TN = (((0,), (0,)), ((), ()))
BATCH_NT = (((2,), (2,)), ((0,), (0,)))
BATCH_NN = (((2,), (1,)), ((0,), (0,)))
BATCH_TN = (((1,), (1,)), ((0,), (0,)))


def _params(sem=None):
    return pltpu.CompilerParams(dimension_semantics=sem, vmem_limit_bytes=VMEM_LIMIT)


def _dot(a, b, dims=None):
    if dims is None:
        return jnp.dot(a, b, preferred_element_type=F32)
    return lax.dot_general(a, b, dims, preferred_element_type=F32)


def _group_ones():
    i = lax.broadcasted_iota(jnp.int32, (LANES, LANES), 0) >> 6
    j = lax.broadcasted_iota(jnp.int32, (LANES, LANES), 1) >> 6
    return (i == j).astype(BF16)


def _split3(x):
    a = x.astype(BF16)
    r = x - a.astype(F32)
    b = r.astype(BF16)
    c = (r - b.astype(F32)).astype(BF16)
    return a, b, c


def _groupsum(x, ones, pieces=2):
    total = None
    for _ in range(pieces):
        piece = x.astype(BF16)
        part = _dot(piece, ones)
        total = part if total is None else total + part
        x = x - piece.astype(F32)
    return total


def _head_masks():
    lane = lax.broadcasted_iota(jnp.int32, (1, LANES), 1)
    return [(lane < HEAD_DIM).astype(F32), (lane >= HEAD_DIM).astype(F32)]


def _head_norm(raw, ones):
    r = lax.rsqrt(_groupsum(raw * raw, ones, 1) * (1.0 / HEAD_DIM) + EPS)
    return raw * r, r


def _head_norm_bwd(dy, xhat, r, gain, ones):
    u = dy * gain
    dgain = jnp.sum(dy * xhat, axis=0, keepdims=True)
    draw = r * (u - xhat * (_groupsum(u * xhat, ones) * (1.0 / HEAD_DIM)))
    return draw, dgain


def _rope(x, cos, s_up, s_dn):
    return x * cos + pltpu.roll(x, LANES - 8, 1) * s_up + pltpu.roll(x, 8, 1) * s_dn


def _rope_bwd(dy, cos, s_up, s_dn):
    return dy * cos + pltpu.roll(dy * s_up, 8, 1) + pltpu.roll(dy * s_dn, LANES - 8, 1)


def _rope_tables(seq):
    half = ROPE_DIM // 2
    inv_freq = jnp.power(jnp.float32(ROPE_THETA), -jnp.arange(half, dtype=F32) * 2.0 / ROPE_DIM)
    ang = jnp.arange(seq).astype(F32)[:, None] * inv_freq[None, :]
    cos, sin = jnp.cos(ang), jnp.sin(ang)
    one = jnp.ones((seq, HEAD_DIM - ROPE_DIM), F32)
    zero_h = jnp.zeros((seq, half), F32)
    zero_r = jnp.zeros((seq, HEAD_DIM - ROPE_DIM), F32)
    c = jnp.concatenate([cos, cos, one], axis=1)
    up = jnp.concatenate([-sin, zero_h, zero_r], axis=1)
    dn = jnp.concatenate([zero_h, sin, zero_r], axis=1)
    return jnp.tile(c, (1, 2)), jnp.tile(up, (1, 2)), jnp.tile(dn, (1, 2))


def _row_tile(rows, cap=256):
    best = rows
    for t in range(8, min(rows, cap) + 1, 8):
        if rows % t == 0:
            best = t
    return best


class _Rider:
    def __init__(self, inputs, out_shapes, n_sems, start, finish, aliases=None, middle=None):
        self.inputs, self.out_shapes, self.n_sems = list(inputs), list(out_shapes), n_sems
        self.start, self.finish, self.middle, self.aliases = start, finish, middle, dict(aliases or {})


def _host_call(body, rider, *, name, grid, in_specs, out_specs, out_shape, scratch_shapes, inputs, semantics):
    if rider is None:
        return pl.pallas_call(body, name=name, grid=grid, in_specs=in_specs, out_specs=out_specs,
                              out_shape=out_shape, scratch_shapes=scratch_shapes,
                              compiler_params=_params(semantics))(*inputs), []
    n_in, n_out, n_scr = len(in_specs), len(out_specs), len(scratch_shapes)
    r_in, r_out = len(rider.inputs), len(rider.out_shapes)

    def wrapped(*refs):
        ins, refs = refs[:n_in], refs[n_in:]
        r_ins, refs = refs[:r_in], refs[r_in:]
        outs, refs = refs[:n_out], refs[n_out:]
        r_outs, refs = refs[:r_out], refs[r_out:]
        scratch, (send_sems, recv_sems) = refs[:n_scr], refs[n_scr:]
        ids = [pl.program_id(a) for a in range(len(grid))]
        first = functools.reduce(lambda p, q: p & q, [i == 0 for i in ids])
        last = functools.reduce(lambda p, q: p & q, [i == g - 1 for i, g in zip(ids, grid)])

        @pl.when(first)
        def _():
            rider.start(r_ins, r_outs, send_sems, recv_sems)

        body(*ins, *outs, *scratch)

        if rider.middle is not None:
            step, steps = ids[0], grid[0]
            for i, g in zip(ids[1:], grid[1:]):
                step, steps = step * g + i, steps * g

            @pl.when(step == (3 * steps) // 4)
            def _():
                rider.middle(r_ins, r_outs, send_sems, recv_sems)

        @pl.when(last)
        def _():
            rider.finish(r_ins, r_outs, send_sems, recv_sems)

    hbm = pl.BlockSpec(memory_space=pl.ANY)
    res = pl.pallas_call(
        wrapped, name=name, grid=grid,
        in_specs=list(in_specs) + [hbm] * r_in, out_specs=list(out_specs) + [hbm] * r_out,
        out_shape=list(out_shape) + rider.out_shapes,
        scratch_shapes=list(scratch_shapes) + [pltpu.SemaphoreType.DMA((rider.n_sems,))] * 2,
        input_output_aliases={n_in + i: n_out + o for i, o in rider.aliases.items()},
        compiler_params=_params(semantics),
    )(*inputs, *rider.inputs)
    return res[:n_out], res[n_out:]


def _idle_host(rider, name):
    def body(o_ref):
        o_ref[...] = jnp.zeros_like(o_ref)

    return _host_call(body, rider, name=name, grid=(1,), in_specs=[],
                      out_specs=[pl.BlockSpec((8, LANES), lambda i: (0, 0))],
                      out_shape=[jax.ShapeDtypeStruct((8, LANES), F32)], scratch_shapes=[], inputs=(),
                      semantics=("arbitrary",))


def _in_proj(x, g_mix, w1, wft):
    t, d = x.shape
    n = w1.shape[1]
    tt = 512

    def body(x_ref, g_ref, w_ref, wf_ref, p_ref, fa_ref, h_ref, ht_ref):
        xx = x_ref[...]
        r = lax.rsqrt(jnp.mean(xx * xx, axis=-1, keepdims=True) + EPS)
        h = (xx * r * g_ref[...]).astype(BF16)
        h_ref[...] = h
        ht_ref[...] = h.T
        for j in range(n // W_GROUP):
            cols = slice(j * W_GROUP, (j + 1) * W_GROUP)
            p_ref[:, cols] = _dot(h, w_ref[:, cols]).astype(BF16)
        fa_ref[...] = _dot(wf_ref[...], h, NT)

    return pl.pallas_call(
        body, name="in_proj", grid=(t // tt,),
        in_specs=[pl.BlockSpec((tt, d), lambda i: (i, 0)), pl.BlockSpec((1, d), lambda i: (0, 0)),
                  pl.BlockSpec(memory_space=pltpu.VMEM), pl.BlockSpec(memory_space=pltpu.VMEM)],
        out_specs=[pl.BlockSpec((tt, n), lambda i: (i, 0)), pl.BlockSpec((8, tt), lambda i: (0, i)),
                   pl.BlockSpec((tt, d), lambda i: (i, 0)), pl.BlockSpec((d, tt), lambda i: (0, i))],
        out_shape=[jax.ShapeDtypeStruct((t, n), BF16), jax.ShapeDtypeStruct((8, t), F32),
                   jax.ShapeDtypeStruct((t, d), BF16), jax.ShapeDtypeStruct((d, t), BF16)],
        compiler_params=_params(("arbitrary",)),
    )(x, g_mix, w1, wft)


def _tri(n, upper):
    i = lax.broadcasted_iota(jnp.int32, (n, n), 0)
    j = lax.broadcasted_iota(jnp.int32, (n, n), 1)
    return ((i <= j) if upper else (i >= j)).astype(BF16)


def _gate_fwd(fa_row, b_col, seq):
    t = fa_row.shape[1]
    cb = 256

    def body(fa_ref, b_ref, c_ref):
        tri = _tri(cb, True)
        carry = jnp.zeros((8, 1), F32)
        for k in range(seq // cb):
            z = fa_ref[:, k * cb:(k + 1) * cb] + b_ref[...]
            lf = jnp.minimum(z, 0.0) - jnp.log(1.0 + jnp.exp(-jnp.abs(z)))
            a, b, c = _split3(lf)
            blk = _dot(a, tri) + _dot(b, tri) + _dot(c, tri) + carry
            c_ref[:, k * cb:(k + 1) * cb] = blk
            carry = blk[:, cb - 1:cb]

    return pl.pallas_call(
        body, name="gate_fwd", grid=(t // seq,),
        in_specs=[pl.BlockSpec((8, seq), lambda i: (0, i)), pl.BlockSpec((8, 1), lambda i: (0, 0))],
        out_specs=pl.BlockSpec((8, seq), lambda i: (0, i)),
        out_shape=jax.ShapeDtypeStruct((8, t), F32),
        compiler_params=_params(("arbitrary",)),
    )(fa_row, b_col)


def _gate_bwd(dc_row, fa_row, b_col, seq):
    t = fa_row.shape[1]
    cb = 256

    def body(dc_ref, fa_ref, b_ref, dfa_ref, db_ref):
        @pl.when(pl.program_id(0) == 0)
        def _():
            db_ref[...] = jnp.zeros_like(db_ref)

        tri = _tri(cb, False)
        carry = jnp.zeros((8, 1), F32)
        dbs = jnp.zeros((8, 1), F32)
        for k in reversed(range(seq // cb)):
            a, b, c = _split3(dc_ref[:, k * cb:(k + 1) * cb])
            dlf = _dot(a, tri) + _dot(b, tri) + _dot(c, tri) + carry
            carry = dlf[:, 0:1]
            z = fa_ref[:, k * cb:(k + 1) * cb] + b_ref[...]
            dfa = dlf / (1.0 + jnp.exp(z))
            dfa_ref[:, k * cb:(k + 1) * cb] = dfa
            dbs = dbs + jnp.sum(dfa, axis=1, keepdims=True)
        db_ref[...] += jnp.broadcast_to(dbs, (8, LANES))

    return pl.pallas_call(
        body, name="gate_bwd", grid=(t // seq,),
        in_specs=[pl.BlockSpec((8, seq), lambda i: (0, i)), pl.BlockSpec((8, seq), lambda i: (0, i)),
                  pl.BlockSpec((8, 1), lambda i: (0, 0))],
        out_specs=[pl.BlockSpec((8, seq), lambda i: (0, i)), pl.BlockSpec((8, LANES), lambda i: (0, 0))],
        out_shape=[jax.ShapeDtypeStruct((8, t), F32), jax.ShapeDtypeStruct((8, LANES), F32)],
        compiler_params=_params(("arbitrary",)),
    )(dc_row, fa_row, b_col)


def _attn_out(o_fox, o_dil, x, g_fox, g_dil, w_out):
    t, d = x.shape
    w = o_fox.shape[1]
    tt = 512

    def body(of_ref, od_ref, x_ref, gf_ref, gd_ref, w_ref, x1_ref, ont_ref):
        acc = x_ref[...]
        for k, (o_ref, g_ref) in enumerate(((of_ref, gf_ref), (od_ref, gd_ref))):
            o = o_ref[...]
            r = lax.rsqrt(jnp.mean(o * o, axis=-1, keepdims=True) + EPS)
            on = (o * r * g_ref[...]).astype(BF16)
            ont_ref[k * w:(k + 1) * w, :] = on.T
            acc = acc + _dot(on, w_ref[k * w:(k + 1) * w, :])
        x1_ref[...] = acc

    return pl.pallas_call(
        body, name="attn_out", grid=(t // tt,),
        in_specs=[pl.BlockSpec((tt, w), lambda i: (i, 0)), pl.BlockSpec((tt, w), lambda i: (i, 0)),
                  pl.BlockSpec((tt, d), lambda i: (i, 0)), pl.BlockSpec((1, w), lambda i: (0, 0)),
                  pl.BlockSpec((1, w), lambda i: (0, 0)), pl.BlockSpec(memory_space=pltpu.VMEM)],
        out_specs=[pl.BlockSpec((tt, d), lambda i: (i, 0)), pl.BlockSpec((2 * w, tt), lambda i: (0, i))],
        out_shape=[jax.ShapeDtypeStruct((t, d), F32), jax.ShapeDtypeStruct((2 * w, t), BF16)],
        compiler_params=_params(("arbitrary",)),
    )(o_fox, o_dil, x, g_fox, g_dil, w_out)


def _attn_out_bwd(dx1, o_fox, o_dil, g_fox, g_dil, w_out, rider=None):
    t, d = dx1.shape
    w = o_fox.shape[1]
    tt = 512

    def body(dx_ref, of_ref, od_ref, gf_ref, gd_ref, w_ref, dof_ref, dod_ref, dgf_ref, dgd_ref):
        @pl.when(pl.program_id(0) == 0)
        def _():
            dgf_ref[...] = jnp.zeros_like(dgf_ref)
            dgd_ref[...] = jnp.zeros_like(dgd_ref)

        dxb = dx_ref[...].astype(BF16)
        for k, (o_ref, g_ref, do_ref, dg_ref) in enumerate(
                ((of_ref, gf_ref, dof_ref, dgf_ref), (od_ref, gd_ref, dod_ref, dgd_ref))):
            don = _dot(dxb, w_ref[k * w:(k + 1) * w, :], NT)
            o = o_ref[...]
            r = lax.rsqrt(jnp.mean(o * o, axis=-1, keepdims=True) + EPS)
            xhat = o * r
            u = don * g_ref[...]
            do_ref[...] = r * (u - xhat * jnp.mean(u * xhat, axis=-1, keepdims=True))
            dg_ref[0:1, :] += jnp.sum(don * xhat, axis=0, keepdims=True)

    return _host_call(
        body, rider, name="attn_out_bwd", grid=(t // tt,),
        in_specs=[pl.BlockSpec((tt, d), lambda i: (i, 0)), pl.BlockSpec((tt, w), lambda i: (i, 0)),
                  pl.BlockSpec((tt, w), lambda i: (i, 0)), pl.BlockSpec((1, w), lambda i: (0, 0)),
                  pl.BlockSpec((1, w), lambda i: (0, 0)), pl.BlockSpec(memory_space=pltpu.VMEM)],
        out_specs=[pl.BlockSpec((tt, w), lambda i: (i, 0)), pl.BlockSpec((tt, w), lambda i: (i, 0)),
                   pl.BlockSpec((8, w), lambda i: (0, 0)), pl.BlockSpec((8, w), lambda i: (0, 0))],
        out_shape=[jax.ShapeDtypeStruct((t, w), F32), jax.ShapeDtypeStruct((t, w), F32),
                   jax.ShapeDtypeStruct((8, w), F32), jax.ShapeDtypeStruct((8, w), F32)],
        scratch_shapes=[], inputs=(dx1, o_fox, o_dil, g_fox, g_dil, w_out), semantics=("arbitrary",))


def _ffn_fwd(x1, target, g_ffn, w_gate, w_up, w_down):
    t, d = x1.shape
    f = w_gate.shape[0]
    tt = 256

    def body(x_ref, t_ref, g_ref, wg_ref, wu_ref, wd_ref, a_ref, u_ref, dy_ref, loss_ref):
        xx = x_ref[...]
        r = lax.rsqrt(jnp.mean(xx * xx, axis=-1, keepdims=True) + EPS)
        h = (xx * r * g_ref[...]).astype(BF16)
        a = _dot(h, wg_ref[...], NT)
        u = _dot(h, wu_ref[...], NT)
        a_ref[...] = a.astype(BF16)
        u_ref[...] = u.astype(BF16)
        s = (a / (1.0 + jnp.exp(-a)) * u).astype(BF16)
        y = xx + _dot(s, wd_ref[...])
        e = y - t_ref[...]
        dy_ref[...] = e * (1.0 / d)
        loss_ref[...] = jnp.broadcast_to(0.5 * jnp.sum(e * e) * (1.0 / d), (1, 8, LANES))

    return pl.pallas_call(
        body, name="ffn_fwd", grid=(t // tt,),
        in_specs=[pl.BlockSpec((tt, d), lambda i: (i, 0)), pl.BlockSpec((tt, d), lambda i: (i, 0)),
                  pl.BlockSpec((1, d), lambda i: (0, 0)), pl.BlockSpec(memory_space=pltpu.VMEM),
                  pl.BlockSpec(memory_space=pltpu.VMEM), pl.BlockSpec(memory_space=pltpu.VMEM)],
        out_specs=[pl.BlockSpec((tt, f), lambda i: (i, 0)), pl.BlockSpec((tt, f), lambda i: (i, 0)),
                   pl.BlockSpec((tt, d), lambda i: (i, 0)), pl.BlockSpec((1, 8, LANES), lambda i: (i, 0, 0))],
        out_shape=[jax.ShapeDtypeStruct((t, f), BF16), jax.ShapeDtypeStruct((t, f), BF16),
                   jax.ShapeDtypeStruct((t, d), F32), jax.ShapeDtypeStruct((t // tt, 8, LANES), F32)],
        compiler_params=_params(("arbitrary",)),
    )(x1, target, g_ffn, w_gate, w_up, w_down)


def _ffn_bwd(dy, a, u, x1, g_ffn, w_gate, w_up, w_down):
    t, d = x1.shape
    f = w_gate.shape[0]
    tt = 256

    def body(dy_ref, a_ref, u_ref, x_ref, g_ref, wg_ref, wu_ref, wd_ref,
             dx_ref, s_ref, da_ref, du_ref, h_ref, dg_ref):
        @pl.when(pl.program_id(0) == 0)
        def _():
            dg_ref[...] = jnp.zeros_like(dg_ref)

        dy_ = dy_ref[...]
        ds = _dot(dy_.astype(BF16), wd_ref[...], NT)
        a_ = a_ref[...].astype(F32)
        u_ = u_ref[...].astype(F32)
        sig = 1.0 / (1.0 + jnp.exp(-a_))
        silu = a_ * sig
        s_ref[...] = (silu * u_).astype(BF16)
        da = (ds * u_ * (sig * (1.0 + a_ * (1.0 - sig)))).astype(BF16)
        du = (ds * silu).astype(BF16)
        da_ref[...] = da
        du_ref[...] = du
        dh = _dot(da, wg_ref[...]) + _dot(du, wu_ref[...])
        xx = x_ref[...]
        r = lax.rsqrt(jnp.mean(xx * xx, axis=-1, keepdims=True) + EPS)
        xhat = xx * r
        g = g_ref[...]
        h_ref[...] = (xhat * g).astype(BF16)
        uu = dh * g
        dx_ref[...] = dy_ + r * (uu - xhat * jnp.mean(uu * xhat, axis=-1, keepdims=True))
        dg_ref[0:1, :] += jnp.sum(dh * xhat, axis=0, keepdims=True)

    return pl.pallas_call(
        body, name="ffn_bwd", grid=(t // tt,),
        in_specs=[pl.BlockSpec((tt, d), lambda i: (i, 0)), pl.BlockSpec((tt, f), lambda i: (i, 0)),
                  pl.BlockSpec((tt, f), lambda i: (i, 0)), pl.BlockSpec((tt, d), lambda i: (i, 0)),
                  pl.BlockSpec((1, d), lambda i: (0, 0)), pl.BlockSpec(memory_space=pltpu.VMEM),
                  pl.BlockSpec(memory_space=pltpu.VMEM), pl.BlockSpec(memory_space=pltpu.VMEM)],
        out_specs=[pl.BlockSpec((tt, d), lambda i: (i, 0)), pl.BlockSpec((tt, f), lambda i: (i, 0)),
                   pl.BlockSpec((tt, f), lambda i: (i, 0)), pl.BlockSpec((tt, f), lambda i: (i, 0)),
                   pl.BlockSpec((tt, d), lambda i: (i, 0)), pl.BlockSpec((8, d), lambda i: (0, 0))],
        out_shape=[jax.ShapeDtypeStruct((t, d), F32), jax.ShapeDtypeStruct((t, f), BF16),
                   jax.ShapeDtypeStruct((t, f), BF16), jax.ShapeDtypeStruct((t, f), BF16),
                   jax.ShapeDtypeStruct((t, d), BF16), jax.ShapeDtypeStruct((8, d), F32)],
        compiler_params=_params(("arbitrary",)),
    )(dy, a, u, x1, g_ffn, w_gate, w_up, w_down)


def _in_proj_bwd(dparts, dfa_row, w1, wft, x, g_mix, dx1, rider=None):
    t, d = x.shape
    tt = 512
    npart = len(dparts)

    def body(*refs):
        dp_refs = refs[:npart]
        dfa_ref, w_ref, wf_ref, x_ref, g_ref, dx1_ref, dx_ref, dg_ref = refs[npart:]

        @pl.when(pl.program_id(0) == 0)
        def _():
            dg_ref[...] = jnp.zeros_like(dg_ref)

        dh = _dot(dfa_ref[...].astype(BF16), wf_ref[...], TN)
        for j in range(npart):
            dh = dh + _dot(dp_refs[j][...], w_ref[:, j * W_GROUP:(j + 1) * W_GROUP], NT)
        xx = x_ref[...]
        r = lax.rsqrt(jnp.mean(xx * xx, axis=-1, keepdims=True) + EPS)
        xhat = xx * r
        uu = dh * g_ref[...]
        dx_ref[...] = dx1_ref[...] + r * (uu - xhat * jnp.mean(uu * xhat, axis=-1, keepdims=True))
        dg_ref[0:1, :] += jnp.sum(dh * xhat, axis=0, keepdims=True)

    return _host_call(
        body, rider, name="in_proj_bwd", grid=(t // tt,),
        in_specs=[pl.BlockSpec((tt, W_GROUP), lambda i: (i, 0)) for _ in range(npart)]
        + [pl.BlockSpec((8, tt), lambda i: (0, i)), pl.BlockSpec(memory_space=pltpu.VMEM),
           pl.BlockSpec(memory_space=pltpu.VMEM), pl.BlockSpec((tt, d), lambda i: (i, 0)),
           pl.BlockSpec((1, d), lambda i: (0, 0)), pl.BlockSpec((tt, d), lambda i: (i, 0))],
        out_specs=[pl.BlockSpec((tt, d), lambda i: (i, 0)), pl.BlockSpec((8, d), lambda i: (0, 0))],
        out_shape=[jax.ShapeDtypeStruct((t, d), F32), jax.ShapeDtypeStruct((8, d), F32)],
        scratch_shapes=[], inputs=(*dparts, dfa_row, w1, wft, x, g_mix, dx1), semantics=("arbitrary",))


def _token_matmul(a, b, name, tn, a_is_transposed=True):
    m, t = a.shape if a_is_transposed else a.shape[::-1]
    n = b.shape[1]
    tk = 1024

    def body(a_ref, b_ref, o_ref):
        @pl.when(pl.program_id(1) == 0)
        def _():
            o_ref[...] = jnp.zeros_like(o_ref)

        o_ref[...] += _dot(a_ref[...], b_ref[...].astype(BF16), None if a_is_transposed else TN)

    a_spec = pl.BlockSpec((m, tk), lambda j, k: (0, k)) if a_is_transposed else pl.BlockSpec((tk, m), lambda j, k: (k, 0))
    return pl.pallas_call(
        body, name=name, grid=(n // tn, t // tk),
        in_specs=[a_spec, pl.BlockSpec((tk, tn), lambda j, k: (k, j))],
        out_specs=pl.BlockSpec((m, tn), lambda j, k: (0, j)),
        out_shape=jax.ShapeDtypeStruct((m, n), F32),
        compiler_params=_params(("arbitrary", "arbitrary")),
    )(a, b)


def _token_matmul_parts(at, parts, name):
    m, t = at.shape
    widths = [p.shape[1] for p in parts]
    tk = 1024

    def body(a_ref, *refs):
        o_ref = refs[-1]

        @pl.when(pl.program_id(0) == 0)
        def _():
            o_ref[...] = jnp.zeros_like(o_ref)

        a, first = a_ref[...], 0
        for b_ref, w in zip(refs[:-1], widths):
            o_ref[:, first:first + w] += _dot(a, b_ref[...])
            first += w

    return pl.pallas_call(
        body, name=name, grid=(t // tk,),
        in_specs=[pl.BlockSpec((m, tk), lambda k: (0, k))] + [pl.BlockSpec((tk, w), lambda k: (k, 0)) for w in widths],
        out_specs=pl.BlockSpec((m, sum(widths)), lambda k: (0, 0)),
        out_shape=jax.ShapeDtypeStruct((m, sum(widths)), F32),
        compiler_params=_params(("arbitrary",)),
    )(at, *parts)


def _row_matmul(a_row, b, name):
    t, n = b.shape
    tk = 1024
    nk = t // tk

    def body(a_ref, b_ref, o_ref):
        @pl.when(pl.program_id(0) == 0)
        def _():
            o_ref[...] = jnp.zeros_like(o_ref)

        o_ref[...] += _dot(a_ref[...].astype(BF16), b_ref[...])

    return pl.pallas_call(
        body, name=name, grid=(nk,),
        in_specs=[pl.BlockSpec((8, tk), lambda k: (0, k)), pl.BlockSpec((tk, n), lambda k: (k, 0))],
        out_specs=pl.BlockSpec((8, n), lambda k: (0, 0)),
        out_shape=jax.ShapeDtypeStruct((8, n), F32),
        compiler_params=_params(("arbitrary",)),
    )(a_row, b)


FOX_TQ = 512
SUM_LANE = (HEAD_DIM, 0)


def _fox_fwd(proj, c3, gq, gk, nb, seq, rider=None):
    t = nb * seq
    tq = FOX_TQ
    nq = seq // tq
    npair = N_FOX_HEADS // 2

    def body(q_ref, k_ref, v_ref, c_ref, gq_ref, gk_ref, o_ref, lse_ref, qs, ks, vs):
        ones = _group_ones()
        masks = _head_masks()
        qhat, _ = _head_norm(q_ref[...].astype(F32), ones)
        khat, _ = _head_norm(k_ref[...].astype(F32), ones)
        qs[...] = (qhat * gq_ref[...] * (SCALE * LOG2E)).astype(BF16)
        kn = khat * gk_ref[...]
        vv = v_ref[...].astype(F32)
        lane = lax.broadcasted_iota(jnp.int32, (1, LANES), 1)
        for hd in range(2):
            ks[hd] = (kn * masks[hd]).astype(BF16)
            vs[hd] = (vv * masks[hd] + (lane == SUM_LANE[hd]).astype(F32)).astype(BF16)
        row = lax.broadcasted_iota(jnp.int32, (tq, tq), 0)
        col = lax.broadcasted_iota(jnp.int32, (tq, tq), 1)
        causal = col <= row

        for qi in range(nq):
            q0 = qi * tq
            q_blk = qs[q0:q0 + tq, :]
            o_tot = jnp.zeros((tq, LANES), F32)
            lse_tot = jnp.zeros((tq, LANES), F32)
            for hd in range(2):
                crow = c_ref[0, hd:hd + 1, 0:q0 + tq] * LOG2E
                c0 = crow[:, q0:q0 + 1]
                s_d = _dot(q_blk, ks[hd, q0:q0 + tq, :], NT) + (c0 - crow[:, q0:q0 + tq])
                s_d = jnp.where(causal, s_d, NEG)
                m = jnp.max(s_d, axis=-1, keepdims=True)
                if qi > 0:
                    s_o = _dot(q_blk, ks[hd, 0:q0, :], NT) + (c0 - crow[:, 0:q0])
                    m = jnp.maximum(m, jnp.max(s_o, axis=-1, keepdims=True))
                acc = _dot(jnp.exp2(s_d - m).astype(BF16), vs[hd, q0:q0 + tq, :])
                if qi > 0:
                    acc = acc + _dot(jnp.exp2(s_o - m).astype(BF16), vs[hd, 0:q0, :])
                l = acc[:, SUM_LANE[hd]:SUM_LANE[hd] + 1]
                o_tot = o_tot + (acc / l) * masks[hd]
                lse_tot = lse_tot + (m + jnp.log2(l) - c0) * masks[hd]
            o_ref[q0:q0 + tq, :] = o_tot
            lse_ref[q0:q0 + tq, :] = lse_tot

    blk = lambda off: pl.BlockSpec((seq, LANES), lambda b, p: (b, off + p))
    return _host_call(
        body, rider, name="fox_fwd", grid=(nb, npair),
        in_specs=[blk(0), blk(npair), blk(2 * npair), pl.BlockSpec((1, 2, seq), lambda b, p: (p, 0, b)),
                  pl.BlockSpec((1, LANES), lambda b, p: (0, 0)), pl.BlockSpec((1, LANES), lambda b, p: (0, 0))],
        out_specs=[blk(0), blk(0)],
        out_shape=[jax.ShapeDtypeStruct((t, W_GROUP), F32), jax.ShapeDtypeStruct((t, W_GROUP), F32)],
        scratch_shapes=[pltpu.VMEM((seq, LANES), BF16), pltpu.VMEM((2, seq, LANES), BF16),
                        pltpu.VMEM((2, seq, LANES), BF16)],
        inputs=(proj, proj, proj, c3, gq, gk), semantics=("arbitrary", "arbitrary"))


def _fox_bwd(proj, c3, gq, gk, do, o, lse, nb, seq, rider=None):
    t = nb * seq
    tq = FOX_TQ
    nq = seq // tq
    npair = N_FOX_HEADS // 2

    def body(q_ref, k_ref, v_ref, c_ref, gq_ref, gk_ref, do_ref, o_ref, lse_ref,
             dq_ref, dk_ref, dv_ref, dc_ref, dg_ref, qs, ks, vs, kts, dos, lse_t, delta_t, dqt_acc, dk_acc, dv_acc,
             row_sum):
        @pl.when((pl.program_id(0) == 0) & (pl.program_id(1) == 0))
        def _():
            dg_ref[...] = jnp.zeros_like(dg_ref)

        ones = _group_ones()
        masks = _head_masks()
        qhat, rq = _head_norm(q_ref[...].astype(F32), ones)
        khat, rk = _head_norm(k_ref[...].astype(F32), ones)
        qs[...] = (qhat * gq_ref[...] * (SCALE * LOG2E)).astype(BF16)
        kn = khat * gk_ref[...]
        vv = v_ref[...].astype(F32)
        for hd in range(2):
            ks[hd] = (kn * masks[hd]).astype(BF16)
            vs[hd] = (vv * masks[hd]).astype(BF16)
            kts[hd] = ks[hd].T
        dof = do_ref[...]
        dos[...] = dof.astype(BF16)
        lse_t[...] = lse_ref[...].T
        delta_t[...] = _groupsum(dof * o_ref[...], ones).T
        dqt_acc[...] = jnp.zeros_like(dqt_acc)
        dk_acc[...] = jnp.zeros_like(dk_acc)
        dv_acc[...] = jnp.zeros_like(dv_acc)
        row_sum[...] = jnp.zeros_like(row_sum)
        key = lax.broadcasted_iota(jnp.int32, (tq, tq), 0)
        qry = lax.broadcasted_iota(jnp.int32, (tq, tq), 1)
        causal = key <= qry

        for hd in range(2):
            lane0 = hd * HEAD_DIM
            for kj in range(nq):
                k0 = kj * tq
                k_blk = ks[hd, k0:k0 + tq, :]
                v_blk = vs[hd, k0:k0 + tq, :]
                kt_blk = kts[hd, :, k0:k0 + tq]
                crow = c_ref[0, hd:hd + 1, k0:k0 + tq] * LOG2E
                ck0 = crow[:, 0:1]
                bias = jnp.broadcast_to(ck0 - crow, (LANES, tq)).T[:, 0:1]

                def queries_step(r0, r1, diag, hd=hd, lane0=lane0, k_blk=k_blk, v_blk=v_blk, kt_blk=kt_blk,
                                 bias=bias, ck0=ck0):
                    q_r = qs[r0:r1, :]
                    do_r = dos[r0:r1, :]
                    z = _dot(k_blk, q_r, NT) + bias
                    p = jnp.exp2(z - (lse_t[lane0:lane0 + 1, r0:r1] + ck0))
                    if diag:
                        p = jnp.where(causal, p, 0.0)
                    dp = _dot(v_blk, do_r, NT)
                    ds = p * (dp - delta_t[lane0:lane0 + 1, r0:r1])
                    dsb = ds.astype(BF16)
                    dqt_acc[:, r0:r1] += _dot(kt_blk, dsb)
                    row_sum[hd:hd + 1, r0:r1] += jnp.sum(ds, axis=0, keepdims=True)
                    return _dot(dsb, q_r), _dot(p.astype(BF16), do_r), -jnp.sum(ds, axis=1, keepdims=True)

                dk_j, dv_j, dc_j = queries_step(k0, k0 + tq, True)
                if k0 + tq < seq:
                    dk_o, dv_o, dc_o = queries_step(k0 + tq, seq, False)
                    dk_j, dv_j, dc_j = dk_j + dk_o, dv_j + dv_o, dc_j + dc_o
                dk_acc[k0:k0 + tq, :] += dk_j * masks[hd]
                dv_acc[k0:k0 + tq, :] += dv_j * masks[hd]
                dc_ref[0, hd:hd + 1, k0:k0 + tq] = jnp.broadcast_to(dc_j, (tq, LANES)).T[0:1, :]

        dc_ref[0] += row_sum[0:2, :]

        dq_raw, dgq = _head_norm_bwd(dqt_acc[...].T * SCALE, qhat, rq, gq_ref[...], ones)
        dk_raw, dgk = _head_norm_bwd(dk_acc[...] * LN2, khat, rk, gk_ref[...], ones)
        dq_ref[...] = dq_raw.astype(BF16)
        dk_ref[...] = dk_raw.astype(BF16)
        dv_ref[...] = dv_acc[...].astype(BF16)
        dg_ref[0:1, :] += dgq
        dg_ref[1:2, :] += dgk

    blk = lambda off: pl.BlockSpec((seq, LANES), lambda b, p: (b, off + p))
    vec = pl.BlockSpec((1, LANES), lambda b, p: (0, 0))
    c_spec = pl.BlockSpec((1, 2, seq), lambda b, p: (p, 0, b))
    return _host_call(
        body, rider, name="fox_bwd", grid=(nb, npair),
        in_specs=[blk(0), blk(npair), blk(2 * npair), c_spec, vec, vec, blk(0), blk(0), blk(0)],
        out_specs=[blk(0), blk(0), blk(0), c_spec, pl.BlockSpec((8, LANES), lambda b, p: (0, 0))],
        out_shape=[jax.ShapeDtypeStruct((t, W_GROUP), BF16), jax.ShapeDtypeStruct((t, W_GROUP), BF16),
                   jax.ShapeDtypeStruct((t, W_GROUP), BF16), jax.ShapeDtypeStruct((npair, 2, t), F32),
                   jax.ShapeDtypeStruct((8, LANES), F32)],
        scratch_shapes=[pltpu.VMEM((seq, LANES), BF16), pltpu.VMEM((2, seq, LANES), BF16),
                        pltpu.VMEM((2, seq, LANES), BF16), pltpu.VMEM((2, LANES, seq), BF16),
                        pltpu.VMEM((seq, LANES), BF16), pltpu.VMEM((LANES, seq), F32),
                        pltpu.VMEM((LANES, seq), F32), pltpu.VMEM((LANES, seq), F32),
                        pltpu.VMEM((seq, LANES), F32), pltpu.VMEM((seq, LANES), F32),
                        pltpu.VMEM((8, seq), F32)],
        inputs=(proj, proj, proj, c3, gq, gk, do, o, lse), semantics=("arbitrary", "arbitrary"))


def _dil_prep(q_ref, k_ref, gq_ref, gk_ref, cos_ref, up_ref, dn_ref, ones):
    qhat, rq = _head_norm(q_ref[...].astype(F32), ones)
    khat, rk = _head_norm(k_ref[...].astype(F32), ones)
    cos, up, dn = cos_ref[...], up_ref[...], dn_ref[...]
    qn = _rope(qhat * gq_ref[...], cos, up, dn) * (SCALE * LOG2E)
    kn = _rope(khat * gk_ref[...], cos, up, dn)
    return qhat, rq, khat, rk, qn, kn


def _dil_keys(d, seq, pairs):
    nblk = seq // BAND
    per_res = seq // (d * BAND)
    as_blocks = lambda ref, rows: ref[rows, :].reshape(-1, BAND, LANES)
    if per_res == 1:
        a = lax.broadcasted_iota(jnp.int32, (1, BAND, BAND), 1)
        j = lax.broadcasted_iota(jnp.int32, (1, BAND, BAND), 2)
        causal = jnp.where(j <= a, 0.0, NEG)
        return [as_blocks(src, slice(0, seq)) for src, _ in pairs], [causal]
    for src, dst in pairs:
        dst[:, BAND:, :] = as_blocks(src, slice(0, seq))
        dst[1:, :BAND, :] = as_blocks(src, slice(0, seq - BAND))
        dst[0:1, :BAND, :] = jnp.zeros((1, BAND, LANES), BF16)
    a = lax.broadcasted_iota(jnp.int32, (1, BAND, 2 * BAND), 1)
    j = lax.broadcasted_iota(jnp.int32, (1, BAND, 2 * BAND), 2)
    band = jnp.where(((j < BAND) & (j >= a)) | ((j >= BAND) & (j - BAND <= a)), 0.0, NEG)
    e = lax.broadcasted_iota(jnp.int32, (nblk, 1, 2 * BAND), 0)
    j = lax.broadcasted_iota(jnp.int32, (nblk, 1, 2 * BAND), 2)
    no_prev = jnp.where(((e & (per_res - 1)) == 0) & (j < BAND), NEG, 0.0)
    return [dst[...] for _, dst in pairs], [band + no_prev]


def _regroup(d, seq):
    if d == 1:
        return [(slice(0, seq), slice(0, seq))]
    before, n = d // 4, seq // d
    return [(pl.ds(r1 * (seq // before) + r2, n, stride=4), slice((before * r2 + r1) * n, (before * r2 + r1 + 1) * n))
            for r1 in range(before) for r2 in range(4)]


def _dil_fwd(proj, gq, gk, cos, up, dn, nb, seq):
    t = nb * seq
    npair = W_GROUP // LANES
    off = 3 * npair

    def body(q_ref, k_ref, v_ref, gq_ref, gk_ref, cos_ref, up_ref, dn_ref, o_ref, lse_ref,
             src_a, src_b, qp, kp, vp, kw, vw, m_b, l_b, o_b, state_a, state_b):
        ones = _group_ones()
        masks = _head_masks()
        _, _, _, _, qn, kn = _dil_prep(q_ref, k_ref, gq_ref, gk_ref, cos_ref, up_ref, dn_ref, ones)
        src_a[0] = qn
        src_a[1] = kn
        src_a[2] = v_ref[...].astype(F32)
        nblk = seq // BAND
        src, state = (src_a, src_b), (state_a, state_b)

        for d in DILATIONS:
            last = d == DILATIONS[-1]
            for before, after in _regroup(d, seq):
                qv, kv, vv = src[0].at[0][before, :], src[0].at[1][before, :], src[0].at[2][before, :]
                for hd in range(2):
                    qp[hd, after, :] = (qv * masks[hd]).astype(BF16)
                kp[after, :] = kv.astype(BF16)
                vp[after, :] = vv.astype(BF16)
                if d > 1 and not last:
                    src[1][0, after, :], src[1][1, after, :], src[1][2, after, :] = qv, kv, vv
            if d > 1:
                src = src[::-1]
            (keys_k, keys_v), bias = _dil_keys(d, seq, [(kp, kw), (vp, vw)])
            m_t = jnp.zeros((nblk, BAND, LANES), F32)
            l_t = jnp.zeros((nblk, BAND, LANES), F32)
            o_t = jnp.zeros((nblk, BAND, LANES), F32)
            for hd in range(2):
                s = _dot(qp[hd].reshape(nblk, BAND, LANES), keys_k, BATCH_NT)
                for b_ in bias:
                    s = s + b_
                m = jnp.max(s, axis=-1, keepdims=True)
                p = jnp.exp2(s - m)
                m_t = m_t + m * masks[hd]
                l_t = l_t + jnp.sum(p, axis=-1, keepdims=True) * masks[hd]
                o_t = o_t + _dot(p.astype(BF16), keys_v, BATCH_NN) * masks[hd]
            if d == 1:
                state[0][0] = m_t.reshape(seq, LANES)
                state[0][1] = l_t.reshape(seq, LANES)
                state[0][2] = o_t.reshape(seq, LANES)
                continue
            m_b[...] = m_t.reshape(seq, LANES)
            l_b[...] = l_t.reshape(seq, LANES)
            o_b[...] = o_t.reshape(seq, LANES)
            for before, after in _regroup(d, seq):
                m_old = state[0].at[0][before, :]
                m_new = jnp.maximum(m_old, m_b[after, :])
                w_old = jnp.exp2(m_old - m_new)
                w_new = jnp.exp2(m_b[after, :] - m_new)
                state[1][0, after, :] = m_new
                state[1][1, after, :] = state[0].at[1][before, :] * w_old + l_b[after, :] * w_new
                state[1][2, after, :] = state[0].at[2][before, :] * w_old + o_b[after, :] * w_new
            state = state[::-1]

        l = state[0][1]
        o_b[...] = state[0][2] / l
        l_b[...] = state[0][0] + jnp.log2(l)
        held, spare = [o_b, l_b], [m_b, state[1].at[0]]
        for d in DILATIONS[:0:-1]:
            dests = [o_ref, lse_ref] if d == DILATIONS[1] else spare
            for h, dst in zip(held, dests):
                for before, after in _regroup(d, seq):
                    dst[before, :] = h[after, :]
            held, spare = dests, held

    blk = lambda o_: pl.BlockSpec((seq, LANES), lambda b, p: (b, o_ + p))
    vec = pl.BlockSpec((1, LANES), lambda b, p: (0, 0))
    tab = pl.BlockSpec(memory_space=pltpu.VMEM)
    f32_buf = pltpu.VMEM((seq, LANES), F32)
    f32_x3 = pltpu.VMEM((3, seq, LANES), F32)
    bf16_buf = pltpu.VMEM((seq, LANES), BF16)
    window_buf = pltpu.VMEM((seq // BAND, 2 * BAND, LANES), BF16)
    return pl.pallas_call(
        body, name="dil_fwd", grid=(nb, npair),
        in_specs=[blk(off), blk(off + npair), blk(off + 2 * npair), vec, vec, tab, tab, tab],
        out_specs=[blk(0), blk(0)],
        out_shape=[jax.ShapeDtypeStruct((t, W_GROUP), F32), jax.ShapeDtypeStruct((t, W_GROUP), F32)],
        scratch_shapes=[f32_x3, f32_x3, pltpu.VMEM((2, seq, LANES), BF16), bf16_buf, bf16_buf,
                        window_buf, window_buf, f32_buf, f32_buf, f32_buf, f32_x3, f32_x3],
        compiler_params=_params(("arbitrary", "arbitrary")),
    )(proj, proj, proj, gq, gk, cos, up, dn)


def _dil_bwd(proj, gq, gk, cos, up, dn, do, o, lse, nb, seq, rider=None):
    t = nb * seq
    npair = W_GROUP // LANES
    off = 3 * npair

    def body(q_ref, k_ref, v_ref, gq_ref, gk_ref, cos_ref, up_ref, dn_ref, do_ref, o_ref, lse_ref,
             dq_ref, dk_ref, dv_ref, dg_ref, src_a, src_b, sums_a, sums_b,
             qp, kp, vp, dop, kw, vw, lse_p, delta_p, dq_p, dk_p, dv_p):
        @pl.when((pl.program_id(0) == 0) & (pl.program_id(1) == 0))
        def _():
            dg_ref[...] = jnp.zeros_like(dg_ref)

        ones = _group_ones()
        masks = _head_masks()
        qhat, rq, khat, rk, qn, kn = _dil_prep(q_ref, k_ref, gq_ref, gk_ref, cos_ref, up_ref, dn_ref, ones)
        src_a[0] = qn
        src_a[1] = kn
        src_a[2] = v_ref[...].astype(F32)
        src_a[3] = do_ref[...]
        src_a[4] = lse_ref[...]
        src_a[5] = _groupsum(do_ref[...] * o_ref[...], ones)
        nblk = seq // BAND
        src, sums = (src_a, src_b), (sums_a, sums_b)

        for d in DILATIONS:
            last = d == DILATIONS[-1]
            for before, after in _regroup(d, seq):
                planes = [src[0].at[i][before, :] for i in range(6)]
                for hd in range(2):
                    qp[hd, after, :] = (planes[0] * masks[hd]).astype(BF16)
                    dop[hd, after, :] = (planes[3] * masks[hd]).astype(BF16)
                kp[after, :] = planes[1].astype(BF16)
                vp[after, :] = planes[2].astype(BF16)
                lse_p[after, :] = planes[4]
                delta_p[after, :] = planes[5]
                if d > 1 and not last:
                    for i in range(6):
                        src[1][i, after, :] = planes[i]
            if d > 1:
                src = src[::-1]
            (keys_k, keys_v), bias = _dil_keys(d, seq, [(kp, kw), (vp, vw)])
            nk = keys_k.shape[1]
            dq_b = jnp.zeros((nblk, BAND, LANES), F32)
            dk_b = jnp.zeros((nblk, nk, LANES), F32)
            dv_b = jnp.zeros((nblk, nk, LANES), F32)
            for hd in range(2):
                lane0 = hd * HEAD_DIM
                q3 = qp[hd].reshape(nblk, BAND, LANES)
                do3 = dop[hd].reshape(nblk, BAND, LANES)
                z = _dot(q3, keys_k, BATCH_NT)
                for b_ in bias:
                    z = z + b_
                p = jnp.exp2(z - lse_p[...].reshape(nblk, BAND, LANES)[:, :, lane0:lane0 + 1])
                dp = _dot(do3, keys_v, BATCH_NT)
                ds = (p * (dp - delta_p[...].reshape(nblk, BAND, LANES)[:, :, lane0:lane0 + 1])).astype(BF16)
                dq_b = dq_b + _dot(ds, keys_k, BATCH_NN) * masks[hd]
                dk_b = dk_b + _dot(ds, q3, BATCH_TN)
                dv_b = dv_b + _dot(p.astype(BF16), do3, BATCH_TN)
            dq_p[...] = dq_b.reshape(seq, LANES)
            for acc, out in ((dk_b, dk_p), (dv_b, dv_p)):
                out[...] = acc[:, nk - BAND:, :].reshape(seq, LANES)
                if nk > BAND:
                    out[0:seq - BAND, :] += acc[1:, :BAND, :].reshape(seq - BAND, LANES)
            if d == 1:
                sums[0][0], sums[0][1], sums[0][2] = dq_p[...], dk_p[...], dv_p[...]
                continue
            for before, after in _regroup(d, seq):
                for i, part in enumerate((dq_p, dk_p, dv_p)):
                    sums[1][i, after, :] = sums[0].at[i][before, :] + part[after, :]
            sums = sums[::-1]

        for d in DILATIONS[:0:-1]:
            for i in range(3):
                for before, after in _regroup(d, seq):
                    sums[1].at[i][before, :] = sums[0][i, after, :]
            sums = sums[::-1]

        cos, up, dn = cos_ref[...], up_ref[...], dn_ref[...]
        dq_raw, dgq = _head_norm_bwd(_rope_bwd(sums[0][0] * SCALE, cos, up, dn), qhat, rq, gq_ref[...], ones)
        dk_raw, dgk = _head_norm_bwd(_rope_bwd(sums[0][1] * LN2, cos, up, dn), khat, rk, gk_ref[...], ones)
        dq_ref[...] = dq_raw.astype(BF16)
        dk_ref[...] = dk_raw.astype(BF16)
        dv_ref[...] = sums[0][2].astype(BF16)
        dg_ref[0:1, :] += dgq
        dg_ref[1:2, :] += dgk

    blk = lambda o_: pl.BlockSpec((seq, LANES), lambda b, p: (b, o_ + p))
    vec = pl.BlockSpec((1, LANES), lambda b, p: (0, 0))
    tab = pl.BlockSpec(memory_space=pltpu.VMEM)
    f32_buf = pltpu.VMEM((seq, LANES), F32)
    bf16_buf = pltpu.VMEM((seq, LANES), BF16)
    window_buf = pltpu.VMEM((seq // BAND, 2 * BAND, LANES), BF16)
    bf16_pair = pltpu.VMEM((2, seq, LANES), BF16)
    return _host_call(
        body, rider, name="dil_bwd", grid=(nb, npair),
        in_specs=[blk(off), blk(off + npair), blk(off + 2 * npair), vec, vec, tab, tab, tab,
                  blk(0), blk(0), blk(0)],
        out_specs=[blk(0), blk(0), blk(0), pl.BlockSpec((8, LANES), lambda b, p: (0, 0))],
        out_shape=[jax.ShapeDtypeStruct((t, W_GROUP), BF16), jax.ShapeDtypeStruct((t, W_GROUP), BF16),
                   jax.ShapeDtypeStruct((t, W_GROUP), BF16), jax.ShapeDtypeStruct((8, LANES), F32)],
        scratch_shapes=[pltpu.VMEM((6, seq, LANES), F32)] * 2 + [pltpu.VMEM((3, seq, LANES), F32)] * 2
        + [bf16_pair, bf16_buf, bf16_buf, bf16_pair, window_buf, window_buf] + [f32_buf] * 5,
        inputs=(proj, proj, proj, gq, gk, cos, up, dn, do, o, lse), semantics=("arbitrary", "arbitrary"))


def _adamw(w, g, m, v, name, rider=None):
    row_major = w.ndim == 3 and w.shape[1] == 1
    rows, cols = (w.shape[0], w.shape[2]) if row_major else w.shape[-2:]
    if row_major:
        tr = max(t for t in range(1, 65) if rows % t == 0)
    else:
        tr = _row_tile(rows, 512) if rows >= 8 else rows
    c1 = 1.0 - ADAM_B1 ** ADAM_STEP
    c2 = 1.0 - ADAM_B2 ** ADAM_STEP

    def body(w_ref, g_ref, m_ref, v_ref, d_ref, nm_ref, nv_ref):
        g_ = g_ref[...]
        nm = ADAM_B1 * m_ref[...] + (1.0 - ADAM_B1) * g_
        nv = ADAM_B2 * v_ref[...] + (1.0 - ADAM_B2) * (g_ * g_)
        nm_ref[...] = nm
        nv_ref[...] = nv
        d_ref[...] = -ADAM_LR * ((nm / c1) / (jnp.sqrt(nv / c2) + ADAM_EPS) + ADAM_WD * w_ref[...])

    if row_major:
        spec = pl.BlockSpec((tr, 1, cols), lambda i: (i, 0, 0))
    elif w.ndim == 3:
        spec = pl.BlockSpec((1, tr, cols), lambda i: (0, i, 0))
    else:
        spec = pl.BlockSpec((tr, cols), lambda i: (i, 0))
    shape = jax.ShapeDtypeStruct(w.shape, F32)
    return _host_call(
        body, rider, name=name, grid=(rows // tr,), in_specs=[spec] * 4, out_specs=[spec] * 3,
        out_shape=[shape] * 3, scratch_shapes=[], inputs=(w, g, m, v), semantics=("arbitrary",))


def _place():
    x, y, c = lax.axis_index("x"), lax.axis_index("y"), lax.axis_index("c")
    chips = [(1 - x, y), (x, 1 - y), (1 - x, 1 - y)]
    return x, y, c, chips


def _gather_weight(w, name):
    _, rows, cols = w.shape
    half_rows = rows // 2

    def body(w_ref, out_ref, send_sems, recv_sems):
        x, y, c, chips = _place()
        sibling = (x, y, 1 - c)
        mine = 2 * x + y
        lo = pl.multiple_of(c * half_rows, 16)
        lo_sib = pl.multiple_of((1 - c) * half_rows, 16)
        out_ref[mine] = w_ref[0].astype(BF16)

        def copy(k, shard, first_row, to):
            ref = out_ref.at[shard, pl.ds(first_row, half_rows), :]
            return pltpu.make_async_remote_copy(src_ref=ref, dst_ref=ref, send_sem=send_sems.at[k],
                                                recv_sem=recv_sems.at[k], device_id=to, device_id_type=MESH)

        sends = [copy(k, mine, lo, (cx, cy, c)) for k, (cx, cy) in enumerate(chips)]
        for cp in sends:
            cp.start()
        passed = []
        for k, (cx, cy) in enumerate(chips):
            theirs = 2 * cx + cy
            copy(k, theirs, lo, (cx, cy, c)).wait_recv()
            fw = copy(3 + k, theirs, lo, sibling)
            fw.start()
            passed.append(fw)
        for k, (cx, cy) in enumerate(chips):
            copy(3 + k, 2 * cx + cy, lo_sib, sibling).wait_recv()
        for cp in sends + passed:
            cp.wait_send()

    return pl.pallas_call(
        body, name=name,
        in_specs=[pl.BlockSpec(memory_space=pltpu.VMEM)],
        out_specs=pl.BlockSpec(memory_space=pltpu.VMEM),
        out_shape=jax.ShapeDtypeStruct((4, rows, cols), BF16),
        scratch_shapes=[pltpu.SemaphoreType.DMA((6,)), pltpu.SemaphoreType.DMA((6,))],
        compiler_params=pltpu.CompilerParams(vmem_limit_bytes=VMEM_LIMIT),
    )(w)


def _remote(src, dst, sems, k, to):
    send_sems, recv_sems = sems
    return pltpu.make_async_remote_copy(src_ref=src, dst_ref=dst, send_sem=send_sems.at[k], recv_sem=recv_sems.at[k],
                                        device_id=to, device_id_type=MESH)


def _cast_bf16(parts, name):
    def body(*refs):
        for src, dst in zip(refs[:len(parts)], refs[len(parts):]):
            dst[...] = src[0].astype(BF16)

    return pl.pallas_call(
        body, name=name, in_specs=[pl.BlockSpec(memory_space=pltpu.VMEM)] * len(parts),
        out_specs=[pl.BlockSpec(memory_space=pltpu.VMEM)] * len(parts),
        out_shape=[jax.ShapeDtypeStruct(p.shape[1:], BF16) for p in parts],
        compiler_params=pltpu.CompilerParams(vmem_limit_bytes=VMEM_LIMIT),
    )(*parts)


def _gather_rider(shards):
    def copies(ins, outs, sems, which):
        x, y, c, chips = _place()
        sibling = (x, y, 1 - c)
        mine = 2 * x + y
        made = {name: [] for name in which}
        for i, (p_ref, g_ref) in enumerate(zip(ins, outs)):
            half = p_ref.shape[0] // 2
            lo = pl.multiple_of(c * half, 16)
            lo_sib = pl.multiple_of((1 - c) * half, 16)
            spot = lambda shard, first, g_ref=g_ref, half=half: g_ref.at[shard, pl.ds(first, half), :]
            groups = {
                "own": lambda: [pltpu.make_async_copy(p_ref, g_ref.at[mine], sems[0].at[7 * i + 6])],
                "sends": lambda: [_remote(p_ref.at[pl.ds(lo, half), :], spot(mine, lo), sems, 7 * i + k, (cx, cy, c))
                                  for k, (cx, cy) in enumerate(chips)],
                "arrivals": lambda: [_remote(spot(2 * cx + cy, lo), spot(2 * cx + cy, lo), sems, 7 * i + k, (cx, cy, c))
                                     for k, (cx, cy) in enumerate(chips)],
                "passes": lambda: [_remote(spot(2 * cx + cy, lo), spot(2 * cx + cy, lo), sems, 7 * i + 3 + k, sibling)
                                   for k, (cx, cy) in enumerate(chips)],
                "from_sibling": lambda: [_remote(spot(2 * cx + cy, lo_sib), spot(2 * cx + cy, lo_sib), sems,
                                                 7 * i + 3 + k, sibling) for k, (cx, cy) in enumerate(chips)],
            }
            for name in which:
                made[name] += groups[name]()
        return [made[name] for name in which]

    def start(ins, outs, send_sems, recv_sems):
        own, sends = copies(ins, outs, (send_sems, recv_sems), ("own", "sends"))
        for cp in own + sends:
            cp.start()

    def middle(ins, outs, send_sems, recv_sems):
        arrivals, passes = copies(ins, outs, (send_sems, recv_sems), ("arrivals", "passes"))
        for landed, onward in zip(arrivals, passes):
            landed.wait_recv()
            onward.start()

    def finish(ins, outs, send_sems, recv_sems):
        own, sends, passes, from_sibling = copies(ins, outs, (send_sems, recv_sems),
                                                  ("own", "sends", "passes", "from_sibling"))
        for cp in from_sibling:
            cp.wait_recv()
        for cp in sends + passes:
            cp.wait_send()
        for cp in own:
            cp.wait()

    shapes = [jax.ShapeDtypeStruct((4,) + s.shape, BF16) for s in shards]
    return _Rider(shards, shapes, 7 * len(shards), start, finish, middle=middle)


def _exchange_rider(inputs, out_shapes, n_sems, copies, aliases=None):
    def start(ins, outs, send_sems, recv_sems):
        for cp in copies(ins, outs, (send_sems, recv_sems)):
            cp.start()

    def finish(ins, outs, send_sems, recv_sems):
        for cp in copies(ins, outs, (send_sems, recv_sems)):
            cp.wait()

    return _Rider(inputs, out_shapes, n_sems, start, finish, aliases)


def _swap_rider(grads4):
    halves = [g.shape[1] // 2 for g in grads4]

    def copies(ins, outs, sems):
        x, y, c, _ = _place()
        return [_remote(g.at[:, pl.ds(pl.multiple_of((1 - c) * h, 8), h), :], a, sems, i, (x, y, 1 - c))
                for i, (g, a, h) in enumerate(zip(ins, outs, halves))]

    shapes = [jax.ShapeDtypeStruct((4, h, g.shape[2]), F32) for g, h in zip(grads4, halves)]
    return _exchange_rider(grads4, shapes, len(grads4), copies)


def _chip_sum(g4, from_sibling, name):
    _, rows, cols = g4.shape
    half = rows // 2

    def body(g_ref, s_ref, stage_ref, own_ref):
        x, y, c, chips = _place()
        lo = pl.multiple_of(c * half, 8)
        for k, (cx, cy) in enumerate(chips):
            theirs = 2 * cx + cy
            stage_ref[k] = (g_ref[theirs, pl.ds(lo, half), :] + s_ref[theirs]).astype(BF16)
        mine = 2 * x + y
        own_ref[...] = g_ref[mine, pl.ds(lo, half), :] + s_ref[mine]

    return pl.pallas_call(
        body, name=name, in_specs=[pl.BlockSpec(memory_space=pltpu.VMEM)] * 2,
        out_specs=[pl.BlockSpec(memory_space=pltpu.VMEM)] * 2,
        out_shape=[jax.ShapeDtypeStruct((3, half, cols), BF16), jax.ShapeDtypeStruct((half, cols), F32)],
        compiler_params=pltpu.CompilerParams(vmem_limit_bytes=VMEM_LIMIT),
    )(g4, from_sibling)


def _spread_rider(stages):
    def copies(ins, outs, sems):
        _, _, c, chips = _place()
        return [_remote(st.at[k], ld.at[k], sems, 3 * i + k, (cx, cy, c))
                for i, (st, ld) in enumerate(zip(ins, outs)) for k, (cx, cy) in enumerate(chips)]

    shapes = [jax.ShapeDtypeStruct(s.shape, s.dtype) for s in stages]
    return _exchange_rider(stages, shapes, 3 * len(stages), copies)


def _finish_half(own, landed, name):
    half, cols = own.shape

    def body(own_ref, landed_ref, out_ref):
        c = lax.axis_index("c")
        acc = own_ref[...]
        for k in range(3):
            acc = acc + landed_ref[k].astype(F32)
        out_ref[pl.ds(pl.multiple_of(c * half, 8), half), :] = acc

    return pl.pallas_call(
        body, name=name, in_specs=[pl.BlockSpec(memory_space=pltpu.VMEM)] * 2,
        out_specs=pl.BlockSpec(memory_space=pltpu.VMEM),
        out_shape=jax.ShapeDtypeStruct((2 * half, cols), F32),
        compiler_params=pltpu.CompilerParams(vmem_limit_bytes=VMEM_LIMIT),
    )(own, landed)


def _share_rider(fulls, first_sem=0):
    def copies(ins, outs, sems):
        x, y, c, _ = _place()
        out = []
        for i, full in enumerate(outs):
            half = full.shape[0] // 2
            rows = full.at[pl.ds(pl.multiple_of(c * half, 8), half), :]
            out.append(_remote(rows, rows, sems, first_sem + i, (x, y, 1 - c)))
        return out

    def finish_copies(ins, outs, sems):
        x, y, c, _ = _place()
        out = []
        for i, full in enumerate(outs):
            half = full.shape[0] // 2
            mine = full.at[pl.ds(pl.multiple_of(c * half, 8), half), :]
            theirs = full.at[pl.ds(pl.multiple_of((1 - c) * half, 8), half), :]
            k = first_sem + i
            out.append((_remote(mine, mine, sems, k, (x, y, 1 - c)), _remote(theirs, theirs, sems, k, (x, y, 1 - c))))
        return out

    def start(ins, outs, send_sems, recv_sems):
        for cp in copies(ins, outs, (send_sems, recv_sems)):
            cp.start()

    def finish(ins, outs, send_sems, recv_sems):
        for sent, landed in finish_copies(ins, outs, (send_sems, recv_sems)):
            sent.wait_send()
            landed.wait_recv()

    shapes = [jax.ShapeDtypeStruct(f.shape, f.dtype) for f in fulls]
    return _Rider(fulls, shapes, first_sem + len(fulls), start, finish, aliases={i: i for i in range(len(fulls))})


def _join_riders(a, b):
    n_in, n_out = len(a.inputs), len(a.out_shapes)

    def both(which):
        def run(ins, outs, send_sems, recv_sems):
            getattr(a, which)(ins[:n_in], outs[:n_out], send_sems, recv_sems)
            getattr(b, which)(ins[n_in:], outs[n_out:], send_sems, recv_sems)
        return run

    aliases = dict(a.aliases)
    aliases.update({n_in + i: n_out + o for i, o in b.aliases.items()})
    return _Rider(a.inputs + b.inputs, a.out_shapes + b.out_shapes, max(a.n_sems, b.n_sems), both("start"),
                  both("finish"), aliases)


def _all_sum_small(v):
    shape = v.shape

    def body(v_ref, out_ref, buf, send_sems, recv_sems):
        x, y, c, _ = _place()
        me = 4 * x + 2 * y + c
        buf[me] = v_ref[...]
        flips = [(dx, dy, dc) for dx in (0, 1) for dy in (0, 1) for dc in (0, 1)][1:]

        def copy(k, slot, flip):
            dx, dy, dc = flip
            to = (1 - x if dx else x, 1 - y if dy else y, 1 - c if dc else c)
            return pltpu.make_async_remote_copy(src_ref=buf.at[slot], dst_ref=buf.at[slot], send_sem=send_sems.at[k],
                                                recv_sem=recv_sems.at[k], device_id=to, device_id_type=MESH)

        sends = [copy(k, me, flip) for k, flip in enumerate(flips)]
        for cp in sends:
            cp.start()
        for k, (dx, dy, dc) in enumerate(flips):
            sender = 4 * (1 - x if dx else x) + 2 * (1 - y if dy else y) + (1 - c if dc else c)
            copy(k, sender, (dx, dy, dc)).wait_recv()
        for cp in sends:
            cp.wait_send()
        total = buf[0]
        for i in range(1, 8):
            total = total + buf[i]
        out_ref[...] = total

    return pl.pallas_call(
        body, name="all_sum_small",
        in_specs=[pl.BlockSpec(memory_space=pltpu.VMEM)],
        out_specs=pl.BlockSpec(memory_space=pltpu.VMEM),
        out_shape=jax.ShapeDtypeStruct(shape, F32),
        scratch_shapes=[pltpu.VMEM((8,) + shape, F32), pltpu.SemaphoreType.DMA((7,)), pltpu.SemaphoreType.DMA((7,))],
    )(v)


SMALL = (("g_mix", 1024), ("g_ffn", 1024), ("g_out_fox", 512), ("g_out_dil", 512), ("g_q_fox", 64),
         ("g_k_fox", 64), ("g_q_dil", 64), ("g_k_dil", 64), ("b_forget", 8))
SMALL_PACKED = (32, LANES)


def _local_grads(x, target, gains, w1, wft, dense, packed, nb, seq):
    tile2 = lambda g: jnp.tile(g, (1, 2))
    gq_f, gk_f, gq_d, gk_d = (tile2(gains[n]) for n in ("g_q_fox", "g_k_fox", "g_q_dil", "g_k_dil"))
    b_col = gains["b_forget"].reshape(N_FOX_HEADS, 1)
    cos, up, dn = _rope_tables(seq)
    npair = N_FOX_HEADS // 2

    proj, fa_row, h1, h1_t = _in_proj(x, gains["g_mix"], w1, wft)
    c_row = _gate_fwd(fa_row, b_col, seq)
    c3 = c_row.reshape(npair, 2, nb * seq)
    (o_fox, lse_fox), gathered = _fox_fwd(proj, c3, gq_f, gk_f, nb, seq,
                                          rider=None if packed is None else _gather_rider(packed))
    if packed is not None:
        dense = [g.reshape(-1, g.shape[2]) for g in gathered]
    w_out, w_gate, w_up, w_down = dense
    o_dil, lse_dil = _dil_fwd(proj, gq_d, gk_d, cos, up, dn, nb, seq)
    x1, o_n_t = _attn_out(o_fox, o_dil, x, gains["g_out_fox"], gains["g_out_dil"], w_out)
    a, u, dy, loss_parts = _ffn_fwd(x1, target, gains["g_ffn"], w_gate, w_up, w_down)
    loss = jnp.sum(loss_parts[:, 0, 0])

    dx1, s, da, du, h2, dg_ffn = _ffn_bwd(dy, a, u, x1, gains["g_ffn"], w_gate, w_up, w_down)
    d_w_down = _token_matmul(s, dy, "dw_down", 512, False)
    d_w_gate = _token_matmul(da, h2, "dw_gate", 512, False)
    d_w_up = _token_matmul(du, h2, "dw_up", 512, False)
    d_w_out = _token_matmul(o_n_t, dx1, "dw_out", 1024)
    names = ("w_out", "w_gate", "w_up", "w_down")
    grads4 = [g.reshape(4, -1, g.shape[1]) for g in (d_w_out, d_w_gate, d_w_up, d_w_down)]
    exchange = packed is not None
    (do_fox, do_dil, dg_of, dg_od), _ = _attn_out_bwd(
        dx1, o_fox, o_dil, gains["g_out_fox"], gains["g_out_dil"], w_out)
    (dq_f, dk_f, dv_f, dc3, dg_fox), from_sibling = _fox_bwd(
        proj, c3, gq_f, gk_f, do_fox, o_fox, lse_fox, nb, seq,
        rider=_swap_rider(grads4) if exchange else None)
    if exchange:
        sums = [_chip_sum(g, s, "chip_sum_" + n) for g, s, n in zip(grads4, from_sibling, names)]
    (dq_d, dk_d, dv_d, dg_dil), landed = _dil_bwd(
        proj, gq_d, gk_d, cos, up, dn, do_dil, o_dil, lse_dil, nb, seq,
        rider=_spread_rider([st for st, _ in sums]) if exchange else None)
    if exchange:
        halves = [_finish_half(own, ld, "finish_half_" + n) for (_, own), ld, n in zip(sums, landed, names)]
    dfa_row, db = _gate_bwd(dc3.reshape(N_FOX_HEADS, nb * seq), fa_row, b_col, seq)
    dparts = [dq_f, dk_f, dv_f, dq_d, dk_d, dv_d]
    d_w1 = _token_matmul_parts(h1_t, dparts, "dw_in")
    d_wf = _row_matmul(dfa_row, h1, "dw_forget")
    fox_w = 3 * W_GROUP
    in_order = [(d_w1[:, :fox_w], fox_w), (d_wf.T, N_FOX_HEADS), (d_w1[:, fox_w:], d_w1.shape[1] - fox_w)]
    n_cols = d_w1.shape[1] + N_FOX_HEADS
    if exchange:
        shards = [jnp.stack([_pick_columns(in_order, s * n_cols // 4, (s + 1) * n_cols // 4) for s in range(4)])]
        _, from_sibling = _idle_host(_swap_rider(shards), "swap_w_in")
        stage, own = _chip_sum(shards[0], from_sibling[0], "chip_sum_w_in")
        rider = _join_riders(_spread_rider([stage]), _share_rider(halves, first_sem=3))
    (grad_x, dg_mix), rode = _in_proj_bwd(dparts, dfa_row, w1, wft, x, gains["g_mix"], dx1,
                                          rider=rider if exchange else None)
    if exchange:
        d_w_in = _finish_half(own, rode[0], "finish_half_w_in")
        d_w_out, d_w_gate, d_w_up, d_w_down = rode[1:]
    else:
        d_w_in = _pick_columns(in_order, 0, n_cols)

    fold = lambda g2: (g2[:, :HEAD_DIM] + g2[:, HEAD_DIM:])
    small = {
        "g_mix": dg_mix[0:1], "g_ffn": dg_ffn[0:1], "g_out_fox": dg_of[0:1], "g_out_dil": dg_od[0:1],
        "g_q_fox": fold(dg_fox[0:1]), "g_k_fox": fold(dg_fox[1:2]),
        "g_q_dil": fold(dg_dil[0:1]), "g_k_dil": fold(dg_dil[1:2]),
        "b_forget": db[:, 0].reshape(1, N_FOX_HEADS),
    }
    big = {"w_in": d_w_in, "w_out": d_w_out, "w_gate": d_w_gate, "w_up": d_w_up, "w_down": d_w_down}
    return loss, grad_x, big, small


def _pick_columns(pieces, lo, hi):
    out, first = [], 0
    for a, w in pieces:
        a_lo, a_hi = max(lo, first), min(hi, first + w)
        if a_lo < a_hi:
            out.append(a[:, a_lo - first:a_hi - first])
        first += w
    return out[0] if len(out) == 1 else jnp.concatenate(out, axis=1)


def kernel(x, g_mix, w_in, b_forget, g_q_fox, g_k_fox, g_q_dil, g_k_dil, g_out_fox, g_out_dil, w_out, g_ffn, w_gate, w_up, w_down, loss_target, m_g_mix, m_w_in, m_b_forget, m_g_q_fox, m_g_k_fox, m_g_q_dil, m_g_k_dil, m_g_out_fox, m_g_out_dil, m_w_out, m_g_ffn, m_w_gate, m_w_up, m_w_down, v_g_mix, v_w_in, v_b_forget, v_g_q_fox, v_g_k_fox, v_g_q_dil, v_g_k_dil, v_g_out_fox, v_g_out_dil, v_w_out, v_g_ffn, v_w_gate, v_w_up, v_w_down):
    nb, seq, d = x.shape
    weights = dict(g_mix=g_mix, w_in=w_in, b_forget=b_forget, g_q_fox=g_q_fox, g_k_fox=g_k_fox, g_q_dil=g_q_dil,
                   g_k_dil=g_k_dil, g_out_fox=g_out_fox, g_out_dil=g_out_dil, w_out=w_out, g_ffn=g_ffn,
                   w_gate=w_gate, w_up=w_up, w_down=w_down)
    m_in = dict(g_mix=m_g_mix, w_in=m_w_in, b_forget=m_b_forget, g_q_fox=m_g_q_fox, g_k_fox=m_g_k_fox,
                g_q_dil=m_g_q_dil, g_k_dil=m_g_k_dil, g_out_fox=m_g_out_fox, g_out_dil=m_g_out_dil, w_out=m_w_out,
                g_ffn=m_g_ffn, w_gate=m_w_gate, w_up=m_w_up, w_down=m_w_down)
    v_in = dict(g_mix=v_g_mix, w_in=v_w_in, b_forget=v_b_forget, g_q_fox=v_g_q_fox, g_k_fox=v_g_k_fox,
                g_q_dil=v_g_q_dil, g_k_dil=v_g_k_dil, g_out_fox=v_g_out_fox, g_out_dil=v_g_out_dil, w_out=v_w_out,
                g_ffn=v_g_ffn, w_gate=v_w_gate, w_up=v_w_up, w_down=v_w_down)
    order = ["g_mix", "w_in", "b_forget", "g_q_fox", "g_k_fox", "g_q_dil", "g_k_dil", "g_out_fox", "g_out_dil",
             "w_out", "g_ffn", "w_gate", "w_up", "w_down"]

    w_in_all = _gather_weight(w_in, "gather_w_in")
    in_shards = [(w_in_all[s], w_in_all.shape[2]) for s in range(4)]
    fox_w = 3 * W_GROUP
    n_cols = 4 * w_in_all.shape[2]
    w1 = jnp.concatenate([_pick_columns(in_shards, 0, fox_w), _pick_columns(in_shards, fox_w + N_FOX_HEADS, n_cols)],
                         axis=1)
    wft = _pick_columns(in_shards, fox_w, fox_w + N_FOX_HEADS).T
    swap = lambda a: jnp.transpose(a, (0, 2, 1))
    for n in ("w_gate", "w_up"):
        weights[n], m_in[n], v_in[n] = swap(weights[n]), swap(m_in[n]), swap(v_in[n])
    shards = _cast_bf16([weights[n] for n in ("w_out", "w_gate", "w_up", "w_down")], "cast_shards")

    gains = {n: weights[n] for n, _ in SMALL}
    loss, grad_x, big, small = _local_grads(
        x.reshape(nb * seq, d), loss_target.reshape(nb * seq, d), gains, w1, wft, None, shards, nb, seq)

    grads = {n: big[n][None] for n in ("w_out", "w_gate", "w_up", "w_down")}
    packed = jnp.concatenate([small[n].reshape(-1) for n, _ in SMALL] + [loss.reshape(1)])
    packed = jnp.pad(packed, (0, SMALL_PACKED[0] * SMALL_PACKED[1] - packed.shape[0])).reshape(SMALL_PACKED)
    summed = _all_sum_small(packed).reshape(-1)
    pos = 0
    for n, size in SMALL:
        grads[n] = summed[pos:pos + size].reshape(1, size)
        pos += size
    loss = summed[pos]

    to_entry = lambda a: jnp.transpose(a, (2, 0, 1))
    deltas, new_m, new_v, grad_out = {}, {}, {}, {}
    for n in ["w_down"] + [n for n in order if n != "w_down"]:
        rider = _share_rider([big["w_in"]]) if n == "w_down" else None
        (deltas[n], new_m[n], new_v[n]), shared = _adamw(weights[n], grads[n], m_in[n], v_in[n], "adamw_" + n, rider)
        if rider is not None:
            grads["w_in"] = to_entry(shared[0][None])
            weights["w_in"], m_in["w_in"], v_in["w_in"] = (to_entry(a) for a in (w_in, m_w_in, v_w_in))
        grad_out[n] = grads[n]
    for n in ("w_gate", "w_up"):
        grad_out[n], deltas[n], new_m[n], new_v[n] = (swap(a) for a in (grad_out[n], deltas[n], new_m[n], new_v[n]))
    from_entry = lambda a: jnp.transpose(a, (1, 2, 0))
    grad_out["w_in"], deltas["w_in"], new_m["w_in"], new_v["w_in"] = (
        from_entry(a) for a in (grad_out["w_in"], deltas["w_in"], new_m["w_in"], new_v["w_in"]))

    return (loss, grad_x.reshape(nb, seq, d), *[grad_out[n] for n in order], *[deltas[n] for n in order],
            *[new_m[n] for n in order], *[new_v[n] for n in order])
```

```python
import functools
import math

import jax
import jax.numpy as jnp
from jax import lax
from jax.experimental import pallas as pl
from jax.experimental.pallas import tpu as pltpu

F32, BF16 = jnp.float32, jnp.bfloat16
MESH = pl.DeviceIdType.MESH

EPS = 1e-6
NEG = -1e30
HEAD_DIM = 64
SCALE = HEAD_DIM ** -0.5
LOG2E = math.log2(math.e)
LN2 = math.log(2.0)
ROPE_THETA = 500000.0
ROPE_DIM = HEAD_DIM // 4
LANES = 128
W_GROUP = 512
N_FOX_HEADS = 8
VMEM_LIMIT = 56 * 1024 * 1024
DILATIONS = (1, 4, 16)
BAND = 128

ADAM_LR, ADAM_B1, ADAM_B2, ADAM_EPS, ADAM_WD, ADAM_STEP = 0.001, 0.9, 0.999, 1e-08, 0.01, 10

NT = (((1,), (1,)), ((), ()))
TN = (((0,), (0,)), ((), ()))
BATCH_NT = (((2,), (2,)), ((0,), (0,)))
BATCH_NN = (((2,), (1,)), ((0,), (0,)))
BATCH_TN = (((1,), (1,)), ((0,), (0,)))


def _params(sem=None):
    return pltpu.CompilerParams(dimension_semantics=sem, vmem_limit_bytes=VMEM_LIMIT)


def _dot(a, b, dims=None):
    if dims is None:
        return jnp.dot(a, b, preferred_element_type=F32)
    return lax.dot_general(a, b, dims, preferred_element_type=F32)


def _group_ones():
    i = lax.broadcasted_iota(jnp.int32, (LANES, LANES), 0) >> 6
    j = lax.broadcasted_iota(jnp.int32, (LANES, LANES), 1) >> 6
    return (i == j).astype(BF16)


def _split3(x):
    a = x.astype(BF16)
    r = x - a.astype(F32)
    b = r.astype(BF16)
    c = (r - b.astype(F32)).astype(BF16)
    return a, b, c


def _groupsum(x, ones, pieces=2):
    total = None
    for _ in range(pieces):
        piece = x.astype(BF16)
        part = _dot(piece, ones)
        total = part if total is None else total + part
        x = x - piece.astype(F32)
    return total


def _head_masks():
    lane = lax.broadcasted_iota(jnp.int32, (1, LANES), 1)
    return [(lane < HEAD_DIM).astype(F32), (lane >= HEAD_DIM).astype(F32)]


def _head_norm(raw, ones):
    r = lax.rsqrt(_groupsum(raw * raw, ones, 1) * (1.0 / HEAD_DIM) + EPS)
    return raw * r, r


def _head_norm_bwd(dy, xhat, r, gain, ones):
    u = dy * gain
    dgain = jnp.sum(dy * xhat, axis=0, keepdims=True)
    draw = r * (u - xhat * (_groupsum(u * xhat, ones) * (1.0 / HEAD_DIM)))
    return draw, dgain


def _rope(x, cos, s_up, s_dn):
    return x * cos + pltpu.roll(x, LANES - 8, 1) * s_up + pltpu.roll(x, 8, 1) * s_dn


def _rope_bwd(dy, cos, s_up, s_dn):
    return dy * cos + pltpu.roll(dy * s_up, 8, 1) + pltpu.roll(dy * s_dn, LANES - 8, 1)


def _rope_tables(seq):
    half = ROPE_DIM // 2
    inv_freq = jnp.power(jnp.float32(ROPE_THETA), -jnp.arange(half, dtype=F32) * 2.0 / ROPE_DIM)
    ang = jnp.arange(seq).astype(F32)[:, None] * inv_freq[None, :]
    cos, sin = jnp.cos(ang), jnp.sin(ang)
    one = jnp.ones((seq, HEAD_DIM - ROPE_DIM), F32)
    zero_h = jnp.zeros((seq, half), F32)
    zero_r = jnp.zeros((seq, HEAD_DIM - ROPE_DIM), F32)
    c = jnp.concatenate([cos, cos, one], axis=1)
    up = jnp.concatenate([-sin, zero_h, zero_r], axis=1)
    dn = jnp.concatenate([zero_h, sin, zero_r], axis=1)
    return jnp.tile(c, (1, 2)), jnp.tile(up, (1, 2)), jnp.tile(dn, (1, 2))


def _row_tile(rows, cap=256):
    best = rows
    for t in range(8, min(rows, cap) + 1, 8):
        if rows % t == 0:
            best = t
    return best


class _Rider:
    def __init__(self, inputs, out_shapes, n_sems, start, finish, aliases=None, middle=None):
        self.inputs, self.out_shapes, self.n_sems = list(inputs), list(out_shapes), n_sems
        self.start, self.finish, self.middle, self.aliases = start, finish, middle, dict(aliases or {})


def _host_call(body, rider, *, name, grid, in_specs, out_specs, out_shape, scratch_shapes, inputs, semantics):
    if rider is None:
        return pl.pallas_call(body, name=name, grid=grid, in_specs=in_specs, out_specs=out_specs,
                              out_shape=out_shape, scratch_shapes=scratch_shapes,
                              compiler_params=_params(semantics))(*inputs), []
    n_in, n_out, n_scr = len(in_specs), len(out_specs), len(scratch_shapes)
    r_in, r_out = len(rider.inputs), len(rider.out_shapes)

    def wrapped(*refs):
        ins, refs = refs[:n_in], refs[n_in:]
        r_ins, refs = refs[:r_in], refs[r_in:]
        outs, refs = refs[:n_out], refs[n_out:]
        r_outs, refs = refs[:r_out], refs[r_out:]
        scratch, (send_sems, recv_sems) = refs[:n_scr], refs[n_scr:]
        ids = [pl.program_id(a) for a in range(len(grid))]
        first = functools.reduce(lambda p, q: p & q, [i == 0 for i in ids])
        last = functools.reduce(lambda p, q: p & q, [i == g - 1 for i, g in zip(ids, grid)])

        @pl.when(first)
        def _():
            rider.start(r_ins, r_outs, send_sems, recv_sems)

        body(*ins, *outs, *scratch)

        if rider.middle is not None:
            step, steps = ids[0], grid[0]
            for i, g in zip(ids[1:], grid[1:]):
                step, steps = step * g + i, steps * g

            @pl.when(step == (3 * steps) // 4)
            def _():
                rider.middle(r_ins, r_outs, send_sems, recv_sems)

        @pl.when(last)
        def _():
            rider.finish(r_ins, r_outs, send_sems, recv_sems)

    hbm = pl.BlockSpec(memory_space=pl.ANY)
    res = pl.pallas_call(
        wrapped, name=name, grid=grid,
        in_specs=list(in_specs) + [hbm] * r_in, out_specs=list(out_specs) + [hbm] * r_out,
        out_shape=list(out_shape) + rider.out_shapes,
        scratch_shapes=list(scratch_shapes) + [pltpu.SemaphoreType.DMA((rider.n_sems,))] * 2,
        input_output_aliases={n_in + i: n_out + o for i, o in rider.aliases.items()},
        compiler_params=_params(semantics),
    )(*inputs, *rider.inputs)
    return res[:n_out], res[n_out:]


def _idle_host(rider, name):
    def body(o_ref):
        o_ref[...] = jnp.zeros_like(o_ref)

    return _host_call(body, rider, name=name, grid=(1,), in_specs=[],
                      out_specs=[pl.BlockSpec((8, LANES), lambda i: (0, 0))],
                      out_shape=[jax.ShapeDtypeStruct((8, LANES), F32)], scratch_shapes=[], inputs=(),
                      semantics=("arbitrary",))


def _in_proj(x, g_mix, w1, wft):
    t, d = x.shape
    n = w1.shape[1]
    tt = 512

    def body(x_ref, g_ref, w_ref, wf_ref, p_ref, fa_ref, h_ref, ht_ref):
        xx = x_ref[...]
        r = lax.rsqrt(jnp.mean(xx * xx, axis=-1, keepdims=True) + EPS)
        h = (xx * r * g_ref[...]).astype(BF16)
        h_ref[...] = h
        ht_ref[...] = h.T
        for j in range(n // W_GROUP):
            cols = slice(j * W_GROUP, (j + 1) * W_GROUP)
            p_ref[:, cols] = _dot(h, w_ref[:, cols]).astype(BF16)
        fa_ref[...] = _dot(wf_ref[...], h, NT)

    return pl.pallas_call(
        body, name="in_proj", grid=(t // tt,),
        in_specs=[pl.BlockSpec((tt, d), lambda i: (i, 0)), pl.BlockSpec((1, d), lambda i: (0, 0)),
                  pl.BlockSpec(memory_space=pltpu.VMEM), pl.BlockSpec(memory_space=pltpu.VMEM)],
        out_specs=[pl.BlockSpec((tt, n), lambda i: (i, 0)), pl.BlockSpec((8, tt), lambda i: (0, i)),
                   pl.BlockSpec((tt, d), lambda i: (i, 0)), pl.BlockSpec((d, tt), lambda i: (0, i))],
        out_shape=[jax.ShapeDtypeStruct((t, n), BF16), jax.ShapeDtypeStruct((8, t), F32),
                   jax.ShapeDtypeStruct((t, d), BF16), jax.ShapeDtypeStruct((d, t), BF16)],
        compiler_params=_params(("arbitrary",)),
    )(x, g_mix, w1, wft)


def _tri(n, upper):
    i = lax.broadcasted_iota(jnp.int32, (n, n), 0)
    j = lax.broadcasted_iota(jnp.int32, (n, n), 1)
    return ((i <= j) if upper else (i >= j)).astype(BF16)


def _gate_fwd(fa_row, b_col, seq):
    t = fa_row.shape[1]
    cb = 256

    def body(fa_ref, b_ref, c_ref):
        tri = _tri(cb, True)
        carry = jnp.zeros((8, 1), F32)
        for k in range(seq // cb):
            z = fa_ref[:, k * cb:(k + 1) * cb] + b_ref[...]
            lf = jnp.minimum(z, 0.0) - jnp.log(1.0 + jnp.exp(-jnp.abs(z)))
            a, b, c = _split3(lf)
            blk = _dot(a, tri) + _dot(b, tri) + _dot(c, tri) + carry
            c_ref[:, k * cb:(k + 1) * cb] = blk
            carry = blk[:, cb - 1:cb]

    return pl.pallas_call(
        body, name="gate_fwd", grid=(t // seq,),
        in_specs=[pl.BlockSpec((8, seq), lambda i: (0, i)), pl.BlockSpec((8, 1), lambda i: (0, 0))],
        out_specs=pl.BlockSpec((8, seq), lambda i: (0, i)),
        out_shape=jax.ShapeDtypeStruct((8, t), F32),
        compiler_params=_params(("arbitrary",)),
    )(fa_row, b_col)


def _gate_bwd(dc_row, fa_row, b_col, seq):
    t = fa_row.shape[1]
    cb = 256

    def body(dc_ref, fa_ref, b_ref, dfa_ref, db_ref):
        @pl.when(pl.program_id(0) == 0)
        def _():
            db_ref[...] = jnp.zeros_like(db_ref)

        tri = _tri(cb, False)
        carry = jnp.zeros((8, 1), F32)
        dbs = jnp.zeros((8, 1), F32)
        for k in reversed(range(seq // cb)):
            a, b, c = _split3(dc_ref[:, k * cb:(k + 1) * cb])
            dlf = _dot(a, tri) + _dot(b, tri) + _dot(c, tri) + carry
            carry = dlf[:, 0:1]
            z = fa_ref[:, k * cb:(k + 1) * cb] + b_ref[...]
            dfa = dlf / (1.0 + jnp.exp(z))
            dfa_ref[:, k * cb:(k + 1) * cb] = dfa
            dbs = dbs + jnp.sum(dfa, axis=1, keepdims=True)
        db_ref[...] += jnp.broadcast_to(dbs, (8, LANES))

    return pl.pallas_call(
        body, name="gate_bwd", grid=(t // seq,),
        in_specs=[pl.BlockSpec((8, seq), lambda i: (0, i)), pl.BlockSpec((8, seq), lambda i: (0, i)),
                  pl.BlockSpec((8, 1), lambda i: (0, 0))],
        out_specs=[pl.BlockSpec((8, seq), lambda i: (0, i)), pl.BlockSpec((8, LANES), lambda i: (0, 0))],
        out_shape=[jax.ShapeDtypeStruct((8, t), F32), jax.ShapeDtypeStruct((8, LANES), F32)],
        compiler_params=_params(("arbitrary",)),
    )(dc_row, fa_row, b_col)


def _attn_out(o_fox, o_dil, x, g_fox, g_dil, w_out):
    t, d = x.shape
    w = o_fox.shape[1]
    tt = 512

    def body(of_ref, od_ref, x_ref, gf_ref, gd_ref, w_ref, x1_ref, ont_ref):
        acc = x_ref[...]
        for k, (o_ref, g_ref) in enumerate(((of_ref, gf_ref), (od_ref, gd_ref))):
            o = o_ref[...]
            r = lax.rsqrt(jnp.mean(o * o, axis=-1, keepdims=True) + EPS)
            on = (o * r * g_ref[...]).astype(BF16)
            ont_ref[k * w:(k + 1) * w, :] = on.T
            acc = acc + _dot(on, w_ref[k * w:(k + 1) * w, :])
        x1_ref[...] = acc

    return pl.pallas_call(
        body, name="attn_out", grid=(t // tt,),
        in_specs=[pl.BlockSpec((tt, w), lambda i: (i, 0)), pl.BlockSpec((tt, w), lambda i: (i, 0)),
                  pl.BlockSpec((tt, d), lambda i: (i, 0)), pl.BlockSpec((1, w), lambda i: (0, 0)),
                  pl.BlockSpec((1, w), lambda i: (0, 0)), pl.BlockSpec(memory_space=pltpu.VMEM)],
        out_specs=[pl.BlockSpec((tt, d), lambda i: (i, 0)), pl.BlockSpec((2 * w, tt), lambda i: (0, i))],
        out_shape=[jax.ShapeDtypeStruct((t, d), F32), jax.ShapeDtypeStruct((2 * w, t), BF16)],
        compiler_params=_params(("arbitrary",)),
    )(o_fox, o_dil, x, g_fox, g_dil, w_out)


def _attn_out_bwd(dx1, o_fox, o_dil, g_fox, g_dil, w_out, rider=None):
    t, d = dx1.shape
    w = o_fox.shape[1]
    tt = 512

    def body(dx_ref, of_ref, od_ref, gf_ref, gd_ref, w_ref, dof_ref, dod_ref, dgf_ref, dgd_ref):
        @pl.when(pl.program_id(0) == 0)
        def _():
            dgf_ref[...] = jnp.zeros_like(dgf_ref)
            dgd_ref[...] = jnp.zeros_like(dgd_ref)

        dxb = dx_ref[...].astype(BF16)
        for k, (o_ref, g_ref, do_ref, dg_ref) in enumerate(
                ((of_ref, gf_ref, dof_ref, dgf_ref), (od_ref, gd_ref, dod_ref, dgd_ref))):
            don = _dot(dxb, w_ref[k * w:(k + 1) * w, :], NT)
            o = o_ref[...]
            r = lax.rsqrt(jnp.mean(o * o, axis=-1, keepdims=True) + EPS)
            xhat = o * r
            u = don * g_ref[...]
            do_ref[...] = r * (u - xhat * jnp.mean(u * xhat, axis=-1, keepdims=True))
            dg_ref[0:1, :] += jnp.sum(don * xhat, axis=0, keepdims=True)

    return _host_call(
        body, rider, name="attn_out_bwd", grid=(t // tt,),
        in_specs=[pl.BlockSpec((tt, d), lambda i: (i, 0)), pl.BlockSpec((tt, w), lambda i: (i, 0)),
                  pl.BlockSpec((tt, w), lambda i: (i, 0)), pl.BlockSpec((1, w), lambda i: (0, 0)),
                  pl.BlockSpec((1, w), lambda i: (0, 0)), pl.BlockSpec(memory_space=pltpu.VMEM)],
        out_specs=[pl.BlockSpec((tt, w), lambda i: (i, 0)), pl.BlockSpec((tt, w), lambda i: (i, 0)),
                   pl.BlockSpec((8, w), lambda i: (0, 0)), pl.BlockSpec((8, w), lambda i: (0, 0))],
        out_shape=[jax.ShapeDtypeStruct((t, w), F32), jax.ShapeDtypeStruct((t, w), F32),
                   jax.ShapeDtypeStruct((8, w), F32), jax.ShapeDtypeStruct((8, w), F32)],
        scratch_shapes=[], inputs=(dx1, o_fox, o_dil, g_fox, g_dil, w_out), semantics=("arbitrary",))


def _ffn_fwd(x1, target, g_ffn, w_gate, w_up, w_down):
    t, d = x1.shape
    f = w_gate.shape[0]
    tt = 256

    def body(x_ref, t_ref, g_ref, wg_ref, wu_ref, wd_ref, a_ref, u_ref, dy_ref, loss_ref):
        xx = x_ref[...]
        r = lax.rsqrt(jnp.mean(xx * xx, axis=-1, keepdims=True) + EPS)
        h = (xx * r * g_ref[...]).astype(BF16)
        a = _dot(h, wg_ref[...], NT)
        u = _dot(h, wu_ref[...], NT)
        a_ref[...] = a.astype(BF16)
        u_ref[...] = u.astype(BF16)
        s = (a / (1.0 + jnp.exp(-a)) * u).astype(BF16)
        y = xx + _dot(s, wd_ref[...])
        e = y - t_ref[...]
        dy_ref[...] = e * (1.0 / d)
        loss_ref[...] = jnp.broadcast_to(0.5 * jnp.sum(e * e) * (1.0 / d), (1, 8, LANES))

    return pl.pallas_call(
        body, name="ffn_fwd", grid=(t // tt,),
        in_specs=[pl.BlockSpec((tt, d), lambda i: (i, 0)), pl.BlockSpec((tt, d), lambda i: (i, 0)),
                  pl.BlockSpec((1, d), lambda i: (0, 0)), pl.BlockSpec(memory_space=pltpu.VMEM),
                  pl.BlockSpec(memory_space=pltpu.VMEM), pl.BlockSpec(memory_space=pltpu.VMEM)],
        out_specs=[pl.BlockSpec((tt, f), lambda i: (i, 0)), pl.BlockSpec((tt, f), lambda i: (i, 0)),
                   pl.BlockSpec((tt, d), lambda i: (i, 0)), pl.BlockSpec((1, 8, LANES), lambda i: (i, 0, 0))],
        out_shape=[jax.ShapeDtypeStruct((t, f), BF16), jax.ShapeDtypeStruct((t, f), BF16),
                   jax.ShapeDtypeStruct((t, d), F32), jax.ShapeDtypeStruct((t // tt, 8, LANES), F32)],
        compiler_params=_params(("arbitrary",)),
    )(x1, target, g_ffn, w_gate, w_up, w_down)


def _ffn_bwd(dy, a, u, x1, g_ffn, w_gate, w_up, w_down):
    t, d = x1.shape
    f = w_gate.shape[0]
    tt = 256

    def body(dy_ref, a_ref, u_ref, x_ref, g_ref, wg_ref, wu_ref, wd_ref,
             dx_ref, s_ref, da_ref, du_ref, h_ref, dg_ref):
        @pl.when(pl.program_id(0) == 0)
        def _():
            dg_ref[...] = jnp.zeros_like(dg_ref)

        dy_ = dy_ref[...]
        dyb = dy_.astype(BF16)
        dh = jnp.zeros((tt, d), F32)
        for cols in (slice(0, f // 2), slice(f // 2, f)):
            ds = _dot(dyb, wd_ref[cols, :], NT)
            a_ = a_ref[:, cols].astype(F32)
            u_ = u_ref[:, cols].astype(F32)
            sig = 1.0 / (1.0 + jnp.exp(-a_))
            silu = a_ * sig
            s_ref[:, cols] = (silu * u_).astype(BF16)
            da = (ds * u_ * (sig * (1.0 + a_ * (1.0 - sig)))).astype(BF16)
            du = (ds * silu).astype(BF16)
            da_ref[:, cols] = da
            du_ref[:, cols] = du
            dh = dh + _dot(da, wg_ref[cols, :]) + _dot(du, wu_ref[cols, :])
        xx = x_ref[...]
        r = lax.rsqrt(jnp.mean(xx * xx, axis=-1, keepdims=True) + EPS)
        xhat = xx * r
        g = g_ref[...]
        h_ref[...] = (xhat * g).astype(BF16)
        uu = dh * g
        dx_ref[...] = dy_ + r * (uu - xhat * jnp.mean(uu * xhat, axis=-1, keepdims=True))
        dg_ref[0:1, :] += jnp.sum(dh * xhat, axis=0, keepdims=True)

    return pl.pallas_call(
        body, name="ffn_bwd", grid=(t // tt,),
        in_specs=[pl.BlockSpec((tt, d), lambda i: (i, 0)), pl.BlockSpec((tt, f), lambda i: (i, 0)),
                  pl.BlockSpec((tt, f), lambda i: (i, 0)), pl.BlockSpec((tt, d), lambda i: (i, 0)),
                  pl.BlockSpec((1, d), lambda i: (0, 0)), pl.BlockSpec(memory_space=pltpu.VMEM),
                  pl.BlockSpec(memory_space=pltpu.VMEM), pl.BlockSpec(memory_space=pltpu.VMEM)],
        out_specs=[pl.BlockSpec((tt, d), lambda i: (i, 0)), pl.BlockSpec((tt, f), lambda i: (i, 0)),
                   pl.BlockSpec((tt, f), lambda i: (i, 0)), pl.BlockSpec((tt, f), lambda i: (i, 0)),
                   pl.BlockSpec((tt, d), lambda i: (i, 0)), pl.BlockSpec((8, d), lambda i: (0, 0))],
        out_shape=[jax.ShapeDtypeStruct((t, d), F32), jax.ShapeDtypeStruct((t, f), BF16),
                   jax.ShapeDtypeStruct((t, f), BF16), jax.ShapeDtypeStruct((t, f), BF16),
                   jax.ShapeDtypeStruct((t, d), BF16), jax.ShapeDtypeStruct((8, d), F32)],
        compiler_params=_params(("arbitrary",)),
    )(dy, a, u, x1, g_ffn, w_gate, w_up, w_down)


def _in_proj_bwd(dparts, dfa_row, w1, wft, x, g_mix, dx1, rider=None):
    t, d = x.shape
    tt = 512
    npart = len(dparts)

    def body(*refs):
        dp_refs = refs[:npart]
        dfa_ref, w_ref, wf_ref, x_ref, g_ref, dx1_ref, dx_ref, dg_ref = refs[npart:]

        @pl.when(pl.program_id(0) == 0)
        def _():
            dg_ref[...] = jnp.zeros_like(dg_ref)

        dh = _dot(dfa_ref[...].astype(BF16), wf_ref[...], TN)
        for j in range(npart):
            dh = dh + _dot(dp_refs[j][...], w_ref[:, j * W_GROUP:(j + 1) * W_GROUP], NT)
        xx = x_ref[...]
        r = lax.rsqrt(jnp.mean(xx * xx, axis=-1, keepdims=True) + EPS)
        xhat = xx * r
        uu = dh * g_ref[...]
        dx_ref[...] = dx1_ref[...] + r * (uu - xhat * jnp.mean(uu * xhat, axis=-1, keepdims=True))
        dg_ref[0:1, :] += jnp.sum(dh * xhat, axis=0, keepdims=True)

    return _host_call(
        body, rider, name="in_proj_bwd", grid=(t // tt,),
        in_specs=[pl.BlockSpec((tt, W_GROUP), lambda i: (i, 0)) for _ in range(npart)]
        + [pl.BlockSpec((8, tt), lambda i: (0, i)), pl.BlockSpec(memory_space=pltpu.VMEM),
           pl.BlockSpec(memory_space=pltpu.VMEM), pl.BlockSpec((tt, d), lambda i: (i, 0)),
           pl.BlockSpec((1, d), lambda i: (0, 0)), pl.BlockSpec((tt, d), lambda i: (i, 0))],
        out_specs=[pl.BlockSpec((tt, d), lambda i: (i, 0)), pl.BlockSpec((8, d), lambda i: (0, 0))],
        out_shape=[jax.ShapeDtypeStruct((t, d), F32), jax.ShapeDtypeStruct((8, d), F32)],
        scratch_shapes=[], inputs=(*dparts, dfa_row, w1, wft, x, g_mix, dx1), semantics=("arbitrary",))


def _token_matmul(a, b, name, tn, a_is_transposed=True):
    m, t = a.shape if a_is_transposed else a.shape[::-1]
    n = b.shape[1]
    tk = 1024

    def body(a_ref, b_ref, o_ref):
        @pl.when(pl.program_id(1) == 0)
        def _():
            o_ref[...] = jnp.zeros_like(o_ref)

        o_ref[...] += _dot(a_ref[...], b_ref[...].astype(BF16), None if a_is_transposed else TN)

    a_spec = pl.BlockSpec((m, tk), lambda j, k: (0, k)) if a_is_transposed else pl.BlockSpec((tk, m), lambda j, k: (k, 0))
    return pl.pallas_call(
        body, name=name, grid=(n // tn, t // tk),
        in_specs=[a_spec, pl.BlockSpec((tk, tn), lambda j, k: (k, j))],
        out_specs=pl.BlockSpec((m, tn), lambda j, k: (0, j)),
        out_shape=jax.ShapeDtypeStruct((m, n), F32),
        compiler_params=_params(("arbitrary", "arbitrary")),
    )(a, b)


def _token_matmul_parts(at, parts, name):
    m, t = at.shape
    widths = [p.shape[1] for p in parts]
    tk = 1024

    def body(a_ref, *refs):
        o_ref = refs[-1]

        @pl.when(pl.program_id(0) == 0)
        def _():
            o_ref[...] = jnp.zeros_like(o_ref)

        a, first = a_ref[...], 0
        for b_ref, w in zip(refs[:-1], widths):
            o_ref[:, first:first + w] += _dot(a, b_ref[...])
            first += w

    return pl.pallas_call(
        body, name=name, grid=(t // tk,),
        in_specs=[pl.BlockSpec((m, tk), lambda k: (0, k))] + [pl.BlockSpec((tk, w), lambda k: (k, 0)) for w in widths],
        out_specs=pl.BlockSpec((m, sum(widths)), lambda k: (0, 0)),
        out_shape=jax.ShapeDtypeStruct((m, sum(widths)), F32),
        compiler_params=_params(("arbitrary",)),
    )(at, *parts)


def _row_matmul(a_row, b, name):
    t, n = b.shape
    tk = 1024
    nk = t // tk

    def body(a_ref, b_ref, o_ref):
        @pl.when(pl.program_id(0) == 0)
        def _():
            o_ref[...] = jnp.zeros_like(o_ref)

        o_ref[...] += _dot(a_ref[...].astype(BF16), b_ref[...])

    return pl.pallas_call(
        body, name=name, grid=(nk,),
        in_specs=[pl.BlockSpec((8, tk), lambda k: (0, k)), pl.BlockSpec((tk, n), lambda k: (k, 0))],
        out_specs=pl.BlockSpec((8, n), lambda k: (0, 0)),
        out_shape=jax.ShapeDtypeStruct((8, n), F32),
        compiler_params=_params(("arbitrary",)),
    )(a_row, b)


FOX_TQ = 512
SUM_LANE = (HEAD_DIM, 0)


def _fox_fwd(proj, c3, gq, gk, nb, seq, rider=None):
    t = nb * seq
    tq = FOX_TQ
    nq = seq // tq
    npair = N_FOX_HEADS // 2

    def body(q_ref, k_ref, v_ref, c_ref, gq_ref, gk_ref, o_ref, lse_ref, qs, ks, vs):
        ones = _group_ones()
        masks = _head_masks()
        qhat, _ = _head_norm(q_ref[...].astype(F32), ones)
        khat, _ = _head_norm(k_ref[...].astype(F32), ones)
        qs[...] = (qhat * gq_ref[...] * (SCALE * LOG2E)).astype(BF16)
        kn = khat * gk_ref[...]
        vv = v_ref[...].astype(F32)
        lane = lax.broadcasted_iota(jnp.int32, (1, LANES), 1)
        for hd in range(2):
            ks[hd] = (kn * masks[hd]).astype(BF16)
            vs[hd] = (vv * masks[hd] + (lane == SUM_LANE[hd]).astype(F32)).astype(BF16)
        row = lax.broadcasted_iota(jnp.int32, (tq, tq), 0)
        col = lax.broadcasted_iota(jnp.int32, (tq, tq), 1)
        causal = col <= row

        for qi in range(nq):
            q0 = qi * tq
            q_blk = qs[q0:q0 + tq, :]
            o_tot = jnp.zeros((tq, LANES), F32)
            lse_tot = jnp.zeros((tq, LANES), F32)
            for hd in range(2):
                crow = c_ref[0, hd:hd + 1, 0:q0 + tq] * LOG2E
                c0 = crow[:, q0:q0 + 1]
                s_d = _dot(q_blk, ks[hd, q0:q0 + tq, :], NT) + (c0 - crow[:, q0:q0 + tq])
                s_d = jnp.where(causal, s_d, NEG)
                m = jnp.max(s_d, axis=-1, keepdims=True)
                if qi > 0:
                    s_o = _dot(q_blk, ks[hd, 0:q0, :], NT) + (c0 - crow[:, 0:q0])
                    m = jnp.maximum(m, jnp.max(s_o, axis=-1, keepdims=True))
                acc = _dot(jnp.exp2(s_d - m).astype(BF16), vs[hd, q0:q0 + tq, :])
                if qi > 0:
                    acc = acc + _dot(jnp.exp2(s_o - m).astype(BF16), vs[hd, 0:q0, :])
                l = acc[:, SUM_LANE[hd]:SUM_LANE[hd] + 1]
                o_tot = o_tot + (acc / l) * masks[hd]
                lse_tot = lse_tot + (m + jnp.log2(l) - c0) * masks[hd]
            o_ref[q0:q0 + tq, :] = o_tot
            lse_ref[q0:q0 + tq, :] = lse_tot

    blk = lambda off: pl.BlockSpec((seq, LANES), lambda b, p: (b, off + p))
    return _host_call(
        body, rider, name="fox_fwd", grid=(nb, npair),
        in_specs=[blk(0), blk(npair), blk(2 * npair), pl.BlockSpec((1, 2, seq), lambda b, p: (p, 0, b)),
                  pl.BlockSpec((1, LANES), lambda b, p: (0, 0)), pl.BlockSpec((1, LANES), lambda b, p: (0, 0))],
        out_specs=[blk(0), blk(0)],
        out_shape=[jax.ShapeDtypeStruct((t, W_GROUP), F32), jax.ShapeDtypeStruct((t, W_GROUP), F32)],
        scratch_shapes=[pltpu.VMEM((seq, LANES), BF16), pltpu.VMEM((2, seq, LANES), BF16),
                        pltpu.VMEM((2, seq, LANES), BF16)],
        inputs=(proj, proj, proj, c3, gq, gk), semantics=("arbitrary", "arbitrary"))


def _fox_bwd(proj, c3, gq, gk, do, o, lse, nb, seq, rider=None):
    t = nb * seq
    tq = FOX_TQ
    nq = seq // tq
    npair = N_FOX_HEADS // 2

    def body(q_ref, k_ref, v_ref, c_ref, gq_ref, gk_ref, do_ref, o_ref, lse_ref,
             dq_ref, dk_ref, dv_ref, dc_ref, dg_ref, qs, ks, vs, kts, dos, lse_t, delta_t, dqt_acc, dk_acc, dv_acc,
             row_sum):
        @pl.when((pl.program_id(0) == 0) & (pl.program_id(1) == 0))
        def _():
            dg_ref[...] = jnp.zeros_like(dg_ref)

        ones = _group_ones()
        masks = _head_masks()
        qhat, rq = _head_norm(q_ref[...].astype(F32), ones)
        khat, rk = _head_norm(k_ref[...].astype(F32), ones)
        qs[...] = (qhat * gq_ref[...] * (SCALE * LOG2E)).astype(BF16)
        kn = khat * gk_ref[...]
        vv = v_ref[...].astype(F32)
        for hd in range(2):
            ks[hd] = (kn * masks[hd]).astype(BF16)
            vs[hd] = (vv * masks[hd]).astype(BF16)
            kts[hd] = ks[hd].T
        dof = do_ref[...]
        dos[...] = dof.astype(BF16)
        lse_t[...] = lse_ref[...].T
        delta_t[...] = _groupsum(dof * o_ref[...], ones).T
        dqt_acc[...] = jnp.zeros_like(dqt_acc)
        dk_acc[...] = jnp.zeros_like(dk_acc)
        dv_acc[...] = jnp.zeros_like(dv_acc)
        row_sum[...] = jnp.zeros_like(row_sum)
        key = lax.broadcasted_iota(jnp.int32, (tq, tq), 0)
        qry = lax.broadcasted_iota(jnp.int32, (tq, tq), 1)
        causal = key <= qry

        for hd in range(2):
            lane0 = hd * HEAD_DIM
            for kj in range(nq):
                k0 = kj * tq
                k_blk = ks[hd, k0:k0 + tq, :]
                v_blk = vs[hd, k0:k0 + tq, :]
                kt_blk = kts[hd, :, k0:k0 + tq]
                crow = c_ref[0, hd:hd + 1, k0:k0 + tq] * LOG2E
                ck0 = crow[:, 0:1]
                bias = jnp.broadcast_to(ck0 - crow, (LANES, tq)).T[:, 0:1]

                def queries_step(r0, r1, diag, hd=hd, lane0=lane0, k_blk=k_blk, v_blk=v_blk, kt_blk=kt_blk,
                                 bias=bias, ck0=ck0):
                    q_r = qs[r0:r1, :]
                    do_r = dos[r0:r1, :]
                    z = _dot(k_blk, q_r, NT) + bias
                    p = jnp.exp2(z - (lse_t[lane0:lane0 + 1, r0:r1] + ck0))
                    if diag:
                        p = jnp.where(causal, p, 0.0)
                    dp = _dot(v_blk, do_r, NT)
                    ds = p * (dp - delta_t[lane0:lane0 + 1, r0:r1])
                    dsb = ds.astype(BF16)
                    dqt_acc[:, r0:r1] += _dot(kt_blk, dsb)
                    row_sum[hd:hd + 1, r0:r1] += jnp.sum(ds, axis=0, keepdims=True)
                    return _dot(dsb, q_r), _dot(p.astype(BF16), do_r), -jnp.sum(ds, axis=1, keepdims=True)

                dk_j, dv_j, dc_j = queries_step(k0, k0 + tq, True)
                if k0 + tq < seq:
                    dk_o, dv_o, dc_o = queries_step(k0 + tq, seq, False)
                    dk_j, dv_j, dc_j = dk_j + dk_o, dv_j + dv_o, dc_j + dc_o
                dk_acc[k0:k0 + tq, :] += dk_j * masks[hd]
                dv_acc[k0:k0 + tq, :] += dv_j * masks[hd]
                dc_ref[0, hd:hd + 1, k0:k0 + tq] = jnp.broadcast_to(dc_j, (tq, LANES)).T[0:1, :]

        dc_ref[0] += row_sum[0:2, :]

        dq_raw, dgq = _head_norm_bwd(dqt_acc[...].T * SCALE, qhat, rq, gq_ref[...], ones)
        dk_raw, dgk = _head_norm_bwd(dk_acc[...] * LN2, khat, rk, gk_ref[...], ones)
        dq_ref[...] = dq_raw.astype(BF16)
        dk_ref[...] = dk_raw.astype(BF16)
        dv_ref[...] = dv_acc[...].astype(BF16)
        dg_ref[0:1, :] += dgq
        dg_ref[1:2, :] += dgk

    blk = lambda off: pl.BlockSpec((seq, LANES), lambda b, p: (b, off + p))
    vec = pl.BlockSpec((1, LANES), lambda b, p: (0, 0))
    c_spec = pl.BlockSpec((1, 2, seq), lambda b, p: (p, 0, b))
    return _host_call(
        body, rider, name="fox_bwd", grid=(nb, npair),
        in_specs=[blk(0), blk(npair), blk(2 * npair), c_spec, vec, vec, blk(0), blk(0), blk(0)],
        out_specs=[blk(0), blk(0), blk(0), c_spec, pl.BlockSpec((8, LANES), lambda b, p: (0, 0))],
        out_shape=[jax.ShapeDtypeStruct((t, W_GROUP), BF16), jax.ShapeDtypeStruct((t, W_GROUP), BF16),
                   jax.ShapeDtypeStruct((t, W_GROUP), BF16), jax.ShapeDtypeStruct((npair, 2, t), F32),
                   jax.ShapeDtypeStruct((8, LANES), F32)],
        scratch_shapes=[pltpu.VMEM((seq, LANES), BF16), pltpu.VMEM((2, seq, LANES), BF16),
                        pltpu.VMEM((2, seq, LANES), BF16), pltpu.VMEM((2, LANES, seq), BF16),
                        pltpu.VMEM((seq, LANES), BF16), pltpu.VMEM((LANES, seq), F32),
                        pltpu.VMEM((LANES, seq), F32), pltpu.VMEM((LANES, seq), F32),
                        pltpu.VMEM((seq, LANES), F32), pltpu.VMEM((seq, LANES), F32),
                        pltpu.VMEM((8, seq), F32)],
        inputs=(proj, proj, proj, c3, gq, gk, do, o, lse), semantics=("arbitrary", "arbitrary"))


def _dil_prep(q_ref, k_ref, gq_ref, gk_ref, cos_ref, up_ref, dn_ref, ones):
    qhat, rq = _head_norm(q_ref[...].astype(F32), ones)
    khat, rk = _head_norm(k_ref[...].astype(F32), ones)
    cos, up, dn = cos_ref[...], up_ref[...], dn_ref[...]
    qn = _rope(qhat * gq_ref[...], cos, up, dn) * (SCALE * LOG2E)
    kn = _rope(khat * gk_ref[...], cos, up, dn)
    return qhat, rq, khat, rk, qn, kn


def _dil_keys(d, seq, pairs):
    nblk = seq // BAND
    per_res = seq // (d * BAND)
    as_blocks = lambda ref, rows: ref[rows, :].reshape(-1, BAND, LANES)
    if per_res == 1:
        a = lax.broadcasted_iota(jnp.int32, (1, BAND, BAND), 1)
        j = lax.broadcasted_iota(jnp.int32, (1, BAND, BAND), 2)
        causal = jnp.where(j <= a, 0.0, NEG)
        return [as_blocks(src, slice(0, seq)) for src, _ in pairs], [causal]
    for src, dst in pairs:
        dst[:, BAND:, :] = as_blocks(src, slice(0, seq))
        dst[1:, :BAND, :] = as_blocks(src, slice(0, seq - BAND))
        dst[0:1, :BAND, :] = jnp.zeros((1, BAND, LANES), BF16)
    a = lax.broadcasted_iota(jnp.int32, (1, BAND, 2 * BAND), 1)
    j = lax.broadcasted_iota(jnp.int32, (1, BAND, 2 * BAND), 2)
    band = jnp.where(((j < BAND) & (j >= a)) | ((j >= BAND) & (j - BAND <= a)), 0.0, NEG)
    e = lax.broadcasted_iota(jnp.int32, (nblk, 1, 2 * BAND), 0)
    j = lax.broadcasted_iota(jnp.int32, (nblk, 1, 2 * BAND), 2)
    no_prev = jnp.where(((e & (per_res - 1)) == 0) & (j < BAND), NEG, 0.0)
    return [dst[...] for _, dst in pairs], [band + no_prev]


def _regroup(d, seq):
    if d == 1:
        return [(slice(0, seq), slice(0, seq))]
    before, n = d // 4, seq // d
    return [(pl.ds(r1 * (seq // before) + r2, n, stride=4), slice((before * r2 + r1) * n, (before * r2 + r1 + 1) * n))
            for r1 in range(before) for r2 in range(4)]


def _dil_fwd(proj, gq, gk, cos, up, dn, nb, seq):
    t = nb * seq
    npair = W_GROUP // LANES
    off = 3 * npair

    def body(q_ref, k_ref, v_ref, gq_ref, gk_ref, cos_ref, up_ref, dn_ref, o_ref, lse_ref,
             src_a, src_b, qp, kp, vp, kw, vw, m_b, l_b, o_b, state_a, state_b):
        ones = _group_ones()
        masks = _head_masks()
        _, _, _, _, qn, kn = _dil_prep(q_ref, k_ref, gq_ref, gk_ref, cos_ref, up_ref, dn_ref, ones)
        src_a[0] = qn
        src_a[1] = kn
        src_a[2] = v_ref[...].astype(F32)
        nblk = seq // BAND
        src, state = (src_a, src_b), (state_a, state_b)

        for d in DILATIONS:
            last = d == DILATIONS[-1]
            for before, after in _regroup(d, seq):
                qv, kv, vv = src[0].at[0][before, :], src[0].at[1][before, :], src[0].at[2][before, :]
                for hd in range(2):
                    qp[hd, after, :] = (qv * masks[hd]).astype(BF16)
                kp[after, :] = kv.astype(BF16)
                vp[after, :] = vv.astype(BF16)
                if d > 1 and not last:
                    src[1][0, after, :], src[1][1, after, :], src[1][2, after, :] = qv, kv, vv
            if d > 1:
                src = src[::-1]
            (keys_k, keys_v), bias = _dil_keys(d, seq, [(kp, kw), (vp, vw)])
            m_t = jnp.zeros((nblk, BAND, LANES), F32)
            l_t = jnp.zeros((nblk, BAND, LANES), F32)
            o_t = jnp.zeros((nblk, BAND, LANES), F32)
            for hd in range(2):
                s = _dot(qp[hd].reshape(nblk, BAND, LANES), keys_k, BATCH_NT)
                for b_ in bias:
                    s = s + b_
                m = jnp.max(s, axis=-1, keepdims=True)
                p = jnp.exp2(s - m)
                m_t = m_t + m * masks[hd]
                l_t = l_t + jnp.sum(p, axis=-1, keepdims=True) * masks[hd]
                o_t = o_t + _dot(p.astype(BF16), keys_v, BATCH_NN) * masks[hd]
            if d == 1:
                state[0][0] = m_t.reshape(seq, LANES)
                state[0][1] = l_t.reshape(seq, LANES)
                state[0][2] = o_t.reshape(seq, LANES)
                continue
            m_b[...] = m_t.reshape(seq, LANES)
            l_b[...] = l_t.reshape(seq, LANES)
            o_b[...] = o_t.reshape(seq, LANES)
            for before, after in _regroup(d, seq):
                m_old = state[0].at[0][before, :]
                m_new = jnp.maximum(m_old, m_b[after, :])
                w_old = jnp.exp2(m_old - m_new)
                w_new = jnp.exp2(m_b[after, :] - m_new)
                state[1][0, after, :] = m_new
                state[1][1, after, :] = state[0].at[1][before, :] * w_old + l_b[after, :] * w_new
                state[1][2, after, :] = state[0].at[2][before, :] * w_old + o_b[after, :] * w_new
            state = state[::-1]

        l = state[0][1]
        o_b[...] = state[0][2] / l
        l_b[...] = state[0][0] + jnp.log2(l)
        held, spare = [o_b, l_b], [m_b, state[1].at[0]]
        for d in DILATIONS[:0:-1]:
            dests = [o_ref, lse_ref] if d == DILATIONS[1] else spare
            for h, dst in zip(held, dests):
                for before, after in _regroup(d, seq):
                    dst[before, :] = h[after, :]
            held, spare = dests, held

    blk = lambda o_: pl.BlockSpec((seq, LANES), lambda b, p: (b, o_ + p))
    vec = pl.BlockSpec((1, LANES), lambda b, p: (0, 0))
    tab = pl.BlockSpec(memory_space=pltpu.VMEM)
    f32_buf = pltpu.VMEM((seq, LANES), F32)
    f32_x3 = pltpu.VMEM((3, seq, LANES), F32)
    bf16_buf = pltpu.VMEM((seq, LANES), BF16)
    window_buf = pltpu.VMEM((seq // BAND, 2 * BAND, LANES), BF16)
    return pl.pallas_call(
        body, name="dil_fwd", grid=(nb, npair),
        in_specs=[blk(off), blk(off + npair), blk(off + 2 * npair), vec, vec, tab, tab, tab],
        out_specs=[blk(0), blk(0)],
        out_shape=[jax.ShapeDtypeStruct((t, W_GROUP), F32), jax.ShapeDtypeStruct((t, W_GROUP), F32)],
        scratch_shapes=[f32_x3, f32_x3, pltpu.VMEM((2, seq, LANES), BF16), bf16_buf, bf16_buf,
                        window_buf, window_buf, f32_buf, f32_buf, f32_buf, f32_x3, f32_x3],
        compiler_params=_params(("arbitrary", "arbitrary")),
    )(proj, proj, proj, gq, gk, cos, up, dn)


def _dil_bwd(proj, gq, gk, cos, up, dn, do, o, lse, nb, seq, rider=None):
    t = nb * seq
    npair = W_GROUP // LANES
    off = 3 * npair

    def body(q_ref, k_ref, v_ref, gq_ref, gk_ref, cos_ref, up_ref, dn_ref, do_ref, o_ref, lse_ref,
             dq_ref, dk_ref, dv_ref, dg_ref, src_a, src_b, sums_a, sums_b,
             qp, kp, vp, dop, kw, vw, lse_p, delta_p, dq_p, dk_p, dv_p):
        @pl.when((pl.program_id(0) == 0) & (pl.program_id(1) == 0))
        def _():
            dg_ref[...] = jnp.zeros_like(dg_ref)

        ones = _group_ones()
        masks = _head_masks()
        qhat, rq, khat, rk, qn, kn = _dil_prep(q_ref, k_ref, gq_ref, gk_ref, cos_ref, up_ref, dn_ref, ones)
        src_a[0] = qn
        src_a[1] = kn
        src_a[2] = v_ref[...].astype(F32)
        src_a[3] = do_ref[...]
        src_a[4] = lse_ref[...]
        src_a[5] = _groupsum(do_ref[...] * o_ref[...], ones)
        nblk = seq // BAND
        src, sums = (src_a, src_b), (sums_a, sums_b)

        for d in DILATIONS:
            last = d == DILATIONS[-1]
            for before, after in _regroup(d, seq):
                planes = [src[0].at[i][before, :] for i in range(6)]
                for hd in range(2):
                    qp[hd, after, :] = (planes[0] * masks[hd]).astype(BF16)
                    dop[hd, after, :] = (planes[3] * masks[hd]).astype(BF16)
                kp[after, :] = planes[1].astype(BF16)
                vp[after, :] = planes[2].astype(BF16)
                lse_p[after, :] = planes[4]
                delta_p[after, :] = planes[5]
                if d > 1 and not last:
                    for i in range(6):
                        src[1][i, after, :] = planes[i]
            if d > 1:
                src = src[::-1]
            (keys_k, keys_v), bias = _dil_keys(d, seq, [(kp, kw), (vp, vw)])
            nk = keys_k.shape[1]
            dq_b = jnp.zeros((nblk, BAND, LANES), F32)
            dk_b = jnp.zeros((nblk, nk, LANES), F32)
            dv_b = jnp.zeros((nblk, nk, LANES), F32)
            for hd in range(2):
                lane0 = hd * HEAD_DIM
                q3 = qp[hd].reshape(nblk, BAND, LANES)
                do3 = dop[hd].reshape(nblk, BAND, LANES)
                z = _dot(q3, keys_k, BATCH_NT)
                for b_ in bias:
                    z = z + b_
                p = jnp.exp2(z - lse_p[...].reshape(nblk, BAND, LANES)[:, :, lane0:lane0 + 1])
                dp = _dot(do3, keys_v, BATCH_NT)
                ds = (p * (dp - delta_p[...].reshape(nblk, BAND, LANES)[:, :, lane0:lane0 + 1])).astype(BF16)
                dq_b = dq_b + _dot(ds, keys_k, BATCH_NN) * masks[hd]
                dk_b = dk_b + _dot(ds, q3, BATCH_TN)
                dv_b = dv_b + _dot(p.astype(BF16), do3, BATCH_TN)
            dq_p[...] = dq_b.reshape(seq, LANES)
            for acc, out in ((dk_b, dk_p), (dv_b, dv_p)):
                out[...] = acc[:, nk - BAND:, :].reshape(seq, LANES)
                if nk > BAND:
                    out[0:seq - BAND, :] += acc[1:, :BAND, :].reshape(seq - BAND, LANES)
            if d == 1:
                sums[0][0], sums[0][1], sums[0][2] = dq_p[...], dk_p[...], dv_p[...]
                continue
            for before, after in _regroup(d, seq):
                for i, part in enumerate((dq_p, dk_p, dv_p)):
                    sums[1][i, after, :] = sums[0].at[i][before, :] + part[after, :]
            sums = sums[::-1]

        for d in DILATIONS[:0:-1]:
            for i in range(3):
                for before, after in _regroup(d, seq):
                    sums[1].at[i][before, :] = sums[0][i, after, :]
            sums = sums[::-1]

        cos, up, dn = cos_ref[...], up_ref[...], dn_ref[...]
        dq_raw, dgq = _head_norm_bwd(_rope_bwd(sums[0][0] * SCALE, cos, up, dn), qhat, rq, gq_ref[...], ones)
        dk_raw, dgk = _head_norm_bwd(_rope_bwd(sums[0][1] * LN2, cos, up, dn), khat, rk, gk_ref[...], ones)
        dq_ref[...] = dq_raw.astype(BF16)
        dk_ref[...] = dk_raw.astype(BF16)
        dv_ref[...] = sums[0][2].astype(BF16)
        dg_ref[0:1, :] += dgq
        dg_ref[1:2, :] += dgk

    blk = lambda o_: pl.BlockSpec((seq, LANES), lambda b, p: (b, o_ + p))
    vec = pl.BlockSpec((1, LANES), lambda b, p: (0, 0))
    tab = pl.BlockSpec(memory_space=pltpu.VMEM)
    f32_buf = pltpu.VMEM((seq, LANES), F32)
    bf16_buf = pltpu.VMEM((seq, LANES), BF16)
    window_buf = pltpu.VMEM((seq // BAND, 2 * BAND, LANES), BF16)
    bf16_pair = pltpu.VMEM((2, seq, LANES), BF16)
    return _host_call(
        body, rider, name="dil_bwd", grid=(nb, npair),
        in_specs=[blk(off), blk(off + npair), blk(off + 2 * npair), vec, vec, tab, tab, tab,
                  blk(0), blk(0), blk(0)],
        out_specs=[blk(0), blk(0), blk(0), pl.BlockSpec((8, LANES), lambda b, p: (0, 0))],
        out_shape=[jax.ShapeDtypeStruct((t, W_GROUP), BF16), jax.ShapeDtypeStruct((t, W_GROUP), BF16),
                   jax.ShapeDtypeStruct((t, W_GROUP), BF16), jax.ShapeDtypeStruct((8, LANES), F32)],
        scratch_shapes=[pltpu.VMEM((6, seq, LANES), F32)] * 2 + [pltpu.VMEM((3, seq, LANES), F32)] * 2
        + [bf16_pair, bf16_buf, bf16_buf, bf16_pair, window_buf, window_buf] + [f32_buf] * 5,
        inputs=(proj, proj, proj, gq, gk, cos, up, dn, do, o, lse), semantics=("arbitrary", "arbitrary"))


def _adamw(w, g, m, v, name, rider=None):
    row_major = w.ndim == 3 and w.shape[1] == 1
    rows, cols = (w.shape[0], w.shape[2]) if row_major else w.shape[-2:]
    if row_major:
        tr = max(t for t in range(1, 65) if rows % t == 0)
    else:
        tr = _row_tile(rows) if rows >= 8 else rows
    c1 = 1.0 - ADAM_B1 ** ADAM_STEP
    c2 = 1.0 - ADAM_B2 ** ADAM_STEP

    def body(w_ref, g_ref, m_ref, v_ref, d_ref, nm_ref, nv_ref):
        g_ = g_ref[...]
        nm = ADAM_B1 * m_ref[...] + (1.0 - ADAM_B1) * g_
        nv = ADAM_B2 * v_ref[...] + (1.0 - ADAM_B2) * (g_ * g_)
        nm_ref[...] = nm
        nv_ref[...] = nv
        d_ref[...] = -ADAM_LR * ((nm / c1) / (jnp.sqrt(nv / c2) + ADAM_EPS) + ADAM_WD * w_ref[...])

    if row_major:
        spec = pl.BlockSpec((tr, 1, cols), lambda i: (i, 0, 0))
    elif w.ndim == 3:
        spec = pl.BlockSpec((1, tr, cols), lambda i: (0, i, 0))
    else:
        spec = pl.BlockSpec((tr, cols), lambda i: (i, 0))
    shape = jax.ShapeDtypeStruct(w.shape, F32)
    return _host_call(
        body, rider, name=name, grid=(rows // tr,), in_specs=[spec] * 4, out_specs=[spec] * 3,
        out_shape=[shape] * 3, scratch_shapes=[], inputs=(w, g, m, v), semantics=("arbitrary",))


def _place():
    x, y, c = lax.axis_index("x"), lax.axis_index("y"), lax.axis_index("c")
    chips = [(1 - x, y), (x, 1 - y), (1 - x, 1 - y)]
    return x, y, c, chips


def _gather_weight(w, name):
    _, rows, cols = w.shape
    half_rows = rows // 2

    def body(w_ref, out_ref, send_sems, recv_sems):
        x, y, c, chips = _place()
        sibling = (x, y, 1 - c)
        mine = 2 * x + y
        lo = pl.multiple_of(c * half_rows, 16)
        lo_sib = pl.multiple_of((1 - c) * half_rows, 16)
        out_ref[mine] = w_ref[0].astype(BF16)

        def copy(k, shard, first_row, to):
            ref = out_ref.at[shard, pl.ds(first_row, half_rows), :]
            return pltpu.make_async_remote_copy(src_ref=ref, dst_ref=ref, send_sem=send_sems.at[k],
                                                recv_sem=recv_sems.at[k], device_id=to, device_id_type=MESH)

        sends = [copy(k, mine, lo, (cx, cy, c)) for k, (cx, cy) in enumerate(chips)]
        for cp in sends:
            cp.start()
        passed = []
        for k, (cx, cy) in enumerate(chips):
            theirs = 2 * cx + cy
            copy(k, theirs, lo, (cx, cy, c)).wait_recv()
            fw = copy(3 + k, theirs, lo, sibling)
            fw.start()
            passed.append(fw)
        for k, (cx, cy) in enumerate(chips):
            copy(3 + k, 2 * cx + cy, lo_sib, sibling).wait_recv()
        for cp in sends + passed:
            cp.wait_send()

    return pl.pallas_call(
        body, name=name,
        in_specs=[pl.BlockSpec(memory_space=pltpu.VMEM)],
        out_specs=pl.BlockSpec(memory_space=pltpu.VMEM),
        out_shape=jax.ShapeDtypeStruct((4, rows, cols), BF16),
        scratch_shapes=[pltpu.SemaphoreType.DMA((6,)), pltpu.SemaphoreType.DMA((6,))],
        compiler_params=pltpu.CompilerParams(vmem_limit_bytes=VMEM_LIMIT),
    )(w)


def _remote(src, dst, sems, k, to):
    send_sems, recv_sems = sems
    return pltpu.make_async_remote_copy(src_ref=src, dst_ref=dst, send_sem=send_sems.at[k], recv_sem=recv_sems.at[k],
                                        device_id=to, device_id_type=MESH)


def _cast_bf16(parts, name):
    def body(*refs):
        for src, dst in zip(refs[:len(parts)], refs[len(parts):]):
            dst[...] = src[0].astype(BF16)

    return pl.pallas_call(
        body, name=name, in_specs=[pl.BlockSpec(memory_space=pltpu.VMEM)] * len(parts),
        out_specs=[pl.BlockSpec(memory_space=pltpu.VMEM)] * len(parts),
        out_shape=[jax.ShapeDtypeStruct(p.shape[1:], BF16) for p in parts],
        compiler_params=pltpu.CompilerParams(vmem_limit_bytes=VMEM_LIMIT),
    )(*parts)


def _gather_rider(shards):
    def copies(ins, outs, sems, which):
        x, y, c, chips = _place()
        sibling = (x, y, 1 - c)
        mine = 2 * x + y
        made = {name: [] for name in which}
        for i, (p_ref, g_ref) in enumerate(zip(ins, outs)):
            half = p_ref.shape[0] // 2
            lo = pl.multiple_of(c * half, 16)
            lo_sib = pl.multiple_of((1 - c) * half, 16)
            spot = lambda shard, first, g_ref=g_ref, half=half: g_ref.at[shard, pl.ds(first, half), :]
            groups = {
                "own": lambda: [pltpu.make_async_copy(p_ref, g_ref.at[mine], sems[0].at[7 * i + 6])],
                "sends": lambda: [_remote(p_ref.at[pl.ds(lo, half), :], spot(mine, lo), sems, 7 * i + k, (cx, cy, c))
                                  for k, (cx, cy) in enumerate(chips)],
                "arrivals": lambda: [_remote(spot(2 * cx + cy, lo), spot(2 * cx + cy, lo), sems, 7 * i + k, (cx, cy, c))
                                     for k, (cx, cy) in enumerate(chips)],
                "passes": lambda: [_remote(spot(2 * cx + cy, lo), spot(2 * cx + cy, lo), sems, 7 * i + 3 + k, sibling)
                                   for k, (cx, cy) in enumerate(chips)],
                "from_sibling": lambda: [_remote(spot(2 * cx + cy, lo_sib), spot(2 * cx + cy, lo_sib), sems,
                                                 7 * i + 3 + k, sibling) for k, (cx, cy) in enumerate(chips)],
            }
            for name in which:
                made[name] += groups[name]()
        return [made[name] for name in which]

    def start(ins, outs, send_sems, recv_sems):
        own, sends = copies(ins, outs, (send_sems, recv_sems), ("own", "sends"))
        for cp in own + sends:
            cp.start()

    def middle(ins, outs, send_sems, recv_sems):
        arrivals, passes = copies(ins, outs, (send_sems, recv_sems), ("arrivals", "passes"))
        for landed, onward in zip(arrivals, passes):
            landed.wait_recv()
            onward.start()

    def finish(ins, outs, send_sems, recv_sems):
        own, sends, passes, from_sibling = copies(ins, outs, (send_sems, recv_sems),
                                                  ("own", "sends", "passes", "from_sibling"))
        for cp in from_sibling:
            cp.wait_recv()
        for cp in sends + passes:
            cp.wait_send()
        for cp in own:
            cp.wait()

    shapes = [jax.ShapeDtypeStruct((4,) + s.shape, BF16) for s in shards]
    return _Rider(shards, shapes, 7 * len(shards), start, finish, middle=middle)


def _exchange_rider(inputs, out_shapes, n_sems, copies, aliases=None):
    def start(ins, outs, send_sems, recv_sems):
        for cp in copies(ins, outs, (send_sems, recv_sems)):
            cp.start()

    def finish(ins, outs, send_sems, recv_sems):
        for cp in copies(ins, outs, (send_sems, recv_sems)):
            cp.wait()

    return _Rider(inputs, out_shapes, n_sems, start, finish, aliases)


def _swap_rider(grads4):
    halves = [g.shape[1] // 2 for g in grads4]

    def copies(ins, outs, sems):
        x, y, c, _ = _place()
        return [_remote(g.at[:, pl.ds(pl.multiple_of((1 - c) * h, 8), h), :], a, sems, i, (x, y, 1 - c))
                for i, (g, a, h) in enumerate(zip(ins, outs, halves))]

    shapes = [jax.ShapeDtypeStruct((4, h, g.shape[2]), F32) for g, h in zip(grads4, halves)]
    return _exchange_rider(grads4, shapes, len(grads4), copies)


def _chip_sum(g4, from_sibling, name):
    _, rows, cols = g4.shape
    half = rows // 2

    def body(g_ref, s_ref, stage_ref, own_ref):
        x, y, c, chips = _place()
        lo = pl.multiple_of(c * half, 8)
        for k, (cx, cy) in enumerate(chips):
            theirs = 2 * cx + cy
            stage_ref[k] = (g_ref[theirs, pl.ds(lo, half), :] + s_ref[theirs]).astype(BF16)
        mine = 2 * x + y
        own_ref[...] = g_ref[mine, pl.ds(lo, half), :] + s_ref[mine]

    return pl.pallas_call(
        body, name=name, in_specs=[pl.BlockSpec(memory_space=pltpu.VMEM)] * 2,
        out_specs=[pl.BlockSpec(memory_space=pltpu.VMEM)] * 2,
        out_shape=[jax.ShapeDtypeStruct((3, half, cols), BF16), jax.ShapeDtypeStruct((half, cols), F32)],
        compiler_params=pltpu.CompilerParams(vmem_limit_bytes=VMEM_LIMIT),
    )(g4, from_sibling)


def _spread_rider(stages):
    def copies(ins, outs, sems):
        _, _, c, chips = _place()
        return [_remote(st.at[k], ld.at[k], sems, 3 * i + k, (cx, cy, c))
                for i, (st, ld) in enumerate(zip(ins, outs)) for k, (cx, cy) in enumerate(chips)]

    shapes = [jax.ShapeDtypeStruct(s.shape, s.dtype) for s in stages]
    return _exchange_rider(stages, shapes, 3 * len(stages), copies)


def _finish_half(own, landed, name):
    half, cols = own.shape

    def body(own_ref, landed_ref, out_ref):
        c = lax.axis_index("c")
        acc = own_ref[...]
        for k in range(3):
            acc = acc + landed_ref[k].astype(F32)
        out_ref[pl.ds(pl.multiple_of(c * half, 8), half), :] = acc

    return pl.pallas_call(
        body, name=name, in_specs=[pl.BlockSpec(memory_space=pltpu.VMEM)] * 2,
        out_specs=pl.BlockSpec(memory_space=pltpu.VMEM),
        out_shape=jax.ShapeDtypeStruct((2 * half, cols), F32),
        compiler_params=pltpu.CompilerParams(vmem_limit_bytes=VMEM_LIMIT),
    )(own, landed)


def _share_rider(fulls, first_sem=0):
    def copies(ins, outs, sems):
        x, y, c, _ = _place()
        out = []
        for i, full in enumerate(outs):
            half = full.shape[0] // 2
            rows = full.at[pl.ds(pl.multiple_of(c * half, 8), half), :]
            out.append(_remote(rows, rows, sems, first_sem + i, (x, y, 1 - c)))
        return out

    def finish_copies(ins, outs, sems):
        x, y, c, _ = _place()
        out = []
        for i, full in enumerate(outs):
            half = full.shape[0] // 2
            mine = full.at[pl.ds(pl.multiple_of(c * half, 8), half), :]
            theirs = full.at[pl.ds(pl.multiple_of((1 - c) * half, 8), half), :]
            k = first_sem + i
            out.append((_remote(mine, mine, sems, k, (x, y, 1 - c)), _remote(theirs, theirs, sems, k, (x, y, 1 - c))))
        return out

    def start(ins, outs, send_sems, recv_sems):
        for cp in copies(ins, outs, (send_sems, recv_sems)):
            cp.start()

    def finish(ins, outs, send_sems, recv_sems):
        for sent, landed in finish_copies(ins, outs, (send_sems, recv_sems)):
            sent.wait_send()
            landed.wait_recv()

    shapes = [jax.ShapeDtypeStruct(f.shape, f.dtype) for f in fulls]
    return _Rider(fulls, shapes, first_sem + len(fulls), start, finish, aliases={i: i for i in range(len(fulls))})


def _join_riders(a, b):
    n_in, n_out = len(a.inputs), len(a.out_shapes)

    def both(which):
        def run(ins, outs, send_sems, recv_sems):
            getattr(a, which)(ins[:n_in], outs[:n_out], send_sems, recv_sems)
            getattr(b, which)(ins[n_in:], outs[n_out:], send_sems, recv_sems)
        return run

    aliases = dict(a.aliases)
    aliases.update({n_in + i: n_out + o for i, o in b.aliases.items()})
    return _Rider(a.inputs + b.inputs, a.out_shapes + b.out_shapes, max(a.n_sems, b.n_sems), both("start"),
                  both("finish"), aliases)


def _all_sum_small(v):
    shape = v.shape

    def body(v_ref, out_ref, buf, send_sems, recv_sems):
        x, y, c, _ = _place()
        me = 4 * x + 2 * y + c
        buf[me] = v_ref[...]
        flips = [(dx, dy, dc) for dx in (0, 1) for dy in (0, 1) for dc in (0, 1)][1:]

        def copy(k, slot, flip):
            dx, dy, dc = flip
            to = (1 - x if dx else x, 1 - y if dy else y, 1 - c if dc else c)
            return pltpu.make_async_remote_copy(src_ref=buf.at[slot], dst_ref=buf.at[slot], send_sem=send_sems.at[k],
                                                recv_sem=recv_sems.at[k], device_id=to, device_id_type=MESH)

        sends = [copy(k, me, flip) for k, flip in enumerate(flips)]
        for cp in sends:
            cp.start()
        for k, (dx, dy, dc) in enumerate(flips):
            sender = 4 * (1 - x if dx else x) + 2 * (1 - y if dy else y) + (1 - c if dc else c)
            copy(k, sender, (dx, dy, dc)).wait_recv()
        for cp in sends:
            cp.wait_send()
        total = buf[0]
        for i in range(1, 8):
            total = total + buf[i]
        out_ref[...] = total

    return pl.pallas_call(
        body, name="all_sum_small",
        in_specs=[pl.BlockSpec(memory_space=pltpu.VMEM)],
        out_specs=pl.BlockSpec(memory_space=pltpu.VMEM),
        out_shape=jax.ShapeDtypeStruct(shape, F32),
        scratch_shapes=[pltpu.VMEM((8,) + shape, F32), pltpu.SemaphoreType.DMA((7,)), pltpu.SemaphoreType.DMA((7,))],
    )(v)


SMALL = (("g_mix", 1024), ("g_ffn", 1024), ("g_out_fox", 512), ("g_out_dil", 512), ("g_q_fox", 64),
         ("g_k_fox", 64), ("g_q_dil", 64), ("g_k_dil", 64), ("b_forget", 8))
SMALL_PACKED = (32, LANES)


def _local_grads(x, target, gains, w1, wft, dense, packed, nb, seq):
    tile2 = lambda g: jnp.tile(g, (1, 2))
    gq_f, gk_f, gq_d, gk_d = (tile2(gains[n]) for n in ("g_q_fox", "g_k_fox", "g_q_dil", "g_k_dil"))
    b_col = gains["b_forget"].reshape(N_FOX_HEADS, 1)
    cos, up, dn = _rope_tables(seq)
    npair = N_FOX_HEADS // 2

    proj, fa_row, h1, h1_t = _in_proj(x, gains["g_mix"], w1, wft)
    c_row = _gate_fwd(fa_row, b_col, seq)
    c3 = c_row.reshape(npair, 2, nb * seq)
    (o_fox, lse_fox), gathered = _fox_fwd(proj, c3, gq_f, gk_f, nb, seq,
                                          rider=None if packed is None else _gather_rider(packed))
    if packed is not None:
        dense = [g.reshape(-1, g.shape[2]) for g in gathered]
    w_out, w_gate, w_up, w_down = dense
    o_dil, lse_dil = _dil_fwd(proj, gq_d, gk_d, cos, up, dn, nb, seq)
    x1, o_n_t = _attn_out(o_fox, o_dil, x, gains["g_out_fox"], gains["g_out_dil"], w_out)
    a, u, dy, loss_parts = _ffn_fwd(x1, target, gains["g_ffn"], w_gate, w_up, w_down)
    loss = jnp.sum(loss_parts[:, 0, 0])

    dx1, s, da, du, h2, dg_ffn = _ffn_bwd(dy, a, u, x1, gains["g_ffn"], w_gate, w_up, w_down)
    d_w_down = _token_matmul(s, dy, "dw_down", 512, False)
    d_w_gate = _token_matmul(da, h2, "dw_gate", 512, False)
    d_w_up = _token_matmul(du, h2, "dw_up", 512, False)
    d_w_out = _token_matmul(o_n_t, dx1, "dw_out", 1024)
    names = ("w_out", "w_gate", "w_up", "w_down")
    grads4 = [g.reshape(4, -1, g.shape[1]) for g in (d_w_out, d_w_gate, d_w_up, d_w_down)]
    exchange = packed is not None
    (do_fox, do_dil, dg_of, dg_od), _ = _attn_out_bwd(
        dx1, o_fox, o_dil, gains["g_out_fox"], gains["g_out_dil"], w_out)
    (dq_f, dk_f, dv_f, dc3, dg_fox), from_sibling = _fox_bwd(
        proj, c3, gq_f, gk_f, do_fox, o_fox, lse_fox, nb, seq,
        rider=_swap_rider(grads4) if exchange else None)
    if exchange:
        sums = [_chip_sum(g, s, "chip_sum_" + n) for g, s, n in zip(grads4, from_sibling, names)]
    (dq_d, dk_d, dv_d, dg_dil), landed = _dil_bwd(
        proj, gq_d, gk_d, cos, up, dn, do_dil, o_dil, lse_dil, nb, seq,
        rider=_spread_rider([st for st, _ in sums]) if exchange else None)
    if exchange:
        halves = [_finish_half(own, ld, "finish_half_" + n) for (_, own), ld, n in zip(sums, landed, names)]
    dfa_row, db = _gate_bwd(dc3.reshape(N_FOX_HEADS, nb * seq), fa_row, b_col, seq)
    dparts = [dq_f, dk_f, dv_f, dq_d, dk_d, dv_d]
    d_w1 = _token_matmul_parts(h1_t, dparts, "dw_in")
    d_wf = _row_matmul(dfa_row, h1, "dw_forget")
    fox_w = 3 * W_GROUP
    in_order = [(d_w1[:, :fox_w], fox_w), (d_wf.T, N_FOX_HEADS), (d_w1[:, fox_w:], d_w1.shape[1] - fox_w)]
    n_cols = d_w1.shape[1] + N_FOX_HEADS
    if exchange:
        shards = [jnp.stack([_pick_columns(in_order, s * n_cols // 4, (s + 1) * n_cols // 4) for s in range(4)])]
        _, from_sibling = _idle_host(_swap_rider(shards), "swap_w_in")
        stage, own = _chip_sum(shards[0], from_sibling[0], "chip_sum_w_in")
        rider = _join_riders(_spread_rider([stage]), _share_rider(halves, first_sem=3))
    (grad_x, dg_mix), rode = _in_proj_bwd(dparts, dfa_row, w1, wft, x, gains["g_mix"], dx1,
                                          rider=rider if exchange else None)
    if exchange:
        d_w_in = _finish_half(own, rode[0], "finish_half_w_in")
        d_w_out, d_w_gate, d_w_up, d_w_down = rode[1:]
    else:
        d_w_in = _pick_columns(in_order, 0, n_cols)

    fold = lambda g2: (g2[:, :HEAD_DIM] + g2[:, HEAD_DIM:])
    small = {
        "g_mix": dg_mix[0:1], "g_ffn": dg_ffn[0:1], "g_out_fox": dg_of[0:1], "g_out_dil": dg_od[0:1],
        "g_q_fox": fold(dg_fox[0:1]), "g_k_fox": fold(dg_fox[1:2]),
        "g_q_dil": fold(dg_dil[0:1]), "g_k_dil": fold(dg_dil[1:2]),
        "b_forget": db[:, 0].reshape(1, N_FOX_HEADS),
    }
    big = {"w_in": d_w_in, "w_out": d_w_out, "w_gate": d_w_gate, "w_up": d_w_up, "w_down": d_w_down}
    return loss, grad_x, big, small


def _pick_columns(pieces, lo, hi):
    out, first = [], 0
    for a, w in pieces:
        a_lo, a_hi = max(lo, first), min(hi, first + w)
        if a_lo < a_hi:
            out.append(a[:, a_lo - first:a_hi - first])
        first += w
    return out[0] if len(out) == 1 else jnp.concatenate(out, axis=1)


def kernel(x, g_mix, w_in, b_forget, g_q_fox, g_k_fox, g_q_dil, g_k_dil, g_out_fox, g_out_dil, w_out, g_ffn, w_gate, w_up, w_down, loss_target, m_g_mix, m_w_in, m_b_forget, m_g_q_fox, m_g_k_fox, m_g_q_dil, m_g_k_dil, m_g_out_fox, m_g_out_dil, m_w_out, m_g_ffn, m_w_gate, m_w_up, m_w_down, v_g_mix, v_w_in, v_b_forget, v_g_q_fox, v_g_k_fox, v_g_q_dil, v_g_k_dil, v_g_out_fox, v_g_out_dil, v_w_out, v_g_ffn, v_w_gate, v_w_up, v_w_down):
    nb, seq, d = x.shape
    weights = dict(g_mix=g_mix, w_in=w_in, b_forget=b_forget, g_q_fox=g_q_fox, g_k_fox=g_k_fox, g_q_dil=g_q_dil,
                   g_k_dil=g_k_dil, g_out_fox=g_out_fox, g_out_dil=g_out_dil, w_out=w_out, g_ffn=g_ffn,
                   w_gate=w_gate, w_up=w_up, w_down=w_down)
    m_in = dict(g_mix=m_g_mix, w_in=m_w_in, b_forget=m_b_forget, g_q_fox=m_g_q_fox, g_k_fox=m_g_k_fox,
                g_q_dil=m_g_q_dil, g_k_dil=m_g_k_dil, g_out_fox=m_g_out_fox, g_out_dil=m_g_out_dil, w_out=m_w_out,
                g_ffn=m_g_ffn, w_gate=m_w_gate, w_up=m_w_up, w_down=m_w_down)
    v_in = dict(g_mix=v_g_mix, w_in=v_w_in, b_forget=v_b_forget, g_q_fox=v_g_q_fox, g_k_fox=v_g_k_fox,
                g_q_dil=v_g_q_dil, g_k_dil=v_g_k_dil, g_out_fox=v_g_out_fox, g_out_dil=v_g_out_dil, w_out=v_w_out,
                g_ffn=v_g_ffn, w_gate=v_w_gate, w_up=v_w_up, w_down=v_w_down)
    order = ["g_mix", "w_in", "b_forget", "g_q_fox", "g_k_fox", "g_q_dil", "g_k_dil", "g_out_fox", "g_out_dil",
             "w_out", "g_ffn", "w_gate", "w_up", "w_down"]

    w_in_all = _gather_weight(w_in, "gather_w_in")
    in_shards = [(w_in_all[s], w_in_all.shape[2]) for s in range(4)]
    fox_w = 3 * W_GROUP
    n_cols = 4 * w_in_all.shape[2]
    w1 = jnp.concatenate([_pick_columns(in_shards, 0, fox_w), _pick_columns(in_shards, fox_w + N_FOX_HEADS, n_cols)],
                         axis=1)
    wft = _pick_columns(in_shards, fox_w, fox_w + N_FOX_HEADS).T
    swap = lambda a: jnp.transpose(a, (0, 2, 1))
    for n in ("w_gate", "w_up"):
        weights[n], m_in[n], v_in[n] = swap(weights[n]), swap(m_in[n]), swap(v_in[n])
    shards = _cast_bf16([weights[n] for n in ("w_out", "w_gate", "w_up", "w_down")], "cast_shards")

    gains = {n: weights[n] for n, _ in SMALL}
    loss, grad_x, big, small = _local_grads(
        x.reshape(nb * seq, d), loss_target.reshape(nb * seq, d), gains, w1, wft, None, shards, nb, seq)

    grads = {n: big[n][None] for n in ("w_out", "w_gate", "w_up", "w_down")}
    packed = jnp.concatenate([small[n].reshape(-1) for n, _ in SMALL] + [loss.reshape(1)])
    packed = jnp.pad(packed, (0, SMALL_PACKED[0] * SMALL_PACKED[1] - packed.shape[0])).reshape(SMALL_PACKED)
    summed = _all_sum_small(packed).reshape(-1)
    pos = 0
    for n, size in SMALL:
        grads[n] = summed[pos:pos + size].reshape(1, size)
        pos += size
    loss = summed[pos]

    to_entry = lambda a: jnp.transpose(a, (2, 0, 1))
    deltas, new_m, new_v, grad_out = {}, {}, {}, {}
    for n in ["w_down"] + [n for n in order if n != "w_down"]:
        rider = _share_rider([big["w_in"]]) if n == "w_down" else None
        (deltas[n], new_m[n], new_v[n]), shared = _adamw(weights[n], grads[n], m_in[n], v_in[n], "adamw_" + n, rider)
        if rider is not None:
            grads["w_in"] = to_entry(shared[0][None])
            weights["w_in"], m_in["w_in"], v_in["w_in"] = (to_entry(a) for a in (w_in, m_w_in, v_w_in))
        grad_out[n] = grads[n]
    for n in ("w_gate", "w_up"):
        grad_out[n], deltas[n], new_m[n], new_v[n] = (swap(a) for a in (grad_out[n], deltas[n], new_m[n], new_v[n]))
    from_entry = lambda a: jnp.transpose(a, (1, 2, 0))
    grad_out["w_in"], deltas["w_in"], new_m["w_in"], new_v["w_in"] = (
        from_entry(a) for a in (grad_out["w_in"], deltas["w_in"], new_m["w_in"], new_v["w_in"]))

    return (loss, grad_x.reshape(nb, seq, d), *[grad_out[n] for n in order], *[deltas[n] for n in order],
            *[new_m[n] for n in order], *[new_v[n] for n in order])
```
